```python
import jax, jax.numpy as jnp
from jax import lax
import numpy as np

D_MODEL = 2048
BATCH = 8
SEQ = 8192
DEPTH = 1

EPS = 1e-6
HG_HEADS = 16
HG_DK = 128
HG_DV = 128
HG_FDIM = HG_HEADS * HG_DK
HG_VDIM = HG_HEADS * HG_DV
HG_CHUNK = 64
SSM_DINNER = 2 * D_MODEL
SSM_HEADDIM = 64
SSM_HEADS = SSM_DINNER // SSM_HEADDIM
SSM_GROUPS = 8
SSM_HPG = SSM_HEADS // SSM_GROUPS
SSM_DSTATE = 128
SSM_CONV = 4
SSM_CHUNK = 256
SSM_CONV_DIM = SSM_DINNER + 2 * SSM_GROUPS * SSM_DSTATE
D_FF = 5632
FFN_CONV = 3
IN_SIZES = (HG_FDIM, HG_FDIM, HG_VDIM, HG_VDIM,
            SSM_DINNER, SSM_CONV_DIM, SSM_HEADS,
            D_MODEL, D_MODEL)
IN_TOTAL = sum(IN_SIZES)

kernel_name = "hgrn2_mamba2_gated_hybrid_block"


def rmsnorm(x, w):
    xf = x.astype(jnp.float32)
    y = xf * lax.rsqrt(jnp.mean(xf * xf, axis=-1, keepdims=True) + EPS)
    return (y * w.astype(jnp.float32)).astype(x.dtype)


def causal_dwconv(x, w, b):
    k = w.shape[0]
    c = x.shape[-1]
    y = lax.conv_general_dilated(x, w[:, None, :].astype(x.dtype), window_strides=(1,),
                                 padding=[(k - 1, 0)], dimension_numbers=('NWC', 'WIO', 'NWC'),
                                 feature_group_count=c)
    return y + b.astype(x.dtype)


def _to_chunks(a, c):
    b, t = a.shape[:2]
    n = -(-t // c)
    a = jnp.pad(a, ((0, 0), (0, n * c - t)) + ((0, 0),) * (a.ndim - 2))
    return jnp.moveaxis(a.reshape((b, n, c) + a.shape[2:]), 1, 0)


def _from_chunks(y, t):
    n, b, c = y.shape[:3]
    return jnp.moveaxis(y, 0, 1).reshape((b, n * c) + y.shape[3:])[:, :t]


def hgrn2_scan(q, k, v, logf):
    bsz, t, h, dk = q.shape
    dv = v.shape[-1]
    causal = jnp.tril(jnp.ones((HG_CHUNK, HG_CHUNK), dtype=bool))[:, :, None, None]

    def step(S, inp):
        qc, kc, vc, gc = inp
        bcum = jnp.cumsum(gc, axis=1)
        rel = jnp.exp(jnp.where(causal, bcum[:, :, None] - bcum[:, None, :], -jnp.inf))
        scores = jnp.einsum('bthk,bshk,btshk->bhts', qc, kc, rel)
        o = jnp.einsum('bhts,bshv->bthv', scores, vc)
        o = o + jnp.einsum('bthk,bhkv->bthv', qc * jnp.exp(bcum), S)
        b_last = bcum[:, -1]
        S = jnp.exp(b_last)[..., None] * S + jnp.einsum('bshk,bshv->bhkv', kc * jnp.exp(b_last[:, None] - bcum), vc)
        return S, o

    S0 = jnp.zeros((bsz, h, dk, dv), jnp.float32)
    xs = tuple(_to_chunks(a.astype(jnp.float32), HG_CHUNK) for a in (q, k, v, logf))
    _, o = lax.scan(step, S0, xs)
    return _from_chunks(o, t)


def ssd_scan(x, dA, Bm, Cm):
    bsz, t, g, r, p = x.shape
    n = Bm.shape[-1]
    causal = jnp.tril(jnp.ones((SSM_CHUNK, SSM_CHUNK), dtype=bool))[:, :, None, None]

    def step(S, inp):
        xc, ac, bc, cc = inp
        acum = jnp.cumsum(ac, axis=1)
        L = jnp.exp(jnp.where(causal, acum[:, :, None] - acum[:, None, :], -jnp.inf))
        cb = jnp.einsum('btgn,bsgn->bgts', cc, bc)
        y = jnp.einsum('bgts,btsgr,bsgrp->btgrp', cb, L, xc)
        y = y + jnp.einsum('btgn,bgrpn,btgr->btgrp', cc, S, jnp.exp(acum))
        a_last = acum[:, -1]
        S = jnp.exp(a_last)[..., None, None] * S + jnp.einsum(
            'bsgn,bsgr,bsgrp->bgrpn', bc, jnp.exp(a_last[:, None] - acum), xc)
        return S, y

    S0 = jnp.zeros((bsz, g, r, p, n), jnp.float32)
    xs = tuple(_to_chunks(a.astype(jnp.float32), SSM_CHUNK) for a in (x, dA, Bm, Cm))
    _, y = lax.scan(step, S0, xs)
    return _from_chunks(y, t)


def hgrn2_mixer(q, f_raw, i, g, lb, norm_w):
    bsz, t = q.shape[:2]
    f = lb + (1.0 - lb) * jax.nn.sigmoid(f_raw.astype(jnp.float32))
    qh = (jax.nn.silu(q.astype(jnp.float32)) * HG_DK ** -0.5).reshape(bsz, t, HG_HEADS, HG_DK)
    kh = (1.0 - f).reshape(bsz, t, HG_HEADS, HG_DK)
    lfh = jnp.log(f).reshape(bsz, t, HG_HEADS, HG_DK)
    vh = i.reshape(bsz, t, HG_HEADS, HG_DV)
    o = hgrn2_scan(qh, kh, vh, lfh)
    o = rmsnorm(o, norm_w) * jax.nn.silu(g.reshape(bsz, t, HG_HEADS, HG_DV).astype(jnp.float32))
    return o.reshape(bsz, t, HG_VDIM).astype(q.dtype)


def mamba2_mixer(z, xbc, dt_raw, conv_w, conv_b, dt_bias, A_log, D_skip, norm_w):
    bsz, t = z.shape[:2]
    xbc = jax.nn.silu(causal_dwconv(xbc, conv_w, conv_b))
    xs = xbc[..., :SSM_DINNER]
    Bm = xbc[..., SSM_DINNER:SSM_DINNER + SSM_GROUPS * SSM_DSTATE].reshape(bsz, t, SSM_GROUPS, SSM_DSTATE)
    Cm = xbc[..., SSM_DINNER + SSM_GROUPS * SSM_DSTATE:].reshape(bsz, t, SSM_GROUPS, SSM_DSTATE)
    dt = jax.nn.softplus(dt_raw.astype(jnp.float32) + dt_bias.astype(jnp.float32))
    A = -jnp.exp(A_log.astype(jnp.float32))
    xh = xs.astype(jnp.float32).reshape(bsz, t, SSM_GROUPS, SSM_HPG, SSM_HEADDIM)
    dtg = dt.reshape(bsz, t, SSM_GROUPS, SSM_HPG)
    y = ssd_scan(xh * dtg[..., None], dtg * A.reshape(SSM_GROUPS, SSM_HPG), Bm, Cm)
    y = y + D_skip.astype(jnp.float32).reshape(SSM_GROUPS, SSM_HPG, 1) * xh
    y = y.reshape(bsz, t, SSM_DINNER) * jax.nn.silu(z.astype(jnp.float32))
    y = rmsnorm(y.reshape(bsz, t, SSM_GROUPS, SSM_DINNER // SSM_GROUPS),
                norm_w.reshape(SSM_GROUPS, SSM_DINNER // SSM_GROUPS))
    return y.reshape(bsz, t, SSM_DINNER).astype(z.dtype)


def conv_ffn(h, w_up, conv_w, conv_b, w_down):
    gu = jnp.einsum('btd,df->btf', h, w_up)
    gate, up = gu[..., :D_FF], gu[..., D_FF:]
    gate = causal_dwconv(gate, conv_w, conv_b)
    return jnp.einsum('btf,fd->btd', jax.nn.gelu(gate, approximate=True) * up, w_down)


def _fwd_setup_inputs(seed: int = 0) -> dict:
    key = jax.random.key(seed)
    ks = jax.random.split(key, 24)
    nrm = lambda k, shape, s: jax.random.normal(k, shape, jnp.float32) * s
    gain = lambda k, shape: 1.0 + 0.02 * jax.random.normal(k, shape, jnp.float32)
    dt0 = jnp.exp(jax.random.uniform(ks[8], (DEPTH, SSM_HEADS), jnp.float32,
                                     np.log(1e-3), np.log(1e-1)))
    return {
        'x': jax.random.normal(ks[0], (BATCH, SEQ, D_MODEL), jnp.float32),
        'w_in': nrm(ks[1], (DEPTH, D_MODEL, IN_TOTAL), D_MODEL ** -0.5),
        'mix_pre_norm': gain(ks[2], (DEPTH, D_MODEL)),
        'mix_post_norm': gain(ks[3], (DEPTH, D_MODEL)),
        'hg_lb_table': nrm(ks[4], (DEPTH + 1, HG_FDIM), 0.5),
        'hg_out_norm': gain(ks[5], (DEPTH, HG_DV)),
        'ssm_conv_w': nrm(ks[6], (DEPTH, SSM_CONV, SSM_CONV_DIM), SSM_CONV ** -0.5),
        'ssm_conv_b': nrm(ks[7], (DEPTH, SSM_CONV_DIM), 0.01),
        'ssm_dt_bias': dt0 + jnp.log(-jnp.expm1(-dt0)),
        'ssm_A_log': jnp.log(jax.random.uniform(ks[9], (DEPTH, SSM_HEADS), jnp.float32, 1.0, 16.0)),
        'ssm_D': gain(ks[10], (DEPTH, SSM_HEADS)),
        'ssm_out_norm': gain(ks[11], (DEPTH, SSM_DINNER)),
        'w_branch_hg': nrm(ks[12], (DEPTH, HG_VDIM, D_MODEL), HG_VDIM ** -0.5),
        'w_branch_ssm': nrm(ks[13], (DEPTH, SSM_DINNER, D_MODEL), SSM_DINNER ** -0.5),
        'w_out': nrm(ks[14], (DEPTH, D_MODEL, D_MODEL), D_MODEL ** -0.5),
        'ffn_pre_norm': gain(ks[15], (DEPTH, D_MODEL)),
        'ffn_post_norm': gain(ks[16], (DEPTH, D_MODEL)),
        'ffn_w_up': nrm(ks[17], (DEPTH, D_MODEL, 2 * D_FF), D_MODEL ** -0.5),
        'ffn_conv_w': nrm(ks[18], (DEPTH, FFN_CONV, D_FF), FFN_CONV ** -0.5),
        'ffn_conv_b': nrm(ks[19], (DEPTH, D_FF), 0.01),
        'ffn_w_down': nrm(ks[20], (DEPTH, D_FF, D_MODEL), D_FF ** -0.5),
    }


def _fwd_reference(x, w_in, mix_pre_norm, mix_post_norm, hg_lb_table, hg_out_norm, ssm_conv_w, ssm_conv_b,
              ssm_dt_bias, ssm_A_log, ssm_D, ssm_out_norm, w_branch_hg, w_branch_ssm, w_out,
              ffn_pre_norm, ffn_post_norm, ffn_w_up, ffn_conv_w, ffn_conv_b, ffn_w_down):
    lower_bounds = jnp.cumsum(jax.nn.softmax(hg_lb_table.astype(jnp.float32), axis=0), axis=0)
    splits = []
    acc = 0
    for s in IN_SIZES[:-1]:
        acc += s
        splits.append(acc)
    for l in range(DEPTH):
        h = rmsnorm(x, mix_pre_norm[l])
        proj = jnp.einsum('btd,de->bte', h, w_in[l])
        q, f_raw, i, g, z, xbc, dt_raw, gate_hg, gate_ssm = jnp.split(proj, splits, axis=-1)
        y_hg = hgrn2_mixer(q, f_raw, i, g, lower_bounds[l], hg_out_norm[l])
        y_ssm = mamba2_mixer(z, xbc, dt_raw, ssm_conv_w[l], ssm_conv_b[l], ssm_dt_bias[l],
                             ssm_A_log[l], ssm_D[l], ssm_out_norm[l])
        mixed = (jax.nn.sigmoid(gate_hg) * jnp.einsum('btv,vd->btd', y_hg, w_branch_hg[l])
                 + jax.nn.sigmoid(gate_ssm) * jnp.einsum('bte,ed->btd', y_ssm, w_branch_ssm[l]))
        x = x + rmsnorm(jnp.einsum('btd,de->bte', mixed, w_out[l]), mix_post_norm[l])
        h = rmsnorm(x, ffn_pre_norm[l])
        x = x + rmsnorm(conv_ffn(h, ffn_w_up[l], ffn_conv_w[l], ffn_conv_b[l], ffn_w_down[l]), ffn_post_norm[l])
    return x


import jax as _jax
import jax.numpy as _jnp

TWIN_FORMAT = 'train_step'
FWD_PARAMS = ['x', 'w_in', 'mix_pre_norm', 'mix_post_norm', 'hg_lb_table', 'hg_out_norm', 'ssm_conv_w', 'ssm_conv_b', 'ssm_dt_bias', 'ssm_A_log', 'ssm_D', 'ssm_out_norm', 'w_branch_hg', 'w_branch_ssm', 'w_out', 'ffn_pre_norm', 'ffn_post_norm', 'ffn_w_up', 'ffn_conv_w', 'ffn_conv_b', 'ffn_w_down']
TWIN_WEIGHTS = ['w_in', 'mix_pre_norm', 'mix_post_norm', 'hg_lb_table', 'hg_out_norm', 'ssm_conv_w', 'ssm_conv_b', 'ssm_dt_bias', 'ssm_A_log', 'ssm_D', 'ssm_out_norm', 'w_branch_hg', 'w_branch_ssm', 'w_out', 'ffn_pre_norm', 'ffn_post_norm', 'ffn_w_up', 'ffn_conv_w', 'ffn_conv_b', 'ffn_w_down']
TWIN_DIFF_INPUT = 'x'
TWIN_INPUTS = ['x', 'w_in', 'mix_pre_norm', 'mix_post_norm', 'hg_lb_table', 'hg_out_norm', 'ssm_conv_w', 'ssm_conv_b', 'ssm_dt_bias', 'ssm_A_log', 'ssm_D', 'ssm_out_norm', 'w_branch_hg', 'w_branch_ssm', 'w_out', 'ffn_pre_norm', 'ffn_post_norm', 'ffn_w_up', 'ffn_conv_w', 'ffn_conv_b', 'ffn_w_down', 'loss_target', 'm_w_in', 'm_mix_pre_norm', 'm_mix_post_norm', 'm_hg_lb_table', 'm_hg_out_norm', 'm_ssm_conv_w', 'm_ssm_conv_b', 'm_ssm_dt_bias', 'm_ssm_A_log', 'm_ssm_D', 'm_ssm_out_norm', 'm_w_branch_hg', 'm_w_branch_ssm', 'm_w_out', 'm_ffn_pre_norm', 'm_ffn_post_norm', 'm_ffn_w_up', 'm_ffn_conv_w', 'm_ffn_conv_b', 'm_ffn_w_down', 'v_w_in', 'v_mix_pre_norm', 'v_mix_post_norm', 'v_hg_lb_table', 'v_hg_out_norm', 'v_ssm_conv_w', 'v_ssm_conv_b', 'v_ssm_dt_bias', 'v_ssm_A_log', 'v_ssm_D', 'v_ssm_out_norm', 'v_w_branch_hg', 'v_w_branch_ssm', 'v_w_out', 'v_ffn_pre_norm', 'v_ffn_post_norm', 'v_ffn_w_up', 'v_ffn_conv_w', 'v_ffn_conv_b', 'v_ffn_w_down']
TWIN_OUTPUTS = ['loss', 'grad_x', 'grad_w_in', 'grad_mix_pre_norm', 'grad_mix_post_norm', 'grad_hg_lb_table', 'grad_hg_out_norm', 'grad_ssm_conv_w', 'grad_ssm_conv_b', 'grad_ssm_dt_bias', 'grad_ssm_A_log', 'grad_ssm_D', 'grad_ssm_out_norm', 'grad_w_branch_hg', 'grad_w_branch_ssm', 'grad_w_out', 'grad_ffn_pre_norm', 'grad_ffn_post_norm', 'grad_ffn_w_up', 'grad_ffn_conv_w', 'grad_ffn_conv_b', 'grad_ffn_w_down', 'delta_w_in', 'delta_mix_pre_norm', 'delta_mix_post_norm', 'delta_hg_lb_table', 'delta_hg_out_norm', 'delta_ssm_conv_w', 'delta_ssm_conv_b', 'delta_ssm_dt_bias', 'delta_ssm_A_log', 'delta_ssm_D', 'delta_ssm_out_norm', 'delta_w_branch_hg', 'delta_w_branch_ssm', 'delta_w_out', 'delta_ffn_pre_norm', 'delta_ffn_post_norm', 'delta_ffn_w_up', 'delta_ffn_conv_w', 'delta_ffn_conv_b', 'delta_ffn_w_down', 'new_m_w_in', 'new_m_mix_pre_norm', 'new_m_mix_post_norm', 'new_m_hg_lb_table', 'new_m_hg_out_norm', 'new_m_ssm_conv_w', 'new_m_ssm_conv_b', 'new_m_ssm_dt_bias', 'new_m_ssm_A_log', 'new_m_ssm_D', 'new_m_ssm_out_norm', 'new_m_w_branch_hg', 'new_m_w_branch_ssm', 'new_m_w_out', 'new_m_ffn_pre_norm', 'new_m_ffn_post_norm', 'new_m_ffn_w_up', 'new_m_ffn_conv_w', 'new_m_ffn_conv_b', 'new_m_ffn_w_down', 'new_v_w_in', 'new_v_mix_pre_norm', 'new_v_mix_post_norm', 'new_v_hg_lb_table', 'new_v_hg_out_norm', 'new_v_ssm_conv_w', 'new_v_ssm_conv_b', 'new_v_ssm_dt_bias', 'new_v_ssm_A_log', 'new_v_ssm_D', 'new_v_ssm_out_norm', 'new_v_w_branch_hg', 'new_v_w_branch_ssm', 'new_v_w_out', 'new_v_ffn_pre_norm', 'new_v_ffn_post_norm', 'new_v_ffn_w_up', 'new_v_ffn_conv_w', 'new_v_ffn_conv_b', 'new_v_ffn_w_down']
TWIN_LEAF_KINDS = {'loss': 'loss', 'grad_x': 'grad_x', 'grad_w_in': 'grad_w', 'grad_mix_pre_norm': 'grad_w', 'grad_mix_post_norm': 'grad_w', 'grad_hg_lb_table': 'grad_w', 'grad_hg_out_norm': 'grad_w', 'grad_ssm_conv_w': 'grad_w', 'grad_ssm_conv_b': 'grad_w', 'grad_ssm_dt_bias': 'grad_w', 'grad_ssm_A_log': 'grad_w', 'grad_ssm_D': 'grad_w', 'grad_ssm_out_norm': 'grad_w', 'grad_w_branch_hg': 'grad_w', 'grad_w_branch_ssm': 'grad_w', 'grad_w_out': 'grad_w', 'grad_ffn_pre_norm': 'grad_w', 'grad_ffn_post_norm': 'grad_w', 'grad_ffn_w_up': 'grad_w', 'grad_ffn_conv_w': 'grad_w', 'grad_ffn_conv_b': 'grad_w', 'grad_ffn_w_down': 'grad_w', 'delta_w_in': 'delta_w', 'delta_mix_pre_norm': 'delta_w', 'delta_mix_post_norm': 'delta_w', 'delta_hg_lb_table': 'delta_w', 'delta_hg_out_norm': 'delta_w', 'delta_ssm_conv_w': 'delta_w', 'delta_ssm_conv_b': 'delta_w', 'delta_ssm_dt_bias': 'delta_w', 'delta_ssm_A_log': 'delta_w', 'delta_ssm_D': 'delta_w', 'delta_ssm_out_norm': 'delta_w', 'delta_w_branch_hg': 'delta_w', 'delta_w_branch_ssm': 'delta_w', 'delta_w_out': 'delta_w', 'delta_ffn_pre_norm': 'delta_w', 'delta_ffn_post_norm': 'delta_w', 'delta_ffn_w_up': 'delta_w', 'delta_ffn_conv_w': 'delta_w', 'delta_ffn_conv_b': 'delta_w', 'delta_ffn_w_down': 'delta_w', 'new_m_w_in': 'new_m', 'new_m_mix_pre_norm': 'new_m', 'new_m_mix_post_norm': 'new_m', 'new_m_hg_lb_table': 'new_m', 'new_m_hg_out_norm': 'new_m', 'new_m_ssm_conv_w': 'new_m', 'new_m_ssm_conv_b': 'new_m', 'new_m_ssm_dt_bias': 'new_m', 'new_m_ssm_A_log': 'new_m', 'new_m_ssm_D': 'new_m', 'new_m_ssm_out_norm': 'new_m', 'new_m_w_branch_hg': 'new_m', 'new_m_w_branch_ssm': 'new_m', 'new_m_w_out': 'new_m', 'new_m_ffn_pre_norm': 'new_m', 'new_m_ffn_post_norm': 'new_m', 'new_m_ffn_w_up': 'new_m', 'new_m_ffn_conv_w': 'new_m', 'new_m_ffn_conv_b': 'new_m', 'new_m_ffn_w_down': 'new_m', 'new_v_w_in': 'new_v', 'new_v_mix_pre_norm': 'new_v', 'new_v_mix_post_norm': 'new_v', 'new_v_hg_lb_table': 'new_v', 'new_v_hg_out_norm': 'new_v', 'new_v_ssm_conv_w': 'new_v', 'new_v_ssm_conv_b': 'new_v', 'new_v_ssm_dt_bias': 'new_v', 'new_v_ssm_A_log': 'new_v', 'new_v_ssm_D': 'new_v', 'new_v_ssm_out_norm': 'new_v', 'new_v_w_branch_hg': 'new_v', 'new_v_w_branch_ssm': 'new_v', 'new_v_w_out': 'new_v', 'new_v_ffn_pre_norm': 'new_v', 'new_v_ffn_post_norm': 'new_v', 'new_v_ffn_w_up': 'new_v', 'new_v_ffn_conv_w': 'new_v', 'new_v_ffn_conv_b': 'new_v', 'new_v_ffn_w_down': 'new_v'}


def _forward(args):
    return _fwd_reference(*[args[k] for k in FWD_PARAMS])


def _output_shape():
    def fwd():
        inp = _fwd_setup_inputs(0)
        return _fwd_reference(*[inp[k] for k in FWD_PARAMS])
    out = _jax.eval_shape(fwd)
    return out.shape, out.dtype

N_MICROBATCH = 1
ADAM_LR = 0.001
ADAM_B1 = 0.9
ADAM_B2 = 0.999
ADAM_EPS = 1e-08
ADAM_WD = 0.01
ADAM_STEP = 10
PER_EXAMPLE_BATCH_AXIS = {'x': 0, 'loss_target': 0}
SHARED_INPUTS = []
_WEIGHT_DTYPES = {'w_in': _jnp.float32, 'mix_pre_norm': _jnp.float32, 'mix_post_norm': _jnp.float32, 'hg_lb_table': _jnp.float32, 'hg_out_norm': _jnp.float32, 'ssm_conv_w': _jnp.float32, 'ssm_conv_b': _jnp.float32, 'ssm_dt_bias': _jnp.float32, 'ssm_A_log': _jnp.float32, 'ssm_D': _jnp.float32, 'ssm_out_norm': _jnp.float32, 'w_branch_hg': _jnp.float32, 'w_branch_ssm': _jnp.float32, 'w_out': _jnp.float32, 'ffn_pre_norm': _jnp.float32, 'ffn_post_norm': _jnp.float32, 'ffn_w_up': _jnp.float32, 'ffn_conv_w': _jnp.float32, 'ffn_conv_b': _jnp.float32, 'ffn_w_down': _jnp.float32}
MOMENT_SCALE = {'w_in': 1.493875e-01, 'mix_pre_norm': 5.598760e-01, 'mix_post_norm': 3.203090e+01, 'hg_lb_table': 1.453315e-02, 'hg_out_norm': 9.910577e-01, 'ssm_conv_w': 1.978698e-01, 'ssm_conv_b': 4.880209e-01, 'ssm_dt_bias': 3.917534e-01, 'ssm_A_log': 9.937692e-01, 'ssm_D': 1.081108e+00, 'ssm_out_norm': 2.762287e-01, 'w_branch_hg': 1.697275e-01, 'w_branch_ssm': 4.058304e-01, 'w_out': 4.804038e-01, 'ffn_pre_norm': 4.778159e-01, 'ffn_post_norm': 3.197639e+01, 'ffn_w_up': 2.034527e-01, 'ffn_conv_w': 2.146771e-01, 'ffn_conv_b': 4.606490e-01, 'ffn_w_down': 4.388816e-01}


def _to_microbatches(a, axis):
    t = _jnp.moveaxis(a, axis, 0)
    t = t.reshape((N_MICROBATCH, t.shape[0] // N_MICROBATCH) + t.shape[1:])
    return _jnp.moveaxis(t, 1, axis + 1)


def setup_inputs(seed: int = 0) -> dict:
    inp = _fwd_setup_inputs(seed)
    key = _jax.random.fold_in(_jax.random.key(seed), 7919)
    shape, _ = _output_shape()
    out = dict(inp)
    out["loss_target"] = _jax.random.normal(_jax.random.fold_in(key, 0), shape, _jnp.float32)
    for i, name in enumerate(TWIN_WEIGHTS):
        w = inp[name].astype(_jnp.float32)
        if MOMENT_SCALE is None:
            s = _jnp.sqrt(_jnp.mean(_jnp.square(w)) + 1e-30)
        else:
            s = MOMENT_SCALE[name]
        km, kv = _jax.random.split(_jax.random.fold_in(key, i + 1))
        out[name] = w
        out["m_" + name] = s * _jax.random.normal(km, w.shape, _jnp.float32)
        out["v_" + name] = (s * s) * _jax.random.uniform(kv, w.shape, _jnp.float32, 0.5, 1.5)
    if N_MICROBATCH > 1:
        for name, axis in PER_EXAMPLE_BATCH_AXIS.items():
            out[name] = _to_microbatches(out[name], axis)
    return {'x': out['x'], 'w_in': out['w_in'], 'mix_pre_norm': out['mix_pre_norm'], 'mix_post_norm': out['mix_post_norm'], 'hg_lb_table': out['hg_lb_table'], 'hg_out_norm': out['hg_out_norm'], 'ssm_conv_w': out['ssm_conv_w'], 'ssm_conv_b': out['ssm_conv_b'], 'ssm_dt_bias': out['ssm_dt_bias'], 'ssm_A_log': out['ssm_A_log'], 'ssm_D': out['ssm_D'], 'ssm_out_norm': out['ssm_out_norm'], 'w_branch_hg': out['w_branch_hg'], 'w_branch_ssm': out['w_branch_ssm'], 'w_out': out['w_out'], 'ffn_pre_norm': out['ffn_pre_norm'], 'ffn_post_norm': out['ffn_post_norm'], 'ffn_w_up': out['ffn_w_up'], 'ffn_conv_w': out['ffn_conv_w'], 'ffn_conv_b': out['ffn_conv_b'], 'ffn_w_down': out['ffn_w_down'], 'loss_target': out['loss_target'], 'm_w_in': out['m_w_in'], 'm_mix_pre_norm': out['m_mix_pre_norm'], 'm_mix_post_norm': out['m_mix_post_norm'], 'm_hg_lb_table': out['m_hg_lb_table'], 'm_hg_out_norm': out['m_hg_out_norm'], 'm_ssm_conv_w': out['m_ssm_conv_w'], 'm_ssm_conv_b': out['m_ssm_conv_b'], 'm_ssm_dt_bias': out['m_ssm_dt_bias'], 'm_ssm_A_log': out['m_ssm_A_log'], 'm_ssm_D': out['m_ssm_D'], 'm_ssm_out_norm': out['m_ssm_out_norm'], 'm_w_branch_hg': out['m_w_branch_hg'], 'm_w_branch_ssm': out['m_w_branch_ssm'], 'm_w_out': out['m_w_out'], 'm_ffn_pre_norm': out['m_ffn_pre_norm'], 'm_ffn_post_norm': out['m_ffn_post_norm'], 'm_ffn_w_up': out['m_ffn_w_up'], 'm_ffn_conv_w': out['m_ffn_conv_w'], 'm_ffn_conv_b': out['m_ffn_conv_b'], 'm_ffn_w_down': out['m_ffn_w_down'], 'v_w_in': out['v_w_in'], 'v_mix_pre_norm': out['v_mix_pre_norm'], 'v_mix_post_norm': out['v_mix_post_norm'], 'v_hg_lb_table': out['v_hg_lb_table'], 'v_hg_out_norm': out['v_hg_out_norm'], 'v_ssm_conv_w': out['v_ssm_conv_w'], 'v_ssm_conv_b': out['v_ssm_conv_b'], 'v_ssm_dt_bias': out['v_ssm_dt_bias'], 'v_ssm_A_log': out['v_ssm_A_log'], 'v_ssm_D': out['v_ssm_D'], 'v_ssm_out_norm': out['v_ssm_out_norm'], 'v_w_branch_hg': out['v_w_branch_hg'], 'v_w_branch_ssm': out['v_w_branch_ssm'], 'v_w_out': out['v_w_out'], 'v_ffn_pre_norm': out['v_ffn_pre_norm'], 'v_ffn_post_norm': out['v_ffn_post_norm'], 'v_ffn_w_up': out['v_ffn_w_up'], 'v_ffn_conv_w': out['v_ffn_conv_w'], 'v_ffn_conv_b': out['v_ffn_conv_b'], 'v_ffn_w_down': out['v_ffn_w_down']}


def _loss(weights, diff, rest, loss_target):
    with _jax.named_scope("forward"):
        args = {**rest, TWIN_DIFF_INPUT: diff, **{k: w.astype(_WEIGHT_DTYPES[k]) for k, w in weights.items()}}
        y = _forward(args)
    with _jax.named_scope("loss_head"):
        err = _jnp.square(y.astype(_jnp.float32) - loss_target)
        return 0.5 * _jnp.sum(_jnp.mean(err, axis=-1)) if err.ndim else 0.5 * err


def _adamw(w, g, m, v):
    m = ADAM_B1 * m + (1.0 - ADAM_B1) * g
    v = ADAM_B2 * v + (1.0 - ADAM_B2) * _jnp.square(g)
    m_hat = m / (1.0 - ADAM_B1 ** ADAM_STEP)
    v_hat = v / (1.0 - ADAM_B2 ** ADAM_STEP)
    delta = -ADAM_LR * (m_hat / (_jnp.sqrt(v_hat) + ADAM_EPS) + ADAM_WD * w)
    return delta, m, v


def reference(x, w_in, mix_pre_norm, mix_post_norm, hg_lb_table, hg_out_norm, ssm_conv_w, ssm_conv_b, ssm_dt_bias, ssm_A_log, ssm_D, ssm_out_norm, w_branch_hg, w_branch_ssm, w_out, ffn_pre_norm, ffn_post_norm, ffn_w_up, ffn_conv_w, ffn_conv_b, ffn_w_down, loss_target, m_w_in, m_mix_pre_norm, m_mix_post_norm, m_hg_lb_table, m_hg_out_norm, m_ssm_conv_w, m_ssm_conv_b, m_ssm_dt_bias, m_ssm_A_log, m_ssm_D, m_ssm_out_norm, m_w_branch_hg, m_w_branch_ssm, m_w_out, m_ffn_pre_norm, m_ffn_post_norm, m_ffn_w_up, m_ffn_conv_w, m_ffn_conv_b, m_ffn_w_down, v_w_in, v_mix_pre_norm, v_mix_post_norm, v_hg_lb_table, v_hg_out_norm, v_ssm_conv_w, v_ssm_conv_b, v_ssm_dt_bias, v_ssm_A_log, v_ssm_D, v_ssm_out_norm, v_w_branch_hg, v_w_branch_ssm, v_w_out, v_ffn_pre_norm, v_ffn_post_norm, v_ffn_w_up, v_ffn_conv_w, v_ffn_conv_b, v_ffn_w_down):
    given = dict(x=x, w_in=w_in, mix_pre_norm=mix_pre_norm, mix_post_norm=mix_post_norm, hg_lb_table=hg_lb_table, hg_out_norm=hg_out_norm, ssm_conv_w=ssm_conv_w, ssm_conv_b=ssm_conv_b, ssm_dt_bias=ssm_dt_bias, ssm_A_log=ssm_A_log, ssm_D=ssm_D, ssm_out_norm=ssm_out_norm, w_branch_hg=w_branch_hg, w_branch_ssm=w_branch_ssm, w_out=w_out, ffn_pre_norm=ffn_pre_norm, ffn_post_norm=ffn_post_norm, ffn_w_up=ffn_w_up, ffn_conv_w=ffn_conv_w, ffn_conv_b=ffn_conv_b, ffn_w_down=ffn_w_down, loss_target=loss_target, m_w_in=m_w_in, m_mix_pre_norm=m_mix_pre_norm, m_mix_post_norm=m_mix_post_norm, m_hg_lb_table=m_hg_lb_table, m_hg_out_norm=m_hg_out_norm, m_ssm_conv_w=m_ssm_conv_w, m_ssm_conv_b=m_ssm_conv_b, m_ssm_dt_bias=m_ssm_dt_bias, m_ssm_A_log=m_ssm_A_log, m_ssm_D=m_ssm_D, m_ssm_out_norm=m_ssm_out_norm, m_w_branch_hg=m_w_branch_hg, m_w_branch_ssm=m_w_branch_ssm, m_w_out=m_w_out, m_ffn_pre_norm=m_ffn_pre_norm, m_ffn_post_norm=m_ffn_post_norm, m_ffn_w_up=m_ffn_w_up, m_ffn_conv_w=m_ffn_conv_w, m_ffn_conv_b=m_ffn_conv_b, m_ffn_w_down=m_ffn_w_down, v_w_in=v_w_in, v_mix_pre_norm=v_mix_pre_norm, v_mix_post_norm=v_mix_post_norm, v_hg_lb_table=v_hg_lb_table, v_hg_out_norm=v_hg_out_norm, v_ssm_conv_w=v_ssm_conv_w, v_ssm_conv_b=v_ssm_conv_b, v_ssm_dt_bias=v_ssm_dt_bias, v_ssm_A_log=v_ssm_A_log, v_ssm_D=v_ssm_D, v_ssm_out_norm=v_ssm_out_norm, v_w_branch_hg=v_w_branch_hg, v_w_branch_ssm=v_w_branch_ssm, v_w_out=v_w_out, v_ffn_pre_norm=v_ffn_pre_norm, v_ffn_post_norm=v_ffn_post_norm, v_ffn_w_up=v_ffn_w_up, v_ffn_conv_w=v_ffn_conv_w, v_ffn_conv_b=v_ffn_conv_b, v_ffn_w_down=v_ffn_w_down)
    weights = {n: given[n] for n in TWIN_WEIGHTS}
    shared = {n: given[n] for n in SHARED_INPUTS}
    per_example = {n: given[n] for n in ['x']}
    grad_fn = _jax.value_and_grad(_loss, argnums=(0, 1))

    def one_microbatch(ex, loss_target):
        ex = dict(ex)
        diff = ex.pop(TWIN_DIFF_INPUT)
        return grad_fn(weights, diff, {**shared, **ex}, loss_target)

    if N_MICROBATCH == 1:
        loss, (grad_w, grad_x) = one_microbatch(per_example, given["loss_target"])
    else:
        def body(carry, xs):
            loss_sum, grad_sum = carry
            l_k, (gw_k, gx_k) = one_microbatch(xs[0], xs[1])
            with _jax.named_scope("update"):
                return (loss_sum + l_k, _jax.tree.map(_jnp.add, grad_sum, gw_k)), gx_k

        init = (_jnp.zeros((), _jnp.float32), _jax.tree.map(_jnp.zeros_like, weights))
        (loss, grad_w), grad_x = _jax.lax.scan(body, init, (per_example, given["loss_target"]))
    with _jax.named_scope("update"):
        delta_w, new_m, new_v = {}, {}, {}
        for n in TWIN_WEIGHTS:
            delta_w[n], new_m[n], new_v[n] = _adamw(weights[n], grad_w[n], given["m_" + n], given["v_" + n])
    return (loss, grad_x, *[grad_w[n] for n in TWIN_WEIGHTS], *[delta_w[n] for n in TWIN_WEIGHTS],
            *[new_m[n] for n in TWIN_WEIGHTS], *[new_v[n] for n in TWIN_WEIGHTS])
```

```python
import functools

import jax
import jax.numpy as jnp
import numpy as np
from jax import lax
from jax.experimental import pallas as pl
from jax.experimental.pallas import tpu as pltpu

F32 = jnp.float32
BF16 = jnp.bfloat16

D_MODEL = 2048
EPS = 1e-6
HG_HEADS = 16
HG_DK = 128
HG_CHUNK = 64
HG_SUB = 16
SSM_DINNER = 4096
SSM_HEADDIM = 64
SSM_HEADS = 64
SSM_GROUPS = 8
SSM_DSTATE = 128
SSM_CONV = 4
SSM_CHUNK = 256
SSM_CONV_DIM = 6144
D_FF = 5632
FFN_CONV = 3
DT_PAD = 128

ADAM_LR = 0.001
ADAM_B1 = 0.9
ADAM_B2 = 0.999
ADAM_EPS = 1e-08
ADAM_WD = 0.01
ADAM_STEP = 10

VMEM_LIMIT = 56 * 1024 * 1024
HI = lax.Precision.HIGHEST


def _cp(sem, **kw):
    return pltpu.CompilerParams(dimension_semantics=sem, vmem_limit_bytes=VMEM_LIMIT, **kw)


_DIMS = {"nn": (((1,), (0,)), ((), ())), "nt": (((1,), (1,)), ((), ())), "tn": (((0,), (0,)), ((), ()))}


def mm(a, b, mode, *, name, out_dtype=F32, tm=512, tn=512, tk=None, acc=None, n_major=True,
       dims=None, a_off=(0, 0), b_off=(0, 0)):
    if dims is not None:
        M, N, K = dims
    else:
        if mode == "nn":
            (M, K), (K2, N) = a.shape, b.shape
        elif mode == "nt":
            (M, K), (N, K2) = a.shape, b.shape
        else:
            (K, M), (K2, N) = a.shape, b.shape
        assert K == K2, (a.shape, b.shape, mode)
    tm, tn = min(tm, M), min(tn, N)
    tk = K if tk is None else min(tk, K)
    assert M % tm == 0 and N % tn == 0 and K % tk == 0, (M, N, K, tm, tn, tk)
    a_blk = (tk, tm) if mode == "tn" else (tm, tk)
    b_blk = (tn, tk) if mode == "nt" else (tk, tn)
    assert all(o % s == 0 for o, s in zip(a_off, a_blk)) and all(o % s == 0 for o, s in zip(b_off, b_blk))
    ao0, ao1 = a_off[0] // a_blk[0], a_off[1] // a_blk[1]
    bo0, bo1 = b_off[0] // b_blk[0], b_off[1] // b_blk[1]
    nk = K // tk
    if n_major:
        grid = (N // tn, M // tm, nk)
        ij = lambda p0, p1: (p1, p0)
    else:
        grid = (M // tm, N // tn, nk)
        ij = lambda p0, p1: (p0, p1)

    def a_map(p0, p1, k):
        i, _ = ij(p0, p1)
        return (k + ao0, i + ao1) if mode == "tn" else (i + ao0, k + ao1)

    def b_map(p0, p1, k):
        _, j = ij(p0, p1)
        return (j + bo0, k + bo1) if mode == "nt" else (k + bo0, j + bo1)

    def o_map(p0, p1, k):
        return ij(p0, p1)

    a_spec = pl.BlockSpec(a_blk, a_map)
    b_spec = pl.BlockSpec(b_blk, b_map)
    o_spec = pl.BlockSpec((tm, tn), o_map)
    dims = _DIMS[mode]
    has_acc = acc is not None

    def body(*refs):
        if has_acc:
            a_ref, b_ref, c_ref, o_ref, acc_ref = refs
        else:
            a_ref, b_ref, o_ref, acc_ref = refs
        k = pl.program_id(2)
        part = lax.dot_general(a_ref[...], b_ref[...], dims, preferred_element_type=F32)

        @pl.when(k == 0)
        def _():
            acc_ref[...] = part

        @pl.when(k > 0)
        def _():
            acc_ref[...] += part

        @pl.when(k == nk - 1)
        def _():
            r = acc_ref[...]
            if has_acc:
                r = r + c_ref[...].astype(F32)
            o_ref[...] = r.astype(out_dtype)

    in_specs = [a_spec, b_spec] + ([o_spec] if has_acc else [])
    args = (a, b) + ((acc,) if has_acc else ())
    return pl.pallas_call(
        body, name=name, grid=grid, in_specs=in_specs, out_specs=o_spec,
        out_shape=jax.ShapeDtypeStruct((M, N), out_dtype),
        scratch_shapes=[pltpu.VMEM((tm, tn), F32)],
        compiler_params=_cp(("parallel", "parallel", "arbitrary")),
    )(*args)


def _bdot_plain(a, b, mode):
    return lax.dot_general(a.astype(BF16), b.astype(BF16), _DIMS[mode], preferred_element_type=F32)


@functools.partial(jax.custom_vjp, nondiff_argnums=(2,))
def _bdot_vjp(a, b, mode):
    return _bdot_plain(a, b, mode)


def _bdot_fwd(a, b, mode):
    return _bdot_plain(a, b, mode), (a, b)


def _bdot_bwd(mode, res, g):
    a, b = res
    if mode == "nn":
        return _bdot_plain(g, b, "nt"), _bdot_plain(a, g, "tn")
    if mode == "nt":
        return _bdot_plain(g, b, "nn"), _bdot_plain(g, a, "tn")
    return _bdot_plain(b, g, "nt"), _bdot_plain(a, g, "nn")


_bdot_vjp.defvjp(_bdot_fwd, _bdot_bwd)


def _hdot(a, b, mode="nn"):
    return lax.dot_general(a, b, _DIMS[mode], precision=HI, preferred_element_type=F32)


def _sigmoid(x):
    return 1.0 / (1.0 + jnp.exp(-x))


def _silu(x):
    return x * _sigmoid(x)


def _iota(shape, dim):
    return lax.broadcasted_iota(jnp.int32, shape, dim)


def _rms(x, w):
    return x * lax.rsqrt(jnp.mean(x * x, axis=-1, keepdims=True) + EPS) * w


def _hg_chunk(q_raw, f_raw, v, g, st, t0, t1, nw, dot):
    c = q_raw.shape[0]
    m = jnp.maximum(t0, t1)
    e0, e1 = jnp.exp(t0 - m), jnp.exp(t1 - m)
    lb = e0 / (e0 + e1)
    f = lb + (1.0 - lb) * _sigmoid(f_raw)
    k = 1.0 - f
    lf = jnp.log(f)
    qh = _silu(q_raw) * (HG_DK ** -0.5)
    row, col = _iota((c, c), 0), _iota((c, c), 1)
    causal = col <= row
    tril = jnp.where(causal, 1.0, 0.0).astype(F32)
    trilb = jnp.where(causal & (col // HG_SUB == row // HG_SUB), 1.0, 0.0).astype(F32)
    b = _hdot(tril, lf)
    bl = _hdot(trilb, lf)
    a_row = b - bl
    rid = _iota((c, HG_DK), 0)
    qt = qh * jnp.exp(bl)
    kt = k * jnp.exp(-bl)
    scores = jnp.zeros((c, c), F32)
    for j in range(c // HG_SUB):
        if j == 0:
            qj = qt * jnp.exp(jnp.minimum(a_row, 0.0))
        else:
            a_j = jnp.sum(jnp.where(rid == j * HG_SUB - 1, b, 0.0), axis=0, keepdims=True)
            qj = qt * jnp.exp(jnp.minimum(a_row - a_j, 0.0))
        kj = jnp.where(rid // HG_SUB == j, kt, 0.0)
        scores = scores + dot(qj, kj, "nt")
    scores = jnp.where(causal, scores, 0.0)
    o = dot(scores, v, "nn") + dot(qh * jnp.exp(b), st, "nt")
    b_last = jnp.sum(jnp.where(rid == c - 1, b, 0.0), axis=0, keepdims=True)
    st_new = st * jnp.exp(b_last) + dot(v, k * jnp.exp(b_last - b), "tn")
    y = _rms(o, nw) * _silu(g)
    return y, st_new


def hgrn2_fwd(qfig, table, nw, *, step_chunks=2):
    t = qfig.shape[0]
    rows = HG_CHUNK * step_chunks
    nsteps = t // rows

    def body(q_ref, f_ref, v_ref, g_ref, tab_ref, nw_ref, y_ref, s_ref, st_scr):
        @pl.when(pl.program_id(1) == 0)
        def _():
            st_scr[...] = jnp.zeros_like(st_scr)

        t0, t1, nwv = tab_ref[0:1, :], tab_ref[1:2, :], nw_ref[...]
        for c in range(step_chunks):
            sl = pl.ds(c * HG_CHUNK, HG_CHUNK)
            st = st_scr[...]
            s_ref[c] = st
            y, st_new = _hg_chunk(q_ref[sl, :], f_ref[sl, :], v_ref[sl, :], g_ref[sl, :], st, t0, t1, nwv, _bdot_plain)
            y_ref[sl, :] = y.astype(BF16)
            st_scr[...] = st_new

    blk = lambda off: pl.BlockSpec((rows, HG_DK), lambda h, c, off=off: (c, off + h))
    return pl.pallas_call(
        body, name="hgrn2_fwd", grid=(HG_HEADS, nsteps),
        in_specs=[blk(0), blk(HG_HEADS), blk(2 * HG_HEADS), blk(3 * HG_HEADS),
                  pl.BlockSpec((2, HG_DK), lambda h, c: (0, h)), pl.BlockSpec((1, HG_DK), lambda h, c: (0, 0))],
        out_specs=[pl.BlockSpec((rows, HG_DK), lambda h, c: (c, h)),
                   pl.BlockSpec((None, step_chunks, HG_DK, HG_DK), lambda h, c: (h, c, 0, 0))],
        out_shape=[jax.ShapeDtypeStruct((t, HG_HEADS * HG_DK), BF16),
                   jax.ShapeDtypeStruct((HG_HEADS, t // HG_CHUNK, HG_DK, HG_DK), F32)],
        scratch_shapes=[pltpu.VMEM((HG_DK, HG_DK), F32)],
        compiler_params=_cp(("parallel", "arbitrary")),
    )(qfig, qfig, qfig, qfig, table, nw)


def hgrn2_bwd(qfig, table, nw, states, dy, *, step_chunks=2):
    t = qfig.shape[0]
    rows = HG_CHUNK * step_chunks
    nsteps = t // rows

    def body(q_ref, f_ref, v_ref, g_ref, tab_ref, nw_ref, s_ref, dy_ref,
             dq_ref, df_ref, dv_ref, dg_ref, dtab_ref, dnw_ref, dst_scr):
        @pl.when(pl.program_id(1) == 0)
        def _():
            dst_scr[...] = jnp.zeros_like(dst_scr)
            dtab_ref[...] = jnp.zeros_like(dtab_ref)
            dnw_ref[...] = jnp.zeros_like(dnw_ref)

        t0, t1, nwv = tab_ref[0:1, :], tab_ref[1:2, :], nw_ref[...]
        fn = functools.partial(_hg_chunk, dot=_bdot_vjp)
        for c in reversed(range(step_chunks)):
            sl = pl.ds(c * HG_CHUNK, HG_CHUNK)
            _, vjp = jax.vjp(fn, q_ref[sl, :], f_ref[sl, :], v_ref[sl, :], g_ref[sl, :], s_ref[c], t0, t1, nwv)
            dq, df, dv, dg, dst, dt0, dt1, dnw = vjp((dy_ref[sl, :].astype(F32), dst_scr[...]))
            dq_ref[sl, :] = dq.astype(BF16)
            df_ref[sl, :] = df.astype(BF16)
            dv_ref[sl, :] = dv.astype(BF16)
            dg_ref[sl, :] = dg.astype(BF16)
            dst_scr[...] = dst
            dtab_ref[0:1, :] += dt0
            dtab_ref[1:2, :] += dt1
            dnw_ref[...] += dnw

    rev = lambda c: nsteps - 1 - c
    blk = lambda off: pl.BlockSpec((rows, HG_DK), lambda h, c, off=off: (rev(c), off + h))
    oblk = lambda: pl.BlockSpec((rows, HG_DK), lambda h, c: (rev(c), h))
    d = HG_HEADS * HG_DK
    outs = pl.pallas_call(
        body, name="hgrn2_bwd", grid=(HG_HEADS, nsteps),
        in_specs=[blk(0), blk(HG_HEADS), blk(2 * HG_HEADS), blk(3 * HG_HEADS),
                  pl.BlockSpec((2, HG_DK), lambda h, c: (0, h)), pl.BlockSpec((1, HG_DK), lambda h, c: (0, 0)),
                  pl.BlockSpec((None, step_chunks, HG_DK, HG_DK), lambda h, c: (h, rev(c), 0, 0)),
                  pl.BlockSpec((rows, HG_DK), lambda h, c: (rev(c), h))],
        out_specs=[oblk(), oblk(), oblk(), oblk(),
                   pl.BlockSpec((2, HG_DK), lambda h, c: (0, h)),
                   pl.BlockSpec((None, 1, HG_DK), lambda h, c: (h, 0, 0))],
        out_shape=[jax.ShapeDtypeStruct((t, d), BF16)] * 4
        + [jax.ShapeDtypeStruct((2, d), F32), jax.ShapeDtypeStruct((HG_HEADS, 1, HG_DK), F32)],
        scratch_shapes=[pltpu.VMEM((HG_DK, HG_DK), F32)],
        compiler_params=_cp(("parallel", "arbitrary")),
    )(qfig, qfig, qfig, qfig, table, nw, states, dy)
    return outs


def _ssd_chunk(xs2, dt, acum, bm, cm, s2, pair, dot):
    c = xs2.shape[0]
    lane = _iota((DT_PAD, 128), 1)
    expand = jnp.where(_iota((DT_PAD, 128), 0) == 2 * pair + lane // SSM_HEADDIM, 1.0, 0.0).astype(F32)
    sel = jnp.where(_iota((8, DT_PAD), 1) == 2 * pair + _iota((8, DT_PAD), 0), 1.0, 0.0).astype(F32)
    sel = jnp.where(_iota((8, DT_PAD), 0) < 2, sel, 0.0)
    dtx = _hdot(dt, expand)
    acol = _hdot(acum, expand)
    arow8 = _hdot(sel, acum, "nt")
    row, col = _iota((c, c), 0), _iota((c, c), 1)
    causal = col <= row
    cb = dot(cm, bm, "nt")
    x2 = xs2 * dtx
    lane_c = _iota((c, 128), 1)
    y = dot(cm, s2, "nn") * jnp.exp(acol)
    for r in range(2):
        head = (lane_c // SSM_HEADDIM) == r
        a_c = jnp.sum(jnp.where(head & (lane_c % SSM_HEADDIM == 0), acol, 0.0), axis=1, keepdims=True)
        a_r = jnp.sum(jnp.where(_iota((8, c), 0) == r, arow8, 0.0), axis=0, keepdims=True)
        decay = jnp.where(causal, jnp.exp(jnp.minimum(a_c - a_r, 0.0)), 0.0)
        y = y + dot(cb * decay, jnp.where(head, x2, 0.0), "nn")
    a_last = jnp.sum(jnp.where(_iota((c, 128), 0) == c - 1, acol, 0.0), axis=0, keepdims=True)
    s2_new = s2 * jnp.exp(a_last) + dot(bm, x2 * jnp.exp(a_last - acol), "tn")
    return y, s2_new


SSM_PAIRS = SSM_HEADS // 2
PAIRS_PER_GROUP = SSM_PAIRS // SSM_GROUPS
_XS_BLOCKS = SSM_DINNER // 128
_B_BLOCK0 = _XS_BLOCKS
_C_BLOCK0 = _XS_BLOCKS + SSM_GROUPS


def ssd_fwd(xbc_act, dt, acum):
    t = xbc_act.shape[0]
    nc = t // SSM_CHUNK
    c_ = SSM_CHUNK

    def body(xs_ref, b_ref, c_ref, dt_ref, ac_ref, y_ref, s_ref, s_scr):
        p = pl.program_id(1)

        @pl.when(pl.program_id(0) == 0)
        def _():
            s_scr[p] = jnp.zeros((SSM_DSTATE, 128), F32)

        s2 = s_scr[p]
        s_ref[...] = s2
        y, s2_new = _ssd_chunk(xs_ref[...], dt_ref[...], ac_ref[...], b_ref[...], c_ref[...], s2, p, _bdot_plain)
        y_ref[...] = y
        s_scr[p] = s2_new

    return pl.pallas_call(
        body, name="ssd_fwd", grid=(nc, SSM_PAIRS),
        in_specs=[pl.BlockSpec((c_, 128), lambda c, p: (c, p)),
                  pl.BlockSpec((c_, 128), lambda c, p: (c, _B_BLOCK0 + p // PAIRS_PER_GROUP)),
                  pl.BlockSpec((c_, 128), lambda c, p: (c, _C_BLOCK0 + p // PAIRS_PER_GROUP)),
                  pl.BlockSpec((c_, DT_PAD), lambda c, p: (c, 0)),
                  pl.BlockSpec((c_, DT_PAD), lambda c, p: (c, 0))],
        out_specs=[pl.BlockSpec((c_, 128), lambda c, p: (c, p)),
                   pl.BlockSpec((None, None, SSM_DSTATE, 128), lambda c, p: (c, p, 0, 0))],
        out_shape=[jax.ShapeDtypeStruct((t, SSM_DINNER), F32),
                   jax.ShapeDtypeStruct((nc, SSM_PAIRS, SSM_DSTATE, 128), F32)],
        scratch_shapes=[pltpu.VMEM((SSM_PAIRS, SSM_DSTATE, 128), F32)],
        compiler_params=_cp(("arbitrary", "arbitrary")),
    )(xbc_act, xbc_act, xbc_act, dt, acum)


def ssd_bwd(xbc_act, dt, acum, states, dy, dskip):
    t = xbc_act.shape[0]
    nc = t // SSM_CHUNK
    c_ = SSM_CHUNK
    rev = lambda c: nc - 1 - c

    def body(xs_ref, b_ref, c_ref, dt_ref, ac_ref, s_ref, dy_ref, sk_ref,
             dxs_ref, db_ref, dc_ref, ddt_ref, dac_ref, ds_scr):
        p = pl.program_id(1)

        @pl.when(pl.program_id(0) == 0)
        def _():
            ds_scr[p] = jnp.zeros((SSM_DSTATE, 128), F32)

        fn = functools.partial(_ssd_chunk, pair=p, dot=_bdot_vjp)
        _, vjp = jax.vjp(fn, xs_ref[...], dt_ref[...], ac_ref[...], b_ref[...], c_ref[...], s_ref[...])
        dxs, ddt, dac, db, dc, ds = vjp((dy_ref[...], ds_scr[p]))
        dxs_ref[...] = dxs + sk_ref[...]
        ds_scr[p] = ds

        @pl.when(p % PAIRS_PER_GROUP == 0)
        def _():
            db_ref[...] = db
            dc_ref[...] = dc

        @pl.when(p % PAIRS_PER_GROUP != 0)
        def _():
            db_ref[...] += db
            dc_ref[...] += dc

        @pl.when(p == 0)
        def _():
            ddt_ref[...] = ddt
            dac_ref[...] = dac

        @pl.when(p != 0)
        def _():
            ddt_ref[...] += ddt
            dac_ref[...] += dac

    return pl.pallas_call(
        body, name="ssd_bwd", grid=(nc, SSM_PAIRS),
        in_specs=[pl.BlockSpec((c_, 128), lambda c, p: (rev(c), p)),
                  pl.BlockSpec((c_, 128), lambda c, p: (rev(c), _B_BLOCK0 + p // PAIRS_PER_GROUP)),
                  pl.BlockSpec((c_, 128), lambda c, p: (rev(c), _C_BLOCK0 + p // PAIRS_PER_GROUP)),
                  pl.BlockSpec((c_, DT_PAD), lambda c, p: (rev(c), 0)),
                  pl.BlockSpec((c_, DT_PAD), lambda c, p: (rev(c), 0)),
                  pl.BlockSpec((None, None, SSM_DSTATE, 128), lambda c, p: (rev(c), p, 0, 0)),
                  pl.BlockSpec((c_, 128), lambda c, p: (rev(c), p)),
                  pl.BlockSpec((c_, 128), lambda c, p: (rev(c), p))],
        out_specs=[pl.BlockSpec((c_, 128), lambda c, p: (rev(c), p)),
                   pl.BlockSpec((c_, 128), lambda c, p: (rev(c), p // PAIRS_PER_GROUP)),
                   pl.BlockSpec((c_, 128), lambda c, p: (rev(c), p // PAIRS_PER_GROUP)),
                   pl.BlockSpec((c_, DT_PAD), lambda c, p: (rev(c), 0)),
                   pl.BlockSpec((c_, DT_PAD), lambda c, p: (rev(c), 0))],
        out_shape=[jax.ShapeDtypeStruct((t, SSM_DINNER), F32),
                   jax.ShapeDtypeStruct((t, SSM_GROUPS * SSM_DSTATE), F32),
                   jax.ShapeDtypeStruct((t, SSM_GROUPS * SSM_DSTATE), F32),
                   jax.ShapeDtypeStruct((t, DT_PAD), F32),
                   jax.ShapeDtypeStruct((t, DT_PAD), F32)],
        scratch_shapes=[pltpu.VMEM((SSM_PAIRS, SSM_DSTATE, 128), F32)],
        compiler_params=_cp(("arbitrary", "arbitrary")),
    )(xbc_act, xbc_act, xbc_act, dt, acum, states, dy, dskip)


def rowwise(name, fn, row_ins, par_ins, row_outs, acc_outs, *, tt, ncb=1):
    t = row_ins[0][0].shape[0]
    assert t % tt == 0
    n_ri, n_pi, n_ro, n_ao = len(row_ins), len(par_ins), len(row_outs), len(acc_outs)

    def body(*refs):
        i = pl.program_id(1)
        ins = [r[...] for r in refs[:n_ri + n_pi]]
        outs = fn(*ins)
        ro_refs = refs[n_ri + n_pi:n_ri + n_pi + n_ro]
        ao_refs = refs[n_ri + n_pi + n_ro:]
        for r, v in zip(ro_refs, outs[:n_ro]):
            r[...] = v.astype(r.dtype)
        for r, v in zip(ao_refs, outs[n_ro:]):
            @pl.when(i == 0)
            def _(r=r, v=v):
                r[...] = v

            @pl.when(i > 0)
            def _(r=r, v=v):
                r[...] += v

    in_specs = [pl.BlockSpec((tt, bc), lambda j, i, off=off: (i, off + j)) for _, bc, off in row_ins]
    in_specs += [pl.BlockSpec((a.shape[0], bc), lambda j, i, off=off: (0, off + j)) for a, bc, off in par_ins]
    out_specs = [pl.BlockSpec((tt, bc), lambda j, i: (i, j)) for _, bc, _ in row_outs]
    out_specs += [pl.BlockSpec((r, bc), lambda j, i: (0, j)) for r, _, bc in acc_outs]
    out_shape = [jax.ShapeDtypeStruct((t, c), dt) for c, _, dt in row_outs]
    out_shape += [jax.ShapeDtypeStruct((r, c), F32) for r, c, _ in acc_outs]
    return pl.pallas_call(
        body, name=name, grid=(ncb, t // tt), in_specs=in_specs, out_specs=out_specs, out_shape=out_shape,
        compiler_params=_cp(("parallel", "arbitrary")),
    )(*[a for a, _, _ in row_ins], *[a for a, _, _ in par_ins])


def _colsum(v):
    return jnp.sum(v, axis=0, keepdims=True)


def _softplus(x):
    return jnp.maximum(x, 0.0) + jnp.log(1.0 + jnp.exp(-jnp.abs(x)))


def _gelu_tanh(x):
    return 0.5 * x * (1.0 + jnp.tanh(0.7978845608028654 * (x + 0.044715 * (x * x * x))))


D = D_MODEL


def norm_fwd(x, w):
    return rowwise("norm_fwd", lambda xv, wv: (_rms(xv, wv),), [(x, D, 0)], [(w, D, 0)], [(D, D, BF16)], [], tt=256)[0]


def norm_bwd(x, w, dh, dres):
    def fn(xv, dhv, drv, wv):
        _, vjp = jax.vjp(_rms, xv, wv)
        dx, dw = vjp(dhv)
        return dx + drv, dw
    return rowwise("norm_bwd", fn, [(x, D, 0), (dh, D, 0), (dres, D, 0)], [(w, D, 0)], [(D, D, F32)], [(1, D, D)], tt=256)


def _dt_fn(dtr, bias, a_log):
    c = dtr.shape[0]
    dt = _softplus(dtr + bias)
    da = dt * (-jnp.exp(a_log))
    tril = jnp.where(_iota((c, c), 1) <= _iota((c, c), 0), 1.0, 0.0).astype(F32)
    return dt, _hdot(tril, da)


def dt_fwd(dtr, bias, a_log):
    return rowwise("dt_fwd", _dt_fn, [(dtr, DT_PAD, 0)], [(bias, DT_PAD, 0), (a_log, DT_PAD, 0)],
                   [(DT_PAD, DT_PAD, F32), (DT_PAD, DT_PAD, F32)], [], tt=SSM_CHUNK)


def dt_bwd(dtr, bias, a_log, ddt, dacum):
    def fn(dtrv, ddtv, dacv, bv, av):
        _, vjp = jax.vjp(_dt_fn, dtrv, bv, av)
        return vjp((ddtv, dacv))
    return rowwise("dt_bwd", fn, [(dtr, DT_PAD, 0), (ddt, DT_PAD, 0), (dacum, DT_PAD, 0)],
                   [(bias, DT_PAD, 0), (a_log, DT_PAD, 0)],
                   [(DT_PAD, DT_PAD, BF16)], [(1, DT_PAD, DT_PAD), (1, DT_PAD, DT_PAD)], tt=SSM_CHUNK)


GROUP_W = SSM_DINNER // SSM_GROUPS


def _ssm_post_fn(yv, xsv, zv, dexp, nw):
    return _rms((yv + dexp * xsv) * _silu(zv), nw)


def ssm_post_fwd(yssd, xbc_act, z, dexp, nw):
    return rowwise("ssm_post_fwd", lambda *a: (_ssm_post_fn(*a),),
                   [(yssd, GROUP_W, 0), (xbc_act, GROUP_W, 0), (z, GROUP_W, 0)], [(dexp, GROUP_W, 0), (nw, GROUP_W, 0)],
                   [(SSM_DINNER, GROUP_W, BF16)], [], tt=512, ncb=SSM_GROUPS)[0]


def ssm_post_bwd(yssd, xbc_act, z, dexp, nw, dy):
    def fn(yv, xsv, zv, dyv, dv, nv):
        _, vjp = jax.vjp(_ssm_post_fn, yv, xsv, zv, dv, nv)
        return vjp(dyv)
    return rowwise("ssm_post_bwd", fn,
                   [(yssd, GROUP_W, 0), (xbc_act, GROUP_W, 0), (z, GROUP_W, 0), (dy, GROUP_W, 0)],
                   [(dexp, GROUP_W, 0), (nw, GROUP_W, 0)],
                   [(SSM_DINNER, GROUP_W, F32), (SSM_DINNER, GROUP_W, F32), (SSM_DINNER, GROUP_W, BF16)],
                   [(1, SSM_DINNER, GROUP_W), (1, SSM_DINNER, GROUP_W)], tt=512, ncb=SSM_GROUPS)


def _merge_fn(ah, asm, gh, gs):
    return _sigmoid(gh) * ah + _sigmoid(gs) * asm


def merge_fwd(a_hg, a_ssm, gates):
    return rowwise("merge_fwd", lambda *a: (_merge_fn(*a),), [(a_hg, D, 0), (a_ssm, D, 0), (gates, D, 0), (gates, D, 1)], [],
                   [(D, D, BF16)], [], tt=256)[0]


def merge_bwd(a_hg, a_ssm, gates, dmixed):
    def fn(ah, asm, gh, gs, dm):
        _, vjp = jax.vjp(_merge_fn, ah, asm, gh, gs)
        return vjp(dm)
    return rowwise("merge_bwd", fn, [(a_hg, D, 0), (a_ssm, D, 0), (gates, D, 0), (gates, D, 1), (dmixed, D, 0)], [],
                   [(D, D, BF16)] * 4, [], tt=256)


def _post1_fn(xv, uv, wpost, wpre):
    x1 = xv + _rms(uv, wpost)
    return x1, _rms(x1, wpre)


def post1_fwd(x, u, wpost, wpre):
    return rowwise("post1_fwd", _post1_fn, [(x, D, 0), (u, D, 0)], [(wpost, D, 0), (wpre, D, 0)],
                   [(D, D, F32), (D, D, BF16)], [], tt=256)


def post1_bwd(x, u, wpost, wpre, dx1, dh2):
    def fn(xv, uv, d1, d2, wa, wb):
        _, vjp = jax.vjp(_post1_fn, xv, uv, wa, wb)
        dx, du, dwa, dwb = vjp((d1, d2))
        return du, dx, dwa, dwb
    return rowwise("post1_bwd", fn, [(x, D, 0), (u, D, 0), (dx1, D, 0), (dh2, D, 0)], [(wpost, D, 0), (wpre, D, 0)],
                   [(D, D, BF16), (D, D, F32)], [(1, D, D), (1, D, D)], tt=256)


def final_fwd_bwd(x1, fo, w, target):
    def fn(x1v, fov, tv, wv):
        def loss_fn(a, b, c):
            err = a + _rms(b, c) - tv
            return 0.5 * jnp.sum(err * err) * (1.0 / D)
        loss, vjp = jax.vjp(loss_fn, x1v, fov, wv)
        dx, dfo, dw = vjp(jnp.ones((), F32))
        return dx, dfo, dw, jnp.full((1, 128), loss, F32)
    return rowwise("final_fwd_bwd", fn, [(x1, D, 0), (fo, D, 0), (target, D, 0)], [(w, D, 0)],
                   [(D, D, F32), (D, D, BF16)], [(1, D, D), (1, 128, 128)], tt=256)


HALO = 8
CONV_TT = 512
CONV_CB = 512


def _tail(kind, c, up):
    return _silu(c) if kind == "silu" else _gelu_tanh(c) * up


def conv_fwd(name, x, xoff, w, b, kind, up=None, upoff=0, act_dtype=F32):
    t = x.shape[0]
    k_, c_ = w.shape
    tt, cb = CONV_TT, CONV_CB
    hb = tt // HALO
    has_up = up is not None

    def body(*refs):
        if has_up:
            x_ref, xp_ref, w_ref, b_ref, up_ref, c_ref, a_ref, scr = refs
        else:
            x_ref, xp_ref, w_ref, b_ref, c_ref, a_ref, scr = refs
        i = pl.program_id(1)
        scr[0:HALO, :] = jnp.where(i == 0, 0.0, xp_ref[...])
        scr[HALO:HALO + tt, :] = x_ref[...]
        acc = jnp.zeros((tt, cb), F32) + b_ref[...]
        for k in range(k_):
            acc = acc + w_ref[k:k + 1, :] * scr[pl.ds(HALO - (k_ - 1) + k, tt), :]
        c_ref[...] = acc
        a_ref[...] = _tail(kind, acc, up_ref[...] if has_up else None).astype(act_dtype)

    in_specs = [pl.BlockSpec((tt, cb), lambda j, i: (i, xoff + j)),
                pl.BlockSpec((HALO, cb), lambda j, i: (jnp.maximum(i * hb - 1, 0), xoff + j)),
                pl.BlockSpec((k_, cb), lambda j, i: (0, j)),
                pl.BlockSpec((1, cb), lambda j, i: (0, j))]
    args = [x, x, w, b]
    if has_up:
        in_specs.append(pl.BlockSpec((tt, cb), lambda j, i: (i, upoff + j)))
        args.append(up)
    return pl.pallas_call(
        body, name=name, grid=(c_ // cb, t // tt), in_specs=in_specs,
        out_specs=[pl.BlockSpec((tt, cb), lambda j, i: (i, j))] * 2,
        out_shape=[jax.ShapeDtypeStruct((t, c_), F32), jax.ShapeDtypeStruct((t, c_), act_dtype)],
        scratch_shapes=[pltpu.VMEM((tt + HALO, cb), F32)],
        compiler_params=_cp(("parallel", "arbitrary")),
    )(*args)


def conv_bwd(name, x, xoff, c, coff, dact, w, kind, up=None, upoff=0):
    t = x.shape[0]
    k_, c_ = w.shape[0], dact.shape[1]
    tt, cb = CONV_TT, CONV_CB
    hb = tt // HALO
    nt = t // tt
    has_up = up is not None

    def tail_grad(cv, dav, upv):
        if has_up:
            _, vjp = jax.vjp(lambda a, u: _tail(kind, a, u), cv, upv)
            return vjp(dav)
        _, vjp = jax.vjp(lambda a: _tail(kind, a, None), cv)
        return vjp(dav)[0], None

    def body(*refs):
        if has_up:
            (x_ref, xp_ref, c_ref, cn_ref, da_ref, dan_ref, w_ref, up_ref, upn_ref,
             dx_ref, dup_ref, dw_ref, db_ref, xs, dcs) = refs
        else:
            x_ref, xp_ref, c_ref, cn_ref, da_ref, dan_ref, w_ref, dx_ref, dw_ref, db_ref, xs, dcs = refs
        i = pl.program_id(1)
        xs[0:HALO, :] = jnp.where(i == 0, 0.0, xp_ref[...])
        xs[HALO:HALO + tt, :] = x_ref[...]
        dc, dup = tail_grad(c_ref[...], da_ref[...].astype(F32), up_ref[...] if has_up else None)
        dcn, _ = tail_grad(cn_ref[...], dan_ref[...].astype(F32), upn_ref[...] if has_up else None)
        dcs[0:tt, :] = dc
        dcs[tt:tt + HALO, :] = jnp.where(i == nt - 1, 0.0, dcn)
        if has_up:
            dup_ref[...] = dup.astype(BF16)
        dx = jnp.zeros((tt, cb), F32)
        dws = []
        for k in range(k_):
            dx = dx + w_ref[k:k + 1, :] * dcs[pl.ds(k_ - 1 - k, tt), :]
            dws.append(_colsum(dc * xs[pl.ds(HALO - (k_ - 1) + k, tt), :]))
        dx_ref[...] = dx.astype(BF16)

        @pl.when(i == 0)
        def _():
            dw_ref[...] = jnp.zeros_like(dw_ref)
            db_ref[...] = jnp.zeros_like(db_ref)

        for k in range(k_):
            dw_ref[k:k + 1, :] += dws[k]
        db_ref[...] += _colsum(dc)

    tile = lambda off: pl.BlockSpec((tt, cb), lambda j, i, off=off: (i, off + j))
    prev = lambda off: pl.BlockSpec((HALO, cb), lambda j, i, off=off: (jnp.maximum(i * hb - 1, 0), off + j))
    nxt = lambda off: pl.BlockSpec((HALO, cb), lambda j, i, off=off: (jnp.minimum((i + 1) * hb, t // HALO - 1), off + j))
    in_specs = [tile(xoff), prev(xoff), tile(coff), nxt(coff), tile(0), nxt(0),
                pl.BlockSpec((k_, cb), lambda j, i: (0, coff + j))]
    args = [x, x, c, c, dact, dact, w]
    if has_up:
        in_specs += [tile(upoff), nxt(upoff)]
        args += [up, up]
    out_specs = [tile(0)] + ([tile(0)] if has_up else []) + [pl.BlockSpec((HALO, cb), lambda j, i: (0, j)),
                                                            pl.BlockSpec((1, cb), lambda j, i: (0, j))]
    out_shape = [jax.ShapeDtypeStruct((t, c_), BF16)] * (2 if has_up else 1)
    out_shape += [jax.ShapeDtypeStruct((HALO, c_), F32), jax.ShapeDtypeStruct((1, c_), F32)]
    return pl.pallas_call(
        body, name=name, grid=(c_ // cb, nt), in_specs=in_specs, out_specs=out_specs, out_shape=out_shape,
        scratch_shapes=[pltpu.VMEM((tt + HALO, cb), F32), pltpu.VMEM((tt + HALO, cb), F32)],
        compiler_params=_cp(("parallel", "arbitrary")),
    )(*args)


def ew_sum(name, parts, rows, out_dtype, tr):
    c = parts[0][0].shape[1]
    tr = min(tr, rows)
    assert rows % tr == 0 and all(off % tr == 0 for _, off in parts)
    n = len(parts)

    def body(*refs):
        acc = refs[0][...].astype(F32)
        for ref in refs[1:n]:
            acc = acc + ref[...].astype(F32)
        refs[n][...] = acc.astype(out_dtype)

    in_specs = [pl.BlockSpec((tr, c), lambda i, o=off // tr: (i + o, 0)) for _, off in parts]
    return pl.pallas_call(body, name=name, grid=(rows // tr,), in_specs=in_specs,
                          out_specs=pl.BlockSpec((tr, c), lambda i: (i, 0)),
                          out_shape=jax.ShapeDtypeStruct((rows, c), out_dtype),
                          compiler_params=_cp(("parallel",)))(*[a for a, _ in parts])


def fold_heads(dexp):
    def body(d_ref, o_ref):
        sel = jnp.where(_iota((SSM_DINNER, DT_PAD), 0) // SSM_HEADDIM == _iota((SSM_DINNER, DT_PAD), 1), 1.0, 0.0)
        o_ref[...] = _hdot(jnp.broadcast_to(d_ref[...], (8, SSM_DINNER)), sel.astype(F32))[0:1, :]

    return pl.pallas_call(body, name="fold_heads", out_shape=jax.ShapeDtypeStruct((1, DT_PAD), F32),
                          compiler_params=pltpu.CompilerParams(vmem_limit_bytes=VMEM_LIMIT))(dexp)


def adamw(name, w, g, m, v, tr):
    r, c = w.shape
    tr = min(tr, r)
    assert r % tr == 0, (r, tr)

    def body(w_ref, g_ref, m_ref, v_ref, d_ref, nm_ref, nv_ref):
        gv = g_ref[...]
        nm = ADAM_B1 * m_ref[...] + (1.0 - ADAM_B1) * gv
        nv = ADAM_B2 * v_ref[...] + (1.0 - ADAM_B2) * (gv * gv)
        m_hat = nm / (1.0 - ADAM_B1 ** ADAM_STEP)
        v_hat = nv / (1.0 - ADAM_B2 ** ADAM_STEP)
        d_ref[...] = -ADAM_LR * (m_hat / (jnp.sqrt(v_hat) + ADAM_EPS) + ADAM_WD * w_ref[...])
        nm_ref[...] = nm
        nv_ref[...] = nv

    spec = pl.BlockSpec((tr, c), lambda i: (i, 0))
    shp = jax.ShapeDtypeStruct((r, c), F32)
    return pl.pallas_call(body, name=name, grid=(r // tr,), in_specs=[spec] * 4, out_specs=[spec] * 3,
                          out_shape=[shp] * 3, compiler_params=_cp(("parallel",)))(w, g, m, v)


SEG_QFIG, SEG_Z, SEG_XBC, SEG_DT, SEG_G = 0, 8192, 12288, 18432, 18496
IN_TOTAL = 22592
FFN_BLOCKS = D_FF // CONV_CB


def local_step(x, target, wts, par):
    t = x.shape[0]
    pad64 = lambda a: jnp.pad(a, ((0, 0), (0, DT_PAD - a.shape[1])))
    bias, a_log = pad64(par["ssm_dt_bias"]), pad64(par["ssm_A_log"])
    dexp = jnp.repeat(par["ssm_D"], SSM_HEADDIM, axis=1)
    in_t = wts["in_t"]

    h = norm_fwd(x, par["mix_pre_norm"])
    proj = lambda nm, off, n, tn: mm(h, in_t, "nt", name=nm, tn=tn, dims=(t, n, D), b_off=(off, 0))
    qfig = proj("proj_qfig", SEG_QFIG, 8192, 1024)
    z = proj("proj_z", SEG_Z, 4096, 1024)
    xbc = proj("proj_xbc", SEG_XBC, 6144, 1024)
    dtr = mm(h, wts["dt_t"], "nt", name="proj_dt", tn=128)
    gates = mm(h, wts["g_t"], "nt", name="proj_gates", tn=1024)
    y_hg, hg_states = hgrn2_fwd(qfig, par["hg_lb_table"], par["hg_out_norm"])
    c_ssm, xbc_act = conv_fwd("ssm_conv_fwd", xbc, 0, par["ssm_conv_w"], par["ssm_conv_b"], "silu")
    dt, acum = dt_fwd(dtr, bias, a_log)
    yssd, ssd_states = ssd_fwd(xbc_act, dt, acum)
    y_ssm = ssm_post_fwd(yssd, xbc_act, z, dexp, par["ssm_out_norm"])
    a_hg = mm(y_hg, wts["bh"], "nn", name="branch_hg", tn=1024)
    a_ssm = mm(y_ssm, wts["bs"], "nn", name="branch_ssm", tn=1024, tk=2048)
    mixed = merge_fwd(a_hg, a_ssm, gates)
    u = mm(mixed, wts["o"], "nn", name="out_proj", tn=1024)
    x1, h2 = post1_fwd(x, u, par["mix_post_norm"], par["ffn_pre_norm"])
    gu = mm(h2, wts["up"], "nn", name="ffn_up", tn=1024)
    c_ffn, act = conv_fwd("ffn_conv_fwd", gu, 0, par["ffn_conv_w"], par["ffn_conv_b"], "gelu_mul",
                          up=gu, upoff=FFN_BLOCKS, act_dtype=BF16)
    fo = mm(act, wts["dn"], "nn", name="ffn_down", tn=1024, tk=1408)
    dx2, dfo, g_ffn_post, loss = final_fwd_bwd(x1, fo, par["ffn_post_norm"], target)

    dact = mm(dfo, wts["dn"], "nt", name="d_act", tn=1408)
    g_dn = mm(act, dfo, "tn", name="g_ffn_down", out_dtype=BF16, tm=1408, tn=2048, tk=512)
    dgate, dup, g_fcw, g_fcb = conv_bwd("ffn_conv_bwd", gu, 0, c_ffn, 0, dact, par["ffn_conv_w"], "gelu_mul",
                                        up=gu, upoff=FFN_BLOCKS)
    dh2 = mm(dgate, wts["up"], "nt", name="d_h2_gate", tn=1024, tk=1408, dims=(t, D, D_FF))
    dh2 = mm(dup, wts["up"], "nt", name="d_h2_up", tn=1024, tk=1408, dims=(t, D, D_FF), b_off=(0, D_FF), acc=dh2)
    g_up_gate = mm(h2, dgate, "tn", name="g_ffn_up_gate", out_dtype=BF16, tm=2048, tn=1408, tk=512)
    g_up_up = mm(h2, dup, "tn", name="g_ffn_up_up", out_dtype=BF16, tm=2048, tn=1408, tk=512)
    du, dx1, g_mix_post, g_ffn_pre = post1_bwd(x, u, par["mix_post_norm"], par["ffn_pre_norm"], dx2, dh2)
    dmixed = mm(du, wts["o"], "nt", name="d_mixed", tn=1024)
    g_o = mm(mixed, du, "tn", name="g_w_out", out_dtype=BF16, tm=1024, tn=2048, tk=512)
    da_hg, da_ssm, dg_hg, dg_ssm = merge_bwd(a_hg, a_ssm, gates, dmixed)
    dy_hg = mm(da_hg, wts["bh"], "nt", name="d_y_hg", out_dtype=BF16, tn=1024)
    g_bh = mm(y_hg, da_hg, "tn", name="g_w_branch_hg", out_dtype=BF16, tm=1024, tn=2048, tk=512)
    dy_ssm = mm(da_ssm, wts["bs"], "nt", name="d_y_ssm", tn=1024)
    g_bs = mm(y_ssm, da_ssm, "tn", name="g_w_branch_ssm", out_dtype=BF16, tm=1024, tn=2048, tk=512)
    dyssd, dskip, dz, g_dexp, g_ssm_norm = ssm_post_bwd(yssd, xbc_act, z, dexp, par["ssm_out_norm"], dy_ssm)
    dxs, db_, dc_, ddt, dacum = ssd_bwd(xbc_act, dt, acum, ssd_states, dyssd, dskip)
    ddtr, g_dt_bias, g_a_log = dt_bwd(dtr, bias, a_log, ddt, dacum)
    xs_blocks, bc_blocks = SSM_DINNER // CONV_CB, SSM_GROUPS * SSM_DSTATE // CONV_CB
    dxbc_x, g_cw_x, g_cb_x = conv_bwd("ssm_conv_bwd_x", xbc, 0, c_ssm, 0, dxs, par["ssm_conv_w"], "silu")
    dxbc_b, g_cw_b, g_cb_b = conv_bwd("ssm_conv_bwd_b", xbc, xs_blocks, c_ssm, xs_blocks, db_, par["ssm_conv_w"], "silu")
    dxbc_c, g_cw_c, g_cb_c = conv_bwd("ssm_conv_bwd_c", xbc, xs_blocks + bc_blocks, c_ssm, xs_blocks + bc_blocks, dc_,
                                      par["ssm_conv_w"], "silu")
    dq, df, dv, dg, g_table, g_hg_norm = hgrn2_bwd(qfig, par["hg_lb_table"], par["hg_out_norm"], hg_states, dy_hg)

    dsegs = [(dq, SEG_QFIG), (df, SEG_QFIG + 2048), (dv, SEG_QFIG + 4096), (dg, SEG_QFIG + 6144), (dz, SEG_Z),
             (dxbc_x, SEG_XBC), (dxbc_b, SEG_XBC + SSM_DINNER), (dxbc_c, SEG_XBC + SSM_DINNER + 1024)]
    dh = None
    g_in_parts = []
    for n, (dseg, off) in enumerate(dsegs):
        w_ = dseg.shape[1]
        dh = mm(dseg, in_t, "nn", name=f"d_h_{n}", tn=1024, tk=1024, dims=(t, D, w_), b_off=(off, 0), acc=dh)
        g_in_parts.append(mm(dseg, h, "tn", name=f"g_w_in_{n}", out_dtype=BF16, tm=1024, tn=2048, tk=512))
    dh = mm(ddtr, wts["dt_t"], "nn", name="d_h_dt", tn=1024, acc=dh)
    g_dt_t = mm(ddtr, h, "tn", name="g_w_in_dt", out_dtype=BF16, tm=128, tn=2048, tk=512)[:SSM_HEADS]
    for n, dgate_ in enumerate((dg_hg, dg_ssm)):
        dh = mm(dgate_, wts["g_t"], "nn", name=f"d_h_g{n}", tn=1024, tk=1024, dims=(t, D, D), b_off=(n * D, 0), acc=dh)
        g_in_parts.append(mm(dgate_, h, "tn", name=f"g_w_in_g{n}", out_dtype=BF16, tm=1024, tn=2048, tk=512))
    g_in_t = jnp.concatenate(g_in_parts[:8] + [g_dt_t] + g_in_parts[8:], axis=0)
    grad_x, g_mix_pre = norm_bwd(x, par["mix_pre_norm"], dh, dx1)

    big = dict(in_t=g_in_t, bh=g_bh, bs=g_bs, o=g_o, up_gate=g_up_gate, up_up=g_up_up, dn=g_dn)
    g_conv_w = jnp.concatenate([g_cw_x, g_cw_b, g_cw_c], axis=1)[:SSM_CONV]
    g_conv_b = jnp.concatenate([g_cb_x, g_cb_b, g_cb_c], axis=1)
    small = dict(mix_pre_norm=g_mix_pre, mix_post_norm=g_mix_post, hg_lb_table=g_table, hg_out_norm=g_hg_norm,
                 ssm_conv_w=g_conv_w, ssm_conv_b=g_conv_b, ssm_dt_bias=g_dt_bias, ssm_A_log=g_a_log,
                 ssm_D=g_dexp, ssm_out_norm=g_ssm_norm, ffn_pre_norm=g_ffn_pre, ffn_post_norm=g_ffn_post,
                 ffn_conv_w=g_fcw[:FFN_CONV], ffn_conv_b=g_fcb)
    return loss, grad_x, big, small


MESH = pl.DeviceIdType.MESH
ANY = pl.BlockSpec(memory_space=pl.ANY)
N_CHIPS = 4
PACK_ROWS = 9120
PACK_SPLITS = (5648, 6160, 7184, 7696, 9104)
UP_ROWS, UP_COLS = D_MODEL, 2816


def _place():
    x, y, c = lax.axis_index("x"), lax.axis_index("y"), lax.axis_index("c")
    chips = [(1 - x, y), (x, 1 - y), (1 - x, 1 - y)]
    return x, y, c, chips


def _rcopy(src, dst, send_sems, recv_sems, k, dev):
    return pltpu.make_async_remote_copy(src_ref=src, dst_ref=dst, send_sem=send_sems.at[k], recv_sem=recv_sems.at[k],
                                        device_id=dev, device_id_type=MESH)


def gather_weights(p1, p2):
    arrays = ((p1.shape[0] // 2,), (p2.shape[0] // 2,))

    def body(p1_ref, p2_ref, g1_ref, g2_ref, send_sems, recv_sems, local_sems):
        x, y, c, chips = _place()
        own = 2 * x + y
        sib = (x, y, 1 - c)
        pairs = ((p1_ref, g1_ref, arrays[0][0]), (p2_ref, g2_ref, arrays[1][0]))
        local = [pltpu.make_async_copy(p, g.at[own], local_sems.at[a]) for a, (p, g, _) in enumerate(pairs)]
        for cp in local:
            cp.start()
        first, passed = [], []
        for a, (p, g, hrows) in enumerate(pairs):
            mine = pl.ds(pl.multiple_of(c * hrows, 16), hrows)
            for j, chip in enumerate(chips):
                first.append(_rcopy(p.at[mine], g.at[own, mine], send_sems, recv_sems, 6 * a + j, (*chip, c)))
        for cp in first:
            cp.start()
        for a, (p, g, hrows) in enumerate(pairs):
            mine = pl.ds(pl.multiple_of(c * hrows, 16), hrows)
            for j, chip in enumerate(chips):
                theirs = 2 * chip[0] + chip[1]
                _rcopy(g.at[theirs, mine], g.at[theirs, mine], send_sems, recv_sems, 6 * a + j, (*chip, c)).wait_recv()
                fw = _rcopy(g.at[theirs, mine], g.at[theirs, mine], send_sems, recv_sems, 6 * a + 3 + j, sib)
                fw.start()
                passed.append(fw)
        for a, (p, g, hrows) in enumerate(pairs):
            other = pl.ds(pl.multiple_of((1 - c) * hrows, 16), hrows)
            for j, chip in enumerate(chips):
                theirs = 2 * chip[0] + chip[1]
                _rcopy(g.at[theirs, other], g.at[theirs, other], send_sems, recv_sems, 6 * a + 3 + j, sib).wait_recv()
        for cp in first + passed:
            cp.wait_send()
        for cp in local:
            cp.wait()

    return pl.pallas_call(
        body, name="gather_weights", in_specs=[ANY, ANY], out_specs=[ANY, ANY],
        out_shape=[jax.ShapeDtypeStruct((N_CHIPS,) + p1.shape, p1.dtype), jax.ShapeDtypeStruct((N_CHIPS,) + p2.shape, p2.dtype)],
        scratch_shapes=[pltpu.SemaphoreType.DMA((12,)), pltpu.SemaphoreType.DMA((12,)), pltpu.SemaphoreType.DMA((2,))],
    )(p1, p2)


def pair_exchange(g1, g2):
    shapes = [(N_CHIPS, g.shape[1] // 2, g.shape[2]) for g in (g1, g2)]

    def body(g1_ref, g2_ref, o1_ref, r1_ref, o2_ref, r2_ref, send_sems, recv_sems, local_sems):
        x, y, c, _ = _place()
        sib = (x, y, 1 - c)
        cps, loc = [], []
        for a, (g, o, r) in enumerate(((g1_ref, o1_ref, r1_ref), (g2_ref, o2_ref, r2_ref))):
            hrows = shapes[a][1]
            mine = pl.ds(pl.multiple_of(c * hrows, 16), hrows)
            other = pl.ds(pl.multiple_of((1 - c) * hrows, 16), hrows)
            loc.append(pltpu.make_async_copy(g.at[:, mine], o, local_sems.at[a]))
            cps.append(_rcopy(g.at[:, other], r, send_sems, recv_sems, a, sib))
        for cp in loc + cps:
            cp.start()
        for cp in cps:
            cp.wait()
        for cp in loc:
            cp.wait()

    return pl.pallas_call(
        body, name="pair_exchange", in_specs=[ANY, ANY], out_specs=[ANY] * 4,
        out_shape=[jax.ShapeDtypeStruct(shapes[0], g1.dtype)] * 2 + [jax.ShapeDtypeStruct(shapes[1], g2.dtype)] * 2,
        scratch_shapes=[pltpu.SemaphoreType.DMA((2,)), pltpu.SemaphoreType.DMA((2,)), pltpu.SemaphoreType.DMA((2,))],
    )(g1, g2)


def chip_exchange(s1, s2):
    def body(s1_ref, s2_ref, o1_ref, r1_ref, o2_ref, r2_ref, send_sems, recv_sems, local_sems):
        x, y, c, chips = _place()
        own = 2 * x + y
        cps, loc = [], []
        for a, (s, o, r) in enumerate(((s1_ref, o1_ref, r1_ref), (s2_ref, o2_ref, r2_ref))):
            loc.append(pltpu.make_async_copy(s.at[own], o, local_sems.at[a]))
            for j, chip in enumerate(chips):
                cps.append(_rcopy(s.at[2 * chip[0] + chip[1]], r.at[j], send_sems, recv_sems, 3 * a + j, (*chip, c)))
        for cp in loc + cps:
            cp.start()
        for cp in cps:
            cp.wait()
        for cp in loc:
            cp.wait()

    outs = []
    for s in (s1, s2):
        outs += [jax.ShapeDtypeStruct(s.shape[1:], s.dtype), jax.ShapeDtypeStruct((3,) + s.shape[1:], s.dtype)]
    return pl.pallas_call(
        body, name="chip_exchange", in_specs=[ANY, ANY], out_specs=[ANY] * 4, out_shape=outs,
        scratch_shapes=[pltpu.SemaphoreType.DMA((6,)), pltpu.SemaphoreType.DMA((6,)), pltpu.SemaphoreType.DMA((2,))],
    )(s1, s2)


def pair_assemble(r1, r2):
    def body(r1_ref, r2_ref, f1_ref, f2_ref, send_sems, recv_sems, local_sems):
        x, y, c, _ = _place()
        sib = (x, y, 1 - c)
        cps, loc = [], []
        for a, (r, f) in enumerate(((r1_ref, f1_ref), (r2_ref, f2_ref))):
            hrows = r.shape[0]
            mine = pl.ds(pl.multiple_of(c * hrows, 8), hrows)
            loc.append(pltpu.make_async_copy(r, f.at[mine], local_sems.at[a]))
            cps.append(_rcopy(r, f.at[mine], send_sems, recv_sems, a, sib))
        for cp in loc + cps:
            cp.start()
        for a, (r, f) in enumerate(((r1_ref, f1_ref), (r2_ref, f2_ref))):
            hrows = r.shape[0]
            other = pl.ds(pl.multiple_of((1 - c) * hrows, 8), hrows)
            _rcopy(r, f.at[other], send_sems, recv_sems, a, sib).wait_recv()
        for cp in cps:
            cp.wait_send()
        for cp in loc:
            cp.wait()

    return pl.pallas_call(
        body, name="pair_assemble", in_specs=[ANY, ANY], out_specs=[ANY, ANY],
        out_shape=[jax.ShapeDtypeStruct((2 * r.shape[0], r.shape[1]), r.dtype) for r in (r1, r2)],
        scratch_shapes=[pltpu.SemaphoreType.DMA((2,)), pltpu.SemaphoreType.DMA((2,)), pltpu.SemaphoreType.DMA((2,))],
    )(r1, r2)


N_DEV = 8


def gather_small(blk, reduce):
    rows, cols = blk.shape

    def body(x_ref, out_ref, all_ref, send_sems, recv_sems, local_sem):
        x, y, c, chips = _place()
        me, sib = (x, y, c), (x, y, 1 - c)

        def blk_rows(px, py, pc):
            return all_ref.at[pl.ds(pl.multiple_of((4 * px + 2 * py + pc) * rows, 8), rows), :]

        def copy(k, block, to, src=None):
            return _rcopy(blk_rows(*block) if src is None else src, blk_rows(*block), send_sems, recv_sems, k, to)

        mine = pltpu.make_async_copy(x_ref, blk_rows(*me), local_sem)
        mine.start()
        first = [copy(0, me, sib, src=x_ref)] + [copy(1 + j, me, (*chip, c), src=x_ref) for j, chip in enumerate(chips)]
        for cp in first:
            cp.start()
        passed = [copy(4 + j, (*chip, c), sib) for j, chip in enumerate(chips)]
        for j, chip in enumerate(chips):
            copy(1 + j, (*chip, c), me).wait_recv()
            passed[j].start()
        copy(0, sib, me).wait_recv()
        for j, chip in enumerate(chips):
            copy(4 + j, (*chip, 1 - c), me).wait_recv()
        for cp in first + passed:
            cp.wait_send()
        mine.wait()
        if reduce:
            acc = all_ref[0:rows, :]
            for d in range(1, N_DEV):
                acc = acc + all_ref[d * rows:(d + 1) * rows, :]
            out_ref[...] = acc
        else:
            out_ref[...] = all_ref[...]

    vmem = pl.BlockSpec(memory_space=pltpu.VMEM)
    return pl.pallas_call(
        body, name="reduce_small" if reduce else "gather_small", in_specs=[vmem], out_specs=vmem,
        out_shape=jax.ShapeDtypeStruct((rows if reduce else N_DEV * rows, cols), blk.dtype),
        scratch_shapes=[pltpu.VMEM((N_DEV * rows, cols), blk.dtype), pltpu.SemaphoreType.DMA((7,)),
                        pltpu.SemaphoreType.DMA((7,)), pltpu.SemaphoreType.DMA],
        compiler_params=pltpu.CompilerParams(vmem_limit_bytes=VMEM_LIMIT),
    )(blk)


WEIGHTS = ['w_in', 'mix_pre_norm', 'mix_post_norm', 'hg_lb_table', 'hg_out_norm', 'ssm_conv_w', 'ssm_conv_b',
           'ssm_dt_bias', 'ssm_A_log', 'ssm_D', 'ssm_out_norm', 'w_branch_hg', 'w_branch_ssm', 'w_out', 'ffn_pre_norm',
           'ffn_post_norm', 'ffn_w_up', 'ffn_conv_w', 'ffn_conv_b', 'ffn_w_down']
BIG = ('w_in', 'w_branch_hg', 'w_branch_ssm', 'w_out', 'ffn_w_up', 'ffn_w_down')
SMALL = tuple(n for n in WEIGHTS if n not in BIG)
CONV_SHARD = {'ssm_conv_w': SSM_CONV_DIM // N_CHIPS, 'ffn_conv_w': D_FF // N_CHIPS}
LANES = 128


def _pack(parts):
    flat = jnp.concatenate([p.reshape(-1) for p in parts])
    n = flat.shape[0]
    rows = -(-n // (8 * LANES)) * 8
    return jnp.pad(flat, (0, rows * LANES - n)).reshape(rows, LANES)


def _unpack(packed, shapes):
    flat = packed.reshape(-1)
    out, off = [], 0
    for s in shapes:
        n = int(np.prod(s))
        out.append(flat[off:off + n].reshape(s))
        off += n
    return out


def kernel(x, w_in, mix_pre_norm, mix_post_norm, hg_lb_table, hg_out_norm, ssm_conv_w, ssm_conv_b, ssm_dt_bias, ssm_A_log, ssm_D, ssm_out_norm, w_branch_hg, w_branch_ssm, w_out, ffn_pre_norm, ffn_post_norm, ffn_w_up, ffn_conv_w, ffn_conv_b, ffn_w_down, loss_target, m_w_in, m_mix_pre_norm, m_mix_post_norm, m_hg_lb_table, m_hg_out_norm, m_ssm_conv_w, m_ssm_conv_b, m_ssm_dt_bias, m_ssm_A_log, m_ssm_D, m_ssm_out_norm, m_w_branch_hg, m_w_branch_ssm, m_w_out, m_ffn_pre_norm, m_ffn_post_norm, m_ffn_w_up, m_ffn_conv_w, m_ffn_conv_b, m_ffn_w_down, v_w_in, v_mix_pre_norm, v_mix_post_norm, v_hg_lb_table, v_hg_out_norm, v_ssm_conv_w, v_ssm_conv_b, v_ssm_dt_bias, v_ssm_A_log, v_ssm_D, v_ssm_out_norm, v_w_branch_hg, v_w_branch_ssm, v_w_out, v_ffn_pre_norm, v_ffn_post_norm, v_ffn_w_up, v_ffn_conv_w, v_ffn_conv_b, v_ffn_w_down):
    w = dict(w_in=w_in, mix_pre_norm=mix_pre_norm, mix_post_norm=mix_post_norm, hg_lb_table=hg_lb_table, hg_out_norm=hg_out_norm, ssm_conv_w=ssm_conv_w, ssm_conv_b=ssm_conv_b, ssm_dt_bias=ssm_dt_bias, ssm_A_log=ssm_A_log, ssm_D=ssm_D, ssm_out_norm=ssm_out_norm, w_branch_hg=w_branch_hg, w_branch_ssm=w_branch_ssm, w_out=w_out, ffn_pre_norm=ffn_pre_norm, ffn_post_norm=ffn_post_norm, ffn_w_up=ffn_w_up, ffn_conv_w=ffn_conv_w, ffn_conv_b=ffn_conv_b, ffn_w_down=ffn_w_down)
    m = dict(w_in=m_w_in, mix_pre_norm=m_mix_pre_norm, mix_post_norm=m_mix_post_norm, hg_lb_table=m_hg_lb_table, hg_out_norm=m_hg_out_norm, ssm_conv_w=m_ssm_conv_w, ssm_conv_b=m_ssm_conv_b, ssm_dt_bias=m_ssm_dt_bias, ssm_A_log=m_ssm_A_log, ssm_D=m_ssm_D, ssm_out_norm=m_ssm_out_norm, w_branch_hg=m_w_branch_hg, w_branch_ssm=m_w_branch_ssm, w_out=m_w_out, ffn_pre_norm=m_ffn_pre_norm, ffn_post_norm=m_ffn_post_norm, ffn_w_up=m_ffn_w_up, ffn_conv_w=m_ffn_conv_w, ffn_conv_b=m_ffn_conv_b, ffn_w_down=m_ffn_w_down)
    v = dict(w_in=v_w_in, mix_pre_norm=v_mix_pre_norm, mix_post_norm=v_mix_post_norm, hg_lb_table=v_hg_lb_table, hg_out_norm=v_hg_out_norm, ssm_conv_w=v_ssm_conv_w, ssm_conv_b=v_ssm_conv_b, ssm_dt_bias=v_ssm_dt_bias, ssm_A_log=v_ssm_A_log, ssm_D=v_ssm_D, ssm_out_norm=v_ssm_out_norm, w_branch_hg=v_w_branch_hg, w_branch_ssm=v_w_branch_ssm, w_out=v_w_out, ffn_pre_norm=v_ffn_pre_norm, ffn_post_norm=v_ffn_post_norm, ffn_w_up=v_ffn_w_up, ffn_conv_w=v_ffn_conv_w, ffn_conv_b=v_ffn_conv_b, ffn_w_down=v_ffn_w_down)
    shard = 2 * lax.axis_index("x") + lax.axis_index("y")
    bf = lambda a: a.astype(BF16)

    p1 = jnp.concatenate([bf(w_in[0].T), bf(w_branch_hg[0]), bf(w_branch_ssm[0]), bf(w_out[0]), bf(ffn_w_down[0]),
                          jnp.zeros((PACK_ROWS - PACK_SPLITS[-1], D_MODEL), BF16)], axis=0)
    g1, g2 = gather_weights(p1, bf(ffn_w_up[0]))
    s0, s1, s2, s3, s4 = PACK_SPLITS
    in_t = g1[:, :s0].reshape(IN_TOTAL, D_MODEL)
    wts = dict(in_t=in_t, g_t=in_t[SEG_G:], dt_t=jnp.pad(in_t[SEG_DT:SEG_G], ((0, DT_PAD - SSM_HEADS), (0, 0))),
               bh=g1[:, s0:s1].reshape(-1, D_MODEL), bs=g1[:, s1:s2].reshape(-1, D_MODEL),
               o=g1[:, s2:s3].reshape(-1, D_MODEL), dn=g1[:, s3:s4].reshape(-1, D_MODEL),
               up=jnp.transpose(g2, (1, 0, 2)).reshape(D_MODEL, 2 * D_FF))
    conv_cols = max(CONV_SHARD.values())
    padc = lambda a: jnp.pad(a, ((0, 0), (0, conv_cols - a.shape[1])))
    conv_blk = jnp.concatenate([padc(ssm_conv_w[0]), padc(ffn_conv_w[0]), jnp.zeros((1, conv_cols), F32)], axis=0)
    conv_all = gather_small(conv_blk, reduce=False)
    par = {n: w[n] for n in SMALL}
    par["ssm_conv_w"] = jnp.concatenate([conv_all[16 * s:16 * s + SSM_CONV, :CONV_SHARD['ssm_conv_w']] for s in range(N_CHIPS)], axis=1)
    par["ffn_conv_w"] = jnp.concatenate([conv_all[16 * s + SSM_CONV:16 * s + SSM_CONV + FFN_CONV, :CONV_SHARD['ffn_conv_w']]
                                         for s in range(N_CHIPS)], axis=1)

    loss, grad_x, big, small = local_step(x[0], loss_target[0], wts, par)
    loss = lax.psum(loss[0, 0], ("x", "y", "c"))

    zpad = jnp.zeros((N_CHIPS, PACK_ROWS - PACK_SPLITS[-1], D_MODEL), BF16)
    gg1 = jnp.concatenate([big[k].reshape(N_CHIPS, -1, D_MODEL) for k in ("in_t", "bh", "bs", "o", "dn")] + [zpad], axis=1)
    gup = jnp.concatenate([big["up_gate"], big["up_up"]], axis=1)
    gg2 = jnp.transpose(gup.reshape(D_MODEL, N_CHIPS, UP_COLS), (1, 0, 2))
    o1, r1, o2, r2 = pair_exchange(gg1, gg2)
    h1, h2 = PACK_ROWS // 2, UP_ROWS // 2
    flat = lambda a: a.reshape(-1, a.shape[-1])
    c1 = ew_sum("pair_sum_1", [(flat(o1), 0), (flat(r1), 0)], N_CHIPS * h1, BF16, 480).reshape(N_CHIPS, h1, D_MODEL)
    c2 = ew_sum("pair_sum_2", [(flat(o2), 0), (flat(r2), 0)], N_CHIPS * h2, BF16, 512).reshape(N_CHIPS, h2, UP_COLS)
    ob1, rb1, ob2, rb2 = chip_exchange(c1, c2)
    red1 = ew_sum("chip_sum_1", [(ob1, 0)] + [(flat(rb1), j * h1) for j in range(3)], h1, F32, 240)
    red2 = ew_sum("chip_sum_2", [(ob2, 0)] + [(flat(rb2), j * h2) for j in range(3)], h2, F32, 256)
    f1, f2 = pair_assemble(red1, red2)
    grads = dict(w_in=f1[:s0].T, w_branch_hg=f1[s0:s1], w_branch_ssm=f1[s1:s2], w_out=f1[s2:s3], ffn_w_down=f1[s3:s4],
                 ffn_w_up=f2)

    small["hg_out_norm"] = ew_sum("sum_heads", [(small["hg_out_norm"][hd], 0) for hd in range(HG_HEADS)], 1, F32, 1)
    small["ssm_D"] = fold_heads(small["ssm_D"])[:, :SSM_HEADS]
    small["ssm_dt_bias"] = small["ssm_dt_bias"][:, :SSM_HEADS]
    small["ssm_A_log"] = small["ssm_A_log"][:, :SSM_HEADS]
    shapes = [small[n].shape for n in SMALL]
    summed = _unpack(gather_small(_pack([small[n] for n in SMALL]), reduce=True), shapes)
    for n, g in zip(SMALL, summed):
        if n in CONV_SHARD:
            g = lax.dynamic_slice_in_dim(g, shard * CONV_SHARD[n], CONV_SHARD[n], axis=1)
        grads[n] = g

    two_d = lambda a: a.reshape(a.shape[-2], a.shape[-1])
    delta, new_m, new_v = {}, {}, {}
    for n, tr in (("w_in", 64), ("w_branch_hg", 128), ("w_branch_ssm", 128), ("w_out", 128), ("ffn_w_up", 128), ("ffn_w_down", 128)):
        delta[n], new_m[n], new_v[n] = adamw("adamw_" + n, two_d(w[n]), grads[n], two_d(m[n]), two_d(v[n]), tr)
    sm_shapes = [two_d(w[n]).shape for n in SMALL]
    packed = adamw("adamw_small", _pack([two_d(w[n]) for n in SMALL]), _pack([grads[n] for n in SMALL]),
                   _pack([two_d(m[n]) for n in SMALL]), _pack([two_d(v[n]) for n in SMALL]), 1024)
    for res, packed_res in zip((delta, new_m, new_v), packed):
        for n, a in zip(SMALL, _unpack(packed_res, sm_shapes)):
            res[n] = a
    shaped = lambda d: [d[n].reshape(w[n].shape) for n in WEIGHTS]
    return (loss, grad_x[None], *shaped(grads), *shaped(delta), *shaped(new_m), *shaped(new_v))
```

```python
import functools

import jax
import jax.numpy as jnp
import numpy as np
from jax import lax
from jax.experimental import pallas as pl
from jax.experimental.pallas import tpu as pltpu

F32 = jnp.float32
BF16 = jnp.bfloat16

D_MODEL = 2048
EPS = 1e-6
HG_HEADS = 16
HG_DK = 128
HG_CHUNK = 64
HG_SUB = 16
SSM_DINNER = 4096
SSM_HEADDIM = 64
SSM_HEADS = 64
SSM_GROUPS = 8
SSM_DSTATE = 128
SSM_CONV = 4
SSM_CHUNK = 256
SSM_CONV_DIM = 6144
D_FF = 5632
FFN_CONV = 3
DT_PAD = 128

ADAM_LR = 0.001
ADAM_B1 = 0.9
ADAM_B2 = 0.999
ADAM_EPS = 1e-08
ADAM_WD = 0.01
ADAM_STEP = 10

VMEM_LIMIT = 56 * 1024 * 1024
HI = lax.Precision.HIGHEST


def _cp(sem, **kw):
    return pltpu.CompilerParams(dimension_semantics=sem, vmem_limit_bytes=VMEM_LIMIT, **kw)


_DIMS = {"nn": (((1,), (0,)), ((), ())), "nt": (((1,), (1,)), ((), ())), "tn": (((0,), (0,)), ((), ()))}


def mm(a, b, mode, *, name, out_dtype=F32, tm=512, tn=512, tk=None, acc=None, n_major=True,
       dims=None, a_off=(0, 0), b_off=(0, 0)):
    if dims is not None:
        M, N, K = dims
    else:
        if mode == "nn":
            (M, K), (K2, N) = a.shape, b.shape
        elif mode == "nt":
            (M, K), (N, K2) = a.shape, b.shape
        else:
            (K, M), (K2, N) = a.shape, b.shape
        assert K == K2, (a.shape, b.shape, mode)
    tm, tn = min(tm, M), min(tn, N)
    tk = K if tk is None else min(tk, K)
    assert M % tm == 0 and N % tn == 0 and K % tk == 0, (M, N, K, tm, tn, tk)
    a_blk = (tk, tm) if mode == "tn" else (tm, tk)
    b_blk = (tn, tk) if mode == "nt" else (tk, tn)
    assert all(o % s == 0 for o, s in zip(a_off, a_blk)) and all(o % s == 0 for o, s in zip(b_off, b_blk))
    ao0, ao1 = a_off[0] // a_blk[0], a_off[1] // a_blk[1]
    bo0, bo1 = b_off[0] // b_blk[0], b_off[1] // b_blk[1]
    nk = K // tk
    if n_major:
        grid = (N // tn, M // tm, nk)
        ij = lambda p0, p1: (p1, p0)
    else:
        grid = (M // tm, N // tn, nk)
        ij = lambda p0, p1: (p0, p1)

    def a_map(p0, p1, k):
        i, _ = ij(p0, p1)
        return (k + ao0, i + ao1) if mode == "tn" else (i + ao0, k + ao1)

    def b_map(p0, p1, k):
        _, j = ij(p0, p1)
        return (j + bo0, k + bo1) if mode == "nt" else (k + bo0, j + bo1)

    def o_map(p0, p1, k):
        return ij(p0, p1)

    a_spec = pl.BlockSpec(a_blk, a_map)
    b_spec = pl.BlockSpec(b_blk, b_map)
    o_spec = pl.BlockSpec((tm, tn), o_map)
    dims = _DIMS[mode]
    has_acc = acc is not None

    def body(*refs):
        if has_acc:
            a_ref, b_ref, c_ref, o_ref, acc_ref = refs
        else:
            a_ref, b_ref, o_ref, acc_ref = refs
        k = pl.program_id(2)
        part = lax.dot_general(a_ref[...], b_ref[...], dims, preferred_element_type=F32)

        @pl.when(k == 0)
        def _():
            acc_ref[...] = part

        @pl.when(k > 0)
        def _():
            acc_ref[...] += part

        @pl.when(k == nk - 1)
        def _():
            r = acc_ref[...]
            if has_acc:
                r = r + c_ref[...].astype(F32)
            o_ref[...] = r.astype(out_dtype)

    in_specs = [a_spec, b_spec] + ([o_spec] if has_acc else [])
    args = (a, b) + ((acc,) if has_acc else ())
    return pl.pallas_call(
        body, name=name, grid=grid, in_specs=in_specs, out_specs=o_spec,
        out_shape=jax.ShapeDtypeStruct((M, N), out_dtype),
        scratch_shapes=[pltpu.VMEM((tm, tn), F32)],
        compiler_params=_cp(("parallel", "parallel", "arbitrary")),
    )(*args)


def _bdot_plain(a, b, mode):
    return lax.dot_general(a.astype(BF16), b.astype(BF16), _DIMS[mode], preferred_element_type=F32)


@functools.partial(jax.custom_vjp, nondiff_argnums=(2,))
def _bdot_vjp(a, b, mode):
    return _bdot_plain(a, b, mode)


def _bdot_fwd(a, b, mode):
    return _bdot_plain(a, b, mode), (a, b)


def _bdot_bwd(mode, res, g):
    a, b = res
    if mode == "nn":
        return _bdot_plain(g, b, "nt"), _bdot_plain(a, g, "tn")
    if mode == "nt":
        return _bdot_plain(g, b, "nn"), _bdot_plain(g, a, "tn")
    return _bdot_plain(b, g, "nt"), _bdot_plain(a, g, "nn")


_bdot_vjp.defvjp(_bdot_fwd, _bdot_bwd)


def _split3(x):
    x1 = x.astype(BF16)
    r1 = x - x1.astype(F32)
    x2 = r1.astype(BF16)
    return x1, x2, (r1 - x2.astype(F32)).astype(BF16)


def _hdot_impl(a, b, mode, data):
    if data == "a":
        sel = b.astype(BF16)
        parts = [lax.dot_general(p, sel, _DIMS[mode], preferred_element_type=F32) for p in _split3(a)]
    else:
        sel = a.astype(BF16)
        parts = [lax.dot_general(sel, p, _DIMS[mode], preferred_element_type=F32) for p in _split3(b)]
    return (parts[2] + parts[1]) + parts[0]


@functools.partial(jax.custom_vjp, nondiff_argnums=(2, 3))
def _hdot(a, b, mode="nn", data="b"):
    return _hdot_impl(a, b, mode, data)


def _hdot_fwd(a, b, mode, data):
    return _hdot_impl(a, b, mode, data), (a, b)


def _hdot_bwd(mode, data, res, g):
    a, b = res
    if data == "a":
        da = {"nn": lambda: _hdot_impl(g, b, "nt", "a"), "nt": lambda: _hdot_impl(g, b, "nn", "a"),
              "tn": lambda: _hdot_impl(b, g, "nt", "b")}[mode]()
        return da, jnp.zeros_like(b)
    db = {"nn": lambda: _hdot_impl(a, g, "tn", "b"), "nt": lambda: _hdot_impl(g, a, "tn", "a"),
          "tn": lambda: _hdot_impl(a, g, "nn", "b")}[mode]()
    return jnp.zeros_like(a), db


_hdot.defvjp(_hdot_fwd, _hdot_bwd)


def _sigmoid(x):
    return 1.0 / (1.0 + jnp.exp(-x))


def _silu(x):
    return x * _sigmoid(x)


def _iota(shape, dim):
    return lax.broadcasted_iota(jnp.int32, shape, dim)


def _rms(x, w):
    return x * lax.rsqrt(jnp.mean(x * x, axis=-1, keepdims=True) + EPS) * w


def _hg_chunk(q_raw, f_raw, v, g, st, t0, t1, nw, dot):
    c = q_raw.shape[0]
    m = jnp.maximum(t0, t1)
    e0, e1 = jnp.exp(t0 - m), jnp.exp(t1 - m)
    lb = e0 / (e0 + e1)
    f = lb + (1.0 - lb) * _sigmoid(f_raw)
    k = 1.0 - f
    lf = jnp.log(f)
    qh = _silu(q_raw) * (HG_DK ** -0.5)
    row, col = _iota((c, c), 0), _iota((c, c), 1)
    causal = col <= row
    tril = jnp.where(causal, 1.0, 0.0).astype(F32)
    trilb = jnp.where(causal & (col // HG_SUB == row // HG_SUB), 1.0, 0.0).astype(F32)
    b = _hdot(tril, lf)
    bl = _hdot(trilb, lf)
    a_row = b - bl
    rid = _iota((c, HG_DK), 0)
    qt = qh * jnp.exp(bl)
    kt = k * jnp.exp(-bl)
    scores = jnp.zeros((c, c), F32)
    for j in range(c // HG_SUB):
        if j == 0:
            qj = qt * jnp.exp(jnp.minimum(a_row, 0.0))
        else:
            a_j = jnp.sum(jnp.where(rid == j * HG_SUB - 1, b, 0.0), axis=0, keepdims=True)
            qj = qt * jnp.exp(jnp.minimum(a_row - a_j, 0.0))
        kj = jnp.where(rid // HG_SUB == j, kt, 0.0)
        scores = scores + dot(qj, kj, "nt")
    scores = jnp.where(causal, scores, 0.0)
    o = dot(scores, v, "nn") + dot(qh * jnp.exp(b), st, "nt")
    b_last = jnp.sum(jnp.where(rid == c - 1, b, 0.0), axis=0, keepdims=True)
    st_new = st * jnp.exp(b_last) + dot(v, k * jnp.exp(b_last - b), "tn")
    y = _rms(o, nw) * _silu(g)
    return y, st_new


HG_HPS = 4
HG_W = HG_HPS * HG_DK


def hgrn2_fwd(qfig, table, nw, *, step_chunks=2):
    t = qfig.shape[0]
    rows = HG_CHUNK * step_chunks
    nsteps = t // rows
    nh = HG_HEADS // HG_HPS

    def body(q_ref, f_ref, v_ref, g_ref, tab_ref, nw_ref, y_ref, s_ref, st_scr):
        @pl.when(pl.program_id(1) == 0)
        def _():
            st_scr[...] = jnp.zeros_like(st_scr)

        nwv = nw_ref[...]
        for c in range(step_chunks):
            sl = pl.ds(c * HG_CHUNK, HG_CHUNK)
            for hh in range(HG_HPS):
                ln = pl.ds(hh * HG_DK, HG_DK)
                st = st_scr[hh]
                s_ref[hh, c] = st
                y, st_new = _hg_chunk(q_ref[sl, ln], f_ref[sl, ln], v_ref[sl, ln], g_ref[sl, ln], st,
                                      tab_ref[0:1, ln], tab_ref[1:2, ln], nwv, _bdot_vjp)
                y_ref[sl, ln] = y.astype(BF16)
                st_scr[hh] = st_new

    blk = lambda off: pl.BlockSpec((rows, HG_W), lambda h, c, off=off: (c, off + h))
    return pl.pallas_call(
        body, name="hgrn2_fwd", grid=(nh, nsteps),
        in_specs=[blk(0), blk(nh), blk(2 * nh), blk(3 * nh),
                  pl.BlockSpec((2, HG_W), lambda h, c: (0, h)), pl.BlockSpec((1, HG_DK), lambda h, c: (0, 0))],
        out_specs=[pl.BlockSpec((rows, HG_W), lambda h, c: (c, h)),
                   pl.BlockSpec((HG_HPS, step_chunks, HG_DK, HG_DK), lambda h, c: (h, c, 0, 0))],
        out_shape=[jax.ShapeDtypeStruct((t, HG_HEADS * HG_DK), BF16),
                   jax.ShapeDtypeStruct((HG_HEADS, t // HG_CHUNK, HG_DK, HG_DK), F32)],
        scratch_shapes=[pltpu.VMEM((HG_HPS, HG_DK, HG_DK), F32)],
        compiler_params=_cp(("parallel", "arbitrary")),
    )(qfig, qfig, qfig, qfig, table, nw)


def hgrn2_bwd(qfig, table, nw, states, dy, *, step_chunks=2):
    t = qfig.shape[0]
    rows = HG_CHUNK * step_chunks
    nsteps = t // rows
    nh = HG_HEADS // HG_HPS

    def body(q_ref, f_ref, v_ref, g_ref, tab_ref, nw_ref, s_ref, dy_ref,
             dq_ref, df_ref, dv_ref, dg_ref, dtab_ref, dnw_ref, dst_scr):
        @pl.when(pl.program_id(1) == 0)
        def _():
            dst_scr[...] = jnp.zeros_like(dst_scr)
            dtab_ref[...] = jnp.zeros_like(dtab_ref)
            dnw_ref[...] = jnp.zeros_like(dnw_ref)

        nwv = nw_ref[...]
        fn = functools.partial(_hg_chunk, dot=_bdot_vjp)
        for c in reversed(range(step_chunks)):
            sl = pl.ds(c * HG_CHUNK, HG_CHUNK)
            for hh in range(HG_HPS):
                ln = pl.ds(hh * HG_DK, HG_DK)
                _, vjp = jax.vjp(fn, q_ref[sl, ln], f_ref[sl, ln], v_ref[sl, ln], g_ref[sl, ln], s_ref[hh, c],
                                 tab_ref[0:1, ln], tab_ref[1:2, ln], nwv)
                dq, df, dv, dg, dst, dt0, dt1, dnw = vjp((dy_ref[sl, ln].astype(F32), dst_scr[hh]))
                dq_ref[sl, ln] = dq.astype(BF16)
                df_ref[sl, ln] = df.astype(BF16)
                dv_ref[sl, ln] = dv.astype(BF16)
                dg_ref[sl, ln] = dg.astype(BF16)
                dst_scr[hh] = dst
                dtab_ref[0:1, ln] += dt0
                dtab_ref[1:2, ln] += dt1
                dnw_ref[hh] += dnw

    rev = lambda c: nsteps - 1 - c
    blk = lambda off: pl.BlockSpec((rows, HG_W), lambda h, c, off=off: (rev(c), off + h))
    oblk = lambda: pl.BlockSpec((rows, HG_W), lambda h, c: (rev(c), h))
    d = HG_HEADS * HG_DK
    outs = pl.pallas_call(
        body, name="hgrn2_bwd", grid=(nh, nsteps),
        in_specs=[blk(0), blk(nh), blk(2 * nh), blk(3 * nh),
                  pl.BlockSpec((2, HG_W), lambda h, c: (0, h)), pl.BlockSpec((1, HG_DK), lambda h, c: (0, 0)),
                  pl.BlockSpec((HG_HPS, step_chunks, HG_DK, HG_DK), lambda h, c: (h, rev(c), 0, 0)),
                  pl.BlockSpec((rows, HG_W), lambda h, c: (rev(c), h))],
        out_specs=[oblk(), oblk(), oblk(), oblk(),
                   pl.BlockSpec((2, HG_W), lambda h, c: (0, h)),
                   pl.BlockSpec((HG_HPS, 1, HG_DK), lambda h, c: (h, 0, 0))],
        out_shape=[jax.ShapeDtypeStruct((t, d), BF16)] * 4
        + [jax.ShapeDtypeStruct((2, d), F32), jax.ShapeDtypeStruct((HG_HEADS, 1, HG_DK), F32)],
        scratch_shapes=[pltpu.VMEM((HG_HPS, HG_DK, HG_DK), F32)],
        compiler_params=_cp(("parallel", "arbitrary")),
    )(qfig, qfig, qfig, qfig, table, nw, states, dy)
    return outs


def _ssd_chunk(xs2, dt, acum, bm, cm, s2, pair, dot):
    c = xs2.shape[0]
    lane = _iota((DT_PAD, 128), 1)
    expand = jnp.where(_iota((DT_PAD, 128), 0) == 2 * pair + lane // SSM_HEADDIM, 1.0, 0.0).astype(F32)
    sel = jnp.where(_iota((8, DT_PAD), 1) == 2 * pair + _iota((8, DT_PAD), 0), 1.0, 0.0).astype(F32)
    sel = jnp.where(_iota((8, DT_PAD), 0) < 2, sel, 0.0)
    dtx = _hdot(dt, expand, "nn", "a")
    acol = _hdot(acum, expand, "nn", "a")
    arow8 = _hdot(sel, acum, "nt", "b")
    row, col = _iota((c, c), 0), _iota((c, c), 1)
    causal = col <= row
    cb = dot(cm, bm, "nt")
    x2 = xs2 * dtx
    lane_c = _iota((c, 128), 1)
    y = dot(cm, s2, "nn") * jnp.exp(acol)
    for r in range(2):
        head = (lane_c // SSM_HEADDIM) == r
        a_c = jnp.sum(jnp.where(head & (lane_c % SSM_HEADDIM == 0), acol, 0.0), axis=1, keepdims=True)
        a_r = jnp.sum(jnp.where(_iota((8, c), 0) == r, arow8, 0.0), axis=0, keepdims=True)
        decay = jnp.where(causal, jnp.exp(jnp.minimum(a_c - a_r, 0.0)), 0.0)
        y = y + dot(cb * decay, jnp.where(head, x2, 0.0), "nn")
    a_last = jnp.sum(jnp.where(_iota((c, 128), 0) == c - 1, acol, 0.0), axis=0, keepdims=True)
    s2_new = s2 * jnp.exp(a_last) + dot(bm, x2 * jnp.exp(a_last - acol), "tn")
    return y, s2_new


SSM_PAIRS = SSM_HEADS // 2
PAIRS_PER_GROUP = SSM_PAIRS // SSM_GROUPS
SSD_PPS = 4
SSD_W = SSD_PPS * 128
_XS_BLOCKS = SSM_DINNER // 128
_B_BLOCK0 = _XS_BLOCKS
_C_BLOCK0 = _XS_BLOCKS + SSM_GROUPS


def ssd_fwd(xbc_act, dt, acum):
    t = xbc_act.shape[0]
    nc = t // SSM_CHUNK
    c_ = SSM_CHUNK

    def body(xs_ref, b_ref, c_ref, dt_ref, ac_ref, y_ref, s_ref, s_scr):
        q = pl.program_id(1)
        for r in range(SSD_PPS):
            p = SSD_PPS * q + r
            ln = pl.ds(r * 128, 128)

            @pl.when(pl.program_id(0) == 0)
            def _():
                s_scr[p] = jnp.zeros((SSM_DSTATE, 128), F32)

            s2 = s_scr[p]
            s_ref[r] = s2
            y, s2_new = _ssd_chunk(xs_ref[:, ln], dt_ref[...], ac_ref[...], b_ref[...], c_ref[...], s2, p, _bdot_vjp)
            y_ref[:, ln] = y
            s_scr[p] = s2_new

    grp = lambda q: q // (PAIRS_PER_GROUP // SSD_PPS)
    return pl.pallas_call(
        body, name="ssd_fwd", grid=(nc, SSM_PAIRS // SSD_PPS),
        in_specs=[pl.BlockSpec((c_, SSD_W), lambda c, q: (c, q)),
                  pl.BlockSpec((c_, 128), lambda c, q: (c, _B_BLOCK0 + grp(q))),
                  pl.BlockSpec((c_, 128), lambda c, q: (c, _C_BLOCK0 + grp(q))),
                  pl.BlockSpec((c_, DT_PAD), lambda c, q: (c, 0)),
                  pl.BlockSpec((c_, DT_PAD), lambda c, q: (c, 0))],
        out_specs=[pl.BlockSpec((c_, SSD_W), lambda c, q: (c, q)),
                   pl.BlockSpec((None, SSD_PPS, SSM_DSTATE, 128), lambda c, q: (c, q, 0, 0))],
        out_shape=[jax.ShapeDtypeStruct((t, SSM_DINNER), F32),
                   jax.ShapeDtypeStruct((nc, SSM_PAIRS, SSM_DSTATE, 128), F32)],
        scratch_shapes=[pltpu.VMEM((SSM_PAIRS, SSM_DSTATE, 128), F32)],
        compiler_params=_cp(("arbitrary", "arbitrary")),
    )(xbc_act, xbc_act, xbc_act, dt, acum)


def ssd_bwd(xbc_act, dt, acum, states, dy, dskip):
    t = xbc_act.shape[0]
    nc = t // SSM_CHUNK
    c_ = SSM_CHUNK
    rev = lambda c: nc - 1 - c

    def body(xs_ref, b_ref, c_ref, dt_ref, ac_ref, s_ref, dy_ref, sk_ref,
             dxs_ref, db_ref, dc_ref, ddt_ref, dac_ref, ds_scr):
        q = pl.program_id(1)
        steps_per_group = PAIRS_PER_GROUP // SSD_PPS
        db = dc = ddt = dac = None
        for r in range(SSD_PPS):
            p = SSD_PPS * q + r
            ln = pl.ds(r * 128, 128)

            @pl.when(pl.program_id(0) == 0)
            def _():
                ds_scr[p] = jnp.zeros((SSM_DSTATE, 128), F32)

            fn = functools.partial(_ssd_chunk, pair=p, dot=_bdot_vjp)
            _, vjp = jax.vjp(fn, xs_ref[:, ln], dt_ref[...], ac_ref[...], b_ref[...], c_ref[...], s_ref[r])
            dxs, ddt_r, dac_r, db_r, dc_r, ds = vjp((dy_ref[:, ln], ds_scr[p]))
            dxs_ref[:, ln] = dxs + sk_ref[:, ln]
            ds_scr[p] = ds
            db, dc = (db_r, dc_r) if r == 0 else (db + db_r, dc + dc_r)
            ddt, dac = (ddt_r, dac_r) if r == 0 else (ddt + ddt_r, dac + dac_r)

        @pl.when(q % steps_per_group == 0)
        def _():
            db_ref[...] = db
            dc_ref[...] = dc

        @pl.when(q % steps_per_group != 0)
        def _():
            db_ref[...] += db
            dc_ref[...] += dc

        @pl.when(q == 0)
        def _():
            ddt_ref[...] = ddt
            dac_ref[...] = dac

        @pl.when(q != 0)
        def _():
            ddt_ref[...] += ddt
            dac_ref[...] += dac

    grp = lambda q: q // (PAIRS_PER_GROUP // SSD_PPS)
    return pl.pallas_call(
        body, name="ssd_bwd", grid=(nc, SSM_PAIRS // SSD_PPS),
        in_specs=[pl.BlockSpec((c_, SSD_W), lambda c, q: (rev(c), q)),
                  pl.BlockSpec((c_, 128), lambda c, q: (rev(c), _B_BLOCK0 + grp(q))),
                  pl.BlockSpec((c_, 128), lambda c, q: (rev(c), _C_BLOCK0 + grp(q))),
                  pl.BlockSpec((c_, DT_PAD), lambda c, q: (rev(c), 0)),
                  pl.BlockSpec((c_, DT_PAD), lambda c, q: (rev(c), 0)),
                  pl.BlockSpec((None, SSD_PPS, SSM_DSTATE, 128), lambda c, q: (rev(c), q, 0, 0)),
                  pl.BlockSpec((c_, SSD_W), lambda c, q: (rev(c), q)),
                  pl.BlockSpec((c_, SSD_W), lambda c, q: (rev(c), q))],
        out_specs=[pl.BlockSpec((c_, SSD_W), lambda c, q: (rev(c), q)),
                   pl.BlockSpec((c_, 128), lambda c, q: (rev(c), grp(q))),
                   pl.BlockSpec((c_, 128), lambda c, q: (rev(c), grp(q))),
                   pl.BlockSpec((c_, DT_PAD), lambda c, q: (rev(c), 0)),
                   pl.BlockSpec((c_, DT_PAD), lambda c, q: (rev(c), 0))],
        out_shape=[jax.ShapeDtypeStruct((t, SSM_DINNER), F32),
                   jax.ShapeDtypeStruct((t, SSM_GROUPS * SSM_DSTATE), F32),
                   jax.ShapeDtypeStruct((t, SSM_GROUPS * SSM_DSTATE), F32),
                   jax.ShapeDtypeStruct((t, DT_PAD), F32),
                   jax.ShapeDtypeStruct((t, DT_PAD), F32)],
        scratch_shapes=[pltpu.VMEM((SSM_PAIRS, SSM_DSTATE, 128), F32)],
        compiler_params=_cp(("arbitrary", "arbitrary")),
    )(xbc_act, xbc_act, xbc_act, dt, acum, states, dy, dskip)


def rowwise(name, fn, row_ins, par_ins, row_outs, acc_outs, *, tt, ncb=1):
    t = row_ins[0][0].shape[0]
    assert t % tt == 0
    n_ri, n_pi, n_ro, n_ao = len(row_ins), len(par_ins), len(row_outs), len(acc_outs)

    def body(*refs):
        i = pl.program_id(1)
        ins = [r[...] for r in refs[:n_ri + n_pi]]
        outs = fn(*ins)
        ro_refs = refs[n_ri + n_pi:n_ri + n_pi + n_ro]
        ao_refs = refs[n_ri + n_pi + n_ro:]
        for r, v in zip(ro_refs, outs[:n_ro]):
            r[...] = v.astype(r.dtype)
        for r, v in zip(ao_refs, outs[n_ro:]):
            @pl.when(i == 0)
            def _(r=r, v=v):
                r[...] = v

            @pl.when(i > 0)
            def _(r=r, v=v):
                r[...] += v

    in_specs = [pl.BlockSpec((tt, bc), lambda j, i, off=off: (i, off + j)) for _, bc, off in row_ins]
    in_specs += [pl.BlockSpec((a.shape[0], bc), lambda j, i, off=off: (0, off + j)) for a, bc, off in par_ins]
    out_specs = [pl.BlockSpec((tt, bc), lambda j, i: (i, j)) for _, bc, _ in row_outs]
    out_specs += [pl.BlockSpec((r, bc), lambda j, i: (0, j)) for r, _, bc in acc_outs]
    out_shape = [jax.ShapeDtypeStruct((t, c), dt) for c, _, dt in row_outs]
    out_shape += [jax.ShapeDtypeStruct((r, c), F32) for r, c, _ in acc_outs]
    return pl.pallas_call(
        body, name=name, grid=(ncb, t // tt), in_specs=in_specs, out_specs=out_specs, out_shape=out_shape,
        compiler_params=_cp(("parallel", "arbitrary")),
    )(*[a for a, _, _ in row_ins], *[a for a, _, _ in par_ins])


def _colsum(v):
    return jnp.sum(v, axis=0, keepdims=True)


def _softplus(x):
    return jnp.maximum(x, 0.0) + jnp.log(1.0 + jnp.exp(-jnp.abs(x)))


def _gelu_tanh(x):
    return 0.5 * x * (1.0 + jnp.tanh(0.7978845608028654 * (x + 0.044715 * (x * x * x))))


D = D_MODEL


def norm_fwd(x, w):
    return rowwise("norm_fwd", lambda xv, wv: (_rms(xv, wv),), [(x, D, 0)], [(w, D, 0)], [(D, D, BF16)], [], tt=256)[0]


def norm_bwd(x, w, dh, dres):
    def fn(xv, dhv, drv, wv):
        _, vjp = jax.vjp(_rms, xv, wv)
        dx, dw = vjp(dhv)
        return dx + drv, dw
    return rowwise("norm_bwd", fn, [(x, D, 0), (dh, D, 0), (dres, D, 0)], [(w, D, 0)], [(D, D, F32)], [(1, D, D)], tt=256)


def _dt_fn(dtr, bias, a_log):
    c = dtr.shape[0]
    dt = _softplus(dtr + bias)
    da = dt * (-jnp.exp(a_log))
    tril = jnp.where(_iota((c, c), 1) <= _iota((c, c), 0), 1.0, 0.0).astype(F32)
    return dt, _hdot(tril, da)


def dt_fwd(dtr, bias, a_log):
    return rowwise("dt_fwd", _dt_fn, [(dtr, DT_PAD, 0)], [(bias, DT_PAD, 0), (a_log, DT_PAD, 0)],
                   [(DT_PAD, DT_PAD, F32), (DT_PAD, DT_PAD, F32)], [], tt=SSM_CHUNK)


def dt_bwd(dtr, bias, a_log, ddt, dacum):
    def fn(dtrv, ddtv, dacv, bv, av):
        _, vjp = jax.vjp(_dt_fn, dtrv, bv, av)
        return vjp((ddtv, dacv))
    return rowwise("dt_bwd", fn, [(dtr, DT_PAD, 0), (ddt, DT_PAD, 0), (dacum, DT_PAD, 0)],
                   [(bias, DT_PAD, 0), (a_log, DT_PAD, 0)],
                   [(DT_PAD, DT_PAD, BF16)], [(1, DT_PAD, DT_PAD), (1, DT_PAD, DT_PAD)], tt=SSM_CHUNK)


GROUP_W = SSM_DINNER // SSM_GROUPS


def _ssm_post_fn(yv, xsv, zv, dexp, nw):
    return _rms((yv + dexp * xsv) * _silu(zv), nw)


def ssm_post_fwd(yssd, xbc_act, z, dexp, nw):
    return rowwise("ssm_post_fwd", lambda *a: (_ssm_post_fn(*a),),
                   [(yssd, GROUP_W, 0), (xbc_act, GROUP_W, 0), (z, GROUP_W, 0)], [(dexp, GROUP_W, 0), (nw, GROUP_W, 0)],
                   [(SSM_DINNER, GROUP_W, BF16)], [], tt=512, ncb=SSM_GROUPS)[0]


def ssm_post_bwd(yssd, xbc_act, z, dexp, nw, dy):
    def fn(yv, xsv, zv, dyv, dv, nv):
        _, vjp = jax.vjp(_ssm_post_fn, yv, xsv, zv, dv, nv)
        return vjp(dyv)
    return rowwise("ssm_post_bwd", fn,
                   [(yssd, GROUP_W, 0), (xbc_act, GROUP_W, 0), (z, GROUP_W, 0), (dy, GROUP_W, 0)],
                   [(dexp, GROUP_W, 0), (nw, GROUP_W, 0)],
                   [(SSM_DINNER, GROUP_W, F32), (SSM_DINNER, GROUP_W, F32), (SSM_DINNER, GROUP_W, BF16)],
                   [(1, SSM_DINNER, GROUP_W), (1, SSM_DINNER, GROUP_W)], tt=512, ncb=SSM_GROUPS)


def _merge_fn(ah, asm, gh, gs):
    return _sigmoid(gh) * ah + _sigmoid(gs) * asm


def merge_fwd(a_hg, a_ssm, gates):
    return rowwise("merge_fwd", lambda *a: (_merge_fn(*a),), [(a_hg, D, 0), (a_ssm, D, 0), (gates, D, 0), (gates, D, 1)], [],
                   [(D, D, BF16)], [], tt=256)[0]


def merge_bwd(a_hg, a_ssm, gates, dmixed):
    def fn(ah, asm, gh, gs, dm):
        _, vjp = jax.vjp(_merge_fn, ah, asm, gh, gs)
        return vjp(dm)
    return rowwise("merge_bwd", fn, [(a_hg, D, 0), (a_ssm, D, 0), (gates, D, 0), (gates, D, 1), (dmixed, D, 0)], [],
                   [(D, D, BF16)] * 4, [], tt=256)


def _post1_fn(xv, uv, wpost, wpre):
    x1 = xv + _rms(uv, wpost)
    return x1, _rms(x1, wpre)


def post1_fwd(x, u, wpost, wpre):
    return rowwise("post1_fwd", _post1_fn, [(x, D, 0), (u, D, 0)], [(wpost, D, 0), (wpre, D, 0)],
                   [(D, D, F32), (D, D, BF16)], [], tt=256)


def post1_bwd(x, u, wpost, wpre, dx1, dh2):
    def fn(xv, uv, d1, d2, wa, wb):
        _, vjp = jax.vjp(_post1_fn, xv, uv, wa, wb)
        dx, du, dwa, dwb = vjp((d1, d2))
        return du, dx, dwa, dwb
    return rowwise("post1_bwd", fn, [(x, D, 0), (u, D, 0), (dx1, D, 0), (dh2, D, 0)], [(wpost, D, 0), (wpre, D, 0)],
                   [(D, D, BF16), (D, D, F32)], [(1, D, D), (1, D, D)], tt=256)


def final_fwd_bwd(x1, fo, w, target):
    def fn(x1v, fov, tv, wv):
        def loss_fn(a, b, c):
            err = a + _rms(b, c) - tv
            return 0.5 * jnp.sum(err * err) * (1.0 / D)
        loss, vjp = jax.vjp(loss_fn, x1v, fov, wv)
        dx, dfo, dw = vjp(jnp.ones((), F32))
        return dx, dfo, dw, jnp.full((1, 128), loss, F32)
    return rowwise("final_fwd_bwd", fn, [(x1, D, 0), (fo, D, 0), (target, D, 0)], [(w, D, 0)],
                   [(D, D, F32), (D, D, BF16)], [(1, D, D), (1, 128, 128)], tt=256)


HALO = 8
CONV_TT = 512
CONV_CB = 512


def _tail(kind, c, up):
    return _silu(c) if kind == "silu" else _gelu_tanh(c) * up


def conv_fwd(name, x, xoff, w, b, kind, up=None, upoff=0, act_dtype=F32):
    t = x.shape[0]
    k_, c_ = w.shape
    tt, cb = CONV_TT, CONV_CB
    hb = tt // HALO
    has_up = up is not None

    def body(*refs):
        if has_up:
            x_ref, xp_ref, w_ref, b_ref, up_ref, c_ref, a_ref, scr = refs
        else:
            x_ref, xp_ref, w_ref, b_ref, c_ref, a_ref, scr = refs
        i = pl.program_id(1)
        scr[0:HALO, :] = jnp.where(i == 0, 0.0, xp_ref[...])
        scr[HALO:HALO + tt, :] = x_ref[...]
        acc = jnp.zeros((tt, cb), F32) + b_ref[...]
        for k in range(k_):
            acc = acc + w_ref[k:k + 1, :] * scr[pl.ds(HALO - (k_ - 1) + k, tt), :]
        c_ref[...] = acc
        a_ref[...] = _tail(kind, acc, up_ref[...] if has_up else None).astype(act_dtype)

    in_specs = [pl.BlockSpec((tt, cb), lambda j, i: (i, xoff + j)),
                pl.BlockSpec((HALO, cb), lambda j, i: (jnp.maximum(i * hb - 1, 0), xoff + j)),
                pl.BlockSpec((k_, cb), lambda j, i: (0, j)),
                pl.BlockSpec((1, cb), lambda j, i: (0, j))]
    args = [x, x, w, b]
    if has_up:
        in_specs.append(pl.BlockSpec((tt, cb), lambda j, i: (i, upoff + j)))
        args.append(up)
    return pl.pallas_call(
        body, name=name, grid=(c_ // cb, t // tt), in_specs=in_specs,
        out_specs=[pl.BlockSpec((tt, cb), lambda j, i: (i, j))] * 2,
        out_shape=[jax.ShapeDtypeStruct((t, c_), F32), jax.ShapeDtypeStruct((t, c_), act_dtype)],
        scratch_shapes=[pltpu.VMEM((tt + HALO, cb), F32)],
        compiler_params=_cp(("parallel", "arbitrary")),
    )(*args)


def conv_bwd(name, x, xoff, c, coff, dact, w, kind, up=None, upoff=0):
    t = x.shape[0]
    k_, c_ = w.shape[0], dact.shape[1]
    tt, cb = CONV_TT, CONV_CB
    hb = tt // HALO
    nt = t // tt
    has_up = up is not None

    def tail_grad(cv, dav, upv):
        if has_up:
            _, vjp = jax.vjp(lambda a, u: _tail(kind, a, u), cv, upv)
            return vjp(dav)
        _, vjp = jax.vjp(lambda a: _tail(kind, a, None), cv)
        return vjp(dav)[0], None

    def body(*refs):
        if has_up:
            (x_ref, xp_ref, c_ref, cn_ref, da_ref, dan_ref, w_ref, up_ref, upn_ref,
             dx_ref, dup_ref, dw_ref, db_ref, xs, dcs) = refs
        else:
            x_ref, xp_ref, c_ref, cn_ref, da_ref, dan_ref, w_ref, dx_ref, dw_ref, db_ref, xs, dcs = refs
        i = pl.program_id(1)
        xs[0:HALO, :] = jnp.where(i == 0, 0.0, xp_ref[...])
        xs[HALO:HALO + tt, :] = x_ref[...]
        dc, dup = tail_grad(c_ref[...], da_ref[...].astype(F32), up_ref[...] if has_up else None)
        dcn, _ = tail_grad(cn_ref[...], dan_ref[...].astype(F32), upn_ref[...] if has_up else None)
        dcs[0:tt, :] = dc
        dcs[tt:tt + HALO, :] = jnp.where(i == nt - 1, 0.0, dcn)
        if has_up:
            dup_ref[...] = dup.astype(BF16)
        dx = jnp.zeros((tt, cb), F32)
        dws = []
        for k in range(k_):
            dx = dx + w_ref[k:k + 1, :] * dcs[pl.ds(k_ - 1 - k, tt), :]
            dws.append(_colsum(dc * xs[pl.ds(HALO - (k_ - 1) + k, tt), :]))
        dx_ref[...] = dx.astype(BF16)

        @pl.when(i == 0)
        def _():
            dw_ref[...] = jnp.zeros_like(dw_ref)
            db_ref[...] = jnp.zeros_like(db_ref)

        for k in range(k_):
            dw_ref[k:k + 1, :] += dws[k]
        db_ref[...] += _colsum(dc)

    tile = lambda off: pl.BlockSpec((tt, cb), lambda j, i, off=off: (i, off + j))
    prev = lambda off: pl.BlockSpec((HALO, cb), lambda j, i, off=off: (jnp.maximum(i * hb - 1, 0), off + j))
    nxt = lambda off: pl.BlockSpec((HALO, cb), lambda j, i, off=off: (jnp.minimum((i + 1) * hb, t // HALO - 1), off + j))
    in_specs = [tile(xoff), prev(xoff), tile(coff), nxt(coff), tile(0), nxt(0),
                pl.BlockSpec((k_, cb), lambda j, i: (0, coff + j))]
    args = [x, x, c, c, dact, dact, w]
    if has_up:
        in_specs += [tile(upoff), nxt(upoff)]
        args += [up, up]
    out_specs = [tile(0)] + ([tile(0)] if has_up else []) + [pl.BlockSpec((HALO, cb), lambda j, i: (0, j)),
                                                            pl.BlockSpec((1, cb), lambda j, i: (0, j))]
    out_shape = [jax.ShapeDtypeStruct((t, c_), BF16)] * (2 if has_up else 1)
    out_shape += [jax.ShapeDtypeStruct((HALO, c_), F32), jax.ShapeDtypeStruct((1, c_), F32)]
    return pl.pallas_call(
        body, name=name, grid=(c_ // cb, nt), in_specs=in_specs, out_specs=out_specs, out_shape=out_shape,
        scratch_shapes=[pltpu.VMEM((tt + HALO, cb), F32), pltpu.VMEM((tt + HALO, cb), F32)],
        compiler_params=_cp(("parallel", "arbitrary")),
    )(*args)


def ew_sum(name, parts, rows, out_dtype, tr):
    c = parts[0][0].shape[1]
    tr = min(tr, rows)
    assert rows % tr == 0 and all(off % tr == 0 for _, off in parts)
    n = len(parts)

    def body(*refs):
        acc = refs[0][...].astype(F32)
        for ref in refs[1:n]:
            acc = acc + ref[...].astype(F32)
        refs[n][...] = acc.astype(out_dtype)

    in_specs = [pl.BlockSpec((tr, c), lambda i, o=off // tr: (i + o, 0)) for _, off in parts]
    return pl.pallas_call(body, name=name, grid=(rows // tr,), in_specs=in_specs,
                          out_specs=pl.BlockSpec((tr, c), lambda i: (i, 0)),
                          out_shape=jax.ShapeDtypeStruct((rows, c), out_dtype),
                          compiler_params=_cp(("parallel",)))(*[a for a, _ in parts])


def fold_heads(dexp):
    def body(d_ref, o_ref):
        sel = jnp.where(_iota((SSM_DINNER, DT_PAD), 0) // SSM_HEADDIM == _iota((SSM_DINNER, DT_PAD), 1), 1.0, 0.0)
        o_ref[...] = _hdot(jnp.broadcast_to(d_ref[...], (8, SSM_DINNER)), sel.astype(F32), "nn", "a")[0:1, :]

    return pl.pallas_call(body, name="fold_heads", out_shape=jax.ShapeDtypeStruct((1, DT_PAD), F32),
                          compiler_params=pltpu.CompilerParams(vmem_limit_bytes=VMEM_LIMIT))(dexp)


def adamw(name, w, g, m, v, tr):
    r, c = w.shape
    tr = min(tr, r)
    assert r % tr == 0, (r, tr)

    def body(w_ref, g_ref, m_ref, v_ref, d_ref, nm_ref, nv_ref):
        gv = g_ref[...]
        nm = ADAM_B1 * m_ref[...] + (1.0 - ADAM_B1) * gv
        nv = ADAM_B2 * v_ref[...] + (1.0 - ADAM_B2) * (gv * gv)
        m_hat = nm / (1.0 - ADAM_B1 ** ADAM_STEP)
        v_hat = nv / (1.0 - ADAM_B2 ** ADAM_STEP)
        d_ref[...] = -ADAM_LR * (m_hat / (jnp.sqrt(v_hat) + ADAM_EPS) + ADAM_WD * w_ref[...])
        nm_ref[...] = nm
        nv_ref[...] = nv

    spec = pl.BlockSpec((tr, c), lambda i: (i, 0))
    shp = jax.ShapeDtypeStruct((r, c), F32)
    return pl.pallas_call(body, name=name, grid=(r // tr,), in_specs=[spec] * 4, out_specs=[spec] * 3,
                          out_shape=[shp] * 3, compiler_params=_cp(("parallel",)))(w, g, m, v)


SEG_QFIG, SEG_Z, SEG_XBC, SEG_DT, SEG_G = 0, 8192, 12288, 18432, 18496
IN_TOTAL = 22592
FFN_BLOCKS = D_FF // CONV_CB


def local_step(x, target, wts, par):
    t = x.shape[0]
    pad64 = lambda a: jnp.pad(a, ((0, 0), (0, DT_PAD - a.shape[1])))
    bias, a_log = pad64(par["ssm_dt_bias"]), pad64(par["ssm_A_log"])
    dexp = jnp.repeat(par["ssm_D"], SSM_HEADDIM, axis=1)
    in_t = wts["in_t"]

    h = norm_fwd(x, par["mix_pre_norm"])
    proj = lambda nm, off, n, tn: mm(h, in_t, "nt", name=nm, tn=tn, dims=(t, n, D), b_off=(off, 0))
    qfig = proj("proj_qfig", SEG_QFIG, 8192, 1024)
    z = proj("proj_z", SEG_Z, 4096, 1024)
    xbc = proj("proj_xbc", SEG_XBC, 6144, 1024)
    dtr = mm(h, wts["dt_t"], "nt", name="proj_dt", tn=128)
    gates = mm(h, wts["g_t"], "nt", name="proj_gates", tn=1024)
    y_hg, hg_states = hgrn2_fwd(qfig, par["hg_lb_table"], par["hg_out_norm"])
    c_ssm, xbc_act = conv_fwd("ssm_conv_fwd", xbc, 0, par["ssm_conv_w"], par["ssm_conv_b"], "silu")
    dt, acum = dt_fwd(dtr, bias, a_log)
    yssd, ssd_states = ssd_fwd(xbc_act, dt, acum)
    y_ssm = ssm_post_fwd(yssd, xbc_act, z, dexp, par["ssm_out_norm"])
    a_hg = mm(y_hg, wts["bh"], "nn", name="branch_hg", tn=1024)
    a_ssm = mm(y_ssm, wts["bs"], "nn", name="branch_ssm", tn=1024, tk=2048)
    mixed = merge_fwd(a_hg, a_ssm, gates)
    u = mm(mixed, wts["o"], "nn", name="out_proj", tn=1024)
    x1, h2 = post1_fwd(x, u, par["mix_post_norm"], par["ffn_pre_norm"])
    gu = mm(h2, wts["up"], "nn", name="ffn_up", tn=1024)
    c_ffn, act = conv_fwd("ffn_conv_fwd", gu, 0, par["ffn_conv_w"], par["ffn_conv_b"], "gelu_mul",
                          up=gu, upoff=FFN_BLOCKS, act_dtype=BF16)
    fo = mm(act, wts["dn"], "nn", name="ffn_down", tn=1024, tk=1408)
    dx2, dfo, g_ffn_post, loss = final_fwd_bwd(x1, fo, par["ffn_post_norm"], target)

    dact = mm(dfo, wts["dn"], "nt", name="d_act", tn=1408)
    g_dn = mm(act, dfo, "tn", name="g_ffn_down", out_dtype=BF16, tm=1408, tn=2048, tk=512)
    dgate, dup, g_fcw, g_fcb = conv_bwd("ffn_conv_bwd", gu, 0, c_ffn, 0, dact, par["ffn_conv_w"], "gelu_mul",
                                        up=gu, upoff=FFN_BLOCKS)
    dh2 = mm(dgate, wts["up"], "nt", name="d_h2_gate", tn=1024, tk=1408, dims=(t, D, D_FF))
    dh2 = mm(dup, wts["up"], "nt", name="d_h2_up", tn=1024, tk=1408, dims=(t, D, D_FF), b_off=(0, D_FF), acc=dh2)
    g_up_gate = mm(h2, dgate, "tn", name="g_ffn_up_gate", out_dtype=BF16, tm=2048, tn=1408, tk=512)
    g_up_up = mm(h2, dup, "tn", name="g_ffn_up_up", out_dtype=BF16, tm=2048, tn=1408, tk=512)
    du, dx1, g_mix_post, g_ffn_pre = post1_bwd(x, u, par["mix_post_norm"], par["ffn_pre_norm"], dx2, dh2)
    dmixed = mm(du, wts["o"], "nt", name="d_mixed", tn=1024)
    g_o = mm(mixed, du, "tn", name="g_w_out", out_dtype=BF16, tm=1024, tn=2048, tk=512)
    da_hg, da_ssm, dg_hg, dg_ssm = merge_bwd(a_hg, a_ssm, gates, dmixed)
    dy_hg = mm(da_hg, wts["bh"], "nt", name="d_y_hg", out_dtype=BF16, tn=1024)
    g_bh = mm(y_hg, da_hg, "tn", name="g_w_branch_hg", out_dtype=BF16, tm=1024, tn=2048, tk=512)
    dy_ssm = mm(da_ssm, wts["bs"], "nt", name="d_y_ssm", tn=1024)
    g_bs = mm(y_ssm, da_ssm, "tn", name="g_w_branch_ssm", out_dtype=BF16, tm=1024, tn=2048, tk=512)
    dyssd, dskip, dz, g_dexp, g_ssm_norm = ssm_post_bwd(yssd, xbc_act, z, dexp, par["ssm_out_norm"], dy_ssm)
    dxs, db_, dc_, ddt, dacum = ssd_bwd(xbc_act, dt, acum, ssd_states, dyssd, dskip)
    ddtr, g_dt_bias, g_a_log = dt_bwd(dtr, bias, a_log, ddt, dacum)
    xs_blocks, bc_blocks = SSM_DINNER // CONV_CB, SSM_GROUPS * SSM_DSTATE // CONV_CB
    dxbc_x, g_cw_x, g_cb_x = conv_bwd("ssm_conv_bwd_x", xbc, 0, c_ssm, 0, dxs, par["ssm_conv_w"], "silu")
    dxbc_b, g_cw_b, g_cb_b = conv_bwd("ssm_conv_bwd_b", xbc, xs_blocks, c_ssm, xs_blocks, db_, par["ssm_conv_w"], "silu")
    dxbc_c, g_cw_c, g_cb_c = conv_bwd("ssm_conv_bwd_c", xbc, xs_blocks + bc_blocks, c_ssm, xs_blocks + bc_blocks, dc_,
                                      par["ssm_conv_w"], "silu")
    dq, df, dv, dg, g_table, g_hg_norm = hgrn2_bwd(qfig, par["hg_lb_table"], par["hg_out_norm"], hg_states, dy_hg)

    dsegs = [(dq, SEG_QFIG), (df, SEG_QFIG + 2048), (dv, SEG_QFIG + 4096), (dg, SEG_QFIG + 6144), (dz, SEG_Z),
             (dxbc_x, SEG_XBC), (dxbc_b, SEG_XBC + SSM_DINNER), (dxbc_c, SEG_XBC + SSM_DINNER + 1024)]
    dh = None
    g_in_parts = []
    for n, (dseg, off) in enumerate(dsegs):
        w_ = dseg.shape[1]
        dh = mm(dseg, in_t, "nn", name=f"d_h_{n}", tn=1024, tk=1024, dims=(t, D, w_), b_off=(off, 0), acc=dh)
        g_in_parts.append(mm(dseg, h, "tn", name=f"g_w_in_{n}", out_dtype=BF16, tm=1024, tn=2048, tk=512))
    dh = mm(ddtr, wts["dt_t"], "nn", name="d_h_dt", tn=1024, acc=dh)
    g_dt_t = mm(ddtr, h, "tn", name="g_w_in_dt", out_dtype=BF16, tm=128, tn=2048, tk=512)[:SSM_HEADS]
    for n, dgate_ in enumerate((dg_hg, dg_ssm)):
        dh = mm(dgate_, wts["g_t"], "nn", name=f"d_h_g{n}", tn=1024, tk=1024, dims=(t, D, D), b_off=(n * D, 0), acc=dh)
        g_in_parts.append(mm(dgate_, h, "tn", name=f"g_w_in_g{n}", out_dtype=BF16, tm=1024, tn=2048, tk=512))
    g_in_t = jnp.concatenate(g_in_parts[:8] + [g_dt_t] + g_in_parts[8:], axis=0)
    grad_x, g_mix_pre = norm_bwd(x, par["mix_pre_norm"], dh, dx1)

    big = dict(in_t=g_in_t, bh=g_bh, bs=g_bs, o=g_o, up_gate=g_up_gate, up_up=g_up_up, dn=g_dn)
    g_conv_w = jnp.concatenate([g_cw_x, g_cw_b, g_cw_c], axis=1)[:SSM_CONV]
    g_conv_b = jnp.concatenate([g_cb_x, g_cb_b, g_cb_c], axis=1)
    small = dict(mix_pre_norm=g_mix_pre, mix_post_norm=g_mix_post, hg_lb_table=g_table, hg_out_norm=g_hg_norm,
                 ssm_conv_w=g_conv_w, ssm_conv_b=g_conv_b, ssm_dt_bias=g_dt_bias, ssm_A_log=g_a_log,
                 ssm_D=g_dexp, ssm_out_norm=g_ssm_norm, ffn_pre_norm=g_ffn_pre, ffn_post_norm=g_ffn_post,
                 ffn_conv_w=g_fcw[:FFN_CONV], ffn_conv_b=g_fcb)
    return loss, grad_x, big, small


MESH = pl.DeviceIdType.MESH
ANY = pl.BlockSpec(memory_space=pl.ANY)
N_CHIPS = 4
PACK_ROWS = 9120
PACK_SPLITS = (5648, 6160, 7184, 7696, 9104)
UP_ROWS, UP_COLS = D_MODEL, 2816


def _place():
    x, y, c = lax.axis_index("x"), lax.axis_index("y"), lax.axis_index("c")
    chips = [(1 - x, y), (x, 1 - y), (1 - x, 1 - y)]
    return x, y, c, chips


def _rcopy(src, dst, send_sems, recv_sems, k, dev):
    return pltpu.make_async_remote_copy(src_ref=src, dst_ref=dst, send_sem=send_sems.at[k], recv_sem=recv_sems.at[k],
                                        device_id=dev, device_id_type=MESH)


def _pieces(rows, n):
    assert rows % n == 0 and (rows // n) % 16 == 0, (rows, n)
    return [(k * (rows // n), rows // n) for k in range(n)]


def _rows(c, hrows, piece):
    return pl.ds(pl.multiple_of(c * hrows + piece[0], 16), piece[1])


PACK_PIECES = 5
UP_PIECES = 4


def gather_weights(p1, p2):
    plan = [(p1.shape[0] // 2, _pieces(p1.shape[0] // 2, PACK_PIECES)), (p2.shape[0] // 2, _pieces(p2.shape[0] // 2, UP_PIECES))]
    n_sem = sum(2 * 3 * len(pcs) for _, pcs in plan)
    n_loc = sum(2 * len(pcs) for _, pcs in plan)

    def body(p1_ref, p2_ref, g1_ref, g2_ref, send_sems, recv_sems, local_sems):
        x, y, c, chips = _place()
        own = 2 * x + y
        sib = (x, y, 1 - c)
        refs = ((p1_ref, g1_ref), (p2_ref, g2_ref))
        local, first, arrive, passed, from_sib = [], [], [], [], []
        k = l = 0
        for (p, g), (hrows, pcs) in zip(refs, plan):
            for half in range(2):
                for pc in pcs:
                    rows = pl.ds(half * hrows + pc[0], pc[1])
                    local.append(pltpu.make_async_copy(p.at[rows], g.at[own, rows], local_sems.at[l]))
                    l += 1
            for j, chip in enumerate(chips):
                theirs = 2 * chip[0] + chip[1]
                for pc in pcs:
                    mine, other = _rows(c, hrows, pc), _rows(1 - c, hrows, pc)
                    first.append(_rcopy(p.at[mine], g.at[own, mine], send_sems, recv_sems, k, (*chip, c)))
                    arrive.append(_rcopy(g.at[theirs, mine], g.at[theirs, mine], send_sems, recv_sems, k, (*chip, c)))
                    passed.append(_rcopy(g.at[theirs, mine], g.at[theirs, mine], send_sems, recv_sems, k + 1, sib))
                    from_sib.append(_rcopy(g.at[theirs, other], g.at[theirs, other], send_sems, recv_sems, k + 1, sib))
                    k += 2
        for cp in local + first:
            cp.start()
        for got, fw in zip(arrive, passed):
            got.wait_recv()
            fw.start()
        for cp in from_sib:
            cp.wait_recv()
        for cp in first + passed:
            cp.wait_send()
        for cp in local:
            cp.wait()

    return pl.pallas_call(
        body, name="gather_weights", in_specs=[ANY, ANY], out_specs=[ANY, ANY],
        out_shape=[jax.ShapeDtypeStruct((N_CHIPS,) + p1.shape, p1.dtype), jax.ShapeDtypeStruct((N_CHIPS,) + p2.shape, p2.dtype)],
        scratch_shapes=[pltpu.SemaphoreType.DMA((n_sem,)), pltpu.SemaphoreType.DMA((n_sem,)), pltpu.SemaphoreType.DMA((n_loc,))],
    )(p1, p2)


def pair_exchange(g1, g2):
    shapes = [(N_CHIPS, g.shape[1] // 2, g.shape[2]) for g in (g1, g2)]
    plan = [(shapes[0][1], _pieces(shapes[0][1], PACK_PIECES)), (shapes[1][1], _pieces(shapes[1][1], UP_PIECES))]
    n_sem = sum(N_CHIPS * len(pcs) for _, pcs in plan)

    def body(g1_ref, g2_ref, o1_ref, r1_ref, o2_ref, r2_ref, send_sems, recv_sems, local_sems):
        x, y, c, _ = _place()
        sib = (x, y, 1 - c)
        cps, loc = [], []
        k = 0
        for (g, o, r), (hrows, pcs) in zip(((g1_ref, o1_ref, r1_ref), (g2_ref, o2_ref, r2_ref)), plan):
            for s in range(N_CHIPS):
                for pc in pcs:
                    dst = pl.ds(pc[0], pc[1])
                    loc.append(pltpu.make_async_copy(g.at[s, _rows(c, hrows, pc)], o.at[s, dst], local_sems.at[k]))
                    cps.append(_rcopy(g.at[s, _rows(1 - c, hrows, pc)], r.at[s, dst], send_sems, recv_sems, k, sib))
                    k += 1
        for cp in cps + loc:
            cp.start()
        for cp in cps:
            cp.wait()
        for cp in loc:
            cp.wait()

    return pl.pallas_call(
        body, name="pair_exchange", in_specs=[ANY, ANY], out_specs=[ANY] * 4,
        out_shape=[jax.ShapeDtypeStruct(shapes[0], g1.dtype)] * 2 + [jax.ShapeDtypeStruct(shapes[1], g2.dtype)] * 2,
        scratch_shapes=[pltpu.SemaphoreType.DMA((n_sem,)), pltpu.SemaphoreType.DMA((n_sem,)), pltpu.SemaphoreType.DMA((n_sem,))],
    )(g1, g2)


def chip_exchange(s1, s2):
    def body(s1_ref, s2_ref, o1_ref, r1_ref, o2_ref, r2_ref, send_sems, recv_sems, local_sems):
        x, y, c, chips = _place()
        own = 2 * x + y
        cps, loc = [], []
        for a, (s, o, r) in enumerate(((s1_ref, o1_ref, r1_ref), (s2_ref, o2_ref, r2_ref))):
            loc.append(pltpu.make_async_copy(s.at[own], o, local_sems.at[a]))
            for j, chip in enumerate(chips):
                cps.append(_rcopy(s.at[2 * chip[0] + chip[1]], r.at[j], send_sems, recv_sems, 3 * a + j, (*chip, c)))
        for cp in loc + cps:
            cp.start()
        for cp in cps:
            cp.wait()
        for cp in loc:
            cp.wait()

    outs = []
    for s in (s1, s2):
        outs += [jax.ShapeDtypeStruct(s.shape[1:], s.dtype), jax.ShapeDtypeStruct((3,) + s.shape[1:], s.dtype)]
    return pl.pallas_call(
        body, name="chip_exchange", in_specs=[ANY, ANY], out_specs=[ANY] * 4, out_shape=outs,
        scratch_shapes=[pltpu.SemaphoreType.DMA((6,)), pltpu.SemaphoreType.DMA((6,)), pltpu.SemaphoreType.DMA((2,))],
    )(s1, s2)


def pair_assemble(r1, r2):
    plan = [(r1.shape[0], _pieces(r1.shape[0], 3 * PACK_PIECES)), (r2.shape[0], _pieces(r2.shape[0], 2 * UP_PIECES))]
    n_sem = sum(len(pcs) for _, pcs in plan)

    def body(r1_ref, r2_ref, f1_ref, f2_ref, send_sems, recv_sems, local_sems):
        x, y, c, _ = _place()
        sib = (x, y, 1 - c)
        cps, loc, got = [], [], []
        k = 0
        for (r, f), (hrows, pcs) in zip(((r1_ref, f1_ref), (r2_ref, f2_ref)), plan):
            for pc in pcs:
                src = r.at[pl.ds(pc[0], pc[1])]
                loc.append(pltpu.make_async_copy(src, f.at[_rows(c, hrows, pc)], local_sems.at[k]))
                cps.append(_rcopy(src, f.at[_rows(c, hrows, pc)], send_sems, recv_sems, k, sib))
                got.append(_rcopy(src, f.at[_rows(1 - c, hrows, pc)], send_sems, recv_sems, k, sib))
                k += 1
        for cp in cps + loc:
            cp.start()
        for cp in got:
            cp.wait_recv()
        for cp in cps:
            cp.wait_send()
        for cp in loc:
            cp.wait()

    return pl.pallas_call(
        body, name="pair_assemble", in_specs=[ANY, ANY], out_specs=[ANY, ANY],
        out_shape=[jax.ShapeDtypeStruct((2 * r.shape[0], r.shape[1]), r.dtype) for r in (r1, r2)],
        scratch_shapes=[pltpu.SemaphoreType.DMA((n_sem,)), pltpu.SemaphoreType.DMA((n_sem,)), pltpu.SemaphoreType.DMA((n_sem,))],
    )(r1, r2)


N_DEV = 8


def gather_small(blk, reduce):
    rows, cols = blk.shape

    def body(x_ref, out_ref, all_ref, send_sems, recv_sems, local_sem):
        x, y, c, chips = _place()
        me, sib = (x, y, c), (x, y, 1 - c)

        def blk_rows(px, py, pc):
            return all_ref.at[pl.ds(pl.multiple_of((4 * px + 2 * py + pc) * rows, 8), rows), :]

        def copy(k, block, to, src=None):
            return _rcopy(blk_rows(*block) if src is None else src, blk_rows(*block), send_sems, recv_sems, k, to)

        mine = pltpu.make_async_copy(x_ref, blk_rows(*me), local_sem)
        mine.start()
        first = [copy(0, me, sib, src=x_ref)] + [copy(1 + j, me, (*chip, c), src=x_ref) for j, chip in enumerate(chips)]
        for cp in first:
            cp.start()
        passed = [copy(4 + j, (*chip, c), sib) for j, chip in enumerate(chips)]
        for j, chip in enumerate(chips):
            copy(1 + j, (*chip, c), me).wait_recv()
            passed[j].start()
        copy(0, sib, me).wait_recv()
        for j, chip in enumerate(chips):
            copy(4 + j, (*chip, 1 - c), me).wait_recv()
        for cp in first + passed:
            cp.wait_send()
        mine.wait()
        if reduce:
            acc = all_ref[0:rows, :]
            for d in range(1, N_DEV):
                acc = acc + all_ref[d * rows:(d + 1) * rows, :]
            out_ref[...] = acc
        else:
            out_ref[...] = all_ref[...]

    vmem = pl.BlockSpec(memory_space=pltpu.VMEM)
    return pl.pallas_call(
        body, name="reduce_small" if reduce else "gather_small", in_specs=[vmem], out_specs=vmem,
        out_shape=jax.ShapeDtypeStruct((rows if reduce else N_DEV * rows, cols), blk.dtype),
        scratch_shapes=[pltpu.VMEM((N_DEV * rows, cols), blk.dtype), pltpu.SemaphoreType.DMA((7,)),
                        pltpu.SemaphoreType.DMA((7,)), pltpu.SemaphoreType.DMA],
        compiler_params=pltpu.CompilerParams(vmem_limit_bytes=VMEM_LIMIT),
    )(blk)


WEIGHTS = ['w_in', 'mix_pre_norm', 'mix_post_norm', 'hg_lb_table', 'hg_out_norm', 'ssm_conv_w', 'ssm_conv_b',
           'ssm_dt_bias', 'ssm_A_log', 'ssm_D', 'ssm_out_norm', 'w_branch_hg', 'w_branch_ssm', 'w_out', 'ffn_pre_norm',
           'ffn_post_norm', 'ffn_w_up', 'ffn_conv_w', 'ffn_conv_b', 'ffn_w_down']
BIG = ('w_in', 'w_branch_hg', 'w_branch_ssm', 'w_out', 'ffn_w_up', 'ffn_w_down')
SMALL = tuple(n for n in WEIGHTS if n not in BIG)
CONV_SHARD = {'ssm_conv_w': SSM_CONV_DIM // N_CHIPS, 'ffn_conv_w': D_FF // N_CHIPS}
LANES = 128


def _pack(parts):
    flat = jnp.concatenate([p.reshape(-1) for p in parts])
    n = flat.shape[0]
    rows = -(-n // (8 * LANES)) * 8
    return jnp.pad(flat, (0, rows * LANES - n)).reshape(rows, LANES)


def _unpack(packed, shapes):
    flat = packed.reshape(-1)
    out, off = [], 0
    for s in shapes:
        n = int(np.prod(s))
        out.append(flat[off:off + n].reshape(s))
        off += n
    return out


def kernel(x, w_in, mix_pre_norm, mix_post_norm, hg_lb_table, hg_out_norm, ssm_conv_w, ssm_conv_b, ssm_dt_bias, ssm_A_log, ssm_D, ssm_out_norm, w_branch_hg, w_branch_ssm, w_out, ffn_pre_norm, ffn_post_norm, ffn_w_up, ffn_conv_w, ffn_conv_b, ffn_w_down, loss_target, m_w_in, m_mix_pre_norm, m_mix_post_norm, m_hg_lb_table, m_hg_out_norm, m_ssm_conv_w, m_ssm_conv_b, m_ssm_dt_bias, m_ssm_A_log, m_ssm_D, m_ssm_out_norm, m_w_branch_hg, m_w_branch_ssm, m_w_out, m_ffn_pre_norm, m_ffn_post_norm, m_ffn_w_up, m_ffn_conv_w, m_ffn_conv_b, m_ffn_w_down, v_w_in, v_mix_pre_norm, v_mix_post_norm, v_hg_lb_table, v_hg_out_norm, v_ssm_conv_w, v_ssm_conv_b, v_ssm_dt_bias, v_ssm_A_log, v_ssm_D, v_ssm_out_norm, v_w_branch_hg, v_w_branch_ssm, v_w_out, v_ffn_pre_norm, v_ffn_post_norm, v_ffn_w_up, v_ffn_conv_w, v_ffn_conv_b, v_ffn_w_down):
    w = dict(w_in=w_in, mix_pre_norm=mix_pre_norm, mix_post_norm=mix_post_norm, hg_lb_table=hg_lb_table, hg_out_norm=hg_out_norm, ssm_conv_w=ssm_conv_w, ssm_conv_b=ssm_conv_b, ssm_dt_bias=ssm_dt_bias, ssm_A_log=ssm_A_log, ssm_D=ssm_D, ssm_out_norm=ssm_out_norm, w_branch_hg=w_branch_hg, w_branch_ssm=w_branch_ssm, w_out=w_out, ffn_pre_norm=ffn_pre_norm, ffn_post_norm=ffn_post_norm, ffn_w_up=ffn_w_up, ffn_conv_w=ffn_conv_w, ffn_conv_b=ffn_conv_b, ffn_w_down=ffn_w_down)
    m = dict(w_in=m_w_in, mix_pre_norm=m_mix_pre_norm, mix_post_norm=m_mix_post_norm, hg_lb_table=m_hg_lb_table, hg_out_norm=m_hg_out_norm, ssm_conv_w=m_ssm_conv_w, ssm_conv_b=m_ssm_conv_b, ssm_dt_bias=m_ssm_dt_bias, ssm_A_log=m_ssm_A_log, ssm_D=m_ssm_D, ssm_out_norm=m_ssm_out_norm, w_branch_hg=m_w_branch_hg, w_branch_ssm=m_w_branch_ssm, w_out=m_w_out, ffn_pre_norm=m_ffn_pre_norm, ffn_post_norm=m_ffn_post_norm, ffn_w_up=m_ffn_w_up, ffn_conv_w=m_ffn_conv_w, ffn_conv_b=m_ffn_conv_b, ffn_w_down=m_ffn_w_down)
    v = dict(w_in=v_w_in, mix_pre_norm=v_mix_pre_norm, mix_post_norm=v_mix_post_norm, hg_lb_table=v_hg_lb_table, hg_out_norm=v_hg_out_norm, ssm_conv_w=v_ssm_conv_w, ssm_conv_b=v_ssm_conv_b, ssm_dt_bias=v_ssm_dt_bias, ssm_A_log=v_ssm_A_log, ssm_D=v_ssm_D, ssm_out_norm=v_ssm_out_norm, w_branch_hg=v_w_branch_hg, w_branch_ssm=v_w_branch_ssm, w_out=v_w_out, ffn_pre_norm=v_ffn_pre_norm, ffn_post_norm=v_ffn_post_norm, ffn_w_up=v_ffn_w_up, ffn_conv_w=v_ffn_conv_w, ffn_conv_b=v_ffn_conv_b, ffn_w_down=v_ffn_w_down)
    shard = 2 * lax.axis_index("x") + lax.axis_index("y")
    bf = lambda a: a.astype(BF16)

    p1 = jnp.concatenate([bf(w_in[0].T), bf(w_branch_hg[0]), bf(w_branch_ssm[0]), bf(w_out[0]), bf(ffn_w_down[0]),
                          jnp.zeros((PACK_ROWS - PACK_SPLITS[-1], D_MODEL), BF16)], axis=0)
    g1, g2 = gather_weights(p1, bf(ffn_w_up[0]))
    s0, s1, s2, s3, s4 = PACK_SPLITS
    in_t = g1[:, :s0].reshape(IN_TOTAL, D_MODEL)
    wts = dict(in_t=in_t, g_t=in_t[SEG_G:], dt_t=jnp.pad(in_t[SEG_DT:SEG_G], ((0, DT_PAD - SSM_HEADS), (0, 0))),
               bh=g1[:, s0:s1].reshape(-1, D_MODEL), bs=g1[:, s1:s2].reshape(-1, D_MODEL),
               o=g1[:, s2:s3].reshape(-1, D_MODEL), dn=g1[:, s3:s4].reshape(-1, D_MODEL),
               up=jnp.transpose(g2, (1, 0, 2)).reshape(D_MODEL, 2 * D_FF))
    conv_cols = max(CONV_SHARD.values())
    padc = lambda a: jnp.pad(a, ((0, 0), (0, conv_cols - a.shape[1])))
    conv_blk = jnp.concatenate([padc(ssm_conv_w[0]), padc(ffn_conv_w[0]), jnp.zeros((1, conv_cols), F32)], axis=0)
    conv_all = gather_small(conv_blk, reduce=False)
    par = {n: w[n] for n in SMALL}
    par["ssm_conv_w"] = jnp.concatenate([conv_all[16 * s:16 * s + SSM_CONV, :CONV_SHARD['ssm_conv_w']] for s in range(N_CHIPS)], axis=1)
    par["ffn_conv_w"] = jnp.concatenate([conv_all[16 * s + SSM_CONV:16 * s + SSM_CONV + FFN_CONV, :CONV_SHARD['ffn_conv_w']]
                                         for s in range(N_CHIPS)], axis=1)

    loss, grad_x, big, small = local_step(x[0], loss_target[0], wts, par)
    loss = lax.psum(loss[0, 0], ("x", "y", "c"))

    zpad = jnp.zeros((N_CHIPS, PACK_ROWS - PACK_SPLITS[-1], D_MODEL), BF16)
    gg1 = jnp.concatenate([big[k].reshape(N_CHIPS, -1, D_MODEL) for k in ("in_t", "bh", "bs", "o", "dn")] + [zpad], axis=1)
    gup = jnp.concatenate([big["up_gate"], big["up_up"]], axis=1)
    gg2 = jnp.transpose(gup.reshape(D_MODEL, N_CHIPS, UP_COLS), (1, 0, 2))
    o1, r1, o2, r2 = pair_exchange(gg1, gg2)
    h1, h2 = PACK_ROWS // 2, UP_ROWS // 2
    flat = lambda a: a.reshape(-1, a.shape[-1])
    c1 = ew_sum("pair_sum_1", [(flat(o1), 0), (flat(r1), 0)], N_CHIPS * h1, BF16, 480).reshape(N_CHIPS, h1, D_MODEL)
    c2 = ew_sum("pair_sum_2", [(flat(o2), 0), (flat(r2), 0)], N_CHIPS * h2, BF16, 512).reshape(N_CHIPS, h2, UP_COLS)
    ob1, rb1, ob2, rb2 = chip_exchange(c1, c2)
    red1 = ew_sum("chip_sum_1", [(ob1, 0)] + [(flat(rb1), j * h1) for j in range(3)], h1, F32, 240)
    red2 = ew_sum("chip_sum_2", [(ob2, 0)] + [(flat(rb2), j * h2) for j in range(3)], h2, F32, 256)
    f1, f2 = pair_assemble(red1, red2)
    grads = dict(w_in=f1[:s0].T, w_branch_hg=f1[s0:s1], w_branch_ssm=f1[s1:s2], w_out=f1[s2:s3], ffn_w_down=f1[s3:s4],
                 ffn_w_up=f2)

    small["hg_out_norm"] = ew_sum("sum_heads", [(small["hg_out_norm"][hd], 0) for hd in range(HG_HEADS)], 1, F32, 1)
    small["ssm_D"] = fold_heads(small["ssm_D"])[:, :SSM_HEADS]
    small["ssm_dt_bias"] = small["ssm_dt_bias"][:, :SSM_HEADS]
    small["ssm_A_log"] = small["ssm_A_log"][:, :SSM_HEADS]
    shapes = [small[n].shape for n in SMALL]
    summed = _unpack(gather_small(_pack([small[n] for n in SMALL]), reduce=True), shapes)
    for n, g in zip(SMALL, summed):
        if n in CONV_SHARD:
            g = lax.dynamic_slice_in_dim(g, shard * CONV_SHARD[n], CONV_SHARD[n], axis=1)
        grads[n] = g

    two_d = lambda a: a.reshape(a.shape[-2], a.shape[-1])
    delta, new_m, new_v = {}, {}, {}
    for n, tr in (("w_in", 64), ("w_branch_hg", 128), ("w_branch_ssm", 128), ("w_out", 128), ("ffn_w_up", 128), ("ffn_w_down", 128)):
        delta[n], new_m[n], new_v[n] = adamw("adamw_" + n, two_d(w[n]), grads[n], two_d(m[n]), two_d(v[n]), tr)
    sm_shapes = [two_d(w[n]).shape for n in SMALL]
    packed = adamw("adamw_small", _pack([two_d(w[n]) for n in SMALL]), _pack([grads[n] for n in SMALL]),
                   _pack([two_d(m[n]) for n in SMALL]), _pack([two_d(v[n]) for n in SMALL]), 1024)
    for res, packed_res in zip((delta, new_m, new_v), packed):
        for n, a in zip(SMALL, _unpack(packed_res, sm_shapes)):
            res[n] = a
    shaped = lambda d: [d[n].reshape(w[n].shape) for n in WEIGHTS]
    return (loss, grad_x[None], *shaped(grads), *shaped(delta), *shaped(new_m), *shaped(new_v))
```

```python
import functools

import jax
import jax.numpy as jnp
import numpy as np
from jax import lax
from jax.experimental import pallas as pl
from jax.experimental.pallas import tpu as pltpu

F32 = jnp.float32
BF16 = jnp.bfloat16

D_MODEL = 2048
EPS = 1e-6
HG_HEADS = 16
HG_DK = 128
HG_CHUNK = 64
HG_SUB = 16
SSM_DINNER = 4096
SSM_HEADDIM = 64
SSM_HEADS = 64
SSM_GROUPS = 8
SSM_DSTATE = 128
SSM_CONV = 4
SSM_CHUNK = 256
SSM_CONV_DIM = 6144
D_FF = 5632
FFN_CONV = 3
DT_PAD = 128

ADAM_LR = 0.001
ADAM_B1 = 0.9
ADAM_B2 = 0.999
ADAM_EPS = 1e-08
ADAM_WD = 0.01
ADAM_STEP = 10

VMEM_LIMIT = 56 * 1024 * 1024
HI = lax.Precision.HIGHEST


def _cp(sem, **kw):
    return pltpu.CompilerParams(dimension_semantics=sem, vmem_limit_bytes=VMEM_LIMIT, **kw)


_DIMS = {"nn": (((1,), (0,)), ((), ())), "nt": (((1,), (1,)), ((), ())), "tn": (((0,), (0,)), ((), ()))}


def mm(a, b, mode, *, name, out_dtype=F32, tm=512, tn=512, tk=None, acc=None, n_major=True,
       dims=None, a_off=(0, 0), b_off=(0, 0)):
    if dims is not None:
        M, N, K = dims
    else:
        if mode == "nn":
            (M, K), (K2, N) = a.shape, b.shape
        elif mode == "nt":
            (M, K), (N, K2) = a.shape, b.shape
        else:
            (K, M), (K2, N) = a.shape, b.shape
        assert K == K2, (a.shape, b.shape, mode)
    tm, tn = min(tm, M), min(tn, N)
    tk = K if tk is None else min(tk, K)
    assert M % tm == 0 and N % tn == 0 and K % tk == 0, (M, N, K, tm, tn, tk)
    a_blk = (tk, tm) if mode == "tn" else (tm, tk)
    b_blk = (tn, tk) if mode == "nt" else (tk, tn)
    assert all(o % s == 0 for o, s in zip(a_off, a_blk)) and all(o % s == 0 for o, s in zip(b_off, b_blk))
    ao0, ao1 = a_off[0] // a_blk[0], a_off[1] // a_blk[1]
    bo0, bo1 = b_off[0] // b_blk[0], b_off[1] // b_blk[1]
    nk = K // tk
    if n_major:
        grid = (N // tn, M // tm, nk)
        ij = lambda p0, p1: (p1, p0)
    else:
        grid = (M // tm, N // tn, nk)
        ij = lambda p0, p1: (p0, p1)

    def a_map(p0, p1, k):
        i, _ = ij(p0, p1)
        return (k + ao0, i + ao1) if mode == "tn" else (i + ao0, k + ao1)

    def b_map(p0, p1, k):
        _, j = ij(p0, p1)
        return (j + bo0, k + bo1) if mode == "nt" else (k + bo0, j + bo1)

    def o_map(p0, p1, k):
        return ij(p0, p1)

    a_spec = pl.BlockSpec(a_blk, a_map)
    b_spec = pl.BlockSpec(b_blk, b_map)
    o_spec = pl.BlockSpec((tm, tn), o_map)
    dims = _DIMS[mode]
    has_acc = acc is not None

    def body(*refs):
        if has_acc:
            a_ref, b_ref, c_ref, o_ref, acc_ref = refs
        else:
            a_ref, b_ref, o_ref, acc_ref = refs
        k = pl.program_id(2)
        part = lax.dot_general(a_ref[...], b_ref[...], dims, preferred_element_type=F32)

        @pl.when(k == 0)
        def _():
            acc_ref[...] = part

        @pl.when(k > 0)
        def _():
            acc_ref[...] += part

        @pl.when(k == nk - 1)
        def _():
            r = acc_ref[...]
            if has_acc:
                r = r + c_ref[...].astype(F32)
            o_ref[...] = r.astype(out_dtype)

    in_specs = [a_spec, b_spec] + ([o_spec] if has_acc else [])
    args = (a, b) + ((acc,) if has_acc else ())
    return pl.pallas_call(
        body, name=name, grid=grid, in_specs=in_specs, out_specs=o_spec,
        out_shape=jax.ShapeDtypeStruct((M, N), out_dtype),
        scratch_shapes=[pltpu.VMEM((tm, tn), F32)],
        compiler_params=_cp(("parallel", "parallel", "arbitrary")),
    )(*args)


def mm_segments(name, segs, bs, *, tm, tn, tk):
    m_, n_ = segs[0][0].shape[0], bs[0].shape[1]
    steps, k0 = [], 0
    for a, bi, row in segs:
        w = a.shape[1]
        tks = min(tk, w)
        assert w % tks == 0 and row % tks == 0 and tks == min(tk, bs[bi].shape[0]), (w, row, tks)
        steps.append((k0, w // tks, tks, bi, row // tks))
        k0 += w // tks
    nk = k0
    assert m_ % tm == 0 and n_ % tn == 0

    def a_spec(k_first, count, tks):
        return pl.BlockSpec((tm, tks), lambda j, i, k: (i, jnp.clip(k - k_first, 0, count - 1)))

    def b_spec(bi):
        mine = [s for s in steps if s[3] == bi]

        def index(j, i, k):
            blk = mine[0][4]
            for k_first, count, _, _, first_blk in mine:
                blk = jnp.where(k >= k_first, first_blk + jnp.minimum(k - k_first, count - 1), blk)
            return (blk, j)
        return pl.BlockSpec((mine[0][2], tn), index)

    ns = len(segs)

    def body(*refs):
        a_refs, b_refs, o_ref, acc_ref = refs[:ns], refs[ns:ns + len(bs)], refs[-2], refs[-1]
        k = pl.program_id(2)

        @pl.when(k == 0)
        def _():
            acc_ref[...] = jnp.zeros_like(acc_ref)

        for a_ref, (k_first, count, _, bi, _) in zip(a_refs, steps):
            @pl.when((k >= k_first) & (k < k_first + count))
            def _(a_ref=a_ref, bi=bi):
                acc_ref[...] += jnp.dot(a_ref[...], b_refs[bi][...], preferred_element_type=F32)

        @pl.when(k == nk - 1)
        def _():
            o_ref[...] = acc_ref[...]

    return pl.pallas_call(
        body, name=name, grid=(n_ // tn, m_ // tm, nk),
        in_specs=[a_spec(s[0], s[1], s[2]) for s in steps] + [b_spec(bi) for bi in range(len(bs))],
        out_specs=pl.BlockSpec((tm, tn), lambda j, i, k: (i, j)),
        out_shape=jax.ShapeDtypeStruct((m_, n_), F32),
        scratch_shapes=[pltpu.VMEM((tm, tn), F32)],
        compiler_params=_cp(("parallel", "parallel", "arbitrary")),
    )(*[a for a, _, _ in segs], *bs)


def _bdot_plain(a, b, mode):
    return lax.dot_general(a.astype(BF16), b.astype(BF16), _DIMS[mode], preferred_element_type=F32)


@functools.partial(jax.custom_vjp, nondiff_argnums=(2,))
def _bdot_vjp(a, b, mode):
    return _bdot_plain(a, b, mode)


def _bdot_fwd(a, b, mode):
    return _bdot_plain(a, b, mode), (a, b)


def _bdot_bwd(mode, res, g):
    a, b = res
    if mode == "nn":
        return _bdot_plain(g, b, "nt"), _bdot_plain(a, g, "tn")
    if mode == "nt":
        return _bdot_plain(g, b, "nn"), _bdot_plain(g, a, "tn")
    return _bdot_plain(b, g, "nt"), _bdot_plain(a, g, "nn")


_bdot_vjp.defvjp(_bdot_fwd, _bdot_bwd)


def _split3(x):
    x1 = x.astype(BF16)
    r1 = x - x1.astype(F32)
    x2 = r1.astype(BF16)
    return x1, x2, (r1 - x2.astype(F32)).astype(BF16)


def _hdot_impl(a, b, mode, data):
    if data == "a":
        sel = b.astype(BF16)
        parts = [lax.dot_general(p, sel, _DIMS[mode], preferred_element_type=F32) for p in _split3(a)]
    else:
        sel = a.astype(BF16)
        parts = [lax.dot_general(sel, p, _DIMS[mode], preferred_element_type=F32) for p in _split3(b)]
    return (parts[2] + parts[1]) + parts[0]


@functools.partial(jax.custom_vjp, nondiff_argnums=(2, 3))
def _hdot(a, b, mode="nn", data="b"):
    return _hdot_impl(a, b, mode, data)


def _hdot_fwd(a, b, mode, data):
    return _hdot_impl(a, b, mode, data), (a, b)


def _hdot_bwd(mode, data, res, g):
    a, b = res
    if data == "a":
        da = {"nn": lambda: _hdot_impl(g, b, "nt", "a"), "nt": lambda: _hdot_impl(g, b, "nn", "a"),
              "tn": lambda: _hdot_impl(b, g, "nt", "b")}[mode]()
        return da, jnp.zeros_like(b)
    db = {"nn": lambda: _hdot_impl(a, g, "tn", "b"), "nt": lambda: _hdot_impl(g, a, "tn", "a"),
          "tn": lambda: _hdot_impl(a, g, "nn", "b")}[mode]()
    return jnp.zeros_like(a), db


_hdot.defvjp(_hdot_fwd, _hdot_bwd)


def _sigmoid(x):
    return 1.0 / (1.0 + jnp.exp(-x))


def _silu(x):
    return x * _sigmoid(x)


def _iota(shape, dim):
    return lax.broadcasted_iota(jnp.int32, shape, dim)


def _rms(x, w):
    return x * lax.rsqrt(jnp.mean(x * x, axis=-1, keepdims=True) + EPS) * w


def _hg_chunk(q_raw, f_raw, v, g, st, t0, t1, nw, dot):
    c = q_raw.shape[0]
    m = jnp.maximum(t0, t1)
    e0, e1 = jnp.exp(t0 - m), jnp.exp(t1 - m)
    lb = e0 / (e0 + e1)
    f = lb + (1.0 - lb) * _sigmoid(f_raw)
    k = 1.0 - f
    lf = jnp.log(f)
    qh = _silu(q_raw) * (HG_DK ** -0.5)
    row, col = _iota((c, c), 0), _iota((c, c), 1)
    causal = col <= row
    tril = jnp.where(causal, 1.0, 0.0).astype(F32)
    trilb = jnp.where(causal & (col // HG_SUB == row // HG_SUB), 1.0, 0.0).astype(F32)
    b = _hdot(tril, lf)
    bl = _hdot(trilb, lf)
    a_row = b - bl
    rid = _iota((c, HG_DK), 0)
    qt = qh * jnp.exp(bl)
    kt = k * jnp.exp(-bl)
    scores = jnp.zeros((c, c), F32)
    for j in range(c // HG_SUB):
        if j == 0:
            qj = qt * jnp.exp(jnp.minimum(a_row, 0.0))
        else:
            a_j = jnp.sum(jnp.where(rid == j * HG_SUB - 1, b, 0.0), axis=0, keepdims=True)
            qj = qt * jnp.exp(jnp.minimum(a_row - a_j, 0.0))
        kj = jnp.where(rid // HG_SUB == j, kt, 0.0)
        scores = scores + dot(qj, kj, "nt")
    scores = jnp.where(causal, scores, 0.0)
    o = dot(scores, v, "nn") + dot(qh * jnp.exp(b), st, "nt")
    b_last = jnp.sum(jnp.where(rid == c - 1, b, 0.0), axis=0, keepdims=True)
    st_new = st * jnp.exp(b_last) + dot(v, k * jnp.exp(b_last - b), "tn")
    y = _rms(o, nw) * _silu(g)
    return y, st_new


HG_HPS = 4
HG_W = HG_HPS * HG_DK


def hgrn2_fwd(qfig, table, nw, *, step_chunks=2):
    t = qfig.shape[0]
    rows = HG_CHUNK * step_chunks
    nsteps = t // rows
    nh = HG_HEADS // HG_HPS

    def body(q_ref, f_ref, v_ref, g_ref, tab_ref, nw_ref, y_ref, s_ref, st_scr):
        @pl.when(pl.program_id(1) == 0)
        def _():
            st_scr[...] = jnp.zeros_like(st_scr)

        nwv = nw_ref[...]
        for c in range(step_chunks):
            sl = pl.ds(c * HG_CHUNK, HG_CHUNK)
            for hh in range(HG_HPS):
                ln = pl.ds(hh * HG_DK, HG_DK)
                st = st_scr[hh]
                s_ref[hh, c] = st
                y, st_new = _hg_chunk(q_ref[sl, ln], f_ref[sl, ln], v_ref[sl, ln], g_ref[sl, ln], st,
                                      tab_ref[0:1, ln], tab_ref[1:2, ln], nwv, _bdot_vjp)
                y_ref[sl, ln] = y.astype(BF16)
                st_scr[hh] = st_new

    blk = lambda off: pl.BlockSpec((rows, HG_W), lambda h, c, off=off: (c, off + h))
    return pl.pallas_call(
        body, name="hgrn2_fwd", grid=(nh, nsteps),
        in_specs=[blk(0), blk(nh), blk(2 * nh), blk(3 * nh),
                  pl.BlockSpec((2, HG_W), lambda h, c: (0, h)), pl.BlockSpec((1, HG_DK), lambda h, c: (0, 0))],
        out_specs=[pl.BlockSpec((rows, HG_W), lambda h, c: (c, h)),
                   pl.BlockSpec((HG_HPS, step_chunks, HG_DK, HG_DK), lambda h, c: (h, c, 0, 0))],
        out_shape=[jax.ShapeDtypeStruct((t, HG_HEADS * HG_DK), BF16),
                   jax.ShapeDtypeStruct((HG_HEADS, t // HG_CHUNK, HG_DK, HG_DK), F32)],
        scratch_shapes=[pltpu.VMEM((HG_HPS, HG_DK, HG_DK), F32)],
        compiler_params=_cp(("parallel", "arbitrary")),
    )(qfig, qfig, qfig, qfig, table, nw)


def hgrn2_bwd(qfig, table, nw, states, dy, *, step_chunks=2):
    t = qfig.shape[0]
    rows = HG_CHUNK * step_chunks
    nsteps = t // rows
    nh = HG_HEADS // HG_HPS

    def body(q_ref, f_ref, v_ref, g_ref, tab_ref, nw_ref, s_ref, dy_ref,
             dq_ref, df_ref, dv_ref, dg_ref, dtab_ref, dnw_ref, dst_scr):
        @pl.when(pl.program_id(1) == 0)
        def _():
            dst_scr[...] = jnp.zeros_like(dst_scr)
            dtab_ref[...] = jnp.zeros_like(dtab_ref)
            dnw_ref[...] = jnp.zeros_like(dnw_ref)

        nwv = nw_ref[...]
        fn = functools.partial(_hg_chunk, dot=_bdot_vjp)
        for c in reversed(range(step_chunks)):
            sl = pl.ds(c * HG_CHUNK, HG_CHUNK)
            for hh in range(HG_HPS):
                ln = pl.ds(hh * HG_DK, HG_DK)
                _, vjp = jax.vjp(fn, q_ref[sl, ln], f_ref[sl, ln], v_ref[sl, ln], g_ref[sl, ln], s_ref[hh, c],
                                 tab_ref[0:1, ln], tab_ref[1:2, ln], nwv)
                dq, df, dv, dg, dst, dt0, dt1, dnw = vjp((dy_ref[sl, ln].astype(F32), dst_scr[hh]))
                dq_ref[sl, ln] = dq.astype(BF16)
                df_ref[sl, ln] = df.astype(BF16)
                dv_ref[sl, ln] = dv.astype(BF16)
                dg_ref[sl, ln] = dg.astype(BF16)
                dst_scr[hh] = dst
                dtab_ref[0:1, ln] += dt0
                dtab_ref[1:2, ln] += dt1
                dnw_ref[hh] += dnw

    rev = lambda c: nsteps - 1 - c
    blk = lambda off: pl.BlockSpec((rows, HG_W), lambda h, c, off=off: (rev(c), off + h))
    oblk = lambda: pl.BlockSpec((rows, HG_W), lambda h, c: (rev(c), h))
    d = HG_HEADS * HG_DK
    outs = pl.pallas_call(
        body, name="hgrn2_bwd", grid=(nh, nsteps),
        in_specs=[blk(0), blk(nh), blk(2 * nh), blk(3 * nh),
                  pl.BlockSpec((2, HG_W), lambda h, c: (0, h)), pl.BlockSpec((1, HG_DK), lambda h, c: (0, 0)),
                  pl.BlockSpec((HG_HPS, step_chunks, HG_DK, HG_DK), lambda h, c: (h, rev(c), 0, 0)),
                  pl.BlockSpec((rows, HG_W), lambda h, c: (rev(c), h))],
        out_specs=[oblk(), oblk(), oblk(), oblk(),
                   pl.BlockSpec((2, HG_W), lambda h, c: (0, h)),
                   pl.BlockSpec((HG_HPS, 1, HG_DK), lambda h, c: (h, 0, 0))],
        out_shape=[jax.ShapeDtypeStruct((t, d), BF16)] * 4
        + [jax.ShapeDtypeStruct((2, d), F32), jax.ShapeDtypeStruct((HG_HEADS, 1, HG_DK), F32)],
        scratch_shapes=[pltpu.VMEM((HG_HPS, HG_DK, HG_DK), F32)],
        compiler_params=_cp(("parallel", "arbitrary")),
    )(qfig, qfig, qfig, qfig, table, nw, states, dy)
    return outs


def _ssd_chunk(xs2, dt, acum, bm, cm, s2, pair, dot):
    c = xs2.shape[0]
    lane = _iota((DT_PAD, 128), 1)
    expand = jnp.where(_iota((DT_PAD, 128), 0) == 2 * pair + lane // SSM_HEADDIM, 1.0, 0.0).astype(F32)
    sel = jnp.where(_iota((8, DT_PAD), 1) == 2 * pair + _iota((8, DT_PAD), 0), 1.0, 0.0).astype(F32)
    sel = jnp.where(_iota((8, DT_PAD), 0) < 2, sel, 0.0)
    dtx = _hdot(dt, expand, "nn", "a")
    acol = _hdot(acum, expand, "nn", "a")
    arow8 = _hdot(sel, acum, "nt", "b")
    row, col = _iota((c, c), 0), _iota((c, c), 1)
    causal = col <= row
    cb = dot(cm, bm, "nt")
    x2 = xs2 * dtx
    lane_c = _iota((c, 128), 1)
    y = dot(cm, s2, "nn") * jnp.exp(acol)
    for r in range(2):
        head = (lane_c // SSM_HEADDIM) == r
        a_c = jnp.sum(jnp.where(head & (lane_c % SSM_HEADDIM == 0), acol, 0.0), axis=1, keepdims=True)
        a_r = jnp.sum(jnp.where(_iota((8, c), 0) == r, arow8, 0.0), axis=0, keepdims=True)
        decay = jnp.where(causal, jnp.exp(jnp.minimum(a_c - a_r, 0.0)), 0.0)
        y = y + dot(cb * decay, jnp.where(head, x2, 0.0), "nn")
    a_last = jnp.sum(jnp.where(_iota((c, 128), 0) == c - 1, acol, 0.0), axis=0, keepdims=True)
    s2_new = s2 * jnp.exp(a_last) + dot(bm, x2 * jnp.exp(a_last - acol), "tn")
    return y, s2_new


SSM_PAIRS = SSM_HEADS // 2
PAIRS_PER_GROUP = SSM_PAIRS // SSM_GROUPS
SSD_PPS = 4
SSD_W = SSD_PPS * 128
_XS_BLOCKS = SSM_DINNER // 128
_B_BLOCK0 = _XS_BLOCKS
_C_BLOCK0 = _XS_BLOCKS + SSM_GROUPS


def ssd_fwd(xbc_act, dt, acum):
    t = xbc_act.shape[0]
    nc = t // SSM_CHUNK
    c_ = SSM_CHUNK

    def body(xs_ref, b_ref, c_ref, dt_ref, ac_ref, y_ref, s_ref, s_scr):
        q = pl.program_id(1)
        for r in range(SSD_PPS):
            p = SSD_PPS * q + r
            ln = pl.ds(r * 128, 128)

            @pl.when(pl.program_id(0) == 0)
            def _():
                s_scr[p] = jnp.zeros((SSM_DSTATE, 128), F32)

            s2 = s_scr[p]
            s_ref[r] = s2
            y, s2_new = _ssd_chunk(xs_ref[:, ln], dt_ref[...], ac_ref[...], b_ref[...], c_ref[...], s2, p, _bdot_vjp)
            y_ref[:, ln] = y
            s_scr[p] = s2_new

    grp = lambda q: q // (PAIRS_PER_GROUP // SSD_PPS)
    return pl.pallas_call(
        body, name="ssd_fwd", grid=(nc, SSM_PAIRS // SSD_PPS),
        in_specs=[pl.BlockSpec((c_, SSD_W), lambda c, q: (c, q)),
                  pl.BlockSpec((c_, 128), lambda c, q: (c, _B_BLOCK0 + grp(q))),
                  pl.BlockSpec((c_, 128), lambda c, q: (c, _C_BLOCK0 + grp(q))),
                  pl.BlockSpec((c_, DT_PAD), lambda c, q: (c, 0)),
                  pl.BlockSpec((c_, DT_PAD), lambda c, q: (c, 0))],
        out_specs=[pl.BlockSpec((c_, SSD_W), lambda c, q: (c, q)),
                   pl.BlockSpec((None, SSD_PPS, SSM_DSTATE, 128), lambda c, q: (c, q, 0, 0))],
        out_shape=[jax.ShapeDtypeStruct((t, SSM_DINNER), F32),
                   jax.ShapeDtypeStruct((nc, SSM_PAIRS, SSM_DSTATE, 128), F32)],
        scratch_shapes=[pltpu.VMEM((SSM_PAIRS, SSM_DSTATE, 128), F32)],
        compiler_params=_cp(("arbitrary", "arbitrary")),
    )(xbc_act, xbc_act, xbc_act, dt, acum)


def ssd_bwd(xbc_act, dt, acum, states, dy, dskip):
    t = xbc_act.shape[0]
    nc = t // SSM_CHUNK
    c_ = SSM_CHUNK
    rev = lambda c: nc - 1 - c

    def body(xs_ref, b_ref, c_ref, dt_ref, ac_ref, s_ref, dy_ref, sk_ref,
             dxs_ref, db_ref, dc_ref, ddt_ref, dac_ref, ds_scr):
        q = pl.program_id(1)
        steps_per_group = PAIRS_PER_GROUP // SSD_PPS
        db = dc = ddt = dac = None
        for r in range(SSD_PPS):
            p = SSD_PPS * q + r
            ln = pl.ds(r * 128, 128)

            @pl.when(pl.program_id(0) == 0)
            def _():
                ds_scr[p] = jnp.zeros((SSM_DSTATE, 128), F32)

            fn = functools.partial(_ssd_chunk, pair=p, dot=_bdot_vjp)
            _, vjp = jax.vjp(fn, xs_ref[:, ln], dt_ref[...], ac_ref[...], b_ref[...], c_ref[...], s_ref[r])
            dxs, ddt_r, dac_r, db_r, dc_r, ds = vjp((dy_ref[:, ln], ds_scr[p]))
            dxs_ref[:, ln] = dxs + sk_ref[:, ln]
            ds_scr[p] = ds
            db, dc = (db_r, dc_r) if r == 0 else (db + db_r, dc + dc_r)
            ddt, dac = (ddt_r, dac_r) if r == 0 else (ddt + ddt_r, dac + dac_r)

        @pl.when(q % steps_per_group == 0)
        def _():
            db_ref[...] = db
            dc_ref[...] = dc

        @pl.when(q % steps_per_group != 0)
        def _():
            db_ref[...] += db
            dc_ref[...] += dc

        @pl.when(q == 0)
        def _():
            ddt_ref[...] = ddt
            dac_ref[...] = dac

        @pl.when(q != 0)
        def _():
            ddt_ref[...] += ddt
            dac_ref[...] += dac

    grp = lambda q: q // (PAIRS_PER_GROUP // SSD_PPS)
    return pl.pallas_call(
        body, name="ssd_bwd", grid=(nc, SSM_PAIRS // SSD_PPS),
        in_specs=[pl.BlockSpec((c_, SSD_W), lambda c, q: (rev(c), q)),
                  pl.BlockSpec((c_, 128), lambda c, q: (rev(c), _B_BLOCK0 + grp(q))),
                  pl.BlockSpec((c_, 128), lambda c, q: (rev(c), _C_BLOCK0 + grp(q))),
                  pl.BlockSpec((c_, DT_PAD), lambda c, q: (rev(c), 0)),
                  pl.BlockSpec((c_, DT_PAD), lambda c, q: (rev(c), 0)),
                  pl.BlockSpec((None, SSD_PPS, SSM_DSTATE, 128), lambda c, q: (rev(c), q, 0, 0)),
                  pl.BlockSpec((c_, SSD_W), lambda c, q: (rev(c), q)),
                  pl.BlockSpec((c_, SSD_W), lambda c, q: (rev(c), q))],
        out_specs=[pl.BlockSpec((c_, SSD_W), lambda c, q: (rev(c), q)),
                   pl.BlockSpec((c_, 128), lambda c, q: (rev(c), grp(q))),
                   pl.BlockSpec((c_, 128), lambda c, q: (rev(c), grp(q))),
                   pl.BlockSpec((c_, DT_PAD), lambda c, q: (rev(c), 0)),
                   pl.BlockSpec((c_, DT_PAD), lambda c, q: (rev(c), 0))],
        out_shape=[jax.ShapeDtypeStruct((t, SSM_DINNER), F32),
                   jax.ShapeDtypeStruct((t, SSM_GROUPS * SSM_DSTATE), F32),
                   jax.ShapeDtypeStruct((t, SSM_GROUPS * SSM_DSTATE), F32),
                   jax.ShapeDtypeStruct((t, DT_PAD), F32),
                   jax.ShapeDtypeStruct((t, DT_PAD), F32)],
        scratch_shapes=[pltpu.VMEM((SSM_PAIRS, SSM_DSTATE, 128), F32)],
        compiler_params=_cp(("arbitrary", "arbitrary")),
    )(xbc_act, xbc_act, xbc_act, dt, acum, states, dy, dskip)


def rowwise(name, fn, row_ins, par_ins, row_outs, acc_outs, *, tt, ncb=1):
    t = row_ins[0][0].shape[0]
    assert t % tt == 0
    n_ri, n_pi, n_ro, n_ao = len(row_ins), len(par_ins), len(row_outs), len(acc_outs)

    def body(*refs):
        i = pl.program_id(1)
        ins = [r[...] for r in refs[:n_ri + n_pi]]
        outs = fn(*ins)
        ro_refs = refs[n_ri + n_pi:n_ri + n_pi + n_ro]
        ao_refs = refs[n_ri + n_pi + n_ro:]
        for r, v in zip(ro_refs, outs[:n_ro]):
            r[...] = v.astype(r.dtype)
        for r, v in zip(ao_refs, outs[n_ro:]):
            @pl.when(i == 0)
            def _(r=r, v=v):
                r[...] = v

            @pl.when(i > 0)
            def _(r=r, v=v):
                r[...] += v

    in_specs = [pl.BlockSpec((tt, bc), lambda j, i, off=off: (i, off + j)) for _, bc, off in row_ins]
    in_specs += [pl.BlockSpec((a.shape[0], bc), lambda j, i, off=off: (0, off + j)) for a, bc, off in par_ins]
    out_specs = [pl.BlockSpec((tt, bc), lambda j, i: (i, j)) for _, bc, _ in row_outs]
    out_specs += [pl.BlockSpec((r, bc), lambda j, i: (0, j)) for r, _, bc in acc_outs]
    out_shape = [jax.ShapeDtypeStruct((t, c), dt) for c, _, dt in row_outs]
    out_shape += [jax.ShapeDtypeStruct((r, c), F32) for r, c, _ in acc_outs]
    return pl.pallas_call(
        body, name=name, grid=(ncb, t // tt), in_specs=in_specs, out_specs=out_specs, out_shape=out_shape,
        compiler_params=_cp(("parallel", "arbitrary")),
    )(*[a for a, _, _ in row_ins], *[a for a, _, _ in par_ins])


def _colsum(v):
    return jnp.sum(v, axis=0, keepdims=True)


def _softplus(x):
    return jnp.maximum(x, 0.0) + jnp.log(1.0 + jnp.exp(-jnp.abs(x)))


def _gelu_tanh(x):
    return 0.5 * x * (1.0 + jnp.tanh(0.7978845608028654 * (x + 0.044715 * (x * x * x))))


D = D_MODEL


def norm_fwd(x, w):
    return rowwise("norm_fwd", lambda xv, wv: (_rms(xv, wv),), [(x, D, 0)], [(w, D, 0)], [(D, D, BF16)], [], tt=256)[0]


def norm_bwd(x, w, dh, dres):
    def fn(xv, dhv, drv, wv):
        _, vjp = jax.vjp(_rms, xv, wv)
        dx, dw = vjp(dhv)
        return dx + drv, dw
    return rowwise("norm_bwd", fn, [(x, D, 0), (dh, D, 0), (dres, D, 0)], [(w, D, 0)], [(D, D, F32)], [(1, D, D)], tt=256)


def _dt_fn(dtr, bias, a_log):
    c = dtr.shape[0]
    dt = _softplus(dtr + bias)
    da = dt * (-jnp.exp(a_log))
    tril = jnp.where(_iota((c, c), 1) <= _iota((c, c), 0), 1.0, 0.0).astype(F32)
    return dt, _hdot(tril, da)


def dt_fwd(dtr, bias, a_log):
    return rowwise("dt_fwd", _dt_fn, [(dtr, DT_PAD, 0)], [(bias, DT_PAD, 0), (a_log, DT_PAD, 0)],
                   [(DT_PAD, DT_PAD, F32), (DT_PAD, DT_PAD, F32)], [], tt=SSM_CHUNK)


def dt_bwd(dtr, bias, a_log, ddt, dacum):
    def fn(dtrv, ddtv, dacv, bv, av):
        _, vjp = jax.vjp(_dt_fn, dtrv, bv, av)
        return vjp((ddtv, dacv))
    return rowwise("dt_bwd", fn, [(dtr, DT_PAD, 0), (ddt, DT_PAD, 0), (dacum, DT_PAD, 0)],
                   [(bias, DT_PAD, 0), (a_log, DT_PAD, 0)],
                   [(DT_PAD, DT_PAD, BF16)], [(1, DT_PAD, DT_PAD), (1, DT_PAD, DT_PAD)], tt=SSM_CHUNK)


GROUP_W = SSM_DINNER // SSM_GROUPS


def _ssm_post_fn(yv, xsv, zv, dexp, nw):
    return _rms((yv + dexp * xsv) * _silu(zv), nw)


def ssm_post_fwd(yssd, xbc_act, z, dexp, nw):
    return rowwise("ssm_post_fwd", lambda *a: (_ssm_post_fn(*a),),
                   [(yssd, GROUP_W, 0), (xbc_act, GROUP_W, 0), (z, GROUP_W, 0)], [(dexp, GROUP_W, 0), (nw, GROUP_W, 0)],
                   [(SSM_DINNER, GROUP_W, BF16)], [], tt=512, ncb=SSM_GROUPS)[0]


def ssm_post_bwd(yssd, xbc_act, z, dexp, nw, dy):
    def fn(yv, xsv, zv, dyv, dv, nv):
        _, vjp = jax.vjp(_ssm_post_fn, yv, xsv, zv, dv, nv)
        return vjp(dyv)
    return rowwise("ssm_post_bwd", fn,
                   [(yssd, GROUP_W, 0), (xbc_act, GROUP_W, 0), (z, GROUP_W, 0), (dy, GROUP_W, 0)],
                   [(dexp, GROUP_W, 0), (nw, GROUP_W, 0)],
                   [(SSM_DINNER, GROUP_W, F32), (SSM_DINNER, GROUP_W, F32), (SSM_DINNER, GROUP_W, BF16)],
                   [(1, SSM_DINNER, GROUP_W), (1, SSM_DINNER, GROUP_W)], tt=512, ncb=SSM_GROUPS)


def _merge_fn(ah, asm, gh, gs):
    return _sigmoid(gh) * ah + _sigmoid(gs) * asm


def merge_fwd(a_hg, a_ssm, gates):
    return rowwise("merge_fwd", lambda *a: (_merge_fn(*a),), [(a_hg, D, 0), (a_ssm, D, 0), (gates, D, 0), (gates, D, 1)], [],
                   [(D, D, BF16)], [], tt=256)[0]


def merge_bwd(a_hg, a_ssm, gates, dmixed):
    def fn(ah, asm, gh, gs, dm):
        _, vjp = jax.vjp(_merge_fn, ah, asm, gh, gs)
        return vjp(dm)
    return rowwise("merge_bwd", fn, [(a_hg, D, 0), (a_ssm, D, 0), (gates, D, 0), (gates, D, 1), (dmixed, D, 0)], [],
                   [(D, D, BF16)] * 4, [], tt=256)


def _post1_fn(xv, uv, wpost, wpre):
    x1 = xv + _rms(uv, wpost)
    return x1, _rms(x1, wpre)


def post1_fwd(x, u, wpost, wpre):
    return rowwise("post1_fwd", _post1_fn, [(x, D, 0), (u, D, 0)], [(wpost, D, 0), (wpre, D, 0)],
                   [(D, D, F32), (D, D, BF16)], [], tt=256)


def post1_bwd(x, u, wpost, wpre, dx1, dh2):
    def fn(xv, uv, d1, d2, wa, wb):
        _, vjp = jax.vjp(_post1_fn, xv, uv, wa, wb)
        dx, du, dwa, dwb = vjp((d1, d2))
        return du, dx, dwa, dwb
    return rowwise("post1_bwd", fn, [(x, D, 0), (u, D, 0), (dx1, D, 0), (dh2, D, 0)], [(wpost, D, 0), (wpre, D, 0)],
                   [(D, D, BF16), (D, D, F32)], [(1, D, D), (1, D, D)], tt=256)


def final_fwd_bwd(x1, fo, w, target):
    def fn(x1v, fov, tv, wv):
        def loss_fn(a, b, c):
            err = a + _rms(b, c) - tv
            return 0.5 * jnp.sum(err * err) * (1.0 / D)
        loss, vjp = jax.vjp(loss_fn, x1v, fov, wv)
        dx, dfo, dw = vjp(jnp.ones((), F32))
        return dx, dfo, dw, jnp.full((1, 128), loss, F32)
    return rowwise("final_fwd_bwd", fn, [(x1, D, 0), (fo, D, 0), (target, D, 0)], [(w, D, 0)],
                   [(D, D, F32), (D, D, BF16)], [(1, D, D), (1, 128, 128)], tt=256)


HALO = 8
CONV_TT = 512
CONV_CB = 512


def _tail(kind, c, up):
    return _silu(c) if kind == "silu" else _gelu_tanh(c) * up


def conv_fwd(name, x, xoff, w, b, kind, up=None, upoff=0, act_dtype=F32):
    t = x.shape[0]
    k_, c_ = w.shape
    tt, cb = CONV_TT, CONV_CB
    hb = tt // HALO
    has_up = up is not None

    def body(*refs):
        if has_up:
            x_ref, xp_ref, w_ref, b_ref, up_ref, c_ref, a_ref, scr = refs
        else:
            x_ref, xp_ref, w_ref, b_ref, c_ref, a_ref, scr = refs
        i = pl.program_id(1)
        scr[0:HALO, :] = jnp.where(i == 0, 0.0, xp_ref[...])
        scr[HALO:HALO + tt, :] = x_ref[...]
        acc = jnp.zeros((tt, cb), F32) + b_ref[...]
        for k in range(k_):
            acc = acc + w_ref[k:k + 1, :] * scr[pl.ds(HALO - (k_ - 1) + k, tt), :]
        c_ref[...] = acc
        a_ref[...] = _tail(kind, acc, up_ref[...] if has_up else None).astype(act_dtype)

    in_specs = [pl.BlockSpec((tt, cb), lambda j, i: (i, xoff + j)),
                pl.BlockSpec((HALO, cb), lambda j, i: (jnp.maximum(i * hb - 1, 0), xoff + j)),
                pl.BlockSpec((k_, cb), lambda j, i: (0, j)),
                pl.BlockSpec((1, cb), lambda j, i: (0, j))]
    args = [x, x, w, b]
    if has_up:
        in_specs.append(pl.BlockSpec((tt, cb), lambda j, i: (i, upoff + j)))
        args.append(up)
    return pl.pallas_call(
        body, name=name, grid=(c_ // cb, t // tt), in_specs=in_specs,
        out_specs=[pl.BlockSpec((tt, cb), lambda j, i: (i, j))] * 2,
        out_shape=[jax.ShapeDtypeStruct((t, c_), F32), jax.ShapeDtypeStruct((t, c_), act_dtype)],
        scratch_shapes=[pltpu.VMEM((tt + HALO, cb), F32)],
        compiler_params=_cp(("parallel", "arbitrary")),
    )(*args)


def conv_bwd(name, x, xoff, c, coff, dact, w, kind, up=None, upoff=0):
    t = x.shape[0]
    k_, c_ = w.shape[0], dact.shape[1]
    tt, cb = CONV_TT, CONV_CB
    hb = tt // HALO
    nt = t // tt
    has_up = up is not None

    def tail_grad(cv, dav, upv):
        if has_up:
            _, vjp = jax.vjp(lambda a, u: _tail(kind, a, u), cv, upv)
            return vjp(dav)
        _, vjp = jax.vjp(lambda a: _tail(kind, a, None), cv)
        return vjp(dav)[0], None

    def body(*refs):
        if has_up:
            (x_ref, xp_ref, c_ref, cn_ref, da_ref, dan_ref, w_ref, up_ref, upn_ref,
             dx_ref, dup_ref, dw_ref, db_ref, xs, dcs) = refs
        else:
            x_ref, xp_ref, c_ref, cn_ref, da_ref, dan_ref, w_ref, dx_ref, dw_ref, db_ref, xs, dcs = refs
        i = pl.program_id(1)
        xs[0:HALO, :] = jnp.where(i == 0, 0.0, xp_ref[...])
        xs[HALO:HALO + tt, :] = x_ref[...]
        dc, dup = tail_grad(c_ref[...], da_ref[...].astype(F32), up_ref[...] if has_up else None)
        dcn, _ = tail_grad(cn_ref[...], dan_ref[...].astype(F32), upn_ref[...] if has_up else None)
        dcs[0:tt, :] = dc
        dcs[tt:tt + HALO, :] = jnp.where(i == nt - 1, 0.0, dcn)
        if has_up:
            dup_ref[...] = dup.astype(BF16)
        dx = jnp.zeros((tt, cb), F32)
        dws = []
        for k in range(k_):
            dx = dx + w_ref[k:k + 1, :] * dcs[pl.ds(k_ - 1 - k, tt), :]
            dws.append(_colsum(dc * xs[pl.ds(HALO - (k_ - 1) + k, tt), :]))
        dx_ref[...] = dx.astype(BF16)

        @pl.when(i == 0)
        def _():
            dw_ref[...] = jnp.zeros_like(dw_ref)
            db_ref[...] = jnp.zeros_like(db_ref)

        for k in range(k_):
            dw_ref[k:k + 1, :] += dws[k]
        db_ref[...] += _colsum(dc)

    tile = lambda off: pl.BlockSpec((tt, cb), lambda j, i, off=off: (i, off + j))
    prev = lambda off: pl.BlockSpec((HALO, cb), lambda j, i, off=off: (jnp.maximum(i * hb - 1, 0), off + j))
    nxt = lambda off: pl.BlockSpec((HALO, cb), lambda j, i, off=off: (jnp.minimum((i + 1) * hb, t // HALO - 1), off + j))
    in_specs = [tile(xoff), prev(xoff), tile(coff), nxt(coff), tile(0), nxt(0),
                pl.BlockSpec((k_, cb), lambda j, i: (0, coff + j))]
    args = [x, x, c, c, dact, dact, w]
    if has_up:
        in_specs += [tile(upoff), nxt(upoff)]
        args += [up, up]
    out_specs = [tile(0)] + ([tile(0)] if has_up else []) + [pl.BlockSpec((HALO, cb), lambda j, i: (0, j)),
                                                            pl.BlockSpec((1, cb), lambda j, i: (0, j))]
    out_shape = [jax.ShapeDtypeStruct((t, c_), BF16)] * (2 if has_up else 1)
    out_shape += [jax.ShapeDtypeStruct((HALO, c_), F32), jax.ShapeDtypeStruct((1, c_), F32)]
    return pl.pallas_call(
        body, name=name, grid=(c_ // cb, nt), in_specs=in_specs, out_specs=out_specs, out_shape=out_shape,
        scratch_shapes=[pltpu.VMEM((tt + HALO, cb), F32), pltpu.VMEM((tt + HALO, cb), F32)],
        compiler_params=_cp(("parallel", "arbitrary")),
    )(*args)


def ew_sum(name, parts, rows, out_dtype, tr):
    c = parts[0][0].shape[1]
    tr = min(tr, rows)
    assert rows % tr == 0 and all(off % tr == 0 for _, off in parts)
    n = len(parts)

    def body(*refs):
        acc = refs[0][...].astype(F32)
        for ref in refs[1:n]:
            acc = acc + ref[...].astype(F32)
        refs[n][...] = acc.astype(out_dtype)

    in_specs = [pl.BlockSpec((tr, c), lambda i, o=off // tr: (i + o, 0)) for _, off in parts]
    return pl.pallas_call(body, name=name, grid=(rows // tr,), in_specs=in_specs,
                          out_specs=pl.BlockSpec((tr, c), lambda i: (i, 0)),
                          out_shape=jax.ShapeDtypeStruct((rows, c), out_dtype),
                          compiler_params=_cp(("parallel",)))(*[a for a, _ in parts])


def fold_heads(dexp):
    def body(d_ref, o_ref):
        sel = jnp.where(_iota((SSM_DINNER, DT_PAD), 0) // SSM_HEADDIM == _iota((SSM_DINNER, DT_PAD), 1), 1.0, 0.0)
        o_ref[...] = _hdot(jnp.broadcast_to(d_ref[...], (8, SSM_DINNER)), sel.astype(F32), "nn", "a")[0:1, :]

    return pl.pallas_call(body, name="fold_heads", out_shape=jax.ShapeDtypeStruct((1, DT_PAD), F32),
                          compiler_params=pltpu.CompilerParams(vmem_limit_bytes=VMEM_LIMIT))(dexp)


def adamw(name, w, g, m, v, tr):
    r, c = w.shape
    tr = min(tr, r)
    assert r % tr == 0, (r, tr)

    def body(w_ref, g_ref, m_ref, v_ref, d_ref, nm_ref, nv_ref):
        gv = g_ref[...]
        nm = ADAM_B1 * m_ref[...] + (1.0 - ADAM_B1) * gv
        nv = ADAM_B2 * v_ref[...] + (1.0 - ADAM_B2) * (gv * gv)
        m_hat = nm / (1.0 - ADAM_B1 ** ADAM_STEP)
        v_hat = nv / (1.0 - ADAM_B2 ** ADAM_STEP)
        d_ref[...] = -ADAM_LR * (m_hat / (jnp.sqrt(v_hat) + ADAM_EPS) + ADAM_WD * w_ref[...])
        nm_ref[...] = nm
        nv_ref[...] = nv

    spec = pl.BlockSpec((tr, c), lambda i: (i, 0))
    shp = jax.ShapeDtypeStruct((r, c), F32)
    return pl.pallas_call(body, name=name, grid=(r // tr,), in_specs=[spec] * 4, out_specs=[spec] * 3,
                          out_shape=[shp] * 3, compiler_params=_cp(("parallel",)))(w, g, m, v)


SEG_QFIG, SEG_Z, SEG_XBC, SEG_DT, SEG_G = 0, 8192, 12288, 18432, 18496
IN_TOTAL = 22592
FFN_BLOCKS = D_FF // CONV_CB


def local_step(x, target, wts, par):
    t = x.shape[0]
    pad64 = lambda a: jnp.pad(a, ((0, 0), (0, DT_PAD - a.shape[1])))
    bias, a_log = pad64(par["ssm_dt_bias"]), pad64(par["ssm_A_log"])
    dexp = jnp.repeat(par["ssm_D"], SSM_HEADDIM, axis=1)
    in_t = wts["in_t"]

    h = norm_fwd(x, par["mix_pre_norm"])
    proj = lambda nm, off, n, tn: mm(h, in_t, "nt", name=nm, tn=tn, dims=(t, n, D), b_off=(off, 0))
    qfig = proj("proj_qfig", SEG_QFIG, 8192, 1024)
    z = proj("proj_z", SEG_Z, 4096, 1024)
    xbc = proj("proj_xbc", SEG_XBC, 6144, 1024)
    dtr = mm(h, wts["dt_t"], "nt", name="proj_dt", tn=128)
    gates = mm(h, wts["g_t"], "nt", name="proj_gates", tn=1024)
    y_hg, hg_states = hgrn2_fwd(qfig, par["hg_lb_table"], par["hg_out_norm"])
    c_ssm, xbc_act = conv_fwd("ssm_conv_fwd", xbc, 0, par["ssm_conv_w"], par["ssm_conv_b"], "silu")
    dt, acum = dt_fwd(dtr, bias, a_log)
    yssd, ssd_states = ssd_fwd(xbc_act, dt, acum)
    y_ssm = ssm_post_fwd(yssd, xbc_act, z, dexp, par["ssm_out_norm"])
    a_hg = mm(y_hg, wts["bh"], "nn", name="branch_hg", tn=1024)
    a_ssm = mm(y_ssm, wts["bs"], "nn", name="branch_ssm", tn=1024, tk=2048)
    mixed = merge_fwd(a_hg, a_ssm, gates)
    u = mm(mixed, wts["o"], "nn", name="out_proj", tn=1024)
    x1, h2 = post1_fwd(x, u, par["mix_post_norm"], par["ffn_pre_norm"])
    gu = mm(h2, wts["up"], "nn", name="ffn_up", tn=1024)
    c_ffn, act = conv_fwd("ffn_conv_fwd", gu, 0, par["ffn_conv_w"], par["ffn_conv_b"], "gelu_mul",
                          up=gu, upoff=FFN_BLOCKS, act_dtype=BF16)
    fo = mm(act, wts["dn"], "nn", name="ffn_down", tn=1024, tk=1408)
    dx2, dfo, g_ffn_post, loss = final_fwd_bwd(x1, fo, par["ffn_post_norm"], target)

    dact = mm(dfo, wts["dn"], "nt", name="d_act", tn=1408)
    g_dn = mm(act, dfo, "tn", name="g_ffn_down", out_dtype=BF16, tm=1408, tn=2048, tk=1024)
    dgate, dup, g_fcw, g_fcb = conv_bwd("ffn_conv_bwd", gu, 0, c_ffn, 0, dact, par["ffn_conv_w"], "gelu_mul",
                                        up=gu, upoff=FFN_BLOCKS)
    dh2 = mm(dgate, wts["up"], "nt", name="d_h2_gate", tn=1024, tk=1408, dims=(t, D, D_FF))
    dh2 = mm(dup, wts["up"], "nt", name="d_h2_up", tn=1024, tk=1408, dims=(t, D, D_FF), b_off=(0, D_FF), acc=dh2)
    g_up_gate = mm(h2, dgate, "tn", name="g_ffn_up_gate", out_dtype=BF16, tm=2048, tn=1408, tk=1024)
    g_up_up = mm(h2, dup, "tn", name="g_ffn_up_up", out_dtype=BF16, tm=2048, tn=1408, tk=1024)
    du, dx1, g_mix_post, g_ffn_pre = post1_bwd(x, u, par["mix_post_norm"], par["ffn_pre_norm"], dx2, dh2)
    dmixed = mm(du, wts["o"], "nt", name="d_mixed", tn=1024)
    g_o = mm(mixed, du, "tn", name="g_w_out", out_dtype=BF16, tm=1024, tn=2048, tk=1024)
    da_hg, da_ssm, dg_hg, dg_ssm = merge_bwd(a_hg, a_ssm, gates, dmixed)
    dy_hg = mm(da_hg, wts["bh"], "nt", name="d_y_hg", out_dtype=BF16, tn=1024)
    g_bh = mm(y_hg, da_hg, "tn", name="g_w_branch_hg", out_dtype=BF16, tm=1024, tn=2048, tk=1024)
    dy_ssm = mm(da_ssm, wts["bs"], "nt", name="d_y_ssm", tn=1024)
    g_bs = mm(y_ssm, da_ssm, "tn", name="g_w_branch_ssm", out_dtype=BF16, tm=1024, tn=2048, tk=1024)
    dyssd, dskip, dz, g_dexp, g_ssm_norm = ssm_post_bwd(yssd, xbc_act, z, dexp, par["ssm_out_norm"], dy_ssm)
    dxs, db_, dc_, ddt, dacum = ssd_bwd(xbc_act, dt, acum, ssd_states, dyssd, dskip)
    ddtr, g_dt_bias, g_a_log = dt_bwd(dtr, bias, a_log, ddt, dacum)
    xs_blocks, bc_blocks = SSM_DINNER // CONV_CB, SSM_GROUPS * SSM_DSTATE // CONV_CB
    dxbc_x, g_cw_x, g_cb_x = conv_bwd("ssm_conv_bwd_x", xbc, 0, c_ssm, 0, dxs, par["ssm_conv_w"], "silu")
    dxbc_b, g_cw_b, g_cb_b = conv_bwd("ssm_conv_bwd_b", xbc, xs_blocks, c_ssm, xs_blocks, db_, par["ssm_conv_w"], "silu")
    dxbc_c, g_cw_c, g_cb_c = conv_bwd("ssm_conv_bwd_c", xbc, xs_blocks + bc_blocks, c_ssm, xs_blocks + bc_blocks, dc_,
                                      par["ssm_conv_w"], "silu")
    dq, df, dv, dg, g_table, g_hg_norm = hgrn2_bwd(qfig, par["hg_lb_table"], par["hg_out_norm"], hg_states, dy_hg)

    dsegs = [(dq, SEG_QFIG), (df, SEG_QFIG + 2048), (dv, SEG_QFIG + 4096), (dg, SEG_QFIG + 6144), (dz, SEG_Z),
             (dxbc_x, SEG_XBC), (dxbc_b, SEG_XBC + SSM_DINNER), (dxbc_c, SEG_XBC + SSM_DINNER + 1024)]
    g_in_parts = [mm(dseg, h, "tn", name=f"g_w_in_{n}", out_dtype=BF16, tm=1024, tn=2048, tk=1024)
                  for n, (dseg, _) in enumerate(dsegs)]
    g_dt_t = mm(ddtr, h, "tn", name="g_w_in_dt", out_dtype=BF16, tm=128, tn=2048, tk=1024)[:SSM_HEADS]
    g_in_parts += [mm(dgate_, h, "tn", name=f"g_w_in_g{n}", out_dtype=BF16, tm=1024, tn=2048, tk=1024)
                   for n, dgate_ in enumerate((dg_hg, dg_ssm))]
    dh = mm_segments("d_h", [(dseg, 0, off) for dseg, off in dsegs] + [(ddtr, 1, 0), (dg_hg, 2, 0), (dg_ssm, 2, D)],
                     [in_t, wts["dt_t"], wts["g_t"]], tm=512, tn=1024, tk=1024)
    g_in_t = jnp.concatenate(g_in_parts[:8] + [g_dt_t] + g_in_parts[8:], axis=0)
    grad_x, g_mix_pre = norm_bwd(x, par["mix_pre_norm"], dh, dx1)

    big = dict(in_t=g_in_t, bh=g_bh, bs=g_bs, o=g_o, up_gate=g_up_gate, up_up=g_up_up, dn=g_dn)
    g_conv_w = jnp.concatenate([g_cw_x, g_cw_b, g_cw_c], axis=1)[:SSM_CONV]
    g_conv_b = jnp.concatenate([g_cb_x, g_cb_b, g_cb_c], axis=1)
    small = dict(mix_pre_norm=g_mix_pre, mix_post_norm=g_mix_post, hg_lb_table=g_table, hg_out_norm=g_hg_norm,
                 ssm_conv_w=g_conv_w, ssm_conv_b=g_conv_b, ssm_dt_bias=g_dt_bias, ssm_A_log=g_a_log,
                 ssm_D=g_dexp, ssm_out_norm=g_ssm_norm, ffn_pre_norm=g_ffn_pre, ffn_post_norm=g_ffn_post,
                 ffn_conv_w=g_fcw[:FFN_CONV], ffn_conv_b=g_fcb)
    return loss, grad_x, big, small


MESH = pl.DeviceIdType.MESH
ANY = pl.BlockSpec(memory_space=pl.ANY)
N_CHIPS = 4
PACK_ROWS = 9120
PACK_SPLITS = (5648, 6160, 7184, 7696, 9104)
UP_ROWS, UP_COLS = D_MODEL, 2816


def _place():
    x, y, c = lax.axis_index("x"), lax.axis_index("y"), lax.axis_index("c")
    chips = [(1 - x, y), (x, 1 - y), (1 - x, 1 - y)]
    return x, y, c, chips


def _rcopy(src, dst, send_sems, recv_sems, k, dev):
    return pltpu.make_async_remote_copy(src_ref=src, dst_ref=dst, send_sem=send_sems.at[k], recv_sem=recv_sems.at[k],
                                        device_id=dev, device_id_type=MESH)


def _pieces(rows, n):
    assert rows % n == 0 and (rows // n) % 16 == 0, (rows, n)
    return [(k * (rows // n), rows // n) for k in range(n)]


def _rows(c, hrows, piece):
    return pl.ds(pl.multiple_of(c * hrows + piece[0], 16), piece[1])


PACK_PIECES = 5
UP_PIECES = 4


def gather_weights(p1, p2):
    plan = [(p1.shape[0] // 2, _pieces(p1.shape[0] // 2, PACK_PIECES)), (p2.shape[0] // 2, _pieces(p2.shape[0] // 2, UP_PIECES))]
    n_sem = sum(2 * 3 * len(pcs) for _, pcs in plan)

    def body(p1_ref, p2_ref, g1_ref, g2_ref, send_sems, recv_sems):
        x, y, c, chips = _place()
        own = 2 * x + y
        sib = (x, y, 1 - c)
        refs = ((p1_ref, g1_ref), (p2_ref, g2_ref))
        first, arrive, passed, from_sib = [], [], [], []
        k = 0
        for (p, g), (hrows, pcs) in zip(refs, plan):
            for j, chip in enumerate(chips):
                theirs = 2 * chip[0] + chip[1]
                for pc in pcs:
                    mine, other = _rows(c, hrows, pc), _rows(1 - c, hrows, pc)
                    first.append(_rcopy(p.at[mine], g.at[own, mine], send_sems, recv_sems, k, (*chip, c)))
                    arrive.append(_rcopy(g.at[theirs, mine], g.at[theirs, mine], send_sems, recv_sems, k, (*chip, c)))
                    passed.append(_rcopy(g.at[theirs, mine], g.at[theirs, mine], send_sems, recv_sems, k + 1, sib))
                    from_sib.append(_rcopy(g.at[theirs, other], g.at[theirs, other], send_sems, recv_sems, k + 1, sib))
                    k += 2
        for cp in first:
            cp.start()
        for got, fw in zip(arrive, passed):
            got.wait_recv()
            fw.start()
        for cp in from_sib:
            cp.wait_recv()
        for cp in first + passed:
            cp.wait_send()

    return pl.pallas_call(
        body, name="gather_weights", in_specs=[ANY, ANY], out_specs=[ANY, ANY],
        out_shape=[jax.ShapeDtypeStruct((N_CHIPS,) + p1.shape, p1.dtype), jax.ShapeDtypeStruct((N_CHIPS,) + p2.shape, p2.dtype)],
        scratch_shapes=[pltpu.SemaphoreType.DMA((n_sem,)), pltpu.SemaphoreType.DMA((n_sem,))],
    )(p1, p2)


def pair_exchange(g1, g2):
    shapes = [(N_CHIPS, g.shape[1] // 2, g.shape[2]) for g in (g1, g2)]
    plan = [(shapes[0][1], _pieces(shapes[0][1], PACK_PIECES)), (shapes[1][1], _pieces(shapes[1][1], UP_PIECES))]
    n_sem = sum(N_CHIPS * len(pcs) for _, pcs in plan)

    def body(g1_ref, g2_ref, r1_ref, r2_ref, send_sems, recv_sems):
        x, y, c, _ = _place()
        sib = (x, y, 1 - c)
        cps = []
        for (g, r), (hrows, pcs) in zip(((g1_ref, r1_ref), (g2_ref, r2_ref)), plan):
            for s in range(N_CHIPS):
                for pc in pcs:
                    cps.append(_rcopy(g.at[s, _rows(1 - c, hrows, pc)], r.at[s, pl.ds(pc[0], pc[1])],
                                      send_sems, recv_sems, len(cps), sib))
        for cp in cps:
            cp.start()
        for cp in cps:
            cp.wait()

    return pl.pallas_call(
        body, name="pair_exchange", in_specs=[ANY, ANY], out_specs=[ANY] * 2,
        out_shape=[jax.ShapeDtypeStruct(shapes[0], g1.dtype), jax.ShapeDtypeStruct(shapes[1], g2.dtype)],
        scratch_shapes=[pltpu.SemaphoreType.DMA((n_sem,)), pltpu.SemaphoreType.DMA((n_sem,))],
    )(g1, g2)


def chip_exchange(s1, s2):
    def body(s1_ref, s2_ref, r1_ref, r2_ref, send_sems, recv_sems):
        x, y, c, chips = _place()
        cps = []
        for s, r in ((s1_ref, r1_ref), (s2_ref, r2_ref)):
            for j, chip in enumerate(chips):
                cps.append(_rcopy(s.at[2 * chip[0] + chip[1]], r.at[j], send_sems, recv_sems, len(cps), (*chip, c)))
        for cp in cps:
            cp.start()
        for cp in cps:
            cp.wait()

    return pl.pallas_call(
        body, name="chip_exchange", in_specs=[ANY, ANY], out_specs=[ANY] * 2,
        out_shape=[jax.ShapeDtypeStruct((3,) + s.shape[1:], s.dtype) for s in (s1, s2)],
        scratch_shapes=[pltpu.SemaphoreType.DMA((6,)), pltpu.SemaphoreType.DMA((6,))],
    )(s1, s2)


def pair_assemble(r1, r2):
    plan = [(r1.shape[0], _pieces(r1.shape[0], 3 * PACK_PIECES)), (r2.shape[0], _pieces(r2.shape[0], 2 * UP_PIECES))]
    n_sem = sum(len(pcs) for _, pcs in plan)

    def body(r1_ref, r2_ref, f1_ref, f2_ref, send_sems, recv_sems):
        x, y, c, _ = _place()
        sib = (x, y, 1 - c)
        cps, got = [], []
        for (r, f), (hrows, pcs) in zip(((r1_ref, f1_ref), (r2_ref, f2_ref)), plan):
            for pc in pcs:
                src = r.at[pl.ds(pc[0], pc[1])]
                cps.append(_rcopy(src, f.at[_rows(c, hrows, pc)], send_sems, recv_sems, len(cps), sib))
                got.append(_rcopy(src, f.at[_rows(1 - c, hrows, pc)], send_sems, recv_sems, len(got), sib))
        for cp in cps:
            cp.start()
        for cp in got:
            cp.wait_recv()
        for cp in cps:
            cp.wait_send()

    return pl.pallas_call(
        body, name="pair_assemble", in_specs=[ANY, ANY], out_specs=[ANY, ANY],
        out_shape=[jax.ShapeDtypeStruct((2 * r.shape[0], r.shape[1]), r.dtype) for r in (r1, r2)],
        scratch_shapes=[pltpu.SemaphoreType.DMA((n_sem,)), pltpu.SemaphoreType.DMA((n_sem,))],
    )(r1, r2)


N_DEV = 8


def gather_small(blk, reduce):
    rows, cols = blk.shape

    def body(x_ref, out_ref, all_ref, send_sems, recv_sems, local_sem):
        x, y, c, chips = _place()
        me, sib = (x, y, c), (x, y, 1 - c)

        def blk_rows(px, py, pc):
            return all_ref.at[pl.ds(pl.multiple_of((4 * px + 2 * py + pc) * rows, 8), rows), :]

        def copy(k, block, to, src=None):
            return _rcopy(blk_rows(*block) if src is None else src, blk_rows(*block), send_sems, recv_sems, k, to)

        mine = pltpu.make_async_copy(x_ref, blk_rows(*me), local_sem)
        mine.start()
        first = [copy(0, me, sib, src=x_ref)] + [copy(1 + j, me, (*chip, c), src=x_ref) for j, chip in enumerate(chips)]
        for cp in first:
            cp.start()
        passed = [copy(4 + j, (*chip, c), sib) for j, chip in enumerate(chips)]
        for j, chip in enumerate(chips):
            copy(1 + j, (*chip, c), me).wait_recv()
            passed[j].start()
        copy(0, sib, me).wait_recv()
        for j, chip in enumerate(chips):
            copy(4 + j, (*chip, 1 - c), me).wait_recv()
        for cp in first + passed:
            cp.wait_send()
        mine.wait()
        if reduce:
            acc = all_ref[0:rows, :]
            for d in range(1, N_DEV):
                acc = acc + all_ref[d * rows:(d + 1) * rows, :]
            out_ref[...] = acc
        else:
            out_ref[...] = all_ref[...]

    vmem = pl.BlockSpec(memory_space=pltpu.VMEM)
    return pl.pallas_call(
        body, name="reduce_small" if reduce else "gather_small", in_specs=[vmem], out_specs=vmem,
        out_shape=jax.ShapeDtypeStruct((rows if reduce else N_DEV * rows, cols), blk.dtype),
        scratch_shapes=[pltpu.VMEM((N_DEV * rows, cols), blk.dtype), pltpu.SemaphoreType.DMA((7,)),
                        pltpu.SemaphoreType.DMA((7,)), pltpu.SemaphoreType.DMA],
        compiler_params=pltpu.CompilerParams(vmem_limit_bytes=VMEM_LIMIT),
    )(blk)


WEIGHTS = ['w_in', 'mix_pre_norm', 'mix_post_norm', 'hg_lb_table', 'hg_out_norm', 'ssm_conv_w', 'ssm_conv_b',
           'ssm_dt_bias', 'ssm_A_log', 'ssm_D', 'ssm_out_norm', 'w_branch_hg', 'w_branch_ssm', 'w_out', 'ffn_pre_norm',
           'ffn_post_norm', 'ffn_w_up', 'ffn_conv_w', 'ffn_conv_b', 'ffn_w_down']
BIG = ('w_in', 'w_branch_hg', 'w_branch_ssm', 'w_out', 'ffn_w_up', 'ffn_w_down')
SMALL = tuple(n for n in WEIGHTS if n not in BIG)
CONV_SHARD = {'ssm_conv_w': SSM_CONV_DIM // N_CHIPS, 'ffn_conv_w': D_FF // N_CHIPS}
LANES = 128


def _pack(parts):
    flat = jnp.concatenate([p.reshape(-1) for p in parts])
    n = flat.shape[0]
    rows = -(-n // (8 * LANES)) * 8
    return jnp.pad(flat, (0, rows * LANES - n)).reshape(rows, LANES)


def _unpack(packed, shapes):
    flat = packed.reshape(-1)
    out, off = [], 0
    for s in shapes:
        n = int(np.prod(s))
        out.append(flat[off:off + n].reshape(s))
        off += n
    return out


def kernel(x, w_in, mix_pre_norm, mix_post_norm, hg_lb_table, hg_out_norm, ssm_conv_w, ssm_conv_b, ssm_dt_bias, ssm_A_log, ssm_D, ssm_out_norm, w_branch_hg, w_branch_ssm, w_out, ffn_pre_norm, ffn_post_norm, ffn_w_up, ffn_conv_w, ffn_conv_b, ffn_w_down, loss_target, m_w_in, m_mix_pre_norm, m_mix_post_norm, m_hg_lb_table, m_hg_out_norm, m_ssm_conv_w, m_ssm_conv_b, m_ssm_dt_bias, m_ssm_A_log, m_ssm_D, m_ssm_out_norm, m_w_branch_hg, m_w_branch_ssm, m_w_out, m_ffn_pre_norm, m_ffn_post_norm, m_ffn_w_up, m_ffn_conv_w, m_ffn_conv_b, m_ffn_w_down, v_w_in, v_mix_pre_norm, v_mix_post_norm, v_hg_lb_table, v_hg_out_norm, v_ssm_conv_w, v_ssm_conv_b, v_ssm_dt_bias, v_ssm_A_log, v_ssm_D, v_ssm_out_norm, v_w_branch_hg, v_w_branch_ssm, v_w_out, v_ffn_pre_norm, v_ffn_post_norm, v_ffn_w_up, v_ffn_conv_w, v_ffn_conv_b, v_ffn_w_down):
    w = dict(w_in=w_in, mix_pre_norm=mix_pre_norm, mix_post_norm=mix_post_norm, hg_lb_table=hg_lb_table, hg_out_norm=hg_out_norm, ssm_conv_w=ssm_conv_w, ssm_conv_b=ssm_conv_b, ssm_dt_bias=ssm_dt_bias, ssm_A_log=ssm_A_log, ssm_D=ssm_D, ssm_out_norm=ssm_out_norm, w_branch_hg=w_branch_hg, w_branch_ssm=w_branch_ssm, w_out=w_out, ffn_pre_norm=ffn_pre_norm, ffn_post_norm=ffn_post_norm, ffn_w_up=ffn_w_up, ffn_conv_w=ffn_conv_w, ffn_conv_b=ffn_conv_b, ffn_w_down=ffn_w_down)
    m = dict(w_in=m_w_in, mix_pre_norm=m_mix_pre_norm, mix_post_norm=m_mix_post_norm, hg_lb_table=m_hg_lb_table, hg_out_norm=m_hg_out_norm, ssm_conv_w=m_ssm_conv_w, ssm_conv_b=m_ssm_conv_b, ssm_dt_bias=m_ssm_dt_bias, ssm_A_log=m_ssm_A_log, ssm_D=m_ssm_D, ssm_out_norm=m_ssm_out_norm, w_branch_hg=m_w_branch_hg, w_branch_ssm=m_w_branch_ssm, w_out=m_w_out, ffn_pre_norm=m_ffn_pre_norm, ffn_post_norm=m_ffn_post_norm, ffn_w_up=m_ffn_w_up, ffn_conv_w=m_ffn_conv_w, ffn_conv_b=m_ffn_conv_b, ffn_w_down=m_ffn_w_down)
    v = dict(w_in=v_w_in, mix_pre_norm=v_mix_pre_norm, mix_post_norm=v_mix_post_norm, hg_lb_table=v_hg_lb_table, hg_out_norm=v_hg_out_norm, ssm_conv_w=v_ssm_conv_w, ssm_conv_b=v_ssm_conv_b, ssm_dt_bias=v_ssm_dt_bias, ssm_A_log=v_ssm_A_log, ssm_D=v_ssm_D, ssm_out_norm=v_ssm_out_norm, w_branch_hg=v_w_branch_hg, w_branch_ssm=v_w_branch_ssm, w_out=v_w_out, ffn_pre_norm=v_ffn_pre_norm, ffn_post_norm=v_ffn_post_norm, ffn_w_up=v_ffn_w_up, ffn_conv_w=v_ffn_conv_w, ffn_conv_b=v_ffn_conv_b, ffn_w_down=v_ffn_w_down)
    shard = 2 * lax.axis_index("x") + lax.axis_index("y")
    bf = lambda a: a.astype(BF16)

    p1 = jnp.concatenate([bf(w_in[0].T), bf(w_branch_hg[0]), bf(w_branch_ssm[0]), bf(w_out[0]), bf(ffn_w_down[0]),
                          jnp.zeros((PACK_ROWS - PACK_SPLITS[-1], D_MODEL), BF16)], axis=0)
    core = lax.axis_index("c")
    p2 = bf(ffn_w_up[0])
    g1, g2 = gather_weights(p1, p2)
    g1 = lax.dynamic_update_slice(g1, p1[None], (shard, 0, 0))
    g2 = lax.dynamic_update_slice(g2, p2[None], (shard, 0, 0))
    s0, s1, s2, s3, s4 = PACK_SPLITS
    in_t = g1[:, :s0].reshape(IN_TOTAL, D_MODEL)
    wts = dict(in_t=in_t, g_t=in_t[SEG_G:], dt_t=jnp.pad(in_t[SEG_DT:SEG_G], ((0, DT_PAD - SSM_HEADS), (0, 0))),
               bh=g1[:, s0:s1].reshape(-1, D_MODEL), bs=g1[:, s1:s2].reshape(-1, D_MODEL),
               o=g1[:, s2:s3].reshape(-1, D_MODEL), dn=g1[:, s3:s4].reshape(-1, D_MODEL),
               up=jnp.transpose(g2, (1, 0, 2)).reshape(D_MODEL, 2 * D_FF))
    conv_cols = max(CONV_SHARD.values())
    padc = lambda a: jnp.pad(a, ((0, 0), (0, conv_cols - a.shape[1])))
    conv_blk = jnp.concatenate([padc(ssm_conv_w[0]), padc(ffn_conv_w[0]), jnp.zeros((1, conv_cols), F32)], axis=0)
    conv_all = gather_small(conv_blk, reduce=False)
    par = {n: w[n] for n in SMALL}
    par["ssm_conv_w"] = jnp.concatenate([conv_all[16 * s:16 * s + SSM_CONV, :CONV_SHARD['ssm_conv_w']] for s in range(N_CHIPS)], axis=1)
    par["ffn_conv_w"] = jnp.concatenate([conv_all[16 * s + SSM_CONV:16 * s + SSM_CONV + FFN_CONV, :CONV_SHARD['ffn_conv_w']]
                                         for s in range(N_CHIPS)], axis=1)

    loss, grad_x, big, small = local_step(x[0], loss_target[0], wts, par)
    loss = lax.psum(loss[0, 0], ("x", "y", "c"))

    zpad = jnp.zeros((N_CHIPS, PACK_ROWS - PACK_SPLITS[-1], D_MODEL), BF16)
    gg1 = jnp.concatenate([big[k].reshape(N_CHIPS, -1, D_MODEL) for k in ("in_t", "bh", "bs", "o", "dn")] + [zpad], axis=1)
    gup = jnp.concatenate([big["up_gate"], big["up_up"]], axis=1)
    gg2 = jnp.transpose(gup.reshape(D_MODEL, N_CHIPS, UP_COLS), (1, 0, 2))
    r1, r2 = pair_exchange(gg1, gg2)
    h1, h2 = PACK_ROWS // 2, UP_ROWS // 2
    o1 = lax.dynamic_slice_in_dim(gg1, core * h1, h1, axis=1)
    o2 = lax.dynamic_slice_in_dim(gg2, core * h2, h2, axis=1)
    flat = lambda a: a.reshape(-1, a.shape[-1])
    c1 = ew_sum("pair_sum_1", [(flat(o1), 0), (flat(r1), 0)], N_CHIPS * h1, BF16, 480).reshape(N_CHIPS, h1, D_MODEL)
    c2 = ew_sum("pair_sum_2", [(flat(o2), 0), (flat(r2), 0)], N_CHIPS * h2, BF16, 512).reshape(N_CHIPS, h2, UP_COLS)
    rb1, rb2 = chip_exchange(c1, c2)
    ob1 = lax.dynamic_index_in_dim(c1, shard, axis=0, keepdims=False)
    ob2 = lax.dynamic_index_in_dim(c2, shard, axis=0, keepdims=False)
    red1 = ew_sum("chip_sum_1", [(ob1, 0)] + [(flat(rb1), j * h1) for j in range(3)], h1, F32, 240)
    red2 = ew_sum("chip_sum_2", [(ob2, 0)] + [(flat(rb2), j * h2) for j in range(3)], h2, F32, 256)
    f1, f2 = pair_assemble(red1, red2)
    f1 = lax.dynamic_update_slice(f1, red1, (core * h1, 0))
    f2 = lax.dynamic_update_slice(f2, red2, (core * h2, 0))
    grads = dict(w_in=f1[:s0].T, w_branch_hg=f1[s0:s1], w_branch_ssm=f1[s1:s2], w_out=f1[s2:s3], ffn_w_down=f1[s3:s4],
                 ffn_w_up=f2)

    small["hg_out_norm"] = ew_sum("sum_heads", [(small["hg_out_norm"][hd], 0) for hd in range(HG_HEADS)], 1, F32, 1)
    small["ssm_D"] = fold_heads(small["ssm_D"])[:, :SSM_HEADS]
    small["ssm_dt_bias"] = small["ssm_dt_bias"][:, :SSM_HEADS]
    small["ssm_A_log"] = small["ssm_A_log"][:, :SSM_HEADS]
    shapes = [small[n].shape for n in SMALL]
    summed = _unpack(gather_small(_pack([small[n] for n in SMALL]), reduce=True), shapes)
    for n, g in zip(SMALL, summed):
        if n in CONV_SHARD:
            g = lax.dynamic_slice_in_dim(g, shard * CONV_SHARD[n], CONV_SHARD[n], axis=1)
        grads[n] = g

    two_d = lambda a: a.reshape(a.shape[-2], a.shape[-1])
    delta, new_m, new_v = {}, {}, {}
    for n, tr in (("w_in", 64), ("w_branch_hg", 128), ("w_branch_ssm", 128), ("w_out", 128), ("ffn_w_up", 128), ("ffn_w_down", 128)):
        delta[n], new_m[n], new_v[n] = adamw("adamw_" + n, two_d(w[n]), grads[n], two_d(m[n]), two_d(v[n]), tr)
    sm_shapes = [two_d(w[n]).shape for n in SMALL]
    packed = adamw("adamw_small", _pack([two_d(w[n]) for n in SMALL]), _pack([grads[n] for n in SMALL]),
                   _pack([two_d(m[n]) for n in SMALL]), _pack([two_d(v[n]) for n in SMALL]), 1024)
    for res, packed_res in zip((delta, new_m, new_v), packed):
        for n, a in zip(SMALL, _unpack(packed_res, sm_shapes)):
            res[n] = a
    shaped = lambda d: [d[n].reshape(w[n].shape) for n in WEIGHTS]
    return (loss, grad_x[None], *shaped(grads), *shaped(delta), *shaped(new_m), *shaped(new_v))
```

```python
import functools

import jax
import jax.numpy as jnp
import numpy as np
from jax import lax
from jax.experimental import pallas as pl
from jax.experimental.pallas import tpu as pltpu

F32 = jnp.float32
BF16 = jnp.bfloat16

D_MODEL = 2048
EPS = 1e-6
HG_HEADS = 16
HG_DK = 128
HG_CHUNK = 64
HG_SUB = 16
SSM_DINNER = 4096
SSM_HEADDIM = 64
SSM_HEADS = 64
SSM_GROUPS = 8
SSM_DSTATE = 128
SSM_CONV = 4
SSM_CHUNK = 256
SSM_CONV_DIM = 6144
D_FF = 5632
FFN_CONV = 3
DT_PAD = 128

ADAM_LR = 0.001
ADAM_B1 = 0.9
ADAM_B2 = 0.999
ADAM_EPS = 1e-08
ADAM_WD = 0.01
ADAM_STEP = 10

VMEM_LIMIT = 56 * 1024 * 1024
HI = lax.Precision.HIGHEST


def _cp(sem, **kw):
    return pltpu.CompilerParams(dimension_semantics=sem, vmem_limit_bytes=VMEM_LIMIT, **kw)


_DIMS = {"nn": (((1,), (0,)), ((), ())), "nt": (((1,), (1,)), ((), ())), "tn": (((0,), (0,)), ((), ()))}


def mm(a, b, mode, *, name, out_dtype=F32, tm=512, tn=512, tk=None, acc=None, n_major=True,
       dims=None, a_off=(0, 0), b_off=(0, 0)):
    if dims is not None:
        M, N, K = dims
    else:
        if mode == "nn":
            (M, K), (K2, N) = a.shape, b.shape
        elif mode == "nt":
            (M, K), (N, K2) = a.shape, b.shape
        else:
            (K, M), (K2, N) = a.shape, b.shape
        assert K == K2, (a.shape, b.shape, mode)
    tm, tn = min(tm, M), min(tn, N)
    tk = K if tk is None else min(tk, K)
    assert M % tm == 0 and N % tn == 0 and K % tk == 0, (M, N, K, tm, tn, tk)
    a_blk = (tk, tm) if mode == "tn" else (tm, tk)
    b_blk = (tn, tk) if mode == "nt" else (tk, tn)
    assert all(o % s == 0 for o, s in zip(a_off, a_blk)) and all(o % s == 0 for o, s in zip(b_off, b_blk))
    ao0, ao1 = a_off[0] // a_blk[0], a_off[1] // a_blk[1]
    bo0, bo1 = b_off[0] // b_blk[0], b_off[1] // b_blk[1]
    nk = K // tk
    if n_major:
        grid = (N // tn, M // tm, nk)
        ij = lambda p0, p1: (p1, p0)
    else:
        grid = (M // tm, N // tn, nk)
        ij = lambda p0, p1: (p0, p1)

    def a_map(p0, p1, k):
        i, _ = ij(p0, p1)
        return (k + ao0, i + ao1) if mode == "tn" else (i + ao0, k + ao1)

    def b_map(p0, p1, k):
        _, j = ij(p0, p1)
        return (j + bo0, k + bo1) if mode == "nt" else (k + bo0, j + bo1)

    def o_map(p0, p1, k):
        return ij(p0, p1)

    a_spec = pl.BlockSpec(a_blk, a_map)
    b_spec = pl.BlockSpec(b_blk, b_map)
    o_spec = pl.BlockSpec((tm, tn), o_map)
    dims = _DIMS[mode]
    has_acc = acc is not None

    def body(*refs):
        if has_acc:
            a_ref, b_ref, c_ref, o_ref, acc_ref = refs
        else:
            a_ref, b_ref, o_ref, acc_ref = refs
        k = pl.program_id(2)
        part = lax.dot_general(a_ref[...], b_ref[...], dims, preferred_element_type=F32)

        @pl.when(k == 0)
        def _():
            acc_ref[...] = part

        @pl.when(k > 0)
        def _():
            acc_ref[...] += part

        @pl.when(k == nk - 1)
        def _():
            r = acc_ref[...]
            if has_acc:
                r = r + c_ref[...].astype(F32)
            o_ref[...] = r.astype(out_dtype)

    in_specs = [a_spec, b_spec] + ([o_spec] if has_acc else [])
    args = (a, b) + ((acc,) if has_acc else ())
    return pl.pallas_call(
        body, name=name, grid=grid, in_specs=in_specs, out_specs=o_spec,
        out_shape=jax.ShapeDtypeStruct((M, N), out_dtype),
        scratch_shapes=[pltpu.VMEM((tm, tn), F32)],
        compiler_params=_cp(("parallel", "parallel", "arbitrary")),
    )(*args)


def mm_segments(name, segs, bs, *, tm, tn, tk):
    m_, n_ = segs[0][0].shape[0], bs[0].shape[1]
    steps, k0 = [], 0
    for a, bi, row in segs:
        w = a.shape[1]
        tks = min(tk, w)
        assert w % tks == 0 and row % tks == 0 and tks == min(tk, bs[bi].shape[0]), (w, row, tks)
        steps.append((k0, w // tks, tks, bi, row // tks))
        k0 += w // tks
    nk = k0
    assert m_ % tm == 0 and n_ % tn == 0

    def a_spec(k_first, count, tks):
        return pl.BlockSpec((tm, tks), lambda j, i, k: (i, jnp.clip(k - k_first, 0, count - 1)))

    def b_spec(bi):
        mine = [s for s in steps if s[3] == bi]

        def index(j, i, k):
            blk = mine[0][4]
            for k_first, count, _, _, first_blk in mine:
                blk = jnp.where(k >= k_first, first_blk + jnp.minimum(k - k_first, count - 1), blk)
            return (blk, j)
        return pl.BlockSpec((mine[0][2], tn), index)

    ns = len(segs)

    def body(*refs):
        a_refs, b_refs, o_ref, acc_ref = refs[:ns], refs[ns:ns + len(bs)], refs[-2], refs[-1]
        k = pl.program_id(2)

        @pl.when(k == 0)
        def _():
            acc_ref[...] = jnp.zeros_like(acc_ref)

        for a_ref, (k_first, count, _, bi, _) in zip(a_refs, steps):
            @pl.when((k >= k_first) & (k < k_first + count))
            def _(a_ref=a_ref, bi=bi):
                acc_ref[...] += jnp.dot(a_ref[...], b_refs[bi][...], preferred_element_type=F32)

        @pl.when(k == nk - 1)
        def _():
            o_ref[...] = acc_ref[...]

    return pl.pallas_call(
        body, name=name, grid=(n_ // tn, m_ // tm, nk),
        in_specs=[a_spec(s[0], s[1], s[2]) for s in steps] + [b_spec(bi) for bi in range(len(bs))],
        out_specs=pl.BlockSpec((tm, tn), lambda j, i, k: (i, j)),
        out_shape=jax.ShapeDtypeStruct((m_, n_), F32),
        scratch_shapes=[pltpu.VMEM((tm, tn), F32)],
        compiler_params=_cp(("parallel", "parallel", "arbitrary")),
    )(*[a for a, _, _ in segs], *bs)


def _bdot_plain(a, b, mode):
    return lax.dot_general(a.astype(BF16), b.astype(BF16), _DIMS[mode], preferred_element_type=F32)


@functools.partial(jax.custom_vjp, nondiff_argnums=(2,))
def _bdot_vjp(a, b, mode):
    return _bdot_plain(a, b, mode)


def _bdot_fwd(a, b, mode):
    return _bdot_plain(a, b, mode), (a, b)


def _bdot_bwd(mode, res, g):
    a, b = res
    if mode == "nn":
        return _bdot_plain(g, b, "nt"), _bdot_plain(a, g, "tn")
    if mode == "nt":
        return _bdot_plain(g, b, "nn"), _bdot_plain(g, a, "tn")
    return _bdot_plain(b, g, "nt"), _bdot_plain(a, g, "nn")


_bdot_vjp.defvjp(_bdot_fwd, _bdot_bwd)


def _split3(x):
    x1 = x.astype(BF16)
    r1 = x - x1.astype(F32)
    x2 = r1.astype(BF16)
    return x1, x2, (r1 - x2.astype(F32)).astype(BF16)


def _hdot_impl(a, b, mode, data):
    if data == "a":
        sel = b.astype(BF16)
        parts = [lax.dot_general(p, sel, _DIMS[mode], preferred_element_type=F32) for p in _split3(a)]
    else:
        sel = a.astype(BF16)
        parts = [lax.dot_general(sel, p, _DIMS[mode], preferred_element_type=F32) for p in _split3(b)]
    return (parts[2] + parts[1]) + parts[0]


@functools.partial(jax.custom_vjp, nondiff_argnums=(2, 3))
def _hdot(a, b, mode="nn", data="b"):
    return _hdot_impl(a, b, mode, data)


def _hdot_fwd(a, b, mode, data):
    return _hdot_impl(a, b, mode, data), (a, b)


def _hdot_bwd(mode, data, res, g):
    a, b = res
    if data == "a":
        da = {"nn": lambda: _hdot_impl(g, b, "nt", "a"), "nt": lambda: _hdot_impl(g, b, "nn", "a"),
              "tn": lambda: _hdot_impl(b, g, "nt", "b")}[mode]()
        return da, jnp.zeros_like(b)
    db = {"nn": lambda: _hdot_impl(a, g, "tn", "b"), "nt": lambda: _hdot_impl(g, a, "tn", "a"),
          "tn": lambda: _hdot_impl(a, g, "nn", "b")}[mode]()
    return jnp.zeros_like(a), db


_hdot.defvjp(_hdot_fwd, _hdot_bwd)


def _sigmoid(x):
    return 1.0 / (1.0 + jnp.exp(-x))


def _silu(x):
    return x * _sigmoid(x)


def _iota(shape, dim):
    return lax.broadcasted_iota(jnp.int32, shape, dim)


def _rms(x, w):
    return x * lax.rsqrt(jnp.mean(x * x, axis=-1, keepdims=True) + EPS) * w


def _hg_chunk(q_raw, f_raw, v, g, st, t0, t1, nw, dot):
    c = q_raw.shape[0]
    m = jnp.maximum(t0, t1)
    e0, e1 = jnp.exp(t0 - m), jnp.exp(t1 - m)
    lb = e0 / (e0 + e1)
    f = lb + (1.0 - lb) * _sigmoid(f_raw)
    k = 1.0 - f
    lf = jnp.log(f)
    qh = _silu(q_raw) * (HG_DK ** -0.5)
    row, col = _iota((c, c), 0), _iota((c, c), 1)
    causal = col <= row
    tril = jnp.where(causal, 1.0, 0.0).astype(F32)
    trilb = jnp.where(causal & (col // HG_SUB == row // HG_SUB), 1.0, 0.0).astype(F32)
    b = _hdot(tril, lf)
    bl = _hdot(trilb, lf)
    a_row = b - bl
    rid = _iota((c, HG_DK), 0)
    qt = qh * jnp.exp(bl)
    kt = k * jnp.exp(-bl)
    scores = jnp.zeros((c, c), F32)
    for j in range(c // HG_SUB):
        if j == 0:
            qj = qt * jnp.exp(jnp.minimum(a_row, 0.0))
        else:
            a_j = jnp.sum(jnp.where(rid == j * HG_SUB - 1, b, 0.0), axis=0, keepdims=True)
            qj = qt * jnp.exp(jnp.minimum(a_row - a_j, 0.0))
        kj = jnp.where(rid // HG_SUB == j, kt, 0.0)
        scores = scores + dot(qj, kj, "nt")
    scores = jnp.where(causal, scores, 0.0)
    o = dot(scores, v, "nn") + dot(qh * jnp.exp(b), st, "nt")
    b_last = jnp.sum(jnp.where(rid == c - 1, b, 0.0), axis=0, keepdims=True)
    st_new = st * jnp.exp(b_last) + dot(v, k * jnp.exp(b_last - b), "tn")
    y = _rms(o, nw) * _silu(g)
    return y, st_new


HG_HPS = 4
HG_W = HG_HPS * HG_DK


def hgrn2_fwd(qfig, table, nw, *, step_chunks=2, gather=None):
    t = qfig.shape[0]
    rows = HG_CHUNK * step_chunks
    nsteps = t // rows
    nh = HG_HEADS // HG_HPS
    op = _Gather(*gather) if gather else None
    ng = len(gather[0]) if gather else 0

    def body(*refs):
        q_ref, f_ref, v_ref, g_ref, tab_ref, nw_ref = refs[:6]
        p_refs = refs[6:6 + ng]
        y_ref, s_ref = refs[6 + ng:8 + ng]
        got_refs = refs[8 + ng:8 + 2 * ng]
        st_scr = refs[8 + 2 * ng]
        sems = refs[9 + 2 * ng:]
        first_step = (pl.program_id(0) == 0) & (pl.program_id(1) == 0)
        last_step = (pl.program_id(0) == nh - 1) & (pl.program_id(1) == nsteps - 1)
        if op:
            @pl.when(first_step)
            def _():
                op.start(p_refs, got_refs, *sems)

        @pl.when(pl.program_id(1) == 0)
        def _():
            st_scr[...] = jnp.zeros_like(st_scr)

        nwv = nw_ref[...]
        for c in range(step_chunks):
            sl = pl.ds(c * HG_CHUNK, HG_CHUNK)
            for hh in range(HG_HPS):
                ln = pl.ds(hh * HG_DK, HG_DK)
                st = st_scr[hh]
                s_ref[hh, c] = st
                y, st_new = _hg_chunk(q_ref[sl, ln], f_ref[sl, ln], v_ref[sl, ln], g_ref[sl, ln], st,
                                      tab_ref[0:1, ln], tab_ref[1:2, ln], nwv, _bdot_vjp)
                y_ref[sl, ln] = y.astype(BF16)
                st_scr[hh] = st_new

        if op:
            @pl.when(last_step)
            def _():
                op.finish(p_refs, got_refs, *sems)

    blk = lambda off: pl.BlockSpec((rows, HG_W), lambda h, c, off=off: (c, off + h))
    return pl.pallas_call(
        body, name="hgrn2_fwd", grid=(nh, nsteps),
        in_specs=[blk(0), blk(nh), blk(2 * nh), blk(3 * nh),
                  pl.BlockSpec((2, HG_W), lambda h, c: (0, h)), pl.BlockSpec((1, HG_DK), lambda h, c: (0, 0))] + [ANY] * ng,
        out_specs=[pl.BlockSpec((rows, HG_W), lambda h, c: (c, h)),
                   pl.BlockSpec((HG_HPS, step_chunks, HG_DK, HG_DK), lambda h, c: (h, c, 0, 0))] + [ANY] * ng,
        out_shape=[jax.ShapeDtypeStruct((t, HG_HEADS * HG_DK), BF16),
                   jax.ShapeDtypeStruct((HG_HEADS, t // HG_CHUNK, HG_DK, HG_DK), F32)] + (op.out_shape if op else []),
        scratch_shapes=[pltpu.VMEM((HG_HPS, HG_DK, HG_DK), F32)] + (_sem_pair(op.n_sem) if op else []),
        compiler_params=_cp(("arbitrary", "arbitrary")),
    )(qfig, qfig, qfig, qfig, table, nw, *(gather[0] if gather else ()))


def hgrn2_bwd(qfig, table, nw, states, dy, *, step_chunks=2):
    t = qfig.shape[0]
    rows = HG_CHUNK * step_chunks
    nsteps = t // rows
    nh = HG_HEADS // HG_HPS

    def body(q_ref, f_ref, v_ref, g_ref, tab_ref, nw_ref, s_ref, dy_ref,
             dq_ref, df_ref, dv_ref, dg_ref, dtab_ref, dnw_ref, dst_scr):
        @pl.when(pl.program_id(1) == 0)
        def _():
            dst_scr[...] = jnp.zeros_like(dst_scr)
            dtab_ref[...] = jnp.zeros_like(dtab_ref)
            dnw_ref[...] = jnp.zeros_like(dnw_ref)

        nwv = nw_ref[...]
        fn = functools.partial(_hg_chunk, dot=_bdot_vjp)
        for c in reversed(range(step_chunks)):
            sl = pl.ds(c * HG_CHUNK, HG_CHUNK)
            for hh in range(HG_HPS):
                ln = pl.ds(hh * HG_DK, HG_DK)
                _, vjp = jax.vjp(fn, q_ref[sl, ln], f_ref[sl, ln], v_ref[sl, ln], g_ref[sl, ln], s_ref[hh, c],
                                 tab_ref[0:1, ln], tab_ref[1:2, ln], nwv)
                dq, df, dv, dg, dst, dt0, dt1, dnw = vjp((dy_ref[sl, ln].astype(F32), dst_scr[hh]))
                dq_ref[sl, ln] = dq.astype(BF16)
                df_ref[sl, ln] = df.astype(BF16)
                dv_ref[sl, ln] = dv.astype(BF16)
                dg_ref[sl, ln] = dg.astype(BF16)
                dst_scr[hh] = dst
                dtab_ref[0:1, ln] += dt0
                dtab_ref[1:2, ln] += dt1
                dnw_ref[hh] += dnw

    rev = lambda c: nsteps - 1 - c
    blk = lambda off: pl.BlockSpec((rows, HG_W), lambda h, c, off=off: (rev(c), off + h))
    oblk = lambda: pl.BlockSpec((rows, HG_W), lambda h, c: (rev(c), h))
    d = HG_HEADS * HG_DK
    outs = pl.pallas_call(
        body, name="hgrn2_bwd", grid=(nh, nsteps),
        in_specs=[blk(0), blk(nh), blk(2 * nh), blk(3 * nh),
                  pl.BlockSpec((2, HG_W), lambda h, c: (0, h)), pl.BlockSpec((1, HG_DK), lambda h, c: (0, 0)),
                  pl.BlockSpec((HG_HPS, step_chunks, HG_DK, HG_DK), lambda h, c: (h, rev(c), 0, 0)),
                  pl.BlockSpec((rows, HG_W), lambda h, c: (rev(c), h))],
        out_specs=[oblk(), oblk(), oblk(), oblk(),
                   pl.BlockSpec((2, HG_W), lambda h, c: (0, h)),
                   pl.BlockSpec((HG_HPS, 1, HG_DK), lambda h, c: (h, 0, 0))],
        out_shape=[jax.ShapeDtypeStruct((t, d), BF16)] * 4
        + [jax.ShapeDtypeStruct((2, d), F32), jax.ShapeDtypeStruct((HG_HEADS, 1, HG_DK), F32)],
        scratch_shapes=[pltpu.VMEM((HG_HPS, HG_DK, HG_DK), F32)],
        compiler_params=_cp(("parallel", "arbitrary")),
    )(qfig, qfig, qfig, qfig, table, nw, states, dy)
    return outs


def _ssd_chunk(xs2, dt, acum, bm, cm, s2, pair, dot):
    c = xs2.shape[0]
    lane = _iota((DT_PAD, 128), 1)
    expand = jnp.where(_iota((DT_PAD, 128), 0) == 2 * pair + lane // SSM_HEADDIM, 1.0, 0.0).astype(F32)
    sel = jnp.where(_iota((8, DT_PAD), 1) == 2 * pair + _iota((8, DT_PAD), 0), 1.0, 0.0).astype(F32)
    sel = jnp.where(_iota((8, DT_PAD), 0) < 2, sel, 0.0)
    dtx = _hdot(dt, expand, "nn", "a")
    acol = _hdot(acum, expand, "nn", "a")
    arow8 = _hdot(sel, acum, "nt", "b")
    row, col = _iota((c, c), 0), _iota((c, c), 1)
    causal = col <= row
    cb = dot(cm, bm, "nt")
    x2 = xs2 * dtx
    lane_c = _iota((c, 128), 1)
    y = dot(cm, s2, "nn") * jnp.exp(acol)
    for r in range(2):
        head = (lane_c // SSM_HEADDIM) == r
        a_c = jnp.sum(jnp.where(head & (lane_c % SSM_HEADDIM == 0), acol, 0.0), axis=1, keepdims=True)
        a_r = jnp.sum(jnp.where(_iota((8, c), 0) == r, arow8, 0.0), axis=0, keepdims=True)
        decay = jnp.where(causal, jnp.exp(jnp.minimum(a_c - a_r, 0.0)), 0.0)
        y = y + dot(cb * decay, jnp.where(head, x2, 0.0), "nn")
    a_last = jnp.sum(jnp.where(_iota((c, 128), 0) == c - 1, acol, 0.0), axis=0, keepdims=True)
    s2_new = s2 * jnp.exp(a_last) + dot(bm, x2 * jnp.exp(a_last - acol), "tn")
    return y, s2_new


SSM_PAIRS = SSM_HEADS // 2
PAIRS_PER_GROUP = SSM_PAIRS // SSM_GROUPS
SSD_PPS = 4
SSD_W = SSD_PPS * 128
_XS_BLOCKS = SSM_DINNER // 128
_B_BLOCK0 = _XS_BLOCKS
_C_BLOCK0 = _XS_BLOCKS + SSM_GROUPS


def ssd_fwd(xbc_act, dt, acum):
    t = xbc_act.shape[0]
    nc = t // SSM_CHUNK
    c_ = SSM_CHUNK

    def body(xs_ref, b_ref, c_ref, dt_ref, ac_ref, y_ref, s_ref, s_scr):
        q = pl.program_id(1)
        for r in range(SSD_PPS):
            p = SSD_PPS * q + r
            ln = pl.ds(r * 128, 128)

            @pl.when(pl.program_id(0) == 0)
            def _():
                s_scr[p] = jnp.zeros((SSM_DSTATE, 128), F32)

            s2 = s_scr[p]
            s_ref[r] = s2
            y, s2_new = _ssd_chunk(xs_ref[:, ln], dt_ref[...], ac_ref[...], b_ref[...], c_ref[...], s2, p, _bdot_vjp)
            y_ref[:, ln] = y
            s_scr[p] = s2_new

    grp = lambda q: q // (PAIRS_PER_GROUP // SSD_PPS)
    return pl.pallas_call(
        body, name="ssd_fwd", grid=(nc, SSM_PAIRS // SSD_PPS),
        in_specs=[pl.BlockSpec((c_, SSD_W), lambda c, q: (c, q)),
                  pl.BlockSpec((c_, 128), lambda c, q: (c, _B_BLOCK0 + grp(q))),
                  pl.BlockSpec((c_, 128), lambda c, q: (c, _C_BLOCK0 + grp(q))),
                  pl.BlockSpec((c_, DT_PAD), lambda c, q: (c, 0)),
                  pl.BlockSpec((c_, DT_PAD), lambda c, q: (c, 0))],
        out_specs=[pl.BlockSpec((c_, SSD_W), lambda c, q: (c, q)),
                   pl.BlockSpec((None, SSD_PPS, SSM_DSTATE, 128), lambda c, q: (c, q, 0, 0))],
        out_shape=[jax.ShapeDtypeStruct((t, SSM_DINNER), F32),
                   jax.ShapeDtypeStruct((nc, SSM_PAIRS, SSM_DSTATE, 128), F32)],
        scratch_shapes=[pltpu.VMEM((SSM_PAIRS, SSM_DSTATE, 128), F32)],
        compiler_params=_cp(("arbitrary", "arbitrary")),
    )(xbc_act, xbc_act, xbc_act, dt, acum)


def ssd_bwd(xbc_act, dt, acum, states, dy, dskip, *, exchange=None):
    t = xbc_act.shape[0]
    nc = t // SSM_CHUNK
    c_ = SSM_CHUNK
    rev = lambda c: nc - 1 - c
    nq = SSM_PAIRS // SSD_PPS
    op = _ChipExchange(exchange) if exchange else None
    ne = len(exchange) if exchange else 0

    def body(*refs):
        xs_ref, b_ref, c_ref, dt_ref, ac_ref, s_ref, dy_ref, sk_ref = refs[:8]
        ex_refs = refs[8:8 + ne]
        dxs_ref, db_ref, dc_ref, ddt_ref, dac_ref = refs[8 + ne:13 + ne]
        got_refs = refs[13 + ne:13 + 2 * ne]
        ds_scr = refs[13 + 2 * ne]
        sems = refs[14 + 2 * ne:]
        q = pl.program_id(1)
        if op:
            @pl.when((pl.program_id(0) == 0) & (q == 0))
            def _():
                op.start(ex_refs, got_refs, *sems)

            @pl.when((pl.program_id(0) == nc - 1) & (q == nq - 1))
            def _():
                op.finish(ex_refs, got_refs, *sems)

        steps_per_group = PAIRS_PER_GROUP // SSD_PPS
        db = dc = ddt = dac = None
        for r in range(SSD_PPS):
            p = SSD_PPS * q + r
            ln = pl.ds(r * 128, 128)

            @pl.when(pl.program_id(0) == 0)
            def _():
                ds_scr[p] = jnp.zeros((SSM_DSTATE, 128), F32)

            fn = functools.partial(_ssd_chunk, pair=p, dot=_bdot_vjp)
            _, vjp = jax.vjp(fn, xs_ref[:, ln], dt_ref[...], ac_ref[...], b_ref[...], c_ref[...], s_ref[r])
            dxs, ddt_r, dac_r, db_r, dc_r, ds = vjp((dy_ref[:, ln], ds_scr[p]))
            dxs_ref[:, ln] = dxs + sk_ref[:, ln]
            ds_scr[p] = ds
            db, dc = (db_r, dc_r) if r == 0 else (db + db_r, dc + dc_r)
            ddt, dac = (ddt_r, dac_r) if r == 0 else (ddt + ddt_r, dac + dac_r)

        @pl.when(q % steps_per_group == 0)
        def _():
            db_ref[...] = db
            dc_ref[...] = dc

        @pl.when(q % steps_per_group != 0)
        def _():
            db_ref[...] += db
            dc_ref[...] += dc

        @pl.when(q == 0)
        def _():
            ddt_ref[...] = ddt
            dac_ref[...] = dac

        @pl.when(q != 0)
        def _():
            ddt_ref[...] += ddt
            dac_ref[...] += dac

    grp = lambda q: q // (PAIRS_PER_GROUP // SSD_PPS)
    return pl.pallas_call(
        body, name="ssd_bwd", grid=(nc, SSM_PAIRS // SSD_PPS),
        in_specs=[pl.BlockSpec((c_, SSD_W), lambda c, q: (rev(c), q)),
                  pl.BlockSpec((c_, 128), lambda c, q: (rev(c), _B_BLOCK0 + grp(q))),
                  pl.BlockSpec((c_, 128), lambda c, q: (rev(c), _C_BLOCK0 + grp(q))),
                  pl.BlockSpec((c_, DT_PAD), lambda c, q: (rev(c), 0)),
                  pl.BlockSpec((c_, DT_PAD), lambda c, q: (rev(c), 0)),
                  pl.BlockSpec((None, SSD_PPS, SSM_DSTATE, 128), lambda c, q: (rev(c), q, 0, 0)),
                  pl.BlockSpec((c_, SSD_W), lambda c, q: (rev(c), q)),
                  pl.BlockSpec((c_, SSD_W), lambda c, q: (rev(c), q))] + [ANY] * ne,
        out_specs=[pl.BlockSpec((c_, SSD_W), lambda c, q: (rev(c), q)),
                   pl.BlockSpec((c_, 128), lambda c, q: (rev(c), grp(q))),
                   pl.BlockSpec((c_, 128), lambda c, q: (rev(c), grp(q))),
                   pl.BlockSpec((c_, DT_PAD), lambda c, q: (rev(c), 0)),
                   pl.BlockSpec((c_, DT_PAD), lambda c, q: (rev(c), 0))] + [ANY] * ne,
        out_shape=[jax.ShapeDtypeStruct((t, SSM_DINNER), F32),
                   jax.ShapeDtypeStruct((t, SSM_GROUPS * SSM_DSTATE), F32),
                   jax.ShapeDtypeStruct((t, SSM_GROUPS * SSM_DSTATE), F32),
                   jax.ShapeDtypeStruct((t, DT_PAD), F32),
                   jax.ShapeDtypeStruct((t, DT_PAD), F32)] + (op.out_shape if op else []),
        scratch_shapes=[pltpu.VMEM((SSM_PAIRS, SSM_DSTATE, 128), F32)] + (_sem_pair(op.n_sem) if op else []),
        compiler_params=_cp(("arbitrary", "arbitrary")),
    )(xbc_act, xbc_act, xbc_act, dt, acum, states, dy, dskip, *(exchange or ()))


def rowwise(name, fn, row_ins, par_ins, row_outs, acc_outs, *, tt, ncb=1):
    t = row_ins[0][0].shape[0]
    assert t % tt == 0
    n_ri, n_pi, n_ro, n_ao = len(row_ins), len(par_ins), len(row_outs), len(acc_outs)

    def body(*refs):
        i = pl.program_id(1)
        ins = [r[...] for r in refs[:n_ri + n_pi]]
        outs = fn(*ins)
        ro_refs = refs[n_ri + n_pi:n_ri + n_pi + n_ro]
        ao_refs = refs[n_ri + n_pi + n_ro:]
        for r, v in zip(ro_refs, outs[:n_ro]):
            r[...] = v.astype(r.dtype)
        for r, v in zip(ao_refs, outs[n_ro:]):
            @pl.when(i == 0)
            def _(r=r, v=v):
                r[...] = v

            @pl.when(i > 0)
            def _(r=r, v=v):
                r[...] += v

    in_specs = [pl.BlockSpec((tt, bc), lambda j, i, off=off: (i, off + j)) for _, bc, off in row_ins]
    in_specs += [pl.BlockSpec((a.shape[0], bc), lambda j, i, off=off: (0, off + j)) for a, bc, off in par_ins]
    out_specs = [pl.BlockSpec((tt, bc), lambda j, i: (i, j)) for _, bc, _ in row_outs]
    out_specs += [pl.BlockSpec((r, bc), lambda j, i: (0, j)) for r, _, bc in acc_outs]
    out_shape = [jax.ShapeDtypeStruct((t, c), dt) for c, _, dt in row_outs]
    out_shape += [jax.ShapeDtypeStruct((r, c), F32) for r, c, _ in acc_outs]
    return pl.pallas_call(
        body, name=name, grid=(ncb, t // tt), in_specs=in_specs, out_specs=out_specs, out_shape=out_shape,
        compiler_params=_cp(("parallel", "arbitrary")),
    )(*[a for a, _, _ in row_ins], *[a for a, _, _ in par_ins])


def _colsum(v):
    return jnp.sum(v, axis=0, keepdims=True)


def _softplus(x):
    return jnp.maximum(x, 0.0) + jnp.log(1.0 + jnp.exp(-jnp.abs(x)))


def _gelu_tanh(x):
    return 0.5 * x * (1.0 + jnp.tanh(0.7978845608028654 * (x + 0.044715 * (x * x * x))))


D = D_MODEL


def norm_fwd(x, w):
    return rowwise("norm_fwd", lambda xv, wv: (_rms(xv, wv),), [(x, D, 0)], [(w, D, 0)], [(D, D, BF16)], [], tt=256)[0]


def norm_bwd(x, w, dh, dres):
    def fn(xv, dhv, drv, wv):
        _, vjp = jax.vjp(_rms, xv, wv)
        dx, dw = vjp(dhv)
        return dx + drv, dw
    return rowwise("norm_bwd", fn, [(x, D, 0), (dh, D, 0), (dres, D, 0)], [(w, D, 0)], [(D, D, F32)], [(1, D, D)], tt=256)


def _dt_fn(dtr, bias, a_log):
    c = dtr.shape[0]
    dt = _softplus(dtr + bias)
    da = dt * (-jnp.exp(a_log))
    tril = jnp.where(_iota((c, c), 1) <= _iota((c, c), 0), 1.0, 0.0).astype(F32)
    return dt, _hdot(tril, da)


def dt_fwd(dtr, bias, a_log):
    return rowwise("dt_fwd", _dt_fn, [(dtr, DT_PAD, 0)], [(bias, DT_PAD, 0), (a_log, DT_PAD, 0)],
                   [(DT_PAD, DT_PAD, F32), (DT_PAD, DT_PAD, F32)], [], tt=SSM_CHUNK)


def dt_bwd(dtr, bias, a_log, ddt, dacum):
    def fn(dtrv, ddtv, dacv, bv, av):
        _, vjp = jax.vjp(_dt_fn, dtrv, bv, av)
        return vjp((ddtv, dacv))
    return rowwise("dt_bwd", fn, [(dtr, DT_PAD, 0), (ddt, DT_PAD, 0), (dacum, DT_PAD, 0)],
                   [(bias, DT_PAD, 0), (a_log, DT_PAD, 0)],
                   [(DT_PAD, DT_PAD, BF16)], [(1, DT_PAD, DT_PAD), (1, DT_PAD, DT_PAD)], tt=SSM_CHUNK)


GROUP_W = SSM_DINNER // SSM_GROUPS


def _ssm_post_fn(yv, xsv, zv, dexp, nw):
    return _rms((yv + dexp * xsv) * _silu(zv), nw)


def ssm_post_fwd(yssd, xbc_act, z, dexp, nw):
    return rowwise("ssm_post_fwd", lambda *a: (_ssm_post_fn(*a),),
                   [(yssd, GROUP_W, 0), (xbc_act, GROUP_W, 0), (z, GROUP_W, 0)], [(dexp, GROUP_W, 0), (nw, GROUP_W, 0)],
                   [(SSM_DINNER, GROUP_W, BF16)], [], tt=512, ncb=SSM_GROUPS)[0]


def ssm_post_bwd(yssd, xbc_act, z, dexp, nw, dy):
    def fn(yv, xsv, zv, dyv, dv, nv):
        _, vjp = jax.vjp(_ssm_post_fn, yv, xsv, zv, dv, nv)
        return vjp(dyv)
    return rowwise("ssm_post_bwd", fn,
                   [(yssd, GROUP_W, 0), (xbc_act, GROUP_W, 0), (z, GROUP_W, 0), (dy, GROUP_W, 0)],
                   [(dexp, GROUP_W, 0), (nw, GROUP_W, 0)],
                   [(SSM_DINNER, GROUP_W, F32), (SSM_DINNER, GROUP_W, F32), (SSM_DINNER, GROUP_W, BF16)],
                   [(1, SSM_DINNER, GROUP_W), (1, SSM_DINNER, GROUP_W)], tt=512, ncb=SSM_GROUPS)


def _merge_fn(ah, asm, gh, gs):
    return _sigmoid(gh) * ah + _sigmoid(gs) * asm


def merge_fwd(a_hg, a_ssm, gates):
    return rowwise("merge_fwd", lambda *a: (_merge_fn(*a),), [(a_hg, D, 0), (a_ssm, D, 0), (gates, D, 0), (gates, D, 1)], [],
                   [(D, D, BF16)], [], tt=256)[0]


def merge_bwd(a_hg, a_ssm, gates, dmixed):
    def fn(ah, asm, gh, gs, dm):
        _, vjp = jax.vjp(_merge_fn, ah, asm, gh, gs)
        return vjp(dm)
    return rowwise("merge_bwd", fn, [(a_hg, D, 0), (a_ssm, D, 0), (gates, D, 0), (gates, D, 1), (dmixed, D, 0)], [],
                   [(D, D, BF16)] * 4, [], tt=256)


def _post1_fn(xv, uv, wpost, wpre):
    x1 = xv + _rms(uv, wpost)
    return x1, _rms(x1, wpre)


def post1_fwd(x, u, wpost, wpre):
    return rowwise("post1_fwd", _post1_fn, [(x, D, 0), (u, D, 0)], [(wpost, D, 0), (wpre, D, 0)],
                   [(D, D, F32), (D, D, BF16)], [], tt=256)


def post1_bwd(x, u, wpost, wpre, dx1, dh2):
    def fn(xv, uv, d1, d2, wa, wb):
        _, vjp = jax.vjp(_post1_fn, xv, uv, wa, wb)
        dx, du, dwa, dwb = vjp((d1, d2))
        return du, dx, dwa, dwb
    return rowwise("post1_bwd", fn, [(x, D, 0), (u, D, 0), (dx1, D, 0), (dh2, D, 0)], [(wpost, D, 0), (wpre, D, 0)],
                   [(D, D, BF16), (D, D, F32)], [(1, D, D), (1, D, D)], tt=256)


def final_fwd_bwd(x1, fo, w, target):
    def fn(x1v, fov, tv, wv):
        def loss_fn(a, b, c):
            err = a + _rms(b, c) - tv
            return 0.5 * jnp.sum(err * err) * (1.0 / D)
        loss, vjp = jax.vjp(loss_fn, x1v, fov, wv)
        dx, dfo, dw = vjp(jnp.ones((), F32))
        return dx, dfo, dw, jnp.full((1, 128), loss, F32)
    return rowwise("final_fwd_bwd", fn, [(x1, D, 0), (fo, D, 0), (target, D, 0)], [(w, D, 0)],
                   [(D, D, F32), (D, D, BF16)], [(1, D, D), (1, 128, 128)], tt=256)


HALO = 8
CONV_TT = 512
CONV_CB = 512


def _tail(kind, c, up):
    return _silu(c) if kind == "silu" else _gelu_tanh(c) * up


def conv_fwd(name, x, xoff, w, b, kind, up=None, upoff=0, act_dtype=F32):
    t = x.shape[0]
    k_, c_ = w.shape
    tt, cb = CONV_TT, CONV_CB
    hb = tt // HALO
    has_up = up is not None

    def body(*refs):
        if has_up:
            x_ref, xp_ref, w_ref, b_ref, up_ref, c_ref, a_ref, scr = refs
        else:
            x_ref, xp_ref, w_ref, b_ref, c_ref, a_ref, scr = refs
        i = pl.program_id(1)
        scr[0:HALO, :] = jnp.where(i == 0, 0.0, xp_ref[...])
        scr[HALO:HALO + tt, :] = x_ref[...]
        acc = jnp.zeros((tt, cb), F32) + b_ref[...]
        for k in range(k_):
            acc = acc + w_ref[k:k + 1, :] * scr[pl.ds(HALO - (k_ - 1) + k, tt), :]
        c_ref[...] = acc
        a_ref[...] = _tail(kind, acc, up_ref[...] if has_up else None).astype(act_dtype)

    in_specs = [pl.BlockSpec((tt, cb), lambda j, i: (i, xoff + j)),
                pl.BlockSpec((HALO, cb), lambda j, i: (jnp.maximum(i * hb - 1, 0), xoff + j)),
                pl.BlockSpec((k_, cb), lambda j, i: (0, j)),
                pl.BlockSpec((1, cb), lambda j, i: (0, j))]
    args = [x, x, w, b]
    if has_up:
        in_specs.append(pl.BlockSpec((tt, cb), lambda j, i: (i, upoff + j)))
        args.append(up)
    return pl.pallas_call(
        body, name=name, grid=(c_ // cb, t // tt), in_specs=in_specs,
        out_specs=[pl.BlockSpec((tt, cb), lambda j, i: (i, j))] * 2,
        out_shape=[jax.ShapeDtypeStruct((t, c_), F32), jax.ShapeDtypeStruct((t, c_), act_dtype)],
        scratch_shapes=[pltpu.VMEM((tt + HALO, cb), F32)],
        compiler_params=_cp(("parallel", "arbitrary")),
    )(*args)


def conv_bwd(name, x, xoff, c, coff, dact, w, kind, up=None, upoff=0):
    t = x.shape[0]
    k_, c_ = w.shape[0], dact.shape[1]
    tt, cb = CONV_TT, CONV_CB
    hb = tt // HALO
    nt = t // tt
    has_up = up is not None

    def tail_grad(cv, dav, upv):
        if has_up:
            _, vjp = jax.vjp(lambda a, u: _tail(kind, a, u), cv, upv)
            return vjp(dav)
        _, vjp = jax.vjp(lambda a: _tail(kind, a, None), cv)
        return vjp(dav)[0], None

    def body(*refs):
        if has_up:
            (x_ref, xp_ref, c_ref, cn_ref, da_ref, dan_ref, w_ref, up_ref, upn_ref,
             dx_ref, dup_ref, dw_ref, db_ref, xs, dcs) = refs
        else:
            x_ref, xp_ref, c_ref, cn_ref, da_ref, dan_ref, w_ref, dx_ref, dw_ref, db_ref, xs, dcs = refs
        i = pl.program_id(1)
        xs[0:HALO, :] = jnp.where(i == 0, 0.0, xp_ref[...])
        xs[HALO:HALO + tt, :] = x_ref[...]
        dc, dup = tail_grad(c_ref[...], da_ref[...].astype(F32), up_ref[...] if has_up else None)
        dcn, _ = tail_grad(cn_ref[...], dan_ref[...].astype(F32), upn_ref[...] if has_up else None)
        dcs[0:tt, :] = dc
        dcs[tt:tt + HALO, :] = jnp.where(i == nt - 1, 0.0, dcn)
        if has_up:
            dup_ref[...] = dup.astype(BF16)
        dx = jnp.zeros((tt, cb), F32)
        dws = []
        for k in range(k_):
            dx = dx + w_ref[k:k + 1, :] * dcs[pl.ds(k_ - 1 - k, tt), :]
            dws.append(_colsum(dc * xs[pl.ds(HALO - (k_ - 1) + k, tt), :]))
        dx_ref[...] = dx.astype(BF16)

        @pl.when(i == 0)
        def _():
            dw_ref[...] = jnp.zeros_like(dw_ref)
            db_ref[...] = jnp.zeros_like(db_ref)

        for k in range(k_):
            dw_ref[k:k + 1, :] += dws[k]
        db_ref[...] += _colsum(dc)

    tile = lambda off: pl.BlockSpec((tt, cb), lambda j, i, off=off: (i, off + j))
    prev = lambda off: pl.BlockSpec((HALO, cb), lambda j, i, off=off: (jnp.maximum(i * hb - 1, 0), off + j))
    nxt = lambda off: pl.BlockSpec((HALO, cb), lambda j, i, off=off: (jnp.minimum((i + 1) * hb, t // HALO - 1), off + j))
    in_specs = [tile(xoff), prev(xoff), tile(coff), nxt(coff), tile(0), nxt(0),
                pl.BlockSpec((k_, cb), lambda j, i: (0, coff + j))]
    args = [x, x, c, c, dact, dact, w]
    if has_up:
        in_specs += [tile(upoff), nxt(upoff)]
        args += [up, up]
    out_specs = [tile(0)] + ([tile(0)] if has_up else []) + [pl.BlockSpec((HALO, cb), lambda j, i: (0, j)),
                                                            pl.BlockSpec((1, cb), lambda j, i: (0, j))]
    out_shape = [jax.ShapeDtypeStruct((t, c_), BF16)] * (2 if has_up else 1)
    out_shape += [jax.ShapeDtypeStruct((HALO, c_), F32), jax.ShapeDtypeStruct((1, c_), F32)]
    return pl.pallas_call(
        body, name=name, grid=(c_ // cb, nt), in_specs=in_specs, out_specs=out_specs, out_shape=out_shape,
        scratch_shapes=[pltpu.VMEM((tt + HALO, cb), F32), pltpu.VMEM((tt + HALO, cb), F32)],
        compiler_params=_cp(("parallel", "arbitrary")),
    )(*args)


def ew_sum(name, parts, rows, out_dtype, tr):
    c = parts[0][0].shape[1]
    tr = min(tr, rows)
    assert rows % tr == 0 and all(off % tr == 0 for _, off in parts)
    n = len(parts)

    def body(*refs):
        acc = refs[0][...].astype(F32)
        for ref in refs[1:n]:
            acc = acc + ref[...].astype(F32)
        refs[n][...] = acc.astype(out_dtype)

    in_specs = [pl.BlockSpec((tr, c), lambda i, o=off // tr: (i + o, 0)) for _, off in parts]
    return pl.pallas_call(body, name=name, grid=(rows // tr,), in_specs=in_specs,
                          out_specs=pl.BlockSpec((tr, c), lambda i: (i, 0)),
                          out_shape=jax.ShapeDtypeStruct((rows, c), out_dtype),
                          compiler_params=_cp(("parallel",)))(*[a for a, _ in parts])


def fold_heads(dexp):
    def body(d_ref, o_ref):
        sel = jnp.where(_iota((SSM_DINNER, DT_PAD), 0) // SSM_HEADDIM == _iota((SSM_DINNER, DT_PAD), 1), 1.0, 0.0)
        o_ref[...] = _hdot(jnp.broadcast_to(d_ref[...], (8, SSM_DINNER)), sel.astype(F32), "nn", "a")[0:1, :]

    return pl.pallas_call(body, name="fold_heads", out_shape=jax.ShapeDtypeStruct((1, DT_PAD), F32),
                          compiler_params=pltpu.CompilerParams(vmem_limit_bytes=VMEM_LIMIT))(dexp)


def adamw(name, w, g, m, v, tr):
    r, c = w.shape
    tr = min(tr, r)
    assert r % tr == 0, (r, tr)

    def body(w_ref, g_ref, m_ref, v_ref, d_ref, nm_ref, nv_ref):
        gv = g_ref[...]
        nm = ADAM_B1 * m_ref[...] + (1.0 - ADAM_B1) * gv
        nv = ADAM_B2 * v_ref[...] + (1.0 - ADAM_B2) * (gv * gv)
        m_hat = nm / (1.0 - ADAM_B1 ** ADAM_STEP)
        v_hat = nv / (1.0 - ADAM_B2 ** ADAM_STEP)
        d_ref[...] = -ADAM_LR * (m_hat / (jnp.sqrt(v_hat) + ADAM_EPS) + ADAM_WD * w_ref[...])
        nm_ref[...] = nm
        nv_ref[...] = nv

    spec = pl.BlockSpec((tr, c), lambda i: (i, 0))
    shp = jax.ShapeDtypeStruct((r, c), F32)
    return pl.pallas_call(body, name=name, grid=(r // tr,), in_specs=[spec] * 4, out_specs=[spec] * 3,
                          out_shape=[shp] * 3, compiler_params=_cp(("parallel",)))(w, g, m, v)


SEG_QFIG, SEG_Z, SEG_XBC, SEG_DT, SEG_G = 0, 8192, 12288, 18432, 18496
IN_TOTAL = 22592
FFN_BLOCKS = D_FF // CONV_CB


def _own_slot(gathered, own, shard):
    return lax.dynamic_update_slice(gathered, own[None], (shard,) + (0,) * own.ndim)


def local_step(x, target, wts, par, p_rest, p_up, shard, core):
    t = x.shape[0]
    pad64 = lambda a: jnp.pad(a, ((0, 0), (0, DT_PAD - a.shape[1])))
    bias, a_log = pad64(par["ssm_dt_bias"]), pad64(par["ssm_A_log"])
    dexp = jnp.repeat(par["ssm_D"], SSM_HEADDIM, axis=1)
    in_t = wts["in_t"]

    h = norm_fwd(x, par["mix_pre_norm"])
    proj = lambda nm, off, n, tn: mm(h, in_t, "nt", name=nm, tn=tn, dims=(t, n, D), b_off=(off, 0))
    qfig = proj("proj_qfig", SEG_QFIG, 8192, 1024)
    z = proj("proj_z", SEG_Z, 4096, 1024)
    xbc = proj("proj_xbc", SEG_XBC, 6144, 1024)
    dtr = mm(h, wts["dt_t"], "nt", name="proj_dt", tn=128)
    gates = mm(h, wts["g_t"], "nt", name="proj_gates", tn=1024)
    y_hg, hg_states, g_rest, g_up = hgrn2_fwd(qfig, par["hg_lb_table"], par["hg_out_norm"],
                                              gather=([p_rest, p_up], [REST_PIECES, UP_PIECES]))
    g_rest, g_up = _own_slot(g_rest, p_rest, shard), _own_slot(g_up, p_up, shard)
    r0, r1, r2, r3 = REST_SPLITS
    wts = dict(wts, bh=g_rest[:, :r0].reshape(-1, D), bs=g_rest[:, r0:r1].reshape(-1, D), o=g_rest[:, r1:r2].reshape(-1, D),
               dn=g_rest[:, r2:r3].reshape(-1, D), up=jnp.transpose(g_up, (1, 0, 2)).reshape(D, 2 * D_FF))
    c_ssm, xbc_act = conv_fwd("ssm_conv_fwd", xbc, 0, par["ssm_conv_w"], par["ssm_conv_b"], "silu")
    dt, acum = dt_fwd(dtr, bias, a_log)
    yssd, ssd_states = ssd_fwd(xbc_act, dt, acum)
    y_ssm = ssm_post_fwd(yssd, xbc_act, z, dexp, par["ssm_out_norm"])
    a_hg = mm(y_hg, wts["bh"], "nn", name="branch_hg", tn=1024)
    a_ssm = mm(y_ssm, wts["bs"], "nn", name="branch_ssm", tn=1024, tk=2048)
    mixed = merge_fwd(a_hg, a_ssm, gates)
    u = mm(mixed, wts["o"], "nn", name="out_proj", tn=1024)
    x1, h2 = post1_fwd(x, u, par["mix_post_norm"], par["ffn_pre_norm"])
    gu = mm(h2, wts["up"], "nn", name="ffn_up", tn=1024)
    c_ffn, act = conv_fwd("ffn_conv_fwd", gu, 0, par["ffn_conv_w"], par["ffn_conv_b"], "gelu_mul",
                          up=gu, upoff=FFN_BLOCKS, act_dtype=BF16)
    fo = mm(act, wts["dn"], "nn", name="ffn_down", tn=1024, tk=1408)
    dx2, dfo, g_ffn_post, loss = final_fwd_bwd(x1, fo, par["ffn_post_norm"], target)

    dact = mm(dfo, wts["dn"], "nt", name="d_act", tn=1408)
    g_dn = mm(act, dfo, "tn", name="g_ffn_down", out_dtype=BF16, tm=1408, tn=2048, tk=1024)
    dgate, dup, g_fcw, g_fcb = conv_bwd("ffn_conv_bwd", gu, 0, c_ffn, 0, dact, par["ffn_conv_w"], "gelu_mul",
                                        up=gu, upoff=FFN_BLOCKS)
    dh2 = mm(dgate, wts["up"], "nt", name="d_h2_gate", tn=1024, tk=1408, dims=(t, D, D_FF))
    dh2 = mm(dup, wts["up"], "nt", name="d_h2_up", tn=1024, tk=1408, dims=(t, D, D_FF), b_off=(0, D_FF), acc=dh2)
    g_up_gate = mm(h2, dgate, "tn", name="g_ffn_up_gate", out_dtype=BF16, tm=2048, tn=1408, tk=1024)
    g_up_up = mm(h2, dup, "tn", name="g_ffn_up_up", out_dtype=BF16, tm=2048, tn=1408, tk=1024)
    du, dx1, g_mix_post, g_ffn_pre = post1_bwd(x, u, par["mix_post_norm"], par["ffn_pre_norm"], dx2, dh2)
    dmixed = mm(du, wts["o"], "nt", name="d_mixed", tn=1024)
    g_o = mm(mixed, du, "tn", name="g_w_out", out_dtype=BF16, tm=1024, tn=2048, tk=1024)
    da_hg, da_ssm, dg_hg, dg_ssm = merge_bwd(a_hg, a_ssm, gates, dmixed)
    dy_hg = mm(da_hg, wts["bh"], "nt", name="d_y_hg", out_dtype=BF16, tn=1024)
    g_bh = mm(y_hg, da_hg, "tn", name="g_w_branch_hg", out_dtype=BF16, tm=1024, tn=2048, tk=1024)
    dy_ssm = mm(da_ssm, wts["bs"], "nt", name="d_y_ssm", tn=1024)
    g_bs = mm(y_ssm, da_ssm, "tn", name="g_w_branch_ssm", out_dtype=BF16, tm=1024, tn=2048, tk=1024)
    dyssd, dskip, dz, g_dexp, g_ssm_norm = ssm_post_bwd(yssd, xbc_act, z, dexp, par["ssm_out_norm"], dy_ssm)

    gg_rest = jnp.concatenate([g.reshape(N_CHIPS, -1, D) for g in (g_bh, g_bs, g_o, g_dn)], axis=1)
    gg_up = jnp.transpose(jnp.concatenate([g_up_gate, g_up_up], axis=1).reshape(D, N_CHIPS, UP_COLS), (1, 0, 2))
    c_rest, c_up = pair_reduce("rest", [gg_rest, gg_up], [REST_PIECES, UP_PIECES], [432, 512], core)
    dxs, db_, dc_, ddt, dacum, rb_rest, rb_up = ssd_bwd(xbc_act, dt, acum, ssd_states, dyssd, dskip, exchange=[c_rest, c_up])
    red_rest, red_up = chip_reduce("rest", [c_rest, c_up], [rb_rest, rb_up], [432, 256], shard)
    ddtr, g_dt_bias, g_a_log = dt_bwd(dtr, bias, a_log, ddt, dacum)
    xs_blocks, bc_blocks = SSM_DINNER // CONV_CB, SSM_GROUPS * SSM_DSTATE // CONV_CB
    dxbc_x, g_cw_x, g_cb_x = conv_bwd("ssm_conv_bwd_x", xbc, 0, c_ssm, 0, dxs, par["ssm_conv_w"], "silu")
    dxbc_b, g_cw_b, g_cb_b = conv_bwd("ssm_conv_bwd_b", xbc, xs_blocks, c_ssm, xs_blocks, db_, par["ssm_conv_w"], "silu")
    dxbc_c, g_cw_c, g_cb_c = conv_bwd("ssm_conv_bwd_c", xbc, xs_blocks + bc_blocks, c_ssm, xs_blocks + bc_blocks, dc_,
                                      par["ssm_conv_w"], "silu")
    dq, df, dv, dg, g_table, g_hg_norm = hgrn2_bwd(qfig, par["hg_lb_table"], par["hg_out_norm"], hg_states, dy_hg)

    dsegs = [(dq, SEG_QFIG), (df, SEG_QFIG + 2048), (dv, SEG_QFIG + 4096), (dg, SEG_QFIG + 6144), (dz, SEG_Z),
             (dxbc_x, SEG_XBC), (dxbc_b, SEG_XBC + SSM_DINNER), (dxbc_c, SEG_XBC + SSM_DINNER + 1024)]
    g_in_parts = [mm(dseg, h, "tn", name=f"g_w_in_{n}", out_dtype=BF16, tm=1024, tn=2048, tk=1024)
                  for n, (dseg, _) in enumerate(dsegs)]
    g_dt_t = mm(ddtr, h, "tn", name="g_w_in_dt", out_dtype=BF16, tm=128, tn=2048, tk=1024)[:SSM_HEADS]
    g_in_parts += [mm(dgate_, h, "tn", name=f"g_w_in_g{n}", out_dtype=BF16, tm=1024, tn=2048, tk=1024)
                   for n, dgate_ in enumerate((dg_hg, dg_ssm))]
    dh = mm_segments("d_h", [(dseg, 0, off) for dseg, off in dsegs] + [(ddtr, 1, 0), (dg_hg, 2, 0), (dg_ssm, 2, D)],
                     [in_t, wts["dt_t"], wts["g_t"]], tm=512, tn=1024, tk=1024)
    g_in_t = jnp.concatenate(g_in_parts[:8] + [g_dt_t] + g_in_parts[8:], axis=0)
    grad_x, g_mix_pre = norm_bwd(x, par["mix_pre_norm"], dh, dx1)

    big = dict(in_t=g_in_t, rest=red_rest, up=red_up)
    g_conv_w = jnp.concatenate([g_cw_x, g_cw_b, g_cw_c], axis=1)[:SSM_CONV]
    g_conv_b = jnp.concatenate([g_cb_x, g_cb_b, g_cb_c], axis=1)
    small = dict(mix_pre_norm=g_mix_pre, mix_post_norm=g_mix_post, hg_lb_table=g_table, hg_out_norm=g_hg_norm,
                 ssm_conv_w=g_conv_w, ssm_conv_b=g_conv_b, ssm_dt_bias=g_dt_bias, ssm_A_log=g_a_log,
                 ssm_D=g_dexp, ssm_out_norm=g_ssm_norm, ffn_pre_norm=g_ffn_pre, ffn_post_norm=g_ffn_post,
                 ffn_conv_w=g_fcw[:FFN_CONV], ffn_conv_b=g_fcb)
    return loss, grad_x, big, small


MESH = pl.DeviceIdType.MESH
ANY = pl.BlockSpec(memory_space=pl.ANY)
N_CHIPS = 4
IN_SHARD = 5648
IN_ROWS = 5760
REST_SPLITS = (512, 1536, 2048, 3456)
UP_COLS = 2816
IN_PIECES, REST_PIECES, UP_PIECES = 3, 4, 4


def _place():
    x, y, c = lax.axis_index("x"), lax.axis_index("y"), lax.axis_index("c")
    chips = [(1 - x, y), (x, 1 - y), (1 - x, 1 - y)]
    return x, y, c, chips


def _rcopy(src, dst, send_sems, recv_sems, k, dev):
    return pltpu.make_async_remote_copy(src_ref=src, dst_ref=dst, send_sem=send_sems.at[k], recv_sem=recv_sems.at[k],
                                        device_id=dev, device_id_type=MESH)


def _pieces(rows, n):
    assert rows % n == 0 and (rows // n) % 16 == 0, (rows, n)
    return [(k * (rows // n), rows // n) for k in range(n)]


def _rows(c, hrows, piece):
    return pl.ds(pl.multiple_of(c * hrows + piece[0], 16), piece[1])


def _half_plan(arrays, pieces):
    return [(a.shape[-2] // 2, _pieces(a.shape[-2] // 2, n)) for a, n in zip(arrays, pieces)]


def _sem_pair(n):
    return [pltpu.SemaphoreType.DMA((n,)), pltpu.SemaphoreType.DMA((n,))]


class _Gather:
    def __init__(self, ps, pieces):
        self.plan = _half_plan(ps, pieces)
        self.n_sem = sum(2 * 3 * len(pcs) for _, pcs in self.plan)
        self.out_shape = [jax.ShapeDtypeStruct((N_CHIPS,) + p.shape, p.dtype) for p in ps]

    def _copies(self, p_refs, g_refs, send_sems, recv_sems, only_first=False):
        x, y, c, chips = _place()
        own = 2 * x + y
        sib = (x, y, 1 - c)
        first, arrive, passed, from_sib = [], [], [], []
        k = 0
        for p, g, (hrows, pcs) in zip(p_refs, g_refs, self.plan):
            for chip in chips:
                theirs = 2 * chip[0] + chip[1]
                for pc in pcs:
                    mine, other = _rows(c, hrows, pc), _rows(1 - c, hrows, pc)
                    first.append(_rcopy(p.at[mine], g.at[own, mine], send_sems, recv_sems, k, (*chip, c)))
                    if not only_first:
                        arrive.append(_rcopy(g.at[theirs, mine], g.at[theirs, mine], send_sems, recv_sems, k, (*chip, c)))
                        passed.append(_rcopy(g.at[theirs, mine], g.at[theirs, mine], send_sems, recv_sems, k + 1, sib))
                        from_sib.append(_rcopy(g.at[theirs, other], g.at[theirs, other], send_sems, recv_sems, k + 1, sib))
                    k += 2
        return first, arrive, passed, from_sib

    def start(self, p_refs, g_refs, send_sems, recv_sems):
        for cp in self._copies(p_refs, g_refs, send_sems, recv_sems, only_first=True)[0]:
            cp.start()

    def finish(self, p_refs, g_refs, send_sems, recv_sems):
        first, arrive, passed, from_sib = self._copies(p_refs, g_refs, send_sems, recv_sems)
        for got, fw in zip(arrive, passed):
            got.wait_recv()
            fw.start()
        for cp in from_sib:
            cp.wait_recv()
        for cp in first + passed:
            cp.wait_send()


def gather_weights(name, ps, pieces):
    op = _Gather(ps, pieces)
    n = len(ps)

    def body(*refs):
        p_refs, g_refs, sems = refs[:n], refs[n:2 * n], refs[2 * n:]
        op.start(p_refs, g_refs, *sems)
        op.finish(p_refs, g_refs, *sems)

    return pl.pallas_call(body, name=name, in_specs=[ANY] * n, out_specs=[ANY] * n, out_shape=op.out_shape,
                          scratch_shapes=_sem_pair(op.n_sem))(*ps)


def pair_exchange(name, gs, pieces):
    plan = _half_plan(gs, pieces)
    n_sem = sum(N_CHIPS * len(pcs) for _, pcs in plan)
    n = len(gs)

    def body(*refs):
        g_refs, r_refs, send_sems, recv_sems = refs[:n], refs[n:2 * n], refs[2 * n], refs[2 * n + 1]
        x, y, c, _ = _place()
        sib = (x, y, 1 - c)
        cps = []
        for g, r, (hrows, pcs) in zip(g_refs, r_refs, plan):
            for s in range(N_CHIPS):
                for pc in pcs:
                    cps.append(_rcopy(g.at[s, _rows(1 - c, hrows, pc)], r.at[s, pl.ds(pc[0], pc[1])],
                                      send_sems, recv_sems, len(cps), sib))
        for cp in cps:
            cp.start()
        for cp in cps:
            cp.wait()

    return pl.pallas_call(
        body, name=name, in_specs=[ANY] * n, out_specs=[ANY] * n,
        out_shape=[jax.ShapeDtypeStruct((N_CHIPS, g.shape[1] // 2, g.shape[2]), g.dtype) for g in gs],
        scratch_shapes=_sem_pair(n_sem))(*gs)


class _ChipExchange:
    def __init__(self, ss):
        self.n_sem = 3 * len(ss)
        self.out_shape = [jax.ShapeDtypeStruct((3,) + s.shape[1:], s.dtype) for s in ss]

    def _copies(self, s_refs, r_refs, send_sems, recv_sems):
        x, y, c, chips = _place()
        cps = []
        for s, r in zip(s_refs, r_refs):
            for j, chip in enumerate(chips):
                cps.append(_rcopy(s.at[2 * chip[0] + chip[1]], r.at[j], send_sems, recv_sems, len(cps), (*chip, c)))
        return cps

    def start(self, *refs):
        for cp in self._copies(*refs):
            cp.start()

    def finish(self, *refs):
        for cp in self._copies(*refs):
            cp.wait()


def chip_exchange(name, ss):
    op = _ChipExchange(ss)
    n = len(ss)

    def body(*refs):
        s_refs, r_refs, sems = refs[:n], refs[n:2 * n], refs[2 * n:]
        op.start(s_refs, r_refs, *sems)
        op.finish(s_refs, r_refs, *sems)

    return pl.pallas_call(body, name=name, in_specs=[ANY] * n, out_specs=[ANY] * n, out_shape=op.out_shape,
                          scratch_shapes=_sem_pair(op.n_sem))(*ss)


def pair_assemble(name, rs, pieces):
    plan = [(r.shape[0], _pieces(r.shape[0], n_)) for r, n_ in zip(rs, pieces)]
    n_sem = sum(len(pcs) for _, pcs in plan)
    n = len(rs)

    def body(*refs):
        r_refs, f_refs, send_sems, recv_sems = refs[:n], refs[n:2 * n], refs[2 * n], refs[2 * n + 1]
        x, y, c, _ = _place()
        sib = (x, y, 1 - c)
        cps, got = [], []
        for r, f, (hrows, pcs) in zip(r_refs, f_refs, plan):
            for pc in pcs:
                src = r.at[pl.ds(pc[0], pc[1])]
                cps.append(_rcopy(src, f.at[_rows(c, hrows, pc)], send_sems, recv_sems, len(cps), sib))
                got.append(_rcopy(src, f.at[_rows(1 - c, hrows, pc)], send_sems, recv_sems, len(got), sib))
        for cp in cps:
            cp.start()
        for cp in got:
            cp.wait_recv()
        for cp in cps:
            cp.wait_send()

    return pl.pallas_call(
        body, name=name, in_specs=[ANY] * n, out_specs=[ANY] * n,
        out_shape=[jax.ShapeDtypeStruct((2 * r.shape[0], r.shape[1]), r.dtype) for r in rs],
        scratch_shapes=_sem_pair(n_sem))(*rs)


def pair_reduce(tag, ggs, pieces, trs, core):
    recv = pair_exchange("pair_exchange_" + tag, ggs, pieces)
    flat = lambda a: a.reshape(-1, a.shape[-1])
    out = []
    for n, (gg, r, tr) in enumerate(zip(ggs, recv, trs)):
        h = gg.shape[1] // 2
        own = lax.dynamic_slice_in_dim(gg, core * h, h, axis=1)
        out.append(ew_sum(f"pair_sum_{tag}_{n}", [(flat(own), 0), (flat(r), 0)], N_CHIPS * h, BF16, tr).reshape(r.shape))
    return out


def chip_reduce(tag, cs, rbs, trs, shard):
    out = []
    for n, (c, rb, tr) in enumerate(zip(cs, rbs, trs)):
        h = c.shape[1]
        own = lax.dynamic_index_in_dim(c, shard, axis=0, keepdims=False)
        parts = [(own, 0)] + [(rb.reshape(-1, rb.shape[-1]), j * h) for j in range(3)]
        out.append(ew_sum(f"chip_sum_{tag}_{n}", parts, h, F32, tr))
    return out


N_DEV = 8


def gather_small(blk, reduce):
    rows, cols = blk.shape

    def body(x_ref, out_ref, all_ref, send_sems, recv_sems, local_sem):
        x, y, c, chips = _place()
        me, sib = (x, y, c), (x, y, 1 - c)

        def blk_rows(px, py, pc):
            return all_ref.at[pl.ds(pl.multiple_of((4 * px + 2 * py + pc) * rows, 8), rows), :]

        def copy(k, block, to, src=None):
            return _rcopy(blk_rows(*block) if src is None else src, blk_rows(*block), send_sems, recv_sems, k, to)

        mine = pltpu.make_async_copy(x_ref, blk_rows(*me), local_sem)
        mine.start()
        first = [copy(0, me, sib, src=x_ref)] + [copy(1 + j, me, (*chip, c), src=x_ref) for j, chip in enumerate(chips)]
        for cp in first:
            cp.start()
        passed = [copy(4 + j, (*chip, c), sib) for j, chip in enumerate(chips)]
        for j, chip in enumerate(chips):
            copy(1 + j, (*chip, c), me).wait_recv()
            passed[j].start()
        copy(0, sib, me).wait_recv()
        for j, chip in enumerate(chips):
            copy(4 + j, (*chip, 1 - c), me).wait_recv()
        for cp in first + passed:
            cp.wait_send()
        mine.wait()
        if reduce:
            acc = all_ref[0:rows, :]
            for d in range(1, N_DEV):
                acc = acc + all_ref[d * rows:(d + 1) * rows, :]
            out_ref[...] = acc
        else:
            out_ref[...] = all_ref[...]

    vmem = pl.BlockSpec(memory_space=pltpu.VMEM)
    return pl.pallas_call(
        body, name="reduce_small" if reduce else "gather_small", in_specs=[vmem], out_specs=vmem,
        out_shape=jax.ShapeDtypeStruct((rows if reduce else N_DEV * rows, cols), blk.dtype),
        scratch_shapes=[pltpu.VMEM((N_DEV * rows, cols), blk.dtype), pltpu.SemaphoreType.DMA((7,)),
                        pltpu.SemaphoreType.DMA((7,)), pltpu.SemaphoreType.DMA],
        compiler_params=pltpu.CompilerParams(vmem_limit_bytes=VMEM_LIMIT),
    )(blk)


WEIGHTS = ['w_in', 'mix_pre_norm', 'mix_post_norm', 'hg_lb_table', 'hg_out_norm', 'ssm_conv_w', 'ssm_conv_b',
           'ssm_dt_bias', 'ssm_A_log', 'ssm_D', 'ssm_out_norm', 'w_branch_hg', 'w_branch_ssm', 'w_out', 'ffn_pre_norm',
           'ffn_post_norm', 'ffn_w_up', 'ffn_conv_w', 'ffn_conv_b', 'ffn_w_down']
BIG = ('w_in', 'w_branch_hg', 'w_branch_ssm', 'w_out', 'ffn_w_up', 'ffn_w_down')
SMALL = tuple(n for n in WEIGHTS if n not in BIG)
CONV_SHARD = {'ssm_conv_w': SSM_CONV_DIM // N_CHIPS, 'ffn_conv_w': D_FF // N_CHIPS}
LANES = 128


def _pack(parts):
    flat = jnp.concatenate([p.reshape(-1) for p in parts])
    n = flat.shape[0]
    rows = -(-n // (8 * LANES)) * 8
    return jnp.pad(flat, (0, rows * LANES - n)).reshape(rows, LANES)


def _unpack(packed, shapes):
    flat = packed.reshape(-1)
    out, off = [], 0
    for s in shapes:
        n = int(np.prod(s))
        out.append(flat[off:off + n].reshape(s))
        off += n
    return out


def kernel(x, w_in, mix_pre_norm, mix_post_norm, hg_lb_table, hg_out_norm, ssm_conv_w, ssm_conv_b, ssm_dt_bias, ssm_A_log, ssm_D, ssm_out_norm, w_branch_hg, w_branch_ssm, w_out, ffn_pre_norm, ffn_post_norm, ffn_w_up, ffn_conv_w, ffn_conv_b, ffn_w_down, loss_target, m_w_in, m_mix_pre_norm, m_mix_post_norm, m_hg_lb_table, m_hg_out_norm, m_ssm_conv_w, m_ssm_conv_b, m_ssm_dt_bias, m_ssm_A_log, m_ssm_D, m_ssm_out_norm, m_w_branch_hg, m_w_branch_ssm, m_w_out, m_ffn_pre_norm, m_ffn_post_norm, m_ffn_w_up, m_ffn_conv_w, m_ffn_conv_b, m_ffn_w_down, v_w_in, v_mix_pre_norm, v_mix_post_norm, v_hg_lb_table, v_hg_out_norm, v_ssm_conv_w, v_ssm_conv_b, v_ssm_dt_bias, v_ssm_A_log, v_ssm_D, v_ssm_out_norm, v_w_branch_hg, v_w_branch_ssm, v_w_out, v_ffn_pre_norm, v_ffn_post_norm, v_ffn_w_up, v_ffn_conv_w, v_ffn_conv_b, v_ffn_w_down):
    w = dict(w_in=w_in, mix_pre_norm=mix_pre_norm, mix_post_norm=mix_post_norm, hg_lb_table=hg_lb_table, hg_out_norm=hg_out_norm, ssm_conv_w=ssm_conv_w, ssm_conv_b=ssm_conv_b, ssm_dt_bias=ssm_dt_bias, ssm_A_log=ssm_A_log, ssm_D=ssm_D, ssm_out_norm=ssm_out_norm, w_branch_hg=w_branch_hg, w_branch_ssm=w_branch_ssm, w_out=w_out, ffn_pre_norm=ffn_pre_norm, ffn_post_norm=ffn_post_norm, ffn_w_up=ffn_w_up, ffn_conv_w=ffn_conv_w, ffn_conv_b=ffn_conv_b, ffn_w_down=ffn_w_down)
    m = dict(w_in=m_w_in, mix_pre_norm=m_mix_pre_norm, mix_post_norm=m_mix_post_norm, hg_lb_table=m_hg_lb_table, hg_out_norm=m_hg_out_norm, ssm_conv_w=m_ssm_conv_w, ssm_conv_b=m_ssm_conv_b, ssm_dt_bias=m_ssm_dt_bias, ssm_A_log=m_ssm_A_log, ssm_D=m_ssm_D, ssm_out_norm=m_ssm_out_norm, w_branch_hg=m_w_branch_hg, w_branch_ssm=m_w_branch_ssm, w_out=m_w_out, ffn_pre_norm=m_ffn_pre_norm, ffn_post_norm=m_ffn_post_norm, ffn_w_up=m_ffn_w_up, ffn_conv_w=m_ffn_conv_w, ffn_conv_b=m_ffn_conv_b, ffn_w_down=m_ffn_w_down)
    v = dict(w_in=v_w_in, mix_pre_norm=v_mix_pre_norm, mix_post_norm=v_mix_post_norm, hg_lb_table=v_hg_lb_table, hg_out_norm=v_hg_out_norm, ssm_conv_w=v_ssm_conv_w, ssm_conv_b=v_ssm_conv_b, ssm_dt_bias=v_ssm_dt_bias, ssm_A_log=v_ssm_A_log, ssm_D=v_ssm_D, ssm_out_norm=v_ssm_out_norm, w_branch_hg=v_w_branch_hg, w_branch_ssm=v_w_branch_ssm, w_out=v_w_out, ffn_pre_norm=v_ffn_pre_norm, ffn_post_norm=v_ffn_post_norm, ffn_w_up=v_ffn_w_up, ffn_conv_w=v_ffn_conv_w, ffn_conv_b=v_ffn_conv_b, ffn_w_down=v_ffn_w_down)
    shard = 2 * lax.axis_index("x") + lax.axis_index("y")
    bf = lambda a: a.astype(BF16)

    core = lax.axis_index("c")
    p_in = jnp.concatenate([bf(w_in[0].T), jnp.zeros((IN_ROWS - IN_SHARD, D_MODEL), BF16)], axis=0)
    p_rest = jnp.concatenate([bf(w_branch_hg[0]), bf(w_branch_ssm[0]), bf(w_out[0]), bf(ffn_w_down[0])], axis=0)
    p_up = bf(ffn_w_up[0])
    (g_in,) = gather_weights("gather_w_in", [p_in], [IN_PIECES])
    in_t = _own_slot(g_in, p_in, shard)[:, :IN_SHARD].reshape(IN_TOTAL, D_MODEL)
    wts = dict(in_t=in_t, g_t=in_t[SEG_G:], dt_t=jnp.pad(in_t[SEG_DT:SEG_G], ((0, DT_PAD - SSM_HEADS), (0, 0))))
    conv_cols = max(CONV_SHARD.values())
    padc = lambda a: jnp.pad(a, ((0, 0), (0, conv_cols - a.shape[1])))
    conv_blk = jnp.concatenate([padc(ssm_conv_w[0]), padc(ffn_conv_w[0]), jnp.zeros((1, conv_cols), F32)], axis=0)
    conv_all = gather_small(conv_blk, reduce=False)
    par = {n: w[n] for n in SMALL}
    par["ssm_conv_w"] = jnp.concatenate([conv_all[16 * s:16 * s + SSM_CONV, :CONV_SHARD['ssm_conv_w']] for s in range(N_CHIPS)], axis=1)
    par["ffn_conv_w"] = jnp.concatenate([conv_all[16 * s + SSM_CONV:16 * s + SSM_CONV + FFN_CONV, :CONV_SHARD['ffn_conv_w']]
                                         for s in range(N_CHIPS)], axis=1)

    loss, grad_x, big, small = local_step(x[0], loss_target[0], wts, par, p_rest, p_up, shard, core)
    loss = lax.psum(loss[0, 0], ("x", "y", "c"))

    zpad = jnp.zeros((N_CHIPS, IN_ROWS - IN_SHARD, D_MODEL), BF16)
    gg_in = jnp.concatenate([big["in_t"].reshape(N_CHIPS, IN_SHARD, D_MODEL), zpad], axis=1)
    (c_in,) = pair_reduce("in", [gg_in], [IN_PIECES], [960], core)
    (rb_in,) = chip_exchange("chip_exchange_in", [c_in])
    (red_in,) = chip_reduce("in", [c_in], [rb_in], [480], shard)
    halves = [red_in, big["rest"], big["up"]]
    wholes = pair_assemble("pair_assemble", halves, [IN_PIECES, REST_PIECES, UP_PIECES])
    f_in, f_rest, f_up = [lax.dynamic_update_slice(f, r, (core * r.shape[0], 0)) for f, r in zip(wholes, halves)]
    r0, r1, r2, r3 = REST_SPLITS
    grads = dict(w_in=f_in[:IN_SHARD].T, w_branch_hg=f_rest[:r0], w_branch_ssm=f_rest[r0:r1], w_out=f_rest[r1:r2],
                 ffn_w_down=f_rest[r2:r3], ffn_w_up=f_up)

    small["hg_out_norm"] = ew_sum("sum_heads", [(small["hg_out_norm"][hd], 0) for hd in range(HG_HEADS)], 1, F32, 1)
    small["ssm_D"] = fold_heads(small["ssm_D"])[:, :SSM_HEADS]
    small["ssm_dt_bias"] = small["ssm_dt_bias"][:, :SSM_HEADS]
    small["ssm_A_log"] = small["ssm_A_log"][:, :SSM_HEADS]
    shapes = [small[n].shape for n in SMALL]
    summed = _unpack(gather_small(_pack([small[n] for n in SMALL]), reduce=True), shapes)
    for n, g in zip(SMALL, summed):
        if n in CONV_SHARD:
            g = lax.dynamic_slice_in_dim(g, shard * CONV_SHARD[n], CONV_SHARD[n], axis=1)
        grads[n] = g

    two_d = lambda a: a.reshape(a.shape[-2], a.shape[-1])
    delta, new_m, new_v = {}, {}, {}
    for n, tr in (("w_in", 64), ("w_branch_hg", 128), ("w_branch_ssm", 128), ("w_out", 128), ("ffn_w_up", 128), ("ffn_w_down", 128)):
        delta[n], new_m[n], new_v[n] = adamw("adamw_" + n, two_d(w[n]), grads[n], two_d(m[n]), two_d(v[n]), tr)
    sm_shapes = [two_d(w[n]).shape for n in SMALL]
    packed = adamw("adamw_small", _pack([two_d(w[n]) for n in SMALL]), _pack([grads[n] for n in SMALL]),
                   _pack([two_d(m[n]) for n in SMALL]), _pack([two_d(v[n]) for n in SMALL]), 1024)
    for res, packed_res in zip((delta, new_m, new_v), packed):
        for n, a in zip(SMALL, _unpack(packed_res, sm_shapes)):
            res[n] = a
    shaped = lambda d: [d[n].reshape(w[n].shape) for n in WEIGHTS]
    return (loss, grad_x[None], *shaped(grads), *shaped(delta), *shaped(new_m), *shaped(new_v))
```

```python
import functools

import jax
import jax.numpy as jnp
import numpy as np
from jax import lax
from jax.experimental import pallas as pl
from jax.experimental.pallas import tpu as pltpu

F32 = jnp.float32
BF16 = jnp.bfloat16

D_MODEL = 2048
EPS = 1e-6
HG_HEADS = 16
HG_DK = 128
HG_CHUNK = 64
HG_SUB = 16
SSM_DINNER = 4096
SSM_HEADDIM = 64
SSM_HEADS = 64
SSM_GROUPS = 8
SSM_DSTATE = 128
SSM_CONV = 4
SSM_CHUNK = 256
SSM_CONV_DIM = 6144
D_FF = 5632
FFN_CONV = 3
DT_PAD = 128

ADAM_LR = 0.001
ADAM_B1 = 0.9
ADAM_B2 = 0.999
ADAM_EPS = 1e-08
ADAM_WD = 0.01
ADAM_STEP = 10

VMEM_LIMIT = 56 * 1024 * 1024
HI = lax.Precision.HIGHEST


def _cp(sem, **kw):
    return pltpu.CompilerParams(dimension_semantics=sem, vmem_limit_bytes=VMEM_LIMIT, **kw)


_DIMS = {"nn": (((1,), (0,)), ((), ())), "nt": (((1,), (1,)), ((), ())), "tn": (((0,), (0,)), ((), ()))}


def mm(a, b, mode, *, name, out_dtype=F32, tm=512, tn=512, tk=None, acc=None, n_major=True,
       dims=None, a_off=(0, 0), b_off=(0, 0)):
    if dims is not None:
        M, N, K = dims
    else:
        if mode == "nn":
            (M, K), (K2, N) = a.shape, b.shape
        elif mode == "nt":
            (M, K), (N, K2) = a.shape, b.shape
        else:
            (K, M), (K2, N) = a.shape, b.shape
        assert K == K2, (a.shape, b.shape, mode)
    tm, tn = min(tm, M), min(tn, N)
    tk = K if tk is None else min(tk, K)
    assert M % tm == 0 and N % tn == 0 and K % tk == 0, (M, N, K, tm, tn, tk)
    a_blk = (tk, tm) if mode == "tn" else (tm, tk)
    b_blk = (tn, tk) if mode == "nt" else (tk, tn)
    assert all(o % s == 0 for o, s in zip(a_off, a_blk)) and all(o % s == 0 for o, s in zip(b_off, b_blk))
    ao0, ao1 = a_off[0] // a_blk[0], a_off[1] // a_blk[1]
    bo0, bo1 = b_off[0] // b_blk[0], b_off[1] // b_blk[1]
    nk = K // tk
    if n_major:
        grid = (N // tn, M // tm, nk)
        ij = lambda p0, p1: (p1, p0)
    else:
        grid = (M // tm, N // tn, nk)
        ij = lambda p0, p1: (p0, p1)

    def a_map(p0, p1, k):
        i, _ = ij(p0, p1)
        return (k + ao0, i + ao1) if mode == "tn" else (i + ao0, k + ao1)

    def b_map(p0, p1, k):
        _, j = ij(p0, p1)
        return (j + bo0, k + bo1) if mode == "nt" else (k + bo0, j + bo1)

    def o_map(p0, p1, k):
        return ij(p0, p1)

    a_spec = pl.BlockSpec(a_blk, a_map)
    b_spec = pl.BlockSpec(b_blk, b_map)
    o_spec = pl.BlockSpec((tm, tn), o_map)
    dims = _DIMS[mode]
    has_acc = acc is not None

    def body(*refs):
        if has_acc:
            a_ref, b_ref, c_ref, o_ref, acc_ref = refs
        else:
            a_ref, b_ref, o_ref, acc_ref = refs
        k = pl.program_id(2)
        part = lax.dot_general(a_ref[...], b_ref[...], dims, preferred_element_type=F32)

        @pl.when(k == 0)
        def _():
            acc_ref[...] = part

        @pl.when(k > 0)
        def _():
            acc_ref[...] += part

        @pl.when(k == nk - 1)
        def _():
            r = acc_ref[...]
            if has_acc:
                r = r + c_ref[...].astype(F32)
            o_ref[...] = r.astype(out_dtype)

    in_specs = [a_spec, b_spec] + ([o_spec] if has_acc else [])
    args = (a, b) + ((acc,) if has_acc else ())
    return pl.pallas_call(
        body, name=name, grid=grid, in_specs=in_specs, out_specs=o_spec,
        out_shape=jax.ShapeDtypeStruct((M, N), out_dtype),
        scratch_shapes=[pltpu.VMEM((tm, tn), F32)],
        compiler_params=_cp(("parallel", "parallel", "arbitrary")),
    )(*args)


def mm_segments(name, segs, bs, *, tm, tn, tk, exchange=None):
    m_, n_ = segs[0][0].shape[0], bs[0].shape[1]
    op = _ChipExchange(exchange) if exchange else None
    ne = len(exchange) if exchange else 0
    steps, k0 = [], 0
    for a, bi, row in segs:
        w = a.shape[1]
        tks = min(tk, w)
        assert w % tks == 0 and row % tks == 0 and tks == min(tk, bs[bi].shape[0]), (w, row, tks)
        steps.append((k0, w // tks, tks, bi, row // tks))
        k0 += w // tks
    nk = k0
    assert m_ % tm == 0 and n_ % tn == 0

    def a_spec(k_first, count, tks):
        return pl.BlockSpec((tm, tks), lambda j, i, k: (i, jnp.clip(k - k_first, 0, count - 1)))

    def b_spec(bi):
        mine = [s for s in steps if s[3] == bi]

        def index(j, i, k):
            blk = mine[0][4]
            for k_first, count, _, _, first_blk in mine:
                blk = jnp.where(k >= k_first, first_blk + jnp.minimum(k - k_first, count - 1), blk)
            return (blk, j)
        return pl.BlockSpec((mine[0][2], tn), index)

    ns = len(segs)

    nb = len(bs)
    grid = (n_ // tn, m_ // tm, nk)

    def body(*refs):
        a_refs, b_refs = refs[:ns], refs[ns:ns + nb]
        ex_refs = refs[ns + nb:ns + nb + ne]
        o_ref = refs[ns + nb + ne]
        got_refs = refs[ns + nb + ne + 1:ns + nb + 2 * ne + 1]
        acc_ref = refs[ns + nb + 2 * ne + 1]
        sems = refs[ns + nb + 2 * ne + 2:]
        k = pl.program_id(2)
        if op:
            first = (pl.program_id(0) == 0) & (pl.program_id(1) == 0) & (k == 0)
            last = (pl.program_id(0) == grid[0] - 1) & (pl.program_id(1) == grid[1] - 1) & (k == nk - 1)

            @pl.when(first)
            def _():
                op.start(ex_refs, got_refs, *sems)

            @pl.when(last)
            def _():
                op.finish(ex_refs, got_refs, *sems)

        @pl.when(k == 0)
        def _():
            acc_ref[...] = jnp.zeros_like(acc_ref)

        for a_ref, (k_first, count, _, bi, _) in zip(a_refs, steps):
            @pl.when((k >= k_first) & (k < k_first + count))
            def _(a_ref=a_ref, bi=bi):
                acc_ref[...] += jnp.dot(a_ref[...], b_refs[bi][...], preferred_element_type=F32)

        @pl.when(k == nk - 1)
        def _():
            o_ref[...] = acc_ref[...]

    any_spec = pl.BlockSpec(memory_space=pl.ANY)
    outs = pl.pallas_call(
        body, name=name, grid=grid,
        in_specs=[a_spec(s[0], s[1], s[2]) for s in steps] + [b_spec(bi) for bi in range(nb)] + [any_spec] * ne,
        out_specs=[pl.BlockSpec((tm, tn), lambda j, i, k: (i, j))] + [any_spec] * ne,
        out_shape=[jax.ShapeDtypeStruct((m_, n_), F32)] + (op.out_shape if op else []),
        scratch_shapes=[pltpu.VMEM((tm, tn), F32)] + (_sem_pair(op.n_sem) if op else []),
        compiler_params=_cp(("arbitrary", "arbitrary", "arbitrary")),
    )(*[a for a, _, _ in segs], *bs, *(exchange or ()))
    return outs if op else outs[0]


def _bdot_plain(a, b, mode):
    return lax.dot_general(a.astype(BF16), b.astype(BF16), _DIMS[mode], preferred_element_type=F32)


@functools.partial(jax.custom_vjp, nondiff_argnums=(2,))
def _bdot_vjp(a, b, mode):
    return _bdot_plain(a, b, mode)


def _bdot_fwd(a, b, mode):
    return _bdot_plain(a, b, mode), (a, b)


def _bdot_bwd(mode, res, g):
    a, b = res
    if mode == "nn":
        return _bdot_plain(g, b, "nt"), _bdot_plain(a, g, "tn")
    if mode == "nt":
        return _bdot_plain(g, b, "nn"), _bdot_plain(g, a, "tn")
    return _bdot_plain(b, g, "nt"), _bdot_plain(a, g, "nn")


_bdot_vjp.defvjp(_bdot_fwd, _bdot_bwd)


def _split3(x):
    x1 = x.astype(BF16)
    r1 = x - x1.astype(F32)
    x2 = r1.astype(BF16)
    return x1, x2, (r1 - x2.astype(F32)).astype(BF16)


def _hdot_impl(a, b, mode, data):
    if data == "a":
        sel = b.astype(BF16)
        parts = [lax.dot_general(p, sel, _DIMS[mode], preferred_element_type=F32) for p in _split3(a)]
    else:
        sel = a.astype(BF16)
        parts = [lax.dot_general(sel, p, _DIMS[mode], preferred_element_type=F32) for p in _split3(b)]
    return (parts[2] + parts[1]) + parts[0]


@functools.partial(jax.custom_vjp, nondiff_argnums=(2, 3))
def _hdot(a, b, mode="nn", data="b"):
    return _hdot_impl(a, b, mode, data)


def _hdot_fwd(a, b, mode, data):
    return _hdot_impl(a, b, mode, data), (a, b)


def _hdot_bwd(mode, data, res, g):
    a, b = res
    if data == "a":
        da = {"nn": lambda: _hdot_impl(g, b, "nt", "a"), "nt": lambda: _hdot_impl(g, b, "nn", "a"),
              "tn": lambda: _hdot_impl(b, g, "nt", "b")}[mode]()
        return da, jnp.zeros_like(b)
    db = {"nn": lambda: _hdot_impl(a, g, "tn", "b"), "nt": lambda: _hdot_impl(g, a, "tn", "a"),
          "tn": lambda: _hdot_impl(a, g, "nn", "b")}[mode]()
    return jnp.zeros_like(a), db


_hdot.defvjp(_hdot_fwd, _hdot_bwd)


def _sigmoid(x):
    return 1.0 / (1.0 + jnp.exp(-x))


def _silu(x):
    return x * _sigmoid(x)


def _iota(shape, dim):
    return lax.broadcasted_iota(jnp.int32, shape, dim)


def _rms(x, w):
    return x * lax.rsqrt(jnp.mean(x * x, axis=-1, keepdims=True) + EPS) * w


def _hg_chunk(q_raw, f_raw, v, g, st, t0, t1, nw, dot):
    c = q_raw.shape[0]
    m = jnp.maximum(t0, t1)
    e0, e1 = jnp.exp(t0 - m), jnp.exp(t1 - m)
    lb = e0 / (e0 + e1)
    f = lb + (1.0 - lb) * _sigmoid(f_raw)
    k = 1.0 - f
    lf = jnp.log(f)
    qh = _silu(q_raw) * (HG_DK ** -0.5)
    row, col = _iota((c, c), 0), _iota((c, c), 1)
    causal = col <= row
    tril = jnp.where(causal, 1.0, 0.0).astype(F32)
    trilb = jnp.where(causal & (col // HG_SUB == row // HG_SUB), 1.0, 0.0).astype(F32)
    b = _hdot(tril, lf)
    bl = _hdot(trilb, lf)
    a_row = b - bl
    rid = _iota((c, HG_DK), 0)
    qt = qh * jnp.exp(bl)
    kt = k * jnp.exp(-bl)
    scores = jnp.zeros((c, c), F32)
    for j in range(c // HG_SUB):
        if j == 0:
            qj = qt * jnp.exp(jnp.minimum(a_row, 0.0))
        else:
            a_j = jnp.sum(jnp.where(rid == j * HG_SUB - 1, b, 0.0), axis=0, keepdims=True)
            qj = qt * jnp.exp(jnp.minimum(a_row - a_j, 0.0))
        kj = jnp.where(rid // HG_SUB == j, kt, 0.0)
        scores = scores + dot(qj, kj, "nt")
    scores = jnp.where(causal, scores, 0.0)
    o = dot(scores, v, "nn") + dot(qh * jnp.exp(b), st, "nt")
    b_last = jnp.sum(jnp.where(rid == c - 1, b, 0.0), axis=0, keepdims=True)
    st_new = st * jnp.exp(b_last) + dot(v, k * jnp.exp(b_last - b), "tn")
    y = _rms(o, nw) * _silu(g)
    return y, st_new


HG_HPS = 4
HG_W = HG_HPS * HG_DK


def hgrn2_fwd(qfig, table, nw, *, step_chunks=2, gather=None):
    t = qfig.shape[0]
    rows = HG_CHUNK * step_chunks
    nsteps = t // rows
    nh = HG_HEADS // HG_HPS
    op = _Gather(*gather) if gather else None
    ng = len(gather[0]) if gather else 0

    def body(*refs):
        q_ref, f_ref, v_ref, g_ref, tab_ref, nw_ref = refs[:6]
        p_refs = refs[6:6 + ng]
        y_ref, s_ref = refs[6 + ng:8 + ng]
        got_refs = refs[8 + ng:8 + 2 * ng]
        st_scr = refs[8 + 2 * ng]
        sems = refs[9 + 2 * ng:]
        first_step = (pl.program_id(0) == 0) & (pl.program_id(1) == 0)
        last_step = (pl.program_id(0) == nh - 1) & (pl.program_id(1) == nsteps - 1)
        if op:
            @pl.when(first_step)
            def _():
                op.start(p_refs, got_refs, *sems)

        @pl.when(pl.program_id(1) == 0)
        def _():
            st_scr[...] = jnp.zeros_like(st_scr)

        nwv = nw_ref[...]
        for c in range(step_chunks):
            sl = pl.ds(c * HG_CHUNK, HG_CHUNK)
            for hh in range(HG_HPS):
                ln = pl.ds(hh * HG_DK, HG_DK)
                st = st_scr[hh]
                s_ref[hh, c] = st
                y, st_new = _hg_chunk(q_ref[sl, ln], f_ref[sl, ln], v_ref[sl, ln], g_ref[sl, ln], st,
                                      tab_ref[0:1, ln], tab_ref[1:2, ln], nwv, _bdot_vjp)
                y_ref[sl, ln] = y.astype(BF16)
                st_scr[hh] = st_new

        if op:
            @pl.when(last_step)
            def _():
                op.finish(p_refs, got_refs, *sems)

    blk = lambda off: pl.BlockSpec((rows, HG_W), lambda h, c, off=off: (c, off + h))
    return pl.pallas_call(
        body, name="hgrn2_fwd", grid=(nh, nsteps),
        in_specs=[blk(0), blk(nh), blk(2 * nh), blk(3 * nh),
                  pl.BlockSpec((2, HG_W), lambda h, c: (0, h)), pl.BlockSpec((1, HG_DK), lambda h, c: (0, 0))] + [ANY] * ng,
        out_specs=[pl.BlockSpec((rows, HG_W), lambda h, c: (c, h)),
                   pl.BlockSpec((HG_HPS, step_chunks, HG_DK, HG_DK), lambda h, c: (h, c, 0, 0))] + [ANY] * ng,
        out_shape=[jax.ShapeDtypeStruct((t, HG_HEADS * HG_DK), BF16),
                   jax.ShapeDtypeStruct((HG_HEADS, t // HG_CHUNK, HG_DK, HG_DK), F32)] + (op.out_shape if op else []),
        scratch_shapes=[pltpu.VMEM((HG_HPS, HG_DK, HG_DK), F32)] + (_sem_pair(op.n_sem) if op else []),
        compiler_params=_cp(("arbitrary", "arbitrary")),
    )(qfig, qfig, qfig, qfig, table, nw, *(gather[0] if gather else ()))


def hgrn2_bwd(qfig, table, nw, states, dy, *, step_chunks=2):
    t = qfig.shape[0]
    rows = HG_CHUNK * step_chunks
    nsteps = t // rows
    nh = HG_HEADS // HG_HPS

    def body(q_ref, f_ref, v_ref, g_ref, tab_ref, nw_ref, s_ref, dy_ref,
             dq_ref, df_ref, dv_ref, dg_ref, dtab_ref, dnw_ref, dst_scr):
        @pl.when(pl.program_id(1) == 0)
        def _():
            dst_scr[...] = jnp.zeros_like(dst_scr)
            dtab_ref[...] = jnp.zeros_like(dtab_ref)
            dnw_ref[...] = jnp.zeros_like(dnw_ref)

        nwv = nw_ref[...]
        fn = functools.partial(_hg_chunk, dot=_bdot_vjp)
        for c in reversed(range(step_chunks)):
            sl = pl.ds(c * HG_CHUNK, HG_CHUNK)
            for hh in range(HG_HPS):
                ln = pl.ds(hh * HG_DK, HG_DK)
                _, vjp = jax.vjp(fn, q_ref[sl, ln], f_ref[sl, ln], v_ref[sl, ln], g_ref[sl, ln], s_ref[hh, c],
                                 tab_ref[0:1, ln], tab_ref[1:2, ln], nwv)
                dq, df, dv, dg, dst, dt0, dt1, dnw = vjp((dy_ref[sl, ln].astype(F32), dst_scr[hh]))
                dq_ref[sl, ln] = dq.astype(BF16)
                df_ref[sl, ln] = df.astype(BF16)
                dv_ref[sl, ln] = dv.astype(BF16)
                dg_ref[sl, ln] = dg.astype(BF16)
                dst_scr[hh] = dst
                dtab_ref[0:1, ln] += dt0
                dtab_ref[1:2, ln] += dt1
                dnw_ref[hh] += dnw

    rev = lambda c: nsteps - 1 - c
    blk = lambda off: pl.BlockSpec((rows, HG_W), lambda h, c, off=off: (rev(c), off + h))
    oblk = lambda: pl.BlockSpec((rows, HG_W), lambda h, c: (rev(c), h))
    d = HG_HEADS * HG_DK
    outs = pl.pallas_call(
        body, name="hgrn2_bwd", grid=(nh, nsteps),
        in_specs=[blk(0), blk(nh), blk(2 * nh), blk(3 * nh),
                  pl.BlockSpec((2, HG_W), lambda h, c: (0, h)), pl.BlockSpec((1, HG_DK), lambda h, c: (0, 0)),
                  pl.BlockSpec((HG_HPS, step_chunks, HG_DK, HG_DK), lambda h, c: (h, rev(c), 0, 0)),
                  pl.BlockSpec((rows, HG_W), lambda h, c: (rev(c), h))],
        out_specs=[oblk(), oblk(), oblk(), oblk(),
                   pl.BlockSpec((2, HG_W), lambda h, c: (0, h)),
                   pl.BlockSpec((HG_HPS, 1, HG_DK), lambda h, c: (h, 0, 0))],
        out_shape=[jax.ShapeDtypeStruct((t, d), BF16)] * 4
        + [jax.ShapeDtypeStruct((2, d), F32), jax.ShapeDtypeStruct((HG_HEADS, 1, HG_DK), F32)],
        scratch_shapes=[pltpu.VMEM((HG_HPS, HG_DK, HG_DK), F32)],
        compiler_params=_cp(("parallel", "arbitrary")),
    )(qfig, qfig, qfig, qfig, table, nw, states, dy)
    return outs


def _ssd_chunk(xs2, dt, acum, bm, cm, s2, pair, dot):
    c = xs2.shape[0]
    lane = _iota((DT_PAD, 128), 1)
    expand = jnp.where(_iota((DT_PAD, 128), 0) == 2 * pair + lane // SSM_HEADDIM, 1.0, 0.0).astype(F32)
    sel = jnp.where(_iota((8, DT_PAD), 1) == 2 * pair + _iota((8, DT_PAD), 0), 1.0, 0.0).astype(F32)
    sel = jnp.where(_iota((8, DT_PAD), 0) < 2, sel, 0.0)
    dtx = _hdot(dt, expand, "nn", "a")
    acol = _hdot(acum, expand, "nn", "a")
    arow8 = _hdot(sel, acum, "nt", "b")
    row, col = _iota((c, c), 0), _iota((c, c), 1)
    causal = col <= row
    cb = dot(cm, bm, "nt")
    x2 = xs2 * dtx
    lane_c = _iota((c, 128), 1)
    y = dot(cm, s2, "nn") * jnp.exp(acol)
    for r in range(2):
        head = (lane_c // SSM_HEADDIM) == r
        a_c = jnp.sum(jnp.where(head & (lane_c % SSM_HEADDIM == 0), acol, 0.0), axis=1, keepdims=True)
        a_r = jnp.sum(jnp.where(_iota((8, c), 0) == r, arow8, 0.0), axis=0, keepdims=True)
        decay = jnp.where(causal, jnp.exp(jnp.minimum(a_c - a_r, 0.0)), 0.0)
        y = y + dot(cb * decay, jnp.where(head, x2, 0.0), "nn")
    a_last = jnp.sum(jnp.where(_iota((c, 128), 0) == c - 1, acol, 0.0), axis=0, keepdims=True)
    s2_new = s2 * jnp.exp(a_last) + dot(bm, x2 * jnp.exp(a_last - acol), "tn")
    return y, s2_new


SSM_PAIRS = SSM_HEADS // 2
PAIRS_PER_GROUP = SSM_PAIRS // SSM_GROUPS
SSD_PPS = 4
SSD_W = SSD_PPS * 128
_XS_BLOCKS = SSM_DINNER // 128
_B_BLOCK0 = _XS_BLOCKS
_C_BLOCK0 = _XS_BLOCKS + SSM_GROUPS


def ssd_fwd(xbc_act, dt, acum):
    t = xbc_act.shape[0]
    nc = t // SSM_CHUNK
    c_ = SSM_CHUNK

    def body(xs_ref, b_ref, c_ref, dt_ref, ac_ref, y_ref, s_ref, s_scr):
        q = pl.program_id(1)
        for r in range(SSD_PPS):
            p = SSD_PPS * q + r
            ln = pl.ds(r * 128, 128)

            @pl.when(pl.program_id(0) == 0)
            def _():
                s_scr[p] = jnp.zeros((SSM_DSTATE, 128), F32)

            s2 = s_scr[p]
            s_ref[r] = s2
            y, s2_new = _ssd_chunk(xs_ref[:, ln], dt_ref[...], ac_ref[...], b_ref[...], c_ref[...], s2, p, _bdot_vjp)
            y_ref[:, ln] = y
            s_scr[p] = s2_new

    grp = lambda q: q // (PAIRS_PER_GROUP // SSD_PPS)
    return pl.pallas_call(
        body, name="ssd_fwd", grid=(nc, SSM_PAIRS // SSD_PPS),
        in_specs=[pl.BlockSpec((c_, SSD_W), lambda c, q: (c, q)),
                  pl.BlockSpec((c_, 128), lambda c, q: (c, _B_BLOCK0 + grp(q))),
                  pl.BlockSpec((c_, 128), lambda c, q: (c, _C_BLOCK0 + grp(q))),
                  pl.BlockSpec((c_, DT_PAD), lambda c, q: (c, 0)),
                  pl.BlockSpec((c_, DT_PAD), lambda c, q: (c, 0))],
        out_specs=[pl.BlockSpec((c_, SSD_W), lambda c, q: (c, q)),
                   pl.BlockSpec((None, SSD_PPS, SSM_DSTATE, 128), lambda c, q: (c, q, 0, 0))],
        out_shape=[jax.ShapeDtypeStruct((t, SSM_DINNER), F32),
                   jax.ShapeDtypeStruct((nc, SSM_PAIRS, SSM_DSTATE, 128), F32)],
        scratch_shapes=[pltpu.VMEM((SSM_PAIRS, SSM_DSTATE, 128), F32)],
        compiler_params=_cp(("arbitrary", "arbitrary")),
    )(xbc_act, xbc_act, xbc_act, dt, acum)


def ssd_bwd(xbc_act, dt, acum, states, dy, dskip, *, exchange=None):
    t = xbc_act.shape[0]
    nc = t // SSM_CHUNK
    c_ = SSM_CHUNK
    rev = lambda c: nc - 1 - c
    nq = SSM_PAIRS // SSD_PPS
    op = _ChipExchange(exchange) if exchange else None
    ne = len(exchange) if exchange else 0

    def body(*refs):
        xs_ref, b_ref, c_ref, dt_ref, ac_ref, s_ref, dy_ref, sk_ref = refs[:8]
        ex_refs = refs[8:8 + ne]
        dxs_ref, db_ref, dc_ref, ddt_ref, dac_ref = refs[8 + ne:13 + ne]
        got_refs = refs[13 + ne:13 + 2 * ne]
        ds_scr = refs[13 + 2 * ne]
        sems = refs[14 + 2 * ne:]
        q = pl.program_id(1)
        if op:
            @pl.when((pl.program_id(0) == 0) & (q == 0))
            def _():
                op.start(ex_refs, got_refs, *sems)

            @pl.when((pl.program_id(0) == nc - 1) & (q == nq - 1))
            def _():
                op.finish(ex_refs, got_refs, *sems)

        steps_per_group = PAIRS_PER_GROUP // SSD_PPS
        db = dc = ddt = dac = None
        for r in range(SSD_PPS):
            p = SSD_PPS * q + r
            ln = pl.ds(r * 128, 128)

            @pl.when(pl.program_id(0) == 0)
            def _():
                ds_scr[p] = jnp.zeros((SSM_DSTATE, 128), F32)

            fn = functools.partial(_ssd_chunk, pair=p, dot=_bdot_vjp)
            _, vjp = jax.vjp(fn, xs_ref[:, ln], dt_ref[...], ac_ref[...], b_ref[...], c_ref[...], s_ref[r])
            dxs, ddt_r, dac_r, db_r, dc_r, ds = vjp((dy_ref[:, ln], ds_scr[p]))
            dxs_ref[:, ln] = dxs + sk_ref[:, ln]
            ds_scr[p] = ds
            db, dc = (db_r, dc_r) if r == 0 else (db + db_r, dc + dc_r)
            ddt, dac = (ddt_r, dac_r) if r == 0 else (ddt + ddt_r, dac + dac_r)

        @pl.when(q % steps_per_group == 0)
        def _():
            db_ref[...] = db
            dc_ref[...] = dc

        @pl.when(q % steps_per_group != 0)
        def _():
            db_ref[...] += db
            dc_ref[...] += dc

        @pl.when(q == 0)
        def _():
            ddt_ref[...] = ddt
            dac_ref[...] = dac

        @pl.when(q != 0)
        def _():
            ddt_ref[...] += ddt
            dac_ref[...] += dac

    grp = lambda q: q // (PAIRS_PER_GROUP // SSD_PPS)
    return pl.pallas_call(
        body, name="ssd_bwd", grid=(nc, SSM_PAIRS // SSD_PPS),
        in_specs=[pl.BlockSpec((c_, SSD_W), lambda c, q: (rev(c), q)),
                  pl.BlockSpec((c_, 128), lambda c, q: (rev(c), _B_BLOCK0 + grp(q))),
                  pl.BlockSpec((c_, 128), lambda c, q: (rev(c), _C_BLOCK0 + grp(q))),
                  pl.BlockSpec((c_, DT_PAD), lambda c, q: (rev(c), 0)),
                  pl.BlockSpec((c_, DT_PAD), lambda c, q: (rev(c), 0)),
                  pl.BlockSpec((None, SSD_PPS, SSM_DSTATE, 128), lambda c, q: (rev(c), q, 0, 0)),
                  pl.BlockSpec((c_, SSD_W), lambda c, q: (rev(c), q)),
                  pl.BlockSpec((c_, SSD_W), lambda c, q: (rev(c), q))] + [ANY] * ne,
        out_specs=[pl.BlockSpec((c_, SSD_W), lambda c, q: (rev(c), q)),
                   pl.BlockSpec((c_, 128), lambda c, q: (rev(c), grp(q))),
                   pl.BlockSpec((c_, 128), lambda c, q: (rev(c), grp(q))),
                   pl.BlockSpec((c_, DT_PAD), lambda c, q: (rev(c), 0)),
                   pl.BlockSpec((c_, DT_PAD), lambda c, q: (rev(c), 0))] + [ANY] * ne,
        out_shape=[jax.ShapeDtypeStruct((t, SSM_DINNER), F32),
                   jax.ShapeDtypeStruct((t, SSM_GROUPS * SSM_DSTATE), F32),
                   jax.ShapeDtypeStruct((t, SSM_GROUPS * SSM_DSTATE), F32),
                   jax.ShapeDtypeStruct((t, DT_PAD), F32),
                   jax.ShapeDtypeStruct((t, DT_PAD), F32)] + (op.out_shape if op else []),
        scratch_shapes=[pltpu.VMEM((SSM_PAIRS, SSM_DSTATE, 128), F32)] + (_sem_pair(op.n_sem) if op else []),
        compiler_params=_cp(("arbitrary", "arbitrary")),
    )(xbc_act, xbc_act, xbc_act, dt, acum, states, dy, dskip, *(exchange or ()))


def rowwise(name, fn, row_ins, par_ins, row_outs, acc_outs, *, tt, ncb=1):
    t = row_ins[0][0].shape[0]
    assert t % tt == 0
    n_ri, n_pi, n_ro, n_ao = len(row_ins), len(par_ins), len(row_outs), len(acc_outs)

    def body(*refs):
        i = pl.program_id(1)
        ins = [r[...] for r in refs[:n_ri + n_pi]]
        outs = fn(*ins)
        ro_refs = refs[n_ri + n_pi:n_ri + n_pi + n_ro]
        ao_refs = refs[n_ri + n_pi + n_ro:]
        for r, v in zip(ro_refs, outs[:n_ro]):
            r[...] = v.astype(r.dtype)
        for r, v in zip(ao_refs, outs[n_ro:]):
            @pl.when(i == 0)
            def _(r=r, v=v):
                r[...] = v

            @pl.when(i > 0)
            def _(r=r, v=v):
                r[...] += v

    in_specs = [pl.BlockSpec((tt, bc), lambda j, i, off=off: (i, off + j)) for _, bc, off in row_ins]
    in_specs += [pl.BlockSpec((a.shape[0], bc), lambda j, i, off=off: (0, off + j)) for a, bc, off in par_ins]
    out_specs = [pl.BlockSpec((tt, bc), lambda j, i: (i, j)) for _, bc, _ in row_outs]
    out_specs += [pl.BlockSpec((r, bc), lambda j, i: (0, j)) for r, _, bc in acc_outs]
    out_shape = [jax.ShapeDtypeStruct((t, c), dt) for c, _, dt in row_outs]
    out_shape += [jax.ShapeDtypeStruct((r, c), F32) for r, c, _ in acc_outs]
    return pl.pallas_call(
        body, name=name, grid=(ncb, t // tt), in_specs=in_specs, out_specs=out_specs, out_shape=out_shape,
        compiler_params=_cp(("parallel", "arbitrary")),
    )(*[a for a, _, _ in row_ins], *[a for a, _, _ in par_ins])


def _colsum(v):
    return jnp.sum(v, axis=0, keepdims=True)


def _softplus(x):
    return jnp.maximum(x, 0.0) + jnp.log(1.0 + jnp.exp(-jnp.abs(x)))


def _gelu_tanh(x):
    return 0.5 * x * (1.0 + jnp.tanh(0.7978845608028654 * (x + 0.044715 * (x * x * x))))


D = D_MODEL


def norm_fwd(x, w):
    return rowwise("norm_fwd", lambda xv, wv: (_rms(xv, wv),), [(x, D, 0)], [(w, D, 0)], [(D, D, BF16)], [], tt=256)[0]


def norm_bwd(x, w, dh, dres):
    def fn(xv, dhv, drv, wv):
        _, vjp = jax.vjp(_rms, xv, wv)
        dx, dw = vjp(dhv)
        return dx + drv, dw
    return rowwise("norm_bwd", fn, [(x, D, 0), (dh, D, 0), (dres, D, 0)], [(w, D, 0)], [(D, D, F32)], [(1, D, D)], tt=256)


def _dt_fn(dtr, bias, a_log):
    c = dtr.shape[0]
    dt = _softplus(dtr + bias)
    da = dt * (-jnp.exp(a_log))
    tril = jnp.where(_iota((c, c), 1) <= _iota((c, c), 0), 1.0, 0.0).astype(F32)
    return dt, _hdot(tril, da)


def dt_fwd(dtr, bias, a_log):
    return rowwise("dt_fwd", _dt_fn, [(dtr, DT_PAD, 0)], [(bias, DT_PAD, 0), (a_log, DT_PAD, 0)],
                   [(DT_PAD, DT_PAD, F32), (DT_PAD, DT_PAD, F32)], [], tt=SSM_CHUNK)


def dt_bwd(dtr, bias, a_log, ddt, dacum):
    def fn(dtrv, ddtv, dacv, bv, av):
        _, vjp = jax.vjp(_dt_fn, dtrv, bv, av)
        return vjp((ddtv, dacv))
    return rowwise("dt_bwd", fn, [(dtr, DT_PAD, 0), (ddt, DT_PAD, 0), (dacum, DT_PAD, 0)],
                   [(bias, DT_PAD, 0), (a_log, DT_PAD, 0)],
                   [(DT_PAD, DT_PAD, BF16)], [(1, DT_PAD, DT_PAD), (1, DT_PAD, DT_PAD)], tt=SSM_CHUNK)


GROUP_W = SSM_DINNER // SSM_GROUPS


def _ssm_post_fn(yv, xsv, zv, dexp, nw):
    return _rms((yv + dexp * xsv) * _silu(zv), nw)


def ssm_post_fwd(yssd, xbc_act, z, dexp, nw):
    return rowwise("ssm_post_fwd", lambda *a: (_ssm_post_fn(*a),),
                   [(yssd, GROUP_W, 0), (xbc_act, GROUP_W, 0), (z, GROUP_W, 0)], [(dexp, GROUP_W, 0), (nw, GROUP_W, 0)],
                   [(SSM_DINNER, GROUP_W, BF16)], [], tt=512, ncb=SSM_GROUPS)[0]


def ssm_post_bwd(yssd, xbc_act, z, dexp, nw, dy):
    def fn(yv, xsv, zv, dyv, dv, nv):
        _, vjp = jax.vjp(_ssm_post_fn, yv, xsv, zv, dv, nv)
        return vjp(dyv.astype(F32))
    return rowwise("ssm_post_bwd", fn,
                   [(yssd, GROUP_W, 0), (xbc_act, GROUP_W, 0), (z, GROUP_W, 0), (dy, GROUP_W, 0)],
                   [(dexp, GROUP_W, 0), (nw, GROUP_W, 0)],
                   [(SSM_DINNER, GROUP_W, F32), (SSM_DINNER, GROUP_W, F32), (SSM_DINNER, GROUP_W, BF16)],
                   [(1, SSM_DINNER, GROUP_W), (1, SSM_DINNER, GROUP_W)], tt=512, ncb=SSM_GROUPS)


def _merge_fn(ah, asm, gh, gs):
    return _sigmoid(gh) * ah + _sigmoid(gs) * asm


def merge_fwd(a_hg, a_ssm, gates):
    return rowwise("merge_fwd", lambda *a: (_merge_fn(*a),), [(a_hg, D, 0), (a_ssm, D, 0), (gates, D, 0), (gates, D, 1)], [],
                   [(D, D, BF16)], [], tt=256)[0]


def merge_bwd(a_hg, a_ssm, gates, dmixed):
    def fn(ah, asm, gh, gs, dm):
        _, vjp = jax.vjp(_merge_fn, ah, asm, gh, gs)
        return vjp(dm)
    return rowwise("merge_bwd", fn, [(a_hg, D, 0), (a_ssm, D, 0), (gates, D, 0), (gates, D, 1), (dmixed, D, 0)], [],
                   [(D, D, BF16)] * 4, [], tt=256)


def _post1_fn(xv, uv, wpost, wpre):
    x1 = xv + _rms(uv, wpost)
    return x1, _rms(x1, wpre)


def post1_fwd(x, u, wpost, wpre):
    return rowwise("post1_fwd", _post1_fn, [(x, D, 0), (u, D, 0)], [(wpost, D, 0), (wpre, D, 0)],
                   [(D, D, F32), (D, D, BF16)], [], tt=256)


def post1_bwd(x, u, wpost, wpre, dx1, dh2):
    def fn(xv, uv, d1, d2, wa, wb):
        _, vjp = jax.vjp(_post1_fn, xv, uv, wa, wb)
        dx, du, dwa, dwb = vjp((d1, d2))
        return du, dx, dwa, dwb
    return rowwise("post1_bwd", fn, [(x, D, 0), (u, D, 0), (dx1, D, 0), (dh2, D, 0)], [(wpost, D, 0), (wpre, D, 0)],
                   [(D, D, BF16), (D, D, F32)], [(1, D, D), (1, D, D)], tt=256)


def final_fwd_bwd(x1, fo, w, target):
    def fn(x1v, fov, tv, wv):
        def loss_fn(a, b, c):
            err = a + _rms(b, c) - tv
            return 0.5 * jnp.sum(err * err) * (1.0 / D)
        loss, vjp = jax.vjp(loss_fn, x1v, fov, wv)
        dx, dfo, dw = vjp(jnp.ones((), F32))
        return dx, dfo, dw, jnp.full((1, 128), loss, F32)
    return rowwise("final_fwd_bwd", fn, [(x1, D, 0), (fo, D, 0), (target, D, 0)], [(w, D, 0)],
                   [(D, D, F32), (D, D, BF16)], [(1, D, D), (1, 128, 128)], tt=256)


HALO = 8
CONV_TT = 512
CONV_CB = 512


def _tail(kind, c, up):
    return _silu(c) if kind == "silu" else _gelu_tanh(c) * up


def conv_fwd(name, x, xoff, w, b, kind, up=None, upoff=0, act_dtype=F32):
    t = x.shape[0]
    k_, c_ = w.shape
    tt, cb = CONV_TT, CONV_CB
    hb = tt // HALO
    has_up = up is not None

    def body(*refs):
        if has_up:
            x_ref, xp_ref, w_ref, b_ref, up_ref, c_ref, a_ref, scr = refs
        else:
            x_ref, xp_ref, w_ref, b_ref, c_ref, a_ref, scr = refs
        i = pl.program_id(1)
        scr[0:HALO, :] = jnp.where(i == 0, 0.0, xp_ref[...])
        scr[HALO:HALO + tt, :] = x_ref[...]
        acc = jnp.zeros((tt, cb), F32) + b_ref[...]
        for k in range(k_):
            acc = acc + w_ref[k:k + 1, :] * scr[pl.ds(HALO - (k_ - 1) + k, tt), :]
        c_ref[...] = acc
        a_ref[...] = _tail(kind, acc, up_ref[...] if has_up else None).astype(act_dtype)

    in_specs = [pl.BlockSpec((tt, cb), lambda j, i: (i, xoff + j)),
                pl.BlockSpec((HALO, cb), lambda j, i: (jnp.maximum(i * hb - 1, 0), xoff + j)),
                pl.BlockSpec((k_, cb), lambda j, i: (0, j)),
                pl.BlockSpec((1, cb), lambda j, i: (0, j))]
    args = [x, x, w, b]
    if has_up:
        in_specs.append(pl.BlockSpec((tt, cb), lambda j, i: (i, upoff + j)))
        args.append(up)
    return pl.pallas_call(
        body, name=name, grid=(c_ // cb, t // tt), in_specs=in_specs,
        out_specs=[pl.BlockSpec((tt, cb), lambda j, i: (i, j))] * 2,
        out_shape=[jax.ShapeDtypeStruct((t, c_), F32), jax.ShapeDtypeStruct((t, c_), act_dtype)],
        scratch_shapes=[pltpu.VMEM((tt + HALO, cb), F32)],
        compiler_params=_cp(("parallel", "arbitrary")),
    )(*args)


def conv_bwd(name, x, xoff, c, coff, dact, w, kind, up=None, upoff=0):
    t = x.shape[0]
    k_, c_ = w.shape[0], dact.shape[1]
    tt, cb = CONV_TT, CONV_CB
    hb = tt // HALO
    nt = t // tt
    has_up = up is not None

    def tail_grad(cv, dav, upv):
        if has_up:
            _, vjp = jax.vjp(lambda a, u: _tail(kind, a, u), cv, upv)
            return vjp(dav)
        _, vjp = jax.vjp(lambda a: _tail(kind, a, None), cv)
        return vjp(dav)[0], None

    def body(*refs):
        if has_up:
            (x_ref, xp_ref, c_ref, cn_ref, da_ref, dan_ref, w_ref, up_ref, upn_ref,
             dx_ref, dup_ref, dw_ref, db_ref, xs, dcs) = refs
        else:
            x_ref, xp_ref, c_ref, cn_ref, da_ref, dan_ref, w_ref, dx_ref, dw_ref, db_ref, xs, dcs = refs
        i = pl.program_id(1)
        xs[0:HALO, :] = jnp.where(i == 0, 0.0, xp_ref[...])
        xs[HALO:HALO + tt, :] = x_ref[...]
        dc, dup = tail_grad(c_ref[...], da_ref[...].astype(F32), up_ref[...] if has_up else None)
        dcn, _ = tail_grad(cn_ref[...], dan_ref[...].astype(F32), upn_ref[...] if has_up else None)
        dcs[0:tt, :] = dc
        dcs[tt:tt + HALO, :] = jnp.where(i == nt - 1, 0.0, dcn)
        if has_up:
            dup_ref[...] = dup.astype(BF16)
        dx = jnp.zeros((tt, cb), F32)
        dws = []
        for k in range(k_):
            dx = dx + w_ref[k:k + 1, :] * dcs[pl.ds(k_ - 1 - k, tt), :]
            dws.append(_colsum(dc * xs[pl.ds(HALO - (k_ - 1) + k, tt), :]))
        dx_ref[...] = dx.astype(BF16)

        @pl.when(i == 0)
        def _():
            dw_ref[...] = jnp.zeros_like(dw_ref)
            db_ref[...] = jnp.zeros_like(db_ref)

        for k in range(k_):
            dw_ref[k:k + 1, :] += dws[k]
        db_ref[...] += _colsum(dc)

    tile = lambda off: pl.BlockSpec((tt, cb), lambda j, i, off=off: (i, off + j))
    prev = lambda off: pl.BlockSpec((HALO, cb), lambda j, i, off=off: (jnp.maximum(i * hb - 1, 0), off + j))
    nxt = lambda off: pl.BlockSpec((HALO, cb), lambda j, i, off=off: (jnp.minimum((i + 1) * hb, t // HALO - 1), off + j))
    in_specs = [tile(xoff), prev(xoff), tile(coff), nxt(coff), tile(0), nxt(0),
                pl.BlockSpec((k_, cb), lambda j, i: (0, coff + j))]
    args = [x, x, c, c, dact, dact, w]
    if has_up:
        in_specs += [tile(upoff), nxt(upoff)]
        args += [up, up]
    out_specs = [tile(0)] + ([tile(0)] if has_up else []) + [pl.BlockSpec((HALO, cb), lambda j, i: (0, j)),
                                                            pl.BlockSpec((1, cb), lambda j, i: (0, j))]
    out_shape = [jax.ShapeDtypeStruct((t, c_), BF16)] * (2 if has_up else 1)
    out_shape += [jax.ShapeDtypeStruct((HALO, c_), F32), jax.ShapeDtypeStruct((1, c_), F32)]
    return pl.pallas_call(
        body, name=name, grid=(c_ // cb, nt), in_specs=in_specs, out_specs=out_specs, out_shape=out_shape,
        scratch_shapes=[pltpu.VMEM((tt + HALO, cb), F32), pltpu.VMEM((tt + HALO, cb), F32)],
        compiler_params=_cp(("parallel", "arbitrary")),
    )(*args)


def ew_sum(name, parts, rows, out_dtype, tr):
    c = parts[0][0].shape[1]
    tr = min(tr, rows)
    assert rows % tr == 0 and all(off % tr == 0 for _, off in parts)
    n = len(parts)

    def body(*refs):
        acc = refs[0][...].astype(F32)
        for ref in refs[1:n]:
            acc = acc + ref[...].astype(F32)
        refs[n][...] = acc.astype(out_dtype)

    in_specs = [pl.BlockSpec((tr, c), lambda i, o=off // tr: (i + o, 0)) for _, off in parts]
    return pl.pallas_call(body, name=name, grid=(rows // tr,), in_specs=in_specs,
                          out_specs=pl.BlockSpec((tr, c), lambda i: (i, 0)),
                          out_shape=jax.ShapeDtypeStruct((rows, c), out_dtype),
                          compiler_params=_cp(("parallel",)))(*[a for a, _ in parts])


def fold_heads(dexp):
    def body(d_ref, o_ref):
        sel = jnp.where(_iota((SSM_DINNER, DT_PAD), 0) // SSM_HEADDIM == _iota((SSM_DINNER, DT_PAD), 1), 1.0, 0.0)
        o_ref[...] = _hdot(jnp.broadcast_to(d_ref[...], (8, SSM_DINNER)), sel.astype(F32), "nn", "a")[0:1, :]

    return pl.pallas_call(body, name="fold_heads", out_shape=jax.ShapeDtypeStruct((1, DT_PAD), F32),
                          compiler_params=pltpu.CompilerParams(vmem_limit_bytes=VMEM_LIMIT))(dexp)


def adamw(name, w, g, m, v, tr):
    r, c = w.shape
    tr = min(tr, r)
    assert r % tr == 0, (r, tr)

    def body(w_ref, g_ref, m_ref, v_ref, d_ref, nm_ref, nv_ref):
        gv = g_ref[...]
        nm = ADAM_B1 * m_ref[...] + (1.0 - ADAM_B1) * gv
        nv = ADAM_B2 * v_ref[...] + (1.0 - ADAM_B2) * (gv * gv)
        m_hat = nm / (1.0 - ADAM_B1 ** ADAM_STEP)
        v_hat = nv / (1.0 - ADAM_B2 ** ADAM_STEP)
        d_ref[...] = -ADAM_LR * (m_hat / (jnp.sqrt(v_hat) + ADAM_EPS) + ADAM_WD * w_ref[...])
        nm_ref[...] = nm
        nv_ref[...] = nv

    spec = pl.BlockSpec((tr, c), lambda i: (i, 0))
    shp = jax.ShapeDtypeStruct((r, c), F32)
    return pl.pallas_call(body, name=name, grid=(r // tr,), in_specs=[spec] * 4, out_specs=[spec] * 3,
                          out_shape=[shp] * 3, compiler_params=_cp(("parallel",)))(w, g, m, v)


SEG_QFIG, SEG_Z, SEG_XBC, SEG_DT, SEG_G = 0, 8192, 12288, 18432, 18496
IN_TOTAL = 22592
FFN_BLOCKS = D_FF // CONV_CB


def _own_slot(gathered, own, shard):
    slot = lax.broadcasted_iota(jnp.int32, (gathered.shape[0],) + (1,) * own.ndim, 0)
    return jnp.where(slot == shard, own[None], gathered)


def local_step(x, target, wts, par, p_rest, p_up, shard, core):
    t = x.shape[0]
    pad64 = lambda a: jnp.pad(a, ((0, 0), (0, DT_PAD - a.shape[1])))
    bias, a_log = pad64(par["ssm_dt_bias"]), pad64(par["ssm_A_log"])
    dexp = jnp.repeat(par["ssm_D"], SSM_HEADDIM, axis=1)
    in_t = wts["in_t"]

    h = norm_fwd(x, par["mix_pre_norm"])
    proj = lambda nm, off, n, tn: mm(h, in_t, "nt", name=nm, tn=tn, dims=(t, n, D), b_off=(off, 0))
    qfig = proj("proj_qfig", SEG_QFIG, 8192, 1024)
    z = proj("proj_z", SEG_Z, 4096, 1024)
    xbc = proj("proj_xbc", SEG_XBC, 6144, 1024)
    dtr = mm(h, wts["dt_t"], "nt", name="proj_dt", tn=128)
    gates = mm(h, wts["g_t"], "nt", name="proj_gates", tn=1024)
    y_hg, hg_states, g_rest, g_up = hgrn2_fwd(qfig, par["hg_lb_table"], par["hg_out_norm"],
                                              gather=([p_rest, p_up], [REST_PIECES, UP_PIECES]))
    g_rest, g_up = _own_slot(g_rest, p_rest, shard), _own_slot(g_up, p_up, shard)
    r0, r1, r2, r3 = REST_SPLITS
    wts = dict(wts, bh=g_rest[:, :r0].reshape(-1, D), bs=g_rest[:, r0:r1].reshape(-1, D), o=g_rest[:, r1:r2].reshape(-1, D),
               dn=g_rest[:, r2:r3].reshape(-1, D), up=jnp.transpose(g_up, (1, 0, 2)).reshape(D, 2 * D_FF))
    c_ssm, xbc_act = conv_fwd("ssm_conv_fwd", xbc, 0, par["ssm_conv_w"], par["ssm_conv_b"], "silu")
    dt, acum = dt_fwd(dtr, bias, a_log)
    yssd, ssd_states = ssd_fwd(xbc_act, dt, acum)
    y_ssm = ssm_post_fwd(yssd, xbc_act, z, dexp, par["ssm_out_norm"])
    a_hg = mm(y_hg, wts["bh"], "nn", name="branch_hg", tn=1024)
    a_ssm = mm(y_ssm, wts["bs"], "nn", name="branch_ssm", tn=1024)
    mixed = merge_fwd(a_hg, a_ssm, gates)
    u = mm(mixed, wts["o"], "nn", name="out_proj", tn=1024)
    x1, h2 = post1_fwd(x, u, par["mix_post_norm"], par["ffn_pre_norm"])
    gu = mm(h2, wts["up"], "nn", name="ffn_up", tn=1024)
    c_ffn, act = conv_fwd("ffn_conv_fwd", gu, 0, par["ffn_conv_w"], par["ffn_conv_b"], "gelu_mul",
                          up=gu, upoff=FFN_BLOCKS, act_dtype=BF16)
    fo = mm(act, wts["dn"], "nn", name="ffn_down", tn=1024)
    dx2, dfo, g_ffn_post, loss = final_fwd_bwd(x1, fo, par["ffn_post_norm"], target)

    dact = mm(dfo, wts["dn"], "nt", name="d_act", out_dtype=BF16, tn=1408)
    g_dn = mm(act, dfo, "tn", name="g_ffn_down", out_dtype=BF16, tm=1408, tn=2048, tk=1024)
    dgate, dup, g_fcw, g_fcb = conv_bwd("ffn_conv_bwd", gu, 0, c_ffn, 0, dact, par["ffn_conv_w"], "gelu_mul",
                                        up=gu, upoff=FFN_BLOCKS)
    dh2 = mm(dgate, wts["up"], "nt", name="d_h2_gate", tn=1024, tk=1408, dims=(t, D, D_FF))
    dh2 = mm(dup, wts["up"], "nt", name="d_h2_up", tn=1024, tk=1408, dims=(t, D, D_FF), b_off=(0, D_FF), acc=dh2)
    g_up_gate = mm(h2, dgate, "tn", name="g_ffn_up_gate", out_dtype=BF16, tm=2048, tn=1408, tk=1024)
    g_up_up = mm(h2, dup, "tn", name="g_ffn_up_up", out_dtype=BF16, tm=2048, tn=1408, tk=1024)
    du, dx1, g_mix_post, g_ffn_pre = post1_bwd(x, u, par["mix_post_norm"], par["ffn_pre_norm"], dx2, dh2)
    dmixed = mm(du, wts["o"], "nt", name="d_mixed", tn=1024)
    g_o = mm(mixed, du, "tn", name="g_w_out", out_dtype=BF16, tm=1024, tn=2048, tk=1024)
    da_hg, da_ssm, dg_hg, dg_ssm = merge_bwd(a_hg, a_ssm, gates, dmixed)
    dy_hg = mm(da_hg, wts["bh"], "nt", name="d_y_hg", out_dtype=BF16, tn=1024)
    g_bh = mm(y_hg, da_hg, "tn", name="g_w_branch_hg", out_dtype=BF16, tm=1024, tn=2048, tk=1024)
    dy_ssm = mm(da_ssm, wts["bs"], "nt", name="d_y_ssm", out_dtype=BF16, tn=1024)
    g_bs = mm(y_ssm, da_ssm, "tn", name="g_w_branch_ssm", out_dtype=BF16, tm=1024, tn=2048, tk=1024)
    dyssd, dskip, dz, g_dexp, g_ssm_norm = ssm_post_bwd(yssd, xbc_act, z, dexp, par["ssm_out_norm"], dy_ssm)

    gg_rest = jnp.concatenate([g.reshape(N_CHIPS, -1, D) for g in (g_bh, g_bs, g_o, g_dn)], axis=1)
    gg_up = jnp.transpose(jnp.concatenate([g_up_gate, g_up_up], axis=1).reshape(D, N_CHIPS, UP_COLS), (1, 0, 2))
    c_rest, c_up = pair_reduce("rest", [gg_rest, gg_up], [REST_PIECES, UP_PIECES], [432, 512], core)
    dxs, db_, dc_, ddt, dacum, rb_rest, rb_up = ssd_bwd(xbc_act, dt, acum, ssd_states, dyssd, dskip, exchange=[c_rest, c_up])
    red_rest, red_up = chip_reduce("rest", [c_rest, c_up], [rb_rest, rb_up], [432, 256], shard)
    ddtr, g_dt_bias, g_a_log = dt_bwd(dtr, bias, a_log, ddt, dacum)
    xs_blocks, bc_blocks = SSM_DINNER // CONV_CB, SSM_GROUPS * SSM_DSTATE // CONV_CB
    dxbc_x, g_cw_x, g_cb_x = conv_bwd("ssm_conv_bwd_x", xbc, 0, c_ssm, 0, dxs, par["ssm_conv_w"], "silu")
    dxbc_b, g_cw_b, g_cb_b = conv_bwd("ssm_conv_bwd_b", xbc, xs_blocks, c_ssm, xs_blocks, db_, par["ssm_conv_w"], "silu")
    dxbc_c, g_cw_c, g_cb_c = conv_bwd("ssm_conv_bwd_c", xbc, xs_blocks + bc_blocks, c_ssm, xs_blocks + bc_blocks, dc_,
                                      par["ssm_conv_w"], "silu")
    dq, df, dv, dg, g_table, g_hg_norm = hgrn2_bwd(qfig, par["hg_lb_table"], par["hg_out_norm"], hg_states, dy_hg)

    dsegs = [(dq, SEG_QFIG), (df, SEG_QFIG + 2048), (dv, SEG_QFIG + 4096), (dg, SEG_QFIG + 6144), (dz, SEG_Z),
             (dxbc_x, SEG_XBC), (dxbc_b, SEG_XBC + SSM_DINNER), (dxbc_c, SEG_XBC + SSM_DINNER + 1024)]
    g_in_parts = [mm(dseg, h, "tn", name=f"g_w_in_{n}", out_dtype=BF16, tm=1024, tn=2048, tk=1024)
                  for n, (dseg, _) in enumerate(dsegs)]
    g_dt_t = mm(ddtr, h, "tn", name="g_w_in_dt", out_dtype=BF16, tm=128, tn=2048, tk=1024)[:SSM_HEADS]
    g_in_parts += [mm(dgate_, h, "tn", name=f"g_w_in_g{n}", out_dtype=BF16, tm=1024, tn=2048, tk=1024)
                   for n, dgate_ in enumerate((dg_hg, dg_ssm))]
    zpad = jnp.zeros((N_CHIPS, IN_ROWS - IN_SHARD, D), BF16)
    g_in_t = jnp.concatenate(g_in_parts[:8] + [g_dt_t] + g_in_parts[8:], axis=0).reshape(N_CHIPS, IN_SHARD, D)
    (c_in,) = pair_reduce("in", [jnp.concatenate([g_in_t, zpad], axis=1)], [IN_PIECES], [960], core)
    dh, rb_in = mm_segments("d_h", [(dseg, 0, off) for dseg, off in dsegs] + [(ddtr, 1, 0), (dg_hg, 2, 0), (dg_ssm, 2, D)],
                            [in_t, wts["dt_t"], wts["g_t"]], tm=512, tn=1024, tk=1024, exchange=[c_in])
    (red_in,) = chip_reduce("in", [c_in], [rb_in], [480], shard)
    grad_x, g_mix_pre = norm_bwd(x, par["mix_pre_norm"], dh, dx1)

    big = dict(in_t=red_in, rest=red_rest, up=red_up)
    g_conv_w = jnp.concatenate([g_cw_x, g_cw_b, g_cw_c], axis=1)[:SSM_CONV]
    g_conv_b = jnp.concatenate([g_cb_x, g_cb_b, g_cb_c], axis=1)
    small = dict(mix_pre_norm=g_mix_pre, mix_post_norm=g_mix_post, hg_lb_table=g_table, hg_out_norm=g_hg_norm,
                 ssm_conv_w=g_conv_w, ssm_conv_b=g_conv_b, ssm_dt_bias=g_dt_bias, ssm_A_log=g_a_log,
                 ssm_D=g_dexp, ssm_out_norm=g_ssm_norm, ffn_pre_norm=g_ffn_pre, ffn_post_norm=g_ffn_post,
                 ffn_conv_w=g_fcw[:FFN_CONV], ffn_conv_b=g_fcb)
    return loss, grad_x, big, small


MESH = pl.DeviceIdType.MESH
ANY = pl.BlockSpec(memory_space=pl.ANY)
N_CHIPS = 4
IN_SHARD = 5648
IN_ROWS = 5760
REST_SPLITS = (512, 1536, 2048, 3456)
UP_COLS = 2816
IN_PIECES, REST_PIECES, UP_PIECES = 3, 4, 4


def _place():
    x, y, c = lax.axis_index("x"), lax.axis_index("y"), lax.axis_index("c")
    chips = [(1 - x, y), (x, 1 - y), (1 - x, 1 - y)]
    return x, y, c, chips


def _rcopy(src, dst, send_sems, recv_sems, k, dev):
    return pltpu.make_async_remote_copy(src_ref=src, dst_ref=dst, send_sem=send_sems.at[k], recv_sem=recv_sems.at[k],
                                        device_id=dev, device_id_type=MESH)


def _pieces(rows, n):
    assert rows % n == 0 and (rows // n) % 16 == 0, (rows, n)
    return [(k * (rows // n), rows // n) for k in range(n)]


def _rows(c, hrows, piece):
    return pl.ds(pl.multiple_of(c * hrows + piece[0], 16), piece[1])


def _half_plan(arrays, pieces):
    return [(a.shape[-2] // 2, _pieces(a.shape[-2] // 2, n)) for a, n in zip(arrays, pieces)]


def _sem_pair(n):
    return [pltpu.SemaphoreType.DMA((n,)), pltpu.SemaphoreType.DMA((n,))]


class _Gather:
    def __init__(self, ps, pieces):
        self.plan = _half_plan(ps, pieces)
        self.n_sem = sum(2 * 3 * len(pcs) for _, pcs in self.plan)
        self.out_shape = [jax.ShapeDtypeStruct((N_CHIPS,) + p.shape, p.dtype) for p in ps]

    def _copies(self, p_refs, g_refs, send_sems, recv_sems, only_first=False):
        x, y, c, chips = _place()
        own = 2 * x + y
        sib = (x, y, 1 - c)
        first, arrive, passed, from_sib = [], [], [], []
        k = 0
        for p, g, (hrows, pcs) in zip(p_refs, g_refs, self.plan):
            for chip in chips:
                theirs = 2 * chip[0] + chip[1]
                for pc in pcs:
                    mine, other = _rows(c, hrows, pc), _rows(1 - c, hrows, pc)
                    first.append(_rcopy(p.at[mine], g.at[own, mine], send_sems, recv_sems, k, (*chip, c)))
                    if not only_first:
                        arrive.append(_rcopy(g.at[theirs, mine], g.at[theirs, mine], send_sems, recv_sems, k, (*chip, c)))
                        passed.append(_rcopy(g.at[theirs, mine], g.at[theirs, mine], send_sems, recv_sems, k + 1, sib))
                        from_sib.append(_rcopy(g.at[theirs, other], g.at[theirs, other], send_sems, recv_sems, k + 1, sib))
                    k += 2
        return first, arrive, passed, from_sib

    def start(self, p_refs, g_refs, send_sems, recv_sems):
        for cp in self._copies(p_refs, g_refs, send_sems, recv_sems, only_first=True)[0]:
            cp.start()

    def finish(self, p_refs, g_refs, send_sems, recv_sems):
        first, arrive, passed, from_sib = self._copies(p_refs, g_refs, send_sems, recv_sems)
        for got, fw in zip(arrive, passed):
            got.wait_recv()
            fw.start()
        for cp in from_sib:
            cp.wait_recv()
        for cp in first + passed:
            cp.wait_send()


def gather_weights(name, ps, pieces):
    op = _Gather(ps, pieces)
    n = len(ps)

    def body(*refs):
        p_refs, g_refs, sems = refs[:n], refs[n:2 * n], refs[2 * n:]
        op.start(p_refs, g_refs, *sems)
        op.finish(p_refs, g_refs, *sems)

    return pl.pallas_call(body, name=name, in_specs=[ANY] * n, out_specs=[ANY] * n, out_shape=op.out_shape,
                          scratch_shapes=_sem_pair(op.n_sem))(*ps)


def pair_exchange(name, gs, pieces):
    plan = _half_plan(gs, pieces)
    n_sem = sum(N_CHIPS * len(pcs) for _, pcs in plan)
    n = len(gs)

    def body(*refs):
        g_refs, r_refs, send_sems, recv_sems = refs[:n], refs[n:2 * n], refs[2 * n], refs[2 * n + 1]
        x, y, c, _ = _place()
        sib = (x, y, 1 - c)
        cps = []
        for g, r, (hrows, pcs) in zip(g_refs, r_refs, plan):
            for s in range(N_CHIPS):
                for pc in pcs:
                    cps.append(_rcopy(g.at[s, _rows(1 - c, hrows, pc)], r.at[s, pl.ds(pc[0], pc[1])],
                                      send_sems, recv_sems, len(cps), sib))
        for cp in cps:
            cp.start()
        for cp in cps:
            cp.wait()

    return pl.pallas_call(
        body, name=name, in_specs=[ANY] * n, out_specs=[ANY] * n,
        out_shape=[jax.ShapeDtypeStruct((N_CHIPS, g.shape[1] // 2, g.shape[2]), g.dtype) for g in gs],
        scratch_shapes=_sem_pair(n_sem))(*gs)


class _ChipExchange:
    def __init__(self, ss):
        self.n_sem = 3 * len(ss)
        self.out_shape = [jax.ShapeDtypeStruct((3,) + s.shape[1:], s.dtype) for s in ss]

    def _copies(self, s_refs, r_refs, send_sems, recv_sems):
        x, y, c, chips = _place()
        cps = []
        for s, r in zip(s_refs, r_refs):
            for j, chip in enumerate(chips):
                cps.append(_rcopy(s.at[2 * chip[0] + chip[1]], r.at[j], send_sems, recv_sems, len(cps), (*chip, c)))
        return cps

    def start(self, *refs):
        for cp in self._copies(*refs):
            cp.start()

    def finish(self, *refs):
        for cp in self._copies(*refs):
            cp.wait()


def pair_assemble(name, rs, pieces):
    plan = [(r.shape[0], _pieces(r.shape[0], n_)) for r, n_ in zip(rs, pieces)]
    n_sem = sum(len(pcs) for _, pcs in plan)
    n = len(rs)

    def body(*refs):
        r_refs, f_refs, send_sems, recv_sems = refs[:n], refs[n:2 * n], refs[2 * n], refs[2 * n + 1]
        x, y, c, _ = _place()
        sib = (x, y, 1 - c)
        cps, got = [], []
        for r, f, (hrows, pcs) in zip(r_refs, f_refs, plan):
            for pc in pcs:
                src = r.at[pl.ds(pc[0], pc[1])]
                cps.append(_rcopy(src, f.at[_rows(c, hrows, pc)], send_sems, recv_sems, len(cps), sib))
                got.append(_rcopy(src, f.at[_rows(1 - c, hrows, pc)], send_sems, recv_sems, len(got), sib))
        for cp in cps:
            cp.start()
        for cp in got:
            cp.wait_recv()
        for cp in cps:
            cp.wait_send()

    return pl.pallas_call(
        body, name=name, in_specs=[ANY] * n, out_specs=[ANY] * n,
        out_shape=[jax.ShapeDtypeStruct((2 * r.shape[0], r.shape[1]), r.dtype) for r in rs],
        scratch_shapes=_sem_pair(n_sem))(*rs)


def pair_reduce(tag, ggs, pieces, trs, core):
    recv = pair_exchange("pair_exchange_" + tag, ggs, pieces)
    flat = lambda a: a.reshape(-1, a.shape[-1])
    out = []
    for n, (gg, r, tr) in enumerate(zip(ggs, recv, trs)):
        h = gg.shape[1] // 2
        own = lax.dynamic_slice_in_dim(gg, core * h, h, axis=1)
        out.append(ew_sum(f"pair_sum_{tag}_{n}", [(flat(own), 0), (flat(r), 0)], N_CHIPS * h, BF16, tr).reshape(r.shape))
    return out


def chip_reduce(tag, cs, rbs, trs, shard):
    out = []
    for n, (c, rb, tr) in enumerate(zip(cs, rbs, trs)):
        h = c.shape[1]
        own = lax.dynamic_index_in_dim(c, shard, axis=0, keepdims=False)
        parts = [(own, 0)] + [(rb.reshape(-1, rb.shape[-1]), j * h) for j in range(3)]
        out.append(ew_sum(f"chip_sum_{tag}_{n}", parts, h, F32, tr))
    return out


N_DEV = 8


def gather_small(blk, reduce):
    rows, cols = blk.shape

    def body(x_ref, out_ref, all_ref, send_sems, recv_sems, local_sem):
        x, y, c, chips = _place()
        me, sib = (x, y, c), (x, y, 1 - c)

        def blk_rows(px, py, pc):
            return all_ref.at[pl.ds(pl.multiple_of((4 * px + 2 * py + pc) * rows, 8), rows), :]

        def copy(k, block, to, src=None):
            return _rcopy(blk_rows(*block) if src is None else src, blk_rows(*block), send_sems, recv_sems, k, to)

        mine = pltpu.make_async_copy(x_ref, blk_rows(*me), local_sem)
        mine.start()
        first = [copy(0, me, sib, src=x_ref)] + [copy(1 + j, me, (*chip, c), src=x_ref) for j, chip in enumerate(chips)]
        for cp in first:
            cp.start()
        passed = [copy(4 + j, (*chip, c), sib) for j, chip in enumerate(chips)]
        for j, chip in enumerate(chips):
            copy(1 + j, (*chip, c), me).wait_recv()
            passed[j].start()
        copy(0, sib, me).wait_recv()
        for j, chip in enumerate(chips):
            copy(4 + j, (*chip, 1 - c), me).wait_recv()
        for cp in first + passed:
            cp.wait_send()
        mine.wait()
        if reduce:
            acc = all_ref[0:rows, :]
            for d in range(1, N_DEV):
                acc = acc + all_ref[d * rows:(d + 1) * rows, :]
            out_ref[...] = acc
        else:
            out_ref[...] = all_ref[...]

    vmem = pl.BlockSpec(memory_space=pltpu.VMEM)
    return pl.pallas_call(
        body, name="reduce_small" if reduce else "gather_small", in_specs=[vmem], out_specs=vmem,
        out_shape=jax.ShapeDtypeStruct((rows if reduce else N_DEV * rows, cols), blk.dtype),
        scratch_shapes=[pltpu.VMEM((N_DEV * rows, cols), blk.dtype), pltpu.SemaphoreType.DMA((7,)),
                        pltpu.SemaphoreType.DMA((7,)), pltpu.SemaphoreType.DMA],
        compiler_params=pltpu.CompilerParams(vmem_limit_bytes=VMEM_LIMIT),
    )(blk)


WEIGHTS = ['w_in', 'mix_pre_norm', 'mix_post_norm', 'hg_lb_table', 'hg_out_norm', 'ssm_conv_w', 'ssm_conv_b',
           'ssm_dt_bias', 'ssm_A_log', 'ssm_D', 'ssm_out_norm', 'w_branch_hg', 'w_branch_ssm', 'w_out', 'ffn_pre_norm',
           'ffn_post_norm', 'ffn_w_up', 'ffn_conv_w', 'ffn_conv_b', 'ffn_w_down']
BIG = ('w_in', 'w_branch_hg', 'w_branch_ssm', 'w_out', 'ffn_w_up', 'ffn_w_down')
SMALL = tuple(n for n in WEIGHTS if n not in BIG)
CONV_SHARD = {'ssm_conv_w': SSM_CONV_DIM // N_CHIPS, 'ffn_conv_w': D_FF // N_CHIPS}
LANES = 128


def _pack(parts):
    flat = jnp.concatenate([p.reshape(-1) for p in parts])
    n = flat.shape[0]
    rows = -(-n // (8 * LANES)) * 8
    return jnp.pad(flat, (0, rows * LANES - n)).reshape(rows, LANES)


def _unpack(packed, shapes):
    flat = packed.reshape(-1)
    out, off = [], 0
    for s in shapes:
        n = int(np.prod(s))
        out.append(flat[off:off + n].reshape(s))
        off += n
    return out


def kernel(x, w_in, mix_pre_norm, mix_post_norm, hg_lb_table, hg_out_norm, ssm_conv_w, ssm_conv_b, ssm_dt_bias, ssm_A_log, ssm_D, ssm_out_norm, w_branch_hg, w_branch_ssm, w_out, ffn_pre_norm, ffn_post_norm, ffn_w_up, ffn_conv_w, ffn_conv_b, ffn_w_down, loss_target, m_w_in, m_mix_pre_norm, m_mix_post_norm, m_hg_lb_table, m_hg_out_norm, m_ssm_conv_w, m_ssm_conv_b, m_ssm_dt_bias, m_ssm_A_log, m_ssm_D, m_ssm_out_norm, m_w_branch_hg, m_w_branch_ssm, m_w_out, m_ffn_pre_norm, m_ffn_post_norm, m_ffn_w_up, m_ffn_conv_w, m_ffn_conv_b, m_ffn_w_down, v_w_in, v_mix_pre_norm, v_mix_post_norm, v_hg_lb_table, v_hg_out_norm, v_ssm_conv_w, v_ssm_conv_b, v_ssm_dt_bias, v_ssm_A_log, v_ssm_D, v_ssm_out_norm, v_w_branch_hg, v_w_branch_ssm, v_w_out, v_ffn_pre_norm, v_ffn_post_norm, v_ffn_w_up, v_ffn_conv_w, v_ffn_conv_b, v_ffn_w_down):
    w = dict(w_in=w_in, mix_pre_norm=mix_pre_norm, mix_post_norm=mix_post_norm, hg_lb_table=hg_lb_table, hg_out_norm=hg_out_norm, ssm_conv_w=ssm_conv_w, ssm_conv_b=ssm_conv_b, ssm_dt_bias=ssm_dt_bias, ssm_A_log=ssm_A_log, ssm_D=ssm_D, ssm_out_norm=ssm_out_norm, w_branch_hg=w_branch_hg, w_branch_ssm=w_branch_ssm, w_out=w_out, ffn_pre_norm=ffn_pre_norm, ffn_post_norm=ffn_post_norm, ffn_w_up=ffn_w_up, ffn_conv_w=ffn_conv_w, ffn_conv_b=ffn_conv_b, ffn_w_down=ffn_w_down)
    m = dict(w_in=m_w_in, mix_pre_norm=m_mix_pre_norm, mix_post_norm=m_mix_post_norm, hg_lb_table=m_hg_lb_table, hg_out_norm=m_hg_out_norm, ssm_conv_w=m_ssm_conv_w, ssm_conv_b=m_ssm_conv_b, ssm_dt_bias=m_ssm_dt_bias, ssm_A_log=m_ssm_A_log, ssm_D=m_ssm_D, ssm_out_norm=m_ssm_out_norm, w_branch_hg=m_w_branch_hg, w_branch_ssm=m_w_branch_ssm, w_out=m_w_out, ffn_pre_norm=m_ffn_pre_norm, ffn_post_norm=m_ffn_post_norm, ffn_w_up=m_ffn_w_up, ffn_conv_w=m_ffn_conv_w, ffn_conv_b=m_ffn_conv_b, ffn_w_down=m_ffn_w_down)
    v = dict(w_in=v_w_in, mix_pre_norm=v_mix_pre_norm, mix_post_norm=v_mix_post_norm, hg_lb_table=v_hg_lb_table, hg_out_norm=v_hg_out_norm, ssm_conv_w=v_ssm_conv_w, ssm_conv_b=v_ssm_conv_b, ssm_dt_bias=v_ssm_dt_bias, ssm_A_log=v_ssm_A_log, ssm_D=v_ssm_D, ssm_out_norm=v_ssm_out_norm, w_branch_hg=v_w_branch_hg, w_branch_ssm=v_w_branch_ssm, w_out=v_w_out, ffn_pre_norm=v_ffn_pre_norm, ffn_post_norm=v_ffn_post_norm, ffn_w_up=v_ffn_w_up, ffn_conv_w=v_ffn_conv_w, ffn_conv_b=v_ffn_conv_b, ffn_w_down=v_ffn_w_down)
    shard = 2 * lax.axis_index("x") + lax.axis_index("y")
    bf = lambda a: a.astype(BF16)

    core = lax.axis_index("c")
    p_in = jnp.concatenate([bf(w_in[0].T), jnp.zeros((IN_ROWS - IN_SHARD, D_MODEL), BF16)], axis=0)
    p_rest = jnp.concatenate([bf(w_branch_hg[0]), bf(w_branch_ssm[0]), bf(w_out[0]), bf(ffn_w_down[0])], axis=0)
    p_up = bf(ffn_w_up[0])
    (g_in,) = gather_weights("gather_w_in", [p_in], [IN_PIECES])
    in_t = _own_slot(g_in, p_in, shard)[:, :IN_SHARD].reshape(IN_TOTAL, D_MODEL)
    wts = dict(in_t=in_t, g_t=in_t[SEG_G:], dt_t=jnp.pad(in_t[SEG_DT:SEG_G], ((0, DT_PAD - SSM_HEADS), (0, 0))))
    conv_cols = max(CONV_SHARD.values())
    padc = lambda a: jnp.pad(a, ((0, 0), (0, conv_cols - a.shape[1])))
    conv_blk = jnp.concatenate([padc(ssm_conv_w[0]), padc(ffn_conv_w[0]), jnp.zeros((1, conv_cols), F32)], axis=0)
    conv_all = gather_small(conv_blk, reduce=False)
    par = {n: w[n] for n in SMALL}
    par["ssm_conv_w"] = jnp.concatenate([conv_all[16 * s:16 * s + SSM_CONV, :CONV_SHARD['ssm_conv_w']] for s in range(N_CHIPS)], axis=1)
    par["ffn_conv_w"] = jnp.concatenate([conv_all[16 * s + SSM_CONV:16 * s + SSM_CONV + FFN_CONV, :CONV_SHARD['ffn_conv_w']]
                                         for s in range(N_CHIPS)], axis=1)

    loss, grad_x, big, small = local_step(x[0], loss_target[0], wts, par, p_rest, p_up, shard, core)
    loss = lax.psum(loss[0, 0], ("x", "y", "c"))

    halves = [big["in_t"], big["rest"], big["up"]]
    wholes = pair_assemble("pair_assemble", halves, [IN_PIECES, REST_PIECES, UP_PIECES])
    f_in, f_rest, f_up = [_own_slot(f.reshape((2,) + r.shape), r, core).reshape(f.shape) for f, r in zip(wholes, halves)]
    r0, r1, r2, r3 = REST_SPLITS
    grads = dict(w_in=f_in[:IN_SHARD].T, w_branch_hg=f_rest[:r0], w_branch_ssm=f_rest[r0:r1], w_out=f_rest[r1:r2],
                 ffn_w_down=f_rest[r2:r3], ffn_w_up=f_up)

    small["hg_out_norm"] = ew_sum("sum_heads", [(small["hg_out_norm"][hd], 0) for hd in range(HG_HEADS)], 1, F32, 1)
    small["ssm_D"] = fold_heads(small["ssm_D"])[:, :SSM_HEADS]
    small["ssm_dt_bias"] = small["ssm_dt_bias"][:, :SSM_HEADS]
    small["ssm_A_log"] = small["ssm_A_log"][:, :SSM_HEADS]
    shapes = [small[n].shape for n in SMALL]
    summed = _unpack(gather_small(_pack([small[n] for n in SMALL]), reduce=True), shapes)
    for n, g in zip(SMALL, summed):
        if n in CONV_SHARD:
            g = lax.dynamic_slice_in_dim(g, shard * CONV_SHARD[n], CONV_SHARD[n], axis=1)
        grads[n] = g

    two_d = lambda a: a.reshape(a.shape[-2], a.shape[-1])
    delta, new_m, new_v = {}, {}, {}
    for n, tr in (("w_in", 64), ("w_branch_hg", 128), ("w_branch_ssm", 128), ("w_out", 128), ("ffn_w_up", 128), ("ffn_w_down", 128)):
        delta[n], new_m[n], new_v[n] = adamw("adamw_" + n, two_d(w[n]), grads[n], two_d(m[n]), two_d(v[n]), tr)
    sm_shapes = [two_d(w[n]).shape for n in SMALL]
    packed = adamw("adamw_small", _pack([two_d(w[n]) for n in SMALL]), _pack([grads[n] for n in SMALL]),
                   _pack([two_d(m[n]) for n in SMALL]), _pack([two_d(v[n]) for n in SMALL]), 1024)
    for res, packed_res in zip((delta, new_m, new_v), packed):
        for n, a in zip(SMALL, _unpack(packed_res, sm_shapes)):
            res[n] = a
    shaped = lambda d: [d[n].reshape(w[n].shape) for n in WEIGHTS]
    return (loss, grad_x[None], *shaped(grads), *shaped(delta), *shaped(new_m), *shaped(new_v))
```

```python
import functools

import jax
import jax.numpy as jnp
import numpy as np
from jax import lax
from jax.experimental import pallas as pl
from jax.experimental.pallas import tpu as pltpu

F32 = jnp.float32
BF16 = jnp.bfloat16

D_MODEL = 2048
EPS = 1e-6
HG_HEADS = 16
HG_DK = 128
HG_CHUNK = 64
HG_SUB = 16
SSM_DINNER = 4096
SSM_HEADDIM = 64
SSM_HEADS = 64
SSM_GROUPS = 8
SSM_DSTATE = 128
SSM_CONV = 4
SSM_CHUNK = 256
SSM_CONV_DIM = 6144
D_FF = 5632
FFN_CONV = 3
DT_PAD = 128

ADAM_LR = 0.001
ADAM_B1 = 0.9
ADAM_B2 = 0.999
ADAM_EPS = 1e-08
ADAM_WD = 0.01
ADAM_STEP = 10

VMEM_LIMIT = 56 * 1024 * 1024
HI = lax.Precision.HIGHEST


def _cp(sem, **kw):
    return pltpu.CompilerParams(dimension_semantics=sem, vmem_limit_bytes=VMEM_LIMIT, **kw)


_DIMS = {"nn": (((1,), (0,)), ((), ())), "nt": (((1,), (1,)), ((), ())), "tn": (((0,), (0,)), ((), ()))}


def mm(a, b, mode, *, name, out_dtype=F32, tm=512, tn=512, tk=None, acc=None, n_major=True,
       dims=None, a_off=(0, 0), b_off=(0, 0)):
    if dims is not None:
        M, N, K = dims
    else:
        if mode == "nn":
            (M, K), (K2, N) = a.shape, b.shape
        elif mode == "nt":
            (M, K), (N, K2) = a.shape, b.shape
        else:
            (K, M), (K2, N) = a.shape, b.shape
        assert K == K2, (a.shape, b.shape, mode)
    tm, tn = min(tm, M), min(tn, N)
    tk = K if tk is None else min(tk, K)
    assert M % tm == 0 and N % tn == 0 and K % tk == 0, (M, N, K, tm, tn, tk)
    a_blk = (tk, tm) if mode == "tn" else (tm, tk)
    b_blk = (tn, tk) if mode == "nt" else (tk, tn)
    assert all(o % s == 0 for o, s in zip(a_off, a_blk)) and all(o % s == 0 for o, s in zip(b_off, b_blk))
    ao0, ao1 = a_off[0] // a_blk[0], a_off[1] // a_blk[1]
    bo0, bo1 = b_off[0] // b_blk[0], b_off[1] // b_blk[1]
    nk = K // tk
    if n_major:
        grid = (N // tn, M // tm, nk)
        ij = lambda p0, p1: (p1, p0)
    else:
        grid = (M // tm, N // tn, nk)
        ij = lambda p0, p1: (p0, p1)

    def a_map(p0, p1, k):
        i, _ = ij(p0, p1)
        return (k + ao0, i + ao1) if mode == "tn" else (i + ao0, k + ao1)

    def b_map(p0, p1, k):
        _, j = ij(p0, p1)
        return (j + bo0, k + bo1) if mode == "nt" else (k + bo0, j + bo1)

    def o_map(p0, p1, k):
        return ij(p0, p1)

    a_spec = pl.BlockSpec(a_blk, a_map)
    b_spec = pl.BlockSpec(b_blk, b_map)
    o_spec = pl.BlockSpec((tm, tn), o_map)
    dims = _DIMS[mode]
    has_acc = acc is not None

    def body(*refs):
        if has_acc:
            a_ref, b_ref, c_ref, o_ref, acc_ref = refs
        else:
            a_ref, b_ref, o_ref, acc_ref = refs
        k = pl.program_id(2)
        part = lax.dot_general(a_ref[...], b_ref[...], dims, preferred_element_type=F32)

        @pl.when(k == 0)
        def _():
            acc_ref[...] = part

        @pl.when(k > 0)
        def _():
            acc_ref[...] += part

        @pl.when(k == nk - 1)
        def _():
            r = acc_ref[...]
            if has_acc:
                r = r + c_ref[...].astype(F32)
            o_ref[...] = r.astype(out_dtype)

    in_specs = [a_spec, b_spec] + ([o_spec] if has_acc else [])
    args = (a, b) + ((acc,) if has_acc else ())
    return pl.pallas_call(
        body, name=name, grid=grid, in_specs=in_specs, out_specs=o_spec,
        out_shape=jax.ShapeDtypeStruct((M, N), out_dtype),
        scratch_shapes=[pltpu.VMEM((tm, tn), F32)],
        compiler_params=_cp(("parallel", "parallel", "arbitrary")),
    )(*args)


def mm_segments(name, segs, bs, *, tm, tn, tk, acc=None, exchange=None):
    m_, n_ = segs[0][0].shape[0], bs[0].shape[1]
    tm, tn = min(tm, m_), min(tn, n_)
    op = _ChipExchange(exchange) if exchange else None
    ne = (len(exchange) if exchange else 0)
    na = 0 if acc is None else 1
    steps, k0 = [], 0
    for a, bi, row in segs:
        w = a.shape[1]
        tks = min(tk, w)
        assert w % tks == 0 and row % tks == 0 and tks == min(tk, bs[bi].shape[0]), (w, row, tks)
        steps.append((k0, w // tks, tks, bi, row // tks))
        k0 += w // tks
    nk = k0
    assert m_ % tm == 0 and n_ % tn == 0

    def a_spec(k_first, count, tks):
        return pl.BlockSpec((tm, tks), lambda j, i, k: (i, jnp.clip(k - k_first, 0, count - 1)))

    def b_spec(bi):
        mine = [s for s in steps if s[3] == bi]

        def index(j, i, k):
            blk = mine[0][4]
            for k_first, count, _, _, first_blk in mine:
                blk = jnp.where(k >= k_first, first_blk + jnp.minimum(k - k_first, count - 1), blk)
            return (blk, j)
        return pl.BlockSpec((mine[0][2], tn), index)

    ns = len(segs)

    nb = len(bs)
    grid = (n_ // tn, m_ // tm, nk)

    def body(*refs):
        a_refs, b_refs = refs[:ns], refs[ns:ns + nb]
        acc_in = refs[ns + nb] if na else None
        rest = refs[ns + nb + na:]
        ex_refs, o_ref, got_refs, acc_ref, sems = rest[:ne], rest[ne], rest[ne + 1:2 * ne + 1], rest[2 * ne + 1], rest[2 * ne + 2:]
        k = pl.program_id(2)
        if op:
            first = (pl.program_id(0) == 0) & (pl.program_id(1) == 0) & (k == 0)
            last = (pl.program_id(0) == grid[0] - 1) & (pl.program_id(1) == grid[1] - 1) & (k == nk - 1)

            @pl.when(first)
            def _():
                op.start(ex_refs, got_refs, *sems)

            @pl.when(last)
            def _():
                op.finish(ex_refs, got_refs, *sems)

        @pl.when(k == 0)
        def _():
            acc_ref[...] = acc_in[...] if na else jnp.zeros_like(acc_ref)

        for a_ref, (k_first, count, _, bi, _) in zip(a_refs, steps):
            @pl.when((k >= k_first) & (k < k_first + count))
            def _(a_ref=a_ref, bi=bi):
                acc_ref[...] += jnp.dot(a_ref[...], b_refs[bi][...], preferred_element_type=F32)

        @pl.when(k == nk - 1)
        def _():
            o_ref[...] = acc_ref[...]

    any_spec = pl.BlockSpec(memory_space=pl.ANY)
    o_spec = pl.BlockSpec((tm, tn), lambda j, i, k: (i, j))
    outs = pl.pallas_call(
        body, name=name, grid=grid,
        in_specs=[a_spec(s[0], s[1], s[2]) for s in steps] + [b_spec(bi) for bi in range(nb)] + [o_spec] * na + [any_spec] * ne,
        out_specs=[o_spec] + [any_spec] * ne,
        out_shape=[jax.ShapeDtypeStruct((m_, n_), F32)] + (op.out_shape if op else []),
        scratch_shapes=[pltpu.VMEM((tm, tn), F32)] + (_sem_pair(op.n_sem) if op else []),
        compiler_params=_cp(("arbitrary", "arbitrary", "arbitrary")),
    )(*[a for a, _, _ in segs], *bs, *(() if acc is None else (acc,)), *(exchange or ()))
    return outs if op else outs[0]


def _dims(mode, ndim):
    if ndim == 2:
        return _DIMS[mode]
    (ca,), (cb,) = _DIMS[mode][0]
    return (((ca + 1,), (cb + 1,)), ((0,), (0,)))


def _bdot_plain(a, b, mode):
    return lax.dot_general(a.astype(BF16), b.astype(BF16), _dims(mode, a.ndim), preferred_element_type=F32)


@functools.partial(jax.custom_vjp, nondiff_argnums=(2,))
def _bdot_vjp(a, b, mode):
    return _bdot_plain(a, b, mode)


def _bdot_fwd(a, b, mode):
    return _bdot_plain(a, b, mode), (a, b)


def _bdot_bwd(mode, res, g):
    a, b = res
    if mode == "nn":
        return _bdot_plain(g, b, "nt"), _bdot_plain(a, g, "tn")
    if mode == "nt":
        return _bdot_plain(g, b, "nn"), _bdot_plain(g, a, "tn")
    return _bdot_plain(b, g, "nt"), _bdot_plain(a, g, "nn")


_bdot_vjp.defvjp(_bdot_fwd, _bdot_bwd)


def _split3(x):
    x1 = x.astype(BF16)
    r1 = x - x1.astype(F32)
    x2 = r1.astype(BF16)
    return x1, x2, (r1 - x2.astype(F32)).astype(BF16)


def _hdot_impl(a, b, mode, data):
    dims = _dims(mode, a.ndim)
    if data == "a":
        sel = b.astype(BF16)
        parts = [lax.dot_general(p, sel, dims, preferred_element_type=F32) for p in _split3(a)]
    else:
        sel = a.astype(BF16)
        parts = [lax.dot_general(sel, p, dims, preferred_element_type=F32) for p in _split3(b)]
    return (parts[2] + parts[1]) + parts[0]


@functools.partial(jax.custom_vjp, nondiff_argnums=(2, 3))
def _hdot(a, b, mode="nn", data="b"):
    return _hdot_impl(a, b, mode, data)


def _hdot_fwd(a, b, mode, data):
    return _hdot_impl(a, b, mode, data), (a, b)


def _hdot_bwd(mode, data, res, g):
    a, b = res
    if data == "a":
        da = {"nn": lambda: _hdot_impl(g, b, "nt", "a"), "nt": lambda: _hdot_impl(g, b, "nn", "a"),
              "tn": lambda: _hdot_impl(b, g, "nt", "b")}[mode]()
        return da, jnp.zeros_like(b)
    db = {"nn": lambda: _hdot_impl(a, g, "tn", "b"), "nt": lambda: _hdot_impl(g, a, "tn", "a"),
          "tn": lambda: _hdot_impl(a, g, "nn", "b")}[mode]()
    return jnp.zeros_like(a), db


_hdot.defvjp(_hdot_fwd, _hdot_bwd)


def _sigmoid(x):
    return 1.0 / (1.0 + jnp.exp(-x))


def _silu(x):
    return x * _sigmoid(x)


def _iota(shape, dim):
    return lax.broadcasted_iota(jnp.int32, shape, dim)


def _rms(x, w):
    return x * lax.rsqrt(jnp.mean(x * x, axis=-1, keepdims=True) + EPS) * w


def _hg_chunk(q_raw, f_raw, v, g, st, t0, t1, nw, dot):
    nhd, c = q_raw.shape[0], q_raw.shape[1]
    m = jnp.maximum(t0, t1)
    e0, e1 = jnp.exp(t0 - m), jnp.exp(t1 - m)
    lb = e0 / (e0 + e1)
    f = lb + (1.0 - lb) * _sigmoid(f_raw)
    k = 1.0 - f
    lf = jnp.log(f)
    qh = _silu(q_raw) * (HG_DK ** -0.5)
    row, col = _iota((c, c), 0), _iota((c, c), 1)
    causal = col <= row
    tril = jnp.broadcast_to(jnp.where(causal, 1.0, 0.0).astype(F32), (nhd, c, c))
    trilb = jnp.broadcast_to(jnp.where(causal & (col // HG_SUB == row // HG_SUB), 1.0, 0.0).astype(F32), (nhd, c, c))
    b = _hdot(tril, lf)
    bl = _hdot(trilb, lf)
    a_row = b - bl
    rid = _iota((c, HG_DK), 0)
    qt = qh * jnp.exp(bl)
    kt = k * jnp.exp(-bl)
    scores = jnp.zeros((nhd, c, c), F32)
    for j in range(c // HG_SUB):
        if j == 0:
            qj = qt * jnp.exp(jnp.minimum(a_row, 0.0))
        else:
            a_j = jnp.sum(jnp.where(rid == j * HG_SUB - 1, b, 0.0), axis=1, keepdims=True)
            qj = qt * jnp.exp(jnp.minimum(a_row - a_j, 0.0))
        kj = jnp.where(rid // HG_SUB == j, kt, 0.0)
        scores = scores + dot(qj, kj, "nt")
    scores = jnp.where(causal, scores, 0.0)
    o = dot(scores, v, "nn") + dot(qh * jnp.exp(b), st, "nt")
    b_last = jnp.sum(jnp.where(rid == c - 1, b, 0.0), axis=1, keepdims=True)
    st_new = st * jnp.exp(b_last) + dot(v, k * jnp.exp(b_last - b), "tn")
    y = _rms(o, nw) * _silu(g)
    return y, st_new


HG_HPS = 8
HG_W = HG_HPS * HG_DK


def hgrn2_fwd(qfig, table, nw, *, step_chunks=2, gather=None):
    t = qfig.shape[0]
    rows = HG_CHUNK * step_chunks
    nsteps = t // rows
    nh = HG_HEADS // HG_HPS
    op = _Gather(*gather) if gather else None
    ng = len(gather[0]) if gather else 0

    def body(*refs):
        q_ref, f_ref, v_ref, g_ref, tab_ref, nw_ref = refs[:6]
        p_refs = refs[6:6 + ng]
        y_ref, s_ref = refs[6 + ng:8 + ng]
        got_refs = refs[8 + ng:8 + 2 * ng]
        st_scr = refs[8 + 2 * ng]
        sems = refs[9 + 2 * ng:]
        first_step = (pl.program_id(0) == 0) & (pl.program_id(1) == 0)
        last_step = (pl.program_id(0) == nh - 1) & (pl.program_id(1) == nsteps - 1)
        if op:
            @pl.when(first_step)
            def _():
                op.start(p_refs, got_refs, *sems)

        @pl.when(pl.program_id(1) == 0)
        def _():
            st_scr[...] = jnp.zeros_like(st_scr)

        nwv = nw_ref[...]
        lanes = [pl.ds(hh * HG_DK, HG_DK) for hh in range(HG_HPS)]
        t0 = jnp.stack([tab_ref[0:1, ln] for ln in lanes])
        t1 = jnp.stack([tab_ref[1:2, ln] for ln in lanes])
        for c in range(step_chunks):
            sl = pl.ds(c * HG_CHUNK, HG_CHUNK)
            heads = lambda ref: jnp.stack([ref[sl, ln] for ln in lanes])
            st = st_scr[...]
            for hh in range(HG_HPS):
                s_ref[hh, c] = st[hh]
            y, st_new = _hg_chunk(heads(q_ref), heads(f_ref), heads(v_ref), heads(g_ref), st, t0, t1, nwv, _bdot_vjp)
            for hh, ln in enumerate(lanes):
                y_ref[sl, ln] = y[hh].astype(BF16)
            st_scr[...] = st_new

        if op:
            @pl.when(last_step)
            def _():
                op.finish(p_refs, got_refs, *sems)

    blk = lambda off: pl.BlockSpec((rows, HG_W), lambda h, c, off=off: (c, off + h))
    return pl.pallas_call(
        body, name="hgrn2_fwd", grid=(nh, nsteps),
        in_specs=[blk(0), blk(nh), blk(2 * nh), blk(3 * nh),
                  pl.BlockSpec((2, HG_W), lambda h, c: (0, h)), pl.BlockSpec((1, HG_DK), lambda h, c: (0, 0))] + [ANY] * ng,
        out_specs=[pl.BlockSpec((rows, HG_W), lambda h, c: (c, h)),
                   pl.BlockSpec((HG_HPS, step_chunks, HG_DK, HG_DK), lambda h, c: (h, c, 0, 0))] + [ANY] * ng,
        out_shape=[jax.ShapeDtypeStruct((t, HG_HEADS * HG_DK), BF16),
                   jax.ShapeDtypeStruct((HG_HEADS, t // HG_CHUNK, HG_DK, HG_DK), F32)] + (op.out_shape if op else []),
        scratch_shapes=[pltpu.VMEM((HG_HPS, HG_DK, HG_DK), F32)] + (_sem_pair(op.n_sem) if op else []),
        compiler_params=_cp(("arbitrary", "arbitrary")),
    )(qfig, qfig, qfig, qfig, table, nw, *(gather[0] if gather else ()))


def hgrn2_bwd(qfig, table, nw, states, dy, *, step_chunks=2):
    t = qfig.shape[0]
    rows = HG_CHUNK * step_chunks
    nsteps = t // rows
    nh = HG_HEADS // HG_HPS

    def body(q_ref, f_ref, v_ref, g_ref, tab_ref, nw_ref, s_ref, dy_ref,
             dq_ref, df_ref, dv_ref, dg_ref, dtab_ref, dnw_ref, dst_scr):
        @pl.when(pl.program_id(1) == 0)
        def _():
            dst_scr[...] = jnp.zeros_like(dst_scr)
            dtab_ref[...] = jnp.zeros_like(dtab_ref)
            dnw_ref[...] = jnp.zeros_like(dnw_ref)

        nwv = nw_ref[...]
        fn = functools.partial(_hg_chunk, dot=_bdot_vjp)
        lanes = [pl.ds(hh * HG_DK, HG_DK) for hh in range(HG_HPS)]
        t0 = jnp.stack([tab_ref[0:1, ln] for ln in lanes])
        t1 = jnp.stack([tab_ref[1:2, ln] for ln in lanes])
        for c in reversed(range(step_chunks)):
            sl = pl.ds(c * HG_CHUNK, HG_CHUNK)
            heads = lambda ref: jnp.stack([ref[sl, ln] for ln in lanes])
            _, vjp = jax.vjp(fn, heads(q_ref), heads(f_ref), heads(v_ref), heads(g_ref), s_ref[:, c], t0, t1, nwv)
            dq, df, dv, dg, dst, dt0, dt1, dnw = vjp((heads(dy_ref).astype(F32), dst_scr[...]))
            for hh, ln in enumerate(lanes):
                dq_ref[sl, ln] = dq[hh].astype(BF16)
                df_ref[sl, ln] = df[hh].astype(BF16)
                dv_ref[sl, ln] = dv[hh].astype(BF16)
                dg_ref[sl, ln] = dg[hh].astype(BF16)
                dtab_ref[0:1, ln] += dt0[hh]
                dtab_ref[1:2, ln] += dt1[hh]
            dst_scr[...] = dst
            dnw_ref[0] += dnw

    rev = lambda c: nsteps - 1 - c
    blk = lambda off: pl.BlockSpec((rows, HG_W), lambda h, c, off=off: (rev(c), off + h))
    oblk = lambda: pl.BlockSpec((rows, HG_W), lambda h, c: (rev(c), h))
    d = HG_HEADS * HG_DK
    outs = pl.pallas_call(
        body, name="hgrn2_bwd", grid=(nh, nsteps),
        in_specs=[blk(0), blk(nh), blk(2 * nh), blk(3 * nh),
                  pl.BlockSpec((2, HG_W), lambda h, c: (0, h)), pl.BlockSpec((1, HG_DK), lambda h, c: (0, 0)),
                  pl.BlockSpec((HG_HPS, step_chunks, HG_DK, HG_DK), lambda h, c: (h, rev(c), 0, 0)),
                  pl.BlockSpec((rows, HG_W), lambda h, c: (rev(c), h))],
        out_specs=[oblk(), oblk(), oblk(), oblk(),
                   pl.BlockSpec((2, HG_W), lambda h, c: (0, h)),
                   pl.BlockSpec((HG_HPS, 1, HG_DK), lambda h, c: (h, 0, 0))],
        out_shape=[jax.ShapeDtypeStruct((t, d), BF16)] * 4
        + [jax.ShapeDtypeStruct((2, d), F32), jax.ShapeDtypeStruct((HG_HEADS, 1, HG_DK), F32)],
        scratch_shapes=[pltpu.VMEM((HG_HPS, HG_DK, HG_DK), F32)],
        compiler_params=_cp(("parallel", "arbitrary")),
    )(qfig, qfig, qfig, qfig, table, nw, states, dy)
    return outs


def _ssd_chunk(xs2, dt, acum, bm, cm, s2, pair0, dot):
    npr, c = xs2.shape[0], xs2.shape[1]
    sh_e, sh_s = (npr, DT_PAD, 128), (npr, 8, DT_PAD)
    first_head = 2 * (pair0 + _iota(sh_e, 0))
    expand = jnp.where(_iota(sh_e, 1) == first_head + _iota(sh_e, 2) // SSM_HEADDIM, 1.0, 0.0).astype(F32)
    sel = (_iota(sh_s, 2) == 2 * (pair0 + _iota(sh_s, 0)) + _iota(sh_s, 1)) & (_iota(sh_s, 1) < 2)
    sel = jnp.where(sel, 1.0, 0.0).astype(F32)
    per_pair = lambda a: jnp.broadcast_to(a, (npr,) + a.shape)
    dtx = _hdot(per_pair(dt), expand, "nn", "a")
    acol = _hdot(per_pair(acum), expand, "nn", "a")
    arow8 = _hdot(sel, per_pair(acum), "nt", "b")
    row, col = _iota((c, c), 0), _iota((c, c), 1)
    causal = col <= row
    cb = dot(cm, bm, "nt")
    x2 = xs2 * dtx
    lane_c = _iota((c, 128), 1)
    y = dot(per_pair(cm), s2, "nn") * jnp.exp(acol)
    for r in range(2):
        head = (lane_c // SSM_HEADDIM) == r
        a_c = jnp.sum(jnp.where(head & (lane_c % SSM_HEADDIM == 0), acol, 0.0), axis=2, keepdims=True)
        a_r = jnp.sum(jnp.where(_iota((8, c), 0) == r, arow8, 0.0), axis=1, keepdims=True)
        decay = jnp.where(causal, jnp.exp(jnp.minimum(a_c - a_r, 0.0)), 0.0)
        y = y + dot(cb * decay, jnp.where(head, x2, 0.0), "nn")
    a_last = jnp.sum(jnp.where(_iota((c, 128), 0) == c - 1, acol, 0.0), axis=1, keepdims=True)
    s2_new = s2 * jnp.exp(a_last) + dot(per_pair(bm), x2 * jnp.exp(a_last - acol), "tn")
    return y, s2_new


SSM_PAIRS = SSM_HEADS // 2
PAIRS_PER_GROUP = SSM_PAIRS // SSM_GROUPS
SSD_PPS = 4
SSD_W = SSD_PPS * 128
_XS_BLOCKS = SSM_DINNER // 128
_B_BLOCK0 = _XS_BLOCKS
_C_BLOCK0 = _XS_BLOCKS + SSM_GROUPS


def ssd_fwd(xbc_act, dt, acum):
    t = xbc_act.shape[0]
    nc = t // SSM_CHUNK
    c_ = SSM_CHUNK

    def body(xs_ref, b_ref, c_ref, dt_ref, ac_ref, y_ref, s_ref, s_scr):
        q = pl.program_id(1)
        mine = pl.ds(SSD_PPS * q, SSD_PPS)
        lanes = [pl.ds(r * 128, 128) for r in range(SSD_PPS)]

        @pl.when(pl.program_id(0) == 0)
        def _():
            s_scr[mine] = jnp.zeros((SSD_PPS, SSM_DSTATE, 128), F32)

        s2 = s_scr[mine]
        s_ref[...] = s2
        xs = jnp.stack([xs_ref[:, ln] for ln in lanes])
        y, s2_new = _ssd_chunk(xs, dt_ref[...], ac_ref[...], b_ref[...], c_ref[...], s2, SSD_PPS * q, _bdot_vjp)
        for r, ln in enumerate(lanes):
            y_ref[:, ln] = y[r]
        s_scr[mine] = s2_new

    grp = lambda q: q // (PAIRS_PER_GROUP // SSD_PPS)
    return pl.pallas_call(
        body, name="ssd_fwd", grid=(nc, SSM_PAIRS // SSD_PPS),
        in_specs=[pl.BlockSpec((c_, SSD_W), lambda c, q: (c, q)),
                  pl.BlockSpec((c_, 128), lambda c, q: (c, _B_BLOCK0 + grp(q))),
                  pl.BlockSpec((c_, 128), lambda c, q: (c, _C_BLOCK0 + grp(q))),
                  pl.BlockSpec((c_, DT_PAD), lambda c, q: (c, 0)),
                  pl.BlockSpec((c_, DT_PAD), lambda c, q: (c, 0))],
        out_specs=[pl.BlockSpec((c_, SSD_W), lambda c, q: (c, q)),
                   pl.BlockSpec((None, SSD_PPS, SSM_DSTATE, 128), lambda c, q: (c, q, 0, 0))],
        out_shape=[jax.ShapeDtypeStruct((t, SSM_DINNER), F32),
                   jax.ShapeDtypeStruct((nc, SSM_PAIRS, SSM_DSTATE, 128), F32)],
        scratch_shapes=[pltpu.VMEM((SSM_PAIRS, SSM_DSTATE, 128), F32)],
        compiler_params=_cp(("arbitrary", "arbitrary")),
    )(xbc_act, xbc_act, xbc_act, dt, acum)


def ssd_bwd(xbc_act, dt, acum, states, dy, dskip, *, exchange=None):
    t = xbc_act.shape[0]
    nc = t // SSM_CHUNK
    c_ = SSM_CHUNK
    rev = lambda c: nc - 1 - c
    nq = SSM_PAIRS // SSD_PPS
    op = _ChipExchange(exchange) if exchange else None
    ne = len(exchange) if exchange else 0

    def body(*refs):
        xs_ref, b_ref, c_ref, dt_ref, ac_ref, s_ref, dy_ref, sk_ref = refs[:8]
        ex_refs = refs[8:8 + ne]
        dxs_ref, db_ref, dc_ref, ddt_ref, dac_ref = refs[8 + ne:13 + ne]
        got_refs = refs[13 + ne:13 + 2 * ne]
        ds_scr = refs[13 + 2 * ne]
        sems = refs[14 + 2 * ne:]
        q = pl.program_id(1)
        if op:
            @pl.when((pl.program_id(0) == 0) & (q == 0))
            def _():
                op.start(ex_refs, got_refs, *sems)

            @pl.when((pl.program_id(0) == nc - 1) & (q == nq - 1))
            def _():
                op.finish(ex_refs, got_refs, *sems)

        assert SSD_PPS == PAIRS_PER_GROUP
        mine = pl.ds(SSD_PPS * q, SSD_PPS)
        lanes = [pl.ds(r * 128, 128) for r in range(SSD_PPS)]

        @pl.when(pl.program_id(0) == 0)
        def _():
            ds_scr[mine] = jnp.zeros((SSD_PPS, SSM_DSTATE, 128), F32)

        fn = functools.partial(_ssd_chunk, pair0=SSD_PPS * q, dot=_bdot_vjp)
        xs = jnp.stack([xs_ref[:, ln] for ln in lanes])
        dy = jnp.stack([dy_ref[:, ln] for ln in lanes])
        _, vjp = jax.vjp(fn, xs, dt_ref[...], ac_ref[...], b_ref[...], c_ref[...], s_ref[...])
        dxs, ddt, dac, db, dc, ds = vjp((dy, ds_scr[mine]))
        for r, ln in enumerate(lanes):
            dxs_ref[:, ln] = dxs[r] + sk_ref[:, ln]
        ds_scr[mine] = ds
        db_ref[...] = db
        dc_ref[...] = dc

        @pl.when(q == 0)
        def _():
            ddt_ref[...] = ddt
            dac_ref[...] = dac

        @pl.when(q != 0)
        def _():
            ddt_ref[...] += ddt
            dac_ref[...] += dac

    grp = lambda q: q // (PAIRS_PER_GROUP // SSD_PPS)
    return pl.pallas_call(
        body, name="ssd_bwd", grid=(nc, SSM_PAIRS // SSD_PPS),
        in_specs=[pl.BlockSpec((c_, SSD_W), lambda c, q: (rev(c), q)),
                  pl.BlockSpec((c_, 128), lambda c, q: (rev(c), _B_BLOCK0 + grp(q))),
                  pl.BlockSpec((c_, 128), lambda c, q: (rev(c), _C_BLOCK0 + grp(q))),
                  pl.BlockSpec((c_, DT_PAD), lambda c, q: (rev(c), 0)),
                  pl.BlockSpec((c_, DT_PAD), lambda c, q: (rev(c), 0)),
                  pl.BlockSpec((None, SSD_PPS, SSM_DSTATE, 128), lambda c, q: (rev(c), q, 0, 0)),
                  pl.BlockSpec((c_, SSD_W), lambda c, q: (rev(c), q)),
                  pl.BlockSpec((c_, SSD_W), lambda c, q: (rev(c), q))] + [ANY] * ne,
        out_specs=[pl.BlockSpec((c_, SSD_W), lambda c, q: (rev(c), q)),
                   pl.BlockSpec((c_, 128), lambda c, q: (rev(c), grp(q))),
                   pl.BlockSpec((c_, 128), lambda c, q: (rev(c), grp(q))),
                   pl.BlockSpec((c_, DT_PAD), lambda c, q: (rev(c), 0)),
                   pl.BlockSpec((c_, DT_PAD), lambda c, q: (rev(c), 0))] + [ANY] * ne,
        out_shape=[jax.ShapeDtypeStruct((t, SSM_DINNER), F32),
                   jax.ShapeDtypeStruct((t, SSM_GROUPS * SSM_DSTATE), F32),
                   jax.ShapeDtypeStruct((t, SSM_GROUPS * SSM_DSTATE), F32),
                   jax.ShapeDtypeStruct((t, DT_PAD), F32),
                   jax.ShapeDtypeStruct((t, DT_PAD), F32)] + (op.out_shape if op else []),
        scratch_shapes=[pltpu.VMEM((SSM_PAIRS, SSM_DSTATE, 128), F32)] + (_sem_pair(op.n_sem) if op else []),
        compiler_params=_cp(("arbitrary", "arbitrary")),
    )(xbc_act, xbc_act, xbc_act, dt, acum, states, dy, dskip, *(exchange or ()))


def rowwise(name, fn, row_ins, par_ins, row_outs, acc_outs, *, tt, ncb=1):
    t = row_ins[0][0].shape[0]
    assert t % tt == 0
    n_ri, n_pi, n_ro, n_ao = len(row_ins), len(par_ins), len(row_outs), len(acc_outs)

    def body(*refs):
        i = pl.program_id(1)
        ins = [r[...] for r in refs[:n_ri + n_pi]]
        outs = fn(*ins)
        ro_refs = refs[n_ri + n_pi:n_ri + n_pi + n_ro]
        ao_refs = refs[n_ri + n_pi + n_ro:]
        for r, v in zip(ro_refs, outs[:n_ro]):
            r[...] = v.astype(r.dtype)
        for r, v in zip(ao_refs, outs[n_ro:]):
            @pl.when(i == 0)
            def _(r=r, v=v):
                r[...] = v

            @pl.when(i > 0)
            def _(r=r, v=v):
                r[...] += v

    in_specs = [pl.BlockSpec((tt, bc), lambda j, i, off=off: (i, off + j)) for _, bc, off in row_ins]
    in_specs += [pl.BlockSpec((a.shape[0], bc), lambda j, i, off=off: (0, off + j)) for a, bc, off in par_ins]
    out_specs = [pl.BlockSpec((tt, bc), lambda j, i: (i, j)) for _, bc, _ in row_outs]
    out_specs += [pl.BlockSpec((r, bc), lambda j, i: (0, j)) for r, _, bc in acc_outs]
    out_shape = [jax.ShapeDtypeStruct((t, c), dt) for c, _, dt in row_outs]
    out_shape += [jax.ShapeDtypeStruct((r, c), F32) for r, c, _ in acc_outs]
    return pl.pallas_call(
        body, name=name, grid=(ncb, t // tt), in_specs=in_specs, out_specs=out_specs, out_shape=out_shape,
        compiler_params=_cp(("parallel", "arbitrary")),
    )(*[a for a, _, _ in row_ins], *[a for a, _, _ in par_ins])


def _colsum(v):
    return jnp.sum(v, axis=0, keepdims=True)


def _softplus(x):
    return jnp.maximum(x, 0.0) + jnp.log(1.0 + jnp.exp(-jnp.abs(x)))


def _gelu_tanh(x):
    return 0.5 * x * (1.0 + jnp.tanh(0.7978845608028654 * (x + 0.044715 * (x * x * x))))


D = D_MODEL


def norm_fwd(x, w):
    return rowwise("norm_fwd", lambda xv, wv: (_rms(xv, wv),), [(x, D, 0)], [(w, D, 0)], [(D, D, BF16)], [], tt=256)[0]


def norm_bwd(x, w, dh, dres):
    def fn(xv, dhv, drv, wv):
        _, vjp = jax.vjp(_rms, xv, wv)
        dx, dw = vjp(dhv)
        return dx + drv, dw
    return rowwise("norm_bwd", fn, [(x, D, 0), (dh, D, 0), (dres, D, 0)], [(w, D, 0)], [(D, D, F32)], [(1, D, D)], tt=256)


def _dt_fn(dtr, bias, a_log):
    c = dtr.shape[0]
    dt = _softplus(dtr + bias)
    da = dt * (-jnp.exp(a_log))
    tril = jnp.where(_iota((c, c), 1) <= _iota((c, c), 0), 1.0, 0.0).astype(F32)
    return dt, _hdot(tril, da)


def dt_fwd(dtr, bias, a_log):
    return rowwise("dt_fwd", _dt_fn, [(dtr, DT_PAD, 0)], [(bias, DT_PAD, 0), (a_log, DT_PAD, 0)],
                   [(DT_PAD, DT_PAD, F32), (DT_PAD, DT_PAD, F32)], [], tt=SSM_CHUNK)


def dt_bwd(dtr, bias, a_log, ddt, dacum):
    def fn(dtrv, ddtv, dacv, bv, av):
        _, vjp = jax.vjp(_dt_fn, dtrv, bv, av)
        return vjp((ddtv, dacv))
    return rowwise("dt_bwd", fn, [(dtr, DT_PAD, 0), (ddt, DT_PAD, 0), (dacum, DT_PAD, 0)],
                   [(bias, DT_PAD, 0), (a_log, DT_PAD, 0)],
                   [(DT_PAD, DT_PAD, BF16)], [(1, DT_PAD, DT_PAD), (1, DT_PAD, DT_PAD)], tt=SSM_CHUNK)


GROUP_W = SSM_DINNER // SSM_GROUPS


def _ssm_post_fn(yv, xsv, zv, dexp, nw):
    return _rms((yv + dexp * xsv) * _silu(zv), nw)


def ssm_post_fwd(yssd, xbc_act, z, dexp, nw):
    return rowwise("ssm_post_fwd", lambda *a: (_ssm_post_fn(*a),),
                   [(yssd, GROUP_W, 0), (xbc_act, GROUP_W, 0), (z, GROUP_W, 0)], [(dexp, GROUP_W, 0), (nw, GROUP_W, 0)],
                   [(SSM_DINNER, GROUP_W, BF16)], [], tt=512, ncb=SSM_GROUPS)[0]


def ssm_post_bwd(yssd, xbc_act, z, dexp, nw, dy):
    def fn(yv, xsv, zv, dyv, dv, nv):
        _, vjp = jax.vjp(_ssm_post_fn, yv, xsv, zv, dv, nv)
        return vjp(dyv.astype(F32))
    return rowwise("ssm_post_bwd", fn,
                   [(yssd, GROUP_W, 0), (xbc_act, GROUP_W, 0), (z, GROUP_W, 0), (dy, GROUP_W, 0)],
                   [(dexp, GROUP_W, 0), (nw, GROUP_W, 0)],
                   [(SSM_DINNER, GROUP_W, F32), (SSM_DINNER, GROUP_W, F32), (SSM_DINNER, GROUP_W, BF16)],
                   [(1, SSM_DINNER, GROUP_W), (1, SSM_DINNER, GROUP_W)], tt=512, ncb=SSM_GROUPS)


def _merge_fn(ah, asm, gh, gs):
    return _sigmoid(gh) * ah + _sigmoid(gs) * asm


def merge_fwd(a_hg, a_ssm, gates):
    return rowwise("merge_fwd", lambda *a: (_merge_fn(*a),), [(a_hg, D, 0), (a_ssm, D, 0), (gates, D, 0), (gates, D, 1)], [],
                   [(D, D, BF16)], [], tt=256)[0]


def merge_bwd(a_hg, a_ssm, gates, dmixed):
    def fn(ah, asm, gh, gs, dm):
        _, vjp = jax.vjp(_merge_fn, ah, asm, gh, gs)
        return vjp(dm)
    return rowwise("merge_bwd", fn, [(a_hg, D, 0), (a_ssm, D, 0), (gates, D, 0), (gates, D, 1), (dmixed, D, 0)], [],
                   [(D, D, BF16)] * 4, [], tt=256)


def _post1_fn(xv, uv, wpost, wpre):
    x1 = xv + _rms(uv, wpost)
    return x1, _rms(x1, wpre)


def post1_fwd(x, u, wpost, wpre):
    return rowwise("post1_fwd", _post1_fn, [(x, D, 0), (u, D, 0)], [(wpost, D, 0), (wpre, D, 0)],
                   [(D, D, F32), (D, D, BF16)], [], tt=256)


def post1_bwd(x, u, wpost, wpre, dx1, dh2):
    def fn(xv, uv, d1, d2, wa, wb):
        _, vjp = jax.vjp(_post1_fn, xv, uv, wa, wb)
        dx, du, dwa, dwb = vjp((d1, d2))
        return du, dx, dwa, dwb
    return rowwise("post1_bwd", fn, [(x, D, 0), (u, D, 0), (dx1, D, 0), (dh2, D, 0)], [(wpost, D, 0), (wpre, D, 0)],
                   [(D, D, BF16), (D, D, F32)], [(1, D, D), (1, D, D)], tt=256)


def final_fwd_bwd(x1, fo, w, target):
    def fn(x1v, fov, tv, wv):
        def loss_fn(a, b, c):
            err = a + _rms(b, c) - tv
            return 0.5 * jnp.sum(err * err) * (1.0 / D)
        loss, vjp = jax.vjp(loss_fn, x1v, fov, wv)
        dx, dfo, dw = vjp(jnp.ones((), F32))
        return dx, dfo, dw, jnp.full((1, 128), loss, F32)
    return rowwise("final_fwd_bwd", fn, [(x1, D, 0), (fo, D, 0), (target, D, 0)], [(w, D, 0)],
                   [(D, D, F32), (D, D, BF16)], [(1, D, D), (1, 128, 128)], tt=256)


HALO = 8
CONV_TT = 512
CONV_CB = 512


def _tail(kind, c, up):
    return _silu(c) if kind == "silu" else _gelu_tanh(c) * up


def conv_fwd(name, x, xoff, w, b, kind, up=None, upoff=0, act_dtype=F32):
    t = x.shape[0]
    k_, c_ = w.shape
    tt, cb = CONV_TT, CONV_CB
    hb = tt // HALO
    has_up = up is not None

    def body(*refs):
        if has_up:
            x_ref, xp_ref, w_ref, b_ref, up_ref, c_ref, a_ref, scr = refs
        else:
            x_ref, xp_ref, w_ref, b_ref, c_ref, a_ref, scr = refs
        i = pl.program_id(1)
        scr[0:HALO, :] = jnp.where(i == 0, 0.0, xp_ref[...])
        scr[HALO:HALO + tt, :] = x_ref[...]
        acc = jnp.zeros((tt, cb), F32) + b_ref[...]
        for k in range(k_):
            acc = acc + w_ref[k:k + 1, :] * scr[pl.ds(HALO - (k_ - 1) + k, tt), :]
        c_ref[...] = acc
        a_ref[...] = _tail(kind, acc, up_ref[...] if has_up else None).astype(act_dtype)

    in_specs = [pl.BlockSpec((tt, cb), lambda j, i: (i, xoff + j)),
                pl.BlockSpec((HALO, cb), lambda j, i: (jnp.maximum(i * hb - 1, 0), xoff + j)),
                pl.BlockSpec((k_, cb), lambda j, i: (0, j)),
                pl.BlockSpec((1, cb), lambda j, i: (0, j))]
    args = [x, x, w, b]
    if has_up:
        in_specs.append(pl.BlockSpec((tt, cb), lambda j, i: (i, upoff + j)))
        args.append(up)
    return pl.pallas_call(
        body, name=name, grid=(c_ // cb, t // tt), in_specs=in_specs,
        out_specs=[pl.BlockSpec((tt, cb), lambda j, i: (i, j))] * 2,
        out_shape=[jax.ShapeDtypeStruct((t, c_), F32), jax.ShapeDtypeStruct((t, c_), act_dtype)],
        scratch_shapes=[pltpu.VMEM((tt + HALO, cb), F32)],
        compiler_params=_cp(("parallel", "arbitrary")),
    )(*args)


def conv_bwd(name, x, xoff, c, coff, dact, w, kind, up=None, upoff=0):
    t = x.shape[0]
    k_, c_ = w.shape[0], dact.shape[1]
    tt, cb = CONV_TT, CONV_CB
    hb = tt // HALO
    nt = t // tt
    has_up = up is not None

    def tail_grad(cv, dav, upv):
        if has_up:
            _, vjp = jax.vjp(lambda a, u: _tail(kind, a, u), cv, upv)
            return vjp(dav)
        _, vjp = jax.vjp(lambda a: _tail(kind, a, None), cv)
        return vjp(dav)[0], None

    def body(*refs):
        if has_up:
            (x_ref, xp_ref, c_ref, cn_ref, da_ref, dan_ref, w_ref, up_ref, upn_ref,
             dx_ref, dup_ref, dw_ref, db_ref, xs, dcs) = refs
        else:
            x_ref, xp_ref, c_ref, cn_ref, da_ref, dan_ref, w_ref, dx_ref, dw_ref, db_ref, xs, dcs = refs
        i = pl.program_id(1)
        xs[0:HALO, :] = jnp.where(i == 0, 0.0, xp_ref[...])
        xs[HALO:HALO + tt, :] = x_ref[...]
        dc, dup = tail_grad(c_ref[...], da_ref[...].astype(F32), up_ref[...] if has_up else None)
        dcn, _ = tail_grad(cn_ref[...], dan_ref[...].astype(F32), upn_ref[...] if has_up else None)
        dcs[0:tt, :] = dc
        dcs[tt:tt + HALO, :] = jnp.where(i == nt - 1, 0.0, dcn)
        if has_up:
            dup_ref[...] = dup.astype(BF16)
        dx = jnp.zeros((tt, cb), F32)
        dws = []
        for k in range(k_):
            dx = dx + w_ref[k:k + 1, :] * dcs[pl.ds(k_ - 1 - k, tt), :]
            dws.append(_colsum(dc * xs[pl.ds(HALO - (k_ - 1) + k, tt), :]))
        dx_ref[...] = dx.astype(BF16)

        @pl.when(i == 0)
        def _():
            dw_ref[...] = jnp.zeros_like(dw_ref)
            db_ref[...] = jnp.zeros_like(db_ref)

        for k in range(k_):
            dw_ref[k:k + 1, :] += dws[k]
        db_ref[...] += _colsum(dc)

    tile = lambda off: pl.BlockSpec((tt, cb), lambda j, i, off=off: (i, off + j))
    prev = lambda off: pl.BlockSpec((HALO, cb), lambda j, i, off=off: (jnp.maximum(i * hb - 1, 0), off + j))
    nxt = lambda off: pl.BlockSpec((HALO, cb), lambda j, i, off=off: (jnp.minimum((i + 1) * hb, t // HALO - 1), off + j))
    in_specs = [tile(xoff), prev(xoff), tile(coff), nxt(coff), tile(0), nxt(0),
                pl.BlockSpec((k_, cb), lambda j, i: (0, coff + j))]
    args = [x, x, c, c, dact, dact, w]
    if has_up:
        in_specs += [tile(upoff), nxt(upoff)]
        args += [up, up]
    out_specs = [tile(0)] + ([tile(0)] if has_up else []) + [pl.BlockSpec((HALO, cb), lambda j, i: (0, j)),
                                                            pl.BlockSpec((1, cb), lambda j, i: (0, j))]
    out_shape = [jax.ShapeDtypeStruct((t, c_), BF16)] * (2 if has_up else 1)
    out_shape += [jax.ShapeDtypeStruct((HALO, c_), F32), jax.ShapeDtypeStruct((1, c_), F32)]
    return pl.pallas_call(
        body, name=name, grid=(c_ // cb, nt), in_specs=in_specs, out_specs=out_specs, out_shape=out_shape,
        scratch_shapes=[pltpu.VMEM((tt + HALO, cb), F32), pltpu.VMEM((tt + HALO, cb), F32)],
        compiler_params=_cp(("parallel", "arbitrary")),
    )(*args)


def ew_sum(name, parts, rows, out_dtype, tr):
    c = parts[0][0].shape[1]
    tr = min(tr, rows)
    assert rows % tr == 0 and all(off % tr == 0 for _, off in parts)
    n = len(parts)

    def body(*refs):
        acc = refs[0][...].astype(F32)
        for ref in refs[1:n]:
            acc = acc + ref[...].astype(F32)
        refs[n][...] = acc.astype(out_dtype)

    in_specs = [pl.BlockSpec((tr, c), lambda i, o=off // tr: (i + o, 0)) for _, off in parts]
    return pl.pallas_call(body, name=name, grid=(rows // tr,), in_specs=in_specs,
                          out_specs=pl.BlockSpec((tr, c), lambda i: (i, 0)),
                          out_shape=jax.ShapeDtypeStruct((rows, c), out_dtype),
                          compiler_params=_cp(("parallel",)))(*[a for a, _ in parts])


def fold_heads(dexp):
    def body(d_ref, o_ref):
        sel = jnp.where(_iota((SSM_DINNER, DT_PAD), 0) // SSM_HEADDIM == _iota((SSM_DINNER, DT_PAD), 1), 1.0, 0.0)
        o_ref[...] = _hdot(jnp.broadcast_to(d_ref[...], (8, SSM_DINNER)), sel.astype(F32), "nn", "a")[0:1, :]

    return pl.pallas_call(body, name="fold_heads", out_shape=jax.ShapeDtypeStruct((1, DT_PAD), F32),
                          compiler_params=pltpu.CompilerParams(vmem_limit_bytes=VMEM_LIMIT))(dexp)


def adamw(name, w, g, m, v, tr):
    r, c = w.shape
    tr = min(tr, r)
    assert r % tr == 0, (r, tr)

    def body(w_ref, g_ref, m_ref, v_ref, d_ref, nm_ref, nv_ref):
        gv = g_ref[...]
        nm = ADAM_B1 * m_ref[...] + (1.0 - ADAM_B1) * gv
        nv = ADAM_B2 * v_ref[...] + (1.0 - ADAM_B2) * (gv * gv)
        m_hat = nm / (1.0 - ADAM_B1 ** ADAM_STEP)
        v_hat = nv / (1.0 - ADAM_B2 ** ADAM_STEP)
        d_ref[...] = -ADAM_LR * (m_hat / (jnp.sqrt(v_hat) + ADAM_EPS) + ADAM_WD * w_ref[...])
        nm_ref[...] = nm
        nv_ref[...] = nv

    spec = pl.BlockSpec((tr, c), lambda i: (i, 0))
    shp = jax.ShapeDtypeStruct((r, c), F32)
    return pl.pallas_call(body, name=name, grid=(r // tr,), in_specs=[spec] * 4, out_specs=[spec] * 3,
                          out_shape=[shp] * 3, compiler_params=_cp(("parallel",)))(w, g, m, v)


SEG_QFIG, SEG_Z, SEG_XBC, SEG_DT, SEG_G = 0, 8192, 12288, 18432, 18496
IN_TOTAL = 22592
FFN_BLOCKS = D_FF // CONV_CB


def _own_slot(gathered, own, shard):
    slot = lax.broadcasted_iota(jnp.int32, (gathered.shape[0],) + (1,) * own.ndim, 0)
    return jnp.where(slot == shard, own[None], gathered)


def local_step(x, target, wts, par, p_rest, p_up, shard, core):
    t = x.shape[0]
    pad64 = lambda a: jnp.pad(a, ((0, 0), (0, DT_PAD - a.shape[1])))
    bias, a_log = pad64(par["ssm_dt_bias"]), pad64(par["ssm_A_log"])
    dexp = jnp.repeat(par["ssm_D"], SSM_HEADDIM, axis=1)
    in_t = wts["in_t"]

    h = norm_fwd(x, par["mix_pre_norm"])
    proj = lambda nm, off, n, tn: mm(h, in_t, "nt", name=nm, tn=tn, dims=(t, n, D), b_off=(off, 0))
    qfig = proj("proj_qfig", SEG_QFIG, 8192, 1024)
    z = proj("proj_z", SEG_Z, 4096, 1024)
    xbc = proj("proj_xbc", SEG_XBC, 6144, 1024)
    dtr = mm(h, wts["dt_t"], "nt", name="proj_dt", tn=128)
    gates = mm(h, wts["g_t"], "nt", name="proj_gates", tn=1024)
    y_hg, hg_states, g_rest, g_up = hgrn2_fwd(qfig, par["hg_lb_table"], par["hg_out_norm"],
                                              gather=([p_rest, p_up], [REST_PIECES, UP_PIECES]))
    g_rest, g_up = _own_slot(g_rest, p_rest, shard), _own_slot(g_up, p_up, shard)
    r0, r1, r2, r3 = REST_SPLITS
    wts = dict(wts, bh=g_rest[:, :r0].reshape(-1, D), bs=g_rest[:, r0:r1].reshape(-1, D), o=g_rest[:, r1:r2].reshape(-1, D),
               dn=g_rest[:, r2:r3].reshape(-1, D), up=jnp.transpose(g_up, (1, 0, 2)).reshape(D, 2 * D_FF),
               up_t=jnp.transpose(g_up, (0, 2, 1)).reshape(2 * D_FF, D))
    c_ssm, xbc_act = conv_fwd("ssm_conv_fwd", xbc, 0, par["ssm_conv_w"], par["ssm_conv_b"], "silu")
    dt, acum = dt_fwd(dtr, bias, a_log)
    yssd, ssd_states = ssd_fwd(xbc_act, dt, acum)
    y_ssm = ssm_post_fwd(yssd, xbc_act, z, dexp, par["ssm_out_norm"])
    a_hg = mm(y_hg, wts["bh"], "nn", name="branch_hg", tn=1024)
    a_ssm = mm(y_ssm, wts["bs"], "nn", name="branch_ssm", tn=1024)
    mixed = merge_fwd(a_hg, a_ssm, gates)
    u = mm(mixed, wts["o"], "nn", name="out_proj", tn=1024)
    x1, h2 = post1_fwd(x, u, par["mix_post_norm"], par["ffn_pre_norm"])
    gu = mm(h2, wts["up"], "nn", name="ffn_up", tn=1024)
    c_ffn, act = conv_fwd("ffn_conv_fwd", gu, 0, par["ffn_conv_w"], par["ffn_conv_b"], "gelu_mul",
                          up=gu, upoff=FFN_BLOCKS, act_dtype=BF16)
    fo = mm(act, wts["dn"], "nn", name="ffn_down", tn=1024)
    dx2, dfo, g_ffn_post, loss = final_fwd_bwd(x1, fo, par["ffn_post_norm"], target)

    dact = mm(dfo, wts["dn"], "nt", name="d_act", out_dtype=BF16, tn=1408)
    g_dn = mm(act, dfo, "tn", name="g_ffn_down", out_dtype=BF16, tm=1408, tn=2048, tk=1024)
    dgate, dup, g_fcw, g_fcb = conv_bwd("ffn_conv_bwd", gu, 0, c_ffn, 0, dact, par["ffn_conv_w"], "gelu_mul",
                                        up=gu, upoff=FFN_BLOCKS)
    dh2 = mm_segments("d_h2", [(dgate, 0, 0), (dup, 0, D_FF)], [wts["up_t"]], tm=1024, tn=1024, tk=1408)
    g_up_gate = mm(h2, dgate, "tn", name="g_ffn_up_gate", out_dtype=BF16, tm=2048, tn=1408, tk=1024)
    g_up_up = mm(h2, dup, "tn", name="g_ffn_up_up", out_dtype=BF16, tm=2048, tn=1408, tk=1024)
    du, dx1, g_mix_post, g_ffn_pre = post1_bwd(x, u, par["mix_post_norm"], par["ffn_pre_norm"], dx2, dh2)
    dmixed = mm(du, wts["o"], "nt", name="d_mixed", tn=1024)
    g_o = mm(mixed, du, "tn", name="g_w_out", out_dtype=BF16, tm=1024, tn=2048, tk=1024)
    da_hg, da_ssm, dg_hg, dg_ssm = merge_bwd(a_hg, a_ssm, gates, dmixed)
    dy_hg = mm(da_hg, wts["bh"], "nt", name="d_y_hg", out_dtype=BF16, tn=1024)
    g_bh = mm(y_hg, da_hg, "tn", name="g_w_branch_hg", out_dtype=BF16, tm=1024, tn=2048, tk=1024)
    dy_ssm = mm(da_ssm, wts["bs"], "nt", name="d_y_ssm", out_dtype=BF16, tn=1024)
    g_bs = mm(y_ssm, da_ssm, "tn", name="g_w_branch_ssm", out_dtype=BF16, tm=1024, tn=2048, tk=1024)
    dyssd, dskip, dz, g_dexp, g_ssm_norm = ssm_post_bwd(yssd, xbc_act, z, dexp, par["ssm_out_norm"], dy_ssm)

    gg_rest = jnp.concatenate([g.reshape(N_CHIPS, -1, D) for g in (g_bh, g_bs, g_o, g_dn)], axis=1)
    gg_up = jnp.transpose(jnp.concatenate([g_up_gate, g_up_up], axis=1).reshape(D, N_CHIPS, UP_COLS), (1, 0, 2))
    c_rest, c_up = pair_reduce("rest", [gg_rest, gg_up], [REST_PIECES, UP_PIECES], [432, 512], core)
    dxs, db_, dc_, ddt, dacum, rb_rest, rb_up = ssd_bwd(xbc_act, dt, acum, ssd_states, dyssd, dskip, exchange=[c_rest, c_up])
    red_rest, red_up = chip_reduce("rest", [c_rest, c_up], [rb_rest, rb_up], [432, 256], shard)
    ddtr, g_dt_bias, g_a_log = dt_bwd(dtr, bias, a_log, ddt, dacum)
    xs_blocks, bc_blocks = SSM_DINNER // CONV_CB, SSM_GROUPS * SSM_DSTATE // CONV_CB
    dxbc_x, g_cw_x, g_cb_x = conv_bwd("ssm_conv_bwd_x", xbc, 0, c_ssm, 0, dxs, par["ssm_conv_w"], "silu")
    dxbc_b, g_cw_b, g_cb_b = conv_bwd("ssm_conv_bwd_b", xbc, xs_blocks, c_ssm, xs_blocks, db_, par["ssm_conv_w"], "silu")
    dxbc_c, g_cw_c, g_cb_c = conv_bwd("ssm_conv_bwd_c", xbc, xs_blocks + bc_blocks, c_ssm, xs_blocks + bc_blocks, dc_,
                                      par["ssm_conv_w"], "silu")
    dq, df, dv, dg, g_table, g_hg_norm = hgrn2_bwd(qfig, par["hg_lb_table"], par["hg_out_norm"], hg_states, dy_hg)

    dsegs = [(dq, SEG_QFIG), (df, SEG_QFIG + 2048), (dv, SEG_QFIG + 4096), (dg, SEG_QFIG + 6144), (dz, SEG_Z),
             (dxbc_x, SEG_XBC), (dxbc_b, SEG_XBC + SSM_DINNER), (dxbc_c, SEG_XBC + SSM_DINNER + 1024)]
    g_in_parts = [mm(dseg, h, "tn", name=f"g_w_in_{n}", out_dtype=BF16, tm=1024, tn=2048, tk=1024)
                  for n, (dseg, _) in enumerate(dsegs)]
    g_dt_t = mm(ddtr, h, "tn", name="g_w_in_dt", out_dtype=BF16, tm=128, tn=2048, tk=1024)[:SSM_HEADS]
    g_in_parts += [mm(dgate_, h, "tn", name=f"g_w_in_g{n}", out_dtype=BF16, tm=1024, tn=2048, tk=1024)
                   for n, dgate_ in enumerate((dg_hg, dg_ssm))]
    zpad = jnp.zeros((N_CHIPS, IN_ROWS - IN_SHARD, D), BF16)
    g_in_t = jnp.concatenate(g_in_parts[:8] + [g_dt_t] + g_in_parts[8:], axis=0).reshape(N_CHIPS, IN_SHARD, D)
    (c_in,) = pair_reduce("in", [jnp.concatenate([g_in_t, zpad], axis=1)], [IN_PIECES], [960], core)
    dh = mm_segments("d_h_a", [(dseg, 0, off) for dseg, off in dsegs[:5]], [in_t], tm=1024, tn=1024, tk=1024)
    dh, rb_in = mm_segments("d_h_b", [(dseg, 0, off) for dseg, off in dsegs[5:]] + [(ddtr, 1, 0), (dg_hg, 2, 0), (dg_ssm, 2, D)],
                            [in_t, wts["dt_t"], wts["g_t"]], tm=1024, tn=1024, tk=1024, acc=dh, exchange=[c_in])
    (red_in,) = chip_reduce("in", [c_in], [rb_in], [480], shard)
    grad_x, g_mix_pre = norm_bwd(x, par["mix_pre_norm"], dh, dx1)

    big = dict(in_t=red_in, rest=red_rest, up=red_up)
    g_conv_w = jnp.concatenate([g_cw_x, g_cw_b, g_cw_c], axis=1)[:SSM_CONV]
    g_conv_b = jnp.concatenate([g_cb_x, g_cb_b, g_cb_c], axis=1)
    small = dict(mix_pre_norm=g_mix_pre, mix_post_norm=g_mix_post, hg_lb_table=g_table, hg_out_norm=g_hg_norm,
                 ssm_conv_w=g_conv_w, ssm_conv_b=g_conv_b, ssm_dt_bias=g_dt_bias, ssm_A_log=g_a_log,
                 ssm_D=g_dexp, ssm_out_norm=g_ssm_norm, ffn_pre_norm=g_ffn_pre, ffn_post_norm=g_ffn_post,
                 ffn_conv_w=g_fcw[:FFN_CONV], ffn_conv_b=g_fcb)
    return loss, grad_x, big, small


MESH = pl.DeviceIdType.MESH
ANY = pl.BlockSpec(memory_space=pl.ANY)
N_CHIPS = 4
IN_SHARD = 5648
IN_ROWS = 5760
REST_SPLITS = (512, 1536, 2048, 3456)
UP_COLS = 2816
IN_PIECES, REST_PIECES, UP_PIECES = 3, 4, 4


def _place():
    x, y, c = lax.axis_index("x"), lax.axis_index("y"), lax.axis_index("c")
    chips = [(1 - x, y), (x, 1 - y), (1 - x, 1 - y)]
    return x, y, c, chips


def _rcopy(src, dst, send_sems, recv_sems, k, dev):
    return pltpu.make_async_remote_copy(src_ref=src, dst_ref=dst, send_sem=send_sems.at[k], recv_sem=recv_sems.at[k],
                                        device_id=dev, device_id_type=MESH)


def _pieces(rows, n):
    assert rows % n == 0 and (rows // n) % 16 == 0, (rows, n)
    return [(k * (rows // n), rows // n) for k in range(n)]


def _rows(c, hrows, piece):
    return pl.ds(pl.multiple_of(c * hrows + piece[0], 16), piece[1])


def _half_plan(arrays, pieces):
    return [(a.shape[-2] // 2, _pieces(a.shape[-2] // 2, n)) for a, n in zip(arrays, pieces)]


def _sem_pair(n):
    return [pltpu.SemaphoreType.DMA((n,)), pltpu.SemaphoreType.DMA((n,))]


class _Gather:
    def __init__(self, ps, pieces):
        self.plan = _half_plan(ps, pieces)
        self.n_sem = sum(2 * 3 * len(pcs) for _, pcs in self.plan)
        self.out_shape = [jax.ShapeDtypeStruct((N_CHIPS,) + p.shape, p.dtype) for p in ps]

    def _copies(self, p_refs, g_refs, send_sems, recv_sems, only_first=False):
        x, y, c, chips = _place()
        own = 2 * x + y
        sib = (x, y, 1 - c)
        first, arrive, passed, from_sib = [], [], [], []
        k = 0
        for p, g, (hrows, pcs) in zip(p_refs, g_refs, self.plan):
            for chip in chips:
                theirs = 2 * chip[0] + chip[1]
                for pc in pcs:
                    mine, other = _rows(c, hrows, pc), _rows(1 - c, hrows, pc)
                    first.append(_rcopy(p.at[mine], g.at[own, mine], send_sems, recv_sems, k, (*chip, c)))
                    if not only_first:
                        arrive.append(_rcopy(g.at[theirs, mine], g.at[theirs, mine], send_sems, recv_sems, k, (*chip, c)))
                        passed.append(_rcopy(g.at[theirs, mine], g.at[theirs, mine], send_sems, recv_sems, k + 1, sib))
                        from_sib.append(_rcopy(g.at[theirs, other], g.at[theirs, other], send_sems, recv_sems, k + 1, sib))
                    k += 2
        return first, arrive, passed, from_sib

    def start(self, p_refs, g_refs, send_sems, recv_sems):
        for cp in self._copies(p_refs, g_refs, send_sems, recv_sems, only_first=True)[0]:
            cp.start()

    def finish(self, p_refs, g_refs, send_sems, recv_sems):
        first, arrive, passed, from_sib = self._copies(p_refs, g_refs, send_sems, recv_sems)
        for got, fw in zip(arrive, passed):
            got.wait_recv()
            fw.start()
        for cp in from_sib:
            cp.wait_recv()
        for cp in first + passed:
            cp.wait_send()


def gather_weights(name, ps, pieces):
    op = _Gather(ps, pieces)
    n = len(ps)

    def body(*refs):
        p_refs, g_refs, sems = refs[:n], refs[n:2 * n], refs[2 * n:]
        op.start(p_refs, g_refs, *sems)
        op.finish(p_refs, g_refs, *sems)

    return pl.pallas_call(body, name=name, in_specs=[ANY] * n, out_specs=[ANY] * n, out_shape=op.out_shape,
                          scratch_shapes=_sem_pair(op.n_sem))(*ps)


def pair_exchange(name, gs, pieces):
    plan = _half_plan(gs, pieces)
    n_sem = sum(N_CHIPS * len(pcs) for _, pcs in plan)
    n = len(gs)

    def body(*refs):
        g_refs, r_refs, send_sems, recv_sems = refs[:n], refs[n:2 * n], refs[2 * n], refs[2 * n + 1]
        x, y, c, _ = _place()
        sib = (x, y, 1 - c)
        cps = []
        for g, r, (hrows, pcs) in zip(g_refs, r_refs, plan):
            for s in range(N_CHIPS):
                for pc in pcs:
                    cps.append(_rcopy(g.at[s, _rows(1 - c, hrows, pc)], r.at[s, pl.ds(pc[0], pc[1])],
                                      send_sems, recv_sems, len(cps), sib))
        for cp in cps:
            cp.start()
        for cp in cps:
            cp.wait()

    return pl.pallas_call(
        body, name=name, in_specs=[ANY] * n, out_specs=[ANY] * n,
        out_shape=[jax.ShapeDtypeStruct((N_CHIPS, g.shape[1] // 2, g.shape[2]), g.dtype) for g in gs],
        scratch_shapes=_sem_pair(n_sem))(*gs)


class _ChipExchange:
    def __init__(self, ss):
        self.n_sem = 3 * len(ss)
        self.out_shape = [jax.ShapeDtypeStruct((3,) + s.shape[1:], s.dtype) for s in ss]

    def _copies(self, s_refs, r_refs, send_sems, recv_sems):
        x, y, c, chips = _place()
        cps = []
        for s, r in zip(s_refs, r_refs):
            for j, chip in enumerate(chips):
                cps.append(_rcopy(s.at[2 * chip[0] + chip[1]], r.at[j], send_sems, recv_sems, len(cps), (*chip, c)))
        return cps

    def start(self, *refs):
        for cp in self._copies(*refs):
            cp.start()

    def finish(self, *refs):
        for cp in self._copies(*refs):
            cp.wait()


def pair_assemble(name, rs, pieces):
    plan = [(r.shape[0], _pieces(r.shape[0], n_)) for r, n_ in zip(rs, pieces)]
    n_sem = sum(len(pcs) for _, pcs in plan)
    n = len(rs)

    def body(*refs):
        r_refs, f_refs, send_sems, recv_sems = refs[:n], refs[n:2 * n], refs[2 * n], refs[2 * n + 1]
        x, y, c, _ = _place()
        sib = (x, y, 1 - c)
        cps, got = [], []
        for r, f, (hrows, pcs) in zip(r_refs, f_refs, plan):
            for pc in pcs:
                src = r.at[pl.ds(pc[0], pc[1])]
                cps.append(_rcopy(src, f.at[_rows(c, hrows, pc)], send_sems, recv_sems, len(cps), sib))
                got.append(_rcopy(src, f.at[_rows(1 - c, hrows, pc)], send_sems, recv_sems, len(got), sib))
        for cp in cps:
            cp.start()
        for cp in got:
            cp.wait_recv()
        for cp in cps:
            cp.wait_send()

    return pl.pallas_call(
        body, name=name, in_specs=[ANY] * n, out_specs=[ANY] * n,
        out_shape=[jax.ShapeDtypeStruct((2 * r.shape[0], r.shape[1]), r.dtype) for r in rs],
        scratch_shapes=_sem_pair(n_sem))(*rs)


def pair_reduce(tag, ggs, pieces, trs, core):
    recv = pair_exchange("pair_exchange_" + tag, ggs, pieces)
    flat = lambda a: a.reshape(-1, a.shape[-1])
    out = []
    for n, (gg, r, tr) in enumerate(zip(ggs, recv, trs)):
        h = gg.shape[1] // 2
        own = lax.dynamic_slice_in_dim(gg, core * h, h, axis=1)
        out.append(ew_sum(f"pair_sum_{tag}_{n}", [(flat(own), 0), (flat(r), 0)], N_CHIPS * h, BF16, tr).reshape(r.shape))
    return out


def chip_reduce(tag, cs, rbs, trs, shard):
    out = []
    for n, (c, rb, tr) in enumerate(zip(cs, rbs, trs)):
        h = c.shape[1]
        own = lax.dynamic_index_in_dim(c, shard, axis=0, keepdims=False)
        parts = [(own, 0)] + [(rb.reshape(-1, rb.shape[-1]), j * h) for j in range(3)]
        out.append(ew_sum(f"chip_sum_{tag}_{n}", parts, h, F32, tr))
    return out


N_DEV = 8


def gather_small(blk, reduce):
    rows, cols = blk.shape

    def body(x_ref, out_ref, all_ref, send_sems, recv_sems, local_sem):
        x, y, c, chips = _place()
        me, sib = (x, y, c), (x, y, 1 - c)

        def blk_rows(px, py, pc):
            return all_ref.at[pl.ds(pl.multiple_of((4 * px + 2 * py + pc) * rows, 8), rows), :]

        def copy(k, block, to, src=None):
            return _rcopy(blk_rows(*block) if src is None else src, blk_rows(*block), send_sems, recv_sems, k, to)

        mine = pltpu.make_async_copy(x_ref, blk_rows(*me), local_sem)
        mine.start()
        first = [copy(0, me, sib, src=x_ref)] + [copy(1 + j, me, (*chip, c), src=x_ref) for j, chip in enumerate(chips)]
        for cp in first:
            cp.start()
        passed = [copy(4 + j, (*chip, c), sib) for j, chip in enumerate(chips)]
        for j, chip in enumerate(chips):
            copy(1 + j, (*chip, c), me).wait_recv()
            passed[j].start()
        copy(0, sib, me).wait_recv()
        for j, chip in enumerate(chips):
            copy(4 + j, (*chip, 1 - c), me).wait_recv()
        for cp in first + passed:
            cp.wait_send()
        mine.wait()
        if reduce:
            acc = all_ref[0:rows, :]
            for d in range(1, N_DEV):
                acc = acc + all_ref[d * rows:(d + 1) * rows, :]
            out_ref[...] = acc
        else:
            out_ref[...] = all_ref[...]

    vmem = pl.BlockSpec(memory_space=pltpu.VMEM)
    return pl.pallas_call(
        body, name="reduce_small" if reduce else "gather_small", in_specs=[vmem], out_specs=vmem,
        out_shape=jax.ShapeDtypeStruct((rows if reduce else N_DEV * rows, cols), blk.dtype),
        scratch_shapes=[pltpu.VMEM((N_DEV * rows, cols), blk.dtype), pltpu.SemaphoreType.DMA((7,)),
                        pltpu.SemaphoreType.DMA((7,)), pltpu.SemaphoreType.DMA],
        compiler_params=pltpu.CompilerParams(vmem_limit_bytes=VMEM_LIMIT),
    )(blk)


WEIGHTS = ['w_in', 'mix_pre_norm', 'mix_post_norm', 'hg_lb_table', 'hg_out_norm', 'ssm_conv_w', 'ssm_conv_b',
           'ssm_dt_bias', 'ssm_A_log', 'ssm_D', 'ssm_out_norm', 'w_branch_hg', 'w_branch_ssm', 'w_out', 'ffn_pre_norm',
           'ffn_post_norm', 'ffn_w_up', 'ffn_conv_w', 'ffn_conv_b', 'ffn_w_down']
BIG = ('w_in', 'w_branch_hg', 'w_branch_ssm', 'w_out', 'ffn_w_up', 'ffn_w_down')
SMALL = tuple(n for n in WEIGHTS if n not in BIG)
CONV_SHARD = {'ssm_conv_w': SSM_CONV_DIM // N_CHIPS, 'ffn_conv_w': D_FF // N_CHIPS}
LANES = 128


def _pack(parts):
    flat = jnp.concatenate([p.reshape(-1) for p in parts])
    n = flat.shape[0]
    rows = -(-n // (8 * LANES)) * 8
    return jnp.pad(flat, (0, rows * LANES - n)).reshape(rows, LANES)


def _unpack(packed, shapes):
    flat = packed.reshape(-1)
    out, off = [], 0
    for s in shapes:
        n = int(np.prod(s))
        out.append(flat[off:off + n].reshape(s))
        off += n
    return out


def kernel(x, w_in, mix_pre_norm, mix_post_norm, hg_lb_table, hg_out_norm, ssm_conv_w, ssm_conv_b, ssm_dt_bias, ssm_A_log, ssm_D, ssm_out_norm, w_branch_hg, w_branch_ssm, w_out, ffn_pre_norm, ffn_post_norm, ffn_w_up, ffn_conv_w, ffn_conv_b, ffn_w_down, loss_target, m_w_in, m_mix_pre_norm, m_mix_post_norm, m_hg_lb_table, m_hg_out_norm, m_ssm_conv_w, m_ssm_conv_b, m_ssm_dt_bias, m_ssm_A_log, m_ssm_D, m_ssm_out_norm, m_w_branch_hg, m_w_branch_ssm, m_w_out, m_ffn_pre_norm, m_ffn_post_norm, m_ffn_w_up, m_ffn_conv_w, m_ffn_conv_b, m_ffn_w_down, v_w_in, v_mix_pre_norm, v_mix_post_norm, v_hg_lb_table, v_hg_out_norm, v_ssm_conv_w, v_ssm_conv_b, v_ssm_dt_bias, v_ssm_A_log, v_ssm_D, v_ssm_out_norm, v_w_branch_hg, v_w_branch_ssm, v_w_out, v_ffn_pre_norm, v_ffn_post_norm, v_ffn_w_up, v_ffn_conv_w, v_ffn_conv_b, v_ffn_w_down):
    w = dict(w_in=w_in, mix_pre_norm=mix_pre_norm, mix_post_norm=mix_post_norm, hg_lb_table=hg_lb_table, hg_out_norm=hg_out_norm, ssm_conv_w=ssm_conv_w, ssm_conv_b=ssm_conv_b, ssm_dt_bias=ssm_dt_bias, ssm_A_log=ssm_A_log, ssm_D=ssm_D, ssm_out_norm=ssm_out_norm, w_branch_hg=w_branch_hg, w_branch_ssm=w_branch_ssm, w_out=w_out, ffn_pre_norm=ffn_pre_norm, ffn_post_norm=ffn_post_norm, ffn_w_up=ffn_w_up, ffn_conv_w=ffn_conv_w, ffn_conv_b=ffn_conv_b, ffn_w_down=ffn_w_down)
    m = dict(w_in=m_w_in, mix_pre_norm=m_mix_pre_norm, mix_post_norm=m_mix_post_norm, hg_lb_table=m_hg_lb_table, hg_out_norm=m_hg_out_norm, ssm_conv_w=m_ssm_conv_w, ssm_conv_b=m_ssm_conv_b, ssm_dt_bias=m_ssm_dt_bias, ssm_A_log=m_ssm_A_log, ssm_D=m_ssm_D, ssm_out_norm=m_ssm_out_norm, w_branch_hg=m_w_branch_hg, w_branch_ssm=m_w_branch_ssm, w_out=m_w_out, ffn_pre_norm=m_ffn_pre_norm, ffn_post_norm=m_ffn_post_norm, ffn_w_up=m_ffn_w_up, ffn_conv_w=m_ffn_conv_w, ffn_conv_b=m_ffn_conv_b, ffn_w_down=m_ffn_w_down)
    v = dict(w_in=v_w_in, mix_pre_norm=v_mix_pre_norm, mix_post_norm=v_mix_post_norm, hg_lb_table=v_hg_lb_table, hg_out_norm=v_hg_out_norm, ssm_conv_w=v_ssm_conv_w, ssm_conv_b=v_ssm_conv_b, ssm_dt_bias=v_ssm_dt_bias, ssm_A_log=v_ssm_A_log, ssm_D=v_ssm_D, ssm_out_norm=v_ssm_out_norm, w_branch_hg=v_w_branch_hg, w_branch_ssm=v_w_branch_ssm, w_out=v_w_out, ffn_pre_norm=v_ffn_pre_norm, ffn_post_norm=v_ffn_post_norm, ffn_w_up=v_ffn_w_up, ffn_conv_w=v_ffn_conv_w, ffn_conv_b=v_ffn_conv_b, ffn_w_down=v_ffn_w_down)
    shard = 2 * lax.axis_index("x") + lax.axis_index("y")
    bf = lambda a: a.astype(BF16)

    core = lax.axis_index("c")
    p_in = jnp.concatenate([bf(w_in[0].T), jnp.zeros((IN_ROWS - IN_SHARD, D_MODEL), BF16)], axis=0)
    p_rest = jnp.concatenate([bf(w_branch_hg[0]), bf(w_branch_ssm[0]), bf(w_out[0]), bf(ffn_w_down[0])], axis=0)
    p_up = bf(ffn_w_up[0])
    (g_in,) = gather_weights("gather_w_in", [p_in], [IN_PIECES])
    in_t = _own_slot(g_in, p_in, shard)[:, :IN_SHARD].reshape(IN_TOTAL, D_MODEL)
    wts = dict(in_t=in_t, g_t=in_t[SEG_G:], dt_t=jnp.pad(in_t[SEG_DT:SEG_G], ((0, DT_PAD - SSM_HEADS), (0, 0))))
    conv_cols = max(CONV_SHARD.values())
    padc = lambda a: jnp.pad(a, ((0, 0), (0, conv_cols - a.shape[1])))
    conv_blk = jnp.concatenate([padc(ssm_conv_w[0]), padc(ffn_conv_w[0]), jnp.zeros((1, conv_cols), F32)], axis=0)
    conv_all = gather_small(conv_blk, reduce=False)
    par = {n: w[n] for n in SMALL}
    par["ssm_conv_w"] = jnp.concatenate([conv_all[16 * s:16 * s + SSM_CONV, :CONV_SHARD['ssm_conv_w']] for s in range(N_CHIPS)], axis=1)
    par["ffn_conv_w"] = jnp.concatenate([conv_all[16 * s + SSM_CONV:16 * s + SSM_CONV + FFN_CONV, :CONV_SHARD['ffn_conv_w']]
                                         for s in range(N_CHIPS)], axis=1)

    loss, grad_x, big, small = local_step(x[0], loss_target[0], wts, par, p_rest, p_up, shard, core)
    loss = lax.psum(loss[0, 0], ("x", "y", "c"))

    halves = [big["in_t"], big["rest"], big["up"]]
    wholes = pair_assemble("pair_assemble", halves, [IN_PIECES, REST_PIECES, UP_PIECES])
    f_in, f_rest, f_up = [_own_slot(f.reshape((2,) + r.shape), r, core).reshape(f.shape) for f, r in zip(wholes, halves)]
    r0, r1, r2, r3 = REST_SPLITS
    grads = dict(w_in=f_in[:IN_SHARD].T, w_branch_hg=f_rest[:r0], w_branch_ssm=f_rest[r0:r1], w_out=f_rest[r1:r2],
                 ffn_w_down=f_rest[r2:r3], ffn_w_up=f_up)

    small["hg_out_norm"] = ew_sum("sum_heads", [(small["hg_out_norm"][hd], 0) for hd in range(HG_HEADS)], 1, F32, 1)
    small["ssm_D"] = fold_heads(small["ssm_D"])[:, :SSM_HEADS]
    small["ssm_dt_bias"] = small["ssm_dt_bias"][:, :SSM_HEADS]
    small["ssm_A_log"] = small["ssm_A_log"][:, :SSM_HEADS]
    shapes = [small[n].shape for n in SMALL]
    summed = _unpack(gather_small(_pack([small[n] for n in SMALL]), reduce=True), shapes)
    for n, g in zip(SMALL, summed):
        if n in CONV_SHARD:
            g = lax.dynamic_slice_in_dim(g, shard * CONV_SHARD[n], CONV_SHARD[n], axis=1)
        grads[n] = g

    two_d = lambda a: a.reshape(a.shape[-2], a.shape[-1])
    delta, new_m, new_v = {}, {}, {}
    for n, tr in (("w_in", 64), ("w_branch_hg", 128), ("w_branch_ssm", 128), ("w_out", 128), ("ffn_w_up", 128), ("ffn_w_down", 128)):
        delta[n], new_m[n], new_v[n] = adamw("adamw_" + n, two_d(w[n]), grads[n], two_d(m[n]), two_d(v[n]), tr)
    sm_shapes = [two_d(w[n]).shape for n in SMALL]
    packed = adamw("adamw_small", _pack([two_d(w[n]) for n in SMALL]), _pack([grads[n] for n in SMALL]),
                   _pack([two_d(m[n]) for n in SMALL]), _pack([two_d(v[n]) for n in SMALL]), 1024)
    for res, packed_res in zip((delta, new_m, new_v), packed):
        for n, a in zip(SMALL, _unpack(packed_res, sm_shapes)):
            res[n] = a
    shaped = lambda d: [d[n].reshape(w[n].shape) for n in WEIGHTS]
    return (loss, grad_x[None], *shaped(grads), *shaped(delta), *shaped(new_m), *shaped(new_v))
```

```python
import functools

import jax
import jax.numpy as jnp
import numpy as np
from jax import lax
from jax.experimental import pallas as pl
from jax.experimental.pallas import tpu as pltpu

F32 = jnp.float32
BF16 = jnp.bfloat16

D_MODEL = 2048
EPS = 1e-6
HG_HEADS = 16
HG_DK = 128
HG_CHUNK = 64
HG_SUB = 16
SSM_DINNER = 4096
SSM_HEADDIM = 64
SSM_HEADS = 64
SSM_GROUPS = 8
SSM_DSTATE = 128
SSM_CONV = 4
SSM_CHUNK = 256
SSM_CONV_DIM = 6144
D_FF = 5632
FFN_CONV = 3
DT_PAD = 128

ADAM_LR = 0.001
ADAM_B1 = 0.9
ADAM_B2 = 0.999
ADAM_EPS = 1e-08
ADAM_WD = 0.01
ADAM_STEP = 10

VMEM_LIMIT = 56 * 1024 * 1024
HI = lax.Precision.HIGHEST


def _cp(sem, **kw):
    return pltpu.CompilerParams(dimension_semantics=sem, vmem_limit_bytes=VMEM_LIMIT, **kw)


_DIMS = {"nn": (((1,), (0,)), ((), ())), "nt": (((1,), (1,)), ((), ())), "tn": (((0,), (0,)), ((), ()))}


def mm(a, b, mode, *, name, out_dtype=F32, tm=512, tn=512, tk=None, acc=None, n_major=True,
       dims=None, a_off=(0, 0), b_off=(0, 0)):
    if dims is not None:
        M, N, K = dims
    else:
        if mode == "nn":
            (M, K), (K2, N) = a.shape, b.shape
        elif mode == "nt":
            (M, K), (N, K2) = a.shape, b.shape
        else:
            (K, M), (K2, N) = a.shape, b.shape
        assert K == K2, (a.shape, b.shape, mode)
    tm, tn = min(tm, M), min(tn, N)
    tk = K if tk is None else min(tk, K)
    assert M % tm == 0 and N % tn == 0 and K % tk == 0, (M, N, K, tm, tn, tk)
    a_blk = (tk, tm) if mode == "tn" else (tm, tk)
    b_blk = (tn, tk) if mode == "nt" else (tk, tn)
    assert all(o % s == 0 for o, s in zip(a_off, a_blk)) and all(o % s == 0 for o, s in zip(b_off, b_blk))
    ao0, ao1 = a_off[0] // a_blk[0], a_off[1] // a_blk[1]
    bo0, bo1 = b_off[0] // b_blk[0], b_off[1] // b_blk[1]
    nk = K // tk
    if n_major:
        grid = (N // tn, M // tm, nk)
        ij = lambda p0, p1: (p1, p0)
    else:
        grid = (M // tm, N // tn, nk)
        ij = lambda p0, p1: (p0, p1)

    def a_map(p0, p1, k):
        i, _ = ij(p0, p1)
        return (k + ao0, i + ao1) if mode == "tn" else (i + ao0, k + ao1)

    def b_map(p0, p1, k):
        _, j = ij(p0, p1)
        return (j + bo0, k + bo1) if mode == "nt" else (k + bo0, j + bo1)

    def o_map(p0, p1, k):
        return ij(p0, p1)

    a_spec = pl.BlockSpec(a_blk, a_map)
    b_spec = pl.BlockSpec(b_blk, b_map)
    o_spec = pl.BlockSpec((tm, tn), o_map)
    dims = _DIMS[mode]
    has_acc = acc is not None

    def body(*refs):
        if has_acc:
            a_ref, b_ref, c_ref, o_ref, acc_ref = refs
        else:
            a_ref, b_ref, o_ref, acc_ref = refs
        k = pl.program_id(2)
        part = lax.dot_general(a_ref[...], b_ref[...], dims, preferred_element_type=F32)

        @pl.when(k == 0)
        def _():
            acc_ref[...] = part

        @pl.when(k > 0)
        def _():
            acc_ref[...] += part

        @pl.when(k == nk - 1)
        def _():
            r = acc_ref[...]
            if has_acc:
                r = r + c_ref[...].astype(F32)
            o_ref[...] = r.astype(out_dtype)

    in_specs = [a_spec, b_spec] + ([o_spec] if has_acc else [])
    args = (a, b) + ((acc,) if has_acc else ())
    return pl.pallas_call(
        body, name=name, grid=grid, in_specs=in_specs, out_specs=o_spec,
        out_shape=jax.ShapeDtypeStruct((M, N), out_dtype),
        scratch_shapes=[pltpu.VMEM((tm, tn), F32)],
        compiler_params=_cp(("parallel", "parallel", "arbitrary")),
    )(*args)


def mm_segments(name, segs, bs, *, tm, tn, tk, acc=None, exchange=None):
    m_, n_ = segs[0][0].shape[0], bs[0].shape[1]
    tm, tn = min(tm, m_), min(tn, n_)
    op = _ChipExchange(exchange) if exchange else None
    ne = (len(exchange) if exchange else 0)
    na = 0 if acc is None else 1
    steps, k0 = [], 0
    for a, bi, row in segs:
        w = a.shape[1]
        tks = min(tk, w)
        assert w % tks == 0 and row % tks == 0 and tks == min(tk, bs[bi].shape[0]), (w, row, tks)
        steps.append((k0, w // tks, tks, bi, row // tks))
        k0 += w // tks
    nk = k0
    assert m_ % tm == 0 and n_ % tn == 0

    def a_spec(k_first, count, tks):
        return pl.BlockSpec((tm, tks), lambda j, i, k: (i, jnp.clip(k - k_first, 0, count - 1)))

    def b_spec(bi):
        mine = [s for s in steps if s[3] == bi]

        def index(j, i, k):
            blk = mine[0][4]
            for k_first, count, _, _, first_blk in mine:
                blk = jnp.where(k >= k_first, first_blk + jnp.minimum(k - k_first, count - 1), blk)
            return (blk, j)
        return pl.BlockSpec((mine[0][2], tn), index)

    ns = len(segs)

    nb = len(bs)
    grid = (n_ // tn, m_ // tm, nk)

    def body(*refs):
        a_refs, b_refs = refs[:ns], refs[ns:ns + nb]
        acc_in = refs[ns + nb] if na else None
        rest = refs[ns + nb + na:]
        ex_refs, o_ref, got_refs, acc_ref, sems = rest[:ne], rest[ne], rest[ne + 1:2 * ne + 1], rest[2 * ne + 1], rest[2 * ne + 2:]
        k = pl.program_id(2)
        if op:
            first = (pl.program_id(0) == 0) & (pl.program_id(1) == 0) & (k == 0)
            last = (pl.program_id(0) == grid[0] - 1) & (pl.program_id(1) == grid[1] - 1) & (k == nk - 1)

            @pl.when(first)
            def _():
                op.start(ex_refs, got_refs, *sems)

            @pl.when(last)
            def _():
                op.finish(ex_refs, got_refs, *sems)

        @pl.when(k == 0)
        def _():
            acc_ref[...] = acc_in[...] if na else jnp.zeros_like(acc_ref)

        for a_ref, (k_first, count, _, bi, _) in zip(a_refs, steps):
            @pl.when((k >= k_first) & (k < k_first + count))
            def _(a_ref=a_ref, bi=bi):
                acc_ref[...] += jnp.dot(a_ref[...], b_refs[bi][...], preferred_element_type=F32)

        @pl.when(k == nk - 1)
        def _():
            o_ref[...] = acc_ref[...]

    any_spec = pl.BlockSpec(memory_space=pl.ANY)
    o_spec = pl.BlockSpec((tm, tn), lambda j, i, k: (i, j))
    outs = pl.pallas_call(
        body, name=name, grid=grid,
        in_specs=[a_spec(s[0], s[1], s[2]) for s in steps] + [b_spec(bi) for bi in range(nb)] + [o_spec] * na + [any_spec] * ne,
        out_specs=[o_spec] + [any_spec] * ne,
        out_shape=[jax.ShapeDtypeStruct((m_, n_), F32)] + (op.out_shape if op else []),
        scratch_shapes=[pltpu.VMEM((tm, tn), F32)] + (_sem_pair(op.n_sem) if op else []),
        compiler_params=_cp(("arbitrary", "arbitrary", "arbitrary")),
    )(*[a for a, _, _ in segs], *bs, *(() if acc is None else (acc,)), *(exchange or ()))
    return outs if op else outs[0]


def _dims(mode, ndim):
    if ndim == 2:
        return _DIMS[mode]
    (ca,), (cb,) = _DIMS[mode][0]
    return (((ca + 1,), (cb + 1,)), ((0,), (0,)))


def _bdot_plain(a, b, mode):
    return lax.dot_general(a.astype(BF16), b.astype(BF16), _dims(mode, a.ndim), preferred_element_type=F32)


@functools.partial(jax.custom_vjp, nondiff_argnums=(2,))
def _bdot_vjp(a, b, mode):
    return _bdot_plain(a, b, mode)


def _bdot_fwd(a, b, mode):
    return _bdot_plain(a, b, mode), (a, b)


def _bdot_bwd(mode, res, g):
    a, b = res
    if mode == "nn":
        return _bdot_plain(g, b, "nt"), _bdot_plain(a, g, "tn")
    if mode == "nt":
        return _bdot_plain(g, b, "nn"), _bdot_plain(g, a, "tn")
    return _bdot_plain(b, g, "nt"), _bdot_plain(a, g, "nn")


_bdot_vjp.defvjp(_bdot_fwd, _bdot_bwd)


def _split3(x):
    x1 = x.astype(BF16)
    r1 = x - x1.astype(F32)
    x2 = r1.astype(BF16)
    return x1, x2, (r1 - x2.astype(F32)).astype(BF16)


def _hdot_impl(a, b, mode, data):
    dims = _dims(mode, a.ndim)
    if data == "a":
        sel = b.astype(BF16)
        parts = [lax.dot_general(p, sel, dims, preferred_element_type=F32) for p in _split3(a)]
    else:
        sel = a.astype(BF16)
        parts = [lax.dot_general(sel, p, dims, preferred_element_type=F32) for p in _split3(b)]
    return (parts[2] + parts[1]) + parts[0]


@functools.partial(jax.custom_vjp, nondiff_argnums=(2, 3))
def _hdot(a, b, mode="nn", data="b"):
    return _hdot_impl(a, b, mode, data)


def _hdot_fwd(a, b, mode, data):
    return _hdot_impl(a, b, mode, data), (a, b)


def _hdot_bwd(mode, data, res, g):
    a, b = res
    if data == "a":
        da = {"nn": lambda: _hdot_impl(g, b, "nt", "a"), "nt": lambda: _hdot_impl(g, b, "nn", "a"),
              "tn": lambda: _hdot_impl(b, g, "nt", "b")}[mode]()
        return da, jnp.zeros_like(b)
    db = {"nn": lambda: _hdot_impl(a, g, "tn", "b"), "nt": lambda: _hdot_impl(g, a, "tn", "a"),
          "tn": lambda: _hdot_impl(a, g, "nn", "b")}[mode]()
    return jnp.zeros_like(a), db


_hdot.defvjp(_hdot_fwd, _hdot_bwd)


def _sigmoid(x):
    return 1.0 / (1.0 + jnp.exp(-x))


def _silu(x):
    return x * _sigmoid(x)


def _iota(shape, dim):
    return lax.broadcasted_iota(jnp.int32, shape, dim)


def _rms(x, w):
    return x * lax.rsqrt(jnp.mean(x * x, axis=-1, keepdims=True) + EPS) * w


def _hg_chunk(q_raw, f_raw, v, g, st, t0, t1, nw, dot):
    nhd, c = q_raw.shape[0], q_raw.shape[1]
    m = jnp.maximum(t0, t1)
    e0, e1 = jnp.exp(t0 - m), jnp.exp(t1 - m)
    lb = e0 / (e0 + e1)
    f = lb + (1.0 - lb) * _sigmoid(f_raw)
    k = 1.0 - f
    lf = jnp.log(f)
    qh = _silu(q_raw) * (HG_DK ** -0.5)
    row, col = _iota((c, c), 0), _iota((c, c), 1)
    causal = col <= row
    tril = jnp.broadcast_to(jnp.where(causal, 1.0, 0.0).astype(F32), (nhd, c, c))
    trilb = jnp.broadcast_to(jnp.where(causal & (col // HG_SUB == row // HG_SUB), 1.0, 0.0).astype(F32), (nhd, c, c))
    b = _hdot(tril, lf)
    bl = _hdot(trilb, lf)
    a_row = b - bl
    rid = _iota((c, HG_DK), 0)
    qt = qh * jnp.exp(bl)
    kt = k * jnp.exp(-bl)
    scores = jnp.zeros((nhd, c, c), F32)
    for j in range(c // HG_SUB):
        if j == 0:
            qj = qt * jnp.exp(jnp.minimum(a_row, 0.0))
        else:
            a_j = jnp.sum(jnp.where(rid == j * HG_SUB - 1, b, 0.0), axis=1, keepdims=True)
            qj = qt * jnp.exp(jnp.minimum(a_row - a_j, 0.0))
        kj = jnp.where(rid // HG_SUB == j, kt, 0.0)
        scores = scores + dot(qj, kj, "nt")
    scores = jnp.where(causal, scores, 0.0)
    o = dot(scores, v, "nn") + dot(qh * jnp.exp(b), st, "nt")
    b_last = jnp.sum(jnp.where(rid == c - 1, b, 0.0), axis=1, keepdims=True)
    st_new = st * jnp.exp(b_last) + dot(v, k * jnp.exp(b_last - b), "tn")
    y = _rms(o, nw) * _silu(g)
    return y, st_new


HG_HPS = 16
HG_W = HG_HPS * HG_DK


def hgrn2_fwd(qfig, table, nw, *, step_chunks=2, gather=None):
    t = qfig.shape[0]
    rows = HG_CHUNK * step_chunks
    nsteps = t // rows
    nh = HG_HEADS // HG_HPS
    op = _Gather(*gather) if gather else None
    ng = len(gather[0]) if gather else 0

    def body(*refs):
        q_ref, f_ref, v_ref, g_ref, tab_ref, nw_ref = refs[:6]
        p_refs = refs[6:6 + ng]
        y_ref, s_ref = refs[6 + ng:8 + ng]
        got_refs = refs[8 + ng:8 + 2 * ng]
        st_scr = refs[8 + 2 * ng]
        sems = refs[9 + 2 * ng:]
        first_step = (pl.program_id(0) == 0) & (pl.program_id(1) == 0)
        last_step = (pl.program_id(0) == nh - 1) & (pl.program_id(1) == nsteps - 1)
        if op:
            @pl.when(first_step)
            def _():
                op.start(p_refs, got_refs, *sems)

        @pl.when(pl.program_id(1) == 0)
        def _():
            st_scr[...] = jnp.zeros_like(st_scr)

        nwv = nw_ref[...]
        lanes = [pl.ds(hh * HG_DK, HG_DK) for hh in range(HG_HPS)]
        t0 = jnp.stack([tab_ref[0:1, ln] for ln in lanes])
        t1 = jnp.stack([tab_ref[1:2, ln] for ln in lanes])
        for c in range(step_chunks):
            sl = pl.ds(c * HG_CHUNK, HG_CHUNK)
            heads = lambda ref: jnp.stack([ref[sl, ln] for ln in lanes])
            st = st_scr[...]
            for hh in range(HG_HPS):
                s_ref[hh, c] = st[hh]
            y, st_new = _hg_chunk(heads(q_ref), heads(f_ref), heads(v_ref), heads(g_ref), st, t0, t1, nwv, _bdot_vjp)
            for hh, ln in enumerate(lanes):
                y_ref[sl, ln] = y[hh].astype(BF16)
            st_scr[...] = st_new

        if op:
            @pl.when(last_step)
            def _():
                op.finish(p_refs, got_refs, *sems)

    blk = lambda off: pl.BlockSpec((rows, HG_W), lambda h, c, off=off: (c, off + h))
    return pl.pallas_call(
        body, name="hgrn2_fwd", grid=(nh, nsteps),
        in_specs=[blk(0), blk(nh), blk(2 * nh), blk(3 * nh),
                  pl.BlockSpec((2, HG_W), lambda h, c: (0, h)), pl.BlockSpec((1, HG_DK), lambda h, c: (0, 0))] + [ANY] * ng,
        out_specs=[pl.BlockSpec((rows, HG_W), lambda h, c: (c, h)),
                   pl.BlockSpec((HG_HPS, step_chunks, HG_DK, HG_DK), lambda h, c: (h, c, 0, 0))] + [ANY] * ng,
        out_shape=[jax.ShapeDtypeStruct((t, HG_HEADS * HG_DK), BF16),
                   jax.ShapeDtypeStruct((HG_HEADS, t // HG_CHUNK, HG_DK, HG_DK), F32)] + (op.out_shape if op else []),
        scratch_shapes=[pltpu.VMEM((HG_HPS, HG_DK, HG_DK), F32)] + (_sem_pair(op.n_sem) if op else []),
        compiler_params=_cp(("arbitrary", "arbitrary")),
    )(qfig, qfig, qfig, qfig, table, nw, *(gather[0] if gather else ()))


def hgrn2_bwd(qfig, table, nw, states, dy, *, step_chunks=2):
    t = qfig.shape[0]
    rows = HG_CHUNK * step_chunks
    nsteps = t // rows
    nh = HG_HEADS // HG_HPS

    def body(q_ref, f_ref, v_ref, g_ref, tab_ref, nw_ref, s_ref, dy_ref,
             dq_ref, df_ref, dv_ref, dg_ref, dtab_ref, dnw_ref, dst_scr):
        @pl.when(pl.program_id(1) == 0)
        def _():
            dst_scr[...] = jnp.zeros_like(dst_scr)
            dtab_ref[...] = jnp.zeros_like(dtab_ref)
            dnw_ref[...] = jnp.zeros_like(dnw_ref)

        nwv = nw_ref[...]
        fn = functools.partial(_hg_chunk, dot=_bdot_vjp)
        lanes = [pl.ds(hh * HG_DK, HG_DK) for hh in range(HG_HPS)]
        t0 = jnp.stack([tab_ref[0:1, ln] for ln in lanes])
        t1 = jnp.stack([tab_ref[1:2, ln] for ln in lanes])
        for c in reversed(range(step_chunks)):
            sl = pl.ds(c * HG_CHUNK, HG_CHUNK)
            heads = lambda ref: jnp.stack([ref[sl, ln] for ln in lanes])
            _, vjp = jax.vjp(fn, heads(q_ref), heads(f_ref), heads(v_ref), heads(g_ref), s_ref[:, c], t0, t1, nwv)
            dq, df, dv, dg, dst, dt0, dt1, dnw = vjp((heads(dy_ref).astype(F32), dst_scr[...]))
            for hh, ln in enumerate(lanes):
                dq_ref[sl, ln] = dq[hh].astype(BF16)
                df_ref[sl, ln] = df[hh].astype(BF16)
                dv_ref[sl, ln] = dv[hh].astype(BF16)
                dg_ref[sl, ln] = dg[hh].astype(BF16)
                dtab_ref[0:1, ln] += dt0[hh]
                dtab_ref[1:2, ln] += dt1[hh]
            dst_scr[...] = dst
            dnw_ref[0] += dnw

    rev = lambda c: nsteps - 1 - c
    blk = lambda off: pl.BlockSpec((rows, HG_W), lambda h, c, off=off: (rev(c), off + h))
    oblk = lambda: pl.BlockSpec((rows, HG_W), lambda h, c: (rev(c), h))
    d = HG_HEADS * HG_DK
    outs = pl.pallas_call(
        body, name="hgrn2_bwd", grid=(nh, nsteps),
        in_specs=[blk(0), blk(nh), blk(2 * nh), blk(3 * nh),
                  pl.BlockSpec((2, HG_W), lambda h, c: (0, h)), pl.BlockSpec((1, HG_DK), lambda h, c: (0, 0)),
                  pl.BlockSpec((HG_HPS, step_chunks, HG_DK, HG_DK), lambda h, c: (h, rev(c), 0, 0)),
                  pl.BlockSpec((rows, HG_W), lambda h, c: (rev(c), h))],
        out_specs=[oblk(), oblk(), oblk(), oblk(),
                   pl.BlockSpec((2, HG_W), lambda h, c: (0, h)),
                   pl.BlockSpec((HG_HPS, 1, HG_DK), lambda h, c: (h, 0, 0))],
        out_shape=[jax.ShapeDtypeStruct((t, d), BF16)] * 4
        + [jax.ShapeDtypeStruct((2, d), F32), jax.ShapeDtypeStruct((HG_HEADS, 1, HG_DK), F32)],
        scratch_shapes=[pltpu.VMEM((HG_HPS, HG_DK, HG_DK), F32)],
        compiler_params=_cp(("parallel", "arbitrary")),
    )(qfig, qfig, qfig, qfig, table, nw, states, dy)
    return outs


def _ssd_chunk(xs2, dt, acum, bm, cm, s2, pair0, dot):
    npr, c = xs2.shape[0], xs2.shape[1]
    sh_e, sh_s = (npr, DT_PAD, 128), (npr, 8, DT_PAD)
    first_head = 2 * (pair0 + _iota(sh_e, 0))
    expand = jnp.where(_iota(sh_e, 1) == first_head + _iota(sh_e, 2) // SSM_HEADDIM, 1.0, 0.0).astype(F32)
    sel = (_iota(sh_s, 2) == 2 * (pair0 + _iota(sh_s, 0)) + _iota(sh_s, 1)) & (_iota(sh_s, 1) < 2)
    sel = jnp.where(sel, 1.0, 0.0).astype(F32)
    per_pair = lambda a: jnp.broadcast_to(a, (npr,) + a.shape)
    dtx = _hdot(per_pair(dt), expand, "nn", "a")
    acol = _hdot(per_pair(acum), expand, "nn", "a")
    arow8 = _hdot(sel, per_pair(acum), "nt", "b")
    row, col = _iota((c, c), 0), _iota((c, c), 1)
    causal = col <= row
    cb = dot(cm, bm, "nt")
    x2 = xs2 * dtx
    lane_c = _iota((c, 128), 1)
    y = dot(per_pair(cm), s2, "nn") * jnp.exp(acol)
    for r in range(2):
        head = (lane_c // SSM_HEADDIM) == r
        a_c = jnp.sum(jnp.where(head & (lane_c % SSM_HEADDIM == 0), acol, 0.0), axis=2, keepdims=True)
        a_r = jnp.sum(jnp.where(_iota((8, c), 0) == r, arow8, 0.0), axis=1, keepdims=True)
        decay = jnp.where(causal, jnp.exp(jnp.minimum(a_c - a_r, 0.0)), 0.0)
        y = y + dot(cb * decay, jnp.where(head, x2, 0.0), "nn")
    a_last = jnp.sum(jnp.where(_iota((c, 128), 0) == c - 1, acol, 0.0), axis=1, keepdims=True)
    s2_new = s2 * jnp.exp(a_last) + dot(per_pair(bm), x2 * jnp.exp(a_last - acol), "tn")
    return y, s2_new


SSM_PAIRS = SSM_HEADS // 2
PAIRS_PER_GROUP = SSM_PAIRS // SSM_GROUPS
SSD_PPS = 4
SSD_W = SSD_PPS * 128
_XS_BLOCKS = SSM_DINNER // 128
_B_BLOCK0 = _XS_BLOCKS
_C_BLOCK0 = _XS_BLOCKS + SSM_GROUPS


def ssd_fwd(xbc_act, dt, acum):
    t = xbc_act.shape[0]
    nc = t // SSM_CHUNK
    c_ = SSM_CHUNK

    def body(xs_ref, b_ref, c_ref, dt_ref, ac_ref, y_ref, s_ref, s_scr):
        q = pl.program_id(1)
        mine = pl.ds(SSD_PPS * q, SSD_PPS)
        lanes = [pl.ds(r * 128, 128) for r in range(SSD_PPS)]

        @pl.when(pl.program_id(0) == 0)
        def _():
            s_scr[mine] = jnp.zeros((SSD_PPS, SSM_DSTATE, 128), F32)

        s2 = s_scr[mine]
        s_ref[...] = s2
        xs = jnp.stack([xs_ref[:, ln] for ln in lanes])
        y, s2_new = _ssd_chunk(xs, dt_ref[...], ac_ref[...], b_ref[...], c_ref[...], s2, SSD_PPS * q, _bdot_vjp)
        for r, ln in enumerate(lanes):
            y_ref[:, ln] = y[r]
        s_scr[mine] = s2_new

    grp = lambda q: q // (PAIRS_PER_GROUP // SSD_PPS)
    return pl.pallas_call(
        body, name="ssd_fwd", grid=(nc, SSM_PAIRS // SSD_PPS),
        in_specs=[pl.BlockSpec((c_, SSD_W), lambda c, q: (c, q)),
                  pl.BlockSpec((c_, 128), lambda c, q: (c, _B_BLOCK0 + grp(q))),
                  pl.BlockSpec((c_, 128), lambda c, q: (c, _C_BLOCK0 + grp(q))),
                  pl.BlockSpec((c_, DT_PAD), lambda c, q: (c, 0)),
                  pl.BlockSpec((c_, DT_PAD), lambda c, q: (c, 0))],
        out_specs=[pl.BlockSpec((c_, SSD_W), lambda c, q: (c, q)),
                   pl.BlockSpec((None, SSD_PPS, SSM_DSTATE, 128), lambda c, q: (c, q, 0, 0))],
        out_shape=[jax.ShapeDtypeStruct((t, SSM_DINNER), F32),
                   jax.ShapeDtypeStruct((nc, SSM_PAIRS, SSM_DSTATE, 128), F32)],
        scratch_shapes=[pltpu.VMEM((SSM_PAIRS, SSM_DSTATE, 128), F32)],
        compiler_params=_cp(("arbitrary", "arbitrary")),
    )(xbc_act, xbc_act, xbc_act, dt, acum)


def ssd_bwd(xbc_act, dt, acum, states, dy, dskip, *, exchange=None):
    t = xbc_act.shape[0]
    nc = t // SSM_CHUNK
    c_ = SSM_CHUNK
    rev = lambda c: nc - 1 - c
    nq = SSM_PAIRS // SSD_PPS
    op = _ChipExchange(exchange) if exchange else None
    ne = len(exchange) if exchange else 0

    def body(*refs):
        xs_ref, b_ref, c_ref, dt_ref, ac_ref, s_ref, dy_ref, sk_ref = refs[:8]
        ex_refs = refs[8:8 + ne]
        dxs_ref, db_ref, dc_ref, ddt_ref, dac_ref = refs[8 + ne:13 + ne]
        got_refs = refs[13 + ne:13 + 2 * ne]
        ds_scr = refs[13 + 2 * ne]
        sems = refs[14 + 2 * ne:]
        q = pl.program_id(1)
        if op:
            @pl.when((pl.program_id(0) == 0) & (q == 0))
            def _():
                op.start(ex_refs, got_refs, *sems)

            @pl.when((pl.program_id(0) == nc - 1) & (q == nq - 1))
            def _():
                op.finish(ex_refs, got_refs, *sems)

        assert SSD_PPS == PAIRS_PER_GROUP
        mine = pl.ds(SSD_PPS * q, SSD_PPS)
        lanes = [pl.ds(r * 128, 128) for r in range(SSD_PPS)]

        @pl.when(pl.program_id(0) == 0)
        def _():
            ds_scr[mine] = jnp.zeros((SSD_PPS, SSM_DSTATE, 128), F32)

        fn = functools.partial(_ssd_chunk, pair0=SSD_PPS * q, dot=_bdot_vjp)
        xs = jnp.stack([xs_ref[:, ln] for ln in lanes])
        dy = jnp.stack([dy_ref[:, ln] for ln in lanes])
        _, vjp = jax.vjp(fn, xs, dt_ref[...], ac_ref[...], b_ref[...], c_ref[...], s_ref[...])
        dxs, ddt, dac, db, dc, ds = vjp((dy, ds_scr[mine]))
        for r, ln in enumerate(lanes):
            dxs_ref[:, ln] = dxs[r] + sk_ref[:, ln]
        ds_scr[mine] = ds
        db_ref[...] = db
        dc_ref[...] = dc

        @pl.when(q == 0)
        def _():
            ddt_ref[...] = ddt
            dac_ref[...] = dac

        @pl.when(q != 0)
        def _():
            ddt_ref[...] += ddt
            dac_ref[...] += dac

    grp = lambda q: q // (PAIRS_PER_GROUP // SSD_PPS)
    return pl.pallas_call(
        body, name="ssd_bwd", grid=(nc, SSM_PAIRS // SSD_PPS),
        in_specs=[pl.BlockSpec((c_, SSD_W), lambda c, q: (rev(c), q)),
                  pl.BlockSpec((c_, 128), lambda c, q: (rev(c), _B_BLOCK0 + grp(q))),
                  pl.BlockSpec((c_, 128), lambda c, q: (rev(c), _C_BLOCK0 + grp(q))),
                  pl.BlockSpec((c_, DT_PAD), lambda c, q: (rev(c), 0)),
                  pl.BlockSpec((c_, DT_PAD), lambda c, q: (rev(c), 0)),
                  pl.BlockSpec((None, SSD_PPS, SSM_DSTATE, 128), lambda c, q: (rev(c), q, 0, 0)),
                  pl.BlockSpec((c_, SSD_W), lambda c, q: (rev(c), q)),
                  pl.BlockSpec((c_, SSD_W), lambda c, q: (rev(c), q))] + [ANY] * ne,
        out_specs=[pl.BlockSpec((c_, SSD_W), lambda c, q: (rev(c), q)),
                   pl.BlockSpec((c_, 128), lambda c, q: (rev(c), grp(q))),
                   pl.BlockSpec((c_, 128), lambda c, q: (rev(c), grp(q))),
                   pl.BlockSpec((c_, DT_PAD), lambda c, q: (rev(c), 0)),
                   pl.BlockSpec((c_, DT_PAD), lambda c, q: (rev(c), 0))] + [ANY] * ne,
        out_shape=[jax.ShapeDtypeStruct((t, SSM_DINNER), F32),
                   jax.ShapeDtypeStruct((t, SSM_GROUPS * SSM_DSTATE), F32),
                   jax.ShapeDtypeStruct((t, SSM_GROUPS * SSM_DSTATE), F32),
                   jax.ShapeDtypeStruct((t, DT_PAD), F32),
                   jax.ShapeDtypeStruct((t, DT_PAD), F32)] + (op.out_shape if op else []),
        scratch_shapes=[pltpu.VMEM((SSM_PAIRS, SSM_DSTATE, 128), F32)] + (_sem_pair(op.n_sem) if op else []),
        compiler_params=_cp(("arbitrary", "arbitrary")),
    )(xbc_act, xbc_act, xbc_act, dt, acum, states, dy, dskip, *(exchange or ()))


def rowwise(name, fn, row_ins, par_ins, row_outs, acc_outs, *, tt, ncb=1):
    t = row_ins[0][0].shape[0]
    assert t % tt == 0
    n_ri, n_pi, n_ro, n_ao = len(row_ins), len(par_ins), len(row_outs), len(acc_outs)

    def body(*refs):
        i = pl.program_id(1)
        ins = [r[...] for r in refs[:n_ri + n_pi]]
        outs = fn(*ins)
        ro_refs = refs[n_ri + n_pi:n_ri + n_pi + n_ro]
        ao_refs = refs[n_ri + n_pi + n_ro:]
        for r, v in zip(ro_refs, outs[:n_ro]):
            r[...] = v.astype(r.dtype)
        for r, v in zip(ao_refs, outs[n_ro:]):
            @pl.when(i == 0)
            def _(r=r, v=v):
                r[...] = v

            @pl.when(i > 0)
            def _(r=r, v=v):
                r[...] += v

    in_specs = [pl.BlockSpec((tt, bc), lambda j, i, off=off: (i, off + j)) for _, bc, off in row_ins]
    in_specs += [pl.BlockSpec((a.shape[0], bc), lambda j, i, off=off: (0, off + j)) for a, bc, off in par_ins]
    out_specs = [pl.BlockSpec((tt, bc), lambda j, i: (i, j)) for _, bc, _ in row_outs]
    out_specs += [pl.BlockSpec((r, bc), lambda j, i: (0, j)) for r, _, bc in acc_outs]
    out_shape = [jax.ShapeDtypeStruct((t, c), dt) for c, _, dt in row_outs]
    out_shape += [jax.ShapeDtypeStruct((r, c), F32) for r, c, _ in acc_outs]
    return pl.pallas_call(
        body, name=name, grid=(ncb, t // tt), in_specs=in_specs, out_specs=out_specs, out_shape=out_shape,
        compiler_params=_cp(("parallel", "arbitrary")),
    )(*[a for a, _, _ in row_ins], *[a for a, _, _ in par_ins])


def _colsum(v):
    return jnp.sum(v, axis=0, keepdims=True)


def _softplus(x):
    return jnp.maximum(x, 0.0) + jnp.log(1.0 + jnp.exp(-jnp.abs(x)))


def _gelu_tanh(x):
    return 0.5 * x * (1.0 + jnp.tanh(0.7978845608028654 * (x + 0.044715 * (x * x * x))))


D = D_MODEL


def norm_fwd(x, w):
    return rowwise("norm_fwd", lambda xv, wv: (_rms(xv, wv),), [(x, D, 0)], [(w, D, 0)], [(D, D, BF16)], [], tt=256)[0]


def norm_bwd(x, w, dh, dres):
    def fn(xv, dhv, drv, wv):
        _, vjp = jax.vjp(_rms, xv, wv)
        dx, dw = vjp(dhv)
        return dx + drv, dw
    return rowwise("norm_bwd", fn, [(x, D, 0), (dh, D, 0), (dres, D, 0)], [(w, D, 0)], [(D, D, F32)], [(1, D, D)], tt=256)


def _dt_fn(dtr, bias, a_log):
    c = dtr.shape[0]
    dt = _softplus(dtr + bias)
    da = dt * (-jnp.exp(a_log))
    tril = jnp.where(_iota((c, c), 1) <= _iota((c, c), 0), 1.0, 0.0).astype(F32)
    return dt, _hdot(tril, da)


def dt_fwd(dtr, bias, a_log):
    return rowwise("dt_fwd", _dt_fn, [(dtr, DT_PAD, 0)], [(bias, DT_PAD, 0), (a_log, DT_PAD, 0)],
                   [(DT_PAD, DT_PAD, F32), (DT_PAD, DT_PAD, F32)], [], tt=SSM_CHUNK)


def dt_bwd(dtr, bias, a_log, ddt, dacum):
    def fn(dtrv, ddtv, dacv, bv, av):
        _, vjp = jax.vjp(_dt_fn, dtrv, bv, av)
        return vjp((ddtv, dacv))
    return rowwise("dt_bwd", fn, [(dtr, DT_PAD, 0), (ddt, DT_PAD, 0), (dacum, DT_PAD, 0)],
                   [(bias, DT_PAD, 0), (a_log, DT_PAD, 0)],
                   [(DT_PAD, DT_PAD, BF16)], [(1, DT_PAD, DT_PAD), (1, DT_PAD, DT_PAD)], tt=SSM_CHUNK)


GROUP_W = SSM_DINNER // SSM_GROUPS


def _ssm_post_fn(yv, xsv, zv, dexp, nw):
    return _rms((yv + dexp * xsv) * _silu(zv), nw)


def ssm_post_fwd(yssd, xbc_act, z, dexp, nw):
    return rowwise("ssm_post_fwd", lambda *a: (_ssm_post_fn(*a),),
                   [(yssd, GROUP_W, 0), (xbc_act, GROUP_W, 0), (z, GROUP_W, 0)], [(dexp, GROUP_W, 0), (nw, GROUP_W, 0)],
                   [(SSM_DINNER, GROUP_W, BF16)], [], tt=512, ncb=SSM_GROUPS)[0]


def ssm_post_bwd(yssd, xbc_act, z, dexp, nw, dy):
    def fn(yv, xsv, zv, dyv, dv, nv):
        _, vjp = jax.vjp(_ssm_post_fn, yv, xsv, zv, dv, nv)
        return vjp(dyv.astype(F32))
    return rowwise("ssm_post_bwd", fn,
                   [(yssd, GROUP_W, 0), (xbc_act, GROUP_W, 0), (z, GROUP_W, 0), (dy, GROUP_W, 0)],
                   [(dexp, GROUP_W, 0), (nw, GROUP_W, 0)],
                   [(SSM_DINNER, GROUP_W, F32), (SSM_DINNER, GROUP_W, F32), (SSM_DINNER, GROUP_W, BF16)],
                   [(1, SSM_DINNER, GROUP_W), (1, SSM_DINNER, GROUP_W)], tt=512, ncb=SSM_GROUPS)


def _merge_fn(ah, asm, gh, gs):
    return _sigmoid(gh) * ah + _sigmoid(gs) * asm


def merge_fwd(a_hg, a_ssm, gates):
    return rowwise("merge_fwd", lambda *a: (_merge_fn(*a),), [(a_hg, D, 0), (a_ssm, D, 0), (gates, D, 0), (gates, D, 1)], [],
                   [(D, D, BF16)], [], tt=256)[0]


def merge_bwd(a_hg, a_ssm, gates, dmixed):
    def fn(ah, asm, gh, gs, dm):
        _, vjp = jax.vjp(_merge_fn, ah, asm, gh, gs)
        return vjp(dm)
    return rowwise("merge_bwd", fn, [(a_hg, D, 0), (a_ssm, D, 0), (gates, D, 0), (gates, D, 1), (dmixed, D, 0)], [],
                   [(D, D, BF16)] * 4, [], tt=256)


def _post1_fn(xv, uv, wpost, wpre):
    x1 = xv + _rms(uv, wpost)
    return x1, _rms(x1, wpre)


def post1_fwd(x, u, wpost, wpre):
    return rowwise("post1_fwd", _post1_fn, [(x, D, 0), (u, D, 0)], [(wpost, D, 0), (wpre, D, 0)],
                   [(D, D, F32), (D, D, BF16)], [], tt=256)


def post1_bwd(x, u, wpost, wpre, dx1, dh2):
    def fn(xv, uv, d1, d2, wa, wb):
        _, vjp = jax.vjp(_post1_fn, xv, uv, wa, wb)
        dx, du, dwa, dwb = vjp((d1, d2))
        return du, dx, dwa, dwb
    return rowwise("post1_bwd", fn, [(x, D, 0), (u, D, 0), (dx1, D, 0), (dh2, D, 0)], [(wpost, D, 0), (wpre, D, 0)],
                   [(D, D, BF16), (D, D, F32)], [(1, D, D), (1, D, D)], tt=256)


def final_fwd_bwd(x1, fo, w, target):
    def fn(x1v, fov, tv, wv):
        def loss_fn(a, b, c):
            err = a + _rms(b, c) - tv
            return 0.5 * jnp.sum(err * err) * (1.0 / D)
        loss, vjp = jax.vjp(loss_fn, x1v, fov, wv)
        dx, dfo, dw = vjp(jnp.ones((), F32))
        return dx, dfo, dw, jnp.full((1, 128), loss, F32)
    return rowwise("final_fwd_bwd", fn, [(x1, D, 0), (fo, D, 0), (target, D, 0)], [(w, D, 0)],
                   [(D, D, F32), (D, D, BF16)], [(1, D, D), (1, 128, 128)], tt=256)


HALO = 8
CONV_TT = 512
CONV_CB = 512
CONV_RB = 32


def _tail(kind, c, up):
    return _silu(c) if kind == "silu" else _gelu_tanh(c) * up


def conv_fwd(name, x, xoff, w, b, kind, up=None, upoff=0, act_dtype=F32):
    t = x.shape[0]
    k_, c_ = w.shape
    tt, cb = CONV_TT, CONV_CB
    hb = tt // HALO
    has_up = up is not None

    def body(*refs):
        if has_up:
            x_ref, xp_ref, w_ref, b_ref, up_ref, c_ref, a_ref, scr = refs
        else:
            x_ref, xp_ref, w_ref, b_ref, c_ref, a_ref, scr = refs
        i = pl.program_id(1)
        scr[0:HALO, :] = jnp.where(i == 0, 0.0, xp_ref[...])
        scr[HALO:HALO + tt, :] = x_ref[...]
        for r in range(tt // CONV_RB):
            rows = pl.ds(r * CONV_RB, CONV_RB)
            acc = jnp.zeros((CONV_RB, cb), F32) + b_ref[...]
            for k in range(k_):
                acc = acc + w_ref[k:k + 1, :] * scr[pl.ds(r * CONV_RB + HALO - (k_ - 1) + k, CONV_RB), :]
            c_ref[rows, :] = acc
            a_ref[rows, :] = _tail(kind, acc, up_ref[rows, :] if has_up else None).astype(act_dtype)

    in_specs = [pl.BlockSpec((tt, cb), lambda j, i: (i, xoff + j)),
                pl.BlockSpec((HALO, cb), lambda j, i: (jnp.maximum(i * hb - 1, 0), xoff + j)),
                pl.BlockSpec((k_, cb), lambda j, i: (0, j)),
                pl.BlockSpec((1, cb), lambda j, i: (0, j))]
    args = [x, x, w, b]
    if has_up:
        in_specs.append(pl.BlockSpec((tt, cb), lambda j, i: (i, upoff + j)))
        args.append(up)
    return pl.pallas_call(
        body, name=name, grid=(c_ // cb, t // tt), in_specs=in_specs,
        out_specs=[pl.BlockSpec((tt, cb), lambda j, i: (i, j))] * 2,
        out_shape=[jax.ShapeDtypeStruct((t, c_), F32), jax.ShapeDtypeStruct((t, c_), act_dtype)],
        scratch_shapes=[pltpu.VMEM((tt + HALO, cb), F32)],
        compiler_params=_cp(("parallel", "arbitrary")),
    )(*args)


def conv_bwd(name, x, xoff, c, coff, dact, w, kind, up=None, upoff=0):
    t = x.shape[0]
    k_, c_ = w.shape[0], dact.shape[1]
    tt, cb = CONV_TT, CONV_CB
    hb = tt // HALO
    nt = t // tt
    has_up = up is not None

    def tail_grad(cv, dav, upv):
        if has_up:
            _, vjp = jax.vjp(lambda a, u: _tail(kind, a, u), cv, upv)
            return vjp(dav)
        _, vjp = jax.vjp(lambda a: _tail(kind, a, None), cv)
        return vjp(dav)[0], None

    def body(*refs):
        if has_up:
            (x_ref, xp_ref, c_ref, cn_ref, da_ref, dan_ref, w_ref, up_ref, upn_ref,
             dx_ref, dup_ref, dw_ref, db_ref, xs, dcs) = refs
        else:
            x_ref, xp_ref, c_ref, cn_ref, da_ref, dan_ref, w_ref, dx_ref, dw_ref, db_ref, xs, dcs = refs
        i = pl.program_id(1)
        xs[0:HALO, :] = jnp.where(i == 0, 0.0, xp_ref[...])
        xs[HALO:HALO + tt, :] = x_ref[...]
        rb = CONV_RB
        for r in range(tt // rb):
            rows = pl.ds(r * rb, rb)
            dc, dup = tail_grad(c_ref[rows, :], da_ref[rows, :].astype(F32), up_ref[rows, :] if has_up else None)
            dcs[rows, :] = dc
            if has_up:
                dup_ref[rows, :] = dup.astype(BF16)
        dcn, _ = tail_grad(cn_ref[...], dan_ref[...].astype(F32), upn_ref[...] if has_up else None)
        dcs[tt:tt + HALO, :] = jnp.where(i == nt - 1, 0.0, dcn)
        dws = [jnp.zeros((1, cb), F32) for _ in range(k_)]
        dbv = jnp.zeros((1, cb), F32)
        for r in range(tt // rb):
            rows = pl.ds(r * rb, rb)
            dc = dcs[rows, :]
            dx = jnp.zeros((rb, cb), F32)
            for k in range(k_):
                dx = dx + w_ref[k:k + 1, :] * dcs[pl.ds(r * rb + k_ - 1 - k, rb), :]
                dws[k] = dws[k] + _colsum(dc * xs[pl.ds(r * rb + HALO - (k_ - 1) + k, rb), :])
            dbv = dbv + _colsum(dc)
            dx_ref[rows, :] = dx.astype(BF16)

        @pl.when(i == 0)
        def _():
            dw_ref[...] = jnp.zeros_like(dw_ref)
            db_ref[...] = jnp.zeros_like(db_ref)

        for k in range(k_):
            dw_ref[k:k + 1, :] += dws[k]
        db_ref[...] += dbv

    tile = lambda off: pl.BlockSpec((tt, cb), lambda j, i, off=off: (i, off + j))
    prev = lambda off: pl.BlockSpec((HALO, cb), lambda j, i, off=off: (jnp.maximum(i * hb - 1, 0), off + j))
    nxt = lambda off: pl.BlockSpec((HALO, cb), lambda j, i, off=off: (jnp.minimum((i + 1) * hb, t // HALO - 1), off + j))
    in_specs = [tile(xoff), prev(xoff), tile(coff), nxt(coff), tile(0), nxt(0),
                pl.BlockSpec((k_, cb), lambda j, i: (0, coff + j))]
    args = [x, x, c, c, dact, dact, w]
    if has_up:
        in_specs += [tile(upoff), nxt(upoff)]
        args += [up, up]
    out_specs = [tile(0)] + ([tile(0)] if has_up else []) + [pl.BlockSpec((HALO, cb), lambda j, i: (0, j)),
                                                            pl.BlockSpec((1, cb), lambda j, i: (0, j))]
    out_shape = [jax.ShapeDtypeStruct((t, c_), BF16)] * (2 if has_up else 1)
    out_shape += [jax.ShapeDtypeStruct((HALO, c_), F32), jax.ShapeDtypeStruct((1, c_), F32)]
    return pl.pallas_call(
        body, name=name, grid=(c_ // cb, nt), in_specs=in_specs, out_specs=out_specs, out_shape=out_shape,
        scratch_shapes=[pltpu.VMEM((tt + HALO, cb), F32), pltpu.VMEM((tt + HALO, cb), F32)],
        compiler_params=_cp(("parallel", "arbitrary")),
    )(*args)


def ew_sum(name, parts, rows, out_dtype, tr):
    c = parts[0][0].shape[1]
    tr = min(tr, rows)
    assert rows % tr == 0 and all(off % tr == 0 for _, off in parts)
    n = len(parts)

    def body(*refs):
        acc = refs[0][...].astype(F32)
        for ref in refs[1:n]:
            acc = acc + ref[...].astype(F32)
        refs[n][...] = acc.astype(out_dtype)

    in_specs = [pl.BlockSpec((tr, c), lambda i, o=off // tr: (i + o, 0)) for _, off in parts]
    return pl.pallas_call(body, name=name, grid=(rows // tr,), in_specs=in_specs,
                          out_specs=pl.BlockSpec((tr, c), lambda i: (i, 0)),
                          out_shape=jax.ShapeDtypeStruct((rows, c), out_dtype),
                          compiler_params=_cp(("parallel",)))(*[a for a, _ in parts])


def fold_heads(dexp):
    def body(d_ref, o_ref):
        sel = jnp.where(_iota((SSM_DINNER, DT_PAD), 0) // SSM_HEADDIM == _iota((SSM_DINNER, DT_PAD), 1), 1.0, 0.0)
        o_ref[...] = _hdot(jnp.broadcast_to(d_ref[...], (8, SSM_DINNER)), sel.astype(F32), "nn", "a")[0:1, :]

    return pl.pallas_call(body, name="fold_heads", out_shape=jax.ShapeDtypeStruct((1, DT_PAD), F32),
                          compiler_params=pltpu.CompilerParams(vmem_limit_bytes=VMEM_LIMIT))(dexp)


def adamw(name, w, g, m, v, tr):
    r, c = w.shape
    tr = min(tr, r)
    assert r % tr == 0, (r, tr)

    def body(w_ref, g_ref, m_ref, v_ref, d_ref, nm_ref, nv_ref):
        gv = g_ref[...]
        nm = ADAM_B1 * m_ref[...] + (1.0 - ADAM_B1) * gv
        nv = ADAM_B2 * v_ref[...] + (1.0 - ADAM_B2) * (gv * gv)
        m_hat = nm / (1.0 - ADAM_B1 ** ADAM_STEP)
        v_hat = nv / (1.0 - ADAM_B2 ** ADAM_STEP)
        d_ref[...] = -ADAM_LR * (m_hat / (jnp.sqrt(v_hat) + ADAM_EPS) + ADAM_WD * w_ref[...])
        nm_ref[...] = nm
        nv_ref[...] = nv

    spec = pl.BlockSpec((tr, c), lambda i: (i, 0))
    shp = jax.ShapeDtypeStruct((r, c), F32)
    return pl.pallas_call(body, name=name, grid=(r // tr,), in_specs=[spec] * 4, out_specs=[spec] * 3,
                          out_shape=[shp] * 3, compiler_params=_cp(("parallel",)))(w, g, m, v)


SEG_QFIG, SEG_Z, SEG_XBC, SEG_DT, SEG_G = 0, 8192, 12288, 18432, 18496
IN_TOTAL = 22592
FFN_BLOCKS = D_FF // CONV_CB


def _own_slot(gathered, own, shard):
    slot = lax.broadcasted_iota(jnp.int32, (gathered.shape[0],) + (1,) * own.ndim, 0)
    return jnp.where(slot == shard, own[None], gathered)


def local_step(x, target, wts, par, p_rest, p_up, shard, core):
    t = x.shape[0]
    pad64 = lambda a: jnp.pad(a, ((0, 0), (0, DT_PAD - a.shape[1])))
    bias, a_log = pad64(par["ssm_dt_bias"]), pad64(par["ssm_A_log"])
    dexp = jnp.repeat(par["ssm_D"], SSM_HEADDIM, axis=1)
    in_t = wts["in_t"]

    h = norm_fwd(x, par["mix_pre_norm"])
    proj = lambda nm, off, n, tn: mm(h, in_t, "nt", name=nm, tn=tn, dims=(t, n, D), b_off=(off, 0))
    qfig = proj("proj_qfig", SEG_QFIG, 8192, 1024)
    z = proj("proj_z", SEG_Z, 4096, 1024)
    xbc = proj("proj_xbc", SEG_XBC, 6144, 1024)
    dtr = mm(h, wts["dt_t"], "nt", name="proj_dt", tn=128)
    gates = mm(h, wts["g_t"], "nt", name="proj_gates", tn=1024)
    y_hg, hg_states, g_rest, g_up = hgrn2_fwd(qfig, par["hg_lb_table"], par["hg_out_norm"],
                                              gather=([p_rest, p_up], [REST_PIECES, UP_PIECES]))
    g_rest, g_up = _own_slot(g_rest, p_rest, shard), _own_slot(g_up, p_up, shard)
    r0, r1, r2, r3 = REST_SPLITS
    wts = dict(wts, bh=g_rest[:, :r0].reshape(-1, D), bs=g_rest[:, r0:r1].reshape(-1, D), o=g_rest[:, r1:r2].reshape(-1, D),
               dn=g_rest[:, r2:r3].reshape(-1, D), up=jnp.transpose(g_up, (1, 0, 2)).reshape(D, 2 * D_FF),
               up_t=jnp.transpose(g_up, (0, 2, 1)).reshape(2 * D_FF, D))
    c_ssm, xbc_act = conv_fwd("ssm_conv_fwd", xbc, 0, par["ssm_conv_w"], par["ssm_conv_b"], "silu")
    dt, acum = dt_fwd(dtr, bias, a_log)
    yssd, ssd_states = ssd_fwd(xbc_act, dt, acum)
    y_ssm = ssm_post_fwd(yssd, xbc_act, z, dexp, par["ssm_out_norm"])
    a_hg = mm(y_hg, wts["bh"], "nn", name="branch_hg", tn=1024)
    a_ssm = mm(y_ssm, wts["bs"], "nn", name="branch_ssm", tn=1024)
    mixed = merge_fwd(a_hg, a_ssm, gates)
    u = mm(mixed, wts["o"], "nn", name="out_proj", tn=1024)
    x1, h2 = post1_fwd(x, u, par["mix_post_norm"], par["ffn_pre_norm"])
    gu = mm(h2, wts["up"], "nn", name="ffn_up", tn=1024)
    c_ffn, act = conv_fwd("ffn_conv_fwd", gu, 0, par["ffn_conv_w"], par["ffn_conv_b"], "gelu_mul",
                          up=gu, upoff=FFN_BLOCKS, act_dtype=BF16)
    fo = mm(act, wts["dn"], "nn", name="ffn_down", tn=1024)
    dx2, dfo, g_ffn_post, loss = final_fwd_bwd(x1, fo, par["ffn_post_norm"], target)

    dact = mm(dfo, wts["dn"], "nt", name="d_act", out_dtype=BF16, tn=1408)
    g_dn = mm(act, dfo, "tn", name="g_ffn_down", out_dtype=BF16, tm=1408, tn=2048, tk=1024)
    dgate, dup, g_fcw, g_fcb = conv_bwd("ffn_conv_bwd", gu, 0, c_ffn, 0, dact, par["ffn_conv_w"], "gelu_mul",
                                        up=gu, upoff=FFN_BLOCKS)
    dh2 = mm_segments("d_h2", [(dgate, 0, 0), (dup, 0, D_FF)], [wts["up_t"]], tm=1024, tn=1024, tk=1408)
    g_up_gate = mm(h2, dgate, "tn", name="g_ffn_up_gate", out_dtype=BF16, tm=2048, tn=1408, tk=1024)
    g_up_up = mm(h2, dup, "tn", name="g_ffn_up_up", out_dtype=BF16, tm=2048, tn=1408, tk=1024)
    du, dx1, g_mix_post, g_ffn_pre = post1_bwd(x, u, par["mix_post_norm"], par["ffn_pre_norm"], dx2, dh2)
    dmixed = mm(du, wts["o"], "nt", name="d_mixed", tn=1024)
    g_o = mm(mixed, du, "tn", name="g_w_out", out_dtype=BF16, tm=1024, tn=2048, tk=1024)
    da_hg, da_ssm, dg_hg, dg_ssm = merge_bwd(a_hg, a_ssm, gates, dmixed)
    dy_hg = mm(da_hg, wts["bh"], "nt", name="d_y_hg", out_dtype=BF16, tn=1024)
    g_bh = mm(y_hg, da_hg, "tn", name="g_w_branch_hg", out_dtype=BF16, tm=1024, tn=2048, tk=1024)
    dy_ssm = mm(da_ssm, wts["bs"], "nt", name="d_y_ssm", out_dtype=BF16, tn=1024)
    g_bs = mm(y_ssm, da_ssm, "tn", name="g_w_branch_ssm", out_dtype=BF16, tm=1024, tn=2048, tk=1024)
    dyssd, dskip, dz, g_dexp, g_ssm_norm = ssm_post_bwd(yssd, xbc_act, z, dexp, par["ssm_out_norm"], dy_ssm)

    gg_rest = jnp.concatenate([g.reshape(N_CHIPS, -1, D) for g in (g_bh, g_bs, g_o, g_dn)], axis=1)
    gg_up = jnp.transpose(jnp.concatenate([g_up_gate, g_up_up], axis=1).reshape(D, N_CHIPS, UP_COLS), (1, 0, 2))
    c_rest, c_up = pair_reduce("rest", [gg_rest, gg_up], [REST_PIECES, UP_PIECES], [432, 512], core)
    dxs, db_, dc_, ddt, dacum, rb_rest, rb_up = ssd_bwd(xbc_act, dt, acum, ssd_states, dyssd, dskip, exchange=[c_rest, c_up])
    red_rest, red_up = chip_reduce("rest", [c_rest, c_up], [rb_rest, rb_up], [432, 256], shard)
    ddtr, g_dt_bias, g_a_log = dt_bwd(dtr, bias, a_log, ddt, dacum)
    xs_blocks, bc_blocks = SSM_DINNER // CONV_CB, SSM_GROUPS * SSM_DSTATE // CONV_CB
    dxbc_x, g_cw_x, g_cb_x = conv_bwd("ssm_conv_bwd_x", xbc, 0, c_ssm, 0, dxs, par["ssm_conv_w"], "silu")
    dxbc_b, g_cw_b, g_cb_b = conv_bwd("ssm_conv_bwd_b", xbc, xs_blocks, c_ssm, xs_blocks, db_, par["ssm_conv_w"], "silu")
    dxbc_c, g_cw_c, g_cb_c = conv_bwd("ssm_conv_bwd_c", xbc, xs_blocks + bc_blocks, c_ssm, xs_blocks + bc_blocks, dc_,
                                      par["ssm_conv_w"], "silu")
    dq, df, dv, dg, g_table, g_hg_norm = hgrn2_bwd(qfig, par["hg_lb_table"], par["hg_out_norm"], hg_states, dy_hg)

    dsegs = [(dq, SEG_QFIG), (df, SEG_QFIG + 2048), (dv, SEG_QFIG + 4096), (dg, SEG_QFIG + 6144), (dz, SEG_Z),
             (dxbc_x, SEG_XBC), (dxbc_b, SEG_XBC + SSM_DINNER), (dxbc_c, SEG_XBC + SSM_DINNER + 1024)]
    g_in_parts = [mm(dseg, h, "tn", name=f"g_w_in_{n}", out_dtype=BF16, tm=1024, tn=2048, tk=1024)
                  for n, (dseg, _) in enumerate(dsegs)]
    g_dt_t = mm(ddtr, h, "tn", name="g_w_in_dt", out_dtype=BF16, tm=128, tn=2048, tk=1024)[:SSM_HEADS]
    g_in_parts += [mm(dgate_, h, "tn", name=f"g_w_in_g{n}", out_dtype=BF16, tm=1024, tn=2048, tk=1024)
                   for n, dgate_ in enumerate((dg_hg, dg_ssm))]
    zpad = jnp.zeros((N_CHIPS, IN_ROWS - IN_SHARD, D), BF16)
    g_in_t = jnp.concatenate(g_in_parts[:8] + [g_dt_t] + g_in_parts[8:], axis=0).reshape(N_CHIPS, IN_SHARD, D)
    (c_in,) = pair_reduce("in", [jnp.concatenate([g_in_t, zpad], axis=1)], [IN_PIECES], [960], core)
    dh = mm_segments("d_h_a", [(dseg, 0, off) for dseg, off in dsegs[:5]], [in_t], tm=1024, tn=1024, tk=1024)
    dh, rb_in = mm_segments("d_h_b", [(dseg, 0, off) for dseg, off in dsegs[5:]] + [(ddtr, 1, 0), (dg_hg, 2, 0), (dg_ssm, 2, D)],
                            [in_t, wts["dt_t"], wts["g_t"]], tm=1024, tn=1024, tk=1024, acc=dh, exchange=[c_in])
    (red_in,) = chip_reduce("in", [c_in], [rb_in], [480], shard)
    grad_x, g_mix_pre = norm_bwd(x, par["mix_pre_norm"], dh, dx1)

    big = dict(in_t=red_in, rest=red_rest, up=red_up)
    g_conv_w = jnp.concatenate([g_cw_x, g_cw_b, g_cw_c], axis=1)[:SSM_CONV]
    g_conv_b = jnp.concatenate([g_cb_x, g_cb_b, g_cb_c], axis=1)
    small = dict(mix_pre_norm=g_mix_pre, mix_post_norm=g_mix_post, hg_lb_table=g_table, hg_out_norm=g_hg_norm,
                 ssm_conv_w=g_conv_w, ssm_conv_b=g_conv_b, ssm_dt_bias=g_dt_bias, ssm_A_log=g_a_log,
                 ssm_D=g_dexp, ssm_out_norm=g_ssm_norm, ffn_pre_norm=g_ffn_pre, ffn_post_norm=g_ffn_post,
                 ffn_conv_w=g_fcw[:FFN_CONV], ffn_conv_b=g_fcb)
    return loss, grad_x, big, small


MESH = pl.DeviceIdType.MESH
ANY = pl.BlockSpec(memory_space=pl.ANY)
N_CHIPS = 4
IN_SHARD = 5648
IN_ROWS = 5760
REST_SPLITS = (512, 1536, 2048, 3456)
UP_COLS = 2816
IN_PIECES, REST_PIECES, UP_PIECES = 3, 4, 4


def _place():
    x, y, c = lax.axis_index("x"), lax.axis_index("y"), lax.axis_index("c")
    chips = [(1 - x, y), (x, 1 - y), (1 - x, 1 - y)]
    return x, y, c, chips


def _rcopy(src, dst, send_sems, recv_sems, k, dev):
    return pltpu.make_async_remote_copy(src_ref=src, dst_ref=dst, send_sem=send_sems.at[k], recv_sem=recv_sems.at[k],
                                        device_id=dev, device_id_type=MESH)


def _pieces(rows, n):
    assert rows % n == 0 and (rows // n) % 16 == 0, (rows, n)
    return [(k * (rows // n), rows // n) for k in range(n)]


def _rows(c, hrows, piece):
    return pl.ds(pl.multiple_of(c * hrows + piece[0], 16), piece[1])


def _half_plan(arrays, pieces):
    return [(a.shape[-2] // 2, _pieces(a.shape[-2] // 2, n)) for a, n in zip(arrays, pieces)]


def _sem_pair(n):
    return [pltpu.SemaphoreType.DMA((n,)), pltpu.SemaphoreType.DMA((n,))]


class _Gather:
    def __init__(self, ps, pieces):
        self.plan = _half_plan(ps, pieces)
        self.n_sem = sum(2 * 3 * len(pcs) for _, pcs in self.plan)
        self.out_shape = [jax.ShapeDtypeStruct((N_CHIPS,) + p.shape, p.dtype) for p in ps]

    def _copies(self, p_refs, g_refs, send_sems, recv_sems, only_first=False):
        x, y, c, chips = _place()
        own = 2 * x + y
        sib = (x, y, 1 - c)
        first, arrive, passed, from_sib = [], [], [], []
        k = 0
        for p, g, (hrows, pcs) in zip(p_refs, g_refs, self.plan):
            for chip in chips:
                theirs = 2 * chip[0] + chip[1]
                for pc in pcs:
                    mine, other = _rows(c, hrows, pc), _rows(1 - c, hrows, pc)
                    first.append(_rcopy(p.at[mine], g.at[own, mine], send_sems, recv_sems, k, (*chip, c)))
                    if not only_first:
                        arrive.append(_rcopy(g.at[theirs, mine], g.at[theirs, mine], send_sems, recv_sems, k, (*chip, c)))
                        passed.append(_rcopy(g.at[theirs, mine], g.at[theirs, mine], send_sems, recv_sems, k + 1, sib))
                        from_sib.append(_rcopy(g.at[theirs, other], g.at[theirs, other], send_sems, recv_sems, k + 1, sib))
                    k += 2
        return first, arrive, passed, from_sib

    def start(self, p_refs, g_refs, send_sems, recv_sems):
        for cp in self._copies(p_refs, g_refs, send_sems, recv_sems, only_first=True)[0]:
            cp.start()

    def finish(self, p_refs, g_refs, send_sems, recv_sems):
        first, arrive, passed, from_sib = self._copies(p_refs, g_refs, send_sems, recv_sems)
        for got, fw in zip(arrive, passed):
            got.wait_recv()
            fw.start()
        for cp in from_sib:
            cp.wait_recv()
        for cp in first + passed:
            cp.wait_send()


def gather_weights(name, ps, pieces):
    op = _Gather(ps, pieces)
    n = len(ps)

    def body(*refs):
        p_refs, g_refs, sems = refs[:n], refs[n:2 * n], refs[2 * n:]
        op.start(p_refs, g_refs, *sems)
        op.finish(p_refs, g_refs, *sems)

    return pl.pallas_call(body, name=name, in_specs=[ANY] * n, out_specs=[ANY] * n, out_shape=op.out_shape,
                          scratch_shapes=_sem_pair(op.n_sem))(*ps)


def pair_exchange(name, gs, pieces):
    plan = _half_plan(gs, pieces)
    n_sem = sum(N_CHIPS * len(pcs) for _, pcs in plan)
    n = len(gs)

    def body(*refs):
        g_refs, r_refs, send_sems, recv_sems = refs[:n], refs[n:2 * n], refs[2 * n], refs[2 * n + 1]
        x, y, c, _ = _place()
        sib = (x, y, 1 - c)
        cps = []
        for g, r, (hrows, pcs) in zip(g_refs, r_refs, plan):
            for s in range(N_CHIPS):
                for pc in pcs:
                    cps.append(_rcopy(g.at[s, _rows(1 - c, hrows, pc)], r.at[s, pl.ds(pc[0], pc[1])],
                                      send_sems, recv_sems, len(cps), sib))
        for cp in cps:
            cp.start()
        for cp in cps:
            cp.wait()

    return pl.pallas_call(
        body, name=name, in_specs=[ANY] * n, out_specs=[ANY] * n,
        out_shape=[jax.ShapeDtypeStruct((N_CHIPS, g.shape[1] // 2, g.shape[2]), g.dtype) for g in gs],
        scratch_shapes=_sem_pair(n_sem))(*gs)


class _ChipExchange:
    def __init__(self, ss):
        self.n_sem = 3 * len(ss)
        self.out_shape = [jax.ShapeDtypeStruct((3,) + s.shape[1:], s.dtype) for s in ss]

    def _copies(self, s_refs, r_refs, send_sems, recv_sems):
        x, y, c, chips = _place()
        cps = []
        for s, r in zip(s_refs, r_refs):
            for j, chip in enumerate(chips):
                cps.append(_rcopy(s.at[2 * chip[0] + chip[1]], r.at[j], send_sems, recv_sems, len(cps), (*chip, c)))
        return cps

    def start(self, *refs):
        for cp in self._copies(*refs):
            cp.start()

    def finish(self, *refs):
        for cp in self._copies(*refs):
            cp.wait()


def pair_assemble(name, rs, pieces):
    plan = [(r.shape[0], _pieces(r.shape[0], n_)) for r, n_ in zip(rs, pieces)]
    n_sem = sum(len(pcs) for _, pcs in plan)
    n = len(rs)

    def body(*refs):
        r_refs, f_refs, send_sems, recv_sems = refs[:n], refs[n:2 * n], refs[2 * n], refs[2 * n + 1]
        x, y, c, _ = _place()
        sib = (x, y, 1 - c)
        cps, got = [], []
        for r, f, (hrows, pcs) in zip(r_refs, f_refs, plan):
            for pc in pcs:
                src = r.at[pl.ds(pc[0], pc[1])]
                cps.append(_rcopy(src, f.at[_rows(c, hrows, pc)], send_sems, recv_sems, len(cps), sib))
                got.append(_rcopy(src, f.at[_rows(1 - c, hrows, pc)], send_sems, recv_sems, len(got), sib))
        for cp in cps:
            cp.start()
        for cp in got:
            cp.wait_recv()
        for cp in cps:
            cp.wait_send()

    return pl.pallas_call(
        body, name=name, in_specs=[ANY] * n, out_specs=[ANY] * n,
        out_shape=[jax.ShapeDtypeStruct((2 * r.shape[0], r.shape[1]), r.dtype) for r in rs],
        scratch_shapes=_sem_pair(n_sem))(*rs)


def pair_reduce(tag, ggs, pieces, trs, core):
    recv = pair_exchange("pair_exchange_" + tag, ggs, pieces)
    flat = lambda a: a.reshape(-1, a.shape[-1])
    out = []
    for n, (gg, r, tr) in enumerate(zip(ggs, recv, trs)):
        h = gg.shape[1] // 2
        own = lax.dynamic_slice_in_dim(gg, core * h, h, axis=1)
        out.append(ew_sum(f"pair_sum_{tag}_{n}", [(flat(own), 0), (flat(r), 0)], N_CHIPS * h, BF16, tr).reshape(r.shape))
    return out


def chip_reduce(tag, cs, rbs, trs, shard):
    out = []
    for n, (c, rb, tr) in enumerate(zip(cs, rbs, trs)):
        h = c.shape[1]
        own = lax.dynamic_index_in_dim(c, shard, axis=0, keepdims=False)
        parts = [(own, 0)] + [(rb.reshape(-1, rb.shape[-1]), j * h) for j in range(3)]
        out.append(ew_sum(f"chip_sum_{tag}_{n}", parts, h, F32, tr))
    return out


N_DEV = 8


def gather_small(blk, reduce):
    rows, cols = blk.shape

    def body(x_ref, out_ref, all_ref, send_sems, recv_sems, local_sem):
        x, y, c, chips = _place()
        me, sib = (x, y, c), (x, y, 1 - c)

        def blk_rows(px, py, pc):
            return all_ref.at[pl.ds(pl.multiple_of((4 * px + 2 * py + pc) * rows, 8), rows), :]

        def copy(k, block, to, src=None):
            return _rcopy(blk_rows(*block) if src is None else src, blk_rows(*block), send_sems, recv_sems, k, to)

        mine = pltpu.make_async_copy(x_ref, blk_rows(*me), local_sem)
        mine.start()
        first = [copy(0, me, sib, src=x_ref)] + [copy(1 + j, me, (*chip, c), src=x_ref) for j, chip in enumerate(chips)]
        for cp in first:
            cp.start()
        passed = [copy(4 + j, (*chip, c), sib) for j, chip in enumerate(chips)]
        for j, chip in enumerate(chips):
            copy(1 + j, (*chip, c), me).wait_recv()
            passed[j].start()
        copy(0, sib, me).wait_recv()
        for j, chip in enumerate(chips):
            copy(4 + j, (*chip, 1 - c), me).wait_recv()
        for cp in first + passed:
            cp.wait_send()
        mine.wait()
        if reduce:
            acc = all_ref[0:rows, :]
            for d in range(1, N_DEV):
                acc = acc + all_ref[d * rows:(d + 1) * rows, :]
            out_ref[...] = acc
        else:
            out_ref[...] = all_ref[...]

    vmem = pl.BlockSpec(memory_space=pltpu.VMEM)
    return pl.pallas_call(
        body, name="reduce_small" if reduce else "gather_small", in_specs=[vmem], out_specs=vmem,
        out_shape=jax.ShapeDtypeStruct((rows if reduce else N_DEV * rows, cols), blk.dtype),
        scratch_shapes=[pltpu.VMEM((N_DEV * rows, cols), blk.dtype), pltpu.SemaphoreType.DMA((7,)),
                        pltpu.SemaphoreType.DMA((7,)), pltpu.SemaphoreType.DMA],
        compiler_params=pltpu.CompilerParams(vmem_limit_bytes=VMEM_LIMIT),
    )(blk)


WEIGHTS = ['w_in', 'mix_pre_norm', 'mix_post_norm', 'hg_lb_table', 'hg_out_norm', 'ssm_conv_w', 'ssm_conv_b',
           'ssm_dt_bias', 'ssm_A_log', 'ssm_D', 'ssm_out_norm', 'w_branch_hg', 'w_branch_ssm', 'w_out', 'ffn_pre_norm',
           'ffn_post_norm', 'ffn_w_up', 'ffn_conv_w', 'ffn_conv_b', 'ffn_w_down']
BIG = ('w_in', 'w_branch_hg', 'w_branch_ssm', 'w_out', 'ffn_w_up', 'ffn_w_down')
SMALL = tuple(n for n in WEIGHTS if n not in BIG)
CONV_SHARD = {'ssm_conv_w': SSM_CONV_DIM // N_CHIPS, 'ffn_conv_w': D_FF // N_CHIPS}
LANES = 128


def _pack(parts):
    flat = jnp.concatenate([p.reshape(-1) for p in parts])
    n = flat.shape[0]
    rows = -(-n // (8 * LANES)) * 8
    return jnp.pad(flat, (0, rows * LANES - n)).reshape(rows, LANES)


def _unpack(packed, shapes):
    flat = packed.reshape(-1)
    out, off = [], 0
    for s in shapes:
        n = int(np.prod(s))
        out.append(flat[off:off + n].reshape(s))
        off += n
    return out


def kernel(x, w_in, mix_pre_norm, mix_post_norm, hg_lb_table, hg_out_norm, ssm_conv_w, ssm_conv_b, ssm_dt_bias, ssm_A_log, ssm_D, ssm_out_norm, w_branch_hg, w_branch_ssm, w_out, ffn_pre_norm, ffn_post_norm, ffn_w_up, ffn_conv_w, ffn_conv_b, ffn_w_down, loss_target, m_w_in, m_mix_pre_norm, m_mix_post_norm, m_hg_lb_table, m_hg_out_norm, m_ssm_conv_w, m_ssm_conv_b, m_ssm_dt_bias, m_ssm_A_log, m_ssm_D, m_ssm_out_norm, m_w_branch_hg, m_w_branch_ssm, m_w_out, m_ffn_pre_norm, m_ffn_post_norm, m_ffn_w_up, m_ffn_conv_w, m_ffn_conv_b, m_ffn_w_down, v_w_in, v_mix_pre_norm, v_mix_post_norm, v_hg_lb_table, v_hg_out_norm, v_ssm_conv_w, v_ssm_conv_b, v_ssm_dt_bias, v_ssm_A_log, v_ssm_D, v_ssm_out_norm, v_w_branch_hg, v_w_branch_ssm, v_w_out, v_ffn_pre_norm, v_ffn_post_norm, v_ffn_w_up, v_ffn_conv_w, v_ffn_conv_b, v_ffn_w_down):
    w = dict(w_in=w_in, mix_pre_norm=mix_pre_norm, mix_post_norm=mix_post_norm, hg_lb_table=hg_lb_table, hg_out_norm=hg_out_norm, ssm_conv_w=ssm_conv_w, ssm_conv_b=ssm_conv_b, ssm_dt_bias=ssm_dt_bias, ssm_A_log=ssm_A_log, ssm_D=ssm_D, ssm_out_norm=ssm_out_norm, w_branch_hg=w_branch_hg, w_branch_ssm=w_branch_ssm, w_out=w_out, ffn_pre_norm=ffn_pre_norm, ffn_post_norm=ffn_post_norm, ffn_w_up=ffn_w_up, ffn_conv_w=ffn_conv_w, ffn_conv_b=ffn_conv_b, ffn_w_down=ffn_w_down)
    m = dict(w_in=m_w_in, mix_pre_norm=m_mix_pre_norm, mix_post_norm=m_mix_post_norm, hg_lb_table=m_hg_lb_table, hg_out_norm=m_hg_out_norm, ssm_conv_w=m_ssm_conv_w, ssm_conv_b=m_ssm_conv_b, ssm_dt_bias=m_ssm_dt_bias, ssm_A_log=m_ssm_A_log, ssm_D=m_ssm_D, ssm_out_norm=m_ssm_out_norm, w_branch_hg=m_w_branch_hg, w_branch_ssm=m_w_branch_ssm, w_out=m_w_out, ffn_pre_norm=m_ffn_pre_norm, ffn_post_norm=m_ffn_post_norm, ffn_w_up=m_ffn_w_up, ffn_conv_w=m_ffn_conv_w, ffn_conv_b=m_ffn_conv_b, ffn_w_down=m_ffn_w_down)
    v = dict(w_in=v_w_in, mix_pre_norm=v_mix_pre_norm, mix_post_norm=v_mix_post_norm, hg_lb_table=v_hg_lb_table, hg_out_norm=v_hg_out_norm, ssm_conv_w=v_ssm_conv_w, ssm_conv_b=v_ssm_conv_b, ssm_dt_bias=v_ssm_dt_bias, ssm_A_log=v_ssm_A_log, ssm_D=v_ssm_D, ssm_out_norm=v_ssm_out_norm, w_branch_hg=v_w_branch_hg, w_branch_ssm=v_w_branch_ssm, w_out=v_w_out, ffn_pre_norm=v_ffn_pre_norm, ffn_post_norm=v_ffn_post_norm, ffn_w_up=v_ffn_w_up, ffn_conv_w=v_ffn_conv_w, ffn_conv_b=v_ffn_conv_b, ffn_w_down=v_ffn_w_down)
    shard = 2 * lax.axis_index("x") + lax.axis_index("y")
    bf = lambda a: a.astype(BF16)

    core = lax.axis_index("c")
    p_in = jnp.concatenate([bf(w_in[0].T), jnp.zeros((IN_ROWS - IN_SHARD, D_MODEL), BF16)], axis=0)
    p_rest = jnp.concatenate([bf(w_branch_hg[0]), bf(w_branch_ssm[0]), bf(w_out[0]), bf(ffn_w_down[0])], axis=0)
    p_up = bf(ffn_w_up[0])
    (g_in,) = gather_weights("gather_w_in", [p_in], [IN_PIECES])
    in_t = _own_slot(g_in, p_in, shard)[:, :IN_SHARD].reshape(IN_TOTAL, D_MODEL)
    wts = dict(in_t=in_t, g_t=in_t[SEG_G:], dt_t=jnp.pad(in_t[SEG_DT:SEG_G], ((0, DT_PAD - SSM_HEADS), (0, 0))))
    conv_cols = max(CONV_SHARD.values())
    padc = lambda a: jnp.pad(a, ((0, 0), (0, conv_cols - a.shape[1])))
    conv_blk = jnp.concatenate([padc(ssm_conv_w[0]), padc(ffn_conv_w[0]), jnp.zeros((1, conv_cols), F32)], axis=0)
    conv_all = gather_small(conv_blk, reduce=False)
    par = {n: w[n] for n in SMALL}
    par["ssm_conv_w"] = jnp.concatenate([conv_all[16 * s:16 * s + SSM_CONV, :CONV_SHARD['ssm_conv_w']] for s in range(N_CHIPS)], axis=1)
    par["ffn_conv_w"] = jnp.concatenate([conv_all[16 * s + SSM_CONV:16 * s + SSM_CONV + FFN_CONV, :CONV_SHARD['ffn_conv_w']]
                                         for s in range(N_CHIPS)], axis=1)

    loss, grad_x, big, small = local_step(x[0], loss_target[0], wts, par, p_rest, p_up, shard, core)
    loss = lax.psum(loss[0, 0], ("x", "y", "c"))

    halves = [big["in_t"], big["rest"], big["up"]]
    wholes = pair_assemble("pair_assemble", halves, [IN_PIECES, REST_PIECES, UP_PIECES])
    f_in, f_rest, f_up = [_own_slot(f.reshape((2,) + r.shape), r, core).reshape(f.shape) for f, r in zip(wholes, halves)]
    r0, r1, r2, r3 = REST_SPLITS
    grads = dict(w_in=f_in[:IN_SHARD].T, w_branch_hg=f_rest[:r0], w_branch_ssm=f_rest[r0:r1], w_out=f_rest[r1:r2],
                 ffn_w_down=f_rest[r2:r3], ffn_w_up=f_up)

    small["hg_out_norm"] = ew_sum("sum_heads", [(small["hg_out_norm"][hd], 0) for hd in range(HG_HEADS)], 1, F32, 1)
    small["ssm_D"] = fold_heads(small["ssm_D"])[:, :SSM_HEADS]
    small["ssm_dt_bias"] = small["ssm_dt_bias"][:, :SSM_HEADS]
    small["ssm_A_log"] = small["ssm_A_log"][:, :SSM_HEADS]
    shapes = [small[n].shape for n in SMALL]
    summed = _unpack(gather_small(_pack([small[n] for n in SMALL]), reduce=True), shapes)
    for n, g in zip(SMALL, summed):
        if n in CONV_SHARD:
            g = lax.dynamic_slice_in_dim(g, shard * CONV_SHARD[n], CONV_SHARD[n], axis=1)
        grads[n] = g

    two_d = lambda a: a.reshape(a.shape[-2], a.shape[-1])
    delta, new_m, new_v = {}, {}, {}
    for n, tr in (("w_in", 64), ("w_branch_hg", 128), ("w_branch_ssm", 128), ("w_out", 128), ("ffn_w_up", 128), ("ffn_w_down", 128)):
        delta[n], new_m[n], new_v[n] = adamw("adamw_" + n, two_d(w[n]), grads[n], two_d(m[n]), two_d(v[n]), tr)
    sm_shapes = [two_d(w[n]).shape for n in SMALL]
    packed = adamw("adamw_small", _pack([two_d(w[n]) for n in SMALL]), _pack([grads[n] for n in SMALL]),
                   _pack([two_d(m[n]) for n in SMALL]), _pack([two_d(v[n]) for n in SMALL]), 1024)
    for res, packed_res in zip((delta, new_m, new_v), packed):
        for n, a in zip(SMALL, _unpack(packed_res, sm_shapes)):
            res[n] = a
    shaped = lambda d: [d[n].reshape(w[n].shape) for n in WEIGHTS]
    return (loss, grad_x[None], *shaped(grads), *shaped(delta), *shaped(new_m), *shaped(new_v))
```

```python
import functools

import jax
import jax.numpy as jnp
import numpy as np
from jax import lax
from jax.experimental import pallas as pl
from jax.experimental.pallas import tpu as pltpu

F32 = jnp.float32
BF16 = jnp.bfloat16

D_MODEL = 2048
EPS = 1e-6
HG_HEADS = 16
HG_DK = 128
HG_CHUNK = 64
HG_SUB = 16
SSM_DINNER = 4096
SSM_HEADDIM = 64
SSM_HEADS = 64
SSM_GROUPS = 8
SSM_DSTATE = 128
SSM_CONV = 4
SSM_CHUNK = 256
SSM_CONV_DIM = 6144
D_FF = 5632
FFN_CONV = 3
DT_PAD = 128

ADAM_LR = 0.001
ADAM_B1 = 0.9
ADAM_B2 = 0.999
ADAM_EPS = 1e-08
ADAM_WD = 0.01
ADAM_STEP = 10

VMEM_LIMIT = 56 * 1024 * 1024
HI = lax.Precision.HIGHEST


def _cp(sem, **kw):
    return pltpu.CompilerParams(dimension_semantics=sem, vmem_limit_bytes=VMEM_LIMIT, **kw)


_DIMS = {"nn": (((1,), (0,)), ((), ())), "nt": (((1,), (1,)), ((), ())), "tn": (((0,), (0,)), ((), ()))}


def mm(a, b, mode, *, name, out_dtype=F32, tm=1024, tn=512, tk=None, acc=None, n_major=True,
       dims=None, a_off=(0, 0), b_off=(0, 0)):
    if dims is not None:
        M, N, K = dims
    else:
        if mode == "nn":
            (M, K), (K2, N) = a.shape, b.shape
        elif mode == "nt":
            (M, K), (N, K2) = a.shape, b.shape
        else:
            (K, M), (K2, N) = a.shape, b.shape
        assert K == K2, (a.shape, b.shape, mode)
    tm, tn = min(tm, M), min(tn, N)
    tk = K if tk is None else min(tk, K)
    assert M % tm == 0 and N % tn == 0 and K % tk == 0, (M, N, K, tm, tn, tk)
    a_blk = (tk, tm) if mode == "tn" else (tm, tk)
    b_blk = (tn, tk) if mode == "nt" else (tk, tn)
    assert all(o % s == 0 for o, s in zip(a_off, a_blk)) and all(o % s == 0 for o, s in zip(b_off, b_blk))
    ao0, ao1 = a_off[0] // a_blk[0], a_off[1] // a_blk[1]
    bo0, bo1 = b_off[0] // b_blk[0], b_off[1] // b_blk[1]
    nk = K // tk
    if n_major:
        grid = (N // tn, M // tm, nk)
        ij = lambda p0, p1: (p1, p0)
    else:
        grid = (M // tm, N // tn, nk)
        ij = lambda p0, p1: (p0, p1)

    def a_map(p0, p1, k):
        i, _ = ij(p0, p1)
        return (k + ao0, i + ao1) if mode == "tn" else (i + ao0, k + ao1)

    def b_map(p0, p1, k):
        _, j = ij(p0, p1)
        return (j + bo0, k + bo1) if mode == "nt" else (k + bo0, j + bo1)

    def o_map(p0, p1, k):
        return ij(p0, p1)

    a_spec = pl.BlockSpec(a_blk, a_map)
    b_spec = pl.BlockSpec(b_blk, b_map)
    o_spec = pl.BlockSpec((tm, tn), o_map)
    dims = _DIMS[mode]
    has_acc = acc is not None

    def body(*refs):
        if has_acc:
            a_ref, b_ref, c_ref, o_ref, acc_ref = refs
        else:
            a_ref, b_ref, o_ref, acc_ref = refs
        k = pl.program_id(2)
        part = lax.dot_general(a_ref[...], b_ref[...], dims, preferred_element_type=F32)
        if nk == 1:
            o_ref[...] = (part + c_ref[...].astype(F32) if has_acc else part).astype(out_dtype)
            return

        @pl.when(k == 0)
        def _():
            acc_ref[...] = part

        @pl.when(k > 0)
        def _():
            acc_ref[...] += part

        @pl.when(k == nk - 1)
        def _():
            r = acc_ref[...]
            if has_acc:
                r = r + c_ref[...].astype(F32)
            o_ref[...] = r.astype(out_dtype)

    in_specs = [a_spec, b_spec] + ([o_spec] if has_acc else [])
    args = (a, b) + ((acc,) if has_acc else ())
    return pl.pallas_call(
        body, name=name, grid=grid, in_specs=in_specs, out_specs=o_spec,
        out_shape=jax.ShapeDtypeStruct((M, N), out_dtype),
        scratch_shapes=[pltpu.VMEM((tm, tn) if nk > 1 else (8, 128), F32)],
        compiler_params=_cp(("parallel", "parallel", "arbitrary")),
    )(*args)


def mm_segments(name, segs, bs, *, tm, tn, tk, acc=None, exchange=None):
    m_, n_ = segs[0][0].shape[0], bs[0].shape[1]
    tm, tn = min(tm, m_), min(tn, n_)
    op = _ChipExchange(exchange) if exchange else None
    ne = (len(exchange) if exchange else 0)
    na = 0 if acc is None else 1
    steps, k0 = [], 0
    for a, bi, row in segs:
        w = a.shape[1]
        tks = min(tk, w)
        assert w % tks == 0 and row % tks == 0 and tks == min(tk, bs[bi].shape[0]), (w, row, tks)
        steps.append((k0, w // tks, tks, bi, row // tks))
        k0 += w // tks
    nk = k0
    assert m_ % tm == 0 and n_ % tn == 0

    def a_spec(k_first, count, tks):
        return pl.BlockSpec((tm, tks), lambda j, i, k: (i, jnp.clip(k - k_first, 0, count - 1)))

    def b_spec(bi):
        mine = [s for s in steps if s[3] == bi]

        def index(j, i, k):
            blk = mine[0][4]
            for k_first, count, _, _, first_blk in mine:
                blk = jnp.where(k >= k_first, first_blk + jnp.minimum(k - k_first, count - 1), blk)
            return (blk, j)
        return pl.BlockSpec((mine[0][2], tn), index)

    ns = len(segs)

    nb = len(bs)
    grid = (n_ // tn, m_ // tm, nk)

    def body(*refs):
        a_refs, b_refs = refs[:ns], refs[ns:ns + nb]
        acc_in = refs[ns + nb] if na else None
        rest = refs[ns + nb + na:]
        ex_refs, o_ref, got_refs, acc_ref, sems = rest[:ne], rest[ne], rest[ne + 1:2 * ne + 1], rest[2 * ne + 1], rest[2 * ne + 2:]
        k = pl.program_id(2)
        if op:
            first = (pl.program_id(0) == 0) & (pl.program_id(1) == 0) & (k == 0)
            last = (pl.program_id(0) == grid[0] - 1) & (pl.program_id(1) == grid[1] - 1) & (k == nk - 1)

            @pl.when(first)
            def _():
                op.start(ex_refs, got_refs, *sems)

            @pl.when(last)
            def _():
                op.finish(ex_refs, got_refs, *sems)

        @pl.when(k == 0)
        def _():
            acc_ref[...] = acc_in[...] if na else jnp.zeros_like(acc_ref)

        for a_ref, (k_first, count, _, bi, _) in zip(a_refs, steps):
            @pl.when((k >= k_first) & (k < k_first + count))
            def _(a_ref=a_ref, bi=bi):
                acc_ref[...] += jnp.dot(a_ref[...], b_refs[bi][...], preferred_element_type=F32)

        @pl.when(k == nk - 1)
        def _():
            o_ref[...] = acc_ref[...]

    any_spec = pl.BlockSpec(memory_space=pl.ANY)
    o_spec = pl.BlockSpec((tm, tn), lambda j, i, k: (i, j))
    outs = pl.pallas_call(
        body, name=name, grid=grid,
        in_specs=[a_spec(s[0], s[1], s[2]) for s in steps] + [b_spec(bi) for bi in range(nb)] + [o_spec] * na + [any_spec] * ne,
        out_specs=[o_spec] + [any_spec] * ne,
        out_shape=[jax.ShapeDtypeStruct((m_, n_), F32)] + (op.out_shape if op else []),
        scratch_shapes=[pltpu.VMEM((tm, tn), F32)] + (_sem_pair(op.n_sem) if op else []),
        compiler_params=_cp(("arbitrary", "arbitrary", "arbitrary")),
    )(*[a for a, _, _ in segs], *bs, *(() if acc is None else (acc,)), *(exchange or ()))
    return outs if op else outs[0]


def _dims(mode, ndim):
    if ndim == 2:
        return _DIMS[mode]
    (ca,), (cb,) = _DIMS[mode][0]
    return (((ca + 1,), (cb + 1,)), ((0,), (0,)))


def _bdot_plain(a, b, mode):
    return lax.dot_general(a.astype(BF16), b.astype(BF16), _dims(mode, a.ndim), preferred_element_type=F32)


@functools.partial(jax.custom_vjp, nondiff_argnums=(2,))
def _bdot_vjp(a, b, mode):
    return _bdot_plain(a, b, mode)


def _bdot_fwd(a, b, mode):
    return _bdot_plain(a, b, mode), (a, b)


def _bdot_bwd(mode, res, g):
    a, b = res
    if mode == "nn":
        return _bdot_plain(g, b, "nt"), _bdot_plain(a, g, "tn")
    if mode == "nt":
        return _bdot_plain(g, b, "nn"), _bdot_plain(g, a, "tn")
    return _bdot_plain(b, g, "nt"), _bdot_plain(a, g, "nn")


_bdot_vjp.defvjp(_bdot_fwd, _bdot_bwd)


def _split3(x):
    x1 = x.astype(BF16)
    r1 = x - x1.astype(F32)
    x2 = r1.astype(BF16)
    return x1, x2, (r1 - x2.astype(F32)).astype(BF16)


def _hdot_impl(a, b, mode, data):
    dims = _dims(mode, a.ndim)
    if data == "a":
        sel = b.astype(BF16)
        parts = [lax.dot_general(p, sel, dims, preferred_element_type=F32) for p in _split3(a)]
    else:
        sel = a.astype(BF16)
        parts = [lax.dot_general(sel, p, dims, preferred_element_type=F32) for p in _split3(b)]
    return (parts[2] + parts[1]) + parts[0]


@functools.partial(jax.custom_vjp, nondiff_argnums=(2, 3))
def _hdot(a, b, mode="nn", data="b"):
    return _hdot_impl(a, b, mode, data)


def _hdot_fwd(a, b, mode, data):
    return _hdot_impl(a, b, mode, data), (a, b)


def _hdot_bwd(mode, data, res, g):
    a, b = res
    if data == "a":
        da = {"nn": lambda: _hdot_impl(g, b, "nt", "a"), "nt": lambda: _hdot_impl(g, b, "nn", "a"),
              "tn": lambda: _hdot_impl(b, g, "nt", "b")}[mode]()
        return da, jnp.zeros_like(b)
    db = {"nn": lambda: _hdot_impl(a, g, "tn", "b"), "nt": lambda: _hdot_impl(g, a, "tn", "a"),
          "tn": lambda: _hdot_impl(a, g, "nn", "b")}[mode]()
    return jnp.zeros_like(a), db


_hdot.defvjp(_hdot_fwd, _hdot_bwd)


def _sigmoid(x):
    return 1.0 / (1.0 + jnp.exp(-x))


def _silu(x):
    return x * _sigmoid(x)


def _iota(shape, dim):
    return lax.broadcasted_iota(jnp.int32, shape, dim)


def _rms(x, w):
    return x * lax.rsqrt(jnp.mean(x * x, axis=-1, keepdims=True) + EPS) * w


def _hg_chunk(q_raw, f_raw, v, g, st, t0, t1, nw, dot):
    nhd, c = q_raw.shape[0], q_raw.shape[1]
    m = jnp.maximum(t0, t1)
    e0, e1 = jnp.exp(t0 - m), jnp.exp(t1 - m)
    lb = e0 / (e0 + e1)
    f = lb + (1.0 - lb) * _sigmoid(f_raw)
    k = 1.0 - f
    lf = jnp.log(f)
    qh = _silu(q_raw) * (HG_DK ** -0.5)
    row, col = _iota((c, c), 0), _iota((c, c), 1)
    causal = col <= row
    tril = jnp.broadcast_to(jnp.where(causal, 1.0, 0.0).astype(F32), (nhd, c, c))
    trilb = jnp.broadcast_to(jnp.where(causal & (col // HG_SUB == row // HG_SUB), 1.0, 0.0).astype(F32), (nhd, c, c))
    b = _hdot(tril, lf)
    bl = _hdot(trilb, lf)
    a_row = b - bl
    rid = _iota((c, HG_DK), 0)
    qt = qh * jnp.exp(bl)
    kt = k * jnp.exp(-bl)
    scores = jnp.zeros((nhd, c, c), F32)
    for j in range(c // HG_SUB):
        if j == 0:
            qj = qt * jnp.exp(jnp.minimum(a_row, 0.0))
        else:
            a_j = jnp.sum(jnp.where(rid == j * HG_SUB - 1, b, 0.0), axis=1, keepdims=True)
            qj = qt * jnp.exp(jnp.minimum(a_row - a_j, 0.0))
        kj = jnp.where(rid // HG_SUB == j, kt, 0.0)
        scores = scores + dot(qj, kj, "nt")
    scores = jnp.where(causal, scores, 0.0)
    o = dot(scores, v, "nn") + dot(qh * jnp.exp(b), st, "nt")
    b_last = jnp.sum(jnp.where(rid == c - 1, b, 0.0), axis=1, keepdims=True)
    st_new = st * jnp.exp(b_last) + dot(v, k * jnp.exp(b_last - b), "tn")
    y = _rms(o, nw) * _silu(g)
    return y, st_new


HG_HPS = 16
HG_W = HG_HPS * HG_DK


def hgrn2_fwd(qfig, table, nw, *, step_chunks=2, gather=None):
    t = qfig.shape[0]
    rows = HG_CHUNK * step_chunks
    nsteps = t // rows
    nh = HG_HEADS // HG_HPS
    op = _Gather(*gather) if gather else None
    ng = len(gather[0]) if gather else 0

    def body(*refs):
        q_ref, f_ref, v_ref, g_ref, tab_ref, nw_ref = refs[:6]
        p_refs = refs[6:6 + ng]
        y_ref, s_ref = refs[6 + ng:8 + ng]
        got_refs = refs[8 + ng:8 + 2 * ng]
        st_scr = refs[8 + 2 * ng]
        sems = refs[9 + 2 * ng:]
        first_step = (pl.program_id(0) == 0) & (pl.program_id(1) == 0)
        last_step = (pl.program_id(0) == nh - 1) & (pl.program_id(1) == nsteps - 1)
        if op:
            @pl.when(first_step)
            def _():
                op.start(p_refs, got_refs, *sems)

        @pl.when(pl.program_id(1) == 0)
        def _():
            st_scr[...] = jnp.zeros_like(st_scr)

        nwv = nw_ref[...]
        lanes = [pl.ds(hh * HG_DK, HG_DK) for hh in range(HG_HPS)]
        t0 = jnp.stack([tab_ref[0:1, ln] for ln in lanes])
        t1 = jnp.stack([tab_ref[1:2, ln] for ln in lanes])
        for c in range(step_chunks):
            sl = pl.ds(c * HG_CHUNK, HG_CHUNK)
            heads = lambda ref: jnp.stack([ref[sl, ln] for ln in lanes])
            st = st_scr[...]
            for hh in range(HG_HPS):
                s_ref[hh, c] = st[hh]
            y, st_new = _hg_chunk(heads(q_ref), heads(f_ref), heads(v_ref), heads(g_ref), st, t0, t1, nwv, _bdot_vjp)
            for hh, ln in enumerate(lanes):
                y_ref[sl, ln] = y[hh].astype(BF16)
            st_scr[...] = st_new

        if op:
            @pl.when(last_step)
            def _():
                op.finish(p_refs, got_refs, *sems)

    blk = lambda off: pl.BlockSpec((rows, HG_W), lambda h, c, off=off: (c, off + h))
    return pl.pallas_call(
        body, name="hgrn2_fwd", grid=(nh, nsteps),
        in_specs=[blk(0), blk(nh), blk(2 * nh), blk(3 * nh),
                  pl.BlockSpec((2, HG_W), lambda h, c: (0, h)), pl.BlockSpec((1, HG_DK), lambda h, c: (0, 0))] + [ANY] * ng,
        out_specs=[pl.BlockSpec((rows, HG_W), lambda h, c: (c, h)),
                   pl.BlockSpec((HG_HPS, step_chunks, HG_DK, HG_DK), lambda h, c: (h, c, 0, 0))] + [ANY] * ng,
        out_shape=[jax.ShapeDtypeStruct((t, HG_HEADS * HG_DK), BF16),
                   jax.ShapeDtypeStruct((HG_HEADS, t // HG_CHUNK, HG_DK, HG_DK), F32)] + (op.out_shape if op else []),
        scratch_shapes=[pltpu.VMEM((HG_HPS, HG_DK, HG_DK), F32)] + (_sem_pair(op.n_sem) if op else []),
        compiler_params=_cp(("arbitrary", "arbitrary")),
    )(qfig, qfig, qfig, qfig, table, nw, *(gather[0] if gather else ()))


def hgrn2_bwd(qfig, table, nw, states, dy, *, step_chunks=2):
    t = qfig.shape[0]
    rows = HG_CHUNK * step_chunks
    nsteps = t // rows
    nh = HG_HEADS // HG_HPS

    def body(q_ref, f_ref, v_ref, g_ref, tab_ref, nw_ref, s_ref, dy_ref,
             dq_ref, df_ref, dv_ref, dg_ref, dtab_ref, dnw_ref, dst_scr):
        @pl.when(pl.program_id(1) == 0)
        def _():
            dst_scr[...] = jnp.zeros_like(dst_scr)
            dtab_ref[...] = jnp.zeros_like(dtab_ref)
            dnw_ref[...] = jnp.zeros_like(dnw_ref)

        nwv = nw_ref[...]
        fn = functools.partial(_hg_chunk, dot=_bdot_vjp)
        lanes = [pl.ds(hh * HG_DK, HG_DK) for hh in range(HG_HPS)]
        t0 = jnp.stack([tab_ref[0:1, ln] for ln in lanes])
        t1 = jnp.stack([tab_ref[1:2, ln] for ln in lanes])
        for c in reversed(range(step_chunks)):
            sl = pl.ds(c * HG_CHUNK, HG_CHUNK)
            heads = lambda ref: jnp.stack([ref[sl, ln] for ln in lanes])
            _, vjp = jax.vjp(fn, heads(q_ref), heads(f_ref), heads(v_ref), heads(g_ref), s_ref[:, c], t0, t1, nwv)
            dq, df, dv, dg, dst, dt0, dt1, dnw = vjp((heads(dy_ref).astype(F32), dst_scr[...]))
            for hh, ln in enumerate(lanes):
                dq_ref[sl, ln] = dq[hh].astype(BF16)
                df_ref[sl, ln] = df[hh].astype(BF16)
                dv_ref[sl, ln] = dv[hh].astype(BF16)
                dg_ref[sl, ln] = dg[hh].astype(BF16)
                dtab_ref[0:1, ln] += dt0[hh]
                dtab_ref[1:2, ln] += dt1[hh]
            dst_scr[...] = dst
            dnw_ref[0] += dnw

    rev = lambda c: nsteps - 1 - c
    blk = lambda off: pl.BlockSpec((rows, HG_W), lambda h, c, off=off: (rev(c), off + h))
    oblk = lambda: pl.BlockSpec((rows, HG_W), lambda h, c: (rev(c), h))
    d = HG_HEADS * HG_DK
    outs = pl.pallas_call(
        body, name="hgrn2_bwd", grid=(nh, nsteps),
        in_specs=[blk(0), blk(nh), blk(2 * nh), blk(3 * nh),
                  pl.BlockSpec((2, HG_W), lambda h, c: (0, h)), pl.BlockSpec((1, HG_DK), lambda h, c: (0, 0)),
                  pl.BlockSpec((HG_HPS, step_chunks, HG_DK, HG_DK), lambda h, c: (h, rev(c), 0, 0)),
                  pl.BlockSpec((rows, HG_W), lambda h, c: (rev(c), h))],
        out_specs=[oblk(), oblk(), oblk(), oblk(),
                   pl.BlockSpec((2, HG_W), lambda h, c: (0, h)),
                   pl.BlockSpec((HG_HPS, 1, HG_DK), lambda h, c: (h, 0, 0))],
        out_shape=[jax.ShapeDtypeStruct((t, d), BF16)] * 4
        + [jax.ShapeDtypeStruct((2, d), F32), jax.ShapeDtypeStruct((HG_HEADS, 1, HG_DK), F32)],
        scratch_shapes=[pltpu.VMEM((HG_HPS, HG_DK, HG_DK), F32)],
        compiler_params=_cp(("parallel", "arbitrary")),
    )(qfig, qfig, qfig, qfig, table, nw, states, dy)
    return outs


def _ssd_chunk(xs2, dt, acum, bm, cm, s2, pair0, dot):
    npr, c = xs2.shape[0], xs2.shape[1]
    sh_e, sh_s = (npr, DT_PAD, 128), (npr, 8, DT_PAD)
    first_head = 2 * (pair0 + _iota(sh_e, 0))
    expand = jnp.where(_iota(sh_e, 1) == first_head + _iota(sh_e, 2) // SSM_HEADDIM, 1.0, 0.0).astype(F32)
    sel = (_iota(sh_s, 2) == 2 * (pair0 + _iota(sh_s, 0)) + _iota(sh_s, 1)) & (_iota(sh_s, 1) < 2)
    sel = jnp.where(sel, 1.0, 0.0).astype(F32)
    per_pair = lambda a: jnp.broadcast_to(a, (npr,) + a.shape)
    dtx = _hdot(per_pair(dt), expand, "nn", "a")
    acol = _hdot(per_pair(acum), expand, "nn", "a")
    arow8 = _hdot(sel, per_pair(acum), "nt", "b")
    row, col = _iota((c, c), 0), _iota((c, c), 1)
    causal = col <= row
    cb = dot(cm, bm, "nt")
    x2 = xs2 * dtx
    lane_c = _iota((c, 128), 1)
    y = dot(per_pair(cm), s2, "nn") * jnp.exp(acol)
    for r in range(2):
        head = (lane_c // SSM_HEADDIM) == r
        a_c = jnp.sum(jnp.where(head & (lane_c % SSM_HEADDIM == 0), acol, 0.0), axis=2, keepdims=True)
        a_r = jnp.sum(jnp.where(_iota((8, c), 0) == r, arow8, 0.0), axis=1, keepdims=True)
        decay = jnp.where(causal, jnp.exp(jnp.minimum(a_c - a_r, 0.0)), 0.0)
        y = y + dot(cb * decay, jnp.where(head, x2, 0.0), "nn")
    a_last = jnp.sum(jnp.where(_iota((c, 128), 0) == c - 1, acol, 0.0), axis=1, keepdims=True)
    s2_new = s2 * jnp.exp(a_last) + dot(per_pair(bm), x2 * jnp.exp(a_last - acol), "tn")
    return y, s2_new


SSM_PAIRS = SSM_HEADS // 2
PAIRS_PER_GROUP = SSM_PAIRS // SSM_GROUPS
SSD_PPS = 4
SSD_W = SSD_PPS * 128
_XS_BLOCKS = SSM_DINNER // 128
_B_BLOCK0 = _XS_BLOCKS
_C_BLOCK0 = _XS_BLOCKS + SSM_GROUPS


def ssd_fwd(xbc_act, dt, acum, *, gather=None):
    t = xbc_act.shape[0]
    nc = t // SSM_CHUNK
    c_ = SSM_CHUNK
    nq = SSM_PAIRS // SSD_PPS
    op = _Gather(*gather) if gather else None
    ng = len(gather[0]) if gather else 0

    def body(*refs):
        xs_ref, b_ref, c_ref, dt_ref, ac_ref = refs[:5]
        p_refs = refs[5:5 + ng]
        y_ref, s_ref = refs[5 + ng:7 + ng]
        got_refs = refs[7 + ng:7 + 2 * ng]
        s_scr = refs[7 + 2 * ng]
        sems = refs[8 + 2 * ng:]
        q = pl.program_id(1)
        if op:
            @pl.when((pl.program_id(0) == 0) & (q == 0))
            def _():
                op.start(p_refs, got_refs, *sems)

            @pl.when((pl.program_id(0) == nc - 1) & (q == nq - 1))
            def _():
                op.finish(p_refs, got_refs, *sems)

        mine = pl.ds(SSD_PPS * q, SSD_PPS)
        lanes = [pl.ds(r * 128, 128) for r in range(SSD_PPS)]

        @pl.when(pl.program_id(0) == 0)
        def _():
            s_scr[mine] = jnp.zeros((SSD_PPS, SSM_DSTATE, 128), F32)

        s2 = s_scr[mine]
        s_ref[...] = s2
        xs = jnp.stack([xs_ref[:, ln] for ln in lanes])
        y, s2_new = _ssd_chunk(xs, dt_ref[...], ac_ref[...], b_ref[...], c_ref[...], s2, SSD_PPS * q, _bdot_vjp)
        for r, ln in enumerate(lanes):
            y_ref[:, ln] = y[r]
        s_scr[mine] = s2_new

    grp = lambda q: q // (PAIRS_PER_GROUP // SSD_PPS)
    return pl.pallas_call(
        body, name="ssd_fwd", grid=(nc, SSM_PAIRS // SSD_PPS),
        in_specs=[pl.BlockSpec((c_, SSD_W), lambda c, q: (c, q)),
                  pl.BlockSpec((c_, 128), lambda c, q: (c, _B_BLOCK0 + grp(q))),
                  pl.BlockSpec((c_, 128), lambda c, q: (c, _C_BLOCK0 + grp(q))),
                  pl.BlockSpec((c_, DT_PAD), lambda c, q: (c, 0)),
                  pl.BlockSpec((c_, DT_PAD), lambda c, q: (c, 0))] + [ANY] * ng,
        out_specs=[pl.BlockSpec((c_, SSD_W), lambda c, q: (c, q)),
                   pl.BlockSpec((None, SSD_PPS, SSM_DSTATE, 128), lambda c, q: (c, q, 0, 0))] + [ANY] * ng,
        out_shape=[jax.ShapeDtypeStruct((t, SSM_DINNER), F32),
                   jax.ShapeDtypeStruct((nc, SSM_PAIRS, SSM_DSTATE, 128), F32)] + (op.out_shape if op else []),
        scratch_shapes=[pltpu.VMEM((SSM_PAIRS, SSM_DSTATE, 128), F32)] + (_sem_pair(op.n_sem) if op else []),
        compiler_params=_cp(("arbitrary", "arbitrary")),
    )(xbc_act, xbc_act, xbc_act, dt, acum, *(gather[0] if gather else ()))


def ssd_bwd(xbc_act, dt, acum, states, dy, dskip, *, exchange=None):
    t = xbc_act.shape[0]
    nc = t // SSM_CHUNK
    c_ = SSM_CHUNK
    rev = lambda c: nc - 1 - c
    nq = SSM_PAIRS // SSD_PPS
    op = _ChipExchange(exchange) if exchange else None
    ne = len(exchange) if exchange else 0

    def body(*refs):
        xs_ref, b_ref, c_ref, dt_ref, ac_ref, s_ref, dy_ref, sk_ref = refs[:8]
        ex_refs = refs[8:8 + ne]
        dxs_ref, db_ref, dc_ref, ddt_ref, dac_ref = refs[8 + ne:13 + ne]
        got_refs = refs[13 + ne:13 + 2 * ne]
        ds_scr = refs[13 + 2 * ne]
        sems = refs[14 + 2 * ne:]
        q = pl.program_id(1)
        if op:
            @pl.when((pl.program_id(0) == 0) & (q == 0))
            def _():
                op.start(ex_refs, got_refs, *sems)

            @pl.when((pl.program_id(0) == nc - 1) & (q == nq - 1))
            def _():
                op.finish(ex_refs, got_refs, *sems)

        assert SSD_PPS == PAIRS_PER_GROUP
        mine = pl.ds(SSD_PPS * q, SSD_PPS)
        lanes = [pl.ds(r * 128, 128) for r in range(SSD_PPS)]

        @pl.when(pl.program_id(0) == 0)
        def _():
            ds_scr[mine] = jnp.zeros((SSD_PPS, SSM_DSTATE, 128), F32)

        fn = functools.partial(_ssd_chunk, pair0=SSD_PPS * q, dot=_bdot_vjp)
        xs = jnp.stack([xs_ref[:, ln] for ln in lanes])
        dy = jnp.stack([dy_ref[:, ln] for ln in lanes])
        _, vjp = jax.vjp(fn, xs, dt_ref[...], ac_ref[...], b_ref[...], c_ref[...], s_ref[...])
        dxs, ddt, dac, db, dc, ds = vjp((dy, ds_scr[mine]))
        for r, ln in enumerate(lanes):
            dxs_ref[:, ln] = dxs[r] + sk_ref[:, ln]
        ds_scr[mine] = ds
        db_ref[...] = db
        dc_ref[...] = dc

        @pl.when(q == 0)
        def _():
            ddt_ref[...] = ddt
            dac_ref[...] = dac

        @pl.when(q != 0)
        def _():
            ddt_ref[...] += ddt
            dac_ref[...] += dac

    grp = lambda q: q // (PAIRS_PER_GROUP // SSD_PPS)
    return pl.pallas_call(
        body, name="ssd_bwd", grid=(nc, SSM_PAIRS // SSD_PPS),
        in_specs=[pl.BlockSpec((c_, SSD_W), lambda c, q: (rev(c), q)),
                  pl.BlockSpec((c_, 128), lambda c, q: (rev(c), _B_BLOCK0 + grp(q))),
                  pl.BlockSpec((c_, 128), lambda c, q: (rev(c), _C_BLOCK0 + grp(q))),
                  pl.BlockSpec((c_, DT_PAD), lambda c, q: (rev(c), 0)),
                  pl.BlockSpec((c_, DT_PAD), lambda c, q: (rev(c), 0)),
                  pl.BlockSpec((None, SSD_PPS, SSM_DSTATE, 128), lambda c, q: (rev(c), q, 0, 0)),
                  pl.BlockSpec((c_, SSD_W), lambda c, q: (rev(c), q)),
                  pl.BlockSpec((c_, SSD_W), lambda c, q: (rev(c), q))] + [ANY] * ne,
        out_specs=[pl.BlockSpec((c_, SSD_W), lambda c, q: (rev(c), q)),
                   pl.BlockSpec((c_, 128), lambda c, q: (rev(c), grp(q))),
                   pl.BlockSpec((c_, 128), lambda c, q: (rev(c), grp(q))),
                   pl.BlockSpec((c_, DT_PAD), lambda c, q: (rev(c), 0)),
                   pl.BlockSpec((c_, DT_PAD), lambda c, q: (rev(c), 0))] + [ANY] * ne,
        out_shape=[jax.ShapeDtypeStruct((t, SSM_DINNER), F32),
                   jax.ShapeDtypeStruct((t, SSM_GROUPS * SSM_DSTATE), F32),
                   jax.ShapeDtypeStruct((t, SSM_GROUPS * SSM_DSTATE), F32),
                   jax.ShapeDtypeStruct((t, DT_PAD), F32),
                   jax.ShapeDtypeStruct((t, DT_PAD), F32)] + (op.out_shape if op else []),
        scratch_shapes=[pltpu.VMEM((SSM_PAIRS, SSM_DSTATE, 128), F32)] + (_sem_pair(op.n_sem) if op else []),
        compiler_params=_cp(("arbitrary", "arbitrary")),
    )(xbc_act, xbc_act, xbc_act, dt, acum, states, dy, dskip, *(exchange or ()))


def rowwise(name, fn, row_ins, par_ins, row_outs, acc_outs, *, tt, ncb=1):
    t = row_ins[0][0].shape[0]
    assert t % tt == 0
    n_ri, n_pi, n_ro, n_ao = len(row_ins), len(par_ins), len(row_outs), len(acc_outs)

    def body(*refs):
        i = pl.program_id(1)
        ins = [r[...] for r in refs[:n_ri + n_pi]]
        outs = fn(*ins)
        ro_refs = refs[n_ri + n_pi:n_ri + n_pi + n_ro]
        ao_refs = refs[n_ri + n_pi + n_ro:]
        for r, v in zip(ro_refs, outs[:n_ro]):
            r[...] = v.astype(r.dtype)
        for r, v in zip(ao_refs, outs[n_ro:]):
            @pl.when(i == 0)
            def _(r=r, v=v):
                r[...] = v

            @pl.when(i > 0)
            def _(r=r, v=v):
                r[...] += v

    in_specs = [pl.BlockSpec((tt, bc), lambda j, i, off=off: (i, off + j)) for _, bc, off in row_ins]
    in_specs += [pl.BlockSpec((a.shape[0], bc), lambda j, i, off=off: (0, off + j)) for a, bc, off in par_ins]
    out_specs = [pl.BlockSpec((tt, bc), lambda j, i: (i, j)) for _, bc, _ in row_outs]
    out_specs += [pl.BlockSpec((r, bc), lambda j, i: (0, j)) for r, _, bc in acc_outs]
    out_shape = [jax.ShapeDtypeStruct((t, c), dt) for c, _, dt in row_outs]
    out_shape += [jax.ShapeDtypeStruct((r, c), F32) for r, c, _ in acc_outs]
    return pl.pallas_call(
        body, name=name, grid=(ncb, t // tt), in_specs=in_specs, out_specs=out_specs, out_shape=out_shape,
        compiler_params=_cp(("parallel", "arbitrary")),
    )(*[a for a, _, _ in row_ins], *[a for a, _, _ in par_ins])


def _colsum(v):
    return jnp.sum(v, axis=0, keepdims=True)


def _softplus(x):
    return jnp.maximum(x, 0.0) + jnp.log(1.0 + jnp.exp(-jnp.abs(x)))


def _gelu_tanh(x):
    return 0.5 * x * (1.0 + jnp.tanh(0.7978845608028654 * (x + 0.044715 * (x * x * x))))


D = D_MODEL


def norm_fwd(x, w):
    return rowwise("norm_fwd", lambda xv, wv: (_rms(xv, wv),), [(x, D, 0)], [(w, D, 0)], [(D, D, BF16)], [], tt=256)[0]


def norm_bwd(x, w, dh, dres):
    def fn(xv, dhv, drv, wv):
        _, vjp = jax.vjp(_rms, xv, wv)
        dx, dw = vjp(dhv)
        return dx + drv, dw
    return rowwise("norm_bwd", fn, [(x, D, 0), (dh, D, 0), (dres, D, 0)], [(w, D, 0)], [(D, D, F32)], [(1, D, D)], tt=256)


def _dt_fn(dtr, bias, a_log):
    c = dtr.shape[0]
    dt = _softplus(dtr + bias)
    da = dt * (-jnp.exp(a_log))
    tril = jnp.where(_iota((c, c), 1) <= _iota((c, c), 0), 1.0, 0.0).astype(F32)
    return dt, _hdot(tril, da)


def dt_fwd(dtr, bias, a_log):
    return rowwise("dt_fwd", _dt_fn, [(dtr, DT_PAD, 0)], [(bias, DT_PAD, 0), (a_log, DT_PAD, 0)],
                   [(DT_PAD, DT_PAD, F32), (DT_PAD, DT_PAD, F32)], [], tt=SSM_CHUNK)


def dt_bwd(dtr, bias, a_log, ddt, dacum):
    def fn(dtrv, ddtv, dacv, bv, av):
        _, vjp = jax.vjp(_dt_fn, dtrv, bv, av)
        return vjp((ddtv, dacv))
    return rowwise("dt_bwd", fn, [(dtr, DT_PAD, 0), (ddt, DT_PAD, 0), (dacum, DT_PAD, 0)],
                   [(bias, DT_PAD, 0), (a_log, DT_PAD, 0)],
                   [(DT_PAD, DT_PAD, BF16)], [(1, DT_PAD, DT_PAD), (1, DT_PAD, DT_PAD)], tt=SSM_CHUNK)


GROUP_W = SSM_DINNER // SSM_GROUPS


def _ssm_post_fn(yv, xsv, zv, dexp, nw):
    return _rms((yv + dexp * xsv) * _silu(zv), nw)


def ssm_post_fwd(yssd, xbc_act, z, dexp, nw):
    return rowwise("ssm_post_fwd", lambda *a: (_ssm_post_fn(*a),),
                   [(yssd, GROUP_W, 0), (xbc_act, GROUP_W, 0), (z, GROUP_W, 0)], [(dexp, GROUP_W, 0), (nw, GROUP_W, 0)],
                   [(SSM_DINNER, GROUP_W, BF16)], [], tt=512, ncb=SSM_GROUPS)[0]


def ssm_post_bwd(yssd, xbc_act, z, dexp, nw, dy):
    def fn(yv, xsv, zv, dyv, dv, nv):
        _, vjp = jax.vjp(_ssm_post_fn, yv, xsv, zv, dv, nv)
        return vjp(dyv.astype(F32))
    return rowwise("ssm_post_bwd", fn,
                   [(yssd, GROUP_W, 0), (xbc_act, GROUP_W, 0), (z, GROUP_W, 0), (dy, GROUP_W, 0)],
                   [(dexp, GROUP_W, 0), (nw, GROUP_W, 0)],
                   [(SSM_DINNER, GROUP_W, F32), (SSM_DINNER, GROUP_W, F32), (SSM_DINNER, GROUP_W, BF16)],
                   [(1, SSM_DINNER, GROUP_W), (1, SSM_DINNER, GROUP_W)], tt=512, ncb=SSM_GROUPS)


def _merge_fn(ah, asm, gh, gs):
    return _sigmoid(gh) * ah + _sigmoid(gs) * asm


def merge_fwd(a_hg, a_ssm, gates):
    f32 = lambda vals: [v.astype(F32) for v in vals]
    return rowwise("merge_fwd", lambda *a: (_merge_fn(*f32(a)),), [(a_hg, D, 0), (a_ssm, D, 0), (gates, D, 0), (gates, D, 1)], [],
                   [(D, D, BF16)], [], tt=256)[0]


def merge_bwd(a_hg, a_ssm, gates, dmixed):
    def fn(ah, asm, gh, gs, dm):
        _, vjp = jax.vjp(_merge_fn, *[v.astype(F32) for v in (ah, asm, gh, gs)])
        return vjp(dm.astype(F32))
    return rowwise("merge_bwd", fn, [(a_hg, D, 0), (a_ssm, D, 0), (gates, D, 0), (gates, D, 1), (dmixed, D, 0)], [],
                   [(D, D, BF16)] * 4, [], tt=256)


def _post1_fn(xv, uv, wpost, wpre):
    x1 = xv + _rms(uv, wpost)
    return x1, _rms(x1, wpre)


def post1_fwd(x, u, wpost, wpre):
    return rowwise("post1_fwd", _post1_fn, [(x, D, 0), (u, D, 0)], [(wpost, D, 0), (wpre, D, 0)],
                   [(D, D, F32), (D, D, BF16)], [], tt=256)


def post1_bwd(x, u, wpost, wpre, dx1, dh2):
    def fn(xv, uv, d1, d2, wa, wb):
        _, vjp = jax.vjp(_post1_fn, xv, uv, wa, wb)
        dx, du, dwa, dwb = vjp((d1, d2))
        return du, dx, dwa, dwb
    return rowwise("post1_bwd", fn, [(x, D, 0), (u, D, 0), (dx1, D, 0), (dh2, D, 0)], [(wpost, D, 0), (wpre, D, 0)],
                   [(D, D, BF16), (D, D, F32)], [(1, D, D), (1, D, D)], tt=256)


def final_fwd_bwd(x1, fo, w, target):
    def fn(x1v, fov, tv, wv):
        def loss_fn(a, b, c):
            err = a + _rms(b, c) - tv
            return 0.5 * jnp.sum(err * err) * (1.0 / D)
        loss, vjp = jax.vjp(loss_fn, x1v, fov, wv)
        dx, dfo, dw = vjp(jnp.ones((), F32))
        return dx, dfo, dw, jnp.full((1, 128), loss, F32)
    return rowwise("final_fwd_bwd", fn, [(x1, D, 0), (fo, D, 0), (target, D, 0)], [(w, D, 0)],
                   [(D, D, F32), (D, D, BF16)], [(1, D, D), (1, 128, 128)], tt=256)


HALO = 8
CONV_TT = 512
CONV_CB = 512
CONV_RB = 32


def _tail(kind, c, up):
    return _silu(c) if kind == "silu" else _gelu_tanh(c) * up


def conv_fwd(name, x, xoff, w, b, kind, up=None, upoff=0, act_dtype=F32):
    t = x.shape[0]
    k_, c_ = w.shape
    tt, cb = CONV_TT, CONV_CB
    hb = tt // HALO
    has_up = up is not None

    def body(*refs):
        if has_up:
            x_ref, xp_ref, w_ref, b_ref, up_ref, c_ref, a_ref, scr = refs
        else:
            x_ref, xp_ref, w_ref, b_ref, c_ref, a_ref, scr = refs
        i = pl.program_id(1)
        scr[0:HALO, :] = jnp.where(i == 0, 0.0, xp_ref[...])
        scr[HALO:HALO + tt, :] = x_ref[...]
        for r in range(tt // CONV_RB):
            rows = pl.ds(r * CONV_RB, CONV_RB)
            acc = jnp.zeros((CONV_RB, cb), F32) + b_ref[...]
            for k in range(k_):
                acc = acc + w_ref[k:k + 1, :] * scr[pl.ds(r * CONV_RB + HALO - (k_ - 1) + k, CONV_RB), :]
            c_ref[rows, :] = acc
            a_ref[rows, :] = _tail(kind, acc, up_ref[rows, :] if has_up else None).astype(act_dtype)

    in_specs = [pl.BlockSpec((tt, cb), lambda j, i: (i, xoff + j)),
                pl.BlockSpec((HALO, cb), lambda j, i: (jnp.maximum(i * hb - 1, 0), xoff + j)),
                pl.BlockSpec((k_, cb), lambda j, i: (0, j)),
                pl.BlockSpec((1, cb), lambda j, i: (0, j))]
    args = [x, x, w, b]
    if has_up:
        in_specs.append(pl.BlockSpec((tt, cb), lambda j, i: (i, upoff + j)))
        args.append(up)
    return pl.pallas_call(
        body, name=name, grid=(c_ // cb, t // tt), in_specs=in_specs,
        out_specs=[pl.BlockSpec((tt, cb), lambda j, i: (i, j))] * 2,
        out_shape=[jax.ShapeDtypeStruct((t, c_), F32), jax.ShapeDtypeStruct((t, c_), act_dtype)],
        scratch_shapes=[pltpu.VMEM((tt + HALO, cb), F32)],
        compiler_params=_cp(("parallel", "arbitrary")),
    )(*args)


def conv_bwd(name, x, xoff, c, coff, dact, w, kind, up=None, upoff=0):
    t = x.shape[0]
    k_, c_ = w.shape[0], dact.shape[1]
    tt, cb = CONV_TT, CONV_CB
    hb = tt // HALO
    nt = t // tt
    has_up = up is not None

    def tail_grad(cv, dav, upv):
        if has_up:
            _, vjp = jax.vjp(lambda a, u: _tail(kind, a, u), cv, upv)
            return vjp(dav)
        _, vjp = jax.vjp(lambda a: _tail(kind, a, None), cv)
        return vjp(dav)[0], None

    def body(*refs):
        if has_up:
            (x_ref, xp_ref, c_ref, cn_ref, da_ref, dan_ref, w_ref, up_ref, upn_ref,
             dx_ref, dup_ref, dw_ref, db_ref, xs, dcs) = refs
        else:
            x_ref, xp_ref, c_ref, cn_ref, da_ref, dan_ref, w_ref, dx_ref, dw_ref, db_ref, xs, dcs = refs
        i = pl.program_id(1)
        xs[0:HALO, :] = jnp.where(i == 0, 0.0, xp_ref[...])
        xs[HALO:HALO + tt, :] = x_ref[...]
        rb = CONV_RB
        for r in range(tt // rb):
            rows = pl.ds(r * rb, rb)
            dc, dup = tail_grad(c_ref[rows, :], da_ref[rows, :].astype(F32), up_ref[rows, :] if has_up else None)
            dcs[rows, :] = dc
            if has_up:
                dup_ref[rows, :] = dup.astype(BF16)
        dcn, _ = tail_grad(cn_ref[...], dan_ref[...].astype(F32), upn_ref[...] if has_up else None)
        dcs[tt:tt + HALO, :] = jnp.where(i == nt - 1, 0.0, dcn)
        dws = [jnp.zeros((1, cb), F32) for _ in range(k_)]
        dbv = jnp.zeros((1, cb), F32)
        for r in range(tt // rb):
            rows = pl.ds(r * rb, rb)
            dc = dcs[rows, :]
            dx = jnp.zeros((rb, cb), F32)
            for k in range(k_):
                dx = dx + w_ref[k:k + 1, :] * dcs[pl.ds(r * rb + k_ - 1 - k, rb), :]
                dws[k] = dws[k] + _colsum(dc * xs[pl.ds(r * rb + HALO - (k_ - 1) + k, rb), :])
            dbv = dbv + _colsum(dc)
            dx_ref[rows, :] = dx.astype(BF16)

        @pl.when(i == 0)
        def _():
            dw_ref[...] = jnp.zeros_like(dw_ref)
            db_ref[...] = jnp.zeros_like(db_ref)

        for k in range(k_):
            dw_ref[k:k + 1, :] += dws[k]
        db_ref[...] += dbv

    tile = lambda off: pl.BlockSpec((tt, cb), lambda j, i, off=off: (i, off + j))
    prev = lambda off: pl.BlockSpec((HALO, cb), lambda j, i, off=off: (jnp.maximum(i * hb - 1, 0), off + j))
    nxt = lambda off: pl.BlockSpec((HALO, cb), lambda j, i, off=off: (jnp.minimum((i + 1) * hb, t // HALO - 1), off + j))
    in_specs = [tile(xoff), prev(xoff), tile(coff), nxt(coff), tile(0), nxt(0),
                pl.BlockSpec((k_, cb), lambda j, i: (0, coff + j))]
    args = [x, x, c, c, dact, dact, w]
    if has_up:
        in_specs += [tile(upoff), nxt(upoff)]
        args += [up, up]
    out_specs = [tile(0)] + ([tile(0)] if has_up else []) + [pl.BlockSpec((HALO, cb), lambda j, i: (0, j)),
                                                            pl.BlockSpec((1, cb), lambda j, i: (0, j))]
    out_shape = [jax.ShapeDtypeStruct((t, c_), BF16)] * (2 if has_up else 1)
    out_shape += [jax.ShapeDtypeStruct((HALO, c_), F32), jax.ShapeDtypeStruct((1, c_), F32)]
    return pl.pallas_call(
        body, name=name, grid=(c_ // cb, nt), in_specs=in_specs, out_specs=out_specs, out_shape=out_shape,
        scratch_shapes=[pltpu.VMEM((tt + HALO, cb), F32), pltpu.VMEM((tt + HALO, cb), F32)],
        compiler_params=_cp(("parallel", "arbitrary")),
    )(*args)


def ew_sum(name, parts, rows, out_dtype, tr):
    c = parts[0][0].shape[1]
    tr = min(tr, rows)
    assert rows % tr == 0 and all(off % tr == 0 for _, off in parts)
    n = len(parts)

    def body(*refs):
        acc = refs[0][...].astype(F32)
        for ref in refs[1:n]:
            acc = acc + ref[...].astype(F32)
        refs[n][...] = acc.astype(out_dtype)

    in_specs = [pl.BlockSpec((tr, c), lambda i, o=off // tr: (i + o, 0)) for _, off in parts]
    return pl.pallas_call(body, name=name, grid=(rows // tr,), in_specs=in_specs,
                          out_specs=pl.BlockSpec((tr, c), lambda i: (i, 0)),
                          out_shape=jax.ShapeDtypeStruct((rows, c), out_dtype),
                          compiler_params=_cp(("parallel",)))(*[a for a, _ in parts])


def fold_heads(dexp):
    def body(d_ref, o_ref):
        sel = jnp.where(_iota((SSM_DINNER, DT_PAD), 0) // SSM_HEADDIM == _iota((SSM_DINNER, DT_PAD), 1), 1.0, 0.0)
        o_ref[...] = _hdot(jnp.broadcast_to(d_ref[...], (8, SSM_DINNER)), sel.astype(F32), "nn", "a")[0:1, :]

    return pl.pallas_call(body, name="fold_heads", out_shape=jax.ShapeDtypeStruct((1, DT_PAD), F32),
                          compiler_params=pltpu.CompilerParams(vmem_limit_bytes=VMEM_LIMIT))(dexp)


def adamw(name, w, g, m, v, tr):
    r, c = w.shape
    tr = min(tr, r)
    assert r % tr == 0, (r, tr)

    def body(w_ref, g_ref, m_ref, v_ref, d_ref, nm_ref, nv_ref):
        gv = g_ref[...]
        nm = ADAM_B1 * m_ref[...] + (1.0 - ADAM_B1) * gv
        nv = ADAM_B2 * v_ref[...] + (1.0 - ADAM_B2) * (gv * gv)
        m_hat = nm / (1.0 - ADAM_B1 ** ADAM_STEP)
        v_hat = nv / (1.0 - ADAM_B2 ** ADAM_STEP)
        d_ref[...] = -ADAM_LR * (m_hat / (jnp.sqrt(v_hat) + ADAM_EPS) + ADAM_WD * w_ref[...])
        nm_ref[...] = nm
        nv_ref[...] = nv

    spec = pl.BlockSpec((tr, c), lambda i: (i, 0))
    shp = jax.ShapeDtypeStruct((r, c), F32)
    return pl.pallas_call(body, name=name, grid=(r // tr,), in_specs=[spec] * 4, out_specs=[spec] * 3,
                          out_shape=[shp] * 3, compiler_params=_cp(("parallel",)))(w, g, m, v)


SEG_QFIG, SEG_Z, SEG_XBC, SEG_DT, SEG_G = 0, 8192, 12288, 18432, 18496
IN_TOTAL = 22592
FFN_BLOCKS = D_FF // CONV_CB


def _own_slot(gathered, own, shard):
    slot = lax.broadcasted_iota(jnp.int32, (gathered.shape[0],) + (1,) * own.ndim, 0)
    return jnp.where(slot == shard, own[None], gathered)


def local_step(x, target, wts, par, p_rest, p_up, shard, core):
    t = x.shape[0]
    pad64 = lambda a: jnp.pad(a, ((0, 0), (0, DT_PAD - a.shape[1])))
    bias, a_log = pad64(par["ssm_dt_bias"]), pad64(par["ssm_A_log"])
    dexp = jnp.repeat(par["ssm_D"], SSM_HEADDIM, axis=1)
    in_t = wts["in_t"]

    h = norm_fwd(x, par["mix_pre_norm"])
    proj = lambda nm, off, n, tn: mm(h, in_t, "nt", name=nm, tn=tn, dims=(t, n, D), b_off=(off, 0))
    qfig = proj("proj_qfig", SEG_QFIG, 8192, 1024)
    z = proj("proj_z", SEG_Z, 4096, 1024)
    xbc = proj("proj_xbc", SEG_XBC, 6144, 1024)
    dtr = mm(h, wts["dt_t"], "nt", name="proj_dt", tn=128)
    gates = mm(h, wts["g_t"], "nt", name="proj_gates", out_dtype=BF16, tn=1024)
    y_hg, hg_states, g_rest = hgrn2_fwd(qfig, par["hg_lb_table"], par["hg_out_norm"], gather=([p_rest], [REST_PIECES]))
    c_ssm, xbc_act = conv_fwd("ssm_conv_fwd", xbc, 0, par["ssm_conv_w"], par["ssm_conv_b"], "silu")
    dt, acum = dt_fwd(dtr, bias, a_log)
    yssd, ssd_states, g_up = ssd_fwd(xbc_act, dt, acum, gather=([p_up], [UP_PIECES]))
    g_rest, g_up = _own_slot(g_rest, p_rest, shard), _own_slot(g_up, p_up, shard)
    r0, r1, r2, r3 = REST_SPLITS
    wts = dict(wts, bh=g_rest[:, :r0].reshape(-1, D), bs=g_rest[:, r0:r1].reshape(-1, D), o=g_rest[:, r1:r2].reshape(-1, D),
               dn=g_rest[:, r2:r3].reshape(-1, D), up=jnp.transpose(g_up, (1, 0, 2)).reshape(D, 2 * D_FF),
               up_t=jnp.transpose(g_up, (0, 2, 1)).reshape(2 * D_FF, D))
    y_ssm = ssm_post_fwd(yssd, xbc_act, z, dexp, par["ssm_out_norm"])
    a_hg = mm(y_hg, wts["bh"], "nn", name="branch_hg", out_dtype=BF16, tn=1024)
    a_ssm = mm(y_ssm, wts["bs"], "nn", name="branch_ssm", out_dtype=BF16, tn=1024)
    mixed = merge_fwd(a_hg, a_ssm, gates)
    u = mm(mixed, wts["o"], "nn", name="out_proj", tn=1024)
    x1, h2 = post1_fwd(x, u, par["mix_post_norm"], par["ffn_pre_norm"])
    gu = mm(h2, wts["up"], "nn", name="ffn_up", tn=1024)
    c_ffn, act = conv_fwd("ffn_conv_fwd", gu, 0, par["ffn_conv_w"], par["ffn_conv_b"], "gelu_mul",
                          up=gu, upoff=FFN_BLOCKS, act_dtype=BF16)
    fo = mm(act, wts["dn"], "nn", name="ffn_down", tm=512, tn=1024)
    dx2, dfo, g_ffn_post, loss = final_fwd_bwd(x1, fo, par["ffn_post_norm"], target)

    dact = mm(dfo, wts["dn"], "nt", name="d_act", out_dtype=BF16, tn=1408)
    g_dn = mm(act, dfo, "tn", name="g_ffn_down", out_dtype=BF16, tm=1408, tn=2048, tk=1024)
    dgate, dup, g_fcw, g_fcb = conv_bwd("ffn_conv_bwd", gu, 0, c_ffn, 0, dact, par["ffn_conv_w"], "gelu_mul",
                                        up=gu, upoff=FFN_BLOCKS)
    dh2 = mm_segments("d_h2", [(dgate, 0, 0), (dup, 0, D_FF)], [wts["up_t"]], tm=1024, tn=1024, tk=1408)
    g_up_gate = mm(h2, dgate, "tn", name="g_ffn_up_gate", out_dtype=BF16, tm=2048, tn=1408, tk=1024)
    g_up_up = mm(h2, dup, "tn", name="g_ffn_up_up", out_dtype=BF16, tm=2048, tn=1408, tk=1024)
    du, dx1, g_mix_post, g_ffn_pre = post1_bwd(x, u, par["mix_post_norm"], par["ffn_pre_norm"], dx2, dh2)
    dmixed = mm(du, wts["o"], "nt", name="d_mixed", tn=1024)
    g_o = mm(mixed, du, "tn", name="g_w_out", out_dtype=BF16, tm=1024, tn=2048, tk=1024)
    da_hg, da_ssm, dg_hg, dg_ssm = merge_bwd(a_hg, a_ssm, gates, dmixed)
    dy_hg = mm(da_hg, wts["bh"], "nt", name="d_y_hg", out_dtype=BF16, tn=1024)
    g_bh = mm(y_hg, da_hg, "tn", name="g_w_branch_hg", out_dtype=BF16, tm=1024, tn=2048, tk=1024)
    dy_ssm = mm(da_ssm, wts["bs"], "nt", name="d_y_ssm", out_dtype=BF16, tn=1024)
    g_bs = mm(y_ssm, da_ssm, "tn", name="g_w_branch_ssm", out_dtype=BF16, tm=1024, tn=2048, tk=1024)
    dyssd, dskip, dz, g_dexp, g_ssm_norm = ssm_post_bwd(yssd, xbc_act, z, dexp, par["ssm_out_norm"], dy_ssm)

    gg_rest = jnp.concatenate([g.reshape(N_CHIPS, -1, D) for g in (g_bh, g_bs, g_o, g_dn)], axis=1)
    gg_up = jnp.transpose(jnp.concatenate([g_up_gate, g_up_up], axis=1).reshape(D, N_CHIPS, UP_COLS), (1, 0, 2))
    c_rest, c_up = pair_reduce("rest", [gg_rest, gg_up], [REST_PIECES, UP_PIECES], [432, 512], core)
    dxs, db_, dc_, ddt, dacum, rb_rest, rb_up = ssd_bwd(xbc_act, dt, acum, ssd_states, dyssd, dskip, exchange=[c_rest, c_up])
    red_rest, red_up = chip_reduce("rest", [c_rest, c_up], [rb_rest, rb_up], [432, 256], shard)
    ddtr, g_dt_bias, g_a_log = dt_bwd(dtr, bias, a_log, ddt, dacum)
    xs_blocks, bc_blocks = SSM_DINNER // CONV_CB, SSM_GROUPS * SSM_DSTATE // CONV_CB
    dxbc_x, g_cw_x, g_cb_x = conv_bwd("ssm_conv_bwd_x", xbc, 0, c_ssm, 0, dxs, par["ssm_conv_w"], "silu")
    dxbc_b, g_cw_b, g_cb_b = conv_bwd("ssm_conv_bwd_b", xbc, xs_blocks, c_ssm, xs_blocks, db_, par["ssm_conv_w"], "silu")
    dxbc_c, g_cw_c, g_cb_c = conv_bwd("ssm_conv_bwd_c", xbc, xs_blocks + bc_blocks, c_ssm, xs_blocks + bc_blocks, dc_,
                                      par["ssm_conv_w"], "silu")
    dq, df, dv, dg, g_table, g_hg_norm = hgrn2_bwd(qfig, par["hg_lb_table"], par["hg_out_norm"], hg_states, dy_hg)

    dsegs = [(dq, SEG_QFIG), (df, SEG_QFIG + 2048), (dv, SEG_QFIG + 4096), (dg, SEG_QFIG + 6144), (dz, SEG_Z),
             (dxbc_x, SEG_XBC), (dxbc_b, SEG_XBC + SSM_DINNER), (dxbc_c, SEG_XBC + SSM_DINNER + 1024)]
    g_in_parts = [mm(dseg, h, "tn", name=f"g_w_in_{n}", out_dtype=BF16, tm=1024, tn=2048, tk=1024)
                  for n, (dseg, _) in enumerate(dsegs)]
    g_dt_t = mm(ddtr, h, "tn", name="g_w_in_dt", out_dtype=BF16, tm=128, tn=2048, tk=1024)[:SSM_HEADS]
    g_in_parts += [mm(dgate_, h, "tn", name=f"g_w_in_g{n}", out_dtype=BF16, tm=1024, tn=2048, tk=1024)
                   for n, dgate_ in enumerate((dg_hg, dg_ssm))]
    zpad = jnp.zeros((N_CHIPS, IN_ROWS - IN_SHARD, D), BF16)
    g_in_t = jnp.concatenate(g_in_parts[:8] + [g_dt_t] + g_in_parts[8:], axis=0).reshape(N_CHIPS, IN_SHARD, D)
    (c_in,) = pair_reduce("in", [jnp.concatenate([g_in_t, zpad], axis=1)], [IN_PIECES], [960], core)
    dh = mm_segments("d_h_a", [(dseg, 0, off) for dseg, off in dsegs[:5]], [in_t], tm=1024, tn=1024, tk=1024)
    dh, rb_in = mm_segments("d_h_b", [(dseg, 0, off) for dseg, off in dsegs[5:]] + [(ddtr, 1, 0), (dg_hg, 2, 0), (dg_ssm, 2, D)],
                            [in_t, wts["dt_t"], wts["g_t"]], tm=1024, tn=1024, tk=1024, acc=dh, exchange=[c_in])
    (red_in,) = chip_reduce("in", [c_in], [rb_in], [480], shard)
    grad_x, g_mix_pre = norm_bwd(x, par["mix_pre_norm"], dh, dx1)

    big = dict(in_t=red_in, rest=red_rest, up=red_up)
    g_conv_w = jnp.concatenate([g_cw_x, g_cw_b, g_cw_c], axis=1)[:SSM_CONV]
    g_conv_b = jnp.concatenate([g_cb_x, g_cb_b, g_cb_c], axis=1)
    small = dict(mix_pre_norm=g_mix_pre, mix_post_norm=g_mix_post, hg_lb_table=g_table, hg_out_norm=g_hg_norm,
                 ssm_conv_w=g_conv_w, ssm_conv_b=g_conv_b, ssm_dt_bias=g_dt_bias, ssm_A_log=g_a_log,
                 ssm_D=g_dexp, ssm_out_norm=g_ssm_norm, ffn_pre_norm=g_ffn_pre, ffn_post_norm=g_ffn_post,
                 ffn_conv_w=g_fcw[:FFN_CONV], ffn_conv_b=g_fcb)
    return loss, grad_x, big, small


MESH = pl.DeviceIdType.MESH
ANY = pl.BlockSpec(memory_space=pl.ANY)
N_CHIPS = 4
IN_SHARD = 5648
IN_ROWS = 5760
REST_SPLITS = (512, 1536, 2048, 3456)
UP_COLS = 2816
IN_PIECES, REST_PIECES, UP_PIECES = 3, 4, 4


def _place():
    x, y, c = lax.axis_index("x"), lax.axis_index("y"), lax.axis_index("c")
    chips = [(1 - x, y), (x, 1 - y), (1 - x, 1 - y)]
    return x, y, c, chips


def _rcopy(src, dst, send_sems, recv_sems, k, dev):
    return pltpu.make_async_remote_copy(src_ref=src, dst_ref=dst, send_sem=send_sems.at[k], recv_sem=recv_sems.at[k],
                                        device_id=dev, device_id_type=MESH)


def _pieces(rows, n):
    assert rows % n == 0 and (rows // n) % 16 == 0, (rows, n)
    return [(k * (rows // n), rows // n) for k in range(n)]


def _rows(c, hrows, piece):
    return pl.ds(pl.multiple_of(c * hrows + piece[0], 16), piece[1])


def _half_plan(arrays, pieces):
    return [(a.shape[-2] // 2, _pieces(a.shape[-2] // 2, n)) for a, n in zip(arrays, pieces)]


def _sem_pair(n):
    return [pltpu.SemaphoreType.DMA((n,)), pltpu.SemaphoreType.DMA((n,))]


class _Gather:
    def __init__(self, ps, pieces):
        self.plan = _half_plan(ps, pieces)
        self.n_sem = sum(2 * 3 * len(pcs) for _, pcs in self.plan)
        self.out_shape = [jax.ShapeDtypeStruct((N_CHIPS,) + p.shape, p.dtype) for p in ps]

    def _copies(self, p_refs, g_refs, send_sems, recv_sems, only_first=False):
        x, y, c, chips = _place()
        own = 2 * x + y
        sib = (x, y, 1 - c)
        first, arrive, passed, from_sib = [], [], [], []
        k = 0
        for p, g, (hrows, pcs) in zip(p_refs, g_refs, self.plan):
            for chip in chips:
                theirs = 2 * chip[0] + chip[1]
                for pc in pcs:
                    mine, other = _rows(c, hrows, pc), _rows(1 - c, hrows, pc)
                    first.append(_rcopy(p.at[mine], g.at[own, mine], send_sems, recv_sems, k, (*chip, c)))
                    if not only_first:
                        arrive.append(_rcopy(g.at[theirs, mine], g.at[theirs, mine], send_sems, recv_sems, k, (*chip, c)))
                        passed.append(_rcopy(g.at[theirs, mine], g.at[theirs, mine], send_sems, recv_sems, k + 1, sib))
                        from_sib.append(_rcopy(g.at[theirs, other], g.at[theirs, other], send_sems, recv_sems, k + 1, sib))
                    k += 2
        return first, arrive, passed, from_sib

    def start(self, p_refs, g_refs, send_sems, recv_sems):
        for cp in self._copies(p_refs, g_refs, send_sems, recv_sems, only_first=True)[0]:
            cp.start()

    def finish(self, p_refs, g_refs, send_sems, recv_sems):
        first, arrive, passed, from_sib = self._copies(p_refs, g_refs, send_sems, recv_sems)
        for got, fw in zip(arrive, passed):
            got.wait_recv()
            fw.start()
        for cp in from_sib:
            cp.wait_recv()
        for cp in first + passed:
            cp.wait_send()


def gather_weights(name, ps, pieces):
    op = _Gather(ps, pieces)
    n = len(ps)

    def body(*refs):
        p_refs, g_refs, sems = refs[:n], refs[n:2 * n], refs[2 * n:]
        op.start(p_refs, g_refs, *sems)
        op.finish(p_refs, g_refs, *sems)

    return pl.pallas_call(body, name=name, in_specs=[ANY] * n, out_specs=[ANY] * n, out_shape=op.out_shape,
                          scratch_shapes=_sem_pair(op.n_sem))(*ps)


def pair_exchange(name, gs, pieces):
    plan = _half_plan(gs, pieces)
    n_sem = sum(N_CHIPS * len(pcs) for _, pcs in plan)
    n = len(gs)

    def body(*refs):
        g_refs, r_refs, send_sems, recv_sems = refs[:n], refs[n:2 * n], refs[2 * n], refs[2 * n + 1]
        x, y, c, _ = _place()
        sib = (x, y, 1 - c)
        cps = []
        for g, r, (hrows, pcs) in zip(g_refs, r_refs, plan):
            for s in range(N_CHIPS):
                for pc in pcs:
                    cps.append(_rcopy(g.at[s, _rows(1 - c, hrows, pc)], r.at[s, pl.ds(pc[0], pc[1])],
                                      send_sems, recv_sems, len(cps), sib))
        for cp in cps:
            cp.start()
        for cp in cps:
            cp.wait()

    return pl.pallas_call(
        body, name=name, in_specs=[ANY] * n, out_specs=[ANY] * n,
        out_shape=[jax.ShapeDtypeStruct((N_CHIPS, g.shape[1] // 2, g.shape[2]), g.dtype) for g in gs],
        scratch_shapes=_sem_pair(n_sem))(*gs)


class _ChipExchange:
    def __init__(self, ss):
        self.n_sem = 3 * len(ss)
        self.out_shape = [jax.ShapeDtypeStruct((3,) + s.shape[1:], s.dtype) for s in ss]

    def _copies(self, s_refs, r_refs, send_sems, recv_sems):
        x, y, c, chips = _place()
        cps = []
        for s, r in zip(s_refs, r_refs):
            for j, chip in enumerate(chips):
                cps.append(_rcopy(s.at[2 * chip[0] + chip[1]], r.at[j], send_sems, recv_sems, len(cps), (*chip, c)))
        return cps

    def start(self, *refs):
        for cp in self._copies(*refs):
            cp.start()

    def finish(self, *refs):
        for cp in self._copies(*refs):
            cp.wait()


def pair_assemble(name, rs, pieces):
    plan = [(r.shape[0], _pieces(r.shape[0], n_)) for r, n_ in zip(rs, pieces)]
    n_sem = sum(len(pcs) for _, pcs in plan)
    n = len(rs)

    def body(*refs):
        r_refs, f_refs, send_sems, recv_sems = refs[:n], refs[n:2 * n], refs[2 * n], refs[2 * n + 1]
        x, y, c, _ = _place()
        sib = (x, y, 1 - c)
        cps, got = [], []
        for r, f, (hrows, pcs) in zip(r_refs, f_refs, plan):
            for pc in pcs:
                src = r.at[pl.ds(pc[0], pc[1])]
                cps.append(_rcopy(src, f.at[_rows(c, hrows, pc)], send_sems, recv_sems, len(cps), sib))
                got.append(_rcopy(src, f.at[_rows(1 - c, hrows, pc)], send_sems, recv_sems, len(got), sib))
        for cp in cps:
            cp.start()
        for cp in got:
            cp.wait_recv()
        for cp in cps:
            cp.wait_send()

    return pl.pallas_call(
        body, name=name, in_specs=[ANY] * n, out_specs=[ANY] * n,
        out_shape=[jax.ShapeDtypeStruct((2 * r.shape[0], r.shape[1]), r.dtype) for r in rs],
        scratch_shapes=_sem_pair(n_sem))(*rs)


def pair_reduce(tag, ggs, pieces, trs, core):
    recv = pair_exchange("pair_exchange_" + tag, ggs, pieces)
    flat = lambda a: a.reshape(-1, a.shape[-1])
    out = []
    for n, (gg, r, tr) in enumerate(zip(ggs, recv, trs)):
        h = gg.shape[1] // 2
        own = lax.dynamic_slice_in_dim(gg, core * h, h, axis=1)
        out.append(ew_sum(f"pair_sum_{tag}_{n}", [(flat(own), 0), (flat(r), 0)], N_CHIPS * h, BF16, tr).reshape(r.shape))
    return out


def chip_reduce(tag, cs, rbs, trs, shard):
    out = []
    for n, (c, rb, tr) in enumerate(zip(cs, rbs, trs)):
        h = c.shape[1]
        own = lax.dynamic_index_in_dim(c, shard, axis=0, keepdims=False)
        parts = [(own, 0)] + [(rb.reshape(-1, rb.shape[-1]), j * h) for j in range(3)]
        out.append(ew_sum(f"chip_sum_{tag}_{n}", parts, h, F32, tr))
    return out


N_DEV = 8


def gather_small(blk, reduce):
    rows, cols = blk.shape

    def body(x_ref, out_ref, all_ref, send_sems, recv_sems, local_sem):
        x, y, c, chips = _place()
        me, sib = (x, y, c), (x, y, 1 - c)

        def blk_rows(px, py, pc):
            return all_ref.at[pl.ds(pl.multiple_of((4 * px + 2 * py + pc) * rows, 8), rows), :]

        def copy(k, block, to, src=None):
            return _rcopy(blk_rows(*block) if src is None else src, blk_rows(*block), send_sems, recv_sems, k, to)

        mine = pltpu.make_async_copy(x_ref, blk_rows(*me), local_sem)
        mine.start()
        first = [copy(0, me, sib, src=x_ref)] + [copy(1 + j, me, (*chip, c), src=x_ref) for j, chip in enumerate(chips)]
        for cp in first:
            cp.start()
        passed = [copy(4 + j, (*chip, c), sib) for j, chip in enumerate(chips)]
        for j, chip in enumerate(chips):
            copy(1 + j, (*chip, c), me).wait_recv()
            passed[j].start()
        copy(0, sib, me).wait_recv()
        for j, chip in enumerate(chips):
            copy(4 + j, (*chip, 1 - c), me).wait_recv()
        for cp in first + passed:
            cp.wait_send()
        mine.wait()
        if reduce:
            acc = all_ref[0:rows, :]
            for d in range(1, N_DEV):
                acc = acc + all_ref[d * rows:(d + 1) * rows, :]
            out_ref[...] = acc
        else:
            out_ref[...] = all_ref[...]

    vmem = pl.BlockSpec(memory_space=pltpu.VMEM)
    return pl.pallas_call(
        body, name="reduce_small" if reduce else "gather_small", in_specs=[vmem], out_specs=vmem,
        out_shape=jax.ShapeDtypeStruct((rows if reduce else N_DEV * rows, cols), blk.dtype),
        scratch_shapes=[pltpu.VMEM((N_DEV * rows, cols), blk.dtype), pltpu.SemaphoreType.DMA((7,)),
                        pltpu.SemaphoreType.DMA((7,)), pltpu.SemaphoreType.DMA],
        compiler_params=pltpu.CompilerParams(vmem_limit_bytes=VMEM_LIMIT),
    )(blk)


WEIGHTS = ['w_in', 'mix_pre_norm', 'mix_post_norm', 'hg_lb_table', 'hg_out_norm', 'ssm_conv_w', 'ssm_conv_b',
           'ssm_dt_bias', 'ssm_A_log', 'ssm_D', 'ssm_out_norm', 'w_branch_hg', 'w_branch_ssm', 'w_out', 'ffn_pre_norm',
           'ffn_post_norm', 'ffn_w_up', 'ffn_conv_w', 'ffn_conv_b', 'ffn_w_down']
BIG = ('w_in', 'w_branch_hg', 'w_branch_ssm', 'w_out', 'ffn_w_up', 'ffn_w_down')
SMALL = tuple(n for n in WEIGHTS if n not in BIG)
CONV_SHARD = {'ssm_conv_w': SSM_CONV_DIM // N_CHIPS, 'ffn_conv_w': D_FF // N_CHIPS}
LANES = 128


def _pack(parts):
    flat = jnp.concatenate([p.reshape(-1) for p in parts])
    n = flat.shape[0]
    rows = -(-n // (8 * LANES)) * 8
    return jnp.pad(flat, (0, rows * LANES - n)).reshape(rows, LANES)


def _unpack(packed, shapes):
    flat = packed.reshape(-1)
    out, off = [], 0
    for s in shapes:
        n = int(np.prod(s))
        out.append(flat[off:off + n].reshape(s))
        off += n
    return out


def kernel(x, w_in, mix_pre_norm, mix_post_norm, hg_lb_table, hg_out_norm, ssm_conv_w, ssm_conv_b, ssm_dt_bias, ssm_A_log, ssm_D, ssm_out_norm, w_branch_hg, w_branch_ssm, w_out, ffn_pre_norm, ffn_post_norm, ffn_w_up, ffn_conv_w, ffn_conv_b, ffn_w_down, loss_target, m_w_in, m_mix_pre_norm, m_mix_post_norm, m_hg_lb_table, m_hg_out_norm, m_ssm_conv_w, m_ssm_conv_b, m_ssm_dt_bias, m_ssm_A_log, m_ssm_D, m_ssm_out_norm, m_w_branch_hg, m_w_branch_ssm, m_w_out, m_ffn_pre_norm, m_ffn_post_norm, m_ffn_w_up, m_ffn_conv_w, m_ffn_conv_b, m_ffn_w_down, v_w_in, v_mix_pre_norm, v_mix_post_norm, v_hg_lb_table, v_hg_out_norm, v_ssm_conv_w, v_ssm_conv_b, v_ssm_dt_bias, v_ssm_A_log, v_ssm_D, v_ssm_out_norm, v_w_branch_hg, v_w_branch_ssm, v_w_out, v_ffn_pre_norm, v_ffn_post_norm, v_ffn_w_up, v_ffn_conv_w, v_ffn_conv_b, v_ffn_w_down):
    w = dict(w_in=w_in, mix_pre_norm=mix_pre_norm, mix_post_norm=mix_post_norm, hg_lb_table=hg_lb_table, hg_out_norm=hg_out_norm, ssm_conv_w=ssm_conv_w, ssm_conv_b=ssm_conv_b, ssm_dt_bias=ssm_dt_bias, ssm_A_log=ssm_A_log, ssm_D=ssm_D, ssm_out_norm=ssm_out_norm, w_branch_hg=w_branch_hg, w_branch_ssm=w_branch_ssm, w_out=w_out, ffn_pre_norm=ffn_pre_norm, ffn_post_norm=ffn_post_norm, ffn_w_up=ffn_w_up, ffn_conv_w=ffn_conv_w, ffn_conv_b=ffn_conv_b, ffn_w_down=ffn_w_down)
    m = dict(w_in=m_w_in, mix_pre_norm=m_mix_pre_norm, mix_post_norm=m_mix_post_norm, hg_lb_table=m_hg_lb_table, hg_out_norm=m_hg_out_norm, ssm_conv_w=m_ssm_conv_w, ssm_conv_b=m_ssm_conv_b, ssm_dt_bias=m_ssm_dt_bias, ssm_A_log=m_ssm_A_log, ssm_D=m_ssm_D, ssm_out_norm=m_ssm_out_norm, w_branch_hg=m_w_branch_hg, w_branch_ssm=m_w_branch_ssm, w_out=m_w_out, ffn_pre_norm=m_ffn_pre_norm, ffn_post_norm=m_ffn_post_norm, ffn_w_up=m_ffn_w_up, ffn_conv_w=m_ffn_conv_w, ffn_conv_b=m_ffn_conv_b, ffn_w_down=m_ffn_w_down)
    v = dict(w_in=v_w_in, mix_pre_norm=v_mix_pre_norm, mix_post_norm=v_mix_post_norm, hg_lb_table=v_hg_lb_table, hg_out_norm=v_hg_out_norm, ssm_conv_w=v_ssm_conv_w, ssm_conv_b=v_ssm_conv_b, ssm_dt_bias=v_ssm_dt_bias, ssm_A_log=v_ssm_A_log, ssm_D=v_ssm_D, ssm_out_norm=v_ssm_out_norm, w_branch_hg=v_w_branch_hg, w_branch_ssm=v_w_branch_ssm, w_out=v_w_out, ffn_pre_norm=v_ffn_pre_norm, ffn_post_norm=v_ffn_post_norm, ffn_w_up=v_ffn_w_up, ffn_conv_w=v_ffn_conv_w, ffn_conv_b=v_ffn_conv_b, ffn_w_down=v_ffn_w_down)
    shard = 2 * lax.axis_index("x") + lax.axis_index("y")
    bf = lambda a: a.astype(BF16)

    core = lax.axis_index("c")
    p_in = jnp.concatenate([bf(w_in[0].T), jnp.zeros((IN_ROWS - IN_SHARD, D_MODEL), BF16)], axis=0)
    p_rest = jnp.concatenate([bf(w_branch_hg[0]), bf(w_branch_ssm[0]), bf(w_out[0]), bf(ffn_w_down[0])], axis=0)
    p_up = bf(ffn_w_up[0])
    (g_in,) = gather_weights("gather_w_in", [p_in], [IN_PIECES])
    in_t = _own_slot(g_in, p_in, shard)[:, :IN_SHARD].reshape(IN_TOTAL, D_MODEL)
    wts = dict(in_t=in_t, g_t=in_t[SEG_G:], dt_t=jnp.pad(in_t[SEG_DT:SEG_G], ((0, DT_PAD - SSM_HEADS), (0, 0))))
    conv_cols = max(CONV_SHARD.values())
    padc = lambda a: jnp.pad(a, ((0, 0), (0, conv_cols - a.shape[1])))
    conv_blk = jnp.concatenate([padc(ssm_conv_w[0]), padc(ffn_conv_w[0]), jnp.zeros((1, conv_cols), F32)], axis=0)
    conv_all = gather_small(conv_blk, reduce=False)
    par = {n: w[n] for n in SMALL}
    par["ssm_conv_w"] = jnp.concatenate([conv_all[16 * s:16 * s + SSM_CONV, :CONV_SHARD['ssm_conv_w']] for s in range(N_CHIPS)], axis=1)
    par["ffn_conv_w"] = jnp.concatenate([conv_all[16 * s + SSM_CONV:16 * s + SSM_CONV + FFN_CONV, :CONV_SHARD['ffn_conv_w']]
                                         for s in range(N_CHIPS)], axis=1)

    loss, grad_x, big, small = local_step(x[0], loss_target[0], wts, par, p_rest, p_up, shard, core)
    loss = lax.psum(loss[0, 0], ("x", "y", "c"))

    halves = [big["in_t"], big["rest"], big["up"]]
    wholes = pair_assemble("pair_assemble", halves, [IN_PIECES, REST_PIECES, UP_PIECES])
    f_in, f_rest, f_up = [_own_slot(f.reshape((2,) + r.shape), r, core).reshape(f.shape) for f, r in zip(wholes, halves)]
    r0, r1, r2, r3 = REST_SPLITS
    grads = dict(w_in=f_in[:IN_SHARD].T, w_branch_hg=f_rest[:r0], w_branch_ssm=f_rest[r0:r1], w_out=f_rest[r1:r2],
                 ffn_w_down=f_rest[r2:r3], ffn_w_up=f_up)

    small["hg_out_norm"] = ew_sum("sum_heads", [(small["hg_out_norm"][hd], 0) for hd in range(HG_HEADS)], 1, F32, 1)
    small["ssm_D"] = fold_heads(small["ssm_D"])[:, :SSM_HEADS]
    small["ssm_dt_bias"] = small["ssm_dt_bias"][:, :SSM_HEADS]
    small["ssm_A_log"] = small["ssm_A_log"][:, :SSM_HEADS]
    shapes = [small[n].shape for n in SMALL]
    summed = _unpack(gather_small(_pack([small[n] for n in SMALL]), reduce=True), shapes)
    for n, g in zip(SMALL, summed):
        if n in CONV_SHARD:
            g = lax.dynamic_slice_in_dim(g, shard * CONV_SHARD[n], CONV_SHARD[n], axis=1)
        grads[n] = g

    two_d = lambda a: a.reshape(a.shape[-2], a.shape[-1])
    delta, new_m, new_v = {}, {}, {}
    for n, tr in (("w_in", 64), ("w_branch_hg", 128), ("w_branch_ssm", 128), ("w_out", 128), ("ffn_w_up", 128), ("ffn_w_down", 128)):
        delta[n], new_m[n], new_v[n] = adamw("adamw_" + n, two_d(w[n]), grads[n], two_d(m[n]), two_d(v[n]), tr)
    sm_shapes = [two_d(w[n]).shape for n in SMALL]
    packed = adamw("adamw_small", _pack([two_d(w[n]) for n in SMALL]), _pack([grads[n] for n in SMALL]),
                   _pack([two_d(m[n]) for n in SMALL]), _pack([two_d(v[n]) for n in SMALL]), 1024)
    for res, packed_res in zip((delta, new_m, new_v), packed):
        for n, a in zip(SMALL, _unpack(packed_res, sm_shapes)):
            res[n] = a
    shaped = lambda d: [d[n].reshape(w[n].shape) for n in WEIGHTS]
    return (loss, grad_x[None], *shaped(grads), *shaped(delta), *shaped(new_m), *shaped(new_v))
```

```python
import functools

import jax
import jax.numpy as jnp
import numpy as np
from jax import lax
from jax.experimental import pallas as pl
from jax.experimental.pallas import tpu as pltpu

F32 = jnp.float32
BF16 = jnp.bfloat16

D_MODEL = 2048
EPS = 1e-6
HG_HEADS = 16
HG_DK = 128
HG_CHUNK = 64
HG_SUB = 16
SSM_DINNER = 4096
SSM_HEADDIM = 64
SSM_HEADS = 64
SSM_GROUPS = 8
SSM_DSTATE = 128
SSM_CONV = 4
SSM_CHUNK = 256
SSM_CONV_DIM = 6144
D_FF = 5632
FFN_CONV = 3
DT_PAD = 128

ADAM_LR = 0.001
ADAM_B1 = 0.9
ADAM_B2 = 0.999
ADAM_EPS = 1e-08
ADAM_WD = 0.01
ADAM_STEP = 10

VMEM_LIMIT = 56 * 1024 * 1024
HI = lax.Precision.HIGHEST


def _cp(sem, **kw):
    return pltpu.CompilerParams(dimension_semantics=sem, vmem_limit_bytes=VMEM_LIMIT, **kw)


_DIMS = {"nn": (((1,), (0,)), ((), ())), "nt": (((1,), (1,)), ((), ())), "tn": (((0,), (0,)), ((), ()))}


def mm(a, b, mode, *, name, out_dtype=F32, tm=1024, tn=512, tk=None, acc=None, n_major=True,
       dims=None, a_off=(0, 0), b_off=(0, 0)):
    if dims is not None:
        M, N, K = dims
    else:
        if mode == "nn":
            (M, K), (K2, N) = a.shape, b.shape
        elif mode == "nt":
            (M, K), (N, K2) = a.shape, b.shape
        else:
            (K, M), (K2, N) = a.shape, b.shape
        assert K == K2, (a.shape, b.shape, mode)
    tm, tn = min(tm, M), min(tn, N)
    tk = K if tk is None else min(tk, K)
    assert M % tm == 0 and N % tn == 0 and K % tk == 0, (M, N, K, tm, tn, tk)
    a_blk = (tk, tm) if mode == "tn" else (tm, tk)
    b_blk = (tn, tk) if mode == "nt" else (tk, tn)
    assert all(o % s == 0 for o, s in zip(a_off, a_blk)) and all(o % s == 0 for o, s in zip(b_off, b_blk))
    ao0, ao1 = a_off[0] // a_blk[0], a_off[1] // a_blk[1]
    bo0, bo1 = b_off[0] // b_blk[0], b_off[1] // b_blk[1]
    nk = K // tk
    if n_major:
        grid = (N // tn, M // tm, nk)
        ij = lambda p0, p1: (p1, p0)
    else:
        grid = (M // tm, N // tn, nk)
        ij = lambda p0, p1: (p0, p1)

    def a_map(p0, p1, k):
        i, _ = ij(p0, p1)
        return (k + ao0, i + ao1) if mode == "tn" else (i + ao0, k + ao1)

    def b_map(p0, p1, k):
        _, j = ij(p0, p1)
        return (j + bo0, k + bo1) if mode == "nt" else (k + bo0, j + bo1)

    def o_map(p0, p1, k):
        return ij(p0, p1)

    a_spec = pl.BlockSpec(a_blk, a_map)
    b_spec = pl.BlockSpec(b_blk, b_map)
    o_spec = pl.BlockSpec((tm, tn), o_map)
    dims = _DIMS[mode]
    has_acc = acc is not None

    def body(*refs):
        if has_acc:
            a_ref, b_ref, c_ref, o_ref, acc_ref = refs
        else:
            a_ref, b_ref, o_ref, acc_ref = refs
        k = pl.program_id(2)
        part = lax.dot_general(a_ref[...], b_ref[...], dims, preferred_element_type=F32)
        if nk == 1:
            o_ref[...] = (part + c_ref[...].astype(F32) if has_acc else part).astype(out_dtype)
            return

        @pl.when(k == 0)
        def _():
            acc_ref[...] = part

        @pl.when(k > 0)
        def _():
            acc_ref[...] += part

        @pl.when(k == nk - 1)
        def _():
            r = acc_ref[...]
            if has_acc:
                r = r + c_ref[...].astype(F32)
            o_ref[...] = r.astype(out_dtype)

    in_specs = [a_spec, b_spec] + ([o_spec] if has_acc else [])
    args = (a, b) + ((acc,) if has_acc else ())
    return pl.pallas_call(
        body, name=name, grid=grid, in_specs=in_specs, out_specs=o_spec,
        out_shape=jax.ShapeDtypeStruct((M, N), out_dtype),
        scratch_shapes=[pltpu.VMEM((tm, tn) if nk > 1 else (8, 128), F32)],
        compiler_params=_cp(("parallel", "parallel", "arbitrary")),
    )(*args)


def mm_segments(name, segs, bs, *, tm, tn, tk, acc=None, exchange=None, bt=False):
    m_, n_ = segs[0][0].shape[0], bs[0].shape[0 if bt else 1]
    tm, tn = min(tm, m_), min(tn, n_)
    op = _ChipExchange(exchange) if exchange else None
    ne = (len(exchange) if exchange else 0)
    na = 0 if acc is None else 1
    steps, k0 = [], 0
    for a, bi, row in segs:
        w = a.shape[1]
        tks = min(tk, w)
        assert w % tks == 0 and row % tks == 0 and tks == min(tk, bs[bi].shape[1 if bt else 0]), (w, row, tks)
        steps.append((k0, w // tks, tks, bi, row // tks))
        k0 += w // tks
    nk = k0
    assert m_ % tm == 0 and n_ % tn == 0

    def a_spec(k_first, count, tks):
        return pl.BlockSpec((tm, tks), lambda j, i, k: (i, jnp.clip(k - k_first, 0, count - 1)))

    def b_spec(bi):
        mine = [s for s in steps if s[3] == bi]

        def index(j, i, k):
            blk = mine[0][4]
            for k_first, count, _, _, first_blk in mine:
                blk = jnp.where(k >= k_first, first_blk + jnp.minimum(k - k_first, count - 1), blk)
            return (j, blk) if bt else (blk, j)
        return pl.BlockSpec((tn, mine[0][2]) if bt else (mine[0][2], tn), index)

    ns = len(segs)

    nb = len(bs)
    grid = (n_ // tn, m_ // tm, nk)

    def body(*refs):
        a_refs, b_refs = refs[:ns], refs[ns:ns + nb]
        acc_in = refs[ns + nb] if na else None
        rest = refs[ns + nb + na:]
        ex_refs, o_ref, got_refs, acc_ref, sems = rest[:ne], rest[ne], rest[ne + 1:2 * ne + 1], rest[2 * ne + 1], rest[2 * ne + 2:]
        k = pl.program_id(2)
        if op:
            first = (pl.program_id(0) == 0) & (pl.program_id(1) == 0) & (k == 0)
            last = (pl.program_id(0) == grid[0] - 1) & (pl.program_id(1) == grid[1] - 1) & (k == nk - 1)

            @pl.when(first)
            def _():
                op.start(ex_refs, got_refs, *sems)

            @pl.when(last)
            def _():
                op.finish(ex_refs, got_refs, *sems)

        @pl.when(k == 0)
        def _():
            acc_ref[...] = acc_in[...] if na else jnp.zeros_like(acc_ref)

        for a_ref, (k_first, count, _, bi, _) in zip(a_refs, steps):
            @pl.when((k >= k_first) & (k < k_first + count))
            def _(a_ref=a_ref, bi=bi):
                acc_ref[...] += lax.dot_general(a_ref[...], b_refs[bi][...], _DIMS["nt" if bt else "nn"],
                                                preferred_element_type=F32)

        @pl.when(k == nk - 1)
        def _():
            o_ref[...] = acc_ref[...]

    any_spec = pl.BlockSpec(memory_space=pl.ANY)
    o_spec = pl.BlockSpec((tm, tn), lambda j, i, k: (i, j))
    outs = pl.pallas_call(
        body, name=name, grid=grid,
        in_specs=[a_spec(s[0], s[1], s[2]) for s in steps] + [b_spec(bi) for bi in range(nb)] + [o_spec] * na + [any_spec] * ne,
        out_specs=[o_spec] + [any_spec] * ne,
        out_shape=[jax.ShapeDtypeStruct((m_, n_), F32)] + (op.out_shape if op else []),
        scratch_shapes=[pltpu.VMEM((tm, tn), F32)] + (_sem_pair(op.n_sem) if op else []),
        compiler_params=_cp(("arbitrary", "arbitrary", "arbitrary")),
    )(*[a for a, _, _ in segs], *bs, *(() if acc is None else (acc,)), *(exchange or ()))
    return outs if op else outs[0]


def _dims(mode, ndim):
    if ndim == 2:
        return _DIMS[mode]
    (ca,), (cb,) = _DIMS[mode][0]
    return (((ca + 1,), (cb + 1,)), ((0,), (0,)))


def _bdot_plain(a, b, mode):
    return lax.dot_general(a.astype(BF16), b.astype(BF16), _dims(mode, a.ndim), preferred_element_type=F32)


@functools.partial(jax.custom_vjp, nondiff_argnums=(2,))
def _bdot_vjp(a, b, mode):
    return _bdot_plain(a, b, mode)


def _bdot_fwd(a, b, mode):
    return _bdot_plain(a, b, mode), (a, b)


def _bdot_bwd(mode, res, g):
    a, b = res
    if mode == "nn":
        return _bdot_plain(g, b, "nt"), _bdot_plain(a, g, "tn")
    if mode == "nt":
        return _bdot_plain(g, b, "nn"), _bdot_plain(g, a, "tn")
    return _bdot_plain(b, g, "nt"), _bdot_plain(a, g, "nn")


_bdot_vjp.defvjp(_bdot_fwd, _bdot_bwd)


def _split3(x):
    x1 = x.astype(BF16)
    r1 = x - x1.astype(F32)
    x2 = r1.astype(BF16)
    return x1, x2, (r1 - x2.astype(F32)).astype(BF16)


def _hdot_impl(a, b, mode, data):
    dims = _dims(mode, a.ndim)
    if data == "a":
        sel = b.astype(BF16)
        parts = [lax.dot_general(p, sel, dims, preferred_element_type=F32) for p in _split3(a)]
    else:
        sel = a.astype(BF16)
        parts = [lax.dot_general(sel, p, dims, preferred_element_type=F32) for p in _split3(b)]
    return (parts[2] + parts[1]) + parts[0]


@functools.partial(jax.custom_vjp, nondiff_argnums=(2, 3))
def _hdot(a, b, mode="nn", data="b"):
    return _hdot_impl(a, b, mode, data)


def _hdot_fwd(a, b, mode, data):
    return _hdot_impl(a, b, mode, data), (a, b)


def _hdot_bwd(mode, data, res, g):
    a, b = res
    if data == "a":
        da = {"nn": lambda: _hdot_impl(g, b, "nt", "a"), "nt": lambda: _hdot_impl(g, b, "nn", "a"),
              "tn": lambda: _hdot_impl(b, g, "nt", "b")}[mode]()
        return da, jnp.zeros_like(b)
    db = {"nn": lambda: _hdot_impl(a, g, "tn", "b"), "nt": lambda: _hdot_impl(g, a, "tn", "a"),
          "tn": lambda: _hdot_impl(a, g, "nn", "b")}[mode]()
    return jnp.zeros_like(a), db


_hdot.defvjp(_hdot_fwd, _hdot_bwd)


def _sigmoid(x):
    return 1.0 / (1.0 + jnp.exp(-x))


def _silu(x):
    return x * _sigmoid(x)


def _iota(shape, dim):
    return lax.broadcasted_iota(jnp.int32, shape, dim)


def _rms(x, w):
    return x * lax.rsqrt(jnp.mean(x * x, axis=-1, keepdims=True) + EPS) * w


def _hg_chunk(q_raw, f_raw, v, g, st, t0, t1, nw, dot):
    nhd, c = q_raw.shape[0], q_raw.shape[1]
    m = jnp.maximum(t0, t1)
    e0, e1 = jnp.exp(t0 - m), jnp.exp(t1 - m)
    lb = e0 / (e0 + e1)
    f = lb + (1.0 - lb) * _sigmoid(f_raw)
    k = 1.0 - f
    lf = jnp.log(f)
    qh = _silu(q_raw) * (HG_DK ** -0.5)
    row, col = _iota((c, c), 0), _iota((c, c), 1)
    causal = col <= row
    tril = jnp.broadcast_to(jnp.where(causal, 1.0, 0.0).astype(F32), (nhd, c, c))
    trilb = jnp.broadcast_to(jnp.where(causal & (col // HG_SUB == row // HG_SUB), 1.0, 0.0).astype(F32), (nhd, c, c))
    b = _hdot(tril, lf)
    bl = _hdot(trilb, lf)
    a_row = b - bl
    rid = _iota((c, HG_DK), 0)
    qt = qh * jnp.exp(bl)
    kt = k * jnp.exp(-bl)
    scores = jnp.zeros((nhd, c, c), F32)
    for j in range(c // HG_SUB):
        if j == 0:
            qj = qt * jnp.exp(jnp.minimum(a_row, 0.0))
        else:
            a_j = jnp.sum(jnp.where(rid == j * HG_SUB - 1, b, 0.0), axis=1, keepdims=True)
            qj = qt * jnp.exp(jnp.minimum(a_row - a_j, 0.0))
        kj = jnp.where(rid // HG_SUB == j, kt, 0.0)
        scores = scores + dot(qj, kj, "nt")
    scores = jnp.where(causal, scores, 0.0)
    o = dot(scores, v, "nn") + dot(qh * jnp.exp(b), st, "nt")
    b_last = jnp.sum(jnp.where(rid == c - 1, b, 0.0), axis=1, keepdims=True)
    st_new = st * jnp.exp(b_last) + dot(v, k * jnp.exp(b_last - b), "tn")
    y = _rms(o, nw) * _silu(g)
    return y, st_new


HG_HPS = 16
HG_W = HG_HPS * HG_DK


def hgrn2_fwd(qfig, table, nw, *, step_chunks=2, gather=None):
    t = qfig.shape[0]
    rows = HG_CHUNK * step_chunks
    nsteps = t // rows
    nh = HG_HEADS // HG_HPS
    op = _Gather(*gather) if gather else None
    ng = len(gather[0]) if gather else 0

    def body(*refs):
        q_ref, f_ref, v_ref, g_ref, tab_ref, nw_ref = refs[:6]
        p_refs = refs[6:6 + ng]
        y_ref, s_ref = refs[6 + ng:8 + ng]
        got_refs = refs[8 + ng:8 + 2 * ng]
        st_scr = refs[8 + 2 * ng]
        sems = refs[9 + 2 * ng:]
        first_step = (pl.program_id(0) == 0) & (pl.program_id(1) == 0)
        last_step = (pl.program_id(0) == nh - 1) & (pl.program_id(1) == nsteps - 1)
        if op:
            @pl.when(first_step)
            def _():
                op.start(p_refs, got_refs, *sems)

        @pl.when(pl.program_id(1) == 0)
        def _():
            st_scr[...] = jnp.zeros_like(st_scr)

        nwv = nw_ref[...]
        lanes = [pl.ds(hh * HG_DK, HG_DK) for hh in range(HG_HPS)]
        t0 = jnp.stack([tab_ref[0:1, ln] for ln in lanes])
        t1 = jnp.stack([tab_ref[1:2, ln] for ln in lanes])
        for c in range(step_chunks):
            sl = pl.ds(c * HG_CHUNK, HG_CHUNK)
            heads = lambda ref: jnp.stack([ref[sl, ln] for ln in lanes])
            st = st_scr[...]
            for hh in range(HG_HPS):
                s_ref[hh, c] = st[hh]
            y, st_new = _hg_chunk(heads(q_ref), heads(f_ref), heads(v_ref), heads(g_ref), st, t0, t1, nwv, _bdot_vjp)
            for hh, ln in enumerate(lanes):
                y_ref[sl, ln] = y[hh].astype(BF16)
            st_scr[...] = st_new

        if op:
            @pl.when(last_step)
            def _():
                op.finish(p_refs, got_refs, *sems)

    blk = lambda off: pl.BlockSpec((rows, HG_W), lambda h, c, off=off: (c, off + h))
    return pl.pallas_call(
        body, name="hgrn2_fwd", grid=(nh, nsteps),
        in_specs=[blk(0), blk(nh), blk(2 * nh), blk(3 * nh),
                  pl.BlockSpec((2, HG_W), lambda h, c: (0, h)), pl.BlockSpec((1, HG_DK), lambda h, c: (0, 0))] + [ANY] * ng,
        out_specs=[pl.BlockSpec((rows, HG_W), lambda h, c: (c, h)),
                   pl.BlockSpec((HG_HPS, step_chunks, HG_DK, HG_DK), lambda h, c: (h, c, 0, 0))] + [ANY] * ng,
        out_shape=[jax.ShapeDtypeStruct((t, HG_HEADS * HG_DK), BF16),
                   jax.ShapeDtypeStruct((HG_HEADS, t // HG_CHUNK, HG_DK, HG_DK), F32)] + (op.out_shape if op else []),
        scratch_shapes=[pltpu.VMEM((HG_HPS, HG_DK, HG_DK), F32)] + (_sem_pair(op.n_sem) if op else []),
        compiler_params=_cp(("arbitrary", "arbitrary")),
    )(qfig, qfig, qfig, qfig, table, nw, *(gather[0] if gather else ()))


def hgrn2_bwd(qfig, table, nw, states, dy, *, step_chunks=2):
    t = qfig.shape[0]
    rows = HG_CHUNK * step_chunks
    nsteps = t // rows
    nh = HG_HEADS // HG_HPS

    def body(q_ref, f_ref, v_ref, g_ref, tab_ref, nw_ref, s_ref, dy_ref,
             dq_ref, df_ref, dv_ref, dg_ref, dtab_ref, dnw_ref, dst_scr):
        @pl.when(pl.program_id(1) == 0)
        def _():
            dst_scr[...] = jnp.zeros_like(dst_scr)
            dtab_ref[...] = jnp.zeros_like(dtab_ref)
            dnw_ref[...] = jnp.zeros_like(dnw_ref)

        nwv = nw_ref[...]
        fn = functools.partial(_hg_chunk, dot=_bdot_vjp)
        lanes = [pl.ds(hh * HG_DK, HG_DK) for hh in range(HG_HPS)]
        t0 = jnp.stack([tab_ref[0:1, ln] for ln in lanes])
        t1 = jnp.stack([tab_ref[1:2, ln] for ln in lanes])
        for c in reversed(range(step_chunks)):
            sl = pl.ds(c * HG_CHUNK, HG_CHUNK)
            heads = lambda ref: jnp.stack([ref[sl, ln] for ln in lanes])
            _, vjp = jax.vjp(fn, heads(q_ref), heads(f_ref), heads(v_ref), heads(g_ref), s_ref[:, c], t0, t1, nwv)
            dq, df, dv, dg, dst, dt0, dt1, dnw = vjp((heads(dy_ref).astype(F32), dst_scr[...]))
            for hh, ln in enumerate(lanes):
                dq_ref[sl, ln] = dq[hh].astype(BF16)
                df_ref[sl, ln] = df[hh].astype(BF16)
                dv_ref[sl, ln] = dv[hh].astype(BF16)
                dg_ref[sl, ln] = dg[hh].astype(BF16)
                dtab_ref[0:1, ln] += dt0[hh]
                dtab_ref[1:2, ln] += dt1[hh]
            dst_scr[...] = dst
            dnw_ref[0] += dnw

    rev = lambda c: nsteps - 1 - c
    blk = lambda off: pl.BlockSpec((rows, HG_W), lambda h, c, off=off: (rev(c), off + h))
    oblk = lambda: pl.BlockSpec((rows, HG_W), lambda h, c: (rev(c), h))
    d = HG_HEADS * HG_DK
    outs = pl.pallas_call(
        body, name="hgrn2_bwd", grid=(nh, nsteps),
        in_specs=[blk(0), blk(nh), blk(2 * nh), blk(3 * nh),
                  pl.BlockSpec((2, HG_W), lambda h, c: (0, h)), pl.BlockSpec((1, HG_DK), lambda h, c: (0, 0)),
                  pl.BlockSpec((HG_HPS, step_chunks, HG_DK, HG_DK), lambda h, c: (h, rev(c), 0, 0)),
                  pl.BlockSpec((rows, HG_W), lambda h, c: (rev(c), h))],
        out_specs=[oblk(), oblk(), oblk(), oblk(),
                   pl.BlockSpec((2, HG_W), lambda h, c: (0, h)),
                   pl.BlockSpec((HG_HPS, 1, HG_DK), lambda h, c: (h, 0, 0))],
        out_shape=[jax.ShapeDtypeStruct((t, d), BF16)] * 4
        + [jax.ShapeDtypeStruct((2, d), F32), jax.ShapeDtypeStruct((HG_HEADS, 1, HG_DK), F32)],
        scratch_shapes=[pltpu.VMEM((HG_HPS, HG_DK, HG_DK), F32)],
        compiler_params=_cp(("parallel", "arbitrary")),
    )(qfig, qfig, qfig, qfig, table, nw, states, dy)
    return outs


def _ssd_chunk(xs2, dt, acum, bm, cm, s2, pair0, dot):
    npr, c = xs2.shape[0], xs2.shape[1]
    sh_e, sh_s = (npr, DT_PAD, 128), (npr, 8, DT_PAD)
    first_head = 2 * (pair0 + _iota(sh_e, 0))
    expand = jnp.where(_iota(sh_e, 1) == first_head + _iota(sh_e, 2) // SSM_HEADDIM, 1.0, 0.0).astype(F32)
    sel = (_iota(sh_s, 2) == 2 * (pair0 + _iota(sh_s, 0)) + _iota(sh_s, 1)) & (_iota(sh_s, 1) < 2)
    sel = jnp.where(sel, 1.0, 0.0).astype(F32)
    per_pair = lambda a: jnp.broadcast_to(a, (npr,) + a.shape)
    dtx = _hdot(per_pair(dt), expand, "nn", "a")
    acol = _hdot(per_pair(acum), expand, "nn", "a")
    arow8 = _hdot(sel, per_pair(acum), "nt", "b")
    row, col = _iota((c, c), 0), _iota((c, c), 1)
    causal = col <= row
    cb = dot(cm, bm, "nt")
    x2 = xs2 * dtx
    lane_c = _iota((c, 128), 1)
    y = dot(per_pair(cm), s2, "nn") * jnp.exp(acol)
    for r in range(2):
        head = (lane_c // SSM_HEADDIM) == r
        a_c = jnp.sum(jnp.where(head & (lane_c % SSM_HEADDIM == 0), acol, 0.0), axis=2, keepdims=True)
        a_r = jnp.sum(jnp.where(_iota((8, c), 0) == r, arow8, 0.0), axis=1, keepdims=True)
        decay = jnp.exp(jnp.where(causal, a_c - a_r, -1e30))
        y = y + dot(cb * decay, jnp.where(head, x2, 0.0), "nn")
    a_last = jnp.sum(jnp.where(_iota((c, 128), 0) == c - 1, acol, 0.0), axis=1, keepdims=True)
    s2_new = s2 * jnp.exp(a_last) + dot(per_pair(bm), x2 * jnp.exp(a_last - acol), "tn")
    return y, s2_new


SSM_PAIRS = SSM_HEADS // 2
PAIRS_PER_GROUP = SSM_PAIRS // SSM_GROUPS
SSD_PPS = 4
SSD_W = SSD_PPS * 128
_XS_BLOCKS = SSM_DINNER // 128
_B_BLOCK0 = _XS_BLOCKS
_C_BLOCK0 = _XS_BLOCKS + SSM_GROUPS


def ssd_fwd(xbc_act, dt, acum, *, gather=None):
    t = xbc_act.shape[0]
    nc = t // SSM_CHUNK
    c_ = SSM_CHUNK
    nq = SSM_PAIRS // SSD_PPS
    op = _Gather(*gather) if gather else None
    ng = len(gather[0]) if gather else 0

    def body(*refs):
        xs_ref, b_ref, c_ref, dt_ref, ac_ref = refs[:5]
        p_refs = refs[5:5 + ng]
        y_ref, s_ref = refs[5 + ng:7 + ng]
        got_refs = refs[7 + ng:7 + 2 * ng]
        s_scr = refs[7 + 2 * ng]
        sems = refs[8 + 2 * ng:]
        q = pl.program_id(1)
        if op:
            @pl.when((pl.program_id(0) == 0) & (q == 0))
            def _():
                op.start(p_refs, got_refs, *sems)

            @pl.when((pl.program_id(0) == nc - 1) & (q == nq - 1))
            def _():
                op.finish(p_refs, got_refs, *sems)

        mine = pl.ds(SSD_PPS * q, SSD_PPS)
        lanes = [pl.ds(r * 128, 128) for r in range(SSD_PPS)]

        @pl.when(pl.program_id(0) == 0)
        def _():
            s_scr[mine] = jnp.zeros((SSD_PPS, SSM_DSTATE, 128), F32)

        s2 = s_scr[mine]
        s_ref[...] = s2
        xs = jnp.stack([xs_ref[:, ln] for ln in lanes])
        y, s2_new = _ssd_chunk(xs, dt_ref[...], ac_ref[...], b_ref[...], c_ref[...], s2, SSD_PPS * q, _bdot_vjp)
        for r, ln in enumerate(lanes):
            y_ref[:, ln] = y[r]
        s_scr[mine] = s2_new

    grp = lambda q: q // (PAIRS_PER_GROUP // SSD_PPS)
    return pl.pallas_call(
        body, name="ssd_fwd", grid=(nc, SSM_PAIRS // SSD_PPS),
        in_specs=[pl.BlockSpec((c_, SSD_W), lambda c, q: (c, q)),
                  pl.BlockSpec((c_, 128), lambda c, q: (c, _B_BLOCK0 + grp(q))),
                  pl.BlockSpec((c_, 128), lambda c, q: (c, _C_BLOCK0 + grp(q))),
                  pl.BlockSpec((c_, DT_PAD), lambda c, q: (c, 0)),
                  pl.BlockSpec((c_, DT_PAD), lambda c, q: (c, 0))] + [ANY] * ng,
        out_specs=[pl.BlockSpec((c_, SSD_W), lambda c, q: (c, q)),
                   pl.BlockSpec((None, SSD_PPS, SSM_DSTATE, 128), lambda c, q: (c, q, 0, 0))] + [ANY] * ng,
        out_shape=[jax.ShapeDtypeStruct((t, SSM_DINNER), F32),
                   jax.ShapeDtypeStruct((nc, SSM_PAIRS, SSM_DSTATE, 128), F32)] + (op.out_shape if op else []),
        scratch_shapes=[pltpu.VMEM((SSM_PAIRS, SSM_DSTATE, 128), F32)] + (_sem_pair(op.n_sem) if op else []),
        compiler_params=_cp(("arbitrary", "arbitrary")),
    )(xbc_act, xbc_act, xbc_act, dt, acum, *(gather[0] if gather else ()))


def ssd_bwd(xbc_act, dt, acum, states, dy, dskip, *, exchange=None):
    t = xbc_act.shape[0]
    nc = t // SSM_CHUNK
    c_ = SSM_CHUNK
    rev = lambda c: nc - 1 - c
    nq = SSM_PAIRS // SSD_PPS
    op = _ChipExchange(exchange) if exchange else None
    ne = len(exchange) if exchange else 0

    def body(*refs):
        xs_ref, b_ref, c_ref, dt_ref, ac_ref, s_ref, dy_ref, sk_ref = refs[:8]
        ex_refs = refs[8:8 + ne]
        dxs_ref, db_ref, dc_ref, ddt_ref, dac_ref = refs[8 + ne:13 + ne]
        got_refs = refs[13 + ne:13 + 2 * ne]
        ds_scr = refs[13 + 2 * ne]
        sems = refs[14 + 2 * ne:]
        q = pl.program_id(1)
        if op:
            @pl.when((pl.program_id(0) == 0) & (q == 0))
            def _():
                op.start(ex_refs, got_refs, *sems)

            @pl.when((pl.program_id(0) == nc - 1) & (q == nq - 1))
            def _():
                op.finish(ex_refs, got_refs, *sems)

        assert SSD_PPS == PAIRS_PER_GROUP
        mine = pl.ds(SSD_PPS * q, SSD_PPS)
        lanes = [pl.ds(r * 128, 128) for r in range(SSD_PPS)]

        @pl.when(pl.program_id(0) == 0)
        def _():
            ds_scr[mine] = jnp.zeros((SSD_PPS, SSM_DSTATE, 128), F32)

        fn = functools.partial(_ssd_chunk, pair0=SSD_PPS * q, dot=_bdot_vjp)
        xs = jnp.stack([xs_ref[:, ln] for ln in lanes])
        dy = jnp.stack([dy_ref[:, ln] for ln in lanes])
        _, vjp = jax.vjp(fn, xs, dt_ref[...], ac_ref[...], b_ref[...], c_ref[...], s_ref[...])
        dxs, ddt, dac, db, dc, ds = vjp((dy, ds_scr[mine]))
        for r, ln in enumerate(lanes):
            dxs_ref[:, ln] = dxs[r] + sk_ref[:, ln]
        ds_scr[mine] = ds
        db_ref[...] = db
        dc_ref[...] = dc

        @pl.when(q == 0)
        def _():
            ddt_ref[...] = ddt
            dac_ref[...] = dac

        @pl.when(q != 0)
        def _():
            ddt_ref[...] += ddt
            dac_ref[...] += dac

    grp = lambda q: q // (PAIRS_PER_GROUP // SSD_PPS)
    return pl.pallas_call(
        body, name="ssd_bwd", grid=(nc, SSM_PAIRS // SSD_PPS),
        in_specs=[pl.BlockSpec((c_, SSD_W), lambda c, q: (rev(c), q)),
                  pl.BlockSpec((c_, 128), lambda c, q: (rev(c), _B_BLOCK0 + grp(q))),
                  pl.BlockSpec((c_, 128), lambda c, q: (rev(c), _C_BLOCK0 + grp(q))),
                  pl.BlockSpec((c_, DT_PAD), lambda c, q: (rev(c), 0)),
                  pl.BlockSpec((c_, DT_PAD), lambda c, q: (rev(c), 0)),
                  pl.BlockSpec((None, SSD_PPS, SSM_DSTATE, 128), lambda c, q: (rev(c), q, 0, 0)),
                  pl.BlockSpec((c_, SSD_W), lambda c, q: (rev(c), q)),
                  pl.BlockSpec((c_, SSD_W), lambda c, q: (rev(c), q))] + [ANY] * ne,
        out_specs=[pl.BlockSpec((c_, SSD_W), lambda c, q: (rev(c), q)),
                   pl.BlockSpec((c_, 128), lambda c, q: (rev(c), grp(q))),
                   pl.BlockSpec((c_, 128), lambda c, q: (rev(c), grp(q))),
                   pl.BlockSpec((c_, DT_PAD), lambda c, q: (rev(c), 0)),
                   pl.BlockSpec((c_, DT_PAD), lambda c, q: (rev(c), 0))] + [ANY] * ne,
        out_shape=[jax.ShapeDtypeStruct((t, SSM_DINNER), F32),
                   jax.ShapeDtypeStruct((t, SSM_GROUPS * SSM_DSTATE), F32),
                   jax.ShapeDtypeStruct((t, SSM_GROUPS * SSM_DSTATE), F32),
                   jax.ShapeDtypeStruct((t, DT_PAD), F32),
                   jax.ShapeDtypeStruct((t, DT_PAD), F32)] + (op.out_shape if op else []),
        scratch_shapes=[pltpu.VMEM((SSM_PAIRS, SSM_DSTATE, 128), F32)] + (_sem_pair(op.n_sem) if op else []),
        compiler_params=_cp(("arbitrary", "arbitrary")),
    )(xbc_act, xbc_act, xbc_act, dt, acum, states, dy, dskip, *(exchange or ()))


def rowwise(name, fn, row_ins, par_ins, row_outs, acc_outs, *, tt, ncb=1):
    t = row_ins[0][0].shape[0]
    assert t % tt == 0
    n_ri, n_pi, n_ro, n_ao = len(row_ins), len(par_ins), len(row_outs), len(acc_outs)

    def body(*refs):
        i = pl.program_id(1)
        ins = [r[...] for r in refs[:n_ri + n_pi]]
        outs = fn(*ins)
        ro_refs = refs[n_ri + n_pi:n_ri + n_pi + n_ro]
        ao_refs = refs[n_ri + n_pi + n_ro:]
        for r, v in zip(ro_refs, outs[:n_ro]):
            r[...] = v.astype(r.dtype)
        for r, v in zip(ao_refs, outs[n_ro:]):
            @pl.when(i == 0)
            def _(r=r, v=v):
                r[...] = v

            @pl.when(i > 0)
            def _(r=r, v=v):
                r[...] += v

    in_specs = [pl.BlockSpec((tt, bc), lambda j, i, off=off: (i, off + j)) for _, bc, off in row_ins]
    in_specs += [pl.BlockSpec((a.shape[0], bc), lambda j, i, off=off: (0, off + j)) for a, bc, off in par_ins]
    out_specs = [pl.BlockSpec((tt, bc), lambda j, i: (i, j)) for _, bc, _ in row_outs]
    out_specs += [pl.BlockSpec((r, bc), lambda j, i: (0, j)) for r, _, bc in acc_outs]
    out_shape = [jax.ShapeDtypeStruct((t, c), dt) for c, _, dt in row_outs]
    out_shape += [jax.ShapeDtypeStruct((r, c), F32) for r, c, _ in acc_outs]
    return pl.pallas_call(
        body, name=name, grid=(ncb, t // tt), in_specs=in_specs, out_specs=out_specs, out_shape=out_shape,
        compiler_params=_cp(("parallel", "arbitrary")),
    )(*[a for a, _, _ in row_ins], *[a for a, _, _ in par_ins])


def _colsum(v):
    return jnp.sum(v, axis=0, keepdims=True)


def _softplus(x):
    return jnp.maximum(x, 0.0) + jnp.log(1.0 + jnp.exp(-jnp.abs(x)))


def _gelu_tanh(x):
    return 0.5 * x * (1.0 + jnp.tanh(0.7978845608028654 * (x + 0.044715 * (x * x * x))))


D = D_MODEL


def norm_fwd(x, w):
    return rowwise("norm_fwd", lambda xv, wv: (_rms(xv, wv),), [(x, D, 0)], [(w, D, 0)], [(D, D, BF16)], [], tt=256)[0]


def norm_bwd(x, w, dh, dres):
    def fn(xv, dhv, drv, wv):
        _, vjp = jax.vjp(_rms, xv, wv)
        dx, dw = vjp(dhv)
        return dx + drv, dw
    return rowwise("norm_bwd", fn, [(x, D, 0), (dh, D, 0), (dres, D, 0)], [(w, D, 0)], [(D, D, F32)], [(1, D, D)], tt=256)


def _dt_fn(dtr, bias, a_log):
    c = dtr.shape[0]
    dt = _softplus(dtr + bias)
    da = dt * (-jnp.exp(a_log))
    tril = jnp.where(_iota((c, c), 1) <= _iota((c, c), 0), 1.0, 0.0).astype(F32)
    return dt, _hdot(tril, da)


def dt_fwd(dtr, bias, a_log):
    return rowwise("dt_fwd", _dt_fn, [(dtr, DT_PAD, 0)], [(bias, DT_PAD, 0), (a_log, DT_PAD, 0)],
                   [(DT_PAD, DT_PAD, F32), (DT_PAD, DT_PAD, F32)], [], tt=SSM_CHUNK)


def dt_bwd(dtr, bias, a_log, ddt, dacum):
    def fn(dtrv, ddtv, dacv, bv, av):
        _, vjp = jax.vjp(_dt_fn, dtrv, bv, av)
        return vjp((ddtv, dacv))
    return rowwise("dt_bwd", fn, [(dtr, DT_PAD, 0), (ddt, DT_PAD, 0), (dacum, DT_PAD, 0)],
                   [(bias, DT_PAD, 0), (a_log, DT_PAD, 0)],
                   [(DT_PAD, DT_PAD, BF16)], [(1, DT_PAD, DT_PAD), (1, DT_PAD, DT_PAD)], tt=SSM_CHUNK)


GROUP_W = SSM_DINNER // SSM_GROUPS


def _ssm_post_fn(yv, xsv, zv, dexp, nw):
    return _rms((yv + dexp * xsv) * _silu(zv), nw)


def ssm_post_fwd(yssd, xbc_act, z, dexp, nw):
    return rowwise("ssm_post_fwd", lambda *a: (_ssm_post_fn(*a),),
                   [(yssd, GROUP_W, 0), (xbc_act, GROUP_W, 0), (z, GROUP_W, 0)], [(dexp, GROUP_W, 0), (nw, GROUP_W, 0)],
                   [(SSM_DINNER, GROUP_W, BF16)], [], tt=512, ncb=SSM_GROUPS)[0]


def ssm_post_bwd(yssd, xbc_act, z, dexp, nw, dy):
    def fn(yv, xsv, zv, dyv, dv, nv):
        _, vjp = jax.vjp(_ssm_post_fn, yv, xsv, zv, dv, nv)
        return vjp(dyv.astype(F32))
    return rowwise("ssm_post_bwd", fn,
                   [(yssd, GROUP_W, 0), (xbc_act, GROUP_W, 0), (z, GROUP_W, 0), (dy, GROUP_W, 0)],
                   [(dexp, GROUP_W, 0), (nw, GROUP_W, 0)],
                   [(SSM_DINNER, GROUP_W, F32), (SSM_DINNER, GROUP_W, F32), (SSM_DINNER, GROUP_W, BF16)],
                   [(1, SSM_DINNER, GROUP_W), (1, SSM_DINNER, GROUP_W)], tt=512, ncb=SSM_GROUPS)


def _merge_fn(ah, asm, gh, gs):
    return _sigmoid(gh) * ah + _sigmoid(gs) * asm


def merge_fwd(a_hg, a_ssm, gates):
    f32 = lambda vals: [v.astype(F32) for v in vals]
    return rowwise("merge_fwd", lambda *a: (_merge_fn(*f32(a)),), [(a_hg, D, 0), (a_ssm, D, 0), (gates, D, 0), (gates, D, 1)], [],
                   [(D, D, BF16)], [], tt=256)[0]


def merge_bwd(a_hg, a_ssm, gates, dmixed):
    def fn(ah, asm, gh, gs, dm):
        _, vjp = jax.vjp(_merge_fn, *[v.astype(F32) for v in (ah, asm, gh, gs)])
        return vjp(dm.astype(F32))
    return rowwise("merge_bwd", fn, [(a_hg, D, 0), (a_ssm, D, 0), (gates, D, 0), (gates, D, 1), (dmixed, D, 0)], [],
                   [(D, D, BF16)] * 4, [], tt=256)


def _post1_fn(xv, uv, wpost, wpre):
    x1 = xv + _rms(uv, wpost)
    return x1, _rms(x1, wpre)


def post1_fwd(x, u, wpost, wpre):
    return rowwise("post1_fwd", _post1_fn, [(x, D, 0), (u, D, 0)], [(wpost, D, 0), (wpre, D, 0)],
                   [(D, D, F32), (D, D, BF16)], [], tt=256)


def post1_bwd(x, u, wpost, wpre, dx1, dh2):
    def fn(xv, uv, d1, d2, wa, wb):
        _, vjp = jax.vjp(_post1_fn, xv, uv, wa, wb)
        dx, du, dwa, dwb = vjp((d1, d2))
        return du, dx, dwa, dwb
    return rowwise("post1_bwd", fn, [(x, D, 0), (u, D, 0), (dx1, D, 0), (dh2, D, 0)], [(wpost, D, 0), (wpre, D, 0)],
                   [(D, D, BF16), (D, D, F32)], [(1, D, D), (1, D, D)], tt=256)


def final_fwd_bwd(x1, fo, w, target):
    def fn(x1v, fov, tv, wv):
        def loss_fn(a, b, c):
            err = a + _rms(b, c) - tv
            return 0.5 * jnp.sum(err * err) * (1.0 / D)
        loss, vjp = jax.vjp(loss_fn, x1v, fov, wv)
        dx, dfo, dw = vjp(jnp.ones((), F32))
        return dx, dfo, dw, jnp.full((1, 128), loss, F32)
    return rowwise("final_fwd_bwd", fn, [(x1, D, 0), (fo, D, 0), (target, D, 0)], [(w, D, 0)],
                   [(D, D, F32), (D, D, BF16)], [(1, D, D), (1, 128, 128)], tt=256)


HALO = 8
CONV_TT = 512
CONV_CB = 512
CONV_RB = 32


def _tail(kind, c, up):
    return _silu(c) if kind == "silu" else _gelu_tanh(c) * up


def conv_fwd(name, x, xoff, w, b, kind, up=None, upoff=0, act_dtype=F32):
    t = x.shape[0]
    k_, c_ = w.shape
    tt, cb = CONV_TT, CONV_CB
    hb = tt // HALO
    has_up = up is not None

    def body(*refs):
        if has_up:
            x_ref, xp_ref, w_ref, b_ref, up_ref, c_ref, a_ref, scr = refs
        else:
            x_ref, xp_ref, w_ref, b_ref, c_ref, a_ref, scr = refs
        i = pl.program_id(1)
        scr[0:HALO, :] = jnp.where(i == 0, 0.0, xp_ref[...])
        scr[HALO:HALO + tt, :] = x_ref[...]
        for r in range(tt // CONV_RB):
            rows = pl.ds(r * CONV_RB, CONV_RB)
            acc = jnp.zeros((CONV_RB, cb), F32) + b_ref[...]
            for k in range(k_):
                acc = acc + w_ref[k:k + 1, :] * scr[pl.ds(r * CONV_RB + HALO - (k_ - 1) + k, CONV_RB), :]
            c_ref[rows, :] = acc
            a_ref[rows, :] = _tail(kind, acc, up_ref[rows, :] if has_up else None).astype(act_dtype)

    in_specs = [pl.BlockSpec((tt, cb), lambda j, i: (i, xoff + j)),
                pl.BlockSpec((HALO, cb), lambda j, i: (jnp.maximum(i * hb - 1, 0), xoff + j)),
                pl.BlockSpec((k_, cb), lambda j, i: (0, j)),
                pl.BlockSpec((1, cb), lambda j, i: (0, j))]
    args = [x, x, w, b]
    if has_up:
        in_specs.append(pl.BlockSpec((tt, cb), lambda j, i: (i, upoff + j)))
        args.append(up)
    return pl.pallas_call(
        body, name=name, grid=(c_ // cb, t // tt), in_specs=in_specs,
        out_specs=[pl.BlockSpec((tt, cb), lambda j, i: (i, j))] * 2,
        out_shape=[jax.ShapeDtypeStruct((t, c_), F32), jax.ShapeDtypeStruct((t, c_), act_dtype)],
        scratch_shapes=[pltpu.VMEM((tt + HALO, cb), F32)],
        compiler_params=_cp(("parallel", "arbitrary")),
    )(*args)


def conv_bwd(name, x, xoff, c, coff, dact, w, kind, up=None, upoff=0):
    t = x.shape[0]
    k_, c_ = w.shape[0], dact.shape[1]
    tt, cb = CONV_TT, CONV_CB
    hb = tt // HALO
    nt = t // tt
    has_up = up is not None

    def tail_grad(cv, dav, upv):
        if has_up:
            _, vjp = jax.vjp(lambda a, u: _tail(kind, a, u), cv, upv)
            return vjp(dav)
        _, vjp = jax.vjp(lambda a: _tail(kind, a, None), cv)
        return vjp(dav)[0], None

    def body(*refs):
        if has_up:
            (x_ref, xp_ref, c_ref, cn_ref, da_ref, dan_ref, w_ref, up_ref, upn_ref,
             dx_ref, dup_ref, dw_ref, db_ref, xs, dcs) = refs
        else:
            x_ref, xp_ref, c_ref, cn_ref, da_ref, dan_ref, w_ref, dx_ref, dw_ref, db_ref, xs, dcs = refs
        i = pl.program_id(1)
        xs[0:HALO, :] = jnp.where(i == 0, 0.0, xp_ref[...])
        xs[HALO:HALO + tt, :] = x_ref[...]
        rb = CONV_RB
        for r in range(tt // rb):
            rows = pl.ds(r * rb, rb)
            dc, dup = tail_grad(c_ref[rows, :], da_ref[rows, :].astype(F32), up_ref[rows, :] if has_up else None)
            dcs[rows, :] = dc
            if has_up:
                dup_ref[rows, :] = dup.astype(BF16)
        dcn, _ = tail_grad(cn_ref[...], dan_ref[...].astype(F32), upn_ref[...] if has_up else None)
        dcs[tt:tt + HALO, :] = jnp.where(i == nt - 1, 0.0, dcn)
        dws = [jnp.zeros((1, cb), F32) for _ in range(k_)]
        dbv = jnp.zeros((1, cb), F32)
        for r in range(tt // rb):
            rows = pl.ds(r * rb, rb)
            dc = dcs[rows, :]
            dx = jnp.zeros((rb, cb), F32)
            for k in range(k_):
                dx = dx + w_ref[k:k + 1, :] * dcs[pl.ds(r * rb + k_ - 1 - k, rb), :]
                dws[k] = dws[k] + _colsum(dc * xs[pl.ds(r * rb + HALO - (k_ - 1) + k, rb), :])
            dbv = dbv + _colsum(dc)
            dx_ref[rows, :] = dx.astype(BF16)

        @pl.when(i == 0)
        def _():
            dw_ref[...] = jnp.zeros_like(dw_ref)
            db_ref[...] = jnp.zeros_like(db_ref)

        for k in range(k_):
            dw_ref[k:k + 1, :] += dws[k]
        db_ref[...] += dbv

    tile = lambda off: pl.BlockSpec((tt, cb), lambda j, i, off=off: (i, off + j))
    prev = lambda off: pl.BlockSpec((HALO, cb), lambda j, i, off=off: (jnp.maximum(i * hb - 1, 0), off + j))
    nxt = lambda off: pl.BlockSpec((HALO, cb), lambda j, i, off=off: (jnp.minimum((i + 1) * hb, t // HALO - 1), off + j))
    in_specs = [tile(xoff), prev(xoff), tile(coff), nxt(coff), tile(0), nxt(0),
                pl.BlockSpec((k_, cb), lambda j, i: (0, coff + j))]
    args = [x, x, c, c, dact, dact, w]
    if has_up:
        in_specs += [tile(upoff), nxt(upoff)]
        args += [up, up]
    out_specs = [tile(0)] + ([tile(0)] if has_up else []) + [pl.BlockSpec((HALO, cb), lambda j, i: (0, j)),
                                                            pl.BlockSpec((1, cb), lambda j, i: (0, j))]
    out_shape = [jax.ShapeDtypeStruct((t, c_), BF16)] * (2 if has_up else 1)
    out_shape += [jax.ShapeDtypeStruct((HALO, c_), F32), jax.ShapeDtypeStruct((1, c_), F32)]
    return pl.pallas_call(
        body, name=name, grid=(c_ // cb, nt), in_specs=in_specs, out_specs=out_specs, out_shape=out_shape,
        scratch_shapes=[pltpu.VMEM((tt + HALO, cb), F32), pltpu.VMEM((tt + HALO, cb), F32)],
        compiler_params=_cp(("parallel", "arbitrary")),
    )(*args)


def ew_sum(name, parts, rows, out_dtype, tr):
    c = parts[0][0].shape[1]
    tr = min(tr, rows)
    assert rows % tr == 0 and all(off % tr == 0 for _, off in parts)
    n = len(parts)

    def body(*refs):
        acc = refs[0][...].astype(F32)
        for ref in refs[1:n]:
            acc = acc + ref[...].astype(F32)
        refs[n][...] = acc.astype(out_dtype)

    in_specs = [pl.BlockSpec((tr, c), lambda i, o=off // tr: (i + o, 0)) for _, off in parts]
    return pl.pallas_call(body, name=name, grid=(rows // tr,), in_specs=in_specs,
                          out_specs=pl.BlockSpec((tr, c), lambda i: (i, 0)),
                          out_shape=jax.ShapeDtypeStruct((rows, c), out_dtype),
                          compiler_params=_cp(("parallel",)))(*[a for a, _ in parts])


def fold_heads(dexp):
    def body(d_ref, o_ref):
        sel = jnp.where(_iota((SSM_DINNER, DT_PAD), 0) // SSM_HEADDIM == _iota((SSM_DINNER, DT_PAD), 1), 1.0, 0.0)
        o_ref[...] = _hdot(jnp.broadcast_to(d_ref[...], (8, SSM_DINNER)), sel.astype(F32), "nn", "a")[0:1, :]

    return pl.pallas_call(body, name="fold_heads", out_shape=jax.ShapeDtypeStruct((1, DT_PAD), F32),
                          compiler_params=pltpu.CompilerParams(vmem_limit_bytes=VMEM_LIMIT))(dexp)


def adamw(name, w, g, m, v, tr):
    r, c = w.shape
    tr = min(tr, r)
    assert r % tr == 0, (r, tr)

    def body(w_ref, g_ref, m_ref, v_ref, d_ref, nm_ref, nv_ref):
        gv = g_ref[...]
        nm = ADAM_B1 * m_ref[...] + (1.0 - ADAM_B1) * gv
        nv = ADAM_B2 * v_ref[...] + (1.0 - ADAM_B2) * (gv * gv)
        m_hat = nm / (1.0 - ADAM_B1 ** ADAM_STEP)
        v_hat = nv / (1.0 - ADAM_B2 ** ADAM_STEP)
        d_ref[...] = -ADAM_LR * (m_hat / (jnp.sqrt(v_hat) + ADAM_EPS) + ADAM_WD * w_ref[...])
        nm_ref[...] = nm
        nv_ref[...] = nv

    spec = pl.BlockSpec((tr, c), lambda i: (i, 0))
    shp = jax.ShapeDtypeStruct((r, c), F32)
    return pl.pallas_call(body, name=name, grid=(r // tr,), in_specs=[spec] * 4, out_specs=[spec] * 3,
                          out_shape=[shp] * 3, compiler_params=_cp(("parallel",)))(w, g, m, v)


SEG_QFIG, SEG_Z, SEG_XBC, SEG_DT, SEG_G = 0, 8192, 12288, 18432, 18496
IN_TOTAL = 22592
FFN_BLOCKS = D_FF // CONV_CB


def _own_slot(gathered, own, shard):
    slot = lax.broadcasted_iota(jnp.int32, (gathered.shape[0],) + (1,) * own.ndim, 0)
    return jnp.where(slot == shard, own[None], gathered)


def local_step(x, target, wts, par, p_rest, p_up, shard, core):
    t = x.shape[0]
    pad64 = lambda a: jnp.pad(a, ((0, 0), (0, DT_PAD - a.shape[1])))
    bias, a_log = pad64(par["ssm_dt_bias"]), pad64(par["ssm_A_log"])
    dexp = jnp.repeat(par["ssm_D"], SSM_HEADDIM, axis=1)
    in_t = wts["in_t"]

    h = norm_fwd(x, par["mix_pre_norm"])
    proj = lambda nm, off, n, tn: mm(h, in_t, "nt", name=nm, tn=tn, dims=(t, n, D), b_off=(off, 0))
    qfig = proj("proj_qfig", SEG_QFIG, 8192, 1024)
    z = proj("proj_z", SEG_Z, 4096, 1024)
    xbc = proj("proj_xbc", SEG_XBC, 6144, 1024)
    dtr = mm(h, wts["dt_t"], "nt", name="proj_dt", tn=128)
    gates = mm(h, wts["g_t"], "nt", name="proj_gates", out_dtype=BF16, tn=1024)
    y_hg, hg_states, g_rest = hgrn2_fwd(qfig, par["hg_lb_table"], par["hg_out_norm"], gather=([p_rest], [REST_PIECES]))
    c_ssm, xbc_act = conv_fwd("ssm_conv_fwd", xbc, 0, par["ssm_conv_w"], par["ssm_conv_b"], "silu")
    dt, acum = dt_fwd(dtr, bias, a_log)
    yssd, ssd_states, g_up = ssd_fwd(xbc_act, dt, acum, gather=([p_up], [UP_PIECES]))
    g_rest, g_up = _own_slot(g_rest, p_rest, shard), _own_slot(g_up, p_up, shard)
    r0, r1, r2, r3 = REST_SPLITS
    wts = dict(wts, bh=g_rest[:, :r0].reshape(-1, D), bs=g_rest[:, r0:r1].reshape(-1, D), o=g_rest[:, r1:r2].reshape(-1, D),
               dn=g_rest[:, r2:r3].reshape(-1, D), up=jnp.transpose(g_up, (1, 0, 2)).reshape(D, 2 * D_FF))
    y_ssm = ssm_post_fwd(yssd, xbc_act, z, dexp, par["ssm_out_norm"])
    a_hg = mm(y_hg, wts["bh"], "nn", name="branch_hg", out_dtype=BF16, tn=1024)
    a_ssm = mm(y_ssm, wts["bs"], "nn", name="branch_ssm", out_dtype=BF16, tn=1024)
    mixed = merge_fwd(a_hg, a_ssm, gates)
    u = mm(mixed, wts["o"], "nn", name="out_proj", tn=1024)
    x1, h2 = post1_fwd(x, u, par["mix_post_norm"], par["ffn_pre_norm"])
    gu = mm(h2, wts["up"], "nn", name="ffn_up", tn=1024)
    c_ffn, act = conv_fwd("ffn_conv_fwd", gu, 0, par["ffn_conv_w"], par["ffn_conv_b"], "gelu_mul",
                          up=gu, upoff=FFN_BLOCKS, act_dtype=BF16)
    fo = mm(act, wts["dn"], "nn", name="ffn_down", tm=512, tn=1024)
    dx2, dfo, g_ffn_post, loss = final_fwd_bwd(x1, fo, par["ffn_post_norm"], target)

    dact = mm(dfo, wts["dn"], "nt", name="d_act", out_dtype=BF16, tn=1408)
    g_dn = mm(act, dfo, "tn", name="g_ffn_down", out_dtype=BF16, tm=1408, tn=2048, tk=1024)
    dgate, dup, g_fcw, g_fcb = conv_bwd("ffn_conv_bwd", gu, 0, c_ffn, 0, dact, par["ffn_conv_w"], "gelu_mul",
                                        up=gu, upoff=FFN_BLOCKS)
    dh2 = mm_segments("d_h2", [(dgate, 0, 0), (dup, 0, D_FF)], [wts["up"]], tm=1024, tn=1024, tk=1408, bt=True)
    g_up_gate = mm(h2, dgate, "tn", name="g_ffn_up_gate", out_dtype=BF16, tm=2048, tn=1408, tk=1024)
    g_up_up = mm(h2, dup, "tn", name="g_ffn_up_up", out_dtype=BF16, tm=2048, tn=1408, tk=1024)
    du, dx1, g_mix_post, g_ffn_pre = post1_bwd(x, u, par["mix_post_norm"], par["ffn_pre_norm"], dx2, dh2)
    dmixed = mm(du, wts["o"], "nt", name="d_mixed", tn=1024)
    g_o = mm(mixed, du, "tn", name="g_w_out", out_dtype=BF16, tm=1024, tn=2048, tk=2048)
    da_hg, da_ssm, dg_hg, dg_ssm = merge_bwd(a_hg, a_ssm, gates, dmixed)
    dy_hg = mm(da_hg, wts["bh"], "nt", name="d_y_hg", out_dtype=BF16, tn=1024)
    g_bh = mm(y_hg, da_hg, "tn", name="g_w_branch_hg", out_dtype=BF16, tm=1024, tn=2048, tk=2048)
    dy_ssm = mm(da_ssm, wts["bs"], "nt", name="d_y_ssm", out_dtype=BF16, tn=1024)
    g_bs = mm(y_ssm, da_ssm, "tn", name="g_w_branch_ssm", out_dtype=BF16, tm=1024, tn=2048, tk=2048)
    dyssd, dskip, dz, g_dexp, g_ssm_norm = ssm_post_bwd(yssd, xbc_act, z, dexp, par["ssm_out_norm"], dy_ssm)

    gg_rest = jnp.concatenate([g.reshape(N_CHIPS, -1, D) for g in (g_bh, g_bs, g_o, g_dn)], axis=1)
    gg_up = jnp.transpose(jnp.concatenate([g_up_gate, g_up_up], axis=1).reshape(D, N_CHIPS, UP_COLS), (1, 0, 2))
    c_rest, c_up = pair_reduce("rest", [gg_rest, gg_up], [REST_PIECES, UP_PIECES], [432, 512], core)
    dxs, db_, dc_, ddt, dacum, rb_rest, rb_up = ssd_bwd(xbc_act, dt, acum, ssd_states, dyssd, dskip, exchange=[c_rest, c_up])
    red_rest, red_up = chip_reduce("rest", [c_rest, c_up], [rb_rest, rb_up], [432, 256], shard)
    ddtr, g_dt_bias, g_a_log = dt_bwd(dtr, bias, a_log, ddt, dacum)
    xs_blocks, bc_blocks = SSM_DINNER // CONV_CB, SSM_GROUPS * SSM_DSTATE // CONV_CB
    dxbc_x, g_cw_x, g_cb_x = conv_bwd("ssm_conv_bwd_x", xbc, 0, c_ssm, 0, dxs, par["ssm_conv_w"], "silu")
    dxbc_b, g_cw_b, g_cb_b = conv_bwd("ssm_conv_bwd_b", xbc, xs_blocks, c_ssm, xs_blocks, db_, par["ssm_conv_w"], "silu")
    dxbc_c, g_cw_c, g_cb_c = conv_bwd("ssm_conv_bwd_c", xbc, xs_blocks + bc_blocks, c_ssm, xs_blocks + bc_blocks, dc_,
                                      par["ssm_conv_w"], "silu")
    dq, df, dv, dg, g_table, g_hg_norm = hgrn2_bwd(qfig, par["hg_lb_table"], par["hg_out_norm"], hg_states, dy_hg)

    dsegs = [(dq, SEG_QFIG), (df, SEG_QFIG + 2048), (dv, SEG_QFIG + 4096), (dg, SEG_QFIG + 6144), (dz, SEG_Z),
             (dxbc_x, SEG_XBC), (dxbc_b, SEG_XBC + SSM_DINNER), (dxbc_c, SEG_XBC + SSM_DINNER + 1024)]
    g_in_parts = [mm(dseg, h, "tn", name=f"g_w_in_{n}", out_dtype=BF16, tm=1024, tn=2048, tk=2048)
                  for n, (dseg, _) in enumerate(dsegs)]
    g_dt_t = mm(ddtr, h, "tn", name="g_w_in_dt", out_dtype=BF16, tm=128, tn=2048, tk=1024)[:SSM_HEADS]
    g_in_parts += [mm(dgate_, h, "tn", name=f"g_w_in_g{n}", out_dtype=BF16, tm=1024, tn=2048, tk=2048)
                   for n, dgate_ in enumerate((dg_hg, dg_ssm))]
    zpad = jnp.zeros((N_CHIPS, IN_ROWS - IN_SHARD, D), BF16)
    g_in_t = jnp.concatenate(g_in_parts[:8] + [g_dt_t] + g_in_parts[8:], axis=0).reshape(N_CHIPS, IN_SHARD, D)
    (c_in,) = pair_reduce("in", [jnp.concatenate([g_in_t, zpad], axis=1)], [IN_PIECES], [960], core)
    dh = mm_segments("d_h_a", [(dseg, 0, off) for dseg, off in dsegs[:5]], [in_t], tm=1024, tn=1024, tk=1024)
    dh, rb_in = mm_segments("d_h_b", [(dseg, 0, off) for dseg, off in dsegs[5:]] + [(ddtr, 1, 0), (dg_hg, 2, 0), (dg_ssm, 2, D)],
                            [in_t, wts["dt_t"], wts["g_t"]], tm=1024, tn=1024, tk=1024, acc=dh, exchange=[c_in])
    (red_in,) = chip_reduce("in", [c_in], [rb_in], [480], shard)
    grad_x, g_mix_pre = norm_bwd(x, par["mix_pre_norm"], dh, dx1)

    big = dict(in_t=red_in, rest=red_rest, up=red_up)
    g_conv_w = jnp.concatenate([g_cw_x, g_cw_b, g_cw_c], axis=1)[:SSM_CONV]
    g_conv_b = jnp.concatenate([g_cb_x, g_cb_b, g_cb_c], axis=1)
    small = dict(mix_pre_norm=g_mix_pre, mix_post_norm=g_mix_post, hg_lb_table=g_table, hg_out_norm=g_hg_norm,
                 ssm_conv_w=g_conv_w, ssm_conv_b=g_conv_b, ssm_dt_bias=g_dt_bias, ssm_A_log=g_a_log,
                 ssm_D=g_dexp, ssm_out_norm=g_ssm_norm, ffn_pre_norm=g_ffn_pre, ffn_post_norm=g_ffn_post,
                 ffn_conv_w=g_fcw[:FFN_CONV], ffn_conv_b=g_fcb)
    return loss, grad_x, big, small


MESH = pl.DeviceIdType.MESH
ANY = pl.BlockSpec(memory_space=pl.ANY)
N_CHIPS = 4
IN_SHARD = 5648
IN_ROWS = 5760
REST_SPLITS = (512, 1536, 2048, 3456)
UP_COLS = 2816
IN_PIECES, REST_PIECES, UP_PIECES = 3, 4, 4


def _place():
    x, y, c = lax.axis_index("x"), lax.axis_index("y"), lax.axis_index("c")
    chips = [(1 - x, y), (x, 1 - y), (1 - x, 1 - y)]
    return x, y, c, chips


def _rcopy(src, dst, send_sems, recv_sems, k, dev):
    return pltpu.make_async_remote_copy(src_ref=src, dst_ref=dst, send_sem=send_sems.at[k], recv_sem=recv_sems.at[k],
                                        device_id=dev, device_id_type=MESH)


def _pieces(rows, n):
    assert rows % n == 0 and (rows // n) % 16 == 0, (rows, n)
    return [(k * (rows // n), rows // n) for k in range(n)]


def _rows(c, hrows, piece):
    return pl.ds(pl.multiple_of(c * hrows + piece[0], 16), piece[1])


def _half_plan(arrays, pieces):
    return [(a.shape[-2] // 2, _pieces(a.shape[-2] // 2, n)) for a, n in zip(arrays, pieces)]


def _sem_pair(n):
    return [pltpu.SemaphoreType.DMA((n,)), pltpu.SemaphoreType.DMA((n,))]


class _Gather:
    def __init__(self, ps, pieces):
        self.plan = _half_plan(ps, pieces)
        self.n_sem = sum(2 * 3 * len(pcs) for _, pcs in self.plan)
        self.out_shape = [jax.ShapeDtypeStruct((N_CHIPS,) + p.shape, p.dtype) for p in ps]

    def _copies(self, p_refs, g_refs, send_sems, recv_sems, only_first=False):
        x, y, c, chips = _place()
        own = 2 * x + y
        sib = (x, y, 1 - c)
        first, arrive, passed, from_sib = [], [], [], []
        k = 0
        for p, g, (hrows, pcs) in zip(p_refs, g_refs, self.plan):
            for chip in chips:
                theirs = 2 * chip[0] + chip[1]
                for pc in pcs:
                    mine, other = _rows(c, hrows, pc), _rows(1 - c, hrows, pc)
                    first.append(_rcopy(p.at[mine], g.at[own, mine], send_sems, recv_sems, k, (*chip, c)))
                    if not only_first:
                        arrive.append(_rcopy(g.at[theirs, mine], g.at[theirs, mine], send_sems, recv_sems, k, (*chip, c)))
                        passed.append(_rcopy(g.at[theirs, mine], g.at[theirs, mine], send_sems, recv_sems, k + 1, sib))
                        from_sib.append(_rcopy(g.at[theirs, other], g.at[theirs, other], send_sems, recv_sems, k + 1, sib))
                    k += 2
        return first, arrive, passed, from_sib

    def start(self, p_refs, g_refs, send_sems, recv_sems):
        for cp in self._copies(p_refs, g_refs, send_sems, recv_sems, only_first=True)[0]:
            cp.start()

    def finish(self, p_refs, g_refs, send_sems, recv_sems):
        first, arrive, passed, from_sib = self._copies(p_refs, g_refs, send_sems, recv_sems)
        for got, fw in zip(arrive, passed):
            got.wait_recv()
            fw.start()
        for cp in from_sib:
            cp.wait_recv()
        for cp in first + passed:
            cp.wait_send()


def gather_weights(name, ps, pieces):
    op = _Gather(ps, pieces)
    n = len(ps)

    def body(*refs):
        p_refs, g_refs, sems = refs[:n], refs[n:2 * n], refs[2 * n:]
        op.start(p_refs, g_refs, *sems)
        op.finish(p_refs, g_refs, *sems)

    return pl.pallas_call(body, name=name, in_specs=[ANY] * n, out_specs=[ANY] * n, out_shape=op.out_shape,
                          scratch_shapes=_sem_pair(op.n_sem))(*ps)


def pair_exchange(name, gs, pieces):
    plan = _half_plan(gs, pieces)
    n_sem = sum(N_CHIPS * len(pcs) for _, pcs in plan)
    n = len(gs)

    def body(*refs):
        g_refs, r_refs, send_sems, recv_sems = refs[:n], refs[n:2 * n], refs[2 * n], refs[2 * n + 1]
        x, y, c, _ = _place()
        sib = (x, y, 1 - c)
        cps = []
        for g, r, (hrows, pcs) in zip(g_refs, r_refs, plan):
            for s in range(N_CHIPS):
                for pc in pcs:
                    cps.append(_rcopy(g.at[s, _rows(1 - c, hrows, pc)], r.at[s, pl.ds(pc[0], pc[1])],
                                      send_sems, recv_sems, len(cps), sib))
        for cp in cps:
            cp.start()
        for cp in cps:
            cp.wait()

    return pl.pallas_call(
        body, name=name, in_specs=[ANY] * n, out_specs=[ANY] * n,
        out_shape=[jax.ShapeDtypeStruct((N_CHIPS, g.shape[1] // 2, g.shape[2]), g.dtype) for g in gs],
        scratch_shapes=_sem_pair(n_sem))(*gs)


class _ChipExchange:
    def __init__(self, ss):
        self.n_sem = 3 * len(ss)
        self.out_shape = [jax.ShapeDtypeStruct((3,) + s.shape[1:], s.dtype) for s in ss]

    def _copies(self, s_refs, r_refs, send_sems, recv_sems):
        x, y, c, chips = _place()
        cps = []
        for s, r in zip(s_refs, r_refs):
            for j, chip in enumerate(chips):
                cps.append(_rcopy(s.at[2 * chip[0] + chip[1]], r.at[j], send_sems, recv_sems, len(cps), (*chip, c)))
        return cps

    def start(self, *refs):
        for cp in self._copies(*refs):
            cp.start()

    def finish(self, *refs):
        for cp in self._copies(*refs):
            cp.wait()


def pair_assemble(name, rs, pieces):
    plan = [(r.shape[0], _pieces(r.shape[0], n_)) for r, n_ in zip(rs, pieces)]
    n_sem = sum(len(pcs) for _, pcs in plan)
    n = len(rs)

    def body(*refs):
        r_refs, f_refs, send_sems, recv_sems = refs[:n], refs[n:2 * n], refs[2 * n], refs[2 * n + 1]
        x, y, c, _ = _place()
        sib = (x, y, 1 - c)
        cps, got = [], []
        for r, f, (hrows, pcs) in zip(r_refs, f_refs, plan):
            for pc in pcs:
                src = r.at[pl.ds(pc[0], pc[1])]
                cps.append(_rcopy(src, f.at[_rows(c, hrows, pc)], send_sems, recv_sems, len(cps), sib))
                got.append(_rcopy(src, f.at[_rows(1 - c, hrows, pc)], send_sems, recv_sems, len(got), sib))
        for cp in cps:
            cp.start()
        for cp in got:
            cp.wait_recv()
        for cp in cps:
            cp.wait_send()

    return pl.pallas_call(
        body, name=name, in_specs=[ANY] * n, out_specs=[ANY] * n,
        out_shape=[jax.ShapeDtypeStruct((2 * r.shape[0], r.shape[1]), r.dtype) for r in rs],
        scratch_shapes=_sem_pair(n_sem))(*rs)


def pair_reduce(tag, ggs, pieces, trs, core):
    recv = pair_exchange("pair_exchange_" + tag, ggs, pieces)
    flat = lambda a: a.reshape(-1, a.shape[-1])
    out = []
    for n, (gg, r, tr) in enumerate(zip(ggs, recv, trs)):
        h = gg.shape[1] // 2
        own = lax.dynamic_slice_in_dim(gg, core * h, h, axis=1)
        out.append(ew_sum(f"pair_sum_{tag}_{n}", [(flat(own), 0), (flat(r), 0)], N_CHIPS * h, BF16, tr).reshape(r.shape))
    return out


def chip_reduce(tag, cs, rbs, trs, shard):
    out = []
    for n, (c, rb, tr) in enumerate(zip(cs, rbs, trs)):
        h = c.shape[1]
        own = lax.dynamic_index_in_dim(c, shard, axis=0, keepdims=False)
        parts = [(own, 0)] + [(rb.reshape(-1, rb.shape[-1]), j * h) for j in range(3)]
        out.append(ew_sum(f"chip_sum_{tag}_{n}", parts, h, F32, tr))
    return out


N_DEV = 8


def gather_small(blk, reduce):
    rows, cols = blk.shape

    def body(x_ref, out_ref, all_ref, send_sems, recv_sems, local_sem):
        x, y, c, chips = _place()
        me, sib = (x, y, c), (x, y, 1 - c)

        def blk_rows(px, py, pc):
            return all_ref.at[pl.ds(pl.multiple_of((4 * px + 2 * py + pc) * rows, 8), rows), :]

        def copy(k, block, to, src=None):
            return _rcopy(blk_rows(*block) if src is None else src, blk_rows(*block), send_sems, recv_sems, k, to)

        mine = pltpu.make_async_copy(x_ref, blk_rows(*me), local_sem)
        mine.start()
        first = [copy(0, me, sib, src=x_ref)] + [copy(1 + j, me, (*chip, c), src=x_ref) for j, chip in enumerate(chips)]
        for cp in first:
            cp.start()
        passed = [copy(4 + j, (*chip, c), sib) for j, chip in enumerate(chips)]
        for j, chip in enumerate(chips):
            copy(1 + j, (*chip, c), me).wait_recv()
            passed[j].start()
        copy(0, sib, me).wait_recv()
        for j, chip in enumerate(chips):
            copy(4 + j, (*chip, 1 - c), me).wait_recv()
        for cp in first + passed:
            cp.wait_send()
        mine.wait()
        if reduce:
            acc = all_ref[0:rows, :]
            for d in range(1, N_DEV):
                acc = acc + all_ref[d * rows:(d + 1) * rows, :]
            out_ref[...] = acc
        else:
            out_ref[...] = all_ref[...]

    vmem = pl.BlockSpec(memory_space=pltpu.VMEM)
    return pl.pallas_call(
        body, name="reduce_small" if reduce else "gather_small", in_specs=[vmem], out_specs=vmem,
        out_shape=jax.ShapeDtypeStruct((rows if reduce else N_DEV * rows, cols), blk.dtype),
        scratch_shapes=[pltpu.VMEM((N_DEV * rows, cols), blk.dtype), pltpu.SemaphoreType.DMA((7,)),
                        pltpu.SemaphoreType.DMA((7,)), pltpu.SemaphoreType.DMA],
        compiler_params=pltpu.CompilerParams(vmem_limit_bytes=VMEM_LIMIT),
    )(blk)


WEIGHTS = ['w_in', 'mix_pre_norm', 'mix_post_norm', 'hg_lb_table', 'hg_out_norm', 'ssm_conv_w', 'ssm_conv_b',
           'ssm_dt_bias', 'ssm_A_log', 'ssm_D', 'ssm_out_norm', 'w_branch_hg', 'w_branch_ssm', 'w_out', 'ffn_pre_norm',
           'ffn_post_norm', 'ffn_w_up', 'ffn_conv_w', 'ffn_conv_b', 'ffn_w_down']
BIG = ('w_in', 'w_branch_hg', 'w_branch_ssm', 'w_out', 'ffn_w_up', 'ffn_w_down')
SMALL = tuple(n for n in WEIGHTS if n not in BIG)
CONV_SHARD = {'ssm_conv_w': SSM_CONV_DIM // N_CHIPS, 'ffn_conv_w': D_FF // N_CHIPS}
LANES = 128


def _pack(parts):
    flat = jnp.concatenate([p.reshape(-1) for p in parts])
    n = flat.shape[0]
    rows = -(-n // (8 * LANES)) * 8
    return jnp.pad(flat, (0, rows * LANES - n)).reshape(rows, LANES)


def _unpack(packed, shapes):
    flat = packed.reshape(-1)
    out, off = [], 0
    for s in shapes:
        n = int(np.prod(s))
        out.append(flat[off:off + n].reshape(s))
        off += n
    return out


def kernel(x, w_in, mix_pre_norm, mix_post_norm, hg_lb_table, hg_out_norm, ssm_conv_w, ssm_conv_b, ssm_dt_bias, ssm_A_log, ssm_D, ssm_out_norm, w_branch_hg, w_branch_ssm, w_out, ffn_pre_norm, ffn_post_norm, ffn_w_up, ffn_conv_w, ffn_conv_b, ffn_w_down, loss_target, m_w_in, m_mix_pre_norm, m_mix_post_norm, m_hg_lb_table, m_hg_out_norm, m_ssm_conv_w, m_ssm_conv_b, m_ssm_dt_bias, m_ssm_A_log, m_ssm_D, m_ssm_out_norm, m_w_branch_hg, m_w_branch_ssm, m_w_out, m_ffn_pre_norm, m_ffn_post_norm, m_ffn_w_up, m_ffn_conv_w, m_ffn_conv_b, m_ffn_w_down, v_w_in, v_mix_pre_norm, v_mix_post_norm, v_hg_lb_table, v_hg_out_norm, v_ssm_conv_w, v_ssm_conv_b, v_ssm_dt_bias, v_ssm_A_log, v_ssm_D, v_ssm_out_norm, v_w_branch_hg, v_w_branch_ssm, v_w_out, v_ffn_pre_norm, v_ffn_post_norm, v_ffn_w_up, v_ffn_conv_w, v_ffn_conv_b, v_ffn_w_down):
    w = dict(w_in=w_in, mix_pre_norm=mix_pre_norm, mix_post_norm=mix_post_norm, hg_lb_table=hg_lb_table, hg_out_norm=hg_out_norm, ssm_conv_w=ssm_conv_w, ssm_conv_b=ssm_conv_b, ssm_dt_bias=ssm_dt_bias, ssm_A_log=ssm_A_log, ssm_D=ssm_D, ssm_out_norm=ssm_out_norm, w_branch_hg=w_branch_hg, w_branch_ssm=w_branch_ssm, w_out=w_out, ffn_pre_norm=ffn_pre_norm, ffn_post_norm=ffn_post_norm, ffn_w_up=ffn_w_up, ffn_conv_w=ffn_conv_w, ffn_conv_b=ffn_conv_b, ffn_w_down=ffn_w_down)
    m = dict(w_in=m_w_in, mix_pre_norm=m_mix_pre_norm, mix_post_norm=m_mix_post_norm, hg_lb_table=m_hg_lb_table, hg_out_norm=m_hg_out_norm, ssm_conv_w=m_ssm_conv_w, ssm_conv_b=m_ssm_conv_b, ssm_dt_bias=m_ssm_dt_bias, ssm_A_log=m_ssm_A_log, ssm_D=m_ssm_D, ssm_out_norm=m_ssm_out_norm, w_branch_hg=m_w_branch_hg, w_branch_ssm=m_w_branch_ssm, w_out=m_w_out, ffn_pre_norm=m_ffn_pre_norm, ffn_post_norm=m_ffn_post_norm, ffn_w_up=m_ffn_w_up, ffn_conv_w=m_ffn_conv_w, ffn_conv_b=m_ffn_conv_b, ffn_w_down=m_ffn_w_down)
    v = dict(w_in=v_w_in, mix_pre_norm=v_mix_pre_norm, mix_post_norm=v_mix_post_norm, hg_lb_table=v_hg_lb_table, hg_out_norm=v_hg_out_norm, ssm_conv_w=v_ssm_conv_w, ssm_conv_b=v_ssm_conv_b, ssm_dt_bias=v_ssm_dt_bias, ssm_A_log=v_ssm_A_log, ssm_D=v_ssm_D, ssm_out_norm=v_ssm_out_norm, w_branch_hg=v_w_branch_hg, w_branch_ssm=v_w_branch_ssm, w_out=v_w_out, ffn_pre_norm=v_ffn_pre_norm, ffn_post_norm=v_ffn_post_norm, ffn_w_up=v_ffn_w_up, ffn_conv_w=v_ffn_conv_w, ffn_conv_b=v_ffn_conv_b, ffn_w_down=v_ffn_w_down)
    shard = 2 * lax.axis_index("x") + lax.axis_index("y")
    bf = lambda a: a.astype(BF16)

    core = lax.axis_index("c")
    p_in = jnp.concatenate([bf(w_in[0].T), jnp.zeros((IN_ROWS - IN_SHARD, D_MODEL), BF16)], axis=0)
    p_rest = jnp.concatenate([bf(w_branch_hg[0]), bf(w_branch_ssm[0]), bf(w_out[0]), bf(ffn_w_down[0])], axis=0)
    p_up = bf(ffn_w_up[0])
    (g_in,) = gather_weights("gather_w_in", [p_in], [IN_PIECES])
    in_t = _own_slot(g_in, p_in, shard)[:, :IN_SHARD].reshape(IN_TOTAL, D_MODEL)
    wts = dict(in_t=in_t, g_t=in_t[SEG_G:], dt_t=jnp.pad(in_t[SEG_DT:SEG_G], ((0, DT_PAD - SSM_HEADS), (0, 0))))
    conv_cols = max(CONV_SHARD.values())
    padc = lambda a: jnp.pad(a, ((0, 0), (0, conv_cols - a.shape[1])))
    conv_blk = jnp.concatenate([padc(ssm_conv_w[0]), padc(ffn_conv_w[0]), jnp.zeros((1, conv_cols), F32)], axis=0)
    conv_all = gather_small(conv_blk, reduce=False)
    par = {n: w[n] for n in SMALL}
    par["ssm_conv_w"] = jnp.concatenate([conv_all[16 * s:16 * s + SSM_CONV, :CONV_SHARD['ssm_conv_w']] for s in range(N_CHIPS)], axis=1)
    par["ffn_conv_w"] = jnp.concatenate([conv_all[16 * s + SSM_CONV:16 * s + SSM_CONV + FFN_CONV, :CONV_SHARD['ffn_conv_w']]
                                         for s in range(N_CHIPS)], axis=1)

    loss, grad_x, big, small = local_step(x[0], loss_target[0], wts, par, p_rest, p_up, shard, core)
    loss = lax.psum(loss[0, 0], ("x", "y", "c"))

    halves = [big["in_t"], big["rest"], big["up"]]
    wholes = pair_assemble("pair_assemble", halves, [IN_PIECES, REST_PIECES, UP_PIECES])
    f_in, f_rest, f_up = [_own_slot(f.reshape((2,) + r.shape), r, core).reshape(f.shape) for f, r in zip(wholes, halves)]
    r0, r1, r2, r3 = REST_SPLITS
    grads = dict(w_in=f_in[:IN_SHARD].T, w_branch_hg=f_rest[:r0], w_branch_ssm=f_rest[r0:r1], w_out=f_rest[r1:r2],
                 ffn_w_down=f_rest[r2:r3], ffn_w_up=f_up)

    small["hg_out_norm"] = ew_sum("sum_heads", [(small["hg_out_norm"][hd], 0) for hd in range(HG_HEADS)], 1, F32, 1)
    small["ssm_D"] = fold_heads(small["ssm_D"])[:, :SSM_HEADS]
    small["ssm_dt_bias"] = small["ssm_dt_bias"][:, :SSM_HEADS]
    small["ssm_A_log"] = small["ssm_A_log"][:, :SSM_HEADS]
    shapes = [small[n].shape for n in SMALL]
    summed = _unpack(gather_small(_pack([small[n] for n in SMALL]), reduce=True), shapes)
    for n, g in zip(SMALL, summed):
        if n in CONV_SHARD:
            g = lax.dynamic_slice_in_dim(g, shard * CONV_SHARD[n], CONV_SHARD[n], axis=1)
        grads[n] = g

    two_d = lambda a: a.reshape(a.shape[-2], a.shape[-1])
    delta, new_m, new_v = {}, {}, {}
    for n, tr in (("w_in", 64), ("w_branch_hg", 128), ("w_branch_ssm", 128), ("w_out", 128), ("ffn_w_up", 128), ("ffn_w_down", 128)):
        delta[n], new_m[n], new_v[n] = adamw("adamw_" + n, two_d(w[n]), grads[n], two_d(m[n]), two_d(v[n]), tr)
    sm_shapes = [two_d(w[n]).shape for n in SMALL]
    packed = adamw("adamw_small", _pack([two_d(w[n]) for n in SMALL]), _pack([grads[n] for n in SMALL]),
                   _pack([two_d(m[n]) for n in SMALL]), _pack([two_d(v[n]) for n in SMALL]), 1024)
    for res, packed_res in zip((delta, new_m, new_v), packed):
        for n, a in zip(SMALL, _unpack(packed_res, sm_shapes)):
            res[n] = a
    shaped = lambda d: [d[n].reshape(w[n].shape) for n in WEIGHTS]
    return (loss, grad_x[None], *shaped(grads), *shaped(delta), *shaped(new_m), *shaped(new_v))
```

```python
import functools

import jax
import jax.numpy as jnp
import numpy as np
from jax import lax
from jax.experimental import pallas as pl
from jax.experimental.pallas import tpu as pltpu

F32 = jnp.float32
BF16 = jnp.bfloat16

D_MODEL = 2048
EPS = 1e-6
HG_HEADS = 16
HG_DK = 128
HG_CHUNK = 64
HG_SUB = 16
SSM_DINNER = 4096
SSM_HEADDIM = 64
SSM_HEADS = 64
SSM_GROUPS = 8
SSM_DSTATE = 128
SSM_CONV = 4
SSM_CHUNK = 256
SSM_CONV_DIM = 6144
D_FF = 5632
FFN_CONV = 3
DT_PAD = 128

ADAM_LR = 0.001
ADAM_B1 = 0.9
ADAM_B2 = 0.999
ADAM_EPS = 1e-08
ADAM_WD = 0.01
ADAM_STEP = 10

VMEM_LIMIT = 56 * 1024 * 1024
HI = lax.Precision.HIGHEST


def _cp(sem, **kw):
    return pltpu.CompilerParams(dimension_semantics=sem, vmem_limit_bytes=VMEM_LIMIT, **kw)


_DIMS = {"nn": (((1,), (0,)), ((), ())), "nt": (((1,), (1,)), ((), ())), "tn": (((0,), (0,)), ((), ()))}


def mm(a, b, mode, *, name, out_dtype=F32, tm=1024, tn=512, tk=None, acc=None, n_major=True,
       dims=None, a_off=(0, 0), b_off=(0, 0)):
    if dims is not None:
        M, N, K = dims
    else:
        if mode == "nn":
            (M, K), (K2, N) = a.shape, b.shape
        elif mode == "nt":
            (M, K), (N, K2) = a.shape, b.shape
        else:
            (K, M), (K2, N) = a.shape, b.shape
        assert K == K2, (a.shape, b.shape, mode)
    tm, tn = min(tm, M), min(tn, N)
    tk = K if tk is None else min(tk, K)
    assert M % tm == 0 and N % tn == 0 and K % tk == 0, (M, N, K, tm, tn, tk)
    a_blk = (tk, tm) if mode == "tn" else (tm, tk)
    b_blk = (tn, tk) if mode == "nt" else (tk, tn)
    assert all(o % s == 0 for o, s in zip(a_off, a_blk)) and all(o % s == 0 for o, s in zip(b_off, b_blk))
    ao0, ao1 = a_off[0] // a_blk[0], a_off[1] // a_blk[1]
    bo0, bo1 = b_off[0] // b_blk[0], b_off[1] // b_blk[1]
    nk = K // tk
    if n_major:
        grid = (N // tn, M // tm, nk)
        ij = lambda p0, p1: (p1, p0)
    else:
        grid = (M // tm, N // tn, nk)
        ij = lambda p0, p1: (p0, p1)

    def a_map(p0, p1, k):
        i, _ = ij(p0, p1)
        return (k + ao0, i + ao1) if mode == "tn" else (i + ao0, k + ao1)

    def b_map(p0, p1, k):
        _, j = ij(p0, p1)
        return (j + bo0, k + bo1) if mode == "nt" else (k + bo0, j + bo1)

    def o_map(p0, p1, k):
        return ij(p0, p1)

    a_spec = pl.BlockSpec(a_blk, a_map)
    b_spec = pl.BlockSpec(b_blk, b_map)
    o_spec = pl.BlockSpec((tm, tn), o_map)
    dims = _DIMS[mode]
    has_acc = acc is not None

    def body(*refs):
        if has_acc:
            a_ref, b_ref, c_ref, o_ref, acc_ref = refs
        else:
            a_ref, b_ref, o_ref, acc_ref = refs
        k = pl.program_id(2)
        part = lax.dot_general(a_ref[...], b_ref[...], dims, preferred_element_type=F32)
        if nk == 1:
            o_ref[...] = (part + c_ref[...].astype(F32) if has_acc else part).astype(out_dtype)
            return

        @pl.when(k == 0)
        def _():
            acc_ref[...] = part

        @pl.when(k > 0)
        def _():
            acc_ref[...] += part

        @pl.when(k == nk - 1)
        def _():
            r = acc_ref[...]
            if has_acc:
                r = r + c_ref[...].astype(F32)
            o_ref[...] = r.astype(out_dtype)

    in_specs = [a_spec, b_spec] + ([o_spec] if has_acc else [])
    args = (a, b) + ((acc,) if has_acc else ())
    return pl.pallas_call(
        body, name=name, grid=grid, in_specs=in_specs, out_specs=o_spec,
        out_shape=jax.ShapeDtypeStruct((M, N), out_dtype),
        scratch_shapes=[pltpu.VMEM((tm, tn) if nk > 1 else (8, 128), F32)],
        compiler_params=_cp(("parallel", "parallel", "arbitrary")),
    )(*args)


def mm_segments(name, segs, bs, *, tm, tn, tk, acc=None, exchange=None, bt=False):
    m_, n_ = segs[0][0].shape[0], bs[0].shape[0 if bt else 1]
    tm, tn = min(tm, m_), min(tn, n_)
    op = _ChipExchange(exchange) if exchange else None
    ne = (len(exchange) if exchange else 0)
    na = 0 if acc is None else 1
    steps, k0 = [], 0
    for a, bi, row in segs:
        w = a.shape[1]
        tks = min(tk, w)
        assert w % tks == 0 and row % tks == 0 and tks == min(tk, bs[bi].shape[1 if bt else 0]), (w, row, tks)
        steps.append((k0, w // tks, tks, bi, row // tks))
        k0 += w // tks
    nk = k0
    assert m_ % tm == 0 and n_ % tn == 0

    def a_spec(k_first, count, tks):
        return pl.BlockSpec((tm, tks), lambda j, i, k: (i, jnp.clip(k - k_first, 0, count - 1)))

    def b_spec(bi):
        mine = [s for s in steps if s[3] == bi]

        def index(j, i, k):
            blk = mine[0][4]
            for k_first, count, _, _, first_blk in mine:
                blk = jnp.where(k >= k_first, first_blk + jnp.minimum(k - k_first, count - 1), blk)
            return (j, blk) if bt else (blk, j)
        return pl.BlockSpec((tn, mine[0][2]) if bt else (mine[0][2], tn), index)

    ns = len(segs)

    nb = len(bs)
    grid = (n_ // tn, m_ // tm, nk)

    def body(*refs):
        a_refs, b_refs = refs[:ns], refs[ns:ns + nb]
        acc_in = refs[ns + nb] if na else None
        rest = refs[ns + nb + na:]
        ex_refs, o_ref, got_refs, acc_ref, sems = rest[:ne], rest[ne], rest[ne + 1:2 * ne + 1], rest[2 * ne + 1], rest[2 * ne + 2:]
        k = pl.program_id(2)
        if op:
            first = (pl.program_id(0) == 0) & (pl.program_id(1) == 0) & (k == 0)
            last = (pl.program_id(0) == grid[0] - 1) & (pl.program_id(1) == grid[1] - 1) & (k == nk - 1)

            @pl.when(first)
            def _():
                op.start(ex_refs, got_refs, *sems)

            @pl.when(last)
            def _():
                op.finish(ex_refs, got_refs, *sems)

        @pl.when(k == 0)
        def _():
            acc_ref[...] = acc_in[...] if na else jnp.zeros_like(acc_ref)

        for a_ref, (k_first, count, _, bi, _) in zip(a_refs, steps):
            @pl.when((k >= k_first) & (k < k_first + count))
            def _(a_ref=a_ref, bi=bi):
                acc_ref[...] += lax.dot_general(a_ref[...], b_refs[bi][...], _DIMS["nt" if bt else "nn"],
                                                preferred_element_type=F32)

        @pl.when(k == nk - 1)
        def _():
            o_ref[...] = acc_ref[...]

    any_spec = pl.BlockSpec(memory_space=pl.ANY)
    o_spec = pl.BlockSpec((tm, tn), lambda j, i, k: (i, j))
    outs = pl.pallas_call(
        body, name=name, grid=grid,
        in_specs=[a_spec(s[0], s[1], s[2]) for s in steps] + [b_spec(bi) for bi in range(nb)] + [o_spec] * na + [any_spec] * ne,
        out_specs=[o_spec] + [any_spec] * ne,
        out_shape=[jax.ShapeDtypeStruct((m_, n_), F32)] + (op.out_shape if op else []),
        scratch_shapes=[pltpu.VMEM((tm, tn), F32)] + (_sem_pair(op.n_sem) if op else []),
        compiler_params=_cp(("arbitrary", "arbitrary", "arbitrary")),
    )(*[a for a, _, _ in segs], *bs, *(() if acc is None else (acc,)), *(exchange or ()))
    return outs if op else outs[0]


def _dims(mode, ndim):
    if ndim == 2:
        return _DIMS[mode]
    (ca,), (cb,) = _DIMS[mode][0]
    return (((ca + 1,), (cb + 1,)), ((0,), (0,)))


def _bdot_plain(a, b, mode):
    return lax.dot_general(a.astype(BF16), b.astype(BF16), _dims(mode, a.ndim), preferred_element_type=F32)


@functools.partial(jax.custom_vjp, nondiff_argnums=(2,))
def _bdot_vjp(a, b, mode):
    return _bdot_plain(a, b, mode)


def _bdot_fwd(a, b, mode):
    return _bdot_plain(a, b, mode), (a, b)


def _bdot_bwd(mode, res, g):
    a, b = res
    if mode == "nn":
        return _bdot_plain(g, b, "nt"), _bdot_plain(a, g, "tn")
    if mode == "nt":
        return _bdot_plain(g, b, "nn"), _bdot_plain(g, a, "tn")
    return _bdot_plain(b, g, "nt"), _bdot_plain(a, g, "nn")


_bdot_vjp.defvjp(_bdot_fwd, _bdot_bwd)


def _split3(x):
    x1 = x.astype(BF16)
    r1 = x - x1.astype(F32)
    x2 = r1.astype(BF16)
    return x1, x2, (r1 - x2.astype(F32)).astype(BF16)


def _hdot_impl(a, b, mode, data):
    dims = _dims(mode, a.ndim)
    if data == "a":
        sel = b.astype(BF16)
        parts = [lax.dot_general(p, sel, dims, preferred_element_type=F32) for p in _split3(a)]
    else:
        sel = a.astype(BF16)
        parts = [lax.dot_general(sel, p, dims, preferred_element_type=F32) for p in _split3(b)]
    return (parts[2] + parts[1]) + parts[0]


@functools.partial(jax.custom_vjp, nondiff_argnums=(2, 3))
def _hdot(a, b, mode="nn", data="b"):
    return _hdot_impl(a, b, mode, data)


def _hdot_fwd(a, b, mode, data):
    return _hdot_impl(a, b, mode, data), (a, b)


def _hdot_bwd(mode, data, res, g):
    a, b = res
    if data == "a":
        da = {"nn": lambda: _hdot_impl(g, b, "nt", "a"), "nt": lambda: _hdot_impl(g, b, "nn", "a"),
              "tn": lambda: _hdot_impl(b, g, "nt", "b")}[mode]()
        return da, jnp.zeros_like(b)
    db = {"nn": lambda: _hdot_impl(a, g, "tn", "b"), "nt": lambda: _hdot_impl(g, a, "tn", "a"),
          "tn": lambda: _hdot_impl(a, g, "nn", "b")}[mode]()
    return jnp.zeros_like(a), db


_hdot.defvjp(_hdot_fwd, _hdot_bwd)


def _sigmoid(x):
    return 1.0 / (1.0 + jnp.exp(-x))


def _silu(x):
    return x * _sigmoid(x)


def _iota(shape, dim):
    return lax.broadcasted_iota(jnp.int32, shape, dim)


def _rms(x, w):
    return x * lax.rsqrt(jnp.mean(x * x, axis=-1, keepdims=True) + EPS) * w


def _hg_chunk(q_raw, f_raw, v, g, st, t0, t1, nw, dot):
    nhd, c = q_raw.shape[0], q_raw.shape[1]
    m = jnp.maximum(t0, t1)
    e0, e1 = jnp.exp(t0 - m), jnp.exp(t1 - m)
    lb = e0 / (e0 + e1)
    f = lb + (1.0 - lb) * _sigmoid(f_raw)
    k = 1.0 - f
    lf = jnp.log(f)
    qh = _silu(q_raw) * (HG_DK ** -0.5)
    row, col = _iota((c, c), 0), _iota((c, c), 1)
    causal = col <= row
    tril = jnp.broadcast_to(jnp.where(causal, 1.0, 0.0).astype(F32), (nhd, c, c))
    trilb = jnp.broadcast_to(jnp.where(causal & (col // HG_SUB == row // HG_SUB), 1.0, 0.0).astype(F32), (nhd, c, c))
    b = _hdot(tril, lf)
    bl = _hdot(trilb, lf)
    a_row = b - bl
    rid = _iota((c, HG_DK), 0)
    qt = qh * jnp.exp(bl)
    kt = k * jnp.exp(-bl)
    scores = jnp.zeros((nhd, c, c), F32)
    for j in range(c // HG_SUB):
        if j == 0:
            qj = qt * jnp.exp(a_row)
        else:
            a_j = jnp.sum(jnp.where(rid == j * HG_SUB - 1, b, 0.0), axis=1, keepdims=True)
            qj = qt * jnp.exp(jnp.where(rid // HG_SUB >= j, a_row - a_j, -1e30))
        kj = jnp.where(rid // HG_SUB == j, kt, 0.0)
        scores = scores + dot(qj, kj, "nt")
    scores = jnp.where(causal, scores, 0.0)
    o = dot(scores, v, "nn") + dot(qh * jnp.exp(b), st, "nt")
    b_last = jnp.sum(jnp.where(rid == c - 1, b, 0.0), axis=1, keepdims=True)
    st_new = st * jnp.exp(b_last) + dot(v, k * jnp.exp(b_last - b), "tn")
    y = _rms(o, nw) * _silu(g)
    return y, st_new


HG_HPS = 16
HG_W = HG_HPS * HG_DK


def hgrn2_fwd(qfig, table, nw, *, step_chunks=2, gather=None):
    t = qfig.shape[0]
    rows = HG_CHUNK * step_chunks
    nsteps = t // rows
    nh = HG_HEADS // HG_HPS
    op = _Gather(*gather) if gather else None
    ng = len(gather[0]) if gather else 0

    def body(*refs):
        q_ref, f_ref, v_ref, g_ref, tab_ref, nw_ref = refs[:6]
        p_refs = refs[6:6 + ng]
        y_ref, s_ref = refs[6 + ng:8 + ng]
        got_refs = refs[8 + ng:8 + 2 * ng]
        st_scr = refs[8 + 2 * ng]
        sems = refs[9 + 2 * ng:]
        first_step = (pl.program_id(0) == 0) & (pl.program_id(1) == 0)
        last_step = (pl.program_id(0) == nh - 1) & (pl.program_id(1) == nsteps - 1)
        if op:
            @pl.when(first_step)
            def _():
                op.start(p_refs, got_refs, *sems)

        @pl.when(pl.program_id(1) == 0)
        def _():
            st_scr[...] = jnp.zeros_like(st_scr)

        nwv = nw_ref[...]
        lanes = [pl.ds(hh * HG_DK, HG_DK) for hh in range(HG_HPS)]
        t0 = jnp.stack([tab_ref[0:1, ln] for ln in lanes])
        t1 = jnp.stack([tab_ref[1:2, ln] for ln in lanes])
        for c in range(step_chunks):
            sl = pl.ds(c * HG_CHUNK, HG_CHUNK)
            heads = lambda ref: jnp.stack([ref[sl, ln] for ln in lanes])
            st = st_scr[...]
            for hh in range(HG_HPS):
                s_ref[hh, c] = st[hh]
            y, st_new = _hg_chunk(heads(q_ref), heads(f_ref), heads(v_ref), heads(g_ref), st, t0, t1, nwv, _bdot_vjp)
            for hh, ln in enumerate(lanes):
                y_ref[sl, ln] = y[hh].astype(BF16)
            st_scr[...] = st_new

        if op:
            @pl.when(last_step)
            def _():
                op.finish(p_refs, got_refs, *sems)

    blk = lambda off: pl.BlockSpec((rows, HG_W), lambda h, c, off=off: (c, off + h))
    return pl.pallas_call(
        body, name="hgrn2_fwd", grid=(nh, nsteps),
        in_specs=[blk(0), blk(nh), blk(2 * nh), blk(3 * nh),
                  pl.BlockSpec((2, HG_W), lambda h, c: (0, h)), pl.BlockSpec((1, HG_DK), lambda h, c: (0, 0))] + [ANY] * ng,
        out_specs=[pl.BlockSpec((rows, HG_W), lambda h, c: (c, h)),
                   pl.BlockSpec((HG_HPS, step_chunks, HG_DK, HG_DK), lambda h, c: (h, c, 0, 0))] + [ANY] * ng,
        out_shape=[jax.ShapeDtypeStruct((t, HG_HEADS * HG_DK), BF16),
                   jax.ShapeDtypeStruct((HG_HEADS, t // HG_CHUNK, HG_DK, HG_DK), F32)] + (op.out_shape if op else []),
        scratch_shapes=[pltpu.VMEM((HG_HPS, HG_DK, HG_DK), F32)] + (_sem_pair(op.n_sem) if op else []),
        compiler_params=_cp(("arbitrary", "arbitrary")),
    )(qfig, qfig, qfig, qfig, table, nw, *(gather[0] if gather else ()))


def hgrn2_bwd(qfig, table, nw, states, dy, *, step_chunks=2):
    t = qfig.shape[0]
    rows = HG_CHUNK * step_chunks
    nsteps = t // rows
    nh = HG_HEADS // HG_HPS

    def body(q_ref, f_ref, v_ref, g_ref, tab_ref, nw_ref, s_ref, dy_ref,
             dq_ref, df_ref, dv_ref, dg_ref, dtab_ref, dnw_ref, dst_scr):
        @pl.when(pl.program_id(1) == 0)
        def _():
            dst_scr[...] = jnp.zeros_like(dst_scr)
            dtab_ref[...] = jnp.zeros_like(dtab_ref)
            dnw_ref[...] = jnp.zeros_like(dnw_ref)

        nwv = nw_ref[...]
        fn = functools.partial(_hg_chunk, dot=_bdot_vjp)
        lanes = [pl.ds(hh * HG_DK, HG_DK) for hh in range(HG_HPS)]
        t0 = jnp.stack([tab_ref[0:1, ln] for ln in lanes])
        t1 = jnp.stack([tab_ref[1:2, ln] for ln in lanes])
        for c in reversed(range(step_chunks)):
            sl = pl.ds(c * HG_CHUNK, HG_CHUNK)
            heads = lambda ref: jnp.stack([ref[sl, ln] for ln in lanes])
            _, vjp = jax.vjp(fn, heads(q_ref), heads(f_ref), heads(v_ref), heads(g_ref), s_ref[:, c], t0, t1, nwv)
            dq, df, dv, dg, dst, dt0, dt1, dnw = vjp((heads(dy_ref).astype(F32), dst_scr[...]))
            for hh, ln in enumerate(lanes):
                dq_ref[sl, ln] = dq[hh].astype(BF16)
                df_ref[sl, ln] = df[hh].astype(BF16)
                dv_ref[sl, ln] = dv[hh].astype(BF16)
                dg_ref[sl, ln] = dg[hh].astype(BF16)
                dtab_ref[0:1, ln] += dt0[hh]
                dtab_ref[1:2, ln] += dt1[hh]
            dst_scr[...] = dst
            dnw_ref[0] += dnw

    rev = lambda c: nsteps - 1 - c
    blk = lambda off: pl.BlockSpec((rows, HG_W), lambda h, c, off=off: (rev(c), off + h))
    oblk = lambda: pl.BlockSpec((rows, HG_W), lambda h, c: (rev(c), h))
    d = HG_HEADS * HG_DK
    outs = pl.pallas_call(
        body, name="hgrn2_bwd", grid=(nh, nsteps),
        in_specs=[blk(0), blk(nh), blk(2 * nh), blk(3 * nh),
                  pl.BlockSpec((2, HG_W), lambda h, c: (0, h)), pl.BlockSpec((1, HG_DK), lambda h, c: (0, 0)),
                  pl.BlockSpec((HG_HPS, step_chunks, HG_DK, HG_DK), lambda h, c: (h, rev(c), 0, 0)),
                  pl.BlockSpec((rows, HG_W), lambda h, c: (rev(c), h))],
        out_specs=[oblk(), oblk(), oblk(), oblk(),
                   pl.BlockSpec((2, HG_W), lambda h, c: (0, h)),
                   pl.BlockSpec((HG_HPS, 1, HG_DK), lambda h, c: (h, 0, 0))],
        out_shape=[jax.ShapeDtypeStruct((t, d), BF16)] * 4
        + [jax.ShapeDtypeStruct((2, d), F32), jax.ShapeDtypeStruct((HG_HEADS, 1, HG_DK), F32)],
        scratch_shapes=[pltpu.VMEM((HG_HPS, HG_DK, HG_DK), F32)],
        compiler_params=_cp(("parallel", "arbitrary")),
    )(qfig, qfig, qfig, qfig, table, nw, states, dy)
    return outs


def _ssd_chunk(xs2, dt, acum, bm, cm, s2, pair0, dot):
    npr, c = xs2.shape[0], xs2.shape[1]
    sh_e, sh_s = (npr, DT_PAD, 128), (npr, 8, DT_PAD)
    first_head = 2 * (pair0 + _iota(sh_e, 0))
    expand = jnp.where(_iota(sh_e, 1) == first_head + _iota(sh_e, 2) // SSM_HEADDIM, 1.0, 0.0).astype(F32)
    sel = (_iota(sh_s, 2) == 2 * (pair0 + _iota(sh_s, 0)) + _iota(sh_s, 1)) & (_iota(sh_s, 1) < 2)
    sel = jnp.where(sel, 1.0, 0.0).astype(F32)
    per_pair = lambda a: jnp.broadcast_to(a, (npr,) + a.shape)
    dtx = _hdot(per_pair(dt), expand, "nn", "a")
    acol = _hdot(per_pair(acum), expand, "nn", "a")
    arow8 = _hdot(sel, per_pair(acum), "nt", "b")
    row, col = _iota((c, c), 0), _iota((c, c), 1)
    causal = col <= row
    cb = dot(cm, bm, "nt")
    x2 = xs2 * dtx
    lane_c = _iota((c, 128), 1)
    y = dot(per_pair(cm), s2, "nn") * jnp.exp(acol)
    for r in range(2):
        head = (lane_c // SSM_HEADDIM) == r
        a_c = jnp.sum(jnp.where(head & (lane_c % SSM_HEADDIM == 0), acol, 0.0), axis=2, keepdims=True)
        a_r = jnp.sum(jnp.where(_iota((8, c), 0) == r, arow8, 0.0), axis=1, keepdims=True)
        decay = jnp.exp(jnp.where(causal, a_c - a_r, -1e30))
        y = y + dot(cb * decay, jnp.where(head, x2, 0.0), "nn")
    a_last = jnp.sum(jnp.where(_iota((c, 128), 0) == c - 1, acol, 0.0), axis=1, keepdims=True)
    s2_new = s2 * jnp.exp(a_last) + dot(per_pair(bm), x2 * jnp.exp(a_last - acol), "tn")
    return y, s2_new


SSM_PAIRS = SSM_HEADS // 2
PAIRS_PER_GROUP = SSM_PAIRS // SSM_GROUPS
SSD_PPS = 4
SSD_W = SSD_PPS * 128
_XS_BLOCKS = SSM_DINNER // 128
_B_BLOCK0 = _XS_BLOCKS
_C_BLOCK0 = _XS_BLOCKS + SSM_GROUPS


def ssd_fwd(xbc_act, dt, acum, *, gather=None):
    t = xbc_act.shape[0]
    nc = t // SSM_CHUNK
    c_ = SSM_CHUNK
    nq = SSM_PAIRS // SSD_PPS
    op = _Gather(*gather) if gather else None
    ng = len(gather[0]) if gather else 0

    def body(*refs):
        xs_ref, b_ref, c_ref, dt_ref, ac_ref = refs[:5]
        p_refs = refs[5:5 + ng]
        y_ref, s_ref = refs[5 + ng:7 + ng]
        got_refs = refs[7 + ng:7 + 2 * ng]
        s_scr = refs[7 + 2 * ng]
        sems = refs[8 + 2 * ng:]
        q = pl.program_id(1)
        if op:
            @pl.when((pl.program_id(0) == 0) & (q == 0))
            def _():
                op.start(p_refs, got_refs, *sems)

            @pl.when((pl.program_id(0) == nc - 1) & (q == nq - 1))
            def _():
                op.finish(p_refs, got_refs, *sems)

        mine = pl.ds(SSD_PPS * q, SSD_PPS)
        lanes = [pl.ds(r * 128, 128) for r in range(SSD_PPS)]

        @pl.when(pl.program_id(0) == 0)
        def _():
            s_scr[mine] = jnp.zeros((SSD_PPS, SSM_DSTATE, 128), F32)

        s2 = s_scr[mine]
        s_ref[...] = s2
        xs = jnp.stack([xs_ref[:, ln] for ln in lanes])
        y, s2_new = _ssd_chunk(xs, dt_ref[...], ac_ref[...], b_ref[...], c_ref[...], s2, SSD_PPS * q, _bdot_vjp)
        for r, ln in enumerate(lanes):
            y_ref[:, ln] = y[r]
        s_scr[mine] = s2_new

    grp = lambda q: q // (PAIRS_PER_GROUP // SSD_PPS)
    return pl.pallas_call(
        body, name="ssd_fwd", grid=(nc, SSM_PAIRS // SSD_PPS),
        in_specs=[pl.BlockSpec((c_, SSD_W), lambda c, q: (c, q)),
                  pl.BlockSpec((c_, 128), lambda c, q: (c, _B_BLOCK0 + grp(q))),
                  pl.BlockSpec((c_, 128), lambda c, q: (c, _C_BLOCK0 + grp(q))),
                  pl.BlockSpec((c_, DT_PAD), lambda c, q: (c, 0)),
                  pl.BlockSpec((c_, DT_PAD), lambda c, q: (c, 0))] + [ANY] * ng,
        out_specs=[pl.BlockSpec((c_, SSD_W), lambda c, q: (c, q)),
                   pl.BlockSpec((None, SSD_PPS, SSM_DSTATE, 128), lambda c, q: (c, q, 0, 0))] + [ANY] * ng,
        out_shape=[jax.ShapeDtypeStruct((t, SSM_DINNER), F32),
                   jax.ShapeDtypeStruct((nc, SSM_PAIRS, SSM_DSTATE, 128), F32)] + (op.out_shape if op else []),
        scratch_shapes=[pltpu.VMEM((SSM_PAIRS, SSM_DSTATE, 128), F32)] + (_sem_pair(op.n_sem) if op else []),
        compiler_params=_cp(("arbitrary", "arbitrary")),
    )(xbc_act, xbc_act, xbc_act, dt, acum, *(gather[0] if gather else ()))


def ssd_bwd(xbc_act, dt, acum, states, dy, dskip, *, exchange=None):
    t = xbc_act.shape[0]
    nc = t // SSM_CHUNK
    c_ = SSM_CHUNK
    rev = lambda c: nc - 1 - c
    nq = SSM_PAIRS // SSD_PPS
    op = _ChipExchange(exchange) if exchange else None
    ne = len(exchange) if exchange else 0

    def body(*refs):
        xs_ref, b_ref, c_ref, dt_ref, ac_ref, s_ref, dy_ref, sk_ref = refs[:8]
        ex_refs = refs[8:8 + ne]
        dxs_ref, db_ref, dc_ref, ddt_ref, dac_ref = refs[8 + ne:13 + ne]
        got_refs = refs[13 + ne:13 + 2 * ne]
        ds_scr = refs[13 + 2 * ne]
        sems = refs[14 + 2 * ne:]
        q = pl.program_id(1)
        if op:
            @pl.when((pl.program_id(0) == 0) & (q == 0))
            def _():
                op.start(ex_refs, got_refs, *sems)

            @pl.when((pl.program_id(0) == nc - 1) & (q == nq - 1))
            def _():
                op.finish(ex_refs, got_refs, *sems)

        assert SSD_PPS == PAIRS_PER_GROUP
        mine = pl.ds(SSD_PPS * q, SSD_PPS)
        lanes = [pl.ds(r * 128, 128) for r in range(SSD_PPS)]

        @pl.when(pl.program_id(0) == 0)
        def _():
            ds_scr[mine] = jnp.zeros((SSD_PPS, SSM_DSTATE, 128), F32)

        fn = functools.partial(_ssd_chunk, pair0=SSD_PPS * q, dot=_bdot_vjp)
        xs = jnp.stack([xs_ref[:, ln] for ln in lanes])
        dy = jnp.stack([dy_ref[:, ln] for ln in lanes])
        _, vjp = jax.vjp(fn, xs, dt_ref[...], ac_ref[...], b_ref[...], c_ref[...], s_ref[...])
        dxs, ddt, dac, db, dc, ds = vjp((dy, ds_scr[mine]))
        for r, ln in enumerate(lanes):
            dxs_ref[:, ln] = dxs[r] + sk_ref[:, ln]
        ds_scr[mine] = ds
        db_ref[...] = db
        dc_ref[...] = dc

        @pl.when(q == 0)
        def _():
            ddt_ref[...] = ddt
            dac_ref[...] = dac

        @pl.when(q != 0)
        def _():
            ddt_ref[...] += ddt
            dac_ref[...] += dac

    grp = lambda q: q // (PAIRS_PER_GROUP // SSD_PPS)
    return pl.pallas_call(
        body, name="ssd_bwd", grid=(nc, SSM_PAIRS // SSD_PPS),
        in_specs=[pl.BlockSpec((c_, SSD_W), lambda c, q: (rev(c), q)),
                  pl.BlockSpec((c_, 128), lambda c, q: (rev(c), _B_BLOCK0 + grp(q))),
                  pl.BlockSpec((c_, 128), lambda c, q: (rev(c), _C_BLOCK0 + grp(q))),
                  pl.BlockSpec((c_, DT_PAD), lambda c, q: (rev(c), 0)),
                  pl.BlockSpec((c_, DT_PAD), lambda c, q: (rev(c), 0)),
                  pl.BlockSpec((None, SSD_PPS, SSM_DSTATE, 128), lambda c, q: (rev(c), q, 0, 0)),
                  pl.BlockSpec((c_, SSD_W), lambda c, q: (rev(c), q)),
                  pl.BlockSpec((c_, SSD_W), lambda c, q: (rev(c), q))] + [ANY] * ne,
        out_specs=[pl.BlockSpec((c_, SSD_W), lambda c, q: (rev(c), q)),
                   pl.BlockSpec((c_, 128), lambda c, q: (rev(c), grp(q))),
                   pl.BlockSpec((c_, 128), lambda c, q: (rev(c), grp(q))),
                   pl.BlockSpec((c_, DT_PAD), lambda c, q: (rev(c), 0)),
                   pl.BlockSpec((c_, DT_PAD), lambda c, q: (rev(c), 0))] + [ANY] * ne,
        out_shape=[jax.ShapeDtypeStruct((t, SSM_DINNER), F32),
                   jax.ShapeDtypeStruct((t, SSM_GROUPS * SSM_DSTATE), F32),
                   jax.ShapeDtypeStruct((t, SSM_GROUPS * SSM_DSTATE), F32),
                   jax.ShapeDtypeStruct((t, DT_PAD), F32),
                   jax.ShapeDtypeStruct((t, DT_PAD), F32)] + (op.out_shape if op else []),
        scratch_shapes=[pltpu.VMEM((SSM_PAIRS, SSM_DSTATE, 128), F32)] + (_sem_pair(op.n_sem) if op else []),
        compiler_params=_cp(("arbitrary", "arbitrary")),
    )(xbc_act, xbc_act, xbc_act, dt, acum, states, dy, dskip, *(exchange or ()))


def rowwise(name, fn, row_ins, par_ins, row_outs, acc_outs, *, tt, ncb=1):
    t = row_ins[0][0].shape[0]
    assert t % tt == 0
    n_ri, n_pi, n_ro, n_ao = len(row_ins), len(par_ins), len(row_outs), len(acc_outs)

    def body(*refs):
        i = pl.program_id(1)
        ins = [r[...] for r in refs[:n_ri + n_pi]]
        outs = fn(*ins)
        ro_refs = refs[n_ri + n_pi:n_ri + n_pi + n_ro]
        ao_refs = refs[n_ri + n_pi + n_ro:]
        for r, v in zip(ro_refs, outs[:n_ro]):
            r[...] = v.astype(r.dtype)
        for r, v in zip(ao_refs, outs[n_ro:]):
            @pl.when(i == 0)
            def _(r=r, v=v):
                r[...] = v

            @pl.when(i > 0)
            def _(r=r, v=v):
                r[...] += v

    in_specs = [pl.BlockSpec((tt, bc), lambda j, i, off=off: (i, off + j)) for _, bc, off in row_ins]
    in_specs += [pl.BlockSpec((a.shape[0], bc), lambda j, i, off=off: (0, off + j)) for a, bc, off in par_ins]
    out_specs = [pl.BlockSpec((tt, bc), lambda j, i: (i, j)) for _, bc, _ in row_outs]
    out_specs += [pl.BlockSpec((r, bc), lambda j, i: (0, j)) for r, _, bc in acc_outs]
    out_shape = [jax.ShapeDtypeStruct((t, c), dt) for c, _, dt in row_outs]
    out_shape += [jax.ShapeDtypeStruct((r, c), F32) for r, c, _ in acc_outs]
    return pl.pallas_call(
        body, name=name, grid=(ncb, t // tt), in_specs=in_specs, out_specs=out_specs, out_shape=out_shape,
        compiler_params=_cp(("parallel", "arbitrary")),
    )(*[a for a, _, _ in row_ins], *[a for a, _, _ in par_ins])


def _colsum(v):
    return jnp.sum(v, axis=0, keepdims=True)


def _softplus(x):
    return jnp.maximum(x, 0.0) + jnp.log(1.0 + jnp.exp(-jnp.abs(x)))


def _gelu_tanh(x):
    return 0.5 * x * (1.0 + jnp.tanh(0.7978845608028654 * (x + 0.044715 * (x * x * x))))


D = D_MODEL


def norm_fwd(x, w):
    return rowwise("norm_fwd", lambda xv, wv: (_rms(xv, wv),), [(x, D, 0)], [(w, D, 0)], [(D, D, BF16)], [], tt=256)[0]


def norm_bwd(x, w, dh, dres):
    def fn(xv, dhv, drv, wv):
        _, vjp = jax.vjp(_rms, xv, wv)
        dx, dw = vjp(dhv)
        return dx + drv, dw
    return rowwise("norm_bwd", fn, [(x, D, 0), (dh, D, 0), (dres, D, 0)], [(w, D, 0)], [(D, D, F32)], [(1, D, D)], tt=256)


def _dt_fn(dtr, bias, a_log):
    c = dtr.shape[0]
    dt = _softplus(dtr + bias)
    da = dt * (-jnp.exp(a_log))
    tril = jnp.where(_iota((c, c), 1) <= _iota((c, c), 0), 1.0, 0.0).astype(F32)
    return dt, _hdot(tril, da)


def dt_fwd(dtr, bias, a_log):
    return rowwise("dt_fwd", _dt_fn, [(dtr, DT_PAD, 0)], [(bias, DT_PAD, 0), (a_log, DT_PAD, 0)],
                   [(DT_PAD, DT_PAD, F32), (DT_PAD, DT_PAD, F32)], [], tt=SSM_CHUNK)


def dt_bwd(dtr, bias, a_log, ddt, dacum):
    def fn(dtrv, ddtv, dacv, bv, av):
        _, vjp = jax.vjp(_dt_fn, dtrv, bv, av)
        return vjp((ddtv, dacv))
    return rowwise("dt_bwd", fn, [(dtr, DT_PAD, 0), (ddt, DT_PAD, 0), (dacum, DT_PAD, 0)],
                   [(bias, DT_PAD, 0), (a_log, DT_PAD, 0)],
                   [(DT_PAD, DT_PAD, BF16)], [(1, DT_PAD, DT_PAD), (1, DT_PAD, DT_PAD)], tt=SSM_CHUNK)


GROUP_W = SSM_DINNER // SSM_GROUPS


def _ssm_post_fn(yv, xsv, zv, dexp, nw):
    return _rms((yv + dexp * xsv) * _silu(zv), nw)


def ssm_post_fwd(yssd, xbc_act, z, dexp, nw):
    return rowwise("ssm_post_fwd", lambda *a: (_ssm_post_fn(*a),),
                   [(yssd, GROUP_W, 0), (xbc_act, GROUP_W, 0), (z, GROUP_W, 0)], [(dexp, GROUP_W, 0), (nw, GROUP_W, 0)],
                   [(SSM_DINNER, GROUP_W, BF16)], [], tt=512, ncb=SSM_GROUPS)[0]


def ssm_post_bwd(yssd, xbc_act, z, dexp, nw, dy):
    def fn(yv, xsv, zv, dyv, dv, nv):
        _, vjp = jax.vjp(_ssm_post_fn, yv, xsv, zv, dv, nv)
        return vjp(dyv.astype(F32))
    return rowwise("ssm_post_bwd", fn,
                   [(yssd, GROUP_W, 0), (xbc_act, GROUP_W, 0), (z, GROUP_W, 0), (dy, GROUP_W, 0)],
                   [(dexp, GROUP_W, 0), (nw, GROUP_W, 0)],
                   [(SSM_DINNER, GROUP_W, F32), (SSM_DINNER, GROUP_W, F32), (SSM_DINNER, GROUP_W, BF16)],
                   [(1, SSM_DINNER, GROUP_W), (1, SSM_DINNER, GROUP_W)], tt=512, ncb=SSM_GROUPS)


def _merge_fn(ah, asm, gh, gs):
    return _sigmoid(gh) * ah + _sigmoid(gs) * asm


def merge_fwd(a_hg, a_ssm, gates):
    f32 = lambda vals: [v.astype(F32) for v in vals]
    return rowwise("merge_fwd", lambda *a: (_merge_fn(*f32(a)),), [(a_hg, D, 0), (a_ssm, D, 0), (gates, D, 0), (gates, D, 1)], [],
                   [(D, D, BF16)], [], tt=256)[0]


def merge_bwd(a_hg, a_ssm, gates, dmixed):
    def fn(ah, asm, gh, gs, dm):
        _, vjp = jax.vjp(_merge_fn, *[v.astype(F32) for v in (ah, asm, gh, gs)])
        return vjp(dm.astype(F32))
    return rowwise("merge_bwd", fn, [(a_hg, D, 0), (a_ssm, D, 0), (gates, D, 0), (gates, D, 1), (dmixed, D, 0)], [],
                   [(D, D, BF16)] * 4, [], tt=256)


def _post1_fn(xv, uv, wpost, wpre):
    x1 = xv + _rms(uv, wpost)
    return x1, _rms(x1, wpre)


def post1_fwd(x, u, wpost, wpre):
    return rowwise("post1_fwd", _post1_fn, [(x, D, 0), (u, D, 0)], [(wpost, D, 0), (wpre, D, 0)],
                   [(D, D, F32), (D, D, BF16)], [], tt=256)


def post1_bwd(x, u, wpost, wpre, dx1, dh2):
    def fn(xv, uv, d1, d2, wa, wb):
        _, vjp = jax.vjp(_post1_fn, xv, uv, wa, wb)
        dx, du, dwa, dwb = vjp((d1, d2))
        return du, dx, dwa, dwb
    return rowwise("post1_bwd", fn, [(x, D, 0), (u, D, 0), (dx1, D, 0), (dh2, D, 0)], [(wpost, D, 0), (wpre, D, 0)],
                   [(D, D, BF16), (D, D, F32)], [(1, D, D), (1, D, D)], tt=256)


def final_fwd_bwd(x1, fo, w, target):
    def fn(x1v, fov, tv, wv):
        def loss_fn(a, b, c):
            err = a + _rms(b, c) - tv
            return 0.5 * jnp.sum(err * err) * (1.0 / D)
        loss, vjp = jax.vjp(loss_fn, x1v, fov, wv)
        dx, dfo, dw = vjp(jnp.ones((), F32))
        return dx, dfo, dw, jnp.full((1, 128), loss, F32)
    return rowwise("final_fwd_bwd", fn, [(x1, D, 0), (fo, D, 0), (target, D, 0)], [(w, D, 0)],
                   [(D, D, F32), (D, D, BF16)], [(1, D, D), (1, 128, 128)], tt=256)


HALO = 8
CONV_TT = 512
CONV_CB = 512
CONV_RB = 32


def _tail(kind, c, up):
    return _silu(c) if kind == "silu" else _gelu_tanh(c) * up


def conv_fwd(name, x, xoff, w, b, kind, up=None, upoff=0, act_dtype=F32):
    t = x.shape[0]
    k_, c_ = w.shape
    tt, cb = CONV_TT, CONV_CB
    hb = tt // HALO
    has_up = up is not None

    def body(*refs):
        if has_up:
            x_ref, xp_ref, w_ref, b_ref, up_ref, c_ref, a_ref, scr = refs
        else:
            x_ref, xp_ref, w_ref, b_ref, c_ref, a_ref, scr = refs
        i = pl.program_id(1)
        scr[0:HALO, :] = jnp.where(i == 0, 0.0, xp_ref[...])
        scr[HALO:HALO + tt, :] = x_ref[...]
        for r in range(tt // CONV_RB):
            rows = pl.ds(r * CONV_RB, CONV_RB)
            acc = jnp.zeros((CONV_RB, cb), F32) + b_ref[...]
            for k in range(k_):
                acc = acc + w_ref[k:k + 1, :] * scr[pl.ds(r * CONV_RB + HALO - (k_ - 1) + k, CONV_RB), :]
            c_ref[rows, :] = acc
            a_ref[rows, :] = _tail(kind, acc, up_ref[rows, :] if has_up else None).astype(act_dtype)

    in_specs = [pl.BlockSpec((tt, cb), lambda j, i: (i, xoff + j)),
                pl.BlockSpec((HALO, cb), lambda j, i: (jnp.maximum(i * hb - 1, 0), xoff + j)),
                pl.BlockSpec((k_, cb), lambda j, i: (0, j)),
                pl.BlockSpec((1, cb), lambda j, i: (0, j))]
    args = [x, x, w, b]
    if has_up:
        in_specs.append(pl.BlockSpec((tt, cb), lambda j, i: (i, upoff + j)))
        args.append(up)
    return pl.pallas_call(
        body, name=name, grid=(c_ // cb, t // tt), in_specs=in_specs,
        out_specs=[pl.BlockSpec((tt, cb), lambda j, i: (i, j))] * 2,
        out_shape=[jax.ShapeDtypeStruct((t, c_), F32), jax.ShapeDtypeStruct((t, c_), act_dtype)],
        scratch_shapes=[pltpu.VMEM((tt + HALO, cb), F32)],
        compiler_params=_cp(("parallel", "arbitrary")),
    )(*args)


def conv_bwd(name, x, xoff, c, coff, dact, w, kind, up=None, upoff=0):
    t = x.shape[0]
    k_, c_ = w.shape[0], dact.shape[1]
    tt, cb = CONV_TT, CONV_CB
    hb = tt // HALO
    nt = t // tt
    has_up = up is not None

    def tail_grad(cv, dav, upv):
        if has_up:
            _, vjp = jax.vjp(lambda a, u: _tail(kind, a, u), cv, upv)
            return vjp(dav)
        _, vjp = jax.vjp(lambda a: _tail(kind, a, None), cv)
        return vjp(dav)[0], None

    def body(*refs):
        if has_up:
            (x_ref, xp_ref, c_ref, cn_ref, da_ref, dan_ref, w_ref, up_ref, upn_ref,
             dx_ref, dup_ref, dw_ref, db_ref, xs, dcs) = refs
        else:
            x_ref, xp_ref, c_ref, cn_ref, da_ref, dan_ref, w_ref, dx_ref, dw_ref, db_ref, xs, dcs = refs
        i = pl.program_id(1)
        xs[0:HALO, :] = jnp.where(i == 0, 0.0, xp_ref[...])
        xs[HALO:HALO + tt, :] = x_ref[...]
        rb = CONV_RB
        for r in range(tt // rb):
            rows = pl.ds(r * rb, rb)
            dc, dup = tail_grad(c_ref[rows, :], da_ref[rows, :].astype(F32), up_ref[rows, :] if has_up else None)
            dcs[rows, :] = dc
            if has_up:
                dup_ref[rows, :] = dup.astype(BF16)
        dcn, _ = tail_grad(cn_ref[...], dan_ref[...].astype(F32), upn_ref[...] if has_up else None)
        dcs[tt:tt + HALO, :] = jnp.where(i == nt - 1, 0.0, dcn)
        dws = [jnp.zeros((1, cb), F32) for _ in range(k_)]
        dbv = jnp.zeros((1, cb), F32)
        for r in range(tt // rb):
            rows = pl.ds(r * rb, rb)
            dc = dcs[rows, :]
            dx = jnp.zeros((rb, cb), F32)
            for k in range(k_):
                dx = dx + w_ref[k:k + 1, :] * dcs[pl.ds(r * rb + k_ - 1 - k, rb), :]
                dws[k] = dws[k] + _colsum(dc * xs[pl.ds(r * rb + HALO - (k_ - 1) + k, rb), :])
            dbv = dbv + _colsum(dc)
            dx_ref[rows, :] = dx.astype(BF16)

        @pl.when(i == 0)
        def _():
            dw_ref[...] = jnp.zeros_like(dw_ref)
            db_ref[...] = jnp.zeros_like(db_ref)

        for k in range(k_):
            dw_ref[k:k + 1, :] += dws[k]
        db_ref[...] += dbv

    tile = lambda off: pl.BlockSpec((tt, cb), lambda j, i, off=off: (i, off + j))
    prev = lambda off: pl.BlockSpec((HALO, cb), lambda j, i, off=off: (jnp.maximum(i * hb - 1, 0), off + j))
    nxt = lambda off: pl.BlockSpec((HALO, cb), lambda j, i, off=off: (jnp.minimum((i + 1) * hb, t // HALO - 1), off + j))
    in_specs = [tile(xoff), prev(xoff), tile(coff), nxt(coff), tile(0), nxt(0),
                pl.BlockSpec((k_, cb), lambda j, i: (0, coff + j))]
    args = [x, x, c, c, dact, dact, w]
    if has_up:
        in_specs += [tile(upoff), nxt(upoff)]
        args += [up, up]
    out_specs = [tile(0)] + ([tile(0)] if has_up else []) + [pl.BlockSpec((HALO, cb), lambda j, i: (0, j)),
                                                            pl.BlockSpec((1, cb), lambda j, i: (0, j))]
    out_shape = [jax.ShapeDtypeStruct((t, c_), BF16)] * (2 if has_up else 1)
    out_shape += [jax.ShapeDtypeStruct((HALO, c_), F32), jax.ShapeDtypeStruct((1, c_), F32)]
    return pl.pallas_call(
        body, name=name, grid=(c_ // cb, nt), in_specs=in_specs, out_specs=out_specs, out_shape=out_shape,
        scratch_shapes=[pltpu.VMEM((tt + HALO, cb), F32), pltpu.VMEM((tt + HALO, cb), F32)],
        compiler_params=_cp(("parallel", "arbitrary")),
    )(*args)


def ew_sum(name, parts, rows, out_dtype, tr):
    c = parts[0][0].shape[1]
    tr = min(tr, rows)
    assert rows % tr == 0 and all(off % tr == 0 for _, off in parts)
    n = len(parts)

    def body(*refs):
        acc = refs[0][...].astype(F32)
        for ref in refs[1:n]:
            acc = acc + ref[...].astype(F32)
        refs[n][...] = acc.astype(out_dtype)

    in_specs = [pl.BlockSpec((tr, c), lambda i, o=off // tr: (i + o, 0)) for _, off in parts]
    return pl.pallas_call(body, name=name, grid=(rows // tr,), in_specs=in_specs,
                          out_specs=pl.BlockSpec((tr, c), lambda i: (i, 0)),
                          out_shape=jax.ShapeDtypeStruct((rows, c), out_dtype),
                          compiler_params=_cp(("parallel",)))(*[a for a, _ in parts])


def fold_heads(dexp):
    def body(d_ref, o_ref):
        sel = jnp.where(_iota((SSM_DINNER, DT_PAD), 0) // SSM_HEADDIM == _iota((SSM_DINNER, DT_PAD), 1), 1.0, 0.0)
        o_ref[...] = _hdot(jnp.broadcast_to(d_ref[...], (8, SSM_DINNER)), sel.astype(F32), "nn", "a")[0:1, :]

    return pl.pallas_call(body, name="fold_heads", out_shape=jax.ShapeDtypeStruct((1, DT_PAD), F32),
                          compiler_params=pltpu.CompilerParams(vmem_limit_bytes=VMEM_LIMIT))(dexp)


def adamw(name, w, g, m, v, tr):
    r, c = w.shape
    tr = min(tr, r)
    assert r % tr == 0, (r, tr)

    def body(w_ref, g_ref, m_ref, v_ref, d_ref, nm_ref, nv_ref):
        gv = g_ref[...]
        nm = ADAM_B1 * m_ref[...] + (1.0 - ADAM_B1) * gv
        nv = ADAM_B2 * v_ref[...] + (1.0 - ADAM_B2) * (gv * gv)
        m_hat = nm / (1.0 - ADAM_B1 ** ADAM_STEP)
        v_hat = nv / (1.0 - ADAM_B2 ** ADAM_STEP)
        d_ref[...] = -ADAM_LR * (m_hat / (jnp.sqrt(v_hat) + ADAM_EPS) + ADAM_WD * w_ref[...])
        nm_ref[...] = nm
        nv_ref[...] = nv

    spec = pl.BlockSpec((tr, c), lambda i: (i, 0))
    shp = jax.ShapeDtypeStruct((r, c), F32)
    return pl.pallas_call(body, name=name, grid=(r // tr,), in_specs=[spec] * 4, out_specs=[spec] * 3,
                          out_shape=[shp] * 3, compiler_params=_cp(("parallel",)))(w, g, m, v)


SEG_QFIG, SEG_Z, SEG_XBC, SEG_DT, SEG_G = 0, 8192, 12288, 18432, 18496
IN_TOTAL = 22592
FFN_BLOCKS = D_FF // CONV_CB


def _own_slot(gathered, own, shard):
    slot = lax.broadcasted_iota(jnp.int32, (gathered.shape[0],) + (1,) * own.ndim, 0)
    return jnp.where(slot == shard, own[None], gathered)


def local_step(x, target, wts, par, p_rest, p_up, shard, core):
    t = x.shape[0]
    pad64 = lambda a: jnp.pad(a, ((0, 0), (0, DT_PAD - a.shape[1])))
    bias, a_log = pad64(par["ssm_dt_bias"]), pad64(par["ssm_A_log"])
    dexp = jnp.repeat(par["ssm_D"], SSM_HEADDIM, axis=1)
    in_t = wts["in_t"]

    h = norm_fwd(x, par["mix_pre_norm"])
    proj = lambda nm, off, n, tn: mm(h, in_t, "nt", name=nm, tn=tn, dims=(t, n, D), b_off=(off, 0))
    qfig = proj("proj_qfig", SEG_QFIG, 8192, 1024)
    z = proj("proj_z", SEG_Z, 4096, 1024)
    xbc = proj("proj_xbc", SEG_XBC, 6144, 1024)
    dtr = mm(h, wts["dt_t"], "nt", name="proj_dt", tn=128)
    gates = mm(h, wts["g_t"], "nt", name="proj_gates", out_dtype=BF16, tn=1024)
    y_hg, hg_states, g_rest = hgrn2_fwd(qfig, par["hg_lb_table"], par["hg_out_norm"], gather=([p_rest], [REST_PIECES]))
    c_ssm, xbc_act = conv_fwd("ssm_conv_fwd", xbc, 0, par["ssm_conv_w"], par["ssm_conv_b"], "silu")
    dt, acum = dt_fwd(dtr, bias, a_log)
    yssd, ssd_states, g_up = ssd_fwd(xbc_act, dt, acum, gather=([p_up], [UP_PIECES]))
    g_rest, g_up = _own_slot(g_rest, p_rest, shard), _own_slot(g_up, p_up, shard)
    r0, r1, r2, r3 = REST_SPLITS
    wts = dict(wts, bh=g_rest[:, :r0].reshape(-1, D), bs=g_rest[:, r0:r1].reshape(-1, D), o=g_rest[:, r1:r2].reshape(-1, D),
               dn=g_rest[:, r2:r3].reshape(-1, D), up=jnp.transpose(g_up, (1, 0, 2)).reshape(D, 2 * D_FF))
    y_ssm = ssm_post_fwd(yssd, xbc_act, z, dexp, par["ssm_out_norm"])
    a_hg = mm(y_hg, wts["bh"], "nn", name="branch_hg", out_dtype=BF16, tn=1024)
    a_ssm = mm(y_ssm, wts["bs"], "nn", name="branch_ssm", out_dtype=BF16, tn=1024)
    mixed = merge_fwd(a_hg, a_ssm, gates)
    u = mm(mixed, wts["o"], "nn", name="out_proj", tn=1024)
    x1, h2 = post1_fwd(x, u, par["mix_post_norm"], par["ffn_pre_norm"])
    gu = mm(h2, wts["up"], "nn", name="ffn_up", tn=1024)
    c_ffn, act = conv_fwd("ffn_conv_fwd", gu, 0, par["ffn_conv_w"], par["ffn_conv_b"], "gelu_mul",
                          up=gu, upoff=FFN_BLOCKS, act_dtype=BF16)
    fo = mm(act, wts["dn"], "nn", name="ffn_down", tm=512, tn=1024)
    dx2, dfo, g_ffn_post, loss = final_fwd_bwd(x1, fo, par["ffn_post_norm"], target)

    dact = mm(dfo, wts["dn"], "nt", name="d_act", out_dtype=BF16, tn=1408)
    g_dn = mm(act, dfo, "tn", name="g_ffn_down", out_dtype=BF16, tm=1408, tn=1024, tk=2048)
    dgate, dup, g_fcw, g_fcb = conv_bwd("ffn_conv_bwd", gu, 0, c_ffn, 0, dact, par["ffn_conv_w"], "gelu_mul",
                                        up=gu, upoff=FFN_BLOCKS)
    dh2 = mm_segments("d_h2", [(dgate, 0, 0), (dup, 0, D_FF)], [wts["up"]], tm=1024, tn=1024, tk=1408, bt=True)
    g_up_gate = mm(h2, dgate, "tn", name="g_ffn_up_gate", out_dtype=BF16, tm=1024, tn=1408, tk=2048)
    g_up_up = mm(h2, dup, "tn", name="g_ffn_up_up", out_dtype=BF16, tm=1024, tn=1408, tk=2048)
    du, dx1, g_mix_post, g_ffn_pre = post1_bwd(x, u, par["mix_post_norm"], par["ffn_pre_norm"], dx2, dh2)
    dmixed = mm(du, wts["o"], "nt", name="d_mixed", tn=1024)
    g_o = mm(mixed, du, "tn", name="g_w_out", out_dtype=BF16, tm=1024, tn=2048, tk=2048)
    da_hg, da_ssm, dg_hg, dg_ssm = merge_bwd(a_hg, a_ssm, gates, dmixed)
    dy_hg = mm(da_hg, wts["bh"], "nt", name="d_y_hg", out_dtype=BF16, tn=1024)
    g_bh = mm(y_hg, da_hg, "tn", name="g_w_branch_hg", out_dtype=BF16, tm=1024, tn=2048, tk=2048)
    dy_ssm = mm(da_ssm, wts["bs"], "nt", name="d_y_ssm", out_dtype=BF16, tn=1024)
    g_bs = mm(y_ssm, da_ssm, "tn", name="g_w_branch_ssm", out_dtype=BF16, tm=1024, tn=2048, tk=2048)
    dyssd, dskip, dz, g_dexp, g_ssm_norm = ssm_post_bwd(yssd, xbc_act, z, dexp, par["ssm_out_norm"], dy_ssm)

    gg_rest = jnp.concatenate([g.reshape(N_CHIPS, -1, D) for g in (g_bh, g_bs, g_o, g_dn)], axis=1)
    gg_up = jnp.transpose(jnp.concatenate([g_up_gate, g_up_up], axis=1).reshape(D, N_CHIPS, UP_COLS), (1, 0, 2))
    c_rest, c_up = pair_reduce("rest", [gg_rest, gg_up], [REST_PIECES, UP_PIECES], [432, 512], core)
    dxs, db_, dc_, ddt, dacum, rb_rest, rb_up = ssd_bwd(xbc_act, dt, acum, ssd_states, dyssd, dskip, exchange=[c_rest, c_up])
    red_rest, red_up = chip_reduce("rest", [c_rest, c_up], [rb_rest, rb_up], [432, 256], shard)
    ddtr, g_dt_bias, g_a_log = dt_bwd(dtr, bias, a_log, ddt, dacum)
    xs_blocks, bc_blocks = SSM_DINNER // CONV_CB, SSM_GROUPS * SSM_DSTATE // CONV_CB
    dxbc_x, g_cw_x, g_cb_x = conv_bwd("ssm_conv_bwd_x", xbc, 0, c_ssm, 0, dxs, par["ssm_conv_w"], "silu")
    dxbc_b, g_cw_b, g_cb_b = conv_bwd("ssm_conv_bwd_b", xbc, xs_blocks, c_ssm, xs_blocks, db_, par["ssm_conv_w"], "silu")
    dxbc_c, g_cw_c, g_cb_c = conv_bwd("ssm_conv_bwd_c", xbc, xs_blocks + bc_blocks, c_ssm, xs_blocks + bc_blocks, dc_,
                                      par["ssm_conv_w"], "silu")
    dq, df, dv, dg, g_table, g_hg_norm = hgrn2_bwd(qfig, par["hg_lb_table"], par["hg_out_norm"], hg_states, dy_hg)

    dsegs = [(dq, SEG_QFIG), (df, SEG_QFIG + 2048), (dv, SEG_QFIG + 4096), (dg, SEG_QFIG + 6144), (dz, SEG_Z),
             (dxbc_x, SEG_XBC), (dxbc_b, SEG_XBC + SSM_DINNER), (dxbc_c, SEG_XBC + SSM_DINNER + 1024)]
    g_in_parts = [mm(dseg, h, "tn", name=f"g_w_in_{n}", out_dtype=BF16, tm=1024, tn=2048, tk=2048)
                  for n, (dseg, _) in enumerate(dsegs)]
    g_dt_t = mm(ddtr, h, "tn", name="g_w_in_dt", out_dtype=BF16, tm=128, tn=2048, tk=1024)[:SSM_HEADS]
    g_in_parts += [mm(dgate_, h, "tn", name=f"g_w_in_g{n}", out_dtype=BF16, tm=1024, tn=2048, tk=2048)
                   for n, dgate_ in enumerate((dg_hg, dg_ssm))]
    zpad = jnp.zeros((N_CHIPS, IN_ROWS - IN_SHARD, D), BF16)
    g_in_t = jnp.concatenate(g_in_parts[:8] + [g_dt_t] + g_in_parts[8:], axis=0).reshape(N_CHIPS, IN_SHARD, D)
    (c_in,) = pair_reduce("in", [jnp.concatenate([g_in_t, zpad], axis=1)], [IN_PIECES], [960], core)
    dh = mm_segments("d_h_a", [(dseg, 0, off) for dseg, off in dsegs[:5]], [in_t], tm=1024, tn=1024, tk=1024)
    dh, rb_in = mm_segments("d_h_b", [(dseg, 0, off) for dseg, off in dsegs[5:]] + [(ddtr, 1, 0), (dg_hg, 2, 0), (dg_ssm, 2, D)],
                            [in_t, wts["dt_t"], wts["g_t"]], tm=1024, tn=1024, tk=1024, acc=dh, exchange=[c_in])
    (red_in,) = chip_reduce("in", [c_in], [rb_in], [480], shard)
    grad_x, g_mix_pre = norm_bwd(x, par["mix_pre_norm"], dh, dx1)

    big = dict(in_t=red_in, rest=red_rest, up=red_up)
    g_conv_w = jnp.concatenate([g_cw_x, g_cw_b, g_cw_c], axis=1)[:SSM_CONV]
    g_conv_b = jnp.concatenate([g_cb_x, g_cb_b, g_cb_c], axis=1)
    small = dict(mix_pre_norm=g_mix_pre, mix_post_norm=g_mix_post, hg_lb_table=g_table, hg_out_norm=g_hg_norm,
                 ssm_conv_w=g_conv_w, ssm_conv_b=g_conv_b, ssm_dt_bias=g_dt_bias, ssm_A_log=g_a_log,
                 ssm_D=g_dexp, ssm_out_norm=g_ssm_norm, ffn_pre_norm=g_ffn_pre, ffn_post_norm=g_ffn_post,
                 ffn_conv_w=g_fcw[:FFN_CONV], ffn_conv_b=g_fcb)
    return loss, grad_x, big, small


MESH = pl.DeviceIdType.MESH
ANY = pl.BlockSpec(memory_space=pl.ANY)
N_CHIPS = 4
IN_SHARD = 5648
IN_ROWS = 5760
REST_SPLITS = (512, 1536, 2048, 3456)
UP_COLS = 2816
IN_PIECES, REST_PIECES, UP_PIECES = 3, 4, 4


def _place():
    x, y, c = lax.axis_index("x"), lax.axis_index("y"), lax.axis_index("c")
    chips = [(1 - x, y), (x, 1 - y), (1 - x, 1 - y)]
    return x, y, c, chips


def _rcopy(src, dst, send_sems, recv_sems, k, dev):
    return pltpu.make_async_remote_copy(src_ref=src, dst_ref=dst, send_sem=send_sems.at[k], recv_sem=recv_sems.at[k],
                                        device_id=dev, device_id_type=MESH)


def _pieces(rows, n):
    assert rows % n == 0 and (rows // n) % 16 == 0, (rows, n)
    return [(k * (rows // n), rows // n) for k in range(n)]


def _rows(c, hrows, piece):
    return pl.ds(pl.multiple_of(c * hrows + piece[0], 16), piece[1])


def _half_plan(arrays, pieces):
    return [(a.shape[-2] // 2, _pieces(a.shape[-2] // 2, n)) for a, n in zip(arrays, pieces)]


def _sem_pair(n):
    return [pltpu.SemaphoreType.DMA((n,)), pltpu.SemaphoreType.DMA((n,))]


class _Gather:
    def __init__(self, ps, pieces):
        self.plan = _half_plan(ps, pieces)
        self.n_sem = sum(2 * 3 * len(pcs) for _, pcs in self.plan)
        self.out_shape = [jax.ShapeDtypeStruct((N_CHIPS,) + p.shape, p.dtype) for p in ps]

    def _copies(self, p_refs, g_refs, send_sems, recv_sems, only_first=False):
        x, y, c, chips = _place()
        own = 2 * x + y
        sib = (x, y, 1 - c)
        first, arrive, passed, from_sib = [], [], [], []
        k = 0
        for p, g, (hrows, pcs) in zip(p_refs, g_refs, self.plan):
            for chip in chips:
                theirs = 2 * chip[0] + chip[1]
                for pc in pcs:
                    mine, other = _rows(c, hrows, pc), _rows(1 - c, hrows, pc)
                    first.append(_rcopy(p.at[mine], g.at[own, mine], send_sems, recv_sems, k, (*chip, c)))
                    if not only_first:
                        arrive.append(_rcopy(g.at[theirs, mine], g.at[theirs, mine], send_sems, recv_sems, k, (*chip, c)))
                        passed.append(_rcopy(g.at[theirs, mine], g.at[theirs, mine], send_sems, recv_sems, k + 1, sib))
                        from_sib.append(_rcopy(g.at[theirs, other], g.at[theirs, other], send_sems, recv_sems, k + 1, sib))
                    k += 2
        return first, arrive, passed, from_sib

    def start(self, p_refs, g_refs, send_sems, recv_sems):
        for cp in self._copies(p_refs, g_refs, send_sems, recv_sems, only_first=True)[0]:
            cp.start()

    def finish(self, p_refs, g_refs, send_sems, recv_sems):
        first, arrive, passed, from_sib = self._copies(p_refs, g_refs, send_sems, recv_sems)
        for got, fw in zip(arrive, passed):
            got.wait_recv()
            fw.start()
        for cp in from_sib:
            cp.wait_recv()
        for cp in first + passed:
            cp.wait_send()


def gather_weights(name, ps, pieces):
    op = _Gather(ps, pieces)
    n = len(ps)

    def body(*refs):
        p_refs, g_refs, sems = refs[:n], refs[n:2 * n], refs[2 * n:]
        op.start(p_refs, g_refs, *sems)
        op.finish(p_refs, g_refs, *sems)

    return pl.pallas_call(body, name=name, in_specs=[ANY] * n, out_specs=[ANY] * n, out_shape=op.out_shape,
                          scratch_shapes=_sem_pair(op.n_sem))(*ps)


def pair_exchange(name, gs, pieces):
    plan = _half_plan(gs, pieces)
    n_sem = sum(N_CHIPS * len(pcs) for _, pcs in plan)
    n = len(gs)

    def body(*refs):
        g_refs, r_refs, send_sems, recv_sems = refs[:n], refs[n:2 * n], refs[2 * n], refs[2 * n + 1]
        x, y, c, _ = _place()
        sib = (x, y, 1 - c)
        cps = []
        for g, r, (hrows, pcs) in zip(g_refs, r_refs, plan):
            for s in range(N_CHIPS):
                for pc in pcs:
                    cps.append(_rcopy(g.at[s, _rows(1 - c, hrows, pc)], r.at[s, pl.ds(pc[0], pc[1])],
                                      send_sems, recv_sems, len(cps), sib))
        for cp in cps:
            cp.start()
        for cp in cps:
            cp.wait()

    return pl.pallas_call(
        body, name=name, in_specs=[ANY] * n, out_specs=[ANY] * n,
        out_shape=[jax.ShapeDtypeStruct((N_CHIPS, g.shape[1] // 2, g.shape[2]), g.dtype) for g in gs],
        scratch_shapes=_sem_pair(n_sem))(*gs)


class _ChipExchange:
    def __init__(self, ss):
        self.n_sem = 3 * len(ss)
        self.out_shape = [jax.ShapeDtypeStruct((3,) + s.shape[1:], s.dtype) for s in ss]

    def _copies(self, s_refs, r_refs, send_sems, recv_sems):
        x, y, c, chips = _place()
        cps = []
        for s, r in zip(s_refs, r_refs):
            for j, chip in enumerate(chips):
                cps.append(_rcopy(s.at[2 * chip[0] + chip[1]], r.at[j], send_sems, recv_sems, len(cps), (*chip, c)))
        return cps

    def start(self, *refs):
        for cp in self._copies(*refs):
            cp.start()

    def finish(self, *refs):
        for cp in self._copies(*refs):
            cp.wait()


def pair_assemble(name, rs, pieces):
    plan = [(r.shape[0], _pieces(r.shape[0], n_)) for r, n_ in zip(rs, pieces)]
    n_sem = sum(len(pcs) for _, pcs in plan)
    n = len(rs)

    def body(*refs):
        r_refs, f_refs, send_sems, recv_sems = refs[:n], refs[n:2 * n], refs[2 * n], refs[2 * n + 1]
        x, y, c, _ = _place()
        sib = (x, y, 1 - c)
        cps, got = [], []
        for r, f, (hrows, pcs) in zip(r_refs, f_refs, plan):
            for pc in pcs:
                src = r.at[pl.ds(pc[0], pc[1])]
                cps.append(_rcopy(src, f.at[_rows(c, hrows, pc)], send_sems, recv_sems, len(cps), sib))
                got.append(_rcopy(src, f.at[_rows(1 - c, hrows, pc)], send_sems, recv_sems, len(got), sib))
        for cp in cps:
            cp.start()
        for cp in got:
            cp.wait_recv()
        for cp in cps:
            cp.wait_send()

    return pl.pallas_call(
        body, name=name, in_specs=[ANY] * n, out_specs=[ANY] * n,
        out_shape=[jax.ShapeDtypeStruct((2 * r.shape[0], r.shape[1]), r.dtype) for r in rs],
        scratch_shapes=_sem_pair(n_sem))(*rs)


def pair_reduce(tag, ggs, pieces, trs, core):
    recv = pair_exchange("pair_exchange_" + tag, ggs, pieces)
    flat = lambda a: a.reshape(-1, a.shape[-1])
    out = []
    for n, (gg, r, tr) in enumerate(zip(ggs, recv, trs)):
        h = gg.shape[1] // 2
        own = lax.dynamic_slice_in_dim(gg, core * h, h, axis=1)
        out.append(ew_sum(f"pair_sum_{tag}_{n}", [(flat(own), 0), (flat(r), 0)], N_CHIPS * h, BF16, tr).reshape(r.shape))
    return out


def chip_reduce(tag, cs, rbs, trs, shard):
    out = []
    for n, (c, rb, tr) in enumerate(zip(cs, rbs, trs)):
        h = c.shape[1]
        own = lax.dynamic_index_in_dim(c, shard, axis=0, keepdims=False)
        parts = [(own, 0)] + [(rb.reshape(-1, rb.shape[-1]), j * h) for j in range(3)]
        out.append(ew_sum(f"chip_sum_{tag}_{n}", parts, h, F32, tr))
    return out


N_DEV = 8


def gather_small(blk, reduce):
    rows, cols = blk.shape

    def body(x_ref, out_ref, all_ref, send_sems, recv_sems, local_sem):
        x, y, c, chips = _place()
        me, sib = (x, y, c), (x, y, 1 - c)

        def blk_rows(px, py, pc):
            return all_ref.at[pl.ds(pl.multiple_of((4 * px + 2 * py + pc) * rows, 8), rows), :]

        def copy(k, block, to, src=None):
            return _rcopy(blk_rows(*block) if src is None else src, blk_rows(*block), send_sems, recv_sems, k, to)

        mine = pltpu.make_async_copy(x_ref, blk_rows(*me), local_sem)
        mine.start()
        first = [copy(0, me, sib, src=x_ref)] + [copy(1 + j, me, (*chip, c), src=x_ref) for j, chip in enumerate(chips)]
        for cp in first:
            cp.start()
        passed = [copy(4 + j, (*chip, c), sib) for j, chip in enumerate(chips)]
        for j, chip in enumerate(chips):
            copy(1 + j, (*chip, c), me).wait_recv()
            passed[j].start()
        copy(0, sib, me).wait_recv()
        for j, chip in enumerate(chips):
            copy(4 + j, (*chip, 1 - c), me).wait_recv()
        for cp in first + passed:
            cp.wait_send()
        mine.wait()
        if reduce:
            acc = all_ref[0:rows, :]
            for d in range(1, N_DEV):
                acc = acc + all_ref[d * rows:(d + 1) * rows, :]
            out_ref[...] = acc
        else:
            out_ref[...] = all_ref[...]

    vmem = pl.BlockSpec(memory_space=pltpu.VMEM)
    return pl.pallas_call(
        body, name="reduce_small" if reduce else "gather_small", in_specs=[vmem], out_specs=vmem,
        out_shape=jax.ShapeDtypeStruct((rows if reduce else N_DEV * rows, cols), blk.dtype),
        scratch_shapes=[pltpu.VMEM((N_DEV * rows, cols), blk.dtype), pltpu.SemaphoreType.DMA((7,)),
                        pltpu.SemaphoreType.DMA((7,)), pltpu.SemaphoreType.DMA],
        compiler_params=pltpu.CompilerParams(vmem_limit_bytes=VMEM_LIMIT),
    )(blk)


WEIGHTS = ['w_in', 'mix_pre_norm', 'mix_post_norm', 'hg_lb_table', 'hg_out_norm', 'ssm_conv_w', 'ssm_conv_b',
           'ssm_dt_bias', 'ssm_A_log', 'ssm_D', 'ssm_out_norm', 'w_branch_hg', 'w_branch_ssm', 'w_out', 'ffn_pre_norm',
           'ffn_post_norm', 'ffn_w_up', 'ffn_conv_w', 'ffn_conv_b', 'ffn_w_down']
BIG = ('w_in', 'w_branch_hg', 'w_branch_ssm', 'w_out', 'ffn_w_up', 'ffn_w_down')
SMALL = tuple(n for n in WEIGHTS if n not in BIG)
CONV_SHARD = {'ssm_conv_w': SSM_CONV_DIM // N_CHIPS, 'ffn_conv_w': D_FF // N_CHIPS}
LANES = 128


def _pack(parts):
    flat = jnp.concatenate([p.reshape(-1) for p in parts])
    n = flat.shape[0]
    rows = -(-n // (8 * LANES)) * 8
    return jnp.pad(flat, (0, rows * LANES - n)).reshape(rows, LANES)


def _unpack(packed, shapes):
    flat = packed.reshape(-1)
    out, off = [], 0
    for s in shapes:
        n = int(np.prod(s))
        out.append(flat[off:off + n].reshape(s))
        off += n
    return out


def kernel(x, w_in, mix_pre_norm, mix_post_norm, hg_lb_table, hg_out_norm, ssm_conv_w, ssm_conv_b, ssm_dt_bias, ssm_A_log, ssm_D, ssm_out_norm, w_branch_hg, w_branch_ssm, w_out, ffn_pre_norm, ffn_post_norm, ffn_w_up, ffn_conv_w, ffn_conv_b, ffn_w_down, loss_target, m_w_in, m_mix_pre_norm, m_mix_post_norm, m_hg_lb_table, m_hg_out_norm, m_ssm_conv_w, m_ssm_conv_b, m_ssm_dt_bias, m_ssm_A_log, m_ssm_D, m_ssm_out_norm, m_w_branch_hg, m_w_branch_ssm, m_w_out, m_ffn_pre_norm, m_ffn_post_norm, m_ffn_w_up, m_ffn_conv_w, m_ffn_conv_b, m_ffn_w_down, v_w_in, v_mix_pre_norm, v_mix_post_norm, v_hg_lb_table, v_hg_out_norm, v_ssm_conv_w, v_ssm_conv_b, v_ssm_dt_bias, v_ssm_A_log, v_ssm_D, v_ssm_out_norm, v_w_branch_hg, v_w_branch_ssm, v_w_out, v_ffn_pre_norm, v_ffn_post_norm, v_ffn_w_up, v_ffn_conv_w, v_ffn_conv_b, v_ffn_w_down):
    w = dict(w_in=w_in, mix_pre_norm=mix_pre_norm, mix_post_norm=mix_post_norm, hg_lb_table=hg_lb_table, hg_out_norm=hg_out_norm, ssm_conv_w=ssm_conv_w, ssm_conv_b=ssm_conv_b, ssm_dt_bias=ssm_dt_bias, ssm_A_log=ssm_A_log, ssm_D=ssm_D, ssm_out_norm=ssm_out_norm, w_branch_hg=w_branch_hg, w_branch_ssm=w_branch_ssm, w_out=w_out, ffn_pre_norm=ffn_pre_norm, ffn_post_norm=ffn_post_norm, ffn_w_up=ffn_w_up, ffn_conv_w=ffn_conv_w, ffn_conv_b=ffn_conv_b, ffn_w_down=ffn_w_down)
    m = dict(w_in=m_w_in, mix_pre_norm=m_mix_pre_norm, mix_post_norm=m_mix_post_norm, hg_lb_table=m_hg_lb_table, hg_out_norm=m_hg_out_norm, ssm_conv_w=m_ssm_conv_w, ssm_conv_b=m_ssm_conv_b, ssm_dt_bias=m_ssm_dt_bias, ssm_A_log=m_ssm_A_log, ssm_D=m_ssm_D, ssm_out_norm=m_ssm_out_norm, w_branch_hg=m_w_branch_hg, w_branch_ssm=m_w_branch_ssm, w_out=m_w_out, ffn_pre_norm=m_ffn_pre_norm, ffn_post_norm=m_ffn_post_norm, ffn_w_up=m_ffn_w_up, ffn_conv_w=m_ffn_conv_w, ffn_conv_b=m_ffn_conv_b, ffn_w_down=m_ffn_w_down)
    v = dict(w_in=v_w_in, mix_pre_norm=v_mix_pre_norm, mix_post_norm=v_mix_post_norm, hg_lb_table=v_hg_lb_table, hg_out_norm=v_hg_out_norm, ssm_conv_w=v_ssm_conv_w, ssm_conv_b=v_ssm_conv_b, ssm_dt_bias=v_ssm_dt_bias, ssm_A_log=v_ssm_A_log, ssm_D=v_ssm_D, ssm_out_norm=v_ssm_out_norm, w_branch_hg=v_w_branch_hg, w_branch_ssm=v_w_branch_ssm, w_out=v_w_out, ffn_pre_norm=v_ffn_pre_norm, ffn_post_norm=v_ffn_post_norm, ffn_w_up=v_ffn_w_up, ffn_conv_w=v_ffn_conv_w, ffn_conv_b=v_ffn_conv_b, ffn_w_down=v_ffn_w_down)
    shard = 2 * lax.axis_index("x") + lax.axis_index("y")
    bf = lambda a: a.astype(BF16)

    core = lax.axis_index("c")
    p_in = jnp.concatenate([bf(w_in[0].T), jnp.zeros((IN_ROWS - IN_SHARD, D_MODEL), BF16)], axis=0)
    p_rest = jnp.concatenate([bf(w_branch_hg[0]), bf(w_branch_ssm[0]), bf(w_out[0]), bf(ffn_w_down[0])], axis=0)
    p_up = bf(ffn_w_up[0])
    (g_in,) = gather_weights("gather_w_in", [p_in], [IN_PIECES])
    in_t = _own_slot(g_in, p_in, shard)[:, :IN_SHARD].reshape(IN_TOTAL, D_MODEL)
    wts = dict(in_t=in_t, g_t=in_t[SEG_G:], dt_t=jnp.pad(in_t[SEG_DT:SEG_G], ((0, DT_PAD - SSM_HEADS), (0, 0))))
    conv_cols = max(CONV_SHARD.values())
    padc = lambda a: jnp.pad(a, ((0, 0), (0, conv_cols - a.shape[1])))
    conv_blk = jnp.concatenate([padc(ssm_conv_w[0]), padc(ffn_conv_w[0]), jnp.zeros((1, conv_cols), F32)], axis=0)
    conv_all = gather_small(conv_blk, reduce=False)
    par = {n: w[n] for n in SMALL}
    par["ssm_conv_w"] = jnp.concatenate([conv_all[16 * s:16 * s + SSM_CONV, :CONV_SHARD['ssm_conv_w']] for s in range(N_CHIPS)], axis=1)
    par["ffn_conv_w"] = jnp.concatenate([conv_all[16 * s + SSM_CONV:16 * s + SSM_CONV + FFN_CONV, :CONV_SHARD['ffn_conv_w']]
                                         for s in range(N_CHIPS)], axis=1)

    loss, grad_x, big, small = local_step(x[0], loss_target[0], wts, par, p_rest, p_up, shard, core)
    loss = lax.psum(loss[0, 0], ("x", "y", "c"))

    halves = [big["in_t"], big["rest"], big["up"]]
    wholes = pair_assemble("pair_assemble", halves, [IN_PIECES, REST_PIECES, UP_PIECES])
    f_in, f_rest, f_up = [_own_slot(f.reshape((2,) + r.shape), r, core).reshape(f.shape) for f, r in zip(wholes, halves)]
    r0, r1, r2, r3 = REST_SPLITS
    grads = dict(w_in=f_in[:IN_SHARD].T, w_branch_hg=f_rest[:r0], w_branch_ssm=f_rest[r0:r1], w_out=f_rest[r1:r2],
                 ffn_w_down=f_rest[r2:r3], ffn_w_up=f_up)

    small["hg_out_norm"] = ew_sum("sum_heads", [(small["hg_out_norm"][hd], 0) for hd in range(HG_HEADS)], 1, F32, 1)
    small["ssm_D"] = fold_heads(small["ssm_D"])[:, :SSM_HEADS]
    small["ssm_dt_bias"] = small["ssm_dt_bias"][:, :SSM_HEADS]
    small["ssm_A_log"] = small["ssm_A_log"][:, :SSM_HEADS]
    shapes = [small[n].shape for n in SMALL]
    summed = _unpack(gather_small(_pack([small[n] for n in SMALL]), reduce=True), shapes)
    for n, g in zip(SMALL, summed):
        if n in CONV_SHARD:
            g = lax.dynamic_slice_in_dim(g, shard * CONV_SHARD[n], CONV_SHARD[n], axis=1)
        grads[n] = g

    two_d = lambda a: a.reshape(a.shape[-2], a.shape[-1])
    delta, new_m, new_v = {}, {}, {}
    for n, tr in (("w_in", 64), ("w_branch_hg", 128), ("w_branch_ssm", 128), ("w_out", 128), ("ffn_w_up", 128), ("ffn_w_down", 128)):
        delta[n], new_m[n], new_v[n] = adamw("adamw_" + n, two_d(w[n]), grads[n], two_d(m[n]), two_d(v[n]), tr)
    sm_shapes = [two_d(w[n]).shape for n in SMALL]
    packed = adamw("adamw_small", _pack([two_d(w[n]) for n in SMALL]), _pack([grads[n] for n in SMALL]),
                   _pack([two_d(m[n]) for n in SMALL]), _pack([two_d(v[n]) for n in SMALL]), 1024)
    for res, packed_res in zip((delta, new_m, new_v), packed):
        for n, a in zip(SMALL, _unpack(packed_res, sm_shapes)):
            res[n] = a
    shaped = lambda d: [d[n].reshape(w[n].shape) for n in WEIGHTS]
    return (loss, grad_x[None], *shaped(grads), *shaped(delta), *shaped(new_m), *shaped(new_v))
```

```python
import functools

import jax
import jax.numpy as jnp
import numpy as np
from jax import lax
from jax.experimental import pallas as pl
from jax.experimental.pallas import tpu as pltpu

F32 = jnp.float32
BF16 = jnp.bfloat16

D_MODEL = 2048
EPS = 1e-6
HG_HEADS = 16
HG_DK = 128
HG_CHUNK = 64
HG_SUB = 16
SSM_DINNER = 4096
SSM_HEADDIM = 64
SSM_HEADS = 64
SSM_GROUPS = 8
SSM_DSTATE = 128
SSM_CONV = 4
SSM_CHUNK = 256
SSM_CONV_DIM = 6144
D_FF = 5632
FFN_CONV = 3
DT_PAD = 128

ADAM_LR = 0.001
ADAM_B1 = 0.9
ADAM_B2 = 0.999
ADAM_EPS = 1e-08
ADAM_WD = 0.01
ADAM_STEP = 10

VMEM_LIMIT = 56 * 1024 * 1024
HI = lax.Precision.HIGHEST


def _cp(sem, **kw):
    return pltpu.CompilerParams(dimension_semantics=sem, vmem_limit_bytes=VMEM_LIMIT, **kw)


_DIMS = {"nn": (((1,), (0,)), ((), ())), "nt": (((1,), (1,)), ((), ())), "tn": (((0,), (0,)), ((), ()))}


def mm(a, b, mode, *, name, out_dtype=F32, tm=1024, tn=512, tk=None, acc=None, n_major=True,
       dims=None, a_off=(0, 0), b_off=(0, 0)):
    if dims is not None:
        M, N, K = dims
    else:
        if mode == "nn":
            (M, K), (K2, N) = a.shape, b.shape
        elif mode == "nt":
            (M, K), (N, K2) = a.shape, b.shape
        else:
            (K, M), (K2, N) = a.shape, b.shape
        assert K == K2, (a.shape, b.shape, mode)
    tm, tn = min(tm, M), min(tn, N)
    tk = K if tk is None else min(tk, K)
    assert M % tm == 0 and N % tn == 0 and K % tk == 0, (M, N, K, tm, tn, tk)
    a_blk = (tk, tm) if mode == "tn" else (tm, tk)
    b_blk = (tn, tk) if mode == "nt" else (tk, tn)
    assert all(o % s == 0 for o, s in zip(a_off, a_blk)) and all(o % s == 0 for o, s in zip(b_off, b_blk))
    ao0, ao1 = a_off[0] // a_blk[0], a_off[1] // a_blk[1]
    bo0, bo1 = b_off[0] // b_blk[0], b_off[1] // b_blk[1]
    nk = K // tk
    if n_major:
        grid = (N // tn, M // tm, nk)
        ij = lambda p0, p1: (p1, p0)
    else:
        grid = (M // tm, N // tn, nk)
        ij = lambda p0, p1: (p0, p1)

    def a_map(p0, p1, k):
        i, _ = ij(p0, p1)
        return (k + ao0, i + ao1) if mode == "tn" else (i + ao0, k + ao1)

    def b_map(p0, p1, k):
        _, j = ij(p0, p1)
        return (j + bo0, k + bo1) if mode == "nt" else (k + bo0, j + bo1)

    def o_map(p0, p1, k):
        return ij(p0, p1)

    a_spec = pl.BlockSpec(a_blk, a_map)
    b_spec = pl.BlockSpec(b_blk, b_map)
    o_spec = pl.BlockSpec((tm, tn), o_map)
    dims = _DIMS[mode]
    has_acc = acc is not None

    def body(*refs):
        if has_acc:
            a_ref, b_ref, c_ref, o_ref, acc_ref = refs
        else:
            a_ref, b_ref, o_ref, acc_ref = refs
        k = pl.program_id(2)
        part = lax.dot_general(a_ref[...], b_ref[...], dims, preferred_element_type=F32)
        if nk == 1:
            o_ref[...] = (part + c_ref[...].astype(F32) if has_acc else part).astype(out_dtype)
            return

        @pl.when(k == 0)
        def _():
            acc_ref[...] = part

        @pl.when(k > 0)
        def _():
            acc_ref[...] += part

        @pl.when(k == nk - 1)
        def _():
            r = acc_ref[...]
            if has_acc:
                r = r + c_ref[...].astype(F32)
            o_ref[...] = r.astype(out_dtype)

    in_specs = [a_spec, b_spec] + ([o_spec] if has_acc else [])
    args = (a, b) + ((acc,) if has_acc else ())
    return pl.pallas_call(
        body, name=name, grid=grid, in_specs=in_specs, out_specs=o_spec,
        out_shape=jax.ShapeDtypeStruct((M, N), out_dtype),
        scratch_shapes=[pltpu.VMEM((tm, tn) if nk > 1 else (8, 128), F32)],
        compiler_params=_cp(("parallel", "parallel", "arbitrary")),
    )(*args)


def mm_segments(name, segs, bs, *, tm, tn, tk, acc=None, exchange=None, bt=False):
    m_, n_ = segs[0][0].shape[0], bs[0].shape[0 if bt else 1]
    tm, tn = min(tm, m_), min(tn, n_)
    op = _ChipExchange(exchange) if exchange else None
    ne = (len(exchange) if exchange else 0)
    na = 0 if acc is None else 1
    steps, k0 = [], 0
    for a, bi, row in segs:
        w = a.shape[1]
        tks = min(tk, w)
        assert w % tks == 0 and row % tks == 0 and tks == min(tk, bs[bi].shape[1 if bt else 0]), (w, row, tks)
        steps.append((k0, w // tks, tks, bi, row // tks))
        k0 += w // tks
    nk = k0
    assert m_ % tm == 0 and n_ % tn == 0

    def a_spec(k_first, count, tks):
        return pl.BlockSpec((tm, tks), lambda j, i, k: (i, jnp.clip(k - k_first, 0, count - 1)))

    def b_spec(bi):
        mine = [s for s in steps if s[3] == bi]

        def index(j, i, k):
            blk = mine[0][4]
            for k_first, count, _, _, first_blk in mine:
                blk = jnp.where(k >= k_first, first_blk + jnp.minimum(k - k_first, count - 1), blk)
            return (j, blk) if bt else (blk, j)
        return pl.BlockSpec((tn, mine[0][2]) if bt else (mine[0][2], tn), index)

    ns = len(segs)

    nb = len(bs)
    grid = (n_ // tn, m_ // tm, nk)

    def body(*refs):
        a_refs, b_refs = refs[:ns], refs[ns:ns + nb]
        acc_in = refs[ns + nb] if na else None
        rest = refs[ns + nb + na:]
        ex_refs, o_ref, got_refs, acc_ref, sems = rest[:ne], rest[ne], rest[ne + 1:2 * ne + 1], rest[2 * ne + 1], rest[2 * ne + 2:]
        k = pl.program_id(2)
        if op:
            first = (pl.program_id(0) == 0) & (pl.program_id(1) == 0) & (k == 0)
            last = (pl.program_id(0) == grid[0] - 1) & (pl.program_id(1) == grid[1] - 1) & (k == nk - 1)

            @pl.when(first)
            def _():
                op.start(ex_refs, got_refs, *sems)

            @pl.when(last)
            def _():
                op.finish(ex_refs, got_refs, *sems)

        @pl.when(k == 0)
        def _():
            acc_ref[...] = acc_in[...] if na else jnp.zeros_like(acc_ref)

        for a_ref, (k_first, count, _, bi, _) in zip(a_refs, steps):
            @pl.when((k >= k_first) & (k < k_first + count))
            def _(a_ref=a_ref, bi=bi):
                acc_ref[...] += lax.dot_general(a_ref[...], b_refs[bi][...], _DIMS["nt" if bt else "nn"],
                                                preferred_element_type=F32)

        @pl.when(k == nk - 1)
        def _():
            o_ref[...] = acc_ref[...]

    any_spec = pl.BlockSpec(memory_space=pl.ANY)
    o_spec = pl.BlockSpec((tm, tn), lambda j, i, k: (i, j))
    outs = pl.pallas_call(
        body, name=name, grid=grid,
        in_specs=[a_spec(s[0], s[1], s[2]) for s in steps] + [b_spec(bi) for bi in range(nb)] + [o_spec] * na + [any_spec] * ne,
        out_specs=[o_spec] + [any_spec] * ne,
        out_shape=[jax.ShapeDtypeStruct((m_, n_), F32)] + (op.out_shape if op else []),
        scratch_shapes=[pltpu.VMEM((tm, tn), F32)] + (_sem_pair(op.n_sem) if op else []),
        compiler_params=_cp(("arbitrary", "arbitrary", "arbitrary")),
    )(*[a for a, _, _ in segs], *bs, *(() if acc is None else (acc,)), *(exchange or ()))
    return outs if op else outs[0]


def _dims(mode, ndim):
    if ndim == 2:
        return _DIMS[mode]
    (ca,), (cb,) = _DIMS[mode][0]
    return (((ca + 1,), (cb + 1,)), ((0,), (0,)))


def _bdot_plain(a, b, mode):
    return lax.dot_general(a.astype(BF16), b.astype(BF16), _dims(mode, a.ndim), preferred_element_type=F32)


@functools.partial(jax.custom_vjp, nondiff_argnums=(2,))
def _bdot_vjp(a, b, mode):
    return _bdot_plain(a, b, mode)


def _bdot_fwd(a, b, mode):
    return _bdot_plain(a, b, mode), (a, b)


def _bdot_bwd(mode, res, g):
    a, b = res
    if mode == "nn":
        return _bdot_plain(g, b, "nt"), _bdot_plain(a, g, "tn")
    if mode == "nt":
        return _bdot_plain(g, b, "nn"), _bdot_plain(g, a, "tn")
    return _bdot_plain(b, g, "nt"), _bdot_plain(a, g, "nn")


_bdot_vjp.defvjp(_bdot_fwd, _bdot_bwd)


def _split3(x):
    x1 = x.astype(BF16)
    r1 = x - x1.astype(F32)
    x2 = r1.astype(BF16)
    return x1, x2, (r1 - x2.astype(F32)).astype(BF16)


def _hdot_impl(a, b, mode, data):
    dims = _dims(mode, a.ndim)
    if data == "a":
        sel = b.astype(BF16)
        parts = [lax.dot_general(p, sel, dims, preferred_element_type=F32) for p in _split3(a)]
    else:
        sel = a.astype(BF16)
        parts = [lax.dot_general(sel, p, dims, preferred_element_type=F32) for p in _split3(b)]
    return (parts[2] + parts[1]) + parts[0]


@functools.partial(jax.custom_vjp, nondiff_argnums=(2, 3))
def _hdot(a, b, mode="nn", data="b"):
    return _hdot_impl(a, b, mode, data)


def _hdot_fwd(a, b, mode, data):
    return _hdot_impl(a, b, mode, data), (a, b)


def _hdot_bwd(mode, data, res, g):
    a, b = res
    if data == "a":
        da = {"nn": lambda: _hdot_impl(g, b, "nt", "a"), "nt": lambda: _hdot_impl(g, b, "nn", "a"),
              "tn": lambda: _hdot_impl(b, g, "nt", "b")}[mode]()
        return da, jnp.zeros_like(b)
    db = {"nn": lambda: _hdot_impl(a, g, "tn", "b"), "nt": lambda: _hdot_impl(g, a, "tn", "a"),
          "tn": lambda: _hdot_impl(a, g, "nn", "b")}[mode]()
    return jnp.zeros_like(a), db


_hdot.defvjp(_hdot_fwd, _hdot_bwd)


def _sigmoid(x):
    return lax.logistic(x)


def _silu(x):
    return x * _sigmoid(x)


def _iota(shape, dim):
    return lax.broadcasted_iota(jnp.int32, shape, dim)


def _rms(x, w):
    return x * lax.rsqrt(jnp.mean(x * x, axis=-1, keepdims=True) + EPS) * w


def _hg_chunk(q_raw, f_raw, v, g, st, t0, t1, nw, dot):
    nhd, c = q_raw.shape[0], q_raw.shape[1]
    m = jnp.maximum(t0, t1)
    e0, e1 = jnp.exp(t0 - m), jnp.exp(t1 - m)
    lb = e0 / (e0 + e1)
    f = lb + (1.0 - lb) * _sigmoid(f_raw)
    k = 1.0 - f
    lf = jnp.log(f)
    qh = _silu(q_raw) * (HG_DK ** -0.5)
    row, col = _iota((c, c), 0), _iota((c, c), 1)
    causal = col <= row
    tril = jnp.broadcast_to(jnp.where(causal, 1.0, 0.0).astype(F32), (nhd, c, c))
    trilb = jnp.broadcast_to(jnp.where(causal & (col // HG_SUB == row // HG_SUB), 1.0, 0.0).astype(F32), (nhd, c, c))
    b = _hdot(tril, lf)
    bl = _hdot(trilb, lf)
    a_row = b - bl
    rid = _iota((c, HG_DK), 0)
    qt = qh * jnp.exp(bl)
    kt = k * jnp.exp(-bl)
    scores = jnp.zeros((nhd, c, c), F32)
    for j in range(c // HG_SUB):
        if j == 0:
            qj = qt * jnp.exp(a_row)
        else:
            a_j = jnp.sum(jnp.where(rid == j * HG_SUB - 1, b, 0.0), axis=1, keepdims=True)
            qj = qt * jnp.exp(jnp.where(rid // HG_SUB >= j, a_row - a_j, -1e30))
        kj = jnp.where(rid // HG_SUB == j, kt, 0.0)
        scores = scores + dot(qj, kj, "nt")
    scores = jnp.where(causal, scores, 0.0)
    o = dot(scores, v, "nn") + dot(qh * jnp.exp(b), st, "nt")
    b_last = jnp.sum(jnp.where(rid == c - 1, b, 0.0), axis=1, keepdims=True)
    st_new = st * jnp.exp(b_last) + dot(v, k * jnp.exp(b_last - b), "tn")
    y = _rms(o, nw) * _silu(g)
    return y, st_new


HG_HPS = 16
HG_W = HG_HPS * HG_DK


def hgrn2_fwd(qfig, table, nw, *, step_chunks=2, gather=None):
    t = qfig.shape[0]
    rows = HG_CHUNK * step_chunks
    nsteps = t // rows
    nh = HG_HEADS // HG_HPS
    op = _Gather(*gather) if gather else None
    ng = len(gather[0]) if gather else 0

    def body(*refs):
        q_ref, f_ref, v_ref, g_ref, tab_ref, nw_ref = refs[:6]
        p_refs = refs[6:6 + ng]
        y_ref, s_ref = refs[6 + ng:8 + ng]
        got_refs = refs[8 + ng:8 + 2 * ng]
        st_scr = refs[8 + 2 * ng]
        sems = refs[9 + 2 * ng:]
        first_step = (pl.program_id(0) == 0) & (pl.program_id(1) == 0)
        last_step = (pl.program_id(0) == nh - 1) & (pl.program_id(1) == nsteps - 1)
        if op:
            @pl.when(first_step)
            def _():
                op.start(p_refs, got_refs, *sems)

        @pl.when(pl.program_id(1) == 0)
        def _():
            st_scr[...] = jnp.zeros_like(st_scr)

        nwv = nw_ref[...]
        lanes = [pl.ds(hh * HG_DK, HG_DK) for hh in range(HG_HPS)]
        t0 = jnp.stack([tab_ref[0:1, ln] for ln in lanes])
        t1 = jnp.stack([tab_ref[1:2, ln] for ln in lanes])
        for c in range(step_chunks):
            sl = pl.ds(c * HG_CHUNK, HG_CHUNK)
            heads = lambda ref: jnp.stack([ref[sl, ln] for ln in lanes])
            st = st_scr[...]
            for hh in range(HG_HPS):
                s_ref[hh, c] = st[hh]
            y, st_new = _hg_chunk(heads(q_ref), heads(f_ref), heads(v_ref), heads(g_ref), st, t0, t1, nwv, _bdot_vjp)
            for hh, ln in enumerate(lanes):
                y_ref[sl, ln] = y[hh].astype(BF16)
            st_scr[...] = st_new

        if op:
            @pl.when(last_step)
            def _():
                op.finish(p_refs, got_refs, *sems)

    blk = lambda off: pl.BlockSpec((rows, HG_W), lambda h, c, off=off: (c, off + h))
    return pl.pallas_call(
        body, name="hgrn2_fwd", grid=(nh, nsteps),
        in_specs=[blk(0), blk(nh), blk(2 * nh), blk(3 * nh),
                  pl.BlockSpec((2, HG_W), lambda h, c: (0, h)), pl.BlockSpec((1, HG_DK), lambda h, c: (0, 0))] + [ANY] * ng,
        out_specs=[pl.BlockSpec((rows, HG_W), lambda h, c: (c, h)),
                   pl.BlockSpec((HG_HPS, step_chunks, HG_DK, HG_DK), lambda h, c: (h, c, 0, 0))] + [ANY] * ng,
        out_shape=[jax.ShapeDtypeStruct((t, HG_HEADS * HG_DK), BF16),
                   jax.ShapeDtypeStruct((HG_HEADS, t // HG_CHUNK, HG_DK, HG_DK), F32)] + (op.out_shape if op else []),
        scratch_shapes=[pltpu.VMEM((HG_HPS, HG_DK, HG_DK), F32)] + (_sem_pair(op.n_sem) if op else []),
        compiler_params=_cp(("arbitrary", "arbitrary")),
    )(qfig, qfig, qfig, qfig, table, nw, *(gather[0] if gather else ()))


def hgrn2_bwd(qfig, table, nw, states, dy, *, step_chunks=2):
    t = qfig.shape[0]
    rows = HG_CHUNK * step_chunks
    nsteps = t // rows
    nh = HG_HEADS // HG_HPS

    def body(q_ref, f_ref, v_ref, g_ref, tab_ref, nw_ref, s_ref, dy_ref,
             dq_ref, df_ref, dv_ref, dg_ref, dtab_ref, dnw_ref, dst_scr):
        @pl.when(pl.program_id(1) == 0)
        def _():
            dst_scr[...] = jnp.zeros_like(dst_scr)
            dtab_ref[...] = jnp.zeros_like(dtab_ref)
            dnw_ref[...] = jnp.zeros_like(dnw_ref)

        nwv = nw_ref[...]
        fn = functools.partial(_hg_chunk, dot=_bdot_vjp)
        lanes = [pl.ds(hh * HG_DK, HG_DK) for hh in range(HG_HPS)]
        t0 = jnp.stack([tab_ref[0:1, ln] for ln in lanes])
        t1 = jnp.stack([tab_ref[1:2, ln] for ln in lanes])
        for c in reversed(range(step_chunks)):
            sl = pl.ds(c * HG_CHUNK, HG_CHUNK)
            heads = lambda ref: jnp.stack([ref[sl, ln] for ln in lanes])
            _, vjp = jax.vjp(fn, heads(q_ref), heads(f_ref), heads(v_ref), heads(g_ref), s_ref[:, c], t0, t1, nwv)
            dq, df, dv, dg, dst, dt0, dt1, dnw = vjp((heads(dy_ref).astype(F32), dst_scr[...]))
            for hh, ln in enumerate(lanes):
                dq_ref[sl, ln] = dq[hh].astype(BF16)
                df_ref[sl, ln] = df[hh].astype(BF16)
                dv_ref[sl, ln] = dv[hh].astype(BF16)
                dg_ref[sl, ln] = dg[hh].astype(BF16)
                dtab_ref[0:1, ln] += dt0[hh]
                dtab_ref[1:2, ln] += dt1[hh]
            dst_scr[...] = dst
            dnw_ref[0] += dnw

    rev = lambda c: nsteps - 1 - c
    blk = lambda off: pl.BlockSpec((rows, HG_W), lambda h, c, off=off: (rev(c), off + h))
    oblk = lambda: pl.BlockSpec((rows, HG_W), lambda h, c: (rev(c), h))
    d = HG_HEADS * HG_DK
    outs = pl.pallas_call(
        body, name="hgrn2_bwd", grid=(nh, nsteps),
        in_specs=[blk(0), blk(nh), blk(2 * nh), blk(3 * nh),
                  pl.BlockSpec((2, HG_W), lambda h, c: (0, h)), pl.BlockSpec((1, HG_DK), lambda h, c: (0, 0)),
                  pl.BlockSpec((HG_HPS, step_chunks, HG_DK, HG_DK), lambda h, c: (h, rev(c), 0, 0)),
                  pl.BlockSpec((rows, HG_W), lambda h, c: (rev(c), h))],
        out_specs=[oblk(), oblk(), oblk(), oblk(),
                   pl.BlockSpec((2, HG_W), lambda h, c: (0, h)),
                   pl.BlockSpec((HG_HPS, 1, HG_DK), lambda h, c: (h, 0, 0))],
        out_shape=[jax.ShapeDtypeStruct((t, d), BF16)] * 4
        + [jax.ShapeDtypeStruct((2, d), F32), jax.ShapeDtypeStruct((HG_HEADS, 1, HG_DK), F32)],
        scratch_shapes=[pltpu.VMEM((HG_HPS, HG_DK, HG_DK), F32)],
        compiler_params=_cp(("parallel", "arbitrary")),
    )(qfig, qfig, qfig, qfig, table, nw, states, dy)
    return outs


def _ssd_chunk(xs2, dt, acum, bm, cm, s2, pair0, dot):
    npr, c = xs2.shape[0], xs2.shape[1]
    sh_e, sh_s = (npr, DT_PAD, 128), (npr, 8, DT_PAD)
    first_head = 2 * (pair0 + _iota(sh_e, 0))
    expand = jnp.where(_iota(sh_e, 1) == first_head + _iota(sh_e, 2) // SSM_HEADDIM, 1.0, 0.0).astype(F32)
    sel = (_iota(sh_s, 2) == 2 * (pair0 + _iota(sh_s, 0)) + _iota(sh_s, 1)) & (_iota(sh_s, 1) < 2)
    sel = jnp.where(sel, 1.0, 0.0).astype(F32)
    per_pair = lambda a: jnp.broadcast_to(a, (npr,) + a.shape)
    dtx = _hdot(per_pair(dt), expand, "nn", "a")
    acol = _hdot(per_pair(acum), expand, "nn", "a")
    arow8 = _hdot(sel, per_pair(acum), "nt", "b")
    row, col = _iota((c, c), 0), _iota((c, c), 1)
    causal = col <= row
    cb = dot(cm, bm, "nt")
    x2 = xs2 * dtx
    lane_c = _iota((c, 128), 1)
    y = dot(per_pair(cm), s2, "nn") * jnp.exp(acol)
    for r in range(2):
        head = (lane_c // SSM_HEADDIM) == r
        a_c = jnp.sum(jnp.where(head & (lane_c % SSM_HEADDIM == 0), acol, 0.0), axis=2, keepdims=True)
        a_r = jnp.sum(jnp.where(_iota((8, c), 0) == r, arow8, 0.0), axis=1, keepdims=True)
        decay = jnp.exp(jnp.where(causal, a_c - a_r, -1e30))
        y = y + dot(cb * decay, jnp.where(head, x2, 0.0), "nn")
    a_last = jnp.sum(jnp.where(_iota((c, 128), 0) == c - 1, acol, 0.0), axis=1, keepdims=True)
    s2_new = s2 * jnp.exp(a_last) + dot(per_pair(bm), x2 * jnp.exp(a_last - acol), "tn")
    return y, s2_new


SSM_PAIRS = SSM_HEADS // 2
PAIRS_PER_GROUP = SSM_PAIRS // SSM_GROUPS
SSD_PPS = 4
SSD_W = SSD_PPS * 128
_XS_BLOCKS = SSM_DINNER // 128
_B_BLOCK0 = _XS_BLOCKS
_C_BLOCK0 = _XS_BLOCKS + SSM_GROUPS


def ssd_fwd(xbc_act, dt, acum, *, gather=None):
    t = xbc_act.shape[0]
    nc = t // SSM_CHUNK
    c_ = SSM_CHUNK
    nq = SSM_PAIRS // SSD_PPS
    op = _Gather(*gather) if gather else None
    ng = len(gather[0]) if gather else 0

    def body(*refs):
        xs_ref, b_ref, c_ref, dt_ref, ac_ref = refs[:5]
        p_refs = refs[5:5 + ng]
        y_ref, s_ref = refs[5 + ng:7 + ng]
        got_refs = refs[7 + ng:7 + 2 * ng]
        s_scr = refs[7 + 2 * ng]
        sems = refs[8 + 2 * ng:]
        q = pl.program_id(1)
        if op:
            @pl.when((pl.program_id(0) == 0) & (q == 0))
            def _():
                op.start(p_refs, got_refs, *sems)

            @pl.when((pl.program_id(0) == nc - 1) & (q == nq - 1))
            def _():
                op.finish(p_refs, got_refs, *sems)

        mine = pl.ds(SSD_PPS * q, SSD_PPS)
        lanes = [pl.ds(r * 128, 128) for r in range(SSD_PPS)]

        @pl.when(pl.program_id(0) == 0)
        def _():
            s_scr[mine] = jnp.zeros((SSD_PPS, SSM_DSTATE, 128), F32)

        s2 = s_scr[mine]
        s_ref[...] = s2
        xs = jnp.stack([xs_ref[:, ln] for ln in lanes])
        y, s2_new = _ssd_chunk(xs, dt_ref[...], ac_ref[...], b_ref[...], c_ref[...], s2, SSD_PPS * q, _bdot_vjp)
        for r, ln in enumerate(lanes):
            y_ref[:, ln] = y[r]
        s_scr[mine] = s2_new

    grp = lambda q: q // (PAIRS_PER_GROUP // SSD_PPS)
    return pl.pallas_call(
        body, name="ssd_fwd", grid=(nc, SSM_PAIRS // SSD_PPS),
        in_specs=[pl.BlockSpec((c_, SSD_W), lambda c, q: (c, q)),
                  pl.BlockSpec((c_, 128), lambda c, q: (c, _B_BLOCK0 + grp(q))),
                  pl.BlockSpec((c_, 128), lambda c, q: (c, _C_BLOCK0 + grp(q))),
                  pl.BlockSpec((c_, DT_PAD), lambda c, q: (c, 0)),
                  pl.BlockSpec((c_, DT_PAD), lambda c, q: (c, 0))] + [ANY] * ng,
        out_specs=[pl.BlockSpec((c_, SSD_W), lambda c, q: (c, q)),
                   pl.BlockSpec((None, SSD_PPS, SSM_DSTATE, 128), lambda c, q: (c, q, 0, 0))] + [ANY] * ng,
        out_shape=[jax.ShapeDtypeStruct((t, SSM_DINNER), F32),
                   jax.ShapeDtypeStruct((nc, SSM_PAIRS, SSM_DSTATE, 128), F32)] + (op.out_shape if op else []),
        scratch_shapes=[pltpu.VMEM((SSM_PAIRS, SSM_DSTATE, 128), F32)] + (_sem_pair(op.n_sem) if op else []),
        compiler_params=_cp(("arbitrary", "arbitrary")),
    )(xbc_act, xbc_act, xbc_act, dt, acum, *(gather[0] if gather else ()))


def ssd_bwd(xbc_act, dt, acum, states, dy, dskip, *, exchange=None):
    t = xbc_act.shape[0]
    nc = t // SSM_CHUNK
    c_ = SSM_CHUNK
    rev = lambda c: nc - 1 - c
    nq = SSM_PAIRS // SSD_PPS
    op = _ChipExchange(exchange) if exchange else None
    ne = len(exchange) if exchange else 0

    def body(*refs):
        xs_ref, b_ref, c_ref, dt_ref, ac_ref, s_ref, dy_ref, sk_ref = refs[:8]
        ex_refs = refs[8:8 + ne]
        dxs_ref, db_ref, dc_ref, ddt_ref, dac_ref = refs[8 + ne:13 + ne]
        got_refs = refs[13 + ne:13 + 2 * ne]
        ds_scr = refs[13 + 2 * ne]
        sems = refs[14 + 2 * ne:]
        q = pl.program_id(1)
        if op:
            @pl.when((pl.program_id(0) == 0) & (q == 0))
            def _():
                op.start(ex_refs, got_refs, *sems)

            @pl.when((pl.program_id(0) == nc - 1) & (q == nq - 1))
            def _():
                op.finish(ex_refs, got_refs, *sems)

        assert SSD_PPS == PAIRS_PER_GROUP
        mine = pl.ds(SSD_PPS * q, SSD_PPS)
        lanes = [pl.ds(r * 128, 128) for r in range(SSD_PPS)]

        @pl.when(pl.program_id(0) == 0)
        def _():
            ds_scr[mine] = jnp.zeros((SSD_PPS, SSM_DSTATE, 128), F32)

        fn = functools.partial(_ssd_chunk, pair0=SSD_PPS * q, dot=_bdot_vjp)
        xs = jnp.stack([xs_ref[:, ln] for ln in lanes])
        dy = jnp.stack([dy_ref[:, ln] for ln in lanes])
        _, vjp = jax.vjp(fn, xs, dt_ref[...], ac_ref[...], b_ref[...], c_ref[...], s_ref[...])
        dxs, ddt, dac, db, dc, ds = vjp((dy, ds_scr[mine]))
        for r, ln in enumerate(lanes):
            dxs_ref[:, ln] = dxs[r] + sk_ref[:, ln]
        ds_scr[mine] = ds
        db_ref[...] = db
        dc_ref[...] = dc

        @pl.when(q == 0)
        def _():
            ddt_ref[...] = ddt
            dac_ref[...] = dac

        @pl.when(q != 0)
        def _():
            ddt_ref[...] += ddt
            dac_ref[...] += dac

    grp = lambda q: q // (PAIRS_PER_GROUP // SSD_PPS)
    return pl.pallas_call(
        body, name="ssd_bwd", grid=(nc, SSM_PAIRS // SSD_PPS),
        in_specs=[pl.BlockSpec((c_, SSD_W), lambda c, q: (rev(c), q)),
                  pl.BlockSpec((c_, 128), lambda c, q: (rev(c), _B_BLOCK0 + grp(q))),
                  pl.BlockSpec((c_, 128), lambda c, q: (rev(c), _C_BLOCK0 + grp(q))),
                  pl.BlockSpec((c_, DT_PAD), lambda c, q: (rev(c), 0)),
                  pl.BlockSpec((c_, DT_PAD), lambda c, q: (rev(c), 0)),
                  pl.BlockSpec((None, SSD_PPS, SSM_DSTATE, 128), lambda c, q: (rev(c), q, 0, 0)),
                  pl.BlockSpec((c_, SSD_W), lambda c, q: (rev(c), q)),
                  pl.BlockSpec((c_, SSD_W), lambda c, q: (rev(c), q))] + [ANY] * ne,
        out_specs=[pl.BlockSpec((c_, SSD_W), lambda c, q: (rev(c), q)),
                   pl.BlockSpec((c_, 128), lambda c, q: (rev(c), grp(q))),
                   pl.BlockSpec((c_, 128), lambda c, q: (rev(c), grp(q))),
                   pl.BlockSpec((c_, DT_PAD), lambda c, q: (rev(c), 0)),
                   pl.BlockSpec((c_, DT_PAD), lambda c, q: (rev(c), 0))] + [ANY] * ne,
        out_shape=[jax.ShapeDtypeStruct((t, SSM_DINNER), F32),
                   jax.ShapeDtypeStruct((t, SSM_GROUPS * SSM_DSTATE), F32),
                   jax.ShapeDtypeStruct((t, SSM_GROUPS * SSM_DSTATE), F32),
                   jax.ShapeDtypeStruct((t, DT_PAD), F32),
                   jax.ShapeDtypeStruct((t, DT_PAD), F32)] + (op.out_shape if op else []),
        scratch_shapes=[pltpu.VMEM((SSM_PAIRS, SSM_DSTATE, 128), F32)] + (_sem_pair(op.n_sem) if op else []),
        compiler_params=_cp(("arbitrary", "arbitrary")),
    )(xbc_act, xbc_act, xbc_act, dt, acum, states, dy, dskip, *(exchange or ()))


def rowwise(name, fn, row_ins, par_ins, row_outs, acc_outs, *, tt, ncb=1, rb=None):
    t = row_ins[0][0].shape[0]
    assert t % tt == 0
    n_ri, n_pi, n_ro, n_ao = len(row_ins), len(par_ins), len(row_outs), len(acc_outs)
    rb = tt if rb is None else rb
    assert tt % rb == 0

    def body(*refs):
        i = pl.program_id(1)
        ro_refs = refs[n_ri + n_pi:n_ri + n_pi + n_ro]
        ao_refs = refs[n_ri + n_pi + n_ro:]
        pars = [r[...] for r in refs[n_ri:n_ri + n_pi]]
        accs = None
        for blk in range(tt // rb):
            rows = pl.ds(blk * rb, rb)
            outs = fn(*[r[rows, :] for r in refs[:n_ri]], *pars)
            for r, v in zip(ro_refs, outs[:n_ro]):
                r[rows, :] = v.astype(r.dtype)
            accs = list(outs[n_ro:]) if accs is None else [a + v for a, v in zip(accs, outs[n_ro:])]
        for r, v in zip(ao_refs, accs):
            @pl.when(i == 0)
            def _(r=r, v=v):
                r[...] = v

            @pl.when(i > 0)
            def _(r=r, v=v):
                r[...] += v

    in_specs = [pl.BlockSpec((tt, bc), lambda j, i, off=off: (i, off + j)) for _, bc, off in row_ins]
    in_specs += [pl.BlockSpec((a.shape[0], bc), lambda j, i, off=off: (0, off + j)) for a, bc, off in par_ins]
    out_specs = [pl.BlockSpec((tt, bc), lambda j, i: (i, j)) for _, bc, _ in row_outs]
    out_specs += [pl.BlockSpec((r, bc), lambda j, i: (0, j)) for r, _, bc in acc_outs]
    out_shape = [jax.ShapeDtypeStruct((t, c), dt) for c, _, dt in row_outs]
    out_shape += [jax.ShapeDtypeStruct((r, c), F32) for r, c, _ in acc_outs]
    return pl.pallas_call(
        body, name=name, grid=(ncb, t // tt), in_specs=in_specs, out_specs=out_specs, out_shape=out_shape,
        compiler_params=_cp(("parallel", "arbitrary")),
    )(*[a for a, _, _ in row_ins], *[a for a, _, _ in par_ins])


def _colsum(v):
    return jnp.sum(v, axis=0, keepdims=True)


def _softplus(x):
    return jnp.maximum(x, 0.0) + jnp.log(1.0 + jnp.exp(-jnp.abs(x)))


def _gelu_tanh(x):
    return 0.5 * x * (1.0 + jnp.tanh(0.7978845608028654 * (x + 0.044715 * (x * x * x))))


D = D_MODEL
ROW_RB = 16


def norm_fwd(x, w):
    return rowwise("norm_fwd", lambda xv, wv: (_rms(xv, wv),), [(x, D, 0)], [(w, D, 0)], [(D, D, BF16)], [], tt=256, rb=ROW_RB)[0]


def norm_bwd(x, w, dh, dres):
    def fn(xv, dhv, drv, wv):
        _, vjp = jax.vjp(_rms, xv, wv)
        dx, dw = vjp(dhv)
        return dx + drv, dw
    return rowwise("norm_bwd", fn, [(x, D, 0), (dh, D, 0), (dres, D, 0)], [(w, D, 0)], [(D, D, F32)], [(1, D, D)], tt=256, rb=ROW_RB)


def _dt_fn(dtr, bias, a_log):
    c = dtr.shape[0]
    dt = _softplus(dtr + bias)
    da = dt * (-jnp.exp(a_log))
    tril = jnp.where(_iota((c, c), 1) <= _iota((c, c), 0), 1.0, 0.0).astype(F32)
    return dt, _hdot(tril, da)


def dt_fwd(dtr, bias, a_log):
    return rowwise("dt_fwd", _dt_fn, [(dtr, DT_PAD, 0)], [(bias, DT_PAD, 0), (a_log, DT_PAD, 0)],
                   [(DT_PAD, DT_PAD, F32), (DT_PAD, DT_PAD, F32)], [], tt=SSM_CHUNK)


def dt_bwd(dtr, bias, a_log, ddt, dacum):
    def fn(dtrv, ddtv, dacv, bv, av):
        _, vjp = jax.vjp(_dt_fn, dtrv, bv, av)
        return vjp((ddtv, dacv))
    return rowwise("dt_bwd", fn, [(dtr, DT_PAD, 0), (ddt, DT_PAD, 0), (dacum, DT_PAD, 0)],
                   [(bias, DT_PAD, 0), (a_log, DT_PAD, 0)],
                   [(DT_PAD, DT_PAD, BF16)], [(1, DT_PAD, DT_PAD), (1, DT_PAD, DT_PAD)], tt=SSM_CHUNK)


GROUP_W = SSM_DINNER // SSM_GROUPS


def _ssm_post_fn(yv, xsv, zv, dexp, nw):
    return _rms((yv + dexp * xsv) * _silu(zv), nw)


def ssm_post_fwd(yssd, xbc_act, z, dexp, nw):
    return rowwise("ssm_post_fwd", lambda *a: (_ssm_post_fn(*a),),
                   [(yssd, GROUP_W, 0), (xbc_act, GROUP_W, 0), (z, GROUP_W, 0)], [(dexp, GROUP_W, 0), (nw, GROUP_W, 0)],
                   [(SSM_DINNER, GROUP_W, BF16)], [], tt=512, ncb=SSM_GROUPS)[0]


def ssm_post_bwd(yssd, xbc_act, z, dexp, nw, dy):
    def fn(yv, xsv, zv, dyv, dv, nv):
        _, vjp = jax.vjp(_ssm_post_fn, yv, xsv, zv, dv, nv)
        return vjp(dyv.astype(F32))
    return rowwise("ssm_post_bwd", fn,
                   [(yssd, GROUP_W, 0), (xbc_act, GROUP_W, 0), (z, GROUP_W, 0), (dy, GROUP_W, 0)],
                   [(dexp, GROUP_W, 0), (nw, GROUP_W, 0)],
                   [(SSM_DINNER, GROUP_W, F32), (SSM_DINNER, GROUP_W, F32), (SSM_DINNER, GROUP_W, BF16)],
                   [(1, SSM_DINNER, GROUP_W), (1, SSM_DINNER, GROUP_W)], tt=512, ncb=SSM_GROUPS)


def _merge_fn(ah, asm, gh, gs):
    return _sigmoid(gh) * ah + _sigmoid(gs) * asm


def merge_fwd(a_hg, a_ssm, gates):
    f32 = lambda vals: [v.astype(F32) for v in vals]
    return rowwise("merge_fwd", lambda *a: (_merge_fn(*f32(a)),), [(a_hg, D, 0), (a_ssm, D, 0), (gates, D, 0), (gates, D, 1)], [],
                   [(D, D, BF16)], [], tt=256, rb=ROW_RB)[0]


def merge_bwd(a_hg, a_ssm, gates, dmixed):
    def fn(ah, asm, gh, gs, dm):
        _, vjp = jax.vjp(_merge_fn, *[v.astype(F32) for v in (ah, asm, gh, gs)])
        return vjp(dm.astype(F32))
    return rowwise("merge_bwd", fn, [(a_hg, D, 0), (a_ssm, D, 0), (gates, D, 0), (gates, D, 1), (dmixed, D, 0)], [],
                   [(D, D, BF16)] * 4, [], tt=256, rb=ROW_RB)


def _post1_fn(xv, uv, wpost, wpre):
    x1 = xv + _rms(uv, wpost)
    return x1, _rms(x1, wpre)


def post1_fwd(x, u, wpost, wpre):
    return rowwise("post1_fwd", _post1_fn, [(x, D, 0), (u, D, 0)], [(wpost, D, 0), (wpre, D, 0)],
                   [(D, D, F32), (D, D, BF16)], [], tt=256, rb=ROW_RB)


def post1_bwd(x, u, wpost, wpre, dx1, dh2):
    def fn(xv, uv, d1, d2, wa, wb):
        _, vjp = jax.vjp(_post1_fn, xv, uv, wa, wb)
        dx, du, dwa, dwb = vjp((d1, d2))
        return du, dx, dwa, dwb
    return rowwise("post1_bwd", fn, [(x, D, 0), (u, D, 0), (dx1, D, 0), (dh2, D, 0)], [(wpost, D, 0), (wpre, D, 0)],
                   [(D, D, BF16), (D, D, F32)], [(1, D, D), (1, D, D)], tt=256, rb=ROW_RB)


def final_fwd_bwd(x1, fo, w, target):
    def fn(x1v, fov, tv, wv):
        def loss_fn(a, b, c):
            err = a + _rms(b, c) - tv
            return 0.5 * jnp.sum(err * err) * (1.0 / D)
        loss, vjp = jax.vjp(loss_fn, x1v, fov, wv)
        dx, dfo, dw = vjp(jnp.ones((), F32))
        return dx, dfo, dw, jnp.full((1, 128), loss, F32)
    return rowwise("final_fwd_bwd", fn, [(x1, D, 0), (fo, D, 0), (target, D, 0)], [(w, D, 0)],
                   [(D, D, F32), (D, D, BF16)], [(1, D, D), (1, 128, 128)], tt=256, rb=ROW_RB)


HALO = 8
CONV_TT = 512
CONV_CB = 512
CONV_RB = 32


def _tail(kind, c, up):
    return _silu(c) if kind == "silu" else _gelu_tanh(c) * up


def conv_fwd(name, x, xoff, w, b, kind, up=None, upoff=0, act_dtype=F32):
    t = x.shape[0]
    k_, c_ = w.shape
    tt, cb = CONV_TT, CONV_CB
    hb = tt // HALO
    has_up = up is not None

    def body(*refs):
        if has_up:
            x_ref, xp_ref, w_ref, b_ref, up_ref, c_ref, a_ref, scr = refs
        else:
            x_ref, xp_ref, w_ref, b_ref, c_ref, a_ref, scr = refs
        i = pl.program_id(1)
        scr[0:HALO, :] = jnp.where(i == 0, 0.0, xp_ref[...])
        scr[HALO:HALO + tt, :] = x_ref[...]
        for r in range(tt // CONV_RB):
            rows = pl.ds(r * CONV_RB, CONV_RB)
            acc = jnp.zeros((CONV_RB, cb), F32) + b_ref[...]
            for k in range(k_):
                acc = acc + w_ref[k:k + 1, :] * scr[pl.ds(r * CONV_RB + HALO - (k_ - 1) + k, CONV_RB), :]
            c_ref[rows, :] = acc
            a_ref[rows, :] = _tail(kind, acc, up_ref[rows, :] if has_up else None).astype(act_dtype)

    in_specs = [pl.BlockSpec((tt, cb), lambda j, i: (i, xoff + j)),
                pl.BlockSpec((HALO, cb), lambda j, i: (jnp.maximum(i * hb - 1, 0), xoff + j)),
                pl.BlockSpec((k_, cb), lambda j, i: (0, j)),
                pl.BlockSpec((1, cb), lambda j, i: (0, j))]
    args = [x, x, w, b]
    if has_up:
        in_specs.append(pl.BlockSpec((tt, cb), lambda j, i: (i, upoff + j)))
        args.append(up)
    return pl.pallas_call(
        body, name=name, grid=(c_ // cb, t // tt), in_specs=in_specs,
        out_specs=[pl.BlockSpec((tt, cb), lambda j, i: (i, j))] * 2,
        out_shape=[jax.ShapeDtypeStruct((t, c_), F32), jax.ShapeDtypeStruct((t, c_), act_dtype)],
        scratch_shapes=[pltpu.VMEM((tt + HALO, cb), F32)],
        compiler_params=_cp(("parallel", "arbitrary")),
    )(*args)


def conv_bwd(name, x, xoff, c, coff, dact, w, kind, up=None, upoff=0):
    t = x.shape[0]
    k_, c_ = w.shape[0], dact.shape[1]
    tt, cb = CONV_TT, CONV_CB
    hb = tt // HALO
    nt = t // tt
    has_up = up is not None

    def tail_grad(cv, dav, upv):
        if has_up:
            _, vjp = jax.vjp(lambda a, u: _tail(kind, a, u), cv, upv)
            return vjp(dav)
        _, vjp = jax.vjp(lambda a: _tail(kind, a, None), cv)
        return vjp(dav)[0], None

    def body(*refs):
        if has_up:
            (x_ref, xp_ref, c_ref, cn_ref, da_ref, dan_ref, w_ref, up_ref, upn_ref,
             dx_ref, dup_ref, dw_ref, db_ref, xs, dcs) = refs
        else:
            x_ref, xp_ref, c_ref, cn_ref, da_ref, dan_ref, w_ref, dx_ref, dw_ref, db_ref, xs, dcs = refs
        i = pl.program_id(1)
        xs[0:HALO, :] = jnp.where(i == 0, 0.0, xp_ref[...])
        xs[HALO:HALO + tt, :] = x_ref[...]
        rb = CONV_RB
        for r in range(tt // rb):
            rows = pl.ds(r * rb, rb)
            dc, dup = tail_grad(c_ref[rows, :], da_ref[rows, :].astype(F32), up_ref[rows, :] if has_up else None)
            dcs[rows, :] = dc
            if has_up:
                dup_ref[rows, :] = dup.astype(BF16)
        dcn, _ = tail_grad(cn_ref[...], dan_ref[...].astype(F32), upn_ref[...] if has_up else None)
        dcs[tt:tt + HALO, :] = jnp.where(i == nt - 1, 0.0, dcn)
        dws = [jnp.zeros((1, cb), F32) for _ in range(k_)]
        dbv = jnp.zeros((1, cb), F32)
        for r in range(tt // rb):
            rows = pl.ds(r * rb, rb)
            dc = dcs[rows, :]
            dx = jnp.zeros((rb, cb), F32)
            for k in range(k_):
                dx = dx + w_ref[k:k + 1, :] * dcs[pl.ds(r * rb + k_ - 1 - k, rb), :]
                dws[k] = dws[k] + _colsum(dc * xs[pl.ds(r * rb + HALO - (k_ - 1) + k, rb), :])
            dbv = dbv + _colsum(dc)
            dx_ref[rows, :] = dx.astype(BF16)

        @pl.when(i == 0)
        def _():
            dw_ref[...] = jnp.zeros_like(dw_ref)
            db_ref[...] = jnp.zeros_like(db_ref)

        for k in range(k_):
            dw_ref[k:k + 1, :] += dws[k]
        db_ref[...] += dbv

    tile = lambda off: pl.BlockSpec((tt, cb), lambda j, i, off=off: (i, off + j))
    prev = lambda off: pl.BlockSpec((HALO, cb), lambda j, i, off=off: (jnp.maximum(i * hb - 1, 0), off + j))
    nxt = lambda off: pl.BlockSpec((HALO, cb), lambda j, i, off=off: (jnp.minimum((i + 1) * hb, t // HALO - 1), off + j))
    in_specs = [tile(xoff), prev(xoff), tile(coff), nxt(coff), tile(0), nxt(0),
                pl.BlockSpec((k_, cb), lambda j, i: (0, coff + j))]
    args = [x, x, c, c, dact, dact, w]
    if has_up:
        in_specs += [tile(upoff), nxt(upoff)]
        args += [up, up]
    out_specs = [tile(0)] + ([tile(0)] if has_up else []) + [pl.BlockSpec((HALO, cb), lambda j, i: (0, j)),
                                                            pl.BlockSpec((1, cb), lambda j, i: (0, j))]
    out_shape = [jax.ShapeDtypeStruct((t, c_), BF16)] * (2 if has_up else 1)
    out_shape += [jax.ShapeDtypeStruct((HALO, c_), F32), jax.ShapeDtypeStruct((1, c_), F32)]
    return pl.pallas_call(
        body, name=name, grid=(c_ // cb, nt), in_specs=in_specs, out_specs=out_specs, out_shape=out_shape,
        scratch_shapes=[pltpu.VMEM((tt + HALO, cb), F32), pltpu.VMEM((tt + HALO, cb), F32)],
        compiler_params=_cp(("parallel", "arbitrary")),
    )(*args)


def ew_sum(name, parts, rows, out_dtype, tr):
    c = parts[0][0].shape[1]
    tr = min(tr, rows)
    assert rows % tr == 0 and all(off % tr == 0 for _, off in parts)
    n = len(parts)

    def body(*refs):
        acc = refs[0][...].astype(F32)
        for ref in refs[1:n]:
            acc = acc + ref[...].astype(F32)
        refs[n][...] = acc.astype(out_dtype)

    in_specs = [pl.BlockSpec((tr, c), lambda i, o=off // tr: (i + o, 0)) for _, off in parts]
    return pl.pallas_call(body, name=name, grid=(rows // tr,), in_specs=in_specs,
                          out_specs=pl.BlockSpec((tr, c), lambda i: (i, 0)),
                          out_shape=jax.ShapeDtypeStruct((rows, c), out_dtype),
                          compiler_params=_cp(("parallel",)))(*[a for a, _ in parts])


def fold_heads(dexp):
    def body(d_ref, o_ref):
        sel = jnp.where(_iota((SSM_DINNER, DT_PAD), 0) // SSM_HEADDIM == _iota((SSM_DINNER, DT_PAD), 1), 1.0, 0.0)
        o_ref[...] = _hdot(jnp.broadcast_to(d_ref[...], (8, SSM_DINNER)), sel.astype(F32), "nn", "a")[0:1, :]

    return pl.pallas_call(body, name="fold_heads", out_shape=jax.ShapeDtypeStruct((1, DT_PAD), F32),
                          compiler_params=pltpu.CompilerParams(vmem_limit_bytes=VMEM_LIMIT))(dexp)


def adamw(name, w, g, m, v, tr):
    r, c = w.shape
    tr = min(tr, r)
    assert r % tr == 0, (r, tr)

    def body(w_ref, g_ref, m_ref, v_ref, d_ref, nm_ref, nv_ref):
        gv = g_ref[...]
        nm = ADAM_B1 * m_ref[...] + (1.0 - ADAM_B1) * gv
        nv = ADAM_B2 * v_ref[...] + (1.0 - ADAM_B2) * (gv * gv)
        m_hat = nm / (1.0 - ADAM_B1 ** ADAM_STEP)
        v_hat = nv / (1.0 - ADAM_B2 ** ADAM_STEP)
        d_ref[...] = -ADAM_LR * (m_hat / (jnp.sqrt(v_hat) + ADAM_EPS) + ADAM_WD * w_ref[...])
        nm_ref[...] = nm
        nv_ref[...] = nv

    spec = pl.BlockSpec((tr, c), lambda i: (i, 0))
    shp = jax.ShapeDtypeStruct((r, c), F32)
    return pl.pallas_call(body, name=name, grid=(r // tr,), in_specs=[spec] * 4, out_specs=[spec] * 3,
                          out_shape=[shp] * 3, compiler_params=_cp(("parallel",)))(w, g, m, v)


SEG_QFIG, SEG_Z, SEG_XBC, SEG_DT, SEG_G = 0, 8192, 12288, 18432, 18496
IN_TOTAL = 22592
FFN_BLOCKS = D_FF // CONV_CB


def _own_slot(gathered, own, shard):
    slot = lax.broadcasted_iota(jnp.int32, (gathered.shape[0],) + (1,) * own.ndim, 0)
    return jnp.where(slot == shard, own[None], gathered)


def local_step(x, target, wts, par, p_rest, p_up, shard, core):
    t = x.shape[0]
    pad64 = lambda a: jnp.pad(a, ((0, 0), (0, DT_PAD - a.shape[1])))
    bias, a_log = pad64(par["ssm_dt_bias"]), pad64(par["ssm_A_log"])
    dexp = jnp.repeat(par["ssm_D"], SSM_HEADDIM, axis=1)
    in_t = wts["in_t"]

    h = norm_fwd(x, par["mix_pre_norm"])
    proj = lambda nm, off, n, tn: mm(h, in_t, "nt", name=nm, tn=tn, dims=(t, n, D), b_off=(off, 0))
    qfig = proj("proj_qfig", SEG_QFIG, 8192, 1024)
    z = proj("proj_z", SEG_Z, 4096, 1024)
    xbc = proj("proj_xbc", SEG_XBC, 6144, 1024)
    dtr = mm(h, wts["dt_t"], "nt", name="proj_dt", tn=128)
    gates = mm(h, wts["g_t"], "nt", name="proj_gates", out_dtype=BF16, tn=1024)
    y_hg, hg_states, g_rest = hgrn2_fwd(qfig, par["hg_lb_table"], par["hg_out_norm"], gather=([p_rest], [REST_PIECES]))
    c_ssm, xbc_act = conv_fwd("ssm_conv_fwd", xbc, 0, par["ssm_conv_w"], par["ssm_conv_b"], "silu")
    dt, acum = dt_fwd(dtr, bias, a_log)
    yssd, ssd_states, g_up = ssd_fwd(xbc_act, dt, acum, gather=([p_up], [UP_PIECES]))
    g_rest, g_up = _own_slot(g_rest, p_rest, shard), _own_slot(g_up, p_up, shard)
    r0, r1, r2, r3 = REST_SPLITS
    wts = dict(wts, bh=g_rest[:, :r0].reshape(-1, D), bs=g_rest[:, r0:r1].reshape(-1, D), o=g_rest[:, r1:r2].reshape(-1, D),
               dn=g_rest[:, r2:r3].reshape(-1, D), up=jnp.transpose(g_up, (1, 0, 2)).reshape(D, 2 * D_FF))
    y_ssm = ssm_post_fwd(yssd, xbc_act, z, dexp, par["ssm_out_norm"])
    a_hg = mm(y_hg, wts["bh"], "nn", name="branch_hg", out_dtype=BF16, tn=1024)
    a_ssm = mm(y_ssm, wts["bs"], "nn", name="branch_ssm", out_dtype=BF16, tn=1024)
    mixed = merge_fwd(a_hg, a_ssm, gates)
    u = mm(mixed, wts["o"], "nn", name="out_proj", tn=1024)
    x1, h2 = post1_fwd(x, u, par["mix_post_norm"], par["ffn_pre_norm"])
    gu = mm(h2, wts["up"], "nn", name="ffn_up", tn=1024)
    c_ffn, act = conv_fwd("ffn_conv_fwd", gu, 0, par["ffn_conv_w"], par["ffn_conv_b"], "gelu_mul",
                          up=gu, upoff=FFN_BLOCKS, act_dtype=BF16)
    fo = mm(act, wts["dn"], "nn", name="ffn_down", tm=512, tn=1024)
    dx2, dfo, g_ffn_post, loss = final_fwd_bwd(x1, fo, par["ffn_post_norm"], target)

    dact = mm(dfo, wts["dn"], "nt", name="d_act", out_dtype=BF16, tn=1408)
    g_dn = mm(act, dfo, "tn", name="g_ffn_down", out_dtype=BF16, tm=1408, tn=1024, tk=2048)
    dgate, dup, g_fcw, g_fcb = conv_bwd("ffn_conv_bwd", gu, 0, c_ffn, 0, dact, par["ffn_conv_w"], "gelu_mul",
                                        up=gu, upoff=FFN_BLOCKS)
    dh2 = mm_segments("d_h2", [(dgate, 0, 0), (dup, 0, D_FF)], [wts["up"]], tm=1024, tn=1024, tk=1408, bt=True)
    g_up_gate = mm(h2, dgate, "tn", name="g_ffn_up_gate", out_dtype=BF16, tm=1024, tn=1408, tk=2048)
    g_up_up = mm(h2, dup, "tn", name="g_ffn_up_up", out_dtype=BF16, tm=1024, tn=1408, tk=2048)
    du, dx1, g_mix_post, g_ffn_pre = post1_bwd(x, u, par["mix_post_norm"], par["ffn_pre_norm"], dx2, dh2)
    dmixed = mm(du, wts["o"], "nt", name="d_mixed", tn=1024)
    g_o = mm(mixed, du, "tn", name="g_w_out", out_dtype=BF16, tm=1024, tn=2048, tk=2048)
    da_hg, da_ssm, dg_hg, dg_ssm = merge_bwd(a_hg, a_ssm, gates, dmixed)
    dy_hg = mm(da_hg, wts["bh"], "nt", name="d_y_hg", out_dtype=BF16, tn=1024)
    g_bh = mm(y_hg, da_hg, "tn", name="g_w_branch_hg", out_dtype=BF16, tm=1024, tn=2048, tk=2048)
    dy_ssm = mm(da_ssm, wts["bs"], "nt", name="d_y_ssm", out_dtype=BF16, tn=1024)
    g_bs = mm(y_ssm, da_ssm, "tn", name="g_w_branch_ssm", out_dtype=BF16, tm=1024, tn=2048, tk=2048)
    dyssd, dskip, dz, g_dexp, g_ssm_norm = ssm_post_bwd(yssd, xbc_act, z, dexp, par["ssm_out_norm"], dy_ssm)

    gg_rest = jnp.concatenate([g.reshape(N_CHIPS, -1, D) for g in (g_bh, g_bs, g_o, g_dn)], axis=1)
    gg_up = jnp.transpose(jnp.concatenate([g_up_gate, g_up_up], axis=1).reshape(D, N_CHIPS, UP_COLS), (1, 0, 2))
    c_rest, c_up = pair_reduce("rest", [gg_rest, gg_up], [REST_PIECES, UP_PIECES], [432, 512], core)
    dxs, db_, dc_, ddt, dacum, rb_rest, rb_up = ssd_bwd(xbc_act, dt, acum, ssd_states, dyssd, dskip, exchange=[c_rest, c_up])
    red_rest, red_up = chip_reduce("rest", [c_rest, c_up], [rb_rest, rb_up], [432, 256], shard)
    ddtr, g_dt_bias, g_a_log = dt_bwd(dtr, bias, a_log, ddt, dacum)
    xs_blocks, bc_blocks = SSM_DINNER // CONV_CB, SSM_GROUPS * SSM_DSTATE // CONV_CB
    dxbc_x, g_cw_x, g_cb_x = conv_bwd("ssm_conv_bwd_x", xbc, 0, c_ssm, 0, dxs, par["ssm_conv_w"], "silu")
    dxbc_b, g_cw_b, g_cb_b = conv_bwd("ssm_conv_bwd_b", xbc, xs_blocks, c_ssm, xs_blocks, db_, par["ssm_conv_w"], "silu")
    dxbc_c, g_cw_c, g_cb_c = conv_bwd("ssm_conv_bwd_c", xbc, xs_blocks + bc_blocks, c_ssm, xs_blocks + bc_blocks, dc_,
                                      par["ssm_conv_w"], "silu")
    dq, df, dv, dg, g_table, g_hg_norm = hgrn2_bwd(qfig, par["hg_lb_table"], par["hg_out_norm"], hg_states, dy_hg)

    dsegs = [(dq, SEG_QFIG), (df, SEG_QFIG + 2048), (dv, SEG_QFIG + 4096), (dg, SEG_QFIG + 6144), (dz, SEG_Z),
             (dxbc_x, SEG_XBC), (dxbc_b, SEG_XBC + SSM_DINNER), (dxbc_c, SEG_XBC + SSM_DINNER + 1024)]
    g_in_parts = [mm(dseg, h, "tn", name=f"g_w_in_{n}", out_dtype=BF16, tm=1024, tn=2048, tk=2048)
                  for n, (dseg, _) in enumerate(dsegs)]
    g_dt_t = mm(ddtr, h, "tn", name="g_w_in_dt", out_dtype=BF16, tm=128, tn=2048, tk=1024)[:SSM_HEADS]
    g_in_parts += [mm(dgate_, h, "tn", name=f"g_w_in_g{n}", out_dtype=BF16, tm=1024, tn=2048, tk=2048)
                   for n, dgate_ in enumerate((dg_hg, dg_ssm))]
    zpad = jnp.zeros((N_CHIPS, IN_ROWS - IN_SHARD, D), BF16)
    g_in_t = jnp.concatenate(g_in_parts[:8] + [g_dt_t] + g_in_parts[8:], axis=0).reshape(N_CHIPS, IN_SHARD, D)
    (c_in,) = pair_reduce("in", [jnp.concatenate([g_in_t, zpad], axis=1)], [IN_PIECES], [960], core)
    dh = mm_segments("d_h_a", [(dseg, 0, off) for dseg, off in dsegs[:5]], [in_t], tm=1024, tn=1024, tk=1024)
    dh, rb_in = mm_segments("d_h_b", [(dseg, 0, off) for dseg, off in dsegs[5:]] + [(ddtr, 1, 0), (dg_hg, 2, 0), (dg_ssm, 2, D)],
                            [in_t, wts["dt_t"], wts["g_t"]], tm=1024, tn=1024, tk=1024, acc=dh, exchange=[c_in])
    (red_in,) = chip_reduce("in", [c_in], [rb_in], [480], shard)
    grad_x, g_mix_pre = norm_bwd(x, par["mix_pre_norm"], dh, dx1)

    big = dict(in_t=red_in, rest=red_rest, up=red_up)
    g_conv_w = jnp.concatenate([g_cw_x, g_cw_b, g_cw_c], axis=1)[:SSM_CONV]
    g_conv_b = jnp.concatenate([g_cb_x, g_cb_b, g_cb_c], axis=1)
    small = dict(mix_pre_norm=g_mix_pre, mix_post_norm=g_mix_post, hg_lb_table=g_table, hg_out_norm=g_hg_norm,
                 ssm_conv_w=g_conv_w, ssm_conv_b=g_conv_b, ssm_dt_bias=g_dt_bias, ssm_A_log=g_a_log,
                 ssm_D=g_dexp, ssm_out_norm=g_ssm_norm, ffn_pre_norm=g_ffn_pre, ffn_post_norm=g_ffn_post,
                 ffn_conv_w=g_fcw[:FFN_CONV], ffn_conv_b=g_fcb)
    return loss, grad_x, big, small


MESH = pl.DeviceIdType.MESH
ANY = pl.BlockSpec(memory_space=pl.ANY)
N_CHIPS = 4
IN_SHARD = 5648
IN_ROWS = 5760
REST_SPLITS = (512, 1536, 2048, 3456)
UP_COLS = 2816
IN_PIECES, REST_PIECES, UP_PIECES = 3, 4, 4


def _place():
    x, y, c = lax.axis_index("x"), lax.axis_index("y"), lax.axis_index("c")
    chips = [(1 - x, y), (x, 1 - y), (1 - x, 1 - y)]
    return x, y, c, chips


def _rcopy(src, dst, send_sems, recv_sems, k, dev):
    return pltpu.make_async_remote_copy(src_ref=src, dst_ref=dst, send_sem=send_sems.at[k], recv_sem=recv_sems.at[k],
                                        device_id=dev, device_id_type=MESH)


def _pieces(rows, n):
    assert rows % n == 0 and (rows // n) % 16 == 0, (rows, n)
    return [(k * (rows // n), rows // n) for k in range(n)]


def _rows(c, hrows, piece):
    return pl.ds(pl.multiple_of(c * hrows + piece[0], 16), piece[1])


def _half_plan(arrays, pieces):
    return [(a.shape[-2] // 2, _pieces(a.shape[-2] // 2, n)) for a, n in zip(arrays, pieces)]


def _sem_pair(n):
    return [pltpu.SemaphoreType.DMA((n,)), pltpu.SemaphoreType.DMA((n,))]


class _Gather:
    def __init__(self, ps, pieces):
        self.plan = _half_plan(ps, pieces)
        self.n_sem = sum(2 * 3 * len(pcs) for _, pcs in self.plan)
        self.out_shape = [jax.ShapeDtypeStruct((N_CHIPS,) + p.shape, p.dtype) for p in ps]

    def _copies(self, p_refs, g_refs, send_sems, recv_sems, only_first=False):
        x, y, c, chips = _place()
        own = 2 * x + y
        sib = (x, y, 1 - c)
        first, arrive, passed, from_sib = [], [], [], []
        k = 0
        for p, g, (hrows, pcs) in zip(p_refs, g_refs, self.plan):
            for chip in chips:
                theirs = 2 * chip[0] + chip[1]
                for pc in pcs:
                    mine, other = _rows(c, hrows, pc), _rows(1 - c, hrows, pc)
                    first.append(_rcopy(p.at[mine], g.at[own, mine], send_sems, recv_sems, k, (*chip, c)))
                    if not only_first:
                        arrive.append(_rcopy(g.at[theirs, mine], g.at[theirs, mine], send_sems, recv_sems, k, (*chip, c)))
                        passed.append(_rcopy(g.at[theirs, mine], g.at[theirs, mine], send_sems, recv_sems, k + 1, sib))
                        from_sib.append(_rcopy(g.at[theirs, other], g.at[theirs, other], send_sems, recv_sems, k + 1, sib))
                    k += 2
        return first, arrive, passed, from_sib

    def start(self, p_refs, g_refs, send_sems, recv_sems):
        for cp in self._copies(p_refs, g_refs, send_sems, recv_sems, only_first=True)[0]:
            cp.start()

    def finish(self, p_refs, g_refs, send_sems, recv_sems):
        first, arrive, passed, from_sib = self._copies(p_refs, g_refs, send_sems, recv_sems)
        for got, fw in zip(arrive, passed):
            got.wait_recv()
            fw.start()
        for cp in from_sib:
            cp.wait_recv()
        for cp in first + passed:
            cp.wait_send()


def gather_weights(name, ps, pieces):
    op = _Gather(ps, pieces)
    n = len(ps)

    def body(*refs):
        p_refs, g_refs, sems = refs[:n], refs[n:2 * n], refs[2 * n:]
        op.start(p_refs, g_refs, *sems)
        op.finish(p_refs, g_refs, *sems)

    return pl.pallas_call(body, name=name, in_specs=[ANY] * n, out_specs=[ANY] * n, out_shape=op.out_shape,
                          scratch_shapes=_sem_pair(op.n_sem))(*ps)


def pair_exchange(name, gs, pieces):
    plan = _half_plan(gs, pieces)
    n_sem = sum(N_CHIPS * len(pcs) for _, pcs in plan)
    n = len(gs)

    def body(*refs):
        g_refs, r_refs, send_sems, recv_sems = refs[:n], refs[n:2 * n], refs[2 * n], refs[2 * n + 1]
        x, y, c, _ = _place()
        sib = (x, y, 1 - c)
        cps = []
        for g, r, (hrows, pcs) in zip(g_refs, r_refs, plan):
            for s in range(N_CHIPS):
                for pc in pcs:
                    cps.append(_rcopy(g.at[s, _rows(1 - c, hrows, pc)], r.at[s, pl.ds(pc[0], pc[1])],
                                      send_sems, recv_sems, len(cps), sib))
        for cp in cps:
            cp.start()
        for cp in cps:
            cp.wait()

    return pl.pallas_call(
        body, name=name, in_specs=[ANY] * n, out_specs=[ANY] * n,
        out_shape=[jax.ShapeDtypeStruct((N_CHIPS, g.shape[1] // 2, g.shape[2]), g.dtype) for g in gs],
        scratch_shapes=_sem_pair(n_sem))(*gs)


class _ChipExchange:
    def __init__(self, ss):
        self.n_sem = 3 * len(ss)
        self.out_shape = [jax.ShapeDtypeStruct((3,) + s.shape[1:], s.dtype) for s in ss]

    def _copies(self, s_refs, r_refs, send_sems, recv_sems):
        x, y, c, chips = _place()
        cps = []
        for s, r in zip(s_refs, r_refs):
            for j, chip in enumerate(chips):
                cps.append(_rcopy(s.at[2 * chip[0] + chip[1]], r.at[j], send_sems, recv_sems, len(cps), (*chip, c)))
        return cps

    def start(self, *refs):
        for cp in self._copies(*refs):
            cp.start()

    def finish(self, *refs):
        for cp in self._copies(*refs):
            cp.wait()


def pair_assemble(name, rs, pieces):
    plan = [(r.shape[0], _pieces(r.shape[0], n_)) for r, n_ in zip(rs, pieces)]
    n_sem = sum(len(pcs) for _, pcs in plan)
    n = len(rs)

    def body(*refs):
        r_refs, f_refs, send_sems, recv_sems = refs[:n], refs[n:2 * n], refs[2 * n], refs[2 * n + 1]
        x, y, c, _ = _place()
        sib = (x, y, 1 - c)
        cps, got = [], []
        for r, f, (hrows, pcs) in zip(r_refs, f_refs, plan):
            for pc in pcs:
                src = r.at[pl.ds(pc[0], pc[1])]
                cps.append(_rcopy(src, f.at[_rows(c, hrows, pc)], send_sems, recv_sems, len(cps), sib))
                got.append(_rcopy(src, f.at[_rows(1 - c, hrows, pc)], send_sems, recv_sems, len(got), sib))
        for cp in cps:
            cp.start()
        for cp in got:
            cp.wait_recv()
        for cp in cps:
            cp.wait_send()

    return pl.pallas_call(
        body, name=name, in_specs=[ANY] * n, out_specs=[ANY] * n,
        out_shape=[jax.ShapeDtypeStruct((2 * r.shape[0], r.shape[1]), r.dtype) for r in rs],
        scratch_shapes=_sem_pair(n_sem))(*rs)


def pair_reduce(tag, ggs, pieces, trs, core):
    recv = pair_exchange("pair_exchange_" + tag, ggs, pieces)
    flat = lambda a: a.reshape(-1, a.shape[-1])
    out = []
    for n, (gg, r, tr) in enumerate(zip(ggs, recv, trs)):
        h = gg.shape[1] // 2
        own = lax.dynamic_slice_in_dim(gg, core * h, h, axis=1)
        out.append(ew_sum(f"pair_sum_{tag}_{n}", [(flat(own), 0), (flat(r), 0)], N_CHIPS * h, BF16, tr).reshape(r.shape))
    return out


def chip_reduce(tag, cs, rbs, trs, shard):
    out = []
    for n, (c, rb, tr) in enumerate(zip(cs, rbs, trs)):
        h = c.shape[1]
        own = lax.dynamic_index_in_dim(c, shard, axis=0, keepdims=False)
        parts = [(own, 0)] + [(rb.reshape(-1, rb.shape[-1]), j * h) for j in range(3)]
        out.append(ew_sum(f"chip_sum_{tag}_{n}", parts, h, F32, tr))
    return out


N_DEV = 8


def gather_small(blk, reduce):
    rows, cols = blk.shape

    def body(x_ref, out_ref, all_ref, send_sems, recv_sems, local_sem):
        x, y, c, chips = _place()
        me, sib = (x, y, c), (x, y, 1 - c)

        def blk_rows(px, py, pc):
            return all_ref.at[pl.ds(pl.multiple_of((4 * px + 2 * py + pc) * rows, 8), rows), :]

        def copy(k, block, to, src=None):
            return _rcopy(blk_rows(*block) if src is None else src, blk_rows(*block), send_sems, recv_sems, k, to)

        mine = pltpu.make_async_copy(x_ref, blk_rows(*me), local_sem)
        mine.start()
        first = [copy(0, me, sib, src=x_ref)] + [copy(1 + j, me, (*chip, c), src=x_ref) for j, chip in enumerate(chips)]
        for cp in first:
            cp.start()
        passed = [copy(4 + j, (*chip, c), sib) for j, chip in enumerate(chips)]
        for j, chip in enumerate(chips):
            copy(1 + j, (*chip, c), me).wait_recv()
            passed[j].start()
        copy(0, sib, me).wait_recv()
        for j, chip in enumerate(chips):
            copy(4 + j, (*chip, 1 - c), me).wait_recv()
        for cp in first + passed:
            cp.wait_send()
        mine.wait()
        if reduce:
            acc = all_ref[0:rows, :]
            for d in range(1, N_DEV):
                acc = acc + all_ref[d * rows:(d + 1) * rows, :]
            out_ref[...] = acc
        else:
            out_ref[...] = all_ref[...]

    vmem = pl.BlockSpec(memory_space=pltpu.VMEM)
    return pl.pallas_call(
        body, name="reduce_small" if reduce else "gather_small", in_specs=[vmem], out_specs=vmem,
        out_shape=jax.ShapeDtypeStruct((rows if reduce else N_DEV * rows, cols), blk.dtype),
        scratch_shapes=[pltpu.VMEM((N_DEV * rows, cols), blk.dtype), pltpu.SemaphoreType.DMA((7,)),
                        pltpu.SemaphoreType.DMA((7,)), pltpu.SemaphoreType.DMA],
        compiler_params=pltpu.CompilerParams(vmem_limit_bytes=VMEM_LIMIT),
    )(blk)


WEIGHTS = ['w_in', 'mix_pre_norm', 'mix_post_norm', 'hg_lb_table', 'hg_out_norm', 'ssm_conv_w', 'ssm_conv_b',
           'ssm_dt_bias', 'ssm_A_log', 'ssm_D', 'ssm_out_norm', 'w_branch_hg', 'w_branch_ssm', 'w_out', 'ffn_pre_norm',
           'ffn_post_norm', 'ffn_w_up', 'ffn_conv_w', 'ffn_conv_b', 'ffn_w_down']
BIG = ('w_in', 'w_branch_hg', 'w_branch_ssm', 'w_out', 'ffn_w_up', 'ffn_w_down')
SMALL = tuple(n for n in WEIGHTS if n not in BIG)
CONV_SHARD = {'ssm_conv_w': SSM_CONV_DIM // N_CHIPS, 'ffn_conv_w': D_FF // N_CHIPS}
LANES = 128


def _pack(parts):
    flat = jnp.concatenate([p.reshape(-1) for p in parts])
    n = flat.shape[0]
    rows = -(-n // (8 * LANES)) * 8
    return jnp.pad(flat, (0, rows * LANES - n)).reshape(rows, LANES)


def _unpack(packed, shapes):
    flat = packed.reshape(-1)
    out, off = [], 0
    for s in shapes:
        n = int(np.prod(s))
        out.append(flat[off:off + n].reshape(s))
        off += n
    return out


def kernel(x, w_in, mix_pre_norm, mix_post_norm, hg_lb_table, hg_out_norm, ssm_conv_w, ssm_conv_b, ssm_dt_bias, ssm_A_log, ssm_D, ssm_out_norm, w_branch_hg, w_branch_ssm, w_out, ffn_pre_norm, ffn_post_norm, ffn_w_up, ffn_conv_w, ffn_conv_b, ffn_w_down, loss_target, m_w_in, m_mix_pre_norm, m_mix_post_norm, m_hg_lb_table, m_hg_out_norm, m_ssm_conv_w, m_ssm_conv_b, m_ssm_dt_bias, m_ssm_A_log, m_ssm_D, m_ssm_out_norm, m_w_branch_hg, m_w_branch_ssm, m_w_out, m_ffn_pre_norm, m_ffn_post_norm, m_ffn_w_up, m_ffn_conv_w, m_ffn_conv_b, m_ffn_w_down, v_w_in, v_mix_pre_norm, v_mix_post_norm, v_hg_lb_table, v_hg_out_norm, v_ssm_conv_w, v_ssm_conv_b, v_ssm_dt_bias, v_ssm_A_log, v_ssm_D, v_ssm_out_norm, v_w_branch_hg, v_w_branch_ssm, v_w_out, v_ffn_pre_norm, v_ffn_post_norm, v_ffn_w_up, v_ffn_conv_w, v_ffn_conv_b, v_ffn_w_down):
    w = dict(w_in=w_in, mix_pre_norm=mix_pre_norm, mix_post_norm=mix_post_norm, hg_lb_table=hg_lb_table, hg_out_norm=hg_out_norm, ssm_conv_w=ssm_conv_w, ssm_conv_b=ssm_conv_b, ssm_dt_bias=ssm_dt_bias, ssm_A_log=ssm_A_log, ssm_D=ssm_D, ssm_out_norm=ssm_out_norm, w_branch_hg=w_branch_hg, w_branch_ssm=w_branch_ssm, w_out=w_out, ffn_pre_norm=ffn_pre_norm, ffn_post_norm=ffn_post_norm, ffn_w_up=ffn_w_up, ffn_conv_w=ffn_conv_w, ffn_conv_b=ffn_conv_b, ffn_w_down=ffn_w_down)
    m = dict(w_in=m_w_in, mix_pre_norm=m_mix_pre_norm, mix_post_norm=m_mix_post_norm, hg_lb_table=m_hg_lb_table, hg_out_norm=m_hg_out_norm, ssm_conv_w=m_ssm_conv_w, ssm_conv_b=m_ssm_conv_b, ssm_dt_bias=m_ssm_dt_bias, ssm_A_log=m_ssm_A_log, ssm_D=m_ssm_D, ssm_out_norm=m_ssm_out_norm, w_branch_hg=m_w_branch_hg, w_branch_ssm=m_w_branch_ssm, w_out=m_w_out, ffn_pre_norm=m_ffn_pre_norm, ffn_post_norm=m_ffn_post_norm, ffn_w_up=m_ffn_w_up, ffn_conv_w=m_ffn_conv_w, ffn_conv_b=m_ffn_conv_b, ffn_w_down=m_ffn_w_down)
    v = dict(w_in=v_w_in, mix_pre_norm=v_mix_pre_norm, mix_post_norm=v_mix_post_norm, hg_lb_table=v_hg_lb_table, hg_out_norm=v_hg_out_norm, ssm_conv_w=v_ssm_conv_w, ssm_conv_b=v_ssm_conv_b, ssm_dt_bias=v_ssm_dt_bias, ssm_A_log=v_ssm_A_log, ssm_D=v_ssm_D, ssm_out_norm=v_ssm_out_norm, w_branch_hg=v_w_branch_hg, w_branch_ssm=v_w_branch_ssm, w_out=v_w_out, ffn_pre_norm=v_ffn_pre_norm, ffn_post_norm=v_ffn_post_norm, ffn_w_up=v_ffn_w_up, ffn_conv_w=v_ffn_conv_w, ffn_conv_b=v_ffn_conv_b, ffn_w_down=v_ffn_w_down)
    shard = 2 * lax.axis_index("x") + lax.axis_index("y")
    bf = lambda a: a.astype(BF16)

    core = lax.axis_index("c")
    p_in = jnp.concatenate([bf(w_in[0].T), jnp.zeros((IN_ROWS - IN_SHARD, D_MODEL), BF16)], axis=0)
    p_rest = jnp.concatenate([bf(w_branch_hg[0]), bf(w_branch_ssm[0]), bf(w_out[0]), bf(ffn_w_down[0])], axis=0)
    p_up = bf(ffn_w_up[0])
    (g_in,) = gather_weights("gather_w_in", [p_in], [IN_PIECES])
    in_t = _own_slot(g_in, p_in, shard)[:, :IN_SHARD].reshape(IN_TOTAL, D_MODEL)
    wts = dict(in_t=in_t, g_t=in_t[SEG_G:], dt_t=jnp.pad(in_t[SEG_DT:SEG_G], ((0, DT_PAD - SSM_HEADS), (0, 0))))
    conv_cols = max(CONV_SHARD.values())
    padc = lambda a: jnp.pad(a, ((0, 0), (0, conv_cols - a.shape[1])))
    conv_blk = jnp.concatenate([padc(ssm_conv_w[0]), padc(ffn_conv_w[0]), jnp.zeros((1, conv_cols), F32)], axis=0)
    conv_all = gather_small(conv_blk, reduce=False)
    par = {n: w[n] for n in SMALL}
    par["ssm_conv_w"] = jnp.concatenate([conv_all[16 * s:16 * s + SSM_CONV, :CONV_SHARD['ssm_conv_w']] for s in range(N_CHIPS)], axis=1)
    par["ffn_conv_w"] = jnp.concatenate([conv_all[16 * s + SSM_CONV:16 * s + SSM_CONV + FFN_CONV, :CONV_SHARD['ffn_conv_w']]
                                         for s in range(N_CHIPS)], axis=1)

    loss, grad_x, big, small = local_step(x[0], loss_target[0], wts, par, p_rest, p_up, shard, core)
    loss = lax.psum(loss[0, 0], ("x", "y", "c"))

    halves = [big["in_t"], big["rest"], big["up"]]
    wholes = pair_assemble("pair_assemble", halves, [IN_PIECES, REST_PIECES, UP_PIECES])
    f_in, f_rest, f_up = [_own_slot(f.reshape((2,) + r.shape), r, core).reshape(f.shape) for f, r in zip(wholes, halves)]
    r0, r1, r2, r3 = REST_SPLITS
    grads = dict(w_in=f_in[:IN_SHARD].T, w_branch_hg=f_rest[:r0], w_branch_ssm=f_rest[r0:r1], w_out=f_rest[r1:r2],
                 ffn_w_down=f_rest[r2:r3], ffn_w_up=f_up)

    small["hg_out_norm"] = ew_sum("sum_heads", [(small["hg_out_norm"][hd], 0) for hd in range(HG_HEADS)], 1, F32, 1)
    small["ssm_D"] = fold_heads(small["ssm_D"])[:, :SSM_HEADS]
    small["ssm_dt_bias"] = small["ssm_dt_bias"][:, :SSM_HEADS]
    small["ssm_A_log"] = small["ssm_A_log"][:, :SSM_HEADS]
    shapes = [small[n].shape for n in SMALL]
    summed = _unpack(gather_small(_pack([small[n] for n in SMALL]), reduce=True), shapes)
    for n, g in zip(SMALL, summed):
        if n in CONV_SHARD:
            g = lax.dynamic_slice_in_dim(g, shard * CONV_SHARD[n], CONV_SHARD[n], axis=1)
        grads[n] = g

    two_d = lambda a: a.reshape(a.shape[-2], a.shape[-1])
    delta, new_m, new_v = {}, {}, {}
    for n, tr in (("w_in", 64), ("w_branch_hg", 128), ("w_branch_ssm", 128), ("w_out", 128), ("ffn_w_up", 128), ("ffn_w_down", 128)):
        delta[n], new_m[n], new_v[n] = adamw("adamw_" + n, two_d(w[n]), grads[n], two_d(m[n]), two_d(v[n]), tr)
    sm_shapes = [two_d(w[n]).shape for n in SMALL]
    packed = adamw("adamw_small", _pack([two_d(w[n]) for n in SMALL]), _pack([grads[n] for n in SMALL]),
                   _pack([two_d(m[n]) for n in SMALL]), _pack([two_d(v[n]) for n in SMALL]), 1024)
    for res, packed_res in zip((delta, new_m, new_v), packed):
        for n, a in zip(SMALL, _unpack(packed_res, sm_shapes)):
            res[n] = a
    shaped = lambda d: [d[n].reshape(w[n].shape) for n in WEIGHTS]
    return (loss, grad_x[None], *shaped(grads), *shaped(delta), *shaped(new_m), *shaped(new_v))
```

```python
import functools

import jax
import jax.numpy as jnp
import numpy as np
from jax import lax
from jax.experimental import pallas as pl
from jax.experimental.pallas import tpu as pltpu

F32 = jnp.float32
BF16 = jnp.bfloat16

D_MODEL = 2048
EPS = 1e-6
HG_HEADS = 16
HG_DK = 128
HG_CHUNK = 64
HG_SUB = 16
SSM_DINNER = 4096
SSM_HEADDIM = 64
SSM_HEADS = 64
SSM_GROUPS = 8
SSM_DSTATE = 128
SSM_CONV = 4
SSM_CHUNK = 256
SSM_CONV_DIM = 6144
D_FF = 5632
FFN_CONV = 3
DT_PAD = 128

ADAM_LR = 0.001
ADAM_B1 = 0.9
ADAM_B2 = 0.999
ADAM_EPS = 1e-08
ADAM_WD = 0.01
ADAM_STEP = 10

VMEM_LIMIT = 56 * 1024 * 1024
HI = lax.Precision.HIGHEST


def _cp(sem, **kw):
    return pltpu.CompilerParams(dimension_semantics=sem, vmem_limit_bytes=VMEM_LIMIT, **kw)


_DIMS = {"nn": (((1,), (0,)), ((), ())), "nt": (((1,), (1,)), ((), ())), "tn": (((0,), (0,)), ((), ()))}


def mm(a, b, mode, *, name, out_dtype=F32, tm=1024, tn=512, tk=None, acc=None, n_major=True,
       dims=None, a_off=(0, 0), b_off=(0, 0)):
    if dims is not None:
        M, N, K = dims
    else:
        if mode == "nn":
            (M, K), (K2, N) = a.shape, b.shape
        elif mode == "nt":
            (M, K), (N, K2) = a.shape, b.shape
        else:
            (K, M), (K2, N) = a.shape, b.shape
        assert K == K2, (a.shape, b.shape, mode)
    tm, tn = min(tm, M), min(tn, N)
    tk = K if tk is None else min(tk, K)
    assert M % tm == 0 and N % tn == 0 and K % tk == 0, (M, N, K, tm, tn, tk)
    a_blk = (tk, tm) if mode == "tn" else (tm, tk)
    b_blk = (tn, tk) if mode == "nt" else (tk, tn)
    assert all(o % s == 0 for o, s in zip(a_off, a_blk)) and all(o % s == 0 for o, s in zip(b_off, b_blk))
    ao0, ao1 = a_off[0] // a_blk[0], a_off[1] // a_blk[1]
    bo0, bo1 = b_off[0] // b_blk[0], b_off[1] // b_blk[1]
    nk = K // tk
    if n_major:
        grid = (N // tn, M // tm, nk)
        ij = lambda p0, p1: (p1, p0)
    else:
        grid = (M // tm, N // tn, nk)
        ij = lambda p0, p1: (p0, p1)

    def a_map(p0, p1, k):
        i, _ = ij(p0, p1)
        return (k + ao0, i + ao1) if mode == "tn" else (i + ao0, k + ao1)

    def b_map(p0, p1, k):
        _, j = ij(p0, p1)
        return (j + bo0, k + bo1) if mode == "nt" else (k + bo0, j + bo1)

    def o_map(p0, p1, k):
        return ij(p0, p1)

    a_spec = pl.BlockSpec(a_blk, a_map)
    b_spec = pl.BlockSpec(b_blk, b_map)
    o_spec = pl.BlockSpec((tm, tn), o_map)
    dims = _DIMS[mode]
    has_acc = acc is not None

    def body(*refs):
        if has_acc:
            a_ref, b_ref, c_ref, o_ref, acc_ref = refs
        else:
            a_ref, b_ref, o_ref, acc_ref = refs
        k = pl.program_id(2)
        part = lax.dot_general(a_ref[...], b_ref[...], dims, preferred_element_type=F32)
        if nk == 1:
            o_ref[...] = (part + c_ref[...].astype(F32) if has_acc else part).astype(out_dtype)
            return

        @pl.when(k == 0)
        def _():
            acc_ref[...] = part

        @pl.when(k > 0)
        def _():
            acc_ref[...] += part

        @pl.when(k == nk - 1)
        def _():
            r = acc_ref[...]
            if has_acc:
                r = r + c_ref[...].astype(F32)
            o_ref[...] = r.astype(out_dtype)

    in_specs = [a_spec, b_spec] + ([o_spec] if has_acc else [])
    args = (a, b) + ((acc,) if has_acc else ())
    return pl.pallas_call(
        body, name=name, grid=grid, in_specs=in_specs, out_specs=o_spec,
        out_shape=jax.ShapeDtypeStruct((M, N), out_dtype),
        scratch_shapes=[pltpu.VMEM((tm, tn) if nk > 1 else (8, 128), F32)],
        compiler_params=_cp(("parallel", "parallel", "arbitrary")),
    )(*args)


def mm_segments(name, segs, bs, *, tm, tn, tk, acc=None, exchange=None, bt=False):
    m_, n_ = segs[0][0].shape[0], bs[0].shape[0 if bt else 1]
    tm, tn = min(tm, m_), min(tn, n_)
    op = _ChipExchange(exchange) if exchange else None
    ne = (len(exchange) if exchange else 0)
    na = 0 if acc is None else 1
    steps, k0 = [], 0
    for a, bi, row in segs:
        w = a.shape[1]
        tks = min(tk, w)
        assert w % tks == 0 and row % tks == 0 and tks == min(tk, bs[bi].shape[1 if bt else 0]), (w, row, tks)
        steps.append((k0, w // tks, tks, bi, row // tks))
        k0 += w // tks
    nk = k0
    assert m_ % tm == 0 and n_ % tn == 0

    def a_spec(k_first, count, tks):
        return pl.BlockSpec((tm, tks), lambda j, i, k: (i, jnp.clip(k - k_first, 0, count - 1)))

    def b_spec(bi):
        mine = [s for s in steps if s[3] == bi]

        def index(j, i, k):
            blk = mine[0][4]
            for k_first, count, _, _, first_blk in mine:
                blk = jnp.where(k >= k_first, first_blk + jnp.minimum(k - k_first, count - 1), blk)
            return (j, blk) if bt else (blk, j)
        return pl.BlockSpec((tn, mine[0][2]) if bt else (mine[0][2], tn), index)

    ns = len(segs)

    nb = len(bs)
    grid = (n_ // tn, m_ // tm, nk)

    def body(*refs):
        a_refs, b_refs = refs[:ns], refs[ns:ns + nb]
        acc_in = refs[ns + nb] if na else None
        rest = refs[ns + nb + na:]
        ex_refs, o_ref, got_refs, acc_ref, sems = rest[:ne], rest[ne], rest[ne + 1:2 * ne + 1], rest[2 * ne + 1], rest[2 * ne + 2:]
        k = pl.program_id(2)
        if op:
            first = (pl.program_id(0) == 0) & (pl.program_id(1) == 0) & (k == 0)
            last = (pl.program_id(0) == grid[0] - 1) & (pl.program_id(1) == grid[1] - 1) & (k == nk - 1)

            @pl.when(first)
            def _():
                op.start(ex_refs, got_refs, *sems)

            @pl.when(last)
            def _():
                op.finish(ex_refs, got_refs, *sems)

        @pl.when(k == 0)
        def _():
            acc_ref[...] = acc_in[...] if na else jnp.zeros_like(acc_ref)

        for a_ref, (k_first, count, _, bi, _) in zip(a_refs, steps):
            @pl.when((k >= k_first) & (k < k_first + count))
            def _(a_ref=a_ref, bi=bi):
                acc_ref[...] += lax.dot_general(a_ref[...], b_refs[bi][...], _DIMS["nt" if bt else "nn"],
                                                preferred_element_type=F32)

        @pl.when(k == nk - 1)
        def _():
            o_ref[...] = acc_ref[...]

    any_spec = pl.BlockSpec(memory_space=pl.ANY)
    o_spec = pl.BlockSpec((tm, tn), lambda j, i, k: (i, j))
    outs = pl.pallas_call(
        body, name=name, grid=grid,
        in_specs=[a_spec(s[0], s[1], s[2]) for s in steps] + [b_spec(bi) for bi in range(nb)] + [o_spec] * na + [any_spec] * ne,
        out_specs=[o_spec] + [any_spec] * ne,
        out_shape=[jax.ShapeDtypeStruct((m_, n_), F32)] + (op.out_shape if op else []),
        scratch_shapes=[pltpu.VMEM((tm, tn), F32)] + (_sem_pair(op.n_sem) if op else []),
        compiler_params=_cp(("arbitrary", "arbitrary", "arbitrary")),
    )(*[a for a, _, _ in segs], *bs, *(() if acc is None else (acc,)), *(exchange or ()))
    return outs if op else outs[0]


def _dims(mode, ndim):
    if ndim == 2:
        return _DIMS[mode]
    (ca,), (cb,) = _DIMS[mode][0]
    return (((ca + 1,), (cb + 1,)), ((0,), (0,)))


def _bdot_plain(a, b, mode):
    return lax.dot_general(a.astype(BF16), b.astype(BF16), _dims(mode, a.ndim), preferred_element_type=F32)


@functools.partial(jax.custom_vjp, nondiff_argnums=(2,))
def _bdot_vjp(a, b, mode):
    return _bdot_plain(a, b, mode)


def _bdot_fwd(a, b, mode):
    return _bdot_plain(a, b, mode), (a, b)


def _bdot_bwd(mode, res, g):
    a, b = res
    if mode == "nn":
        return _bdot_plain(g, b, "nt"), _bdot_plain(a, g, "tn")
    if mode == "nt":
        return _bdot_plain(g, b, "nn"), _bdot_plain(g, a, "tn")
    return _bdot_plain(b, g, "nt"), _bdot_plain(a, g, "nn")


_bdot_vjp.defvjp(_bdot_fwd, _bdot_bwd)


def _split3(x):
    x1 = x.astype(BF16)
    r1 = x - x1.astype(F32)
    x2 = r1.astype(BF16)
    return x1, x2, (r1 - x2.astype(F32)).astype(BF16)


def _hdot_impl(a, b, mode, data):
    dims = _dims(mode, a.ndim)
    if data == "a":
        sel = b.astype(BF16)
        parts = [lax.dot_general(p, sel, dims, preferred_element_type=F32) for p in _split3(a)]
    else:
        sel = a.astype(BF16)
        parts = [lax.dot_general(sel, p, dims, preferred_element_type=F32) for p in _split3(b)]
    return (parts[2] + parts[1]) + parts[0]


@functools.partial(jax.custom_vjp, nondiff_argnums=(2, 3))
def _hdot(a, b, mode="nn", data="b"):
    return _hdot_impl(a, b, mode, data)


def _hdot_fwd(a, b, mode, data):
    return _hdot_impl(a, b, mode, data), (a, b)


def _hdot_bwd(mode, data, res, g):
    a, b = res
    if data == "a":
        da = {"nn": lambda: _hdot_impl(g, b, "nt", "a"), "nt": lambda: _hdot_impl(g, b, "nn", "a"),
              "tn": lambda: _hdot_impl(b, g, "nt", "b")}[mode]()
        return da, jnp.zeros_like(b)
    db = {"nn": lambda: _hdot_impl(a, g, "tn", "b"), "nt": lambda: _hdot_impl(g, a, "tn", "a"),
          "tn": lambda: _hdot_impl(a, g, "nn", "b")}[mode]()
    return jnp.zeros_like(a), db


_hdot.defvjp(_hdot_fwd, _hdot_bwd)


def _sigmoid(x):
    return lax.logistic(x)


def _silu(x):
    return x * _sigmoid(x)


def _iota(shape, dim):
    return lax.broadcasted_iota(jnp.int32, shape, dim)


def _rms(x, w):
    return x * lax.rsqrt(jnp.mean(x * x, axis=-1, keepdims=True) + EPS) * w


def _hg_chunk(q_raw, f_raw, v, g, st, t0, t1, nw, dot):
    nhd, c = q_raw.shape[0], q_raw.shape[1]
    m = jnp.maximum(t0, t1)
    e0, e1 = jnp.exp(t0 - m), jnp.exp(t1 - m)
    lb = e0 / (e0 + e1)
    f = lb + (1.0 - lb) * _sigmoid(f_raw)
    k = 1.0 - f
    lf = jnp.log(f)
    qh = _silu(q_raw) * (HG_DK ** -0.5)
    row, col = _iota((c, c), 0), _iota((c, c), 1)
    causal = col <= row
    tril = jnp.broadcast_to(jnp.where(causal, 1.0, 0.0).astype(F32), (nhd, c, c))
    trilb = jnp.broadcast_to(jnp.where(causal & (col // HG_SUB == row // HG_SUB), 1.0, 0.0).astype(F32), (nhd, c, c))
    b = _hdot(tril, lf)
    bl = _hdot(trilb, lf)
    a_row = b - bl
    rid = _iota((c, HG_DK), 0)
    qt = qh * jnp.exp(bl)
    kt = k * jnp.exp(-bl)
    scores = jnp.zeros((nhd, c, c), F32)
    for j in range(c // HG_SUB):
        if j == 0:
            qj = qt * jnp.exp(a_row)
        else:
            a_j = jnp.sum(jnp.where(rid == j * HG_SUB - 1, b, 0.0), axis=1, keepdims=True)
            qj = qt * jnp.exp(jnp.where(rid // HG_SUB >= j, a_row - a_j, -1e30))
        kj = jnp.where(rid // HG_SUB == j, kt, 0.0)
        scores = scores + dot(qj, kj, "nt")
    scores = jnp.where(causal, scores, 0.0)
    o = dot(scores, v, "nn") + dot(qh * jnp.exp(b), st, "nt")
    b_last = jnp.sum(jnp.where(rid == c - 1, b, 0.0), axis=1, keepdims=True)
    st_new = st * jnp.exp(b_last) + dot(v, k * jnp.exp(b_last - b), "tn")
    y = _rms(o, nw) * _silu(g)
    return y, st_new


HG_HPS = 16
HG_W = HG_HPS * HG_DK


def hgrn2_fwd(qfig, table, nw, *, step_chunks=2, gather=None):
    t = qfig.shape[0]
    rows = HG_CHUNK * step_chunks
    nsteps = t // rows
    nh = HG_HEADS // HG_HPS
    op = _Gather(*gather) if gather else None
    ng = len(gather[0]) if gather else 0

    def body(*refs):
        q_ref, f_ref, v_ref, g_ref, tab_ref, nw_ref = refs[:6]
        p_refs = refs[6:6 + ng]
        y_ref, s_ref = refs[6 + ng:8 + ng]
        got_refs = refs[8 + ng:8 + 2 * ng]
        st_scr = refs[8 + 2 * ng]
        sems = refs[9 + 2 * ng:]
        first_step = (pl.program_id(0) == 0) & (pl.program_id(1) == 0)
        last_step = (pl.program_id(0) == nh - 1) & (pl.program_id(1) == nsteps - 1)
        if op:
            @pl.when(first_step)
            def _():
                op.start(p_refs, got_refs, *sems)

        @pl.when(pl.program_id(1) == 0)
        def _():
            st_scr[...] = jnp.zeros_like(st_scr)

        nwv = nw_ref[...]
        lanes = [pl.ds(hh * HG_DK, HG_DK) for hh in range(HG_HPS)]
        t0 = jnp.stack([tab_ref[0:1, ln] for ln in lanes])
        t1 = jnp.stack([tab_ref[1:2, ln] for ln in lanes])
        for c in range(step_chunks):
            sl = pl.ds(c * HG_CHUNK, HG_CHUNK)
            heads = lambda ref: jnp.stack([ref[sl, ln] for ln in lanes])
            st = st_scr[...]
            for hh in range(HG_HPS):
                s_ref[hh, c] = st[hh]
            y, st_new = _hg_chunk(heads(q_ref), heads(f_ref), heads(v_ref), heads(g_ref), st, t0, t1, nwv, _bdot_vjp)
            for hh, ln in enumerate(lanes):
                y_ref[sl, ln] = y[hh].astype(BF16)
            st_scr[...] = st_new

        if op:
            @pl.when(last_step)
            def _():
                op.finish(p_refs, got_refs, *sems)

    blk = lambda off: pl.BlockSpec((rows, HG_W), lambda h, c, off=off: (c, off + h))
    return pl.pallas_call(
        body, name="hgrn2_fwd", grid=(nh, nsteps),
        in_specs=[blk(0), blk(nh), blk(2 * nh), blk(3 * nh),
                  pl.BlockSpec((2, HG_W), lambda h, c: (0, h)), pl.BlockSpec((1, HG_DK), lambda h, c: (0, 0))] + [ANY] * ng,
        out_specs=[pl.BlockSpec((rows, HG_W), lambda h, c: (c, h)),
                   pl.BlockSpec((HG_HPS, step_chunks, HG_DK, HG_DK), lambda h, c: (h, c, 0, 0))] + [ANY] * ng,
        out_shape=[jax.ShapeDtypeStruct((t, HG_HEADS * HG_DK), BF16),
                   jax.ShapeDtypeStruct((HG_HEADS, t // HG_CHUNK, HG_DK, HG_DK), F32)] + (op.out_shape if op else []),
        scratch_shapes=[pltpu.VMEM((HG_HPS, HG_DK, HG_DK), F32)] + (_sem_pair(op.n_sem) if op else []),
        compiler_params=_cp(("arbitrary", "arbitrary")),
    )(qfig, qfig, qfig, qfig, table, nw, *(gather[0] if gather else ()))


def hgrn2_bwd(qfig, table, nw, states, dy, *, step_chunks=2):
    t = qfig.shape[0]
    rows = HG_CHUNK * step_chunks
    nsteps = t // rows
    nh = HG_HEADS // HG_HPS
    assert nh == 1
    d = HG_HEADS * HG_DK

    def body(q_ref, f_ref, v_ref, g_ref, tab_ref, nw_ref, s_ref, dy_ref, dqfig_ref, dtab_ref, dnw_ref, dst_scr):
        @pl.when(pl.program_id(1) == 0)
        def _():
            dst_scr[...] = jnp.zeros_like(dst_scr)
            dtab_ref[...] = jnp.zeros_like(dtab_ref)
            dnw_ref[...] = jnp.zeros_like(dnw_ref)

        nwv = nw_ref[...]
        fn = functools.partial(_hg_chunk, dot=_bdot_vjp)
        lanes = [pl.ds(hh * HG_DK, HG_DK) for hh in range(HG_HPS)]
        t0 = jnp.stack([tab_ref[0:1, ln] for ln in lanes])
        t1 = jnp.stack([tab_ref[1:2, ln] for ln in lanes])
        for c in reversed(range(step_chunks)):
            sl = pl.ds(c * HG_CHUNK, HG_CHUNK)
            heads = lambda ref: jnp.stack([ref[sl, ln] for ln in lanes])
            _, vjp = jax.vjp(fn, heads(q_ref), heads(f_ref), heads(v_ref), heads(g_ref), s_ref[:, c], t0, t1, nwv)
            dq, df, dv, dg, dst, dt0, dt1, dnw = vjp((heads(dy_ref).astype(F32), dst_scr[...]))
            for hh, ln in enumerate(lanes):
                for seg, val in enumerate((dq, df, dv, dg)):
                    dqfig_ref[sl, pl.ds(seg * d + hh * HG_DK, HG_DK)] = val[hh].astype(BF16)
                dtab_ref[0:1, ln] += dt0[hh]
                dtab_ref[1:2, ln] += dt1[hh]
            dst_scr[...] = dst
            dnw_ref[0] += dnw

    rev = lambda c: nsteps - 1 - c
    blk = lambda off: pl.BlockSpec((rows, HG_W), lambda h, c, off=off: (rev(c), off + h))
    outs = pl.pallas_call(
        body, name="hgrn2_bwd", grid=(nh, nsteps),
        in_specs=[blk(0), blk(nh), blk(2 * nh), blk(3 * nh),
                  pl.BlockSpec((2, HG_W), lambda h, c: (0, h)), pl.BlockSpec((1, HG_DK), lambda h, c: (0, 0)),
                  pl.BlockSpec((HG_HPS, step_chunks, HG_DK, HG_DK), lambda h, c: (h, rev(c), 0, 0)),
                  pl.BlockSpec((rows, HG_W), lambda h, c: (rev(c), h))],
        out_specs=[pl.BlockSpec((rows, 4 * d), lambda h, c: (rev(c), 0)),
                   pl.BlockSpec((2, HG_W), lambda h, c: (0, h)),
                   pl.BlockSpec((HG_HPS, 1, HG_DK), lambda h, c: (h, 0, 0))],
        out_shape=[jax.ShapeDtypeStruct((t, 4 * d), BF16), jax.ShapeDtypeStruct((2, d), F32),
                   jax.ShapeDtypeStruct((HG_HEADS, 1, HG_DK), F32)],
        scratch_shapes=[pltpu.VMEM((HG_HPS, HG_DK, HG_DK), F32)],
        compiler_params=_cp(("parallel", "arbitrary")),
    )(qfig, qfig, qfig, qfig, table, nw, states, dy)
    return outs


def _ssd_chunk(xs2, dt, acum, bm, cm, s2, pair0, dot):
    npr, c = xs2.shape[0], xs2.shape[1]
    sh_e, sh_s = (npr, DT_PAD, 128), (npr, 8, DT_PAD)
    first_head = 2 * (pair0 + _iota(sh_e, 0))
    expand = jnp.where(_iota(sh_e, 1) == first_head + _iota(sh_e, 2) // SSM_HEADDIM, 1.0, 0.0).astype(F32)
    sel = (_iota(sh_s, 2) == 2 * (pair0 + _iota(sh_s, 0)) + _iota(sh_s, 1)) & (_iota(sh_s, 1) < 2)
    sel = jnp.where(sel, 1.0, 0.0).astype(F32)
    per_pair = lambda a: jnp.broadcast_to(a, (npr,) + a.shape)
    dtx = _hdot(per_pair(dt), expand, "nn", "a")
    acol = _hdot(per_pair(acum), expand, "nn", "a")
    arow8 = _hdot(sel, per_pair(acum), "nt", "b")
    row, col = _iota((c, c), 0), _iota((c, c), 1)
    causal = col <= row
    cb = dot(cm, bm, "nt")
    x2 = xs2 * dtx
    lane_c = _iota((c, 128), 1)
    y = dot(per_pair(cm), s2, "nn") * jnp.exp(acol)
    for r in range(2):
        head = (lane_c // SSM_HEADDIM) == r
        a_c = jnp.sum(jnp.where(head & (lane_c % SSM_HEADDIM == 0), acol, 0.0), axis=2, keepdims=True)
        a_r = jnp.sum(jnp.where(_iota((8, c), 0) == r, arow8, 0.0), axis=1, keepdims=True)
        decay = jnp.exp(jnp.where(causal, a_c - a_r, -1e30))
        y = y + dot(cb * decay, jnp.where(head, x2, 0.0), "nn")
    a_last = jnp.sum(jnp.where(_iota((c, 128), 0) == c - 1, acol, 0.0), axis=1, keepdims=True)
    s2_new = s2 * jnp.exp(a_last) + dot(per_pair(bm), x2 * jnp.exp(a_last - acol), "tn")
    return y, s2_new


SSM_PAIRS = SSM_HEADS // 2
PAIRS_PER_GROUP = SSM_PAIRS // SSM_GROUPS
SSD_PPS = 4
SSD_W = SSD_PPS * 128
_XS_BLOCKS = SSM_DINNER // 128
_B_BLOCK0 = _XS_BLOCKS
_C_BLOCK0 = _XS_BLOCKS + SSM_GROUPS


def ssd_fwd(xbc_act, dt, acum, *, gather=None):
    t = xbc_act.shape[0]
    nc = t // SSM_CHUNK
    c_ = SSM_CHUNK
    nq = SSM_PAIRS // SSD_PPS
    op = _Gather(*gather) if gather else None
    ng = len(gather[0]) if gather else 0

    def body(*refs):
        xs_ref, b_ref, c_ref, dt_ref, ac_ref = refs[:5]
        p_refs = refs[5:5 + ng]
        y_ref, s_ref = refs[5 + ng:7 + ng]
        got_refs = refs[7 + ng:7 + 2 * ng]
        s_scr = refs[7 + 2 * ng]
        sems = refs[8 + 2 * ng:]
        q = pl.program_id(1)
        if op:
            @pl.when((pl.program_id(0) == 0) & (q == 0))
            def _():
                op.start(p_refs, got_refs, *sems)

            @pl.when((pl.program_id(0) == nc - 1) & (q == nq - 1))
            def _():
                op.finish(p_refs, got_refs, *sems)

        mine = pl.ds(SSD_PPS * q, SSD_PPS)
        lanes = [pl.ds(r * 128, 128) for r in range(SSD_PPS)]

        @pl.when(pl.program_id(0) == 0)
        def _():
            s_scr[mine] = jnp.zeros((SSD_PPS, SSM_DSTATE, 128), F32)

        s2 = s_scr[mine]
        s_ref[...] = s2
        xs = jnp.stack([xs_ref[:, ln] for ln in lanes])
        y, s2_new = _ssd_chunk(xs, dt_ref[...], ac_ref[...], b_ref[...], c_ref[...], s2, SSD_PPS * q, _bdot_vjp)
        for r, ln in enumerate(lanes):
            y_ref[:, ln] = y[r]
        s_scr[mine] = s2_new

    grp = lambda q: q // (PAIRS_PER_GROUP // SSD_PPS)
    return pl.pallas_call(
        body, name="ssd_fwd", grid=(nc, SSM_PAIRS // SSD_PPS),
        in_specs=[pl.BlockSpec((c_, SSD_W), lambda c, q: (c, q)),
                  pl.BlockSpec((c_, 128), lambda c, q: (c, _B_BLOCK0 + grp(q))),
                  pl.BlockSpec((c_, 128), lambda c, q: (c, _C_BLOCK0 + grp(q))),
                  pl.BlockSpec((c_, DT_PAD), lambda c, q: (c, 0)),
                  pl.BlockSpec((c_, DT_PAD), lambda c, q: (c, 0))] + [ANY] * ng,
        out_specs=[pl.BlockSpec((c_, SSD_W), lambda c, q: (c, q)),
                   pl.BlockSpec((None, SSD_PPS, SSM_DSTATE, 128), lambda c, q: (c, q, 0, 0))] + [ANY] * ng,
        out_shape=[jax.ShapeDtypeStruct((t, SSM_DINNER), F32),
                   jax.ShapeDtypeStruct((nc, SSM_PAIRS, SSM_DSTATE, 128), F32)] + (op.out_shape if op else []),
        scratch_shapes=[pltpu.VMEM((SSM_PAIRS, SSM_DSTATE, 128), F32)] + (_sem_pair(op.n_sem) if op else []),
        compiler_params=_cp(("arbitrary", "arbitrary")),
    )(xbc_act, xbc_act, xbc_act, dt, acum, *(gather[0] if gather else ()))


def ssd_bwd(xbc_act, dt, acum, states, dy, dskip, *, exchange=None):
    t = xbc_act.shape[0]
    nc = t // SSM_CHUNK
    c_ = SSM_CHUNK
    rev = lambda c: nc - 1 - c
    nq = SSM_PAIRS // SSD_PPS
    op = _ChipExchange(exchange) if exchange else None
    ne = len(exchange) if exchange else 0

    def body(*refs):
        xs_ref, b_ref, c_ref, dt_ref, ac_ref, s_ref, dy_ref, sk_ref = refs[:8]
        ex_refs = refs[8:8 + ne]
        dxs_ref, db_ref, dc_ref, ddt_ref, dac_ref = refs[8 + ne:13 + ne]
        got_refs = refs[13 + ne:13 + 2 * ne]
        ds_scr = refs[13 + 2 * ne]
        sems = refs[14 + 2 * ne:]
        q = pl.program_id(1)
        if op:
            @pl.when((pl.program_id(0) == 0) & (q == 0))
            def _():
                op.start(ex_refs, got_refs, *sems)

            @pl.when((pl.program_id(0) == nc - 1) & (q == nq - 1))
            def _():
                op.finish(ex_refs, got_refs, *sems)

        assert SSD_PPS == PAIRS_PER_GROUP
        mine = pl.ds(SSD_PPS * q, SSD_PPS)
        lanes = [pl.ds(r * 128, 128) for r in range(SSD_PPS)]

        @pl.when(pl.program_id(0) == 0)
        def _():
            ds_scr[mine] = jnp.zeros((SSD_PPS, SSM_DSTATE, 128), F32)

        fn = functools.partial(_ssd_chunk, pair0=SSD_PPS * q, dot=_bdot_vjp)
        xs = jnp.stack([xs_ref[:, ln] for ln in lanes])
        dy = jnp.stack([dy_ref[:, ln] for ln in lanes])
        _, vjp = jax.vjp(fn, xs, dt_ref[...], ac_ref[...], b_ref[...], c_ref[...], s_ref[...])
        dxs, ddt, dac, db, dc, ds = vjp((dy, ds_scr[mine]))
        for r, ln in enumerate(lanes):
            dxs_ref[:, ln] = dxs[r] + sk_ref[:, ln]
        ds_scr[mine] = ds
        db_ref[...] = db
        dc_ref[...] = dc

        @pl.when(q == 0)
        def _():
            ddt_ref[...] = ddt
            dac_ref[...] = dac

        @pl.when(q != 0)
        def _():
            ddt_ref[...] += ddt
            dac_ref[...] += dac

    grp = lambda q: q // (PAIRS_PER_GROUP // SSD_PPS)
    return pl.pallas_call(
        body, name="ssd_bwd", grid=(nc, SSM_PAIRS // SSD_PPS),
        in_specs=[pl.BlockSpec((c_, SSD_W), lambda c, q: (rev(c), q)),
                  pl.BlockSpec((c_, 128), lambda c, q: (rev(c), _B_BLOCK0 + grp(q))),
                  pl.BlockSpec((c_, 128), lambda c, q: (rev(c), _C_BLOCK0 + grp(q))),
                  pl.BlockSpec((c_, DT_PAD), lambda c, q: (rev(c), 0)),
                  pl.BlockSpec((c_, DT_PAD), lambda c, q: (rev(c), 0)),
                  pl.BlockSpec((None, SSD_PPS, SSM_DSTATE, 128), lambda c, q: (rev(c), q, 0, 0)),
                  pl.BlockSpec((c_, SSD_W), lambda c, q: (rev(c), q)),
                  pl.BlockSpec((c_, SSD_W), lambda c, q: (rev(c), q))] + [ANY] * ne,
        out_specs=[pl.BlockSpec((c_, SSD_W), lambda c, q: (rev(c), q)),
                   pl.BlockSpec((c_, 128), lambda c, q: (rev(c), grp(q))),
                   pl.BlockSpec((c_, 128), lambda c, q: (rev(c), grp(q))),
                   pl.BlockSpec((c_, DT_PAD), lambda c, q: (rev(c), 0)),
                   pl.BlockSpec((c_, DT_PAD), lambda c, q: (rev(c), 0))] + [ANY] * ne,
        out_shape=[jax.ShapeDtypeStruct((t, SSM_DINNER), F32),
                   jax.ShapeDtypeStruct((t, SSM_GROUPS * SSM_DSTATE), F32),
                   jax.ShapeDtypeStruct((t, SSM_GROUPS * SSM_DSTATE), F32),
                   jax.ShapeDtypeStruct((t, DT_PAD), F32),
                   jax.ShapeDtypeStruct((t, DT_PAD), F32)] + (op.out_shape if op else []),
        scratch_shapes=[pltpu.VMEM((SSM_PAIRS, SSM_DSTATE, 128), F32)] + (_sem_pair(op.n_sem) if op else []),
        compiler_params=_cp(("arbitrary", "arbitrary")),
    )(xbc_act, xbc_act, xbc_act, dt, acum, states, dy, dskip, *(exchange or ()))


def rowwise(name, fn, row_ins, par_ins, row_outs, acc_outs, *, tt, ncb=1, rb=None):
    t = row_ins[0][0].shape[0]
    assert t % tt == 0
    n_ri, n_pi, n_ro, n_ao = len(row_ins), len(par_ins), len(row_outs), len(acc_outs)
    rb = tt if rb is None else rb
    assert tt % rb == 0

    def body(*refs):
        i = pl.program_id(1)
        ro_refs = refs[n_ri + n_pi:n_ri + n_pi + n_ro]
        ao_refs = refs[n_ri + n_pi + n_ro:]
        pars = [r[...] for r in refs[n_ri:n_ri + n_pi]]
        accs = None
        for blk in range(tt // rb):
            rows = pl.ds(blk * rb, rb)
            outs = fn(*[r[rows, :] for r in refs[:n_ri]], *pars)
            for r, v in zip(ro_refs, outs[:n_ro]):
                r[rows, :] = v.astype(r.dtype)
            accs = list(outs[n_ro:]) if accs is None else [a + v for a, v in zip(accs, outs[n_ro:])]
        for r, v in zip(ao_refs, accs):
            @pl.when(i == 0)
            def _(r=r, v=v):
                r[...] = v

            @pl.when(i > 0)
            def _(r=r, v=v):
                r[...] += v

    in_specs = [pl.BlockSpec((tt, bc), lambda j, i, off=off: (i, off + j)) for _, bc, off in row_ins]
    in_specs += [pl.BlockSpec((a.shape[0], bc), lambda j, i, off=off: (0, off + j)) for a, bc, off in par_ins]
    out_specs = [pl.BlockSpec((tt, bc), lambda j, i: (i, j)) for _, bc, _ in row_outs]
    out_specs += [pl.BlockSpec((r, bc), lambda j, i: (0, j)) for r, _, bc in acc_outs]
    out_shape = [jax.ShapeDtypeStruct((t, c), dt) for c, _, dt in row_outs]
    out_shape += [jax.ShapeDtypeStruct((r, c), F32) for r, c, _ in acc_outs]
    return pl.pallas_call(
        body, name=name, grid=(ncb, t // tt), in_specs=in_specs, out_specs=out_specs, out_shape=out_shape,
        compiler_params=_cp(("parallel", "arbitrary")),
    )(*[a for a, _, _ in row_ins], *[a for a, _, _ in par_ins])


def _colsum(v):
    return jnp.sum(v, axis=0, keepdims=True)


def _softplus(x):
    return jnp.maximum(x, 0.0) + jnp.log(1.0 + jnp.exp(-jnp.abs(x)))


def _gelu_tanh(x):
    return 0.5 * x * (1.0 + jnp.tanh(0.7978845608028654 * (x + 0.044715 * (x * x * x))))


D = D_MODEL
ROW_RB = 16


def norm_fwd(x, w):
    return rowwise("norm_fwd", lambda xv, wv: (_rms(xv, wv),), [(x, D, 0)], [(w, D, 0)], [(D, D, BF16)], [], tt=256, rb=ROW_RB)[0]


def norm_bwd(x, w, dh, dres):
    def fn(xv, dhv, drv, wv):
        _, vjp = jax.vjp(_rms, xv, wv)
        dx, dw = vjp(dhv)
        return dx + drv, dw
    return rowwise("norm_bwd", fn, [(x, D, 0), (dh, D, 0), (dres, D, 0)], [(w, D, 0)], [(D, D, F32)], [(1, D, D)], tt=256, rb=ROW_RB)


def _dt_fn(dtr, bias, a_log):
    c = dtr.shape[0]
    dt = _softplus(dtr + bias)
    da = dt * (-jnp.exp(a_log))
    tril = jnp.where(_iota((c, c), 1) <= _iota((c, c), 0), 1.0, 0.0).astype(F32)
    return dt, _hdot(tril, da)


def dt_fwd(dtr, bias, a_log):
    return rowwise("dt_fwd", _dt_fn, [(dtr, DT_PAD, 0)], [(bias, DT_PAD, 0), (a_log, DT_PAD, 0)],
                   [(DT_PAD, DT_PAD, F32), (DT_PAD, DT_PAD, F32)], [], tt=SSM_CHUNK)


def dt_bwd(dtr, bias, a_log, ddt, dacum):
    def fn(dtrv, ddtv, dacv, bv, av):
        _, vjp = jax.vjp(_dt_fn, dtrv, bv, av)
        return vjp((ddtv, dacv))
    return rowwise("dt_bwd", fn, [(dtr, DT_PAD, 0), (ddt, DT_PAD, 0), (dacum, DT_PAD, 0)],
                   [(bias, DT_PAD, 0), (a_log, DT_PAD, 0)],
                   [(DT_PAD, DT_PAD, BF16)], [(1, DT_PAD, DT_PAD), (1, DT_PAD, DT_PAD)], tt=SSM_CHUNK)


GROUP_W = SSM_DINNER // SSM_GROUPS


def _ssm_post_fn(yv, xsv, zv, dexp, nw):
    return _rms((yv + dexp * xsv) * _silu(zv), nw)


def ssm_post_fwd(yssd, xbc_act, z, dexp, nw):
    return rowwise("ssm_post_fwd", lambda *a: (_ssm_post_fn(*a),),
                   [(yssd, GROUP_W, 0), (xbc_act, GROUP_W, 0), (z, GROUP_W, 0)], [(dexp, GROUP_W, 0), (nw, GROUP_W, 0)],
                   [(SSM_DINNER, GROUP_W, BF16)], [], tt=512, ncb=SSM_GROUPS)[0]


def ssm_post_bwd(yssd, xbc_act, z, dexp, nw, dy):
    def fn(yv, xsv, zv, dyv, dv, nv):
        _, vjp = jax.vjp(_ssm_post_fn, yv, xsv, zv, dv, nv)
        return vjp(dyv.astype(F32))
    return rowwise("ssm_post_bwd", fn,
                   [(yssd, GROUP_W, 0), (xbc_act, GROUP_W, 0), (z, GROUP_W, 0), (dy, GROUP_W, 0)],
                   [(dexp, GROUP_W, 0), (nw, GROUP_W, 0)],
                   [(SSM_DINNER, GROUP_W, F32), (SSM_DINNER, GROUP_W, F32), (SSM_DINNER, GROUP_W, BF16)],
                   [(1, SSM_DINNER, GROUP_W), (1, SSM_DINNER, GROUP_W)], tt=512, ncb=SSM_GROUPS)


def _merge_fn(ah, asm, gh, gs):
    return _sigmoid(gh) * ah + _sigmoid(gs) * asm


def merge_fwd(a_hg, a_ssm, gates):
    f32 = lambda vals: [v.astype(F32) for v in vals]
    return rowwise("merge_fwd", lambda *a: (_merge_fn(*f32(a)),), [(a_hg, D, 0), (a_ssm, D, 0), (gates, D, 0), (gates, D, 1)], [],
                   [(D, D, BF16)], [], tt=256, rb=ROW_RB)[0]


def merge_bwd(a_hg, a_ssm, gates, dmixed):
    def fn(ah, asm, gh, gs, dm):
        _, vjp = jax.vjp(_merge_fn, *[v.astype(F32) for v in (ah, asm, gh, gs)])
        return vjp(dm.astype(F32))
    return rowwise("merge_bwd", fn, [(a_hg, D, 0), (a_ssm, D, 0), (gates, D, 0), (gates, D, 1), (dmixed, D, 0)], [],
                   [(D, D, BF16)] * 4, [], tt=256, rb=ROW_RB)


def _post1_fn(xv, uv, wpost, wpre):
    x1 = xv + _rms(uv, wpost)
    return x1, _rms(x1, wpre)


def post1_fwd(x, u, wpost, wpre):
    return rowwise("post1_fwd", _post1_fn, [(x, D, 0), (u, D, 0)], [(wpost, D, 0), (wpre, D, 0)],
                   [(D, D, F32), (D, D, BF16)], [], tt=256, rb=ROW_RB)


def post1_bwd(x, u, wpost, wpre, dx1, dh2):
    def fn(xv, uv, d1, d2, wa, wb):
        _, vjp = jax.vjp(_post1_fn, xv, uv, wa, wb)
        dx, du, dwa, dwb = vjp((d1, d2))
        return du, dx, dwa, dwb
    return rowwise("post1_bwd", fn, [(x, D, 0), (u, D, 0), (dx1, D, 0), (dh2, D, 0)], [(wpost, D, 0), (wpre, D, 0)],
                   [(D, D, BF16), (D, D, F32)], [(1, D, D), (1, D, D)], tt=256, rb=ROW_RB)


def final_fwd_bwd(x1, fo, w, target):
    def fn(x1v, fov, tv, wv):
        def loss_fn(a, b, c):
            err = a + _rms(b, c) - tv
            return 0.5 * jnp.sum(err * err) * (1.0 / D)
        loss, vjp = jax.vjp(loss_fn, x1v, fov, wv)
        dx, dfo, dw = vjp(jnp.ones((), F32))
        return dx, dfo, dw, jnp.full((1, 128), loss, F32)
    return rowwise("final_fwd_bwd", fn, [(x1, D, 0), (fo, D, 0), (target, D, 0)], [(w, D, 0)],
                   [(D, D, F32), (D, D, BF16)], [(1, D, D), (1, 128, 128)], tt=256, rb=ROW_RB)


HALO = 8
CONV_TT = 512
CONV_CB = 512
CONV_RB = 32


def _tail(kind, c, up):
    return _silu(c) if kind == "silu" else _gelu_tanh(c) * up


def conv_fwd(name, x, xoff, w, b, kind, up=None, upoff=0, act_dtype=F32):
    t = x.shape[0]
    k_, c_ = w.shape
    tt, cb = CONV_TT, CONV_CB
    hb = tt // HALO
    has_up = up is not None

    def body(*refs):
        if has_up:
            x_ref, xp_ref, w_ref, b_ref, up_ref, c_ref, a_ref, scr = refs
        else:
            x_ref, xp_ref, w_ref, b_ref, c_ref, a_ref, scr = refs
        i = pl.program_id(1)
        scr[0:HALO, :] = jnp.where(i == 0, 0.0, xp_ref[...])
        scr[HALO:HALO + tt, :] = x_ref[...]
        for r in range(tt // CONV_RB):
            rows = pl.ds(r * CONV_RB, CONV_RB)
            acc = jnp.zeros((CONV_RB, cb), F32) + b_ref[...]
            for k in range(k_):
                acc = acc + w_ref[k:k + 1, :] * scr[pl.ds(r * CONV_RB + HALO - (k_ - 1) + k, CONV_RB), :]
            c_ref[rows, :] = acc
            a_ref[rows, :] = _tail(kind, acc, up_ref[rows, :] if has_up else None).astype(act_dtype)

    in_specs = [pl.BlockSpec((tt, cb), lambda j, i: (i, xoff + j)),
                pl.BlockSpec((HALO, cb), lambda j, i: (jnp.maximum(i * hb - 1, 0), xoff + j)),
                pl.BlockSpec((k_, cb), lambda j, i: (0, j)),
                pl.BlockSpec((1, cb), lambda j, i: (0, j))]
    args = [x, x, w, b]
    if has_up:
        in_specs.append(pl.BlockSpec((tt, cb), lambda j, i: (i, upoff + j)))
        args.append(up)
    return pl.pallas_call(
        body, name=name, grid=(c_ // cb, t // tt), in_specs=in_specs,
        out_specs=[pl.BlockSpec((tt, cb), lambda j, i: (i, j))] * 2,
        out_shape=[jax.ShapeDtypeStruct((t, c_), F32), jax.ShapeDtypeStruct((t, c_), act_dtype)],
        scratch_shapes=[pltpu.VMEM((tt + HALO, cb), F32)],
        compiler_params=_cp(("parallel", "arbitrary")),
    )(*args)


def conv_bwd(name, x, xoff, c, coff, dact, w, kind, up=None, upoff=0):
    t = x.shape[0]
    k_, c_ = w.shape[0], dact.shape[1]
    tt, cb = CONV_TT, CONV_CB
    hb = tt // HALO
    nt = t // tt
    has_up = up is not None

    def tail_grad(cv, dav, upv):
        if has_up:
            _, vjp = jax.vjp(lambda a, u: _tail(kind, a, u), cv, upv)
            return vjp(dav)
        _, vjp = jax.vjp(lambda a: _tail(kind, a, None), cv)
        return vjp(dav)[0], None

    def body(*refs):
        if has_up:
            (x_ref, xp_ref, c_ref, cn_ref, da_ref, dan_ref, w_ref, up_ref, upn_ref,
             dx_ref, dup_ref, dw_ref, db_ref, xs, dcs) = refs
        else:
            x_ref, xp_ref, c_ref, cn_ref, da_ref, dan_ref, w_ref, dx_ref, dw_ref, db_ref, xs, dcs = refs
        i = pl.program_id(1)
        xs[0:HALO, :] = jnp.where(i == 0, 0.0, xp_ref[...])
        xs[HALO:HALO + tt, :] = x_ref[...]
        rb = CONV_RB
        for r in range(tt // rb):
            rows = pl.ds(r * rb, rb)
            dc, dup = tail_grad(c_ref[rows, :], da_ref[rows, :].astype(F32), up_ref[rows, :] if has_up else None)
            dcs[rows, :] = dc
            if has_up:
                dup_ref[rows, :] = dup.astype(BF16)
        dcn, _ = tail_grad(cn_ref[...], dan_ref[...].astype(F32), upn_ref[...] if has_up else None)
        dcs[tt:tt + HALO, :] = jnp.where(i == nt - 1, 0.0, dcn)
        dws = [jnp.zeros((1, cb), F32) for _ in range(k_)]
        dbv = jnp.zeros((1, cb), F32)
        for r in range(tt // rb):
            rows = pl.ds(r * rb, rb)
            dc = dcs[rows, :]
            dx = jnp.zeros((rb, cb), F32)
            for k in range(k_):
                dx = dx + w_ref[k:k + 1, :] * dcs[pl.ds(r * rb + k_ - 1 - k, rb), :]
                dws[k] = dws[k] + _colsum(dc * xs[pl.ds(r * rb + HALO - (k_ - 1) + k, rb), :])
            dbv = dbv + _colsum(dc)
            dx_ref[rows, :] = dx.astype(BF16)

        @pl.when(i == 0)
        def _():
            dw_ref[...] = jnp.zeros_like(dw_ref)
            db_ref[...] = jnp.zeros_like(db_ref)

        for k in range(k_):
            dw_ref[k:k + 1, :] += dws[k]
        db_ref[...] += dbv

    tile = lambda off: pl.BlockSpec((tt, cb), lambda j, i, off=off: (i, off + j))
    prev = lambda off: pl.BlockSpec((HALO, cb), lambda j, i, off=off: (jnp.maximum(i * hb - 1, 0), off + j))
    nxt = lambda off: pl.BlockSpec((HALO, cb), lambda j, i, off=off: (jnp.minimum((i + 1) * hb, t // HALO - 1), off + j))
    in_specs = [tile(xoff), prev(xoff), tile(coff), nxt(coff), tile(0), nxt(0),
                pl.BlockSpec((k_, cb), lambda j, i: (0, coff + j))]
    args = [x, x, c, c, dact, dact, w]
    if has_up:
        in_specs += [tile(upoff), nxt(upoff)]
        args += [up, up]
    out_specs = [tile(0)] + ([tile(0)] if has_up else []) + [pl.BlockSpec((HALO, cb), lambda j, i: (0, j)),
                                                            pl.BlockSpec((1, cb), lambda j, i: (0, j))]
    out_shape = [jax.ShapeDtypeStruct((t, c_), BF16)] * (2 if has_up else 1)
    out_shape += [jax.ShapeDtypeStruct((HALO, c_), F32), jax.ShapeDtypeStruct((1, c_), F32)]
    return pl.pallas_call(
        body, name=name, grid=(c_ // cb, nt), in_specs=in_specs, out_specs=out_specs, out_shape=out_shape,
        scratch_shapes=[pltpu.VMEM((tt + HALO, cb), F32), pltpu.VMEM((tt + HALO, cb), F32)],
        compiler_params=_cp(("parallel", "arbitrary")),
    )(*args)


def ew_sum(name, parts, rows, out_dtype, tr):
    c = parts[0][0].shape[1]
    tr = min(tr, rows)
    assert rows % tr == 0 and all(off % tr == 0 for _, off in parts)
    n = len(parts)

    def body(*refs):
        acc = refs[0][...].astype(F32)
        for ref in refs[1:n]:
            acc = acc + ref[...].astype(F32)
        refs[n][...] = acc.astype(out_dtype)

    in_specs = [pl.BlockSpec((tr, c), lambda i, o=off // tr: (i + o, 0)) for _, off in parts]
    return pl.pallas_call(body, name=name, grid=(rows // tr,), in_specs=in_specs,
                          out_specs=pl.BlockSpec((tr, c), lambda i: (i, 0)),
                          out_shape=jax.ShapeDtypeStruct((rows, c), out_dtype),
                          compiler_params=_cp(("parallel",)))(*[a for a, _ in parts])


def fold_heads(dexp):
    def body(d_ref, o_ref):
        sel = jnp.where(_iota((SSM_DINNER, DT_PAD), 0) // SSM_HEADDIM == _iota((SSM_DINNER, DT_PAD), 1), 1.0, 0.0)
        o_ref[...] = _hdot(jnp.broadcast_to(d_ref[...], (8, SSM_DINNER)), sel.astype(F32), "nn", "a")[0:1, :]

    return pl.pallas_call(body, name="fold_heads", out_shape=jax.ShapeDtypeStruct((1, DT_PAD), F32),
                          compiler_params=pltpu.CompilerParams(vmem_limit_bytes=VMEM_LIMIT))(dexp)


def adamw(name, w, g, m, v, tr):
    r, c = w.shape
    tr = min(tr, r)
    assert r % tr == 0, (r, tr)

    def body(w_ref, g_ref, m_ref, v_ref, d_ref, nm_ref, nv_ref):
        gv = g_ref[...]
        nm = ADAM_B1 * m_ref[...] + (1.0 - ADAM_B1) * gv
        nv = ADAM_B2 * v_ref[...] + (1.0 - ADAM_B2) * (gv * gv)
        m_hat = nm / (1.0 - ADAM_B1 ** ADAM_STEP)
        v_hat = nv / (1.0 - ADAM_B2 ** ADAM_STEP)
        d_ref[...] = -ADAM_LR * (m_hat / (jnp.sqrt(v_hat) + ADAM_EPS) + ADAM_WD * w_ref[...])
        nm_ref[...] = nm
        nv_ref[...] = nv

    spec = pl.BlockSpec((tr, c), lambda i: (i, 0))
    shp = jax.ShapeDtypeStruct((r, c), F32)
    return pl.pallas_call(body, name=name, grid=(r // tr,), in_specs=[spec] * 4, out_specs=[spec] * 3,
                          out_shape=[shp] * 3, compiler_params=_cp(("parallel",)))(w, g, m, v)


SEG_QFIG, SEG_Z, SEG_XBC, SEG_DT, SEG_G = 0, 8192, 12288, 18432, 18496
IN_TOTAL = 22592
FFN_BLOCKS = D_FF // CONV_CB


def _own_slot(gathered, own, shard):
    slot = lax.broadcasted_iota(jnp.int32, (gathered.shape[0],) + (1,) * own.ndim, 0)
    return jnp.where(slot == shard, own[None], gathered)


def local_step(x, target, wts, par, p_rest, p_up, shard, core):
    t = x.shape[0]
    pad64 = lambda a: jnp.pad(a, ((0, 0), (0, DT_PAD - a.shape[1])))
    bias, a_log = pad64(par["ssm_dt_bias"]), pad64(par["ssm_A_log"])
    dexp = jnp.repeat(par["ssm_D"], SSM_HEADDIM, axis=1)
    in_t = wts["in_t"]

    h = norm_fwd(x, par["mix_pre_norm"])
    proj = lambda nm, off, n, tn: mm(h, in_t, "nt", name=nm, tn=tn, dims=(t, n, D), b_off=(off, 0))
    qfig = proj("proj_qfig", SEG_QFIG, 8192, 1024)
    z = proj("proj_z", SEG_Z, 4096, 1024)
    xbc = proj("proj_xbc", SEG_XBC, 6144, 1024)
    dtr = mm(h, wts["dt_t"], "nt", name="proj_dt", tn=128)
    gates = mm(h, wts["g_t"], "nt", name="proj_gates", out_dtype=BF16, tn=1024)
    y_hg, hg_states, g_rest = hgrn2_fwd(qfig, par["hg_lb_table"], par["hg_out_norm"], gather=([p_rest], [REST_PIECES]))
    c_ssm, xbc_act = conv_fwd("ssm_conv_fwd", xbc, 0, par["ssm_conv_w"], par["ssm_conv_b"], "silu")
    dt, acum = dt_fwd(dtr, bias, a_log)
    yssd, ssd_states, g_up = ssd_fwd(xbc_act, dt, acum, gather=([p_up], [UP_PIECES]))
    g_rest, g_up = _own_slot(g_rest, p_rest, shard), _own_slot(g_up, p_up, shard)
    r0, r1, r2, r3 = REST_SPLITS
    wts = dict(wts, bh=g_rest[:, :r0].reshape(-1, D), bs=g_rest[:, r0:r1].reshape(-1, D), o=g_rest[:, r1:r2].reshape(-1, D),
               dn=g_rest[:, r2:r3].reshape(-1, D), up=jnp.transpose(g_up, (1, 0, 2)).reshape(D, 2 * D_FF))
    y_ssm = ssm_post_fwd(yssd, xbc_act, z, dexp, par["ssm_out_norm"])
    a_hg = mm(y_hg, wts["bh"], "nn", name="branch_hg", out_dtype=BF16, tn=1024)
    a_ssm = mm(y_ssm, wts["bs"], "nn", name="branch_ssm", out_dtype=BF16, tn=1024)
    mixed = merge_fwd(a_hg, a_ssm, gates)
    u = mm(mixed, wts["o"], "nn", name="out_proj", tn=1024)
    x1, h2 = post1_fwd(x, u, par["mix_post_norm"], par["ffn_pre_norm"])
    gu = mm(h2, wts["up"], "nn", name="ffn_up", tn=1024)
    c_ffn, act = conv_fwd("ffn_conv_fwd", gu, 0, par["ffn_conv_w"], par["ffn_conv_b"], "gelu_mul",
                          up=gu, upoff=FFN_BLOCKS, act_dtype=BF16)
    fo = mm(act, wts["dn"], "nn", name="ffn_down", tm=512, tn=1024)
    dx2, dfo, g_ffn_post, loss = final_fwd_bwd(x1, fo, par["ffn_post_norm"], target)

    dact = mm(dfo, wts["dn"], "nt", name="d_act", out_dtype=BF16, tn=1408)
    g_dn = mm(act, dfo, "tn", name="g_ffn_down", out_dtype=BF16, tm=1408, tn=1024, tk=2048)
    dgate, dup, g_fcw, g_fcb = conv_bwd("ffn_conv_bwd", gu, 0, c_ffn, 0, dact, par["ffn_conv_w"], "gelu_mul",
                                        up=gu, upoff=FFN_BLOCKS)
    dh2 = mm_segments("d_h2", [(dgate, 0, 0), (dup, 0, D_FF)], [wts["up"]], tm=1024, tn=1024, tk=1408, bt=True)
    g_up_gate = mm(h2, dgate, "tn", name="g_ffn_up_gate", out_dtype=BF16, tm=1024, tn=1408, tk=2048)
    g_up_up = mm(h2, dup, "tn", name="g_ffn_up_up", out_dtype=BF16, tm=1024, tn=1408, tk=2048)
    du, dx1, g_mix_post, g_ffn_pre = post1_bwd(x, u, par["mix_post_norm"], par["ffn_pre_norm"], dx2, dh2)
    dmixed = mm(du, wts["o"], "nt", name="d_mixed", tn=1024)
    g_o = mm(mixed, du, "tn", name="g_w_out", out_dtype=BF16, tm=1024, tn=2048, tk=2048)
    da_hg, da_ssm, dg_hg, dg_ssm = merge_bwd(a_hg, a_ssm, gates, dmixed)
    dy_hg = mm(da_hg, wts["bh"], "nt", name="d_y_hg", out_dtype=BF16, tn=1024)
    g_bh = mm(y_hg, da_hg, "tn", name="g_w_branch_hg", out_dtype=BF16, tm=1024, tn=2048, tk=2048)
    dy_ssm = mm(da_ssm, wts["bs"], "nt", name="d_y_ssm", out_dtype=BF16, tn=1024)
    g_bs = mm(y_ssm, da_ssm, "tn", name="g_w_branch_ssm", out_dtype=BF16, tm=1024, tn=2048, tk=2048)
    dyssd, dskip, dz, g_dexp, g_ssm_norm = ssm_post_bwd(yssd, xbc_act, z, dexp, par["ssm_out_norm"], dy_ssm)

    gg_rest = jnp.concatenate([g.reshape(N_CHIPS, -1, D) for g in (g_bh, g_bs, g_o, g_dn)], axis=1)
    gg_up = jnp.transpose(jnp.concatenate([g_up_gate, g_up_up], axis=1).reshape(D, N_CHIPS, UP_COLS), (1, 0, 2))
    c_rest, c_up = pair_reduce("rest", [gg_rest, gg_up], [REST_PIECES, UP_PIECES], [432, 512], core)
    dxs, db_, dc_, ddt, dacum, rb_rest, rb_up = ssd_bwd(xbc_act, dt, acum, ssd_states, dyssd, dskip, exchange=[c_rest, c_up])
    red_rest, red_up = chip_reduce("rest", [c_rest, c_up], [rb_rest, rb_up], [432, 256], shard)
    ddtr, g_dt_bias, g_a_log = dt_bwd(dtr, bias, a_log, ddt, dacum)
    xs_blocks, bc_blocks = SSM_DINNER // CONV_CB, SSM_GROUPS * SSM_DSTATE // CONV_CB
    dxbc_x, g_cw_x, g_cb_x = conv_bwd("ssm_conv_bwd_x", xbc, 0, c_ssm, 0, dxs, par["ssm_conv_w"], "silu")
    dxbc_b, g_cw_b, g_cb_b = conv_bwd("ssm_conv_bwd_b", xbc, xs_blocks, c_ssm, xs_blocks, db_, par["ssm_conv_w"], "silu")
    dxbc_c, g_cw_c, g_cb_c = conv_bwd("ssm_conv_bwd_c", xbc, xs_blocks + bc_blocks, c_ssm, xs_blocks + bc_blocks, dc_,
                                      par["ssm_conv_w"], "silu")
    dqfig, g_table, g_hg_norm = hgrn2_bwd(qfig, par["hg_lb_table"], par["hg_out_norm"], hg_states, dy_hg)

    dsegs = [(dqfig, SEG_QFIG), (dz, SEG_Z),
             (dxbc_x, SEG_XBC), (dxbc_b, SEG_XBC + SSM_DINNER), (dxbc_c, SEG_XBC + SSM_DINNER + 1024)]
    g_in_parts = [mm(dseg, h, "tn", name=f"g_w_in_{n}", out_dtype=BF16, tm=1024, tn=2048, tk=2048)
                  for n, (dseg, _) in enumerate(dsegs)]
    g_dt_t = mm(ddtr, h, "tn", name="g_w_in_dt", out_dtype=BF16, tm=128, tn=2048, tk=1024)[:SSM_HEADS]
    g_in_parts += [g_dt_t] + [mm(dgate_, h, "tn", name=f"g_w_in_g{n}", out_dtype=BF16, tm=1024, tn=2048, tk=2048)
                              for n, dgate_ in enumerate((dg_hg, dg_ssm))]
    zpad = jnp.zeros((N_CHIPS, IN_ROWS - IN_SHARD, D), BF16)
    g_in_t = jnp.concatenate(g_in_parts, axis=0).reshape(N_CHIPS, IN_SHARD, D)
    (c_in,) = pair_reduce("in", [jnp.concatenate([g_in_t, zpad], axis=1)], [IN_PIECES], [960], core)
    dh = mm_segments("d_h_a", [(dseg, 0, off) for dseg, off in dsegs[:2]], [in_t], tm=1024, tn=1024, tk=2048)
    dh, rb_in = mm_segments("d_h_b", [(dseg, 0, off) for dseg, off in dsegs[2:]] + [(ddtr, 1, 0), (dg_hg, 2, 0), (dg_ssm, 2, D)],
                            [in_t, wts["dt_t"], wts["g_t"]], tm=1024, tn=1024, tk=1024, acc=dh, exchange=[c_in])
    (red_in,) = chip_reduce("in", [c_in], [rb_in], [480], shard)
    grad_x, g_mix_pre = norm_bwd(x, par["mix_pre_norm"], dh, dx1)

    big = dict(in_t=red_in, rest=red_rest, up=red_up)
    g_conv_w = jnp.concatenate([g_cw_x, g_cw_b, g_cw_c], axis=1)[:SSM_CONV]
    g_conv_b = jnp.concatenate([g_cb_x, g_cb_b, g_cb_c], axis=1)
    small = dict(mix_pre_norm=g_mix_pre, mix_post_norm=g_mix_post, hg_lb_table=g_table, hg_out_norm=g_hg_norm,
                 ssm_conv_w=g_conv_w, ssm_conv_b=g_conv_b, ssm_dt_bias=g_dt_bias, ssm_A_log=g_a_log,
                 ssm_D=g_dexp, ssm_out_norm=g_ssm_norm, ffn_pre_norm=g_ffn_pre, ffn_post_norm=g_ffn_post,
                 ffn_conv_w=g_fcw[:FFN_CONV], ffn_conv_b=g_fcb)
    return loss, grad_x, big, small


MESH = pl.DeviceIdType.MESH
ANY = pl.BlockSpec(memory_space=pl.ANY)
N_CHIPS = 4
IN_SHARD = 5648
IN_ROWS = 5760
REST_SPLITS = (512, 1536, 2048, 3456)
UP_COLS = 2816
IN_PIECES, REST_PIECES, UP_PIECES = 3, 4, 4


def _place():
    x, y, c = lax.axis_index("x"), lax.axis_index("y"), lax.axis_index("c")
    chips = [(1 - x, y), (x, 1 - y), (1 - x, 1 - y)]
    return x, y, c, chips


def _rcopy(src, dst, send_sems, recv_sems, k, dev):
    return pltpu.make_async_remote_copy(src_ref=src, dst_ref=dst, send_sem=send_sems.at[k], recv_sem=recv_sems.at[k],
                                        device_id=dev, device_id_type=MESH)


def _pieces(rows, n):
    assert rows % n == 0 and (rows // n) % 16 == 0, (rows, n)
    return [(k * (rows // n), rows // n) for k in range(n)]


def _rows(c, hrows, piece):
    return pl.ds(pl.multiple_of(c * hrows + piece[0], 16), piece[1])


def _half_plan(arrays, pieces):
    return [(a.shape[-2] // 2, _pieces(a.shape[-2] // 2, n)) for a, n in zip(arrays, pieces)]


def _sem_pair(n):
    return [pltpu.SemaphoreType.DMA((n,)), pltpu.SemaphoreType.DMA((n,))]


class _Gather:
    def __init__(self, ps, pieces):
        self.plan = _half_plan(ps, pieces)
        self.n_sem = sum(2 * 3 * len(pcs) for _, pcs in self.plan)
        self.out_shape = [jax.ShapeDtypeStruct((N_CHIPS,) + p.shape, p.dtype) for p in ps]

    def _copies(self, p_refs, g_refs, send_sems, recv_sems, only_first=False):
        x, y, c, chips = _place()
        own = 2 * x + y
        sib = (x, y, 1 - c)
        first, arrive, passed, from_sib = [], [], [], []
        k = 0
        for p, g, (hrows, pcs) in zip(p_refs, g_refs, self.plan):
            for chip in chips:
                theirs = 2 * chip[0] + chip[1]
                for pc in pcs:
                    mine, other = _rows(c, hrows, pc), _rows(1 - c, hrows, pc)
                    first.append(_rcopy(p.at[mine], g.at[own, mine], send_sems, recv_sems, k, (*chip, c)))
                    if not only_first:
                        arrive.append(_rcopy(g.at[theirs, mine], g.at[theirs, mine], send_sems, recv_sems, k, (*chip, c)))
                        passed.append(_rcopy(g.at[theirs, mine], g.at[theirs, mine], send_sems, recv_sems, k + 1, sib))
                        from_sib.append(_rcopy(g.at[theirs, other], g.at[theirs, other], send_sems, recv_sems, k + 1, sib))
                    k += 2
        return first, arrive, passed, from_sib

    def start(self, p_refs, g_refs, send_sems, recv_sems):
        for cp in self._copies(p_refs, g_refs, send_sems, recv_sems, only_first=True)[0]:
            cp.start()

    def finish(self, p_refs, g_refs, send_sems, recv_sems):
        first, arrive, passed, from_sib = self._copies(p_refs, g_refs, send_sems, recv_sems)
        for got, fw in zip(arrive, passed):
            got.wait_recv()
            fw.start()
        for cp in from_sib:
            cp.wait_recv()
        for cp in first + passed:
            cp.wait_send()


def gather_weights(name, ps, pieces):
    op = _Gather(ps, pieces)
    n = len(ps)

    def body(*refs):
        p_refs, g_refs, sems = refs[:n], refs[n:2 * n], refs[2 * n:]
        op.start(p_refs, g_refs, *sems)
        op.finish(p_refs, g_refs, *sems)

    return pl.pallas_call(body, name=name, in_specs=[ANY] * n, out_specs=[ANY] * n, out_shape=op.out_shape,
                          scratch_shapes=_sem_pair(op.n_sem))(*ps)


def pair_exchange(name, gs, pieces):
    plan = _half_plan(gs, pieces)
    n_sem = sum(N_CHIPS * len(pcs) for _, pcs in plan)
    n = len(gs)

    def body(*refs):
        g_refs, r_refs, send_sems, recv_sems = refs[:n], refs[n:2 * n], refs[2 * n], refs[2 * n + 1]
        x, y, c, _ = _place()
        sib = (x, y, 1 - c)
        cps = []
        for g, r, (hrows, pcs) in zip(g_refs, r_refs, plan):
            for s in range(N_CHIPS):
                for pc in pcs:
                    cps.append(_rcopy(g.at[s, _rows(1 - c, hrows, pc)], r.at[s, pl.ds(pc[0], pc[1])],
                                      send_sems, recv_sems, len(cps), sib))
        for cp in cps:
            cp.start()
        for cp in cps:
            cp.wait()

    return pl.pallas_call(
        body, name=name, in_specs=[ANY] * n, out_specs=[ANY] * n,
        out_shape=[jax.ShapeDtypeStruct((N_CHIPS, g.shape[1] // 2, g.shape[2]), g.dtype) for g in gs],
        scratch_shapes=_sem_pair(n_sem))(*gs)


class _ChipExchange:
    def __init__(self, ss):
        self.n_sem = 3 * len(ss)
        self.out_shape = [jax.ShapeDtypeStruct((3,) + s.shape[1:], s.dtype) for s in ss]

    def _copies(self, s_refs, r_refs, send_sems, recv_sems):
        x, y, c, chips = _place()
        cps = []
        for s, r in zip(s_refs, r_refs):
            for j, chip in enumerate(chips):
                cps.append(_rcopy(s.at[2 * chip[0] + chip[1]], r.at[j], send_sems, recv_sems, len(cps), (*chip, c)))
        return cps

    def start(self, *refs):
        for cp in self._copies(*refs):
            cp.start()

    def finish(self, *refs):
        for cp in self._copies(*refs):
            cp.wait()


def pair_assemble(name, rs, pieces):
    plan = [(r.shape[0], _pieces(r.shape[0], n_)) for r, n_ in zip(rs, pieces)]
    n_sem = sum(len(pcs) for _, pcs in plan)
    n = len(rs)

    def body(*refs):
        r_refs, f_refs, send_sems, recv_sems = refs[:n], refs[n:2 * n], refs[2 * n], refs[2 * n + 1]
        x, y, c, _ = _place()
        sib = (x, y, 1 - c)
        cps, got = [], []
        for r, f, (hrows, pcs) in zip(r_refs, f_refs, plan):
            for pc in pcs:
                src = r.at[pl.ds(pc[0], pc[1])]
                cps.append(_rcopy(src, f.at[_rows(c, hrows, pc)], send_sems, recv_sems, len(cps), sib))
                got.append(_rcopy(src, f.at[_rows(1 - c, hrows, pc)], send_sems, recv_sems, len(got), sib))
        for cp in cps:
            cp.start()
        for cp in got:
            cp.wait_recv()
        for cp in cps:
            cp.wait_send()

    return pl.pallas_call(
        body, name=name, in_specs=[ANY] * n, out_specs=[ANY] * n,
        out_shape=[jax.ShapeDtypeStruct((2 * r.shape[0], r.shape[1]), r.dtype) for r in rs],
        scratch_shapes=_sem_pair(n_sem))(*rs)


def pair_reduce(tag, ggs, pieces, trs, core):
    recv = pair_exchange("pair_exchange_" + tag, ggs, pieces)
    flat = lambda a: a.reshape(-1, a.shape[-1])
    out = []
    for n, (gg, r, tr) in enumerate(zip(ggs, recv, trs)):
        h = gg.shape[1] // 2
        own = lax.dynamic_slice_in_dim(gg, core * h, h, axis=1)
        out.append(ew_sum(f"pair_sum_{tag}_{n}", [(flat(own), 0), (flat(r), 0)], N_CHIPS * h, BF16, tr).reshape(r.shape))
    return out


def chip_reduce(tag, cs, rbs, trs, shard):
    out = []
    for n, (c, rb, tr) in enumerate(zip(cs, rbs, trs)):
        h = c.shape[1]
        own = lax.dynamic_index_in_dim(c, shard, axis=0, keepdims=False)
        parts = [(own, 0)] + [(rb.reshape(-1, rb.shape[-1]), j * h) for j in range(3)]
        out.append(ew_sum(f"chip_sum_{tag}_{n}", parts, h, F32, tr))
    return out


N_DEV = 8


def gather_small(blk, reduce):
    rows, cols = blk.shape

    def body(x_ref, out_ref, all_ref, send_sems, recv_sems, local_sem):
        x, y, c, chips = _place()
        me, sib = (x, y, c), (x, y, 1 - c)

        def blk_rows(px, py, pc):
            return all_ref.at[pl.ds(pl.multiple_of((4 * px + 2 * py + pc) * rows, 8), rows), :]

        def copy(k, block, to, src=None):
            return _rcopy(blk_rows(*block) if src is None else src, blk_rows(*block), send_sems, recv_sems, k, to)

        mine = pltpu.make_async_copy(x_ref, blk_rows(*me), local_sem)
        mine.start()
        first = [copy(0, me, sib, src=x_ref)] + [copy(1 + j, me, (*chip, c), src=x_ref) for j, chip in enumerate(chips)]
        for cp in first:
            cp.start()
        passed = [copy(4 + j, (*chip, c), sib) for j, chip in enumerate(chips)]
        for j, chip in enumerate(chips):
            copy(1 + j, (*chip, c), me).wait_recv()
            passed[j].start()
        copy(0, sib, me).wait_recv()
        for j, chip in enumerate(chips):
            copy(4 + j, (*chip, 1 - c), me).wait_recv()
        for cp in first + passed:
            cp.wait_send()
        mine.wait()
        if reduce:
            acc = all_ref[0:rows, :]
            for d in range(1, N_DEV):
                acc = acc + all_ref[d * rows:(d + 1) * rows, :]
            out_ref[...] = acc
        else:
            out_ref[...] = all_ref[...]

    vmem = pl.BlockSpec(memory_space=pltpu.VMEM)
    return pl.pallas_call(
        body, name="reduce_small" if reduce else "gather_small", in_specs=[vmem], out_specs=vmem,
        out_shape=jax.ShapeDtypeStruct((rows if reduce else N_DEV * rows, cols), blk.dtype),
        scratch_shapes=[pltpu.VMEM((N_DEV * rows, cols), blk.dtype), pltpu.SemaphoreType.DMA((7,)),
                        pltpu.SemaphoreType.DMA((7,)), pltpu.SemaphoreType.DMA],
        compiler_params=pltpu.CompilerParams(vmem_limit_bytes=VMEM_LIMIT),
    )(blk)


WEIGHTS = ['w_in', 'mix_pre_norm', 'mix_post_norm', 'hg_lb_table', 'hg_out_norm', 'ssm_conv_w', 'ssm_conv_b',
           'ssm_dt_bias', 'ssm_A_log', 'ssm_D', 'ssm_out_norm', 'w_branch_hg', 'w_branch_ssm', 'w_out', 'ffn_pre_norm',
           'ffn_post_norm', 'ffn_w_up', 'ffn_conv_w', 'ffn_conv_b', 'ffn_w_down']
BIG = ('w_in', 'w_branch_hg', 'w_branch_ssm', 'w_out', 'ffn_w_up', 'ffn_w_down')
SMALL = tuple(n for n in WEIGHTS if n not in BIG)
CONV_SHARD = {'ssm_conv_w': SSM_CONV_DIM // N_CHIPS, 'ffn_conv_w': D_FF // N_CHIPS}
LANES = 128


def _pack(parts):
    flat = jnp.concatenate([p.reshape(-1) for p in parts])
    n = flat.shape[0]
    rows = -(-n // (8 * LANES)) * 8
    return jnp.pad(flat, (0, rows * LANES - n)).reshape(rows, LANES)


def _unpack(packed, shapes):
    flat = packed.reshape(-1)
    out, off = [], 0
    for s in shapes:
        n = int(np.prod(s))
        out.append(flat[off:off + n].reshape(s))
        off += n
    return out


def kernel(x, w_in, mix_pre_norm, mix_post_norm, hg_lb_table, hg_out_norm, ssm_conv_w, ssm_conv_b, ssm_dt_bias, ssm_A_log, ssm_D, ssm_out_norm, w_branch_hg, w_branch_ssm, w_out, ffn_pre_norm, ffn_post_norm, ffn_w_up, ffn_conv_w, ffn_conv_b, ffn_w_down, loss_target, m_w_in, m_mix_pre_norm, m_mix_post_norm, m_hg_lb_table, m_hg_out_norm, m_ssm_conv_w, m_ssm_conv_b, m_ssm_dt_bias, m_ssm_A_log, m_ssm_D, m_ssm_out_norm, m_w_branch_hg, m_w_branch_ssm, m_w_out, m_ffn_pre_norm, m_ffn_post_norm, m_ffn_w_up, m_ffn_conv_w, m_ffn_conv_b, m_ffn_w_down, v_w_in, v_mix_pre_norm, v_mix_post_norm, v_hg_lb_table, v_hg_out_norm, v_ssm_conv_w, v_ssm_conv_b, v_ssm_dt_bias, v_ssm_A_log, v_ssm_D, v_ssm_out_norm, v_w_branch_hg, v_w_branch_ssm, v_w_out, v_ffn_pre_norm, v_ffn_post_norm, v_ffn_w_up, v_ffn_conv_w, v_ffn_conv_b, v_ffn_w_down):
    w = dict(w_in=w_in, mix_pre_norm=mix_pre_norm, mix_post_norm=mix_post_norm, hg_lb_table=hg_lb_table, hg_out_norm=hg_out_norm, ssm_conv_w=ssm_conv_w, ssm_conv_b=ssm_conv_b, ssm_dt_bias=ssm_dt_bias, ssm_A_log=ssm_A_log, ssm_D=ssm_D, ssm_out_norm=ssm_out_norm, w_branch_hg=w_branch_hg, w_branch_ssm=w_branch_ssm, w_out=w_out, ffn_pre_norm=ffn_pre_norm, ffn_post_norm=ffn_post_norm, ffn_w_up=ffn_w_up, ffn_conv_w=ffn_conv_w, ffn_conv_b=ffn_conv_b, ffn_w_down=ffn_w_down)
    m = dict(w_in=m_w_in, mix_pre_norm=m_mix_pre_norm, mix_post_norm=m_mix_post_norm, hg_lb_table=m_hg_lb_table, hg_out_norm=m_hg_out_norm, ssm_conv_w=m_ssm_conv_w, ssm_conv_b=m_ssm_conv_b, ssm_dt_bias=m_ssm_dt_bias, ssm_A_log=m_ssm_A_log, ssm_D=m_ssm_D, ssm_out_norm=m_ssm_out_norm, w_branch_hg=m_w_branch_hg, w_branch_ssm=m_w_branch_ssm, w_out=m_w_out, ffn_pre_norm=m_ffn_pre_norm, ffn_post_norm=m_ffn_post_norm, ffn_w_up=m_ffn_w_up, ffn_conv_w=m_ffn_conv_w, ffn_conv_b=m_ffn_conv_b, ffn_w_down=m_ffn_w_down)
    v = dict(w_in=v_w_in, mix_pre_norm=v_mix_pre_norm, mix_post_norm=v_mix_post_norm, hg_lb_table=v_hg_lb_table, hg_out_norm=v_hg_out_norm, ssm_conv_w=v_ssm_conv_w, ssm_conv_b=v_ssm_conv_b, ssm_dt_bias=v_ssm_dt_bias, ssm_A_log=v_ssm_A_log, ssm_D=v_ssm_D, ssm_out_norm=v_ssm_out_norm, w_branch_hg=v_w_branch_hg, w_branch_ssm=v_w_branch_ssm, w_out=v_w_out, ffn_pre_norm=v_ffn_pre_norm, ffn_post_norm=v_ffn_post_norm, ffn_w_up=v_ffn_w_up, ffn_conv_w=v_ffn_conv_w, ffn_conv_b=v_ffn_conv_b, ffn_w_down=v_ffn_w_down)
    shard = 2 * lax.axis_index("x") + lax.axis_index("y")
    bf = lambda a: a.astype(BF16)

    core = lax.axis_index("c")
    p_in = jnp.concatenate([bf(w_in[0].T), jnp.zeros((IN_ROWS - IN_SHARD, D_MODEL), BF16)], axis=0)
    p_rest = jnp.concatenate([bf(w_branch_hg[0]), bf(w_branch_ssm[0]), bf(w_out[0]), bf(ffn_w_down[0])], axis=0)
    p_up = bf(ffn_w_up[0])
    (g_in,) = gather_weights("gather_w_in", [p_in], [IN_PIECES])
    in_t = _own_slot(g_in, p_in, shard)[:, :IN_SHARD].reshape(IN_TOTAL, D_MODEL)
    wts = dict(in_t=in_t, g_t=in_t[SEG_G:], dt_t=jnp.pad(in_t[SEG_DT:SEG_G], ((0, DT_PAD - SSM_HEADS), (0, 0))))
    conv_cols = max(CONV_SHARD.values())
    padc = lambda a: jnp.pad(a, ((0, 0), (0, conv_cols - a.shape[1])))
    conv_blk = jnp.concatenate([padc(ssm_conv_w[0]), padc(ffn_conv_w[0]), jnp.zeros((1, conv_cols), F32)], axis=0)
    conv_all = gather_small(conv_blk, reduce=False)
    par = {n: w[n] for n in SMALL}
    par["ssm_conv_w"] = jnp.concatenate([conv_all[16 * s:16 * s + SSM_CONV, :CONV_SHARD['ssm_conv_w']] for s in range(N_CHIPS)], axis=1)
    par["ffn_conv_w"] = jnp.concatenate([conv_all[16 * s + SSM_CONV:16 * s + SSM_CONV + FFN_CONV, :CONV_SHARD['ffn_conv_w']]
                                         for s in range(N_CHIPS)], axis=1)

    loss, grad_x, big, small = local_step(x[0], loss_target[0], wts, par, p_rest, p_up, shard, core)
    loss = lax.psum(loss[0, 0], ("x", "y", "c"))

    halves = [big["in_t"], big["rest"], big["up"]]
    wholes = pair_assemble("pair_assemble", halves, [IN_PIECES, REST_PIECES, UP_PIECES])
    f_in, f_rest, f_up = [_own_slot(f.reshape((2,) + r.shape), r, core).reshape(f.shape) for f, r in zip(wholes, halves)]
    r0, r1, r2, r3 = REST_SPLITS
    grads = dict(w_in=f_in[:IN_SHARD].T, w_branch_hg=f_rest[:r0], w_branch_ssm=f_rest[r0:r1], w_out=f_rest[r1:r2],
                 ffn_w_down=f_rest[r2:r3], ffn_w_up=f_up)

    small["hg_out_norm"] = ew_sum("sum_heads", [(small["hg_out_norm"][hd], 0) for hd in range(HG_HEADS)], 1, F32, 1)
    small["ssm_D"] = fold_heads(small["ssm_D"])[:, :SSM_HEADS]
    small["ssm_dt_bias"] = small["ssm_dt_bias"][:, :SSM_HEADS]
    small["ssm_A_log"] = small["ssm_A_log"][:, :SSM_HEADS]
    shapes = [small[n].shape for n in SMALL]
    summed = _unpack(gather_small(_pack([small[n] for n in SMALL]), reduce=True), shapes)
    for n, g in zip(SMALL, summed):
        if n in CONV_SHARD:
            g = lax.dynamic_slice_in_dim(g, shard * CONV_SHARD[n], CONV_SHARD[n], axis=1)
        grads[n] = g

    two_d = lambda a: a.reshape(a.shape[-2], a.shape[-1])
    delta, new_m, new_v = {}, {}, {}
    for n, tr in (("w_in", 64), ("w_branch_hg", 128), ("w_branch_ssm", 128), ("w_out", 128), ("ffn_w_up", 128), ("ffn_w_down", 128)):
        delta[n], new_m[n], new_v[n] = adamw("adamw_" + n, two_d(w[n]), grads[n], two_d(m[n]), two_d(v[n]), tr)
    sm_shapes = [two_d(w[n]).shape for n in SMALL]
    packed = adamw("adamw_small", _pack([two_d(w[n]) for n in SMALL]), _pack([grads[n] for n in SMALL]),
                   _pack([two_d(m[n]) for n in SMALL]), _pack([two_d(v[n]) for n in SMALL]), 1024)
    for res, packed_res in zip((delta, new_m, new_v), packed):
        for n, a in zip(SMALL, _unpack(packed_res, sm_shapes)):
            res[n] = a
    shaped = lambda d: [d[n].reshape(w[n].shape) for n in WEIGHTS]
    return (loss, grad_x[None], *shaped(grads), *shaped(delta), *shaped(new_m), *shaped(new_v))
```

```python
import functools

import jax
import jax.numpy as jnp
import numpy as np
from jax import lax
from jax.experimental import pallas as pl
from jax.experimental.pallas import tpu as pltpu

F32 = jnp.float32
BF16 = jnp.bfloat16

D_MODEL = 2048
EPS = 1e-6
HG_HEADS = 16
HG_DK = 128
HG_CHUNK = 64
HG_SUB = 16
SSM_DINNER = 4096
SSM_HEADDIM = 64
SSM_HEADS = 64
SSM_GROUPS = 8
SSM_DSTATE = 128
SSM_CONV = 4
SSM_CHUNK = 256
SSM_CONV_DIM = 6144
D_FF = 5632
FFN_CONV = 3
DT_PAD = 128

ADAM_LR = 0.001
ADAM_B1 = 0.9
ADAM_B2 = 0.999
ADAM_EPS = 1e-08
ADAM_WD = 0.01
ADAM_STEP = 10

VMEM_LIMIT = 56 * 1024 * 1024
HI = lax.Precision.HIGHEST


def _cp(sem, **kw):
    return pltpu.CompilerParams(dimension_semantics=sem, vmem_limit_bytes=VMEM_LIMIT, **kw)


_DIMS = {"nn": (((1,), (0,)), ((), ())), "nt": (((1,), (1,)), ((), ())), "tn": (((0,), (0,)), ((), ()))}


def mm(a, b, mode, *, name, out_dtype=F32, tm=1024, tn=512, tk=None, acc=None, n_major=True,
       dims=None, a_off=(0, 0), b_off=(0, 0)):
    if dims is not None:
        M, N, K = dims
    else:
        if mode == "nn":
            (M, K), (K2, N) = a.shape, b.shape
        elif mode == "nt":
            (M, K), (N, K2) = a.shape, b.shape
        else:
            (K, M), (K2, N) = a.shape, b.shape
        assert K == K2, (a.shape, b.shape, mode)
    tm, tn = min(tm, M), min(tn, N)
    tk = K if tk is None else min(tk, K)
    assert M % tm == 0 and N % tn == 0 and K % tk == 0, (M, N, K, tm, tn, tk)
    a_blk = (tk, tm) if mode == "tn" else (tm, tk)
    b_blk = (tn, tk) if mode == "nt" else (tk, tn)
    assert all(o % s == 0 for o, s in zip(a_off, a_blk)) and all(o % s == 0 for o, s in zip(b_off, b_blk))
    ao0, ao1 = a_off[0] // a_blk[0], a_off[1] // a_blk[1]
    bo0, bo1 = b_off[0] // b_blk[0], b_off[1] // b_blk[1]
    nk = K // tk
    if n_major:
        grid = (N // tn, M // tm, nk)
        ij = lambda p0, p1: (p1, p0)
    else:
        grid = (M // tm, N // tn, nk)
        ij = lambda p0, p1: (p0, p1)

    def a_map(p0, p1, k):
        i, _ = ij(p0, p1)
        return (k + ao0, i + ao1) if mode == "tn" else (i + ao0, k + ao1)

    def b_map(p0, p1, k):
        _, j = ij(p0, p1)
        return (j + bo0, k + bo1) if mode == "nt" else (k + bo0, j + bo1)

    def o_map(p0, p1, k):
        return ij(p0, p1)

    a_spec = pl.BlockSpec(a_blk, a_map)
    b_spec = pl.BlockSpec(b_blk, b_map)
    o_spec = pl.BlockSpec((tm, tn), o_map)
    dims = _DIMS[mode]
    has_acc = acc is not None

    def body(*refs):
        if has_acc:
            a_ref, b_ref, c_ref, o_ref, acc_ref = refs
        else:
            a_ref, b_ref, o_ref, acc_ref = refs
        k = pl.program_id(2)
        part = lax.dot_general(a_ref[...], b_ref[...], dims, preferred_element_type=F32)
        if nk == 1:
            o_ref[...] = (part + c_ref[...].astype(F32) if has_acc else part).astype(out_dtype)
            return

        @pl.when(k == 0)
        def _():
            acc_ref[...] = part

        @pl.when(k > 0)
        def _():
            acc_ref[...] += part

        @pl.when(k == nk - 1)
        def _():
            r = acc_ref[...]
            if has_acc:
                r = r + c_ref[...].astype(F32)
            o_ref[...] = r.astype(out_dtype)

    in_specs = [a_spec, b_spec] + ([o_spec] if has_acc else [])
    args = (a, b) + ((acc,) if has_acc else ())
    return pl.pallas_call(
        body, name=name, grid=grid, in_specs=in_specs, out_specs=o_spec,
        out_shape=jax.ShapeDtypeStruct((M, N), out_dtype),
        scratch_shapes=[pltpu.VMEM((tm, tn) if nk > 1 else (8, 128), F32)],
        compiler_params=_cp(("parallel", "parallel", "arbitrary")),
    )(*args)


def mm_segments(name, segs, bs, *, tm, tn, tk, acc=None, exchange=None, bt=False):
    m_, n_ = segs[0][0].shape[0], bs[0].shape[0 if bt else 1]
    tm, tn = min(tm, m_), min(tn, n_)
    op = _ChipExchange(exchange) if exchange else None
    ne = (len(exchange) if exchange else 0)
    na = 0 if acc is None else 1
    steps, k0 = [], 0
    for a, bi, row in segs:
        w = a.shape[1]
        tks = min(tk, w)
        assert w % tks == 0 and row % tks == 0 and tks == min(tk, bs[bi].shape[1 if bt else 0]), (w, row, tks)
        steps.append((k0, w // tks, tks, bi, row // tks))
        k0 += w // tks
    nk = k0
    assert m_ % tm == 0 and n_ % tn == 0

    def a_spec(k_first, count, tks):
        return pl.BlockSpec((tm, tks), lambda j, i, k: (i, jnp.clip(k - k_first, 0, count - 1)))

    def b_spec(bi):
        mine = [s for s in steps if s[3] == bi]

        def index(j, i, k):
            blk = mine[0][4]
            for k_first, count, _, _, first_blk in mine:
                blk = jnp.where(k >= k_first, first_blk + jnp.minimum(k - k_first, count - 1), blk)
            return (j, blk) if bt else (blk, j)
        return pl.BlockSpec((tn, mine[0][2]) if bt else (mine[0][2], tn), index)

    ns = len(segs)

    nb = len(bs)
    grid = (n_ // tn, m_ // tm, nk)

    def body(*refs):
        a_refs, b_refs = refs[:ns], refs[ns:ns + nb]
        acc_in = refs[ns + nb] if na else None
        rest = refs[ns + nb + na:]
        ex_refs, o_ref, got_refs, acc_ref, sems = rest[:ne], rest[ne], rest[ne + 1:2 * ne + 1], rest[2 * ne + 1], rest[2 * ne + 2:]
        k = pl.program_id(2)
        if op:
            first = (pl.program_id(0) == 0) & (pl.program_id(1) == 0) & (k == 0)
            last = (pl.program_id(0) == grid[0] - 1) & (pl.program_id(1) == grid[1] - 1) & (k == nk - 1)

            @pl.when(first)
            def _():
                op.start(ex_refs, got_refs, *sems)

            @pl.when(last)
            def _():
                op.finish(ex_refs, got_refs, *sems)

        @pl.when(k == 0)
        def _():
            acc_ref[...] = acc_in[...] if na else jnp.zeros_like(acc_ref)

        for a_ref, (k_first, count, _, bi, _) in zip(a_refs, steps):
            @pl.when((k >= k_first) & (k < k_first + count))
            def _(a_ref=a_ref, bi=bi):
                acc_ref[...] += lax.dot_general(a_ref[...], b_refs[bi][...], _DIMS["nt" if bt else "nn"],
                                                preferred_element_type=F32)

        @pl.when(k == nk - 1)
        def _():
            o_ref[...] = acc_ref[...]

    any_spec = pl.BlockSpec(memory_space=pl.ANY)
    o_spec = pl.BlockSpec((tm, tn), lambda j, i, k: (i, j))
    outs = pl.pallas_call(
        body, name=name, grid=grid,
        in_specs=[a_spec(s[0], s[1], s[2]) for s in steps] + [b_spec(bi) for bi in range(nb)] + [o_spec] * na + [any_spec] * ne,
        out_specs=[o_spec] + [any_spec] * ne,
        out_shape=[jax.ShapeDtypeStruct((m_, n_), F32)] + (op.out_shape if op else []),
        scratch_shapes=[pltpu.VMEM((tm, tn), F32)] + (_sem_pair(op.n_sem) if op else []),
        compiler_params=_cp(("arbitrary", "arbitrary", "arbitrary")),
    )(*[a for a, _, _ in segs], *bs, *(() if acc is None else (acc,)), *(exchange or ()))
    return outs if op else outs[0]


def _dims(mode, ndim):
    if ndim == 2:
        return _DIMS[mode]
    (ca,), (cb,) = _DIMS[mode][0]
    return (((ca + 1,), (cb + 1,)), ((0,), (0,)))


def _bdot_plain(a, b, mode):
    return lax.dot_general(a.astype(BF16), b.astype(BF16), _dims(mode, a.ndim), preferred_element_type=F32)


@functools.partial(jax.custom_vjp, nondiff_argnums=(2,))
def _bdot_vjp(a, b, mode):
    return _bdot_plain(a, b, mode)


def _bdot_fwd(a, b, mode):
    return _bdot_plain(a, b, mode), (a, b)


def _bdot_bwd(mode, res, g):
    a, b = res
    if mode == "nn":
        return _bdot_plain(g, b, "nt"), _bdot_plain(a, g, "tn")
    if mode == "nt":
        return _bdot_plain(g, b, "nn"), _bdot_plain(g, a, "tn")
    return _bdot_plain(b, g, "nt"), _bdot_plain(a, g, "nn")


_bdot_vjp.defvjp(_bdot_fwd, _bdot_bwd)


def _split3(x):
    x1 = x.astype(BF16)
    r1 = x - x1.astype(F32)
    x2 = r1.astype(BF16)
    return x1, x2, (r1 - x2.astype(F32)).astype(BF16)


def _hdot_impl(a, b, mode, data):
    dims = _dims(mode, a.ndim)
    if data == "a":
        sel = b.astype(BF16)
        parts = [lax.dot_general(p, sel, dims, preferred_element_type=F32) for p in _split3(a)]
    else:
        sel = a.astype(BF16)
        parts = [lax.dot_general(sel, p, dims, preferred_element_type=F32) for p in _split3(b)]
    return (parts[2] + parts[1]) + parts[0]


@functools.partial(jax.custom_vjp, nondiff_argnums=(2, 3))
def _hdot(a, b, mode="nn", data="b"):
    return _hdot_impl(a, b, mode, data)


def _hdot_fwd(a, b, mode, data):
    return _hdot_impl(a, b, mode, data), (a, b)


def _hdot_bwd(mode, data, res, g):
    a, b = res
    if data == "a":
        da = {"nn": lambda: _hdot_impl(g, b, "nt", "a"), "nt": lambda: _hdot_impl(g, b, "nn", "a"),
              "tn": lambda: _hdot_impl(b, g, "nt", "b")}[mode]()
        return da, jnp.zeros_like(b)
    db = {"nn": lambda: _hdot_impl(a, g, "tn", "b"), "nt": lambda: _hdot_impl(g, a, "tn", "a"),
          "tn": lambda: _hdot_impl(a, g, "nn", "b")}[mode]()
    return jnp.zeros_like(a), db


_hdot.defvjp(_hdot_fwd, _hdot_bwd)


def _sigmoid(x):
    return lax.logistic(x)


def _silu(x):
    return x * _sigmoid(x)


def _iota(shape, dim):
    return lax.broadcasted_iota(jnp.int32, shape, dim)


def _rms(x, w):
    return x * lax.rsqrt(jnp.mean(x * x, axis=-1, keepdims=True) + EPS) * w


def _hg_chunk(q_raw, f_raw, v, g, st, t0, t1, nw, dot):
    nhd, c = q_raw.shape[0], q_raw.shape[1]
    m = jnp.maximum(t0, t1)
    e0, e1 = jnp.exp(t0 - m), jnp.exp(t1 - m)
    lb = e0 / (e0 + e1)
    f = lb + (1.0 - lb) * _sigmoid(f_raw)
    k = 1.0 - f
    lf = jnp.log(f)
    qh = _silu(q_raw) * (HG_DK ** -0.5)
    row, col = _iota((c, c), 0), _iota((c, c), 1)
    causal = col <= row
    tril = jnp.broadcast_to(jnp.where(causal, 1.0, 0.0).astype(F32), (nhd, c, c))
    trilb = jnp.broadcast_to(jnp.where(causal & (col // HG_SUB == row // HG_SUB), 1.0, 0.0).astype(F32), (nhd, c, c))
    b = _hdot(tril, lf)
    bl = _hdot(trilb, lf)
    a_row = b - bl
    rid = _iota((c, HG_DK), 0)
    qt = qh * jnp.exp(bl)
    kt = k * jnp.exp(-bl)
    scores = jnp.zeros((nhd, c, c), F32)
    for j in range(c // HG_SUB):
        if j == 0:
            qj = qt * jnp.exp(a_row)
        else:
            a_j = jnp.sum(jnp.where(rid == j * HG_SUB - 1, b, 0.0), axis=1, keepdims=True)
            qj = qt * jnp.exp(jnp.where(rid // HG_SUB >= j, a_row - a_j, -1e30))
        kj = jnp.where(rid // HG_SUB == j, kt, 0.0)
        scores = scores + dot(qj, kj, "nt")
    scores = jnp.where(causal, scores, 0.0)
    o = dot(scores, v, "nn") + dot(qh * jnp.exp(b), st, "nt")
    b_last = jnp.sum(jnp.where(rid == c - 1, b, 0.0), axis=1, keepdims=True)
    st_new = st * jnp.exp(b_last) + dot(v, k * jnp.exp(b_last - b), "tn")
    y = _rms(o, nw) * _silu(g)
    return y, st_new


HG_HPS = 16
HG_W = HG_HPS * HG_DK


def hgrn2_fwd(qfig, table, nw, *, step_chunks=2, gather=None):
    t = qfig.shape[0]
    rows = HG_CHUNK * step_chunks
    nsteps = t // rows
    nh = HG_HEADS // HG_HPS
    op = _Gather(*gather) if gather else None
    ng = len(gather[0]) if gather else 0

    def body(*refs):
        q_ref, f_ref, v_ref, g_ref, tab_ref, nw_ref = refs[:6]
        p_refs = refs[6:6 + ng]
        y_ref, s_ref = refs[6 + ng:8 + ng]
        got_refs = refs[8 + ng:8 + 2 * ng]
        st_scr = refs[8 + 2 * ng]
        sems = refs[9 + 2 * ng:]
        first_step = (pl.program_id(0) == 0) & (pl.program_id(1) == 0)
        last_step = (pl.program_id(0) == nh - 1) & (pl.program_id(1) == nsteps - 1)
        if op:
            @pl.when(first_step)
            def _():
                op.start(p_refs, got_refs, *sems)

        @pl.when(pl.program_id(1) == 0)
        def _():
            st_scr[...] = jnp.zeros_like(st_scr)

        nwv = nw_ref[...]
        lanes = [pl.ds(hh * HG_DK, HG_DK) for hh in range(HG_HPS)]
        t0 = jnp.stack([tab_ref[0:1, ln] for ln in lanes])
        t1 = jnp.stack([tab_ref[1:2, ln] for ln in lanes])
        for c in range(step_chunks):
            sl = pl.ds(c * HG_CHUNK, HG_CHUNK)
            heads = lambda ref: jnp.stack([ref[sl, ln] for ln in lanes])
            st = st_scr[...]
            for hh in range(HG_HPS):
                s_ref[hh, c] = st[hh]
            y, st_new = _hg_chunk(heads(q_ref), heads(f_ref), heads(v_ref), heads(g_ref), st, t0, t1, nwv, _bdot_vjp)
            for hh, ln in enumerate(lanes):
                y_ref[sl, ln] = y[hh].astype(BF16)
            st_scr[...] = st_new

        if op:
            @pl.when(last_step)
            def _():
                op.finish(p_refs, got_refs, *sems)

    blk = lambda off: pl.BlockSpec((rows, HG_W), lambda h, c, off=off: (c, off + h))
    return pl.pallas_call(
        body, name="hgrn2_fwd", grid=(nh, nsteps),
        in_specs=[blk(0), blk(nh), blk(2 * nh), blk(3 * nh),
                  pl.BlockSpec((2, HG_W), lambda h, c: (0, h)), pl.BlockSpec((1, HG_DK), lambda h, c: (0, 0))] + [ANY] * ng,
        out_specs=[pl.BlockSpec((rows, HG_W), lambda h, c: (c, h)),
                   pl.BlockSpec((HG_HPS, step_chunks, HG_DK, HG_DK), lambda h, c: (h, c, 0, 0))] + [ANY] * ng,
        out_shape=[jax.ShapeDtypeStruct((t, HG_HEADS * HG_DK), BF16),
                   jax.ShapeDtypeStruct((HG_HEADS, t // HG_CHUNK, HG_DK, HG_DK), F32)] + (op.out_shape if op else []),
        scratch_shapes=[pltpu.VMEM((HG_HPS, HG_DK, HG_DK), F32)] + (_sem_pair(op.n_sem) if op else []),
        compiler_params=_cp(("arbitrary", "arbitrary")),
    )(qfig, qfig, qfig, qfig, table, nw, *(gather[0] if gather else ()))


def hgrn2_bwd(qfig, table, nw, states, dy, *, step_chunks=2):
    t = qfig.shape[0]
    rows = HG_CHUNK * step_chunks
    nsteps = t // rows
    nh = HG_HEADS // HG_HPS
    assert nh == 1
    d = HG_HEADS * HG_DK

    def body(q_ref, f_ref, v_ref, g_ref, tab_ref, nw_ref, s_ref, dy_ref, dqfig_ref, dtab_ref, dnw_ref, dst_scr):
        @pl.when(pl.program_id(1) == 0)
        def _():
            dst_scr[...] = jnp.zeros_like(dst_scr)
            dtab_ref[...] = jnp.zeros_like(dtab_ref)
            dnw_ref[...] = jnp.zeros_like(dnw_ref)

        nwv = nw_ref[...]
        fn = functools.partial(_hg_chunk, dot=_bdot_vjp)
        lanes = [pl.ds(hh * HG_DK, HG_DK) for hh in range(HG_HPS)]
        t0 = jnp.stack([tab_ref[0:1, ln] for ln in lanes])
        t1 = jnp.stack([tab_ref[1:2, ln] for ln in lanes])
        for c in reversed(range(step_chunks)):
            sl = pl.ds(c * HG_CHUNK, HG_CHUNK)
            heads = lambda ref: jnp.stack([ref[sl, ln] for ln in lanes])
            _, vjp = jax.vjp(fn, heads(q_ref), heads(f_ref), heads(v_ref), heads(g_ref), s_ref[:, c], t0, t1, nwv)
            dq, df, dv, dg, dst, dt0, dt1, dnw = vjp((heads(dy_ref).astype(F32), dst_scr[...]))
            for hh, ln in enumerate(lanes):
                for seg, val in enumerate((dq, df, dv, dg)):
                    dqfig_ref[sl, pl.ds(seg * d + hh * HG_DK, HG_DK)] = val[hh].astype(BF16)
                dtab_ref[0:1, ln] += dt0[hh]
                dtab_ref[1:2, ln] += dt1[hh]
            dst_scr[...] = dst
            dnw_ref[0] += dnw

    rev = lambda c: nsteps - 1 - c
    blk = lambda off: pl.BlockSpec((rows, HG_W), lambda h, c, off=off: (rev(c), off + h))
    outs = pl.pallas_call(
        body, name="hgrn2_bwd", grid=(nh, nsteps),
        in_specs=[blk(0), blk(nh), blk(2 * nh), blk(3 * nh),
                  pl.BlockSpec((2, HG_W), lambda h, c: (0, h)), pl.BlockSpec((1, HG_DK), lambda h, c: (0, 0)),
                  pl.BlockSpec((HG_HPS, step_chunks, HG_DK, HG_DK), lambda h, c: (h, rev(c), 0, 0)),
                  pl.BlockSpec((rows, HG_W), lambda h, c: (rev(c), h))],
        out_specs=[pl.BlockSpec((rows, 4 * d), lambda h, c: (rev(c), 0)),
                   pl.BlockSpec((2, HG_W), lambda h, c: (0, h)),
                   pl.BlockSpec((HG_HPS, 1, HG_DK), lambda h, c: (h, 0, 0))],
        out_shape=[jax.ShapeDtypeStruct((t, 4 * d), BF16), jax.ShapeDtypeStruct((2, d), F32),
                   jax.ShapeDtypeStruct((HG_HEADS, 1, HG_DK), F32)],
        scratch_shapes=[pltpu.VMEM((HG_HPS, HG_DK, HG_DK), F32)],
        compiler_params=_cp(("parallel", "arbitrary")),
    )(qfig, qfig, qfig, qfig, table, nw, states, dy)
    return outs


def _ssd_chunk(xs2, dt, acum, bm, cm, s2, pair0, dot):
    npr, c = xs2.shape[0], xs2.shape[1]
    sh_e, sh_s = (npr, DT_PAD, 128), (npr, 8, DT_PAD)
    first_head = 2 * (pair0 + _iota(sh_e, 0))
    expand = jnp.where(_iota(sh_e, 1) == first_head + _iota(sh_e, 2) // SSM_HEADDIM, 1.0, 0.0).astype(F32)
    sel = (_iota(sh_s, 2) == 2 * (pair0 + _iota(sh_s, 0)) + _iota(sh_s, 1)) & (_iota(sh_s, 1) < 2)
    sel = jnp.where(sel, 1.0, 0.0).astype(F32)
    per_pair = lambda a: jnp.broadcast_to(a, (npr,) + a.shape)
    dtx = _hdot(per_pair(dt), expand, "nn", "a")
    acol = _hdot(per_pair(acum), expand, "nn", "a")
    arow8 = _hdot(sel, per_pair(acum), "nt", "b")
    row, col = _iota((c, c), 0), _iota((c, c), 1)
    causal = col <= row
    cb = dot(cm, bm, "nt")
    x2 = xs2 * dtx
    lane_c = _iota((c, 128), 1)
    y = dot(per_pair(cm), s2, "nn") * jnp.exp(acol)
    for r in range(2):
        head = (lane_c // SSM_HEADDIM) == r
        a_c = jnp.sum(jnp.where(head & (lane_c % SSM_HEADDIM == 0), acol, 0.0), axis=2, keepdims=True)
        a_r = jnp.sum(jnp.where(_iota((8, c), 0) == r, arow8, 0.0), axis=1, keepdims=True)
        decay = jnp.exp(jnp.where(causal, a_c - a_r, -1e30))
        y = y + dot(cb * decay, jnp.where(head, x2, 0.0), "nn")
    a_last = jnp.sum(jnp.where(_iota((c, 128), 0) == c - 1, acol, 0.0), axis=1, keepdims=True)
    s2_new = s2 * jnp.exp(a_last) + dot(per_pair(bm), x2 * jnp.exp(a_last - acol), "tn")
    return y, s2_new


SSM_PAIRS = SSM_HEADS // 2
PAIRS_PER_GROUP = SSM_PAIRS // SSM_GROUPS
SSD_PPS = 4
SSD_W = SSD_PPS * 128
_XS_BLOCKS = SSM_DINNER // 128
_B_BLOCK0 = _XS_BLOCKS
_C_BLOCK0 = _XS_BLOCKS + SSM_GROUPS


def ssd_fwd(xbc_act, dt, acum, *, gather=None):
    t = xbc_act.shape[0]
    nc = t // SSM_CHUNK
    c_ = SSM_CHUNK
    nq = SSM_PAIRS // SSD_PPS
    op = _Gather(*gather) if gather else None
    ng = len(gather[0]) if gather else 0

    def body(*refs):
        xs_ref, b_ref, c_ref, dt_ref, ac_ref = refs[:5]
        p_refs = refs[5:5 + ng]
        y_ref, s_ref = refs[5 + ng:7 + ng]
        got_refs = refs[7 + ng:7 + 2 * ng]
        s_scr = refs[7 + 2 * ng]
        sems = refs[8 + 2 * ng:]
        q = pl.program_id(1)
        if op:
            @pl.when((pl.program_id(0) == 0) & (q == 0))
            def _():
                op.start(p_refs, got_refs, *sems)

            @pl.when((pl.program_id(0) == nc - 1) & (q == nq - 1))
            def _():
                op.finish(p_refs, got_refs, *sems)

        mine = pl.ds(SSD_PPS * q, SSD_PPS)
        lanes = [pl.ds(r * 128, 128) for r in range(SSD_PPS)]

        @pl.when(pl.program_id(0) == 0)
        def _():
            s_scr[mine] = jnp.zeros((SSD_PPS, SSM_DSTATE, 128), F32)

        s2 = s_scr[mine]
        s_ref[...] = s2
        xs = jnp.stack([xs_ref[:, ln] for ln in lanes])
        y, s2_new = _ssd_chunk(xs, dt_ref[...], ac_ref[...], b_ref[...], c_ref[...], s2, SSD_PPS * q, _bdot_vjp)
        for r, ln in enumerate(lanes):
            y_ref[:, ln] = y[r]
        s_scr[mine] = s2_new

    grp = lambda q: q // (PAIRS_PER_GROUP // SSD_PPS)
    return pl.pallas_call(
        body, name="ssd_fwd", grid=(nc, SSM_PAIRS // SSD_PPS),
        in_specs=[pl.BlockSpec((c_, SSD_W), lambda c, q: (c, q)),
                  pl.BlockSpec((c_, 128), lambda c, q: (c, _B_BLOCK0 + grp(q))),
                  pl.BlockSpec((c_, 128), lambda c, q: (c, _C_BLOCK0 + grp(q))),
                  pl.BlockSpec((c_, DT_PAD), lambda c, q: (c, 0)),
                  pl.BlockSpec((c_, DT_PAD), lambda c, q: (c, 0))] + [ANY] * ng,
        out_specs=[pl.BlockSpec((c_, SSD_W), lambda c, q: (c, q)),
                   pl.BlockSpec((None, SSD_PPS, SSM_DSTATE, 128), lambda c, q: (c, q, 0, 0))] + [ANY] * ng,
        out_shape=[jax.ShapeDtypeStruct((t, SSM_DINNER), F32),
                   jax.ShapeDtypeStruct((nc, SSM_PAIRS, SSM_DSTATE, 128), F32)] + (op.out_shape if op else []),
        scratch_shapes=[pltpu.VMEM((SSM_PAIRS, SSM_DSTATE, 128), F32)] + (_sem_pair(op.n_sem) if op else []),
        compiler_params=_cp(("arbitrary", "arbitrary")),
    )(xbc_act, xbc_act, xbc_act, dt, acum, *(gather[0] if gather else ()))


def ssd_bwd(xbc_act, dt, acum, states, dy, dskip, *, exchange=None):
    t = xbc_act.shape[0]
    nc = t // SSM_CHUNK
    c_ = SSM_CHUNK
    rev = lambda c: nc - 1 - c
    nq = SSM_PAIRS // SSD_PPS
    op = _ChipExchange(exchange) if exchange else None
    ne = len(exchange) if exchange else 0

    def body(*refs):
        xs_ref, b_ref, c_ref, dt_ref, ac_ref, s_ref, dy_ref, sk_ref = refs[:8]
        ex_refs = refs[8:8 + ne]
        dxs_ref, db_ref, dc_ref, ddt_ref, dac_ref = refs[8 + ne:13 + ne]
        got_refs = refs[13 + ne:13 + 2 * ne]
        ds_scr = refs[13 + 2 * ne]
        sems = refs[14 + 2 * ne:]
        q = pl.program_id(1)
        if op:
            @pl.when((pl.program_id(0) == 0) & (q == 0))
            def _():
                op.start(ex_refs, got_refs, *sems)

            @pl.when((pl.program_id(0) == nc - 1) & (q == nq - 1))
            def _():
                op.finish(ex_refs, got_refs, *sems)

        assert SSD_PPS == PAIRS_PER_GROUP
        mine = pl.ds(SSD_PPS * q, SSD_PPS)
        lanes = [pl.ds(r * 128, 128) for r in range(SSD_PPS)]

        @pl.when(pl.program_id(0) == 0)
        def _():
            ds_scr[mine] = jnp.zeros((SSD_PPS, SSM_DSTATE, 128), F32)

        fn = functools.partial(_ssd_chunk, pair0=SSD_PPS * q, dot=_bdot_vjp)
        xs = jnp.stack([xs_ref[:, ln] for ln in lanes])
        dy = jnp.stack([dy_ref[:, ln] for ln in lanes])
        _, vjp = jax.vjp(fn, xs, dt_ref[...], ac_ref[...], b_ref[...], c_ref[...], s_ref[...])
        dxs, ddt, dac, db, dc, ds = vjp((dy, ds_scr[mine]))
        for r, ln in enumerate(lanes):
            dxs_ref[:, ln] = dxs[r] + sk_ref[:, ln]
        ds_scr[mine] = ds
        db_ref[...] = db
        dc_ref[...] = dc

        @pl.when(q == 0)
        def _():
            ddt_ref[...] = ddt
            dac_ref[...] = dac

        @pl.when(q != 0)
        def _():
            ddt_ref[...] += ddt
            dac_ref[...] += dac

    grp = lambda q: q // (PAIRS_PER_GROUP // SSD_PPS)
    return pl.pallas_call(
        body, name="ssd_bwd", grid=(nc, SSM_PAIRS // SSD_PPS),
        in_specs=[pl.BlockSpec((c_, SSD_W), lambda c, q: (rev(c), q)),
                  pl.BlockSpec((c_, 128), lambda c, q: (rev(c), _B_BLOCK0 + grp(q))),
                  pl.BlockSpec((c_, 128), lambda c, q: (rev(c), _C_BLOCK0 + grp(q))),
                  pl.BlockSpec((c_, DT_PAD), lambda c, q: (rev(c), 0)),
                  pl.BlockSpec((c_, DT_PAD), lambda c, q: (rev(c), 0)),
                  pl.BlockSpec((None, SSD_PPS, SSM_DSTATE, 128), lambda c, q: (rev(c), q, 0, 0)),
                  pl.BlockSpec((c_, SSD_W), lambda c, q: (rev(c), q)),
                  pl.BlockSpec((c_, SSD_W), lambda c, q: (rev(c), q))] + [ANY] * ne,
        out_specs=[pl.BlockSpec((c_, SSD_W), lambda c, q: (rev(c), q)),
                   pl.BlockSpec((c_, 128), lambda c, q: (rev(c), grp(q))),
                   pl.BlockSpec((c_, 128), lambda c, q: (rev(c), grp(q))),
                   pl.BlockSpec((c_, DT_PAD), lambda c, q: (rev(c), 0)),
                   pl.BlockSpec((c_, DT_PAD), lambda c, q: (rev(c), 0))] + [ANY] * ne,
        out_shape=[jax.ShapeDtypeStruct((t, SSM_DINNER), F32),
                   jax.ShapeDtypeStruct((t, SSM_GROUPS * SSM_DSTATE), F32),
                   jax.ShapeDtypeStruct((t, SSM_GROUPS * SSM_DSTATE), F32),
                   jax.ShapeDtypeStruct((t, DT_PAD), F32),
                   jax.ShapeDtypeStruct((t, DT_PAD), F32)] + (op.out_shape if op else []),
        scratch_shapes=[pltpu.VMEM((SSM_PAIRS, SSM_DSTATE, 128), F32)] + (_sem_pair(op.n_sem) if op else []),
        compiler_params=_cp(("arbitrary", "arbitrary")),
    )(xbc_act, xbc_act, xbc_act, dt, acum, states, dy, dskip, *(exchange or ()))


def rowwise(name, fn, row_ins, par_ins, row_outs, acc_outs, *, tt, ncb=1, rb=None):
    t = row_ins[0][0].shape[0]
    assert t % tt == 0
    n_ri, n_pi, n_ro, n_ao = len(row_ins), len(par_ins), len(row_outs), len(acc_outs)
    rb = tt if rb is None else rb
    assert tt % rb == 0

    def body(*refs):
        i = pl.program_id(1)
        ro_refs = refs[n_ri + n_pi:n_ri + n_pi + n_ro]
        ao_refs = refs[n_ri + n_pi + n_ro:]
        pars = [r[...] for r in refs[n_ri:n_ri + n_pi]]
        accs = None
        for blk in range(tt // rb):
            rows = pl.ds(blk * rb, rb)
            outs = fn(*[r[rows, :] for r in refs[:n_ri]], *pars)
            for r, v in zip(ro_refs, outs[:n_ro]):
                r[rows, :] = v.astype(r.dtype)
            accs = list(outs[n_ro:]) if accs is None else [a + v for a, v in zip(accs, outs[n_ro:])]
        for r, v in zip(ao_refs, accs):
            @pl.when(i == 0)
            def _(r=r, v=v):
                r[...] = v

            @pl.when(i > 0)
            def _(r=r, v=v):
                r[...] += v

    in_specs = [pl.BlockSpec((tt, bc), lambda j, i, off=off: (i, off + j)) for _, bc, off in row_ins]
    in_specs += [pl.BlockSpec((a.shape[0], bc), lambda j, i, off=off: (0, off + j)) for a, bc, off in par_ins]
    out_specs = [pl.BlockSpec((tt, bc), lambda j, i: (i, j)) for _, bc, _ in row_outs]
    out_specs += [pl.BlockSpec((r, bc), lambda j, i: (0, j)) for r, _, bc in acc_outs]
    out_shape = [jax.ShapeDtypeStruct((t, c), dt) for c, _, dt in row_outs]
    out_shape += [jax.ShapeDtypeStruct((r, c), F32) for r, c, _ in acc_outs]
    return pl.pallas_call(
        body, name=name, grid=(ncb, t // tt), in_specs=in_specs, out_specs=out_specs, out_shape=out_shape,
        compiler_params=_cp(("parallel", "arbitrary")),
    )(*[a for a, _, _ in row_ins], *[a for a, _, _ in par_ins])


def _colsum(v):
    return jnp.sum(v, axis=0, keepdims=True)


def _softplus(x):
    return jnp.maximum(x, 0.0) + jnp.log(1.0 + jnp.exp(-jnp.abs(x)))


def _gelu_tanh(x):
    return 0.5 * x * (1.0 + jnp.tanh(0.7978845608028654 * (x + 0.044715 * (x * x * x))))


D = D_MODEL
ROW_RB = 16


def norm_fwd(x, w):
    return rowwise("norm_fwd", lambda xv, wv: (_rms(xv, wv),), [(x, D, 0)], [(w, D, 0)], [(D, D, BF16)], [], tt=256, rb=ROW_RB)[0]


def norm_bwd(x, w, dh, dres):
    def fn(xv, dhv, drv, wv):
        _, vjp = jax.vjp(_rms, xv, wv)
        dx, dw = vjp(dhv)
        return dx + drv, dw
    return rowwise("norm_bwd", fn, [(x, D, 0), (dh, D, 0), (dres, D, 0)], [(w, D, 0)], [(D, D, F32)], [(1, D, D)], tt=256, rb=ROW_RB)


def _dt_fn(dtr, bias, a_log):
    c = dtr.shape[0]
    dt = _softplus(dtr + bias)
    da = dt * (-jnp.exp(a_log))
    tril = jnp.where(_iota((c, c), 1) <= _iota((c, c), 0), 1.0, 0.0).astype(F32)
    return dt, _hdot(tril, da)


def dt_fwd(dtr, bias, a_log):
    return rowwise("dt_fwd", _dt_fn, [(dtr, DT_PAD, 0)], [(bias, DT_PAD, 0), (a_log, DT_PAD, 0)],
                   [(DT_PAD, DT_PAD, F32), (DT_PAD, DT_PAD, F32)], [], tt=SSM_CHUNK)


def dt_bwd(dtr, bias, a_log, ddt, dacum):
    def fn(dtrv, ddtv, dacv, bv, av):
        _, vjp = jax.vjp(_dt_fn, dtrv, bv, av)
        return vjp((ddtv, dacv))
    return rowwise("dt_bwd", fn, [(dtr, DT_PAD, 0), (ddt, DT_PAD, 0), (dacum, DT_PAD, 0)],
                   [(bias, DT_PAD, 0), (a_log, DT_PAD, 0)],
                   [(DT_PAD, DT_PAD, BF16)], [(1, DT_PAD, DT_PAD), (1, DT_PAD, DT_PAD)], tt=SSM_CHUNK)


GROUP_W = SSM_DINNER // SSM_GROUPS


def _ssm_post_fn(yv, xsv, zv, dexp, nw):
    return _rms((yv + dexp * xsv) * _silu(zv), nw)


def ssm_post_fwd(yssd, xbc_act, z, dexp, nw):
    return rowwise("ssm_post_fwd", lambda *a: (_ssm_post_fn(*a),),
                   [(yssd, GROUP_W, 0), (xbc_act, GROUP_W, 0), (z, GROUP_W, 0)], [(dexp, GROUP_W, 0), (nw, GROUP_W, 0)],
                   [(SSM_DINNER, GROUP_W, BF16)], [], tt=512, ncb=SSM_GROUPS)[0]


def ssm_post_bwd(yssd, xbc_act, z, dexp, nw, dy):
    def fn(yv, xsv, zv, dyv, dv, nv):
        _, vjp = jax.vjp(_ssm_post_fn, yv, xsv, zv, dv, nv)
        return vjp(dyv.astype(F32))
    return rowwise("ssm_post_bwd", fn,
                   [(yssd, GROUP_W, 0), (xbc_act, GROUP_W, 0), (z, GROUP_W, 0), (dy, GROUP_W, 0)],
                   [(dexp, GROUP_W, 0), (nw, GROUP_W, 0)],
                   [(SSM_DINNER, GROUP_W, F32), (SSM_DINNER, GROUP_W, F32), (SSM_DINNER, GROUP_W, BF16)],
                   [(1, SSM_DINNER, GROUP_W), (1, SSM_DINNER, GROUP_W)], tt=512, ncb=SSM_GROUPS)


def _merge_fn(ah, asm, gh, gs):
    return _sigmoid(gh) * ah + _sigmoid(gs) * asm


def merge_fwd(a_hg, a_ssm, gates):
    f32 = lambda vals: [v.astype(F32) for v in vals]
    return rowwise("merge_fwd", lambda *a: (_merge_fn(*f32(a)),), [(a_hg, D, 0), (a_ssm, D, 0), (gates, D, 0), (gates, D, 1)], [],
                   [(D, D, BF16)], [], tt=256, rb=ROW_RB)[0]


def merge_bwd(a_hg, a_ssm, gates, dmixed):
    def fn(ah, asm, gh, gs, dm):
        _, vjp = jax.vjp(_merge_fn, *[v.astype(F32) for v in (ah, asm, gh, gs)])
        return vjp(dm.astype(F32))
    return rowwise("merge_bwd", fn, [(a_hg, D, 0), (a_ssm, D, 0), (gates, D, 0), (gates, D, 1), (dmixed, D, 0)], [],
                   [(D, D, BF16)] * 4, [], tt=256, rb=ROW_RB)


def _post1_fn(xv, uv, wpost, wpre):
    x1 = xv + _rms(uv, wpost)
    return x1, _rms(x1, wpre)


def post1_fwd(x, u, wpost, wpre):
    return rowwise("post1_fwd", _post1_fn, [(x, D, 0), (u, D, 0)], [(wpost, D, 0), (wpre, D, 0)],
                   [(D, D, F32), (D, D, BF16)], [], tt=256, rb=ROW_RB)


def post1_bwd(x, u, wpost, wpre, dx1, dh2):
    def fn(xv, uv, d1, d2, wa, wb):
        _, vjp = jax.vjp(_post1_fn, xv, uv, wa, wb)
        dx, du, dwa, dwb = vjp((d1, d2))
        return du, dx, dwa, dwb
    return rowwise("post1_bwd", fn, [(x, D, 0), (u, D, 0), (dx1, D, 0), (dh2, D, 0)], [(wpost, D, 0), (wpre, D, 0)],
                   [(D, D, BF16), (D, D, F32)], [(1, D, D), (1, D, D)], tt=256, rb=ROW_RB)


def final_fwd_bwd(x1, fo, w, target):
    def fn(x1v, fov, tv, wv):
        def loss_fn(a, b, c):
            err = a + _rms(b, c) - tv
            return 0.5 * jnp.sum(err * err) * (1.0 / D)
        loss, vjp = jax.vjp(loss_fn, x1v, fov, wv)
        dx, dfo, dw = vjp(jnp.ones((), F32))
        return dx, dfo, dw, jnp.full((1, 128), loss, F32)
    return rowwise("final_fwd_bwd", fn, [(x1, D, 0), (fo, D, 0), (target, D, 0)], [(w, D, 0)],
                   [(D, D, F32), (D, D, BF16)], [(1, D, D), (1, 128, 128)], tt=256, rb=ROW_RB)


HALO = 8
CONV_TT = 512
CONV_CB = 512
CONV_RB = 32


def _tail(kind, c, up):
    return _silu(c) if kind == "silu" else _gelu_tanh(c) * up


def conv_fwd(name, x, xoff, w, b, kind, up=None, upoff=0, act_dtype=F32):
    t = x.shape[0]
    k_, c_ = w.shape
    tt, cb = CONV_TT, CONV_CB
    hb = tt // HALO
    has_up = up is not None

    def body(*refs):
        if has_up:
            x_ref, xp_ref, w_ref, b_ref, up_ref, c_ref, a_ref, scr = refs
        else:
            x_ref, xp_ref, w_ref, b_ref, c_ref, a_ref, scr = refs
        i = pl.program_id(1)
        scr[0:HALO, :] = jnp.where(i == 0, 0.0, xp_ref[...])
        scr[HALO:HALO + tt, :] = x_ref[...]
        for r in range(tt // CONV_RB):
            rows = pl.ds(r * CONV_RB, CONV_RB)
            acc = jnp.zeros((CONV_RB, cb), F32) + b_ref[...]
            for k in range(k_):
                acc = acc + w_ref[k:k + 1, :] * scr[pl.ds(r * CONV_RB + HALO - (k_ - 1) + k, CONV_RB), :]
            c_ref[rows, :] = acc
            a_ref[rows, :] = _tail(kind, acc, up_ref[rows, :] if has_up else None).astype(act_dtype)

    in_specs = [pl.BlockSpec((tt, cb), lambda j, i: (i, xoff + j)),
                pl.BlockSpec((HALO, cb), lambda j, i: (jnp.maximum(i * hb - 1, 0), xoff + j)),
                pl.BlockSpec((k_, cb), lambda j, i: (0, j)),
                pl.BlockSpec((1, cb), lambda j, i: (0, j))]
    args = [x, x, w, b]
    if has_up:
        in_specs.append(pl.BlockSpec((tt, cb), lambda j, i: (i, upoff + j)))
        args.append(up)
    return pl.pallas_call(
        body, name=name, grid=(c_ // cb, t // tt), in_specs=in_specs,
        out_specs=[pl.BlockSpec((tt, cb), lambda j, i: (i, j))] * 2,
        out_shape=[jax.ShapeDtypeStruct((t, c_), F32), jax.ShapeDtypeStruct((t, c_), act_dtype)],
        scratch_shapes=[pltpu.VMEM((tt + HALO, cb), F32)],
        compiler_params=_cp(("parallel", "arbitrary")),
    )(*args)


def conv_bwd(name, x, xoff, c, coff, dact, w, kind, up=None, upoff=0):
    t = x.shape[0]
    k_, c_ = w.shape[0], dact.shape[1]
    tt, cb = CONV_TT, CONV_CB
    hb = tt // HALO
    nt = t // tt
    has_up = up is not None

    def tail_grad(cv, dav, upv):
        if has_up:
            _, vjp = jax.vjp(lambda a, u: _tail(kind, a, u), cv, upv)
            return vjp(dav)
        _, vjp = jax.vjp(lambda a: _tail(kind, a, None), cv)
        return vjp(dav)[0], None

    def body(*refs):
        if has_up:
            (x_ref, xp_ref, c_ref, cn_ref, da_ref, dan_ref, w_ref, up_ref, upn_ref,
             dx_ref, dup_ref, dw_ref, db_ref, xs, dcs) = refs
        else:
            x_ref, xp_ref, c_ref, cn_ref, da_ref, dan_ref, w_ref, dx_ref, dw_ref, db_ref, xs, dcs = refs
        i = pl.program_id(1)
        xs[0:HALO, :] = jnp.where(i == 0, 0.0, xp_ref[...])
        xs[HALO:HALO + tt, :] = x_ref[...]
        rb = CONV_RB
        for r in range(tt // rb):
            rows = pl.ds(r * rb, rb)
            dc, dup = tail_grad(c_ref[rows, :], da_ref[rows, :].astype(F32), up_ref[rows, :] if has_up else None)
            dcs[rows, :] = dc
            if has_up:
                dup_ref[rows, :] = dup.astype(BF16)
        dcn, _ = tail_grad(cn_ref[...], dan_ref[...].astype(F32), upn_ref[...] if has_up else None)
        dcs[tt:tt + HALO, :] = jnp.where(i == nt - 1, 0.0, dcn)
        dws = [jnp.zeros((1, cb), F32) for _ in range(k_)]
        dbv = jnp.zeros((1, cb), F32)
        for r in range(tt // rb):
            rows = pl.ds(r * rb, rb)
            dc = dcs[rows, :]
            dx = jnp.zeros((rb, cb), F32)
            for k in range(k_):
                dx = dx + w_ref[k:k + 1, :] * dcs[pl.ds(r * rb + k_ - 1 - k, rb), :]
                dws[k] = dws[k] + _colsum(dc * xs[pl.ds(r * rb + HALO - (k_ - 1) + k, rb), :])
            dbv = dbv + _colsum(dc)
            dx_ref[rows, :] = dx.astype(BF16)

        @pl.when(i == 0)
        def _():
            dw_ref[...] = jnp.zeros_like(dw_ref)
            db_ref[...] = jnp.zeros_like(db_ref)

        for k in range(k_):
            dw_ref[k:k + 1, :] += dws[k]
        db_ref[...] += dbv

    tile = lambda off: pl.BlockSpec((tt, cb), lambda j, i, off=off: (i, off + j))
    prev = lambda off: pl.BlockSpec((HALO, cb), lambda j, i, off=off: (jnp.maximum(i * hb - 1, 0), off + j))
    nxt = lambda off: pl.BlockSpec((HALO, cb), lambda j, i, off=off: (jnp.minimum((i + 1) * hb, t // HALO - 1), off + j))
    in_specs = [tile(xoff), prev(xoff), tile(coff), nxt(coff), tile(0), nxt(0),
                pl.BlockSpec((k_, cb), lambda j, i: (0, coff + j))]
    args = [x, x, c, c, dact, dact, w]
    if has_up:
        in_specs += [tile(upoff), nxt(upoff)]
        args += [up, up]
    out_specs = [tile(0)] + ([tile(0)] if has_up else []) + [pl.BlockSpec((HALO, cb), lambda j, i: (0, j)),
                                                            pl.BlockSpec((1, cb), lambda j, i: (0, j))]
    out_shape = [jax.ShapeDtypeStruct((t, c_), BF16)] * (2 if has_up else 1)
    out_shape += [jax.ShapeDtypeStruct((HALO, c_), F32), jax.ShapeDtypeStruct((1, c_), F32)]
    return pl.pallas_call(
        body, name=name, grid=(c_ // cb, nt), in_specs=in_specs, out_specs=out_specs, out_shape=out_shape,
        scratch_shapes=[pltpu.VMEM((tt + HALO, cb), F32), pltpu.VMEM((tt + HALO, cb), F32)],
        compiler_params=_cp(("parallel", "arbitrary")),
    )(*args)


def ew_sum(name, parts, rows, out_dtype, tr):
    c = parts[0][0].shape[1]
    tr = min(tr, rows)
    assert rows % tr == 0 and all(off % tr == 0 for _, off in parts)
    n = len(parts)

    def body(*refs):
        acc = refs[0][...].astype(F32)
        for ref in refs[1:n]:
            acc = acc + ref[...].astype(F32)
        refs[n][...] = acc.astype(out_dtype)

    in_specs = [pl.BlockSpec((tr, c), lambda i, o=off // tr: (i + o, 0)) for _, off in parts]
    return pl.pallas_call(body, name=name, grid=(rows // tr,), in_specs=in_specs,
                          out_specs=pl.BlockSpec((tr, c), lambda i: (i, 0)),
                          out_shape=jax.ShapeDtypeStruct((rows, c), out_dtype),
                          compiler_params=_cp(("parallel",)))(*[a for a, _ in parts])


def fold_heads(dexp):
    def body(d_ref, o_ref):
        sel = jnp.where(_iota((SSM_DINNER, DT_PAD), 0) // SSM_HEADDIM == _iota((SSM_DINNER, DT_PAD), 1), 1.0, 0.0)
        o_ref[...] = _hdot(jnp.broadcast_to(d_ref[...], (8, SSM_DINNER)), sel.astype(F32), "nn", "a")[0:1, :]

    return pl.pallas_call(body, name="fold_heads", out_shape=jax.ShapeDtypeStruct((1, DT_PAD), F32),
                          compiler_params=pltpu.CompilerParams(vmem_limit_bytes=VMEM_LIMIT))(dexp)


def adamw(name, w, g, m, v, tr):
    r, c = w.shape
    tr = min(tr, r)
    assert r % tr == 0, (r, tr)

    def body(w_ref, g_ref, m_ref, v_ref, d_ref, nm_ref, nv_ref):
        gv = g_ref[...]
        nm = ADAM_B1 * m_ref[...] + (1.0 - ADAM_B1) * gv
        nv = ADAM_B2 * v_ref[...] + (1.0 - ADAM_B2) * (gv * gv)
        m_hat = nm / (1.0 - ADAM_B1 ** ADAM_STEP)
        v_hat = nv / (1.0 - ADAM_B2 ** ADAM_STEP)
        d_ref[...] = -ADAM_LR * (m_hat / (jnp.sqrt(v_hat) + ADAM_EPS) + ADAM_WD * w_ref[...])
        nm_ref[...] = nm
        nv_ref[...] = nv

    spec = pl.BlockSpec((tr, c), lambda i: (i, 0))
    shp = jax.ShapeDtypeStruct((r, c), F32)
    return pl.pallas_call(body, name=name, grid=(r // tr,), in_specs=[spec] * 4, out_specs=[spec] * 3,
                          out_shape=[shp] * 3, compiler_params=_cp(("parallel",)))(w, g, m, v)


SEG_QFIG, SEG_Z, SEG_XBC, SEG_DT, SEG_G = 0, 8192, 12288, 18432, 18496
IN_TOTAL = 22592
FFN_BLOCKS = D_FF // CONV_CB


def _own_slot(gathered, own, shard):
    slot = lax.broadcasted_iota(jnp.int32, (gathered.shape[0],) + (1,) * own.ndim, 0)
    return jnp.where(slot == shard, own[None], gathered)


def local_step(x, target, wts, par, p_rest, p_up, shard, core):
    t = x.shape[0]
    pad64 = lambda a: jnp.pad(a, ((0, 0), (0, DT_PAD - a.shape[1])))
    bias, a_log = pad64(par["ssm_dt_bias"]), pad64(par["ssm_A_log"])
    dexp = jnp.repeat(par["ssm_D"], SSM_HEADDIM, axis=1)
    in_t = wts["in_t"]

    h = norm_fwd(x, par["mix_pre_norm"])
    proj = lambda nm, off, n, tn: mm(h, in_t, "nt", name=nm, tn=tn, dims=(t, n, D), b_off=(off, 0))
    qfig = proj("proj_qfig", SEG_QFIG, 8192, 1024)
    z = proj("proj_z", SEG_Z, 4096, 1024)
    xbc = proj("proj_xbc", SEG_XBC, 6144, 1024)
    dtr = mm(h, wts["dt_t"], "nt", name="proj_dt", tn=128)
    gates = mm(h, wts["g_t"], "nt", name="proj_gates", out_dtype=BF16, tn=1024)
    y_hg, hg_states, g_rest = hgrn2_fwd(qfig, par["hg_lb_table"], par["hg_out_norm"], gather=([p_rest], [REST_PIECES]))
    c_ssm, xbc_act = conv_fwd("ssm_conv_fwd", xbc, 0, par["ssm_conv_w"], par["ssm_conv_b"], "silu")
    dt, acum = dt_fwd(dtr, bias, a_log)
    yssd, ssd_states, g_up = ssd_fwd(xbc_act, dt, acum, gather=([p_up], [UP_PIECES]))
    g_rest, g_up = _own_slot(g_rest, p_rest, shard), _own_slot(g_up, p_up, shard)
    r0, r1, r2, r3 = REST_SPLITS
    wts = dict(wts, bh=g_rest[:, :r0].reshape(-1, D), bs=g_rest[:, r0:r1].reshape(-1, D), o=g_rest[:, r1:r2].reshape(-1, D),
               dn=g_rest[:, r2:r3].reshape(-1, D), up=jnp.transpose(g_up, (1, 0, 2)).reshape(D, 2 * D_FF))
    y_ssm = ssm_post_fwd(yssd, xbc_act, z, dexp, par["ssm_out_norm"])
    a_hg = mm(y_hg, wts["bh"], "nn", name="branch_hg", out_dtype=BF16, tn=1024)
    a_ssm = mm(y_ssm, wts["bs"], "nn", name="branch_ssm", out_dtype=BF16, tn=1024)
    mixed = merge_fwd(a_hg, a_ssm, gates)
    u = mm(mixed, wts["o"], "nn", name="out_proj", tn=1024)
    x1, h2 = post1_fwd(x, u, par["mix_post_norm"], par["ffn_pre_norm"])
    gu = mm(h2, wts["up"], "nn", name="ffn_up", tn=1024)
    c_ffn, act = conv_fwd("ffn_conv_fwd", gu, 0, par["ffn_conv_w"], par["ffn_conv_b"], "gelu_mul",
                          up=gu, upoff=FFN_BLOCKS, act_dtype=BF16)
    fo = mm(act, wts["dn"], "nn", name="ffn_down", tm=512, tn=1024)
    dx2, dfo, g_ffn_post, loss = final_fwd_bwd(x1, fo, par["ffn_post_norm"], target)

    dact = mm(dfo, wts["dn"], "nt", name="d_act", out_dtype=BF16, tn=1408)
    g_dn = mm(act, dfo, "tn", name="g_ffn_down", out_dtype=BF16, tm=1408, tn=1024, tk=2048)
    dgate, dup, g_fcw, g_fcb = conv_bwd("ffn_conv_bwd", gu, 0, c_ffn, 0, dact, par["ffn_conv_w"], "gelu_mul",
                                        up=gu, upoff=FFN_BLOCKS)
    dh2 = mm_segments("d_h2", [(dgate, 0, 0), (dup, 0, D_FF)], [wts["up"]], tm=1024, tn=1024, tk=1408, bt=True)
    g_up_gate = mm(h2, dgate, "tn", name="g_ffn_up_gate", out_dtype=BF16, tm=1024, tn=1408, tk=2048)
    g_up_up = mm(h2, dup, "tn", name="g_ffn_up_up", out_dtype=BF16, tm=1024, tn=1408, tk=2048)
    du, dx1, g_mix_post, g_ffn_pre = post1_bwd(x, u, par["mix_post_norm"], par["ffn_pre_norm"], dx2, dh2)
    dmixed = mm(du, wts["o"], "nt", name="d_mixed", tn=1024)
    g_o = mm(mixed, du, "tn", name="g_w_out", out_dtype=BF16, tm=1024, tn=2048, tk=2048)
    da_hg, da_ssm, dg_hg, dg_ssm = merge_bwd(a_hg, a_ssm, gates, dmixed)
    dy_hg = mm(da_hg, wts["bh"], "nt", name="d_y_hg", out_dtype=BF16, tn=1024)
    g_bh = mm(y_hg, da_hg, "tn", name="g_w_branch_hg", out_dtype=BF16, tm=1024, tn=2048, tk=2048)
    dy_ssm = mm(da_ssm, wts["bs"], "nt", name="d_y_ssm", out_dtype=BF16, tn=1024)
    g_bs = mm(y_ssm, da_ssm, "tn", name="g_w_branch_ssm", out_dtype=BF16, tm=1024, tn=2048, tk=2048)
    dyssd, dskip, dz, g_dexp, g_ssm_norm = ssm_post_bwd(yssd, xbc_act, z, dexp, par["ssm_out_norm"], dy_ssm)

    gg_rest = jnp.concatenate([g.reshape(N_CHIPS, -1, D) for g in (g_bh, g_bs, g_o, g_dn)], axis=1)
    gg_up = jnp.transpose(jnp.concatenate([g_up_gate, g_up_up], axis=1).reshape(D, N_CHIPS, UP_COLS), (1, 0, 2))
    c_rest, c_up = pair_reduce("rest", [gg_rest, gg_up], [REST_PIECES, UP_PIECES], [432, 512], core)
    dxs, db_, dc_, ddt, dacum, rb_rest, rb_up = ssd_bwd(xbc_act, dt, acum, ssd_states, dyssd, dskip, exchange=[c_rest, c_up])
    red_rest, red_up = chip_reduce("rest", [c_rest, c_up], [rb_rest, rb_up], [432, 256], shard)
    ddtr, g_dt_bias, g_a_log = dt_bwd(dtr, bias, a_log, ddt, dacum)
    xs_blocks, bc_blocks = SSM_DINNER // CONV_CB, SSM_GROUPS * SSM_DSTATE // CONV_CB
    dxbc_x, g_cw_x, g_cb_x = conv_bwd("ssm_conv_bwd_x", xbc, 0, c_ssm, 0, dxs, par["ssm_conv_w"], "silu")
    dxbc_b, g_cw_b, g_cb_b = conv_bwd("ssm_conv_bwd_b", xbc, xs_blocks, c_ssm, xs_blocks, db_, par["ssm_conv_w"], "silu")
    dxbc_c, g_cw_c, g_cb_c = conv_bwd("ssm_conv_bwd_c", xbc, xs_blocks + bc_blocks, c_ssm, xs_blocks + bc_blocks, dc_,
                                      par["ssm_conv_w"], "silu")
    dqfig, g_table, g_hg_norm = hgrn2_bwd(qfig, par["hg_lb_table"], par["hg_out_norm"], hg_states, dy_hg)

    dsegs = [(dqfig, SEG_QFIG), (dz, SEG_Z),
             (dxbc_x, SEG_XBC), (dxbc_b, SEG_XBC + SSM_DINNER), (dxbc_c, SEG_XBC + SSM_DINNER + 1024)]
    g_in_parts = [mm(dseg, h, "tn", name=f"g_w_in_{n}", out_dtype=BF16, tm=1024, tn=2048, tk=2048)
                  for n, (dseg, _) in enumerate(dsegs)]
    g_dt_t = mm(ddtr, h, "tn", name="g_w_in_dt", out_dtype=BF16, tm=128, tn=2048, tk=1024)[:SSM_HEADS]
    g_in_parts += [g_dt_t] + [mm(dgate_, h, "tn", name=f"g_w_in_g{n}", out_dtype=BF16, tm=1024, tn=2048, tk=2048)
                              for n, dgate_ in enumerate((dg_hg, dg_ssm))]
    zpad = jnp.zeros((N_CHIPS, IN_ROWS - IN_SHARD, D), BF16)
    g_in_t = jnp.concatenate(g_in_parts, axis=0).reshape(N_CHIPS, IN_SHARD, D)
    (c_in,) = pair_reduce("in", [jnp.concatenate([g_in_t, zpad], axis=1)], [IN_PIECES], [960], core)
    dh = mm_segments("d_h_a", [(dseg, 0, off) for dseg, off in dsegs[:2]], [in_t], tm=1024, tn=1024, tk=2048)
    dh, rb_in = mm_segments("d_h_b", [(dseg, 0, off) for dseg, off in dsegs[2:]] + [(ddtr, 1, 0), (dg_hg, 2, 0), (dg_ssm, 2, D)],
                            [in_t, wts["dt_t"], wts["g_t"]], tm=1024, tn=1024, tk=1024, acc=dh, exchange=[c_in])
    (red_in,) = chip_reduce("in", [c_in], [rb_in], [480], shard)
    grad_x, g_mix_pre = norm_bwd(x, par["mix_pre_norm"], dh, dx1)

    big = dict(in_t=red_in, rest=red_rest, up=red_up)
    g_conv_w = jnp.concatenate([g_cw_x, g_cw_b, g_cw_c], axis=1)[:SSM_CONV]
    g_conv_b = jnp.concatenate([g_cb_x, g_cb_b, g_cb_c], axis=1)
    small = dict(mix_pre_norm=g_mix_pre, mix_post_norm=g_mix_post, hg_lb_table=g_table, hg_out_norm=g_hg_norm,
                 ssm_conv_w=g_conv_w, ssm_conv_b=g_conv_b, ssm_dt_bias=g_dt_bias, ssm_A_log=g_a_log,
                 ssm_D=g_dexp, ssm_out_norm=g_ssm_norm, ffn_pre_norm=g_ffn_pre, ffn_post_norm=g_ffn_post,
                 ffn_conv_w=g_fcw[:FFN_CONV], ffn_conv_b=g_fcb)
    return loss, grad_x, big, small


MESH = pl.DeviceIdType.MESH
ANY = pl.BlockSpec(memory_space=pl.ANY)
N_CHIPS = 4
IN_SHARD = 5648
IN_ROWS = 5760
REST_SPLITS = (512, 1536, 2048, 3456)
UP_COLS = 2816
IN_PIECES, REST_PIECES, UP_PIECES = 3, 4, 4


def _place():
    x, y, c = lax.axis_index("x"), lax.axis_index("y"), lax.axis_index("c")
    chips = [(1 - x, y), (x, 1 - y), (1 - x, 1 - y)]
    return x, y, c, chips


def _rcopy(src, dst, send_sems, recv_sems, k, dev):
    return pltpu.make_async_remote_copy(src_ref=src, dst_ref=dst, send_sem=send_sems.at[k], recv_sem=recv_sems.at[k],
                                        device_id=dev, device_id_type=MESH)


def _pieces(rows, n):
    assert rows % n == 0 and (rows // n) % 16 == 0, (rows, n)
    return [(k * (rows // n), rows // n) for k in range(n)]


def _rows(c, hrows, piece):
    return pl.ds(pl.multiple_of(c * hrows + piece[0], 16), piece[1])


def _half_plan(arrays, pieces):
    return [(a.shape[-2] // 2, _pieces(a.shape[-2] // 2, n)) for a, n in zip(arrays, pieces)]


def _sem_pair(n):
    return [pltpu.SemaphoreType.DMA((n,)), pltpu.SemaphoreType.DMA((n,))]


class _Gather:
    def __init__(self, ps, pieces, fill_own=False):
        self.plan = _half_plan(ps, pieces)
        self.n_ring = sum(2 * 3 * len(pcs) for _, pcs in self.plan)
        self.fill_own = fill_own
        self.n_sem = self.n_ring + (sum(2 * len(pcs) for _, pcs in self.plan) if fill_own else 0)
        self.out_shape = [jax.ShapeDtypeStruct((N_CHIPS,) + p.shape, p.dtype) for p in ps]

    def _own(self, p_refs, g_refs, send_sems, recv_sems):
        x, y, c, _ = _place()
        cps = []
        for p, g, (hrows, pcs) in zip(p_refs, g_refs, self.plan):
            for half in range(2):
                for pc in pcs:
                    rows = pl.ds(half * hrows + pc[0], pc[1])
                    cps.append(_rcopy(p.at[rows], g.at[2 * x + y, rows], send_sems, recv_sems, self.n_ring + len(cps), (x, y, 1 - c)))
        return cps

    def _copies(self, p_refs, g_refs, send_sems, recv_sems, only_first=False):
        x, y, c, chips = _place()
        own = 2 * x + y
        sib = (x, y, 1 - c)
        first, arrive, passed, from_sib = [], [], [], []
        k = 0
        for p, g, (hrows, pcs) in zip(p_refs, g_refs, self.plan):
            for chip in chips:
                theirs = 2 * chip[0] + chip[1]
                for pc in pcs:
                    mine, other = _rows(c, hrows, pc), _rows(1 - c, hrows, pc)
                    first.append(_rcopy(p.at[mine], g.at[own, mine], send_sems, recv_sems, k, (*chip, c)))
                    if not only_first:
                        arrive.append(_rcopy(g.at[theirs, mine], g.at[theirs, mine], send_sems, recv_sems, k, (*chip, c)))
                        passed.append(_rcopy(g.at[theirs, mine], g.at[theirs, mine], send_sems, recv_sems, k + 1, sib))
                        from_sib.append(_rcopy(g.at[theirs, other], g.at[theirs, other], send_sems, recv_sems, k + 1, sib))
                    k += 2
        return first, arrive, passed, from_sib

    def start(self, p_refs, g_refs, send_sems, recv_sems):
        for cp in self._copies(p_refs, g_refs, send_sems, recv_sems, only_first=True)[0]:
            cp.start()
        if self.fill_own:
            for cp in self._own(p_refs, g_refs, send_sems, recv_sems):
                cp.start()

    def finish(self, p_refs, g_refs, send_sems, recv_sems):
        first, arrive, passed, from_sib = self._copies(p_refs, g_refs, send_sems, recv_sems)
        for got, fw in zip(arrive, passed):
            got.wait_recv()
            fw.start()
        for cp in from_sib:
            cp.wait_recv()
        for cp in first + passed:
            cp.wait_send()
        if self.fill_own:
            for cp in self._own(p_refs, g_refs, send_sems, recv_sems):
                cp.wait()


def gather_weights(name, ps, pieces):
    op = _Gather(ps, pieces, fill_own=True)
    n = len(ps)

    def body(*refs):
        p_refs, g_refs, sems = refs[:n], refs[n:2 * n], refs[2 * n:]
        op.start(p_refs, g_refs, *sems)
        op.finish(p_refs, g_refs, *sems)

    return pl.pallas_call(body, name=name, in_specs=[ANY] * n, out_specs=[ANY] * n, out_shape=op.out_shape,
                          scratch_shapes=_sem_pair(op.n_sem))(*ps)


def pair_exchange(name, gs, pieces):
    plan = _half_plan(gs, pieces)
    n_sem = sum(N_CHIPS * len(pcs) for _, pcs in plan)
    n = len(gs)

    def body(*refs):
        g_refs, r_refs, send_sems, recv_sems = refs[:n], refs[n:2 * n], refs[2 * n], refs[2 * n + 1]
        x, y, c, _ = _place()
        sib = (x, y, 1 - c)
        cps = []
        for g, r, (hrows, pcs) in zip(g_refs, r_refs, plan):
            for s in range(N_CHIPS):
                for pc in pcs:
                    cps.append(_rcopy(g.at[s, _rows(1 - c, hrows, pc)], r.at[s, pl.ds(pc[0], pc[1])],
                                      send_sems, recv_sems, len(cps), sib))
        for cp in cps:
            cp.start()
        for cp in cps:
            cp.wait()

    return pl.pallas_call(
        body, name=name, in_specs=[ANY] * n, out_specs=[ANY] * n,
        out_shape=[jax.ShapeDtypeStruct((N_CHIPS, g.shape[1] // 2, g.shape[2]), g.dtype) for g in gs],
        scratch_shapes=_sem_pair(n_sem))(*gs)


class _ChipExchange:
    def __init__(self, ss):
        self.n_sem = 3 * len(ss)
        self.out_shape = [jax.ShapeDtypeStruct((3,) + s.shape[1:], s.dtype) for s in ss]

    def _copies(self, s_refs, r_refs, send_sems, recv_sems):
        x, y, c, chips = _place()
        cps = []
        for s, r in zip(s_refs, r_refs):
            for j, chip in enumerate(chips):
                cps.append(_rcopy(s.at[2 * chip[0] + chip[1]], r.at[j], send_sems, recv_sems, len(cps), (*chip, c)))
        return cps

    def start(self, *refs):
        for cp in self._copies(*refs):
            cp.start()

    def finish(self, *refs):
        for cp in self._copies(*refs):
            cp.wait()


def pair_assemble(name, rs, pieces):
    plan = [(r.shape[0], _pieces(r.shape[0], n_)) for r, n_ in zip(rs, pieces)]
    n_sem = sum(len(pcs) for _, pcs in plan)
    n = len(rs)

    def body(*refs):
        r_refs, f_refs, send_sems, recv_sems = refs[:n], refs[n:2 * n], refs[2 * n], refs[2 * n + 1]
        x, y, c, _ = _place()
        sib = (x, y, 1 - c)
        cps, got = [], []
        for r, f, (hrows, pcs) in zip(r_refs, f_refs, plan):
            for pc in pcs:
                src = r.at[pl.ds(pc[0], pc[1])]
                cps.append(_rcopy(src, f.at[_rows(c, hrows, pc)], send_sems, recv_sems, len(cps), sib))
                got.append(_rcopy(src, f.at[_rows(1 - c, hrows, pc)], send_sems, recv_sems, len(got), sib))
        for cp in cps:
            cp.start()
        for cp in got:
            cp.wait_recv()
        for cp in cps:
            cp.wait_send()

    return pl.pallas_call(
        body, name=name, in_specs=[ANY] * n, out_specs=[ANY] * n,
        out_shape=[jax.ShapeDtypeStruct((2 * r.shape[0], r.shape[1]), r.dtype) for r in rs],
        scratch_shapes=_sem_pair(n_sem))(*rs)


def pair_reduce(tag, ggs, pieces, trs, core):
    recv = pair_exchange("pair_exchange_" + tag, ggs, pieces)
    flat = lambda a: a.reshape(-1, a.shape[-1])
    out = []
    for n, (gg, r, tr) in enumerate(zip(ggs, recv, trs)):
        h = gg.shape[1] // 2
        own = lax.dynamic_slice_in_dim(gg, core * h, h, axis=1)
        out.append(ew_sum(f"pair_sum_{tag}_{n}", [(flat(own), 0), (flat(r), 0)], N_CHIPS * h, BF16, tr).reshape(r.shape))
    return out


def chip_reduce(tag, cs, rbs, trs, shard):
    out = []
    for n, (c, rb, tr) in enumerate(zip(cs, rbs, trs)):
        h = c.shape[1]
        own = lax.dynamic_index_in_dim(c, shard, axis=0, keepdims=False)
        parts = [(own, 0)] + [(rb.reshape(-1, rb.shape[-1]), j * h) for j in range(3)]
        out.append(ew_sum(f"chip_sum_{tag}_{n}", parts, h, F32, tr))
    return out


N_DEV = 8


def gather_small(blk, reduce):
    rows, cols = blk.shape

    def body(x_ref, out_ref, all_ref, send_sems, recv_sems, local_sem):
        x, y, c, chips = _place()
        me, sib = (x, y, c), (x, y, 1 - c)

        def blk_rows(px, py, pc):
            return all_ref.at[pl.ds(pl.multiple_of((4 * px + 2 * py + pc) * rows, 8), rows), :]

        def copy(k, block, to, src=None):
            return _rcopy(blk_rows(*block) if src is None else src, blk_rows(*block), send_sems, recv_sems, k, to)

        mine = pltpu.make_async_copy(x_ref, blk_rows(*me), local_sem)
        mine.start()
        first = [copy(0, me, sib, src=x_ref)] + [copy(1 + j, me, (*chip, c), src=x_ref) for j, chip in enumerate(chips)]
        for cp in first:
            cp.start()
        passed = [copy(4 + j, (*chip, c), sib) for j, chip in enumerate(chips)]
        for j, chip in enumerate(chips):
            copy(1 + j, (*chip, c), me).wait_recv()
            passed[j].start()
        copy(0, sib, me).wait_recv()
        for j, chip in enumerate(chips):
            copy(4 + j, (*chip, 1 - c), me).wait_recv()
        for cp in first + passed:
            cp.wait_send()
        mine.wait()
        if reduce:
            acc = all_ref[0:rows, :]
            for d in range(1, N_DEV):
                acc = acc + all_ref[d * rows:(d + 1) * rows, :]
            out_ref[...] = acc
        else:
            out_ref[...] = all_ref[...]

    vmem = pl.BlockSpec(memory_space=pltpu.VMEM)
    return pl.pallas_call(
        body, name="reduce_small" if reduce else "gather_small", in_specs=[vmem], out_specs=vmem,
        out_shape=jax.ShapeDtypeStruct((rows if reduce else N_DEV * rows, cols), blk.dtype),
        scratch_shapes=[pltpu.VMEM((N_DEV * rows, cols), blk.dtype), pltpu.SemaphoreType.DMA((7,)),
                        pltpu.SemaphoreType.DMA((7,)), pltpu.SemaphoreType.DMA],
        compiler_params=pltpu.CompilerParams(vmem_limit_bytes=VMEM_LIMIT),
    )(blk)


WEIGHTS = ['w_in', 'mix_pre_norm', 'mix_post_norm', 'hg_lb_table', 'hg_out_norm', 'ssm_conv_w', 'ssm_conv_b',
           'ssm_dt_bias', 'ssm_A_log', 'ssm_D', 'ssm_out_norm', 'w_branch_hg', 'w_branch_ssm', 'w_out', 'ffn_pre_norm',
           'ffn_post_norm', 'ffn_w_up', 'ffn_conv_w', 'ffn_conv_b', 'ffn_w_down']
BIG = ('w_in', 'w_branch_hg', 'w_branch_ssm', 'w_out', 'ffn_w_up', 'ffn_w_down')
SMALL = tuple(n for n in WEIGHTS if n not in BIG)
CONV_SHARD = {'ssm_conv_w': SSM_CONV_DIM // N_CHIPS, 'ffn_conv_w': D_FF // N_CHIPS}
LANES = 128


def _pack(parts):
    flat = jnp.concatenate([p.reshape(-1) for p in parts])
    n = flat.shape[0]
    rows = -(-n // (8 * LANES)) * 8
    return jnp.pad(flat, (0, rows * LANES - n)).reshape(rows, LANES)


def _unpack(packed, shapes):
    flat = packed.reshape(-1)
    out, off = [], 0
    for s in shapes:
        n = int(np.prod(s))
        out.append(flat[off:off + n].reshape(s))
        off += n
    return out


def kernel(x, w_in, mix_pre_norm, mix_post_norm, hg_lb_table, hg_out_norm, ssm_conv_w, ssm_conv_b, ssm_dt_bias, ssm_A_log, ssm_D, ssm_out_norm, w_branch_hg, w_branch_ssm, w_out, ffn_pre_norm, ffn_post_norm, ffn_w_up, ffn_conv_w, ffn_conv_b, ffn_w_down, loss_target, m_w_in, m_mix_pre_norm, m_mix_post_norm, m_hg_lb_table, m_hg_out_norm, m_ssm_conv_w, m_ssm_conv_b, m_ssm_dt_bias, m_ssm_A_log, m_ssm_D, m_ssm_out_norm, m_w_branch_hg, m_w_branch_ssm, m_w_out, m_ffn_pre_norm, m_ffn_post_norm, m_ffn_w_up, m_ffn_conv_w, m_ffn_conv_b, m_ffn_w_down, v_w_in, v_mix_pre_norm, v_mix_post_norm, v_hg_lb_table, v_hg_out_norm, v_ssm_conv_w, v_ssm_conv_b, v_ssm_dt_bias, v_ssm_A_log, v_ssm_D, v_ssm_out_norm, v_w_branch_hg, v_w_branch_ssm, v_w_out, v_ffn_pre_norm, v_ffn_post_norm, v_ffn_w_up, v_ffn_conv_w, v_ffn_conv_b, v_ffn_w_down):
    w = dict(w_in=w_in, mix_pre_norm=mix_pre_norm, mix_post_norm=mix_post_norm, hg_lb_table=hg_lb_table, hg_out_norm=hg_out_norm, ssm_conv_w=ssm_conv_w, ssm_conv_b=ssm_conv_b, ssm_dt_bias=ssm_dt_bias, ssm_A_log=ssm_A_log, ssm_D=ssm_D, ssm_out_norm=ssm_out_norm, w_branch_hg=w_branch_hg, w_branch_ssm=w_branch_ssm, w_out=w_out, ffn_pre_norm=ffn_pre_norm, ffn_post_norm=ffn_post_norm, ffn_w_up=ffn_w_up, ffn_conv_w=ffn_conv_w, ffn_conv_b=ffn_conv_b, ffn_w_down=ffn_w_down)
    m = dict(w_in=m_w_in, mix_pre_norm=m_mix_pre_norm, mix_post_norm=m_mix_post_norm, hg_lb_table=m_hg_lb_table, hg_out_norm=m_hg_out_norm, ssm_conv_w=m_ssm_conv_w, ssm_conv_b=m_ssm_conv_b, ssm_dt_bias=m_ssm_dt_bias, ssm_A_log=m_ssm_A_log, ssm_D=m_ssm_D, ssm_out_norm=m_ssm_out_norm, w_branch_hg=m_w_branch_hg, w_branch_ssm=m_w_branch_ssm, w_out=m_w_out, ffn_pre_norm=m_ffn_pre_norm, ffn_post_norm=m_ffn_post_norm, ffn_w_up=m_ffn_w_up, ffn_conv_w=m_ffn_conv_w, ffn_conv_b=m_ffn_conv_b, ffn_w_down=m_ffn_w_down)
    v = dict(w_in=v_w_in, mix_pre_norm=v_mix_pre_norm, mix_post_norm=v_mix_post_norm, hg_lb_table=v_hg_lb_table, hg_out_norm=v_hg_out_norm, ssm_conv_w=v_ssm_conv_w, ssm_conv_b=v_ssm_conv_b, ssm_dt_bias=v_ssm_dt_bias, ssm_A_log=v_ssm_A_log, ssm_D=v_ssm_D, ssm_out_norm=v_ssm_out_norm, w_branch_hg=v_w_branch_hg, w_branch_ssm=v_w_branch_ssm, w_out=v_w_out, ffn_pre_norm=v_ffn_pre_norm, ffn_post_norm=v_ffn_post_norm, ffn_w_up=v_ffn_w_up, ffn_conv_w=v_ffn_conv_w, ffn_conv_b=v_ffn_conv_b, ffn_w_down=v_ffn_w_down)
    shard = 2 * lax.axis_index("x") + lax.axis_index("y")
    bf = lambda a: a.astype(BF16)

    core = lax.axis_index("c")
    p_in = jnp.concatenate([bf(w_in[0].T), jnp.zeros((IN_ROWS - IN_SHARD, D_MODEL), BF16)], axis=0)
    p_rest = jnp.concatenate([bf(w_branch_hg[0]), bf(w_branch_ssm[0]), bf(w_out[0]), bf(ffn_w_down[0])], axis=0)
    p_up = bf(ffn_w_up[0])
    (g_in,) = gather_weights("gather_w_in", [p_in], [IN_PIECES])
    in_t = g_in[:, :IN_SHARD].reshape(IN_TOTAL, D_MODEL)
    wts = dict(in_t=in_t, g_t=in_t[SEG_G:], dt_t=jnp.pad(in_t[SEG_DT:SEG_G], ((0, DT_PAD - SSM_HEADS), (0, 0))))
    conv_cols = max(CONV_SHARD.values())
    padc = lambda a: jnp.pad(a, ((0, 0), (0, conv_cols - a.shape[1])))
    conv_blk = jnp.concatenate([padc(ssm_conv_w[0]), padc(ffn_conv_w[0]), jnp.zeros((1, conv_cols), F32)], axis=0)
    conv_all = gather_small(conv_blk, reduce=False)
    par = {n: w[n] for n in SMALL}
    par["ssm_conv_w"] = jnp.concatenate([conv_all[16 * s:16 * s + SSM_CONV, :CONV_SHARD['ssm_conv_w']] for s in range(N_CHIPS)], axis=1)
    par["ffn_conv_w"] = jnp.concatenate([conv_all[16 * s + SSM_CONV:16 * s + SSM_CONV + FFN_CONV, :CONV_SHARD['ffn_conv_w']]
                                         for s in range(N_CHIPS)], axis=1)

    loss, grad_x, big, small = local_step(x[0], loss_target[0], wts, par, p_rest, p_up, shard, core)
    loss = lax.psum(loss[0, 0], ("x", "y", "c"))

    halves = [big["in_t"], big["rest"], big["up"]]
    wholes = pair_assemble("pair_assemble", halves, [IN_PIECES, REST_PIECES, UP_PIECES])
    f_in, f_rest, f_up = [_own_slot(f.reshape((2,) + r.shape), r, core).reshape(f.shape) for f, r in zip(wholes, halves)]
    r0, r1, r2, r3 = REST_SPLITS
    grads = dict(w_in=f_in[:IN_SHARD].T, w_branch_hg=f_rest[:r0], w_branch_ssm=f_rest[r0:r1], w_out=f_rest[r1:r2],
                 ffn_w_down=f_rest[r2:r3], ffn_w_up=f_up)

    small["hg_out_norm"] = ew_sum("sum_heads", [(small["hg_out_norm"][hd], 0) for hd in range(HG_HEADS)], 1, F32, 1)
    small["ssm_D"] = fold_heads(small["ssm_D"])[:, :SSM_HEADS]
    small["ssm_dt_bias"] = small["ssm_dt_bias"][:, :SSM_HEADS]
    small["ssm_A_log"] = small["ssm_A_log"][:, :SSM_HEADS]
    shapes = [small[n].shape for n in SMALL]
    summed = _unpack(gather_small(_pack([small[n] for n in SMALL]), reduce=True), shapes)
    for n, g in zip(SMALL, summed):
        if n in CONV_SHARD:
            g = lax.dynamic_slice_in_dim(g, shard * CONV_SHARD[n], CONV_SHARD[n], axis=1)
        grads[n] = g

    two_d = lambda a: a.reshape(a.shape[-2], a.shape[-1])
    delta, new_m, new_v = {}, {}, {}
    for n, tr in (("w_in", 64), ("w_branch_hg", 128), ("w_branch_ssm", 128), ("w_out", 128), ("ffn_w_up", 128), ("ffn_w_down", 128)):
        delta[n], new_m[n], new_v[n] = adamw("adamw_" + n, two_d(w[n]), grads[n], two_d(m[n]), two_d(v[n]), tr)
    sm_shapes = [two_d(w[n]).shape for n in SMALL]
    packed = adamw("adamw_small", _pack([two_d(w[n]) for n in SMALL]), _pack([grads[n] for n in SMALL]),
                   _pack([two_d(m[n]) for n in SMALL]), _pack([two_d(v[n]) for n in SMALL]), 1024)
    for res, packed_res in zip((delta, new_m, new_v), packed):
        for n, a in zip(SMALL, _unpack(packed_res, sm_shapes)):
            res[n] = a
    shaped = lambda d: [d[n].reshape(w[n].shape) for n in WEIGHTS]
    return (loss, grad_x[None], *shaped(grads), *shaped(delta), *shaped(new_m), *shaped(new_v))
```
